```python
import jax
import jax.numpy as jnp
from jax import lax
import numpy as np

D_MODEL = 1024
BATCH = 8
SEQ = 8192
DEPTH = 2

GRID_W = 64
CTX_LEN = 256
EPS = 1e-6
N_MOD = 6
HEAD_DIM = 64
N_Q_HEADS = D_MODEL // 128
N_KV_HEADS = N_Q_HEADS // 4
Q_GROUP = N_Q_HEADS // N_KV_HEADS
Q_W = N_Q_HEADS * HEAD_DIM
KV_W = N_KV_HEADS * HEAD_DIM
Q_BLOCK = 128
ROPE_THETA = 10000.0
ROPE_AXIS_DIM = HEAD_DIM // 2
ROPE_FREQS = ROPE_AXIS_DIM // 2
GMLP_CHUNK = 128
GMLP_GROUPS = 4
GMLP_WIDTH = D_MODEL // 2
GMLP_GROUP_W = GMLP_WIDTH // GMLP_GROUPS
GLA_HEADS = 4
GLA_QK_W = D_MODEL // 4
GLA_V_W = D_MODEL // 2
GLA_DK = GLA_QK_W // GLA_HEADS
GLA_DV = GLA_V_W // GLA_HEADS
GLA_RANK = 16
GLA_TAU = 16.0
GLA_CHUNK = 64
FFN_HIDDEN = 128 * ((8 * D_MODEL // 3 + 127) // 128)
CONV_WIDTH = 3
IN_SPLITS = (GMLP_WIDTH, GMLP_WIDTH, Q_W, KV_W, KV_W, GLA_QK_W, GLA_QK_W, GLA_V_W, GLA_RANK, GLA_RANK, GLA_V_W, D_MODEL, D_MODEL, D_MODEL)
IN_WIDTH = sum(IN_SPLITS)

kernel_name = "hybrid_gated_branch_diffusion_block"


def rms_norm(x, g):
    xf = x.astype(jnp.float32)
    y = xf * lax.rsqrt(jnp.mean(xf * xf, axis=-1, keepdims=True) + EPS)
    return (y * g.astype(jnp.float32)).astype(x.dtype)


def adaln(cond, w, b):
    m = jax.nn.silu(cond) @ w + b
    return m.reshape(cond.shape[0], 1, N_MOD, D_MODEL)


def modulate(h, shift, scale):
    return h * (1 + scale) + shift


def split_in(p):
    out, start = [], 0
    for w in IN_SPLITS:
        out.append(p[..., start:start + w])
        start += w
    return out


def to_heads(p, n_heads, dim):
    return p.reshape(p.shape[0], p.shape[1], n_heads, dim)


def axial_rope_tables(n_tokens, dtype):
    rows = n_tokens // GRID_W
    t_row = jnp.repeat(jnp.arange(rows, dtype=jnp.int32), GRID_W)
    t_col = jnp.tile(jnp.arange(GRID_W, dtype=jnp.int32), rows)
    inv_freq = ROPE_THETA ** (-jnp.arange(ROPE_FREQS, dtype=jnp.float32) / ROPE_FREQS)
    ang_r = t_row.astype(jnp.float32)[:, None] * inv_freq
    ang_c = t_col.astype(jnp.float32)[:, None] * inv_freq
    return (jnp.cos(ang_r).astype(dtype)[:, None, :], jnp.sin(ang_r).astype(dtype)[:, None, :],
            jnp.cos(ang_c).astype(dtype)[:, None, :], jnp.sin(ang_c).astype(dtype)[:, None, :])


def rope_half(x, cos, sin):
    x1, x2 = x[..., :ROPE_FREQS], x[..., ROPE_FREQS:]
    return jnp.concatenate([x1 * cos - x2 * sin, x1 * sin + x2 * cos], axis=-1)


def apply_axial_rope(x, tables):
    cr, sr, cc, sc = tables
    return jnp.concatenate([rope_half(x[..., :ROPE_AXIS_DIM], cr, sr),
                            rope_half(x[..., ROPE_AXIS_DIM:], cc, sc)], axis=-1)


def block_attention(q, k, v):
    b, t = q.shape[0], q.shape[1]
    nb = t // Q_BLOCK
    qb = q.reshape(b, nb, Q_BLOCK, N_KV_HEADS, Q_GROUP, HEAD_DIM).transpose(1, 0, 2, 3, 4, 5)
    scale = HEAD_DIM ** -0.5

    def one_block(qi):
        s = jnp.einsum('bqhgd,bkhd->bhgqk', qi, k).astype(jnp.float32) * scale
        p = jax.nn.softmax(s, axis=-1).astype(v.dtype)
        return jnp.einsum('bhgqk,bkhd->bqhgd', p, v)

    o = lax.map(one_block, qb)
    return o.transpose(1, 0, 2, 3, 4, 5).reshape(b, t, Q_W)


def chunk_gmlp(u, v, norm_g, w_s, b_s):
    b, t, _ = u.shape
    n = t // GMLP_CHUNK
    u = jax.nn.gelu(u)
    v = rms_norm(jax.nn.gelu(v).reshape(b, n, GMLP_CHUNK, GMLP_GROUPS, GMLP_GROUP_W),
                 norm_g.reshape(GMLP_GROUPS, GMLP_GROUP_W))
    f = jnp.einsum('gij,bnjgc->bnigc', w_s, v) + b_s.T[:, :, None]
    return u * f.reshape(b, t, GMLP_WIDTH)


def gla_inputs(p_q, p_k, p_v, p_af, p_ab, w_a2, b_a):
    b, t, _ = p_q.shape
    q = to_heads(p_q, GLA_HEADS, GLA_DK) * (GLA_DK ** -0.5)
    k = to_heads(p_k, GLA_HEADS, GLA_DK)
    v = to_heads(p_v, GLA_HEADS, GLA_DV)

    def log_decay(a1, w2, b2):
        z = (a1 @ w2 + b2).astype(jnp.float32)
        return (jax.nn.log_sigmoid(z) / GLA_TAU).reshape(b, t, GLA_HEADS, GLA_DK)

    return q, k, v, log_decay(p_af, w_a2[0], b_a[0]), log_decay(p_ab, w_a2[1], b_a[1])


def gla_chunked(q, k, v, log_a, s0):
    b, t, h, dk = q.shape
    dv = v.shape[-1]
    n = t // GLA_CHUNK
    q = q.astype(jnp.float32).reshape(b, n, GLA_CHUNK, h, dk)
    k = k.astype(jnp.float32).reshape(b, n, GLA_CHUNK, h, dk)
    v = v.astype(jnp.float32).reshape(b, n, GLA_CHUNK, h, dv)
    cum = jnp.cumsum(log_a.astype(jnp.float32).reshape(b, n, GLA_CHUNK, h, dk), axis=2)
    cum_last = cum[:, :, -1:]
    q_in = q * jnp.exp(cum)
    k_in = k * jnp.exp(-cum)
    k_st = k * jnp.exp(cum_last - cum)
    mask = jnp.tril(jnp.ones((GLA_CHUNK, GLA_CHUNK), dtype=bool))
    att = jnp.where(mask, jnp.einsum('bnihd,bnjhd->bnhij', q_in, k_in), 0.0)
    o = jnp.einsum('bnhij,bnjhv->bnihv', att, v)
    u = jnp.einsum('bnjhd,bnjhv->nbhdv', k_st, v)
    decay = jnp.exp(cum_last[:, :, 0]).transpose(1, 0, 2, 3)

    def step(s, inp):
        d, u_n = inp
        return d[..., None] * s + u_n, s

    s_final, s_in = lax.scan(step, s0.astype(jnp.float32), (decay, u))
    o = o + jnp.einsum('bnihd,nbhdv->bnihv', q_in, s_in)
    return o.reshape(b, t, h, dv), s_final


def gla_chunked_reverse(q, k, v, log_a, s0):
    o, s = gla_chunked(q[:, ::-1], k[:, ::-1], v[:, ::-1], log_a[:, ::-1], s0)
    return o[:, ::-1], s


def gla_output(o, r, g):
    b, t = o.shape[0], o.shape[1]
    o = rms_norm(o, g.reshape(GLA_HEADS, GLA_DV)).reshape(b, t, GLA_V_W)
    return (o * jax.nn.silu(r.astype(jnp.float32))).astype(r.dtype)


def merge_branches(gate_logits, y_a, y_b, y_c, w_a, w_b, w_c, w_o):
    g_a, g_b, g_c = gate_logits
    merged = (jax.nn.sigmoid(g_a) * (y_a @ w_a) + jax.nn.sigmoid(g_b) * (y_b @ w_b)
              + jax.nn.sigmoid(g_c) * (y_c @ w_c))
    return merged @ w_o


def conv_ffn(h, w_up, cw, cb, w_down):
    t = h.shape[1]
    half = CONV_WIDTH // 2
    a = h @ w_up
    ap = jnp.pad(a, ((0, 0), (half, half), (0, 0)))
    a = cb + sum(ap[:, j:j + t] * cw[j] for j in range(CONV_WIDTH))
    g, val = jnp.split(a, 2, axis=-1)
    return (jax.nn.silu(g) * val) @ w_down


def _fwd_setup_inputs(seed: int = 0) -> dict:
    key = jax.random.key(seed)
    ks = jax.random.split(key, 26)
    f32 = jnp.float32
    nrm = lambda k, shape, scale: jax.random.normal(k, shape, f32) * scale
    gain = lambda k, shape: 1.0 + 0.02 * jax.random.normal(k, shape, f32)
    F2 = 2 * FFN_HIDDEN
    return {
        'x': nrm(ks[0], (BATCH, SEQ, D_MODEL), 1.0),
        'c': nrm(ks[1], (BATCH, D_MODEL), 1.0),
        'ctx': nrm(ks[2], (BATCH, CTX_LEN, D_MODEL), 1.0),
        'c_ctx': nrm(ks[3], (D_MODEL,), 1.0),
        'w_ada': nrm(ks[4], (DEPTH, D_MODEL, N_MOD * D_MODEL), 0.5 * D_MODEL ** -0.5),
        'b_ada': nrm(ks[5], (DEPTH, N_MOD * D_MODEL), 0.02),
        'norm1_g': gain(ks[6], (DEPTH, D_MODEL)),
        'norm2_g': gain(ks[7], (DEPTH, D_MODEL)),
        'w_in': nrm(ks[8], (DEPTH, D_MODEL, IN_WIDTH), D_MODEL ** -0.5),
        'q_norm_g': gain(ks[9], (DEPTH, HEAD_DIM)),
        'k_norm_g': gain(ks[10], (DEPTH, HEAD_DIM)),
        'gmlp_norm_g': gain(ks[11], (DEPTH, GMLP_WIDTH)),
        'w_spatial': nrm(ks[12], (DEPTH, GMLP_GROUPS, GMLP_CHUNK, GMLP_CHUNK), 0.5 * GMLP_CHUNK ** -0.5),
        'b_spatial': gain(ks[13], (DEPTH, GMLP_GROUPS, GMLP_CHUNK)),
        'w_alpha2': nrm(ks[14], (DEPTH, 2, GLA_RANK, GLA_QK_W), GLA_RANK ** -0.5),
        'b_alpha': nrm(ks[15], (DEPTH, 2, GLA_QK_W), 0.02),
        'gla_norm_g': gain(ks[16], (DEPTH, GLA_V_W)),
        'w_br_a': nrm(ks[17], (DEPTH, GMLP_WIDTH, D_MODEL), GMLP_WIDTH ** -0.5),
        'w_br_b': nrm(ks[18], (DEPTH, Q_W, D_MODEL), Q_W ** -0.5),
        'w_br_c': nrm(ks[19], (DEPTH, GLA_V_W, D_MODEL), GLA_V_W ** -0.5),
        'w_out': nrm(ks[20], (DEPTH, D_MODEL, D_MODEL), D_MODEL ** -0.5),
        'w_ffn_up': nrm(ks[21], (DEPTH, D_MODEL, F2), D_MODEL ** -0.5),
        'conv_w': nrm(ks[22], (DEPTH, CONV_WIDTH, F2), CONV_WIDTH ** -0.5),
        'conv_b': nrm(ks[23], (DEPTH, F2), 0.02),
        'w_ffn_down': nrm(ks[24], (DEPTH, FFN_HIDDEN, D_MODEL), FFN_HIDDEN ** -0.5),
        'final_norm_g': gain(ks[25], (D_MODEL,)),
    }


def _fwd_reference(x, c, ctx, c_ctx, w_ada, b_ada, norm1_g, norm2_g, w_in, q_norm_g, k_norm_g,
              gmlp_norm_g, w_spatial, b_spatial, w_alpha2, b_alpha, gla_norm_g, w_br_a, w_br_b,
              w_br_c, w_out, w_ffn_up, conv_w, conv_b, w_ffn_down, final_norm_g):
    rope_tab = axial_rope_tables(x.shape[1], x.dtype)
    xc = ctx
    for l in range(DEPTH):
        last = l == DEPTH - 1
        mod_x = adaln(c, w_ada[l], b_ada[l])
        mod_c = adaln(c_ctx[None], w_ada[l], b_ada[l])
        hx = modulate(rms_norm(x, norm1_g[l]), mod_x[:, :, 0], mod_x[:, :, 1])
        hc = modulate(rms_norm(xc, norm1_g[l]), mod_c[:, :, 0], mod_c[:, :, 1])
        px = split_in(hx @ w_in[l])
        pc = split_in(hc @ w_in[l])

        kc = rms_norm(to_heads(pc[3], N_KV_HEADS, HEAD_DIM), k_norm_g[l])
        vc = to_heads(pc[4], N_KV_HEADS, HEAD_DIM)
        qx = apply_axial_rope(rms_norm(to_heads(px[2], N_Q_HEADS, HEAD_DIM), q_norm_g[l]), rope_tab)
        kx = apply_axial_rope(rms_norm(to_heads(px[3], N_KV_HEADS, HEAD_DIM), k_norm_g[l]), rope_tab)
        vx = to_heads(px[4], N_KV_HEADS, HEAD_DIM)
        att_x = block_attention(qx, jnp.concatenate([kc, kx], axis=1), jnp.concatenate([vc, vx], axis=1))

        qgc, kgc, vgc, lfc, lbc = gla_inputs(*pc[5:10], w_alpha2[l], b_alpha[l])
        s0 = jnp.zeros((xc.shape[0], GLA_HEADS, GLA_DK, GLA_DV), jnp.float32)
        oc_f, sc_f = gla_chunked(qgc, kgc, vgc, lfc, s0)
        oc_b, sc_b = gla_chunked_reverse(qgc, kgc, vgc, lbc, s0)
        qgx, kgx, vgx, lfx, lbx = gla_inputs(*px[5:10], w_alpha2[l], b_alpha[l])
        ox_f, _ = gla_chunked(qgx, kgx, vgx, lfx, sc_f)
        ox_b, _ = gla_chunked_reverse(qgx, kgx, vgx, lbx, sc_b)
        gla_x = gla_output(ox_f + ox_b, px[10], gla_norm_g[l])

        gm_x = chunk_gmlp(px[0], px[1], gmlp_norm_g[l], w_spatial[l], b_spatial[l])

        mix_x = merge_branches(px[11:14], gm_x, att_x, gla_x, w_br_a[l], w_br_b[l], w_br_c[l], w_out[l])
        x_mid = x + mod_x[:, :, 2] * mix_x
        hx2 = modulate(rms_norm(x_mid, norm2_g[l]), mod_x[:, :, 3], mod_x[:, :, 4])
        x = x_mid + mod_x[:, :, 5] * conv_ffn(hx2, w_ffn_up[l], conv_w[l], conv_b[l], w_ffn_down[l])

        if not last:
            qc = rms_norm(to_heads(pc[2], N_Q_HEADS, HEAD_DIM), q_norm_g[l])
            att_c = block_attention(qc, kc, vc)
            gla_c = gla_output(oc_f + oc_b, pc[10], gla_norm_g[l])
            gm_c = chunk_gmlp(pc[0], pc[1], gmlp_norm_g[l], w_spatial[l], b_spatial[l])
            mix_c = merge_branches(pc[11:14], gm_c, att_c, gla_c, w_br_a[l], w_br_b[l], w_br_c[l], w_out[l])
            xc_mid = xc + mod_c[:, :, 2] * mix_c
            hc2 = modulate(rms_norm(xc_mid, norm2_g[l]), mod_c[:, :, 3], mod_c[:, :, 4])
            xc = xc_mid + mod_c[:, :, 5] * conv_ffn(hc2, w_ffn_up[l], conv_w[l], conv_b[l], w_ffn_down[l])
    return rms_norm(x, final_norm_g)


import jax as _jax
import jax.numpy as _jnp

TWIN_FORMAT = 'train_step'
FWD_PARAMS = ['x', 'c', 'ctx', 'c_ctx', 'w_ada', 'b_ada', 'norm1_g', 'norm2_g', 'w_in', 'q_norm_g', 'k_norm_g', 'gmlp_norm_g', 'w_spatial', 'b_spatial', 'w_alpha2', 'b_alpha', 'gla_norm_g', 'w_br_a', 'w_br_b', 'w_br_c', 'w_out', 'w_ffn_up', 'conv_w', 'conv_b', 'w_ffn_down', 'final_norm_g']
TWIN_WEIGHTS = ['c_ctx', 'w_ada', 'b_ada', 'norm1_g', 'norm2_g', 'w_in', 'q_norm_g', 'k_norm_g', 'gmlp_norm_g', 'w_spatial', 'b_spatial', 'w_alpha2', 'b_alpha', 'gla_norm_g', 'w_br_a', 'w_br_b', 'w_br_c', 'w_out', 'w_ffn_up', 'conv_w', 'conv_b', 'w_ffn_down', 'final_norm_g']
TWIN_DIFF_INPUT = 'x'
TWIN_INPUTS = ['x', 'c', 'ctx', 'c_ctx', 'w_ada', 'b_ada', 'norm1_g', 'norm2_g', 'w_in', 'q_norm_g', 'k_norm_g', 'gmlp_norm_g', 'w_spatial', 'b_spatial', 'w_alpha2', 'b_alpha', 'gla_norm_g', 'w_br_a', 'w_br_b', 'w_br_c', 'w_out', 'w_ffn_up', 'conv_w', 'conv_b', 'w_ffn_down', 'final_norm_g', 'loss_target', 'm_c_ctx', 'm_w_ada', 'm_b_ada', 'm_norm1_g', 'm_norm2_g', 'm_w_in', 'm_q_norm_g', 'm_k_norm_g', 'm_gmlp_norm_g', 'm_w_spatial', 'm_b_spatial', 'm_w_alpha2', 'm_b_alpha', 'm_gla_norm_g', 'm_w_br_a', 'm_w_br_b', 'm_w_br_c', 'm_w_out', 'm_w_ffn_up', 'm_conv_w', 'm_conv_b', 'm_w_ffn_down', 'm_final_norm_g', 'v_c_ctx', 'v_w_ada', 'v_b_ada', 'v_norm1_g', 'v_norm2_g', 'v_w_in', 'v_q_norm_g', 'v_k_norm_g', 'v_gmlp_norm_g', 'v_w_spatial', 'v_b_spatial', 'v_w_alpha2', 'v_b_alpha', 'v_gla_norm_g', 'v_w_br_a', 'v_w_br_b', 'v_w_br_c', 'v_w_out', 'v_w_ffn_up', 'v_conv_w', 'v_conv_b', 'v_w_ffn_down', 'v_final_norm_g']
TWIN_OUTPUTS = ['loss', 'grad_x', 'grad_c_ctx', 'grad_w_ada', 'grad_b_ada', 'grad_norm1_g', 'grad_norm2_g', 'grad_w_in', 'grad_q_norm_g', 'grad_k_norm_g', 'grad_gmlp_norm_g', 'grad_w_spatial', 'grad_b_spatial', 'grad_w_alpha2', 'grad_b_alpha', 'grad_gla_norm_g', 'grad_w_br_a', 'grad_w_br_b', 'grad_w_br_c', 'grad_w_out', 'grad_w_ffn_up', 'grad_conv_w', 'grad_conv_b', 'grad_w_ffn_down', 'grad_final_norm_g', 'delta_c_ctx', 'delta_w_ada', 'delta_b_ada', 'delta_norm1_g', 'delta_norm2_g', 'delta_w_in', 'delta_q_norm_g', 'delta_k_norm_g', 'delta_gmlp_norm_g', 'delta_w_spatial', 'delta_b_spatial', 'delta_w_alpha2', 'delta_b_alpha', 'delta_gla_norm_g', 'delta_w_br_a', 'delta_w_br_b', 'delta_w_br_c', 'delta_w_out', 'delta_w_ffn_up', 'delta_conv_w', 'delta_conv_b', 'delta_w_ffn_down', 'delta_final_norm_g', 'new_m_c_ctx', 'new_m_w_ada', 'new_m_b_ada', 'new_m_norm1_g', 'new_m_norm2_g', 'new_m_w_in', 'new_m_q_norm_g', 'new_m_k_norm_g', 'new_m_gmlp_norm_g', 'new_m_w_spatial', 'new_m_b_spatial', 'new_m_w_alpha2', 'new_m_b_alpha', 'new_m_gla_norm_g', 'new_m_w_br_a', 'new_m_w_br_b', 'new_m_w_br_c', 'new_m_w_out', 'new_m_w_ffn_up', 'new_m_conv_w', 'new_m_conv_b', 'new_m_w_ffn_down', 'new_m_final_norm_g', 'new_v_c_ctx', 'new_v_w_ada', 'new_v_b_ada', 'new_v_norm1_g', 'new_v_norm2_g', 'new_v_w_in', 'new_v_q_norm_g', 'new_v_k_norm_g', 'new_v_gmlp_norm_g', 'new_v_w_spatial', 'new_v_b_spatial', 'new_v_w_alpha2', 'new_v_b_alpha', 'new_v_gla_norm_g', 'new_v_w_br_a', 'new_v_w_br_b', 'new_v_w_br_c', 'new_v_w_out', 'new_v_w_ffn_up', 'new_v_conv_w', 'new_v_conv_b', 'new_v_w_ffn_down', 'new_v_final_norm_g']
TWIN_LEAF_KINDS = {'loss': 'loss', 'grad_x': 'grad_x', 'grad_c_ctx': 'grad_w', 'grad_w_ada': 'grad_w', 'grad_b_ada': 'grad_w', 'grad_norm1_g': 'grad_w', 'grad_norm2_g': 'grad_w', 'grad_w_in': 'grad_w', 'grad_q_norm_g': 'grad_w', 'grad_k_norm_g': 'grad_w', 'grad_gmlp_norm_g': 'grad_w', 'grad_w_spatial': 'grad_w', 'grad_b_spatial': 'grad_w', 'grad_w_alpha2': 'grad_w', 'grad_b_alpha': 'grad_w', 'grad_gla_norm_g': 'grad_w', 'grad_w_br_a': 'grad_w', 'grad_w_br_b': 'grad_w', 'grad_w_br_c': 'grad_w', 'grad_w_out': 'grad_w', 'grad_w_ffn_up': 'grad_w', 'grad_conv_w': 'grad_w', 'grad_conv_b': 'grad_w', 'grad_w_ffn_down': 'grad_w', 'grad_final_norm_g': 'grad_w', 'delta_c_ctx': 'delta_w', 'delta_w_ada': 'delta_w', 'delta_b_ada': 'delta_w', 'delta_norm1_g': 'delta_w', 'delta_norm2_g': 'delta_w', 'delta_w_in': 'delta_w', 'delta_q_norm_g': 'delta_w', 'delta_k_norm_g': 'delta_w', 'delta_gmlp_norm_g': 'delta_w', 'delta_w_spatial': 'delta_w', 'delta_b_spatial': 'delta_w', 'delta_w_alpha2': 'delta_w', 'delta_b_alpha': 'delta_w', 'delta_gla_norm_g': 'delta_w', 'delta_w_br_a': 'delta_w', 'delta_w_br_b': 'delta_w', 'delta_w_br_c': 'delta_w', 'delta_w_out': 'delta_w', 'delta_w_ffn_up': 'delta_w', 'delta_conv_w': 'delta_w', 'delta_conv_b': 'delta_w', 'delta_w_ffn_down': 'delta_w', 'delta_final_norm_g': 'delta_w', 'new_m_c_ctx': 'new_m', 'new_m_w_ada': 'new_m', 'new_m_b_ada': 'new_m', 'new_m_norm1_g': 'new_m', 'new_m_norm2_g': 'new_m', 'new_m_w_in': 'new_m', 'new_m_q_norm_g': 'new_m', 'new_m_k_norm_g': 'new_m', 'new_m_gmlp_norm_g': 'new_m', 'new_m_w_spatial': 'new_m', 'new_m_b_spatial': 'new_m', 'new_m_w_alpha2': 'new_m', 'new_m_b_alpha': 'new_m', 'new_m_gla_norm_g': 'new_m', 'new_m_w_br_a': 'new_m', 'new_m_w_br_b': 'new_m', 'new_m_w_br_c': 'new_m', 'new_m_w_out': 'new_m', 'new_m_w_ffn_up': 'new_m', 'new_m_conv_w': 'new_m', 'new_m_conv_b': 'new_m', 'new_m_w_ffn_down': 'new_m', 'new_m_final_norm_g': 'new_m', 'new_v_c_ctx': 'new_v', 'new_v_w_ada': 'new_v', 'new_v_b_ada': 'new_v', 'new_v_norm1_g': 'new_v', 'new_v_norm2_g': 'new_v', 'new_v_w_in': 'new_v', 'new_v_q_norm_g': 'new_v', 'new_v_k_norm_g': 'new_v', 'new_v_gmlp_norm_g': 'new_v', 'new_v_w_spatial': 'new_v', 'new_v_b_spatial': 'new_v', 'new_v_w_alpha2': 'new_v', 'new_v_b_alpha': 'new_v', 'new_v_gla_norm_g': 'new_v', 'new_v_w_br_a': 'new_v', 'new_v_w_br_b': 'new_v', 'new_v_w_br_c': 'new_v', 'new_v_w_out': 'new_v', 'new_v_w_ffn_up': 'new_v', 'new_v_conv_w': 'new_v', 'new_v_conv_b': 'new_v', 'new_v_w_ffn_down': 'new_v', 'new_v_final_norm_g': 'new_v'}


def _forward(args):
    return _fwd_reference(*[args[k] for k in FWD_PARAMS])


def _output_shape():
    def fwd():
        inp = _fwd_setup_inputs(0)
        return _fwd_reference(*[inp[k] for k in FWD_PARAMS])
    out = _jax.eval_shape(fwd)
    return out.shape, out.dtype

N_MICROBATCH = 1
ADAM_LR = 0.001
ADAM_B1 = 0.9
ADAM_B2 = 0.999
ADAM_EPS = 1e-08
ADAM_WD = 0.01
ADAM_STEP = 10
PER_EXAMPLE_BATCH_AXIS = {'x': 0, 'c': 0, 'ctx': 0, 'loss_target': 0}
SHARED_INPUTS = []
_WEIGHT_DTYPES = {'c_ctx': _jnp.float32, 'w_ada': _jnp.float32, 'b_ada': _jnp.float32, 'norm1_g': _jnp.float32, 'norm2_g': _jnp.float32, 'w_in': _jnp.float32, 'q_norm_g': _jnp.float32, 'k_norm_g': _jnp.float32, 'gmlp_norm_g': _jnp.float32, 'w_spatial': _jnp.float32, 'b_spatial': _jnp.float32, 'w_alpha2': _jnp.float32, 'b_alpha': _jnp.float32, 'gla_norm_g': _jnp.float32, 'w_br_a': _jnp.float32, 'w_br_b': _jnp.float32, 'w_br_c': _jnp.float32, 'w_out': _jnp.float32, 'w_ffn_up': _jnp.float32, 'conv_w': _jnp.float32, 'conv_b': _jnp.float32, 'w_ffn_down': _jnp.float32, 'final_norm_g': _jnp.float32}
MOMENT_SCALE = {'c_ctx': 1.149396e-02, 'w_ada': 7.220353e-02, 'b_ada': 1.284280e-01, 'norm1_g': 6.702746e-02, 'norm2_g': 7.611927e-02, 'w_in': 2.903589e-02, 'q_norm_g': 1.277726e-02, 'k_norm_g': 1.401854e-02, 'gmlp_norm_g': 2.046540e-02, 'w_spatial': 4.348662e-02, 'b_spatial': 4.414807e-02, 'w_alpha2': 6.903229e-03, 'b_alpha': 1.786619e-02, 'gla_norm_g': 4.228877e-02, 'w_br_a': 3.332059e-02, 'w_br_b': 1.338190e-02, 'w_br_c': 2.714462e-02, 'w_out': 4.534660e-02, 'w_ffn_up': 3.329504e-02, 'conv_w': 3.315420e-02, 'conv_b': 2.961919e-02, 'w_ffn_down': 5.416019e-02, 'final_norm_g': 6.404848e+01}


def _to_microbatches(a, axis):
    t = _jnp.moveaxis(a, axis, 0)
    t = t.reshape((N_MICROBATCH, t.shape[0] // N_MICROBATCH) + t.shape[1:])
    return _jnp.moveaxis(t, 1, axis + 1)


def setup_inputs(seed: int = 0) -> dict:
    inp = _fwd_setup_inputs(seed)
    key = _jax.random.fold_in(_jax.random.key(seed), 7919)
    shape, _ = _output_shape()
    out = dict(inp)
    out["loss_target"] = _jax.random.normal(_jax.random.fold_in(key, 0), shape, _jnp.float32)
    for i, name in enumerate(TWIN_WEIGHTS):
        w = inp[name].astype(_jnp.float32)
        if MOMENT_SCALE is None:
            s = _jnp.sqrt(_jnp.mean(_jnp.square(w)) + 1e-30)
        else:
            s = MOMENT_SCALE[name]
        km, kv = _jax.random.split(_jax.random.fold_in(key, i + 1))
        out[name] = w
        out["m_" + name] = s * _jax.random.normal(km, w.shape, _jnp.float32)
        out["v_" + name] = (s * s) * _jax.random.uniform(kv, w.shape, _jnp.float32, 0.5, 1.5)
    if N_MICROBATCH > 1:
        for name, axis in PER_EXAMPLE_BATCH_AXIS.items():
            out[name] = _to_microbatches(out[name], axis)
    return {'x': out['x'], 'c': out['c'], 'ctx': out['ctx'], 'c_ctx': out['c_ctx'], 'w_ada': out['w_ada'], 'b_ada': out['b_ada'], 'norm1_g': out['norm1_g'], 'norm2_g': out['norm2_g'], 'w_in': out['w_in'], 'q_norm_g': out['q_norm_g'], 'k_norm_g': out['k_norm_g'], 'gmlp_norm_g': out['gmlp_norm_g'], 'w_spatial': out['w_spatial'], 'b_spatial': out['b_spatial'], 'w_alpha2': out['w_alpha2'], 'b_alpha': out['b_alpha'], 'gla_norm_g': out['gla_norm_g'], 'w_br_a': out['w_br_a'], 'w_br_b': out['w_br_b'], 'w_br_c': out['w_br_c'], 'w_out': out['w_out'], 'w_ffn_up': out['w_ffn_up'], 'conv_w': out['conv_w'], 'conv_b': out['conv_b'], 'w_ffn_down': out['w_ffn_down'], 'final_norm_g': out['final_norm_g'], 'loss_target': out['loss_target'], 'm_c_ctx': out['m_c_ctx'], 'm_w_ada': out['m_w_ada'], 'm_b_ada': out['m_b_ada'], 'm_norm1_g': out['m_norm1_g'], 'm_norm2_g': out['m_norm2_g'], 'm_w_in': out['m_w_in'], 'm_q_norm_g': out['m_q_norm_g'], 'm_k_norm_g': out['m_k_norm_g'], 'm_gmlp_norm_g': out['m_gmlp_norm_g'], 'm_w_spatial': out['m_w_spatial'], 'm_b_spatial': out['m_b_spatial'], 'm_w_alpha2': out['m_w_alpha2'], 'm_b_alpha': out['m_b_alpha'], 'm_gla_norm_g': out['m_gla_norm_g'], 'm_w_br_a': out['m_w_br_a'], 'm_w_br_b': out['m_w_br_b'], 'm_w_br_c': out['m_w_br_c'], 'm_w_out': out['m_w_out'], 'm_w_ffn_up': out['m_w_ffn_up'], 'm_conv_w': out['m_conv_w'], 'm_conv_b': out['m_conv_b'], 'm_w_ffn_down': out['m_w_ffn_down'], 'm_final_norm_g': out['m_final_norm_g'], 'v_c_ctx': out['v_c_ctx'], 'v_w_ada': out['v_w_ada'], 'v_b_ada': out['v_b_ada'], 'v_norm1_g': out['v_norm1_g'], 'v_norm2_g': out['v_norm2_g'], 'v_w_in': out['v_w_in'], 'v_q_norm_g': out['v_q_norm_g'], 'v_k_norm_g': out['v_k_norm_g'], 'v_gmlp_norm_g': out['v_gmlp_norm_g'], 'v_w_spatial': out['v_w_spatial'], 'v_b_spatial': out['v_b_spatial'], 'v_w_alpha2': out['v_w_alpha2'], 'v_b_alpha': out['v_b_alpha'], 'v_gla_norm_g': out['v_gla_norm_g'], 'v_w_br_a': out['v_w_br_a'], 'v_w_br_b': out['v_w_br_b'], 'v_w_br_c': out['v_w_br_c'], 'v_w_out': out['v_w_out'], 'v_w_ffn_up': out['v_w_ffn_up'], 'v_conv_w': out['v_conv_w'], 'v_conv_b': out['v_conv_b'], 'v_w_ffn_down': out['v_w_ffn_down'], 'v_final_norm_g': out['v_final_norm_g']}


def _loss(weights, diff, rest, loss_target):
    with _jax.named_scope("forward"):
        args = {**rest, TWIN_DIFF_INPUT: diff, **{k: w.astype(_WEIGHT_DTYPES[k]) for k, w in weights.items()}}
        y = _forward(args)
    with _jax.named_scope("loss_head"):
        err = _jnp.square(y.astype(_jnp.float32) - loss_target)
        return 0.5 * _jnp.sum(_jnp.mean(err, axis=-1)) if err.ndim else 0.5 * err


def _adamw(w, g, m, v):
    m = ADAM_B1 * m + (1.0 - ADAM_B1) * g
    v = ADAM_B2 * v + (1.0 - ADAM_B2) * _jnp.square(g)
    m_hat = m / (1.0 - ADAM_B1 ** ADAM_STEP)
    v_hat = v / (1.0 - ADAM_B2 ** ADAM_STEP)
    delta = -ADAM_LR * (m_hat / (_jnp.sqrt(v_hat) + ADAM_EPS) + ADAM_WD * w)
    return delta, m, v


def reference(x, c, ctx, c_ctx, w_ada, b_ada, norm1_g, norm2_g, w_in, q_norm_g, k_norm_g, gmlp_norm_g, w_spatial, b_spatial, w_alpha2, b_alpha, gla_norm_g, w_br_a, w_br_b, w_br_c, w_out, w_ffn_up, conv_w, conv_b, w_ffn_down, final_norm_g, loss_target, m_c_ctx, m_w_ada, m_b_ada, m_norm1_g, m_norm2_g, m_w_in, m_q_norm_g, m_k_norm_g, m_gmlp_norm_g, m_w_spatial, m_b_spatial, m_w_alpha2, m_b_alpha, m_gla_norm_g, m_w_br_a, m_w_br_b, m_w_br_c, m_w_out, m_w_ffn_up, m_conv_w, m_conv_b, m_w_ffn_down, m_final_norm_g, v_c_ctx, v_w_ada, v_b_ada, v_norm1_g, v_norm2_g, v_w_in, v_q_norm_g, v_k_norm_g, v_gmlp_norm_g, v_w_spatial, v_b_spatial, v_w_alpha2, v_b_alpha, v_gla_norm_g, v_w_br_a, v_w_br_b, v_w_br_c, v_w_out, v_w_ffn_up, v_conv_w, v_conv_b, v_w_ffn_down, v_final_norm_g):
    given = dict(x=x, c=c, ctx=ctx, c_ctx=c_ctx, w_ada=w_ada, b_ada=b_ada, norm1_g=norm1_g, norm2_g=norm2_g, w_in=w_in, q_norm_g=q_norm_g, k_norm_g=k_norm_g, gmlp_norm_g=gmlp_norm_g, w_spatial=w_spatial, b_spatial=b_spatial, w_alpha2=w_alpha2, b_alpha=b_alpha, gla_norm_g=gla_norm_g, w_br_a=w_br_a, w_br_b=w_br_b, w_br_c=w_br_c, w_out=w_out, w_ffn_up=w_ffn_up, conv_w=conv_w, conv_b=conv_b, w_ffn_down=w_ffn_down, final_norm_g=final_norm_g, loss_target=loss_target, m_c_ctx=m_c_ctx, m_w_ada=m_w_ada, m_b_ada=m_b_ada, m_norm1_g=m_norm1_g, m_norm2_g=m_norm2_g, m_w_in=m_w_in, m_q_norm_g=m_q_norm_g, m_k_norm_g=m_k_norm_g, m_gmlp_norm_g=m_gmlp_norm_g, m_w_spatial=m_w_spatial, m_b_spatial=m_b_spatial, m_w_alpha2=m_w_alpha2, m_b_alpha=m_b_alpha, m_gla_norm_g=m_gla_norm_g, m_w_br_a=m_w_br_a, m_w_br_b=m_w_br_b, m_w_br_c=m_w_br_c, m_w_out=m_w_out, m_w_ffn_up=m_w_ffn_up, m_conv_w=m_conv_w, m_conv_b=m_conv_b, m_w_ffn_down=m_w_ffn_down, m_final_norm_g=m_final_norm_g, v_c_ctx=v_c_ctx, v_w_ada=v_w_ada, v_b_ada=v_b_ada, v_norm1_g=v_norm1_g, v_norm2_g=v_norm2_g, v_w_in=v_w_in, v_q_norm_g=v_q_norm_g, v_k_norm_g=v_k_norm_g, v_gmlp_norm_g=v_gmlp_norm_g, v_w_spatial=v_w_spatial, v_b_spatial=v_b_spatial, v_w_alpha2=v_w_alpha2, v_b_alpha=v_b_alpha, v_gla_norm_g=v_gla_norm_g, v_w_br_a=v_w_br_a, v_w_br_b=v_w_br_b, v_w_br_c=v_w_br_c, v_w_out=v_w_out, v_w_ffn_up=v_w_ffn_up, v_conv_w=v_conv_w, v_conv_b=v_conv_b, v_w_ffn_down=v_w_ffn_down, v_final_norm_g=v_final_norm_g)
    weights = {n: given[n] for n in TWIN_WEIGHTS}
    shared = {n: given[n] for n in SHARED_INPUTS}
    per_example = {n: given[n] for n in ['x', 'c', 'ctx']}
    grad_fn = _jax.value_and_grad(_loss, argnums=(0, 1))

    def one_microbatch(ex, loss_target):
        ex = dict(ex)
        diff = ex.pop(TWIN_DIFF_INPUT)
        return grad_fn(weights, diff, {**shared, **ex}, loss_target)

    if N_MICROBATCH == 1:
        loss, (grad_w, grad_x) = one_microbatch(per_example, given["loss_target"])
    else:
        def body(carry, xs):
            loss_sum, grad_sum = carry
            l_k, (gw_k, gx_k) = one_microbatch(xs[0], xs[1])
            with _jax.named_scope("update"):
                return (loss_sum + l_k, _jax.tree.map(_jnp.add, grad_sum, gw_k)), gx_k

        init = (_jnp.zeros((), _jnp.float32), _jax.tree.map(_jnp.zeros_like, weights))
        (loss, grad_w), grad_x = _jax.lax.scan(body, init, (per_example, given["loss_target"]))
    with _jax.named_scope("update"):
        delta_w, new_m, new_v = {}, {}, {}
        for n in TWIN_WEIGHTS:
            delta_w[n], new_m[n], new_v[n] = _adamw(weights[n], grad_w[n], given["m_" + n], given["v_" + n])
    return (loss, grad_x, *[grad_w[n] for n in TWIN_WEIGHTS], *[delta_w[n] for n in TWIN_WEIGHTS],
            *[new_m[n] for n in TWIN_WEIGHTS], *[new_v[n] for n in TWIN_WEIGHTS])
```

```python
import functools

import jax
import jax.numpy as jnp
from jax import lax
from jax.experimental import pallas as pl
from jax.experimental.pallas import tpu as pltpu

F32 = jnp.float32
BF16 = jnp.bfloat16
HI = lax.Precision.HIGHEST
MESH_ID = pl.DeviceIdType.MESH

N_DEV = 8
EPS = 1e-6
D = 1024
N_MOD = 6
HEAD_DIM = 64
N_Q_HEADS = 8
N_KV_HEADS = 2
Q_GROUP = 4
Q_W = 512
KV_W = 128
GRID_W = 64
ROPE_THETA = 10000.0
ROPE_FREQS = 16
GMLP_CHUNK = 128
GMLP_GROUPS = 4
GMLP_W = 512
GLA_HEADS = 4
GLA_QK_W = 256
GLA_V_W = 512
GLA_DK = 64
GLA_DV = 128
GLA_RANK = 16
GLA_TAU = 16.0
GLA_CHUNK = 64
FFN_H = 2816
F2 = 2 * FFN_H
IN_SPLITS = (512, 512, 512, 128, 128, 256, 256, 512, 16, 16, 512, 1024, 1024, 1024)
IN_WIDTH = sum(IN_SPLITS)

C_GA, C_GB, C_GC = 0, 1024, 2048
C_U, C_V, C_Q, C_GV, C_R = 3072, 3584, 4096, 4608, 5120
C_GQ, C_GK = 5632, 5888
C_K, C_VV, C_AF = 6144, 6272, 6400
PW = 6656

ADAM_LR = 0.001
ADAM_B1 = 0.9
ADAM_B2 = 0.999
ADAM_EPS = 1e-08
ADAM_WD = 0.01
ADAM_STEP = 10

V7X_VMEM_LIMIT = 56 * 1024 * 1024

_ARB1 = pltpu.CompilerParams(dimension_semantics=("arbitrary",), vmem_limit_bytes=V7X_VMEM_LIMIT)


def _pick(dim, prefs):
    for p in prefs:
        if dim % p == 0:
            return p
    return dim


def _hdot(a, b, dims=(((1,), (0,)), ((), ()))):
    return lax.dot_general(a, b, dims, precision=HI, preferred_element_type=F32)


_NT = (((1,), (1,)), ((), ()))
_TN = (((0,), (0,)), ((), ()))
_NN = (((1,), (0,)), ((), ()))


def _matmul(name, a, b, mode, out_dtype, *, a_halves=False, b_halves=False, o_halves=False):
    def dims2(x, halves):
        return (x.shape[1], 2 * x.shape[2]) if halves else x.shape

    ar, ac = dims2(a, a_halves)
    br, bc = dims2(b, b_halves)
    if mode == "nn":
        M, K, N = ar, ac, bc
    elif mode == "nt":
        M, K, N = ar, ac, br
    else:
        M, K, N = ac, ar, bc
    row_prefs = (768, 512, 384, 256, 128)
    n_unit = N // 2 if (o_halves or (b_halves and mode != "nt")) else N
    k_unit = K // 2 if (a_halves and mode != "tn") else K
    if mode == "tn":
        tm = _pick(M, (1024, 1408, 512, 256, 128))
        tk = _pick(K, row_prefs)
    else:
        tm = _pick(M, row_prefs)
        tk = _pick(k_unit, (1024, 1408, 512, 256, 128))
    tn = _pick(n_unit, (512, 1408, 256, 128))
    nk = K // tk

    def spec(shape2, halves, blk, imap):
        if not halves:
            return pl.BlockSpec(blk, imap)
        nhalf = (shape2[1] // 2) // blk[1]

        def im(i, j, k):
            r, c = imap(i, j, k)
            return (c // nhalf, r, c % nhalf)
        return pl.BlockSpec((None,) + blk, im)

    if mode == "nn":
        a_spec = spec((ar, ac), a_halves, (tm, tk), lambda i, j, k: (i, k))
        b_spec = spec((br, bc), b_halves, (tk, tn), lambda i, j, k: (k, j))
        dn = _NN
    elif mode == "nt":
        a_spec = spec((ar, ac), a_halves, (tm, tk), lambda i, j, k: (i, k))
        b_spec = spec((br, bc), b_halves, (tn, tk), lambda i, j, k: (j, k))
        dn = _NT
    else:
        a_spec = spec((ar, ac), a_halves, (tk, tm), lambda i, j, k: (k, i))
        b_spec = spec((br, bc), b_halves, (tk, tn), lambda i, j, k: (k, j))
        dn = _TN
    o_spec = spec((M, N), o_halves, (tm, tn), lambda i, j, k: (i, j))
    o_shape = (2, M, N // 2) if o_halves else (M, N)

    def body(a_ref, b_ref, o_ref, acc_ref):
        k = pl.program_id(2)
        part = lax.dot_general(a_ref[...], b_ref[...], dn, preferred_element_type=F32)
        if nk == 1:
            o_ref[...] = part.astype(o_ref.dtype)
        else:
            @pl.when(k == 0)
            def _():
                acc_ref[...] = part

            @pl.when(k > 0)
            def _():
                acc_ref[...] += part

            @pl.when(k == nk - 1)
            def _():
                o_ref[...] = acc_ref[...].astype(o_ref.dtype)

    return pl.pallas_call(
        body, name=name, grid=(M // tm, N // tn, nk),
        in_specs=[a_spec, b_spec], out_specs=o_spec,
        out_shape=jax.ShapeDtypeStruct(o_shape, out_dtype),
        scratch_shapes=[pltpu.VMEM((tm, tn), F32)],
        compiler_params=pltpu.CompilerParams(dimension_semantics=("parallel", "parallel", "arbitrary"),
                                             vmem_limit_bytes=V7X_VMEM_LIMIT),
    )(a, b)


def _full_spec(shape):
    nd = len(shape)
    return pl.BlockSpec(tuple(shape), lambda i, _nd=nd: (0,) * _nd)


def _row_spec(R, W, cb):
    return pl.BlockSpec((R, W), lambda i, _cb=cb: (i, _cb))


def _load_rows(refs, specs):
    vals = []
    for ref, (_, W, _, pw) in zip(refs, specs):
        if pw == W:
            vals.append(ref[...].astype(F32))
        else:
            vals.append([ref[:, k * pw:(k + 1) * pw].astype(F32) for k in range(W // pw)])
    return vals


def _load_params(refs, specs):
    vals = []
    for ref, (arr, split) in zip(refs, specs):
        if split:
            vals.append([ref[k] for k in range(arr.shape[0])])
        else:
            vals.append(ref[...])
    return vals


def _mod_spec(nct, width):
    return pl.BlockSpec((None, 1, width), lambda i: (jnp.minimum(i // nct, 1), 0, 0))


def _rowwise(name, f, R, nct, rows, consts, params, cparams, mods, outs, post=None):
    n = rows[0][0].shape[0]
    nr, nc, npar, ncp, nm = len(rows), len(consts), len(params), len(cparams), len(mods)

    def body(*refs):
        pos = 0
        rr = refs[pos:pos + nr]; pos += nr
        cr = refs[pos:pos + nc]; pos += nc
        pr = refs[pos:pos + npar]; pos += npar
        cpr = refs[pos:pos + ncp]; pos += ncp
        mr = refs[pos:pos + nm]; pos += nm
        orefs = refs[pos:]
        res = f(_load_rows(rr, rows), _load_params(pr, params), [m[...] for m in mr],
                _load_rows(cr, consts), _load_params(cpr, cparams))
        if post is not None:
            res = post(res, _load_rows(cr, consts))
        for o_ref, r in zip(orefs, res):
            o_ref[...] = r.astype(o_ref.dtype)

    in_specs = ([_row_spec(R, W, cb) for (_, W, cb, _) in rows + consts]
                + [_full_spec(a.shape) for (a, _) in params + cparams]
                + [_mod_spec(nct, m.shape[2]) for m in mods])
    args = [a for (a, _, _, _) in rows + consts] + [a for (a, _) in params + cparams] + list(mods)
    return pl.pallas_call(
        body, name=name, grid=(n // R,), in_specs=in_specs,
        out_specs=[_row_spec(R, w, 0) for (w, _) in outs],
        out_shape=[jax.ShapeDtypeStruct((n, w), dt) for (w, dt) in outs],
        compiler_params=_ARB1,
    )(*args)


def _rowwise_bwd(name, f, R, nct, rows, consts, params, cparams, mods, douts, drow, adds=None, pre=None):
    n = rows[0][0].shape[0]
    adds = adds or [None] * len(rows)
    nr, nc, npar, ncp, nm, nd = len(rows), len(consts), len(params), len(cparams), len(mods), len(douts)
    add_ix = [k for k in range(nr) if adds[k] is not None]
    out_ix = [k for k in range(nr) if drow[k] is not None]

    def body(*refs):
        i = pl.program_id(0)
        pos = 0
        rr = refs[pos:pos + nr]; pos += nr
        cr = refs[pos:pos + nc]; pos += nc
        pr = refs[pos:pos + npar]; pos += npar
        cpr = refs[pos:pos + ncp]; pos += ncp
        mr = refs[pos:pos + nm]; pos += nm
        dr = refs[pos:pos + nd]; pos += nd
        ar = refs[pos:pos + len(add_ix)]; pos += len(add_ix)
        drr = refs[pos:pos + len(out_ix)]; pos += len(out_ix)
        dpr = refs[pos:pos + npar]; pos += npar
        dmr = refs[pos:pos + nm]; pos += nm

        cv = _load_rows(cr, consts)
        cpv = _load_params(cpr, cparams)
        _, vjp = jax.vjp(lambda rv, pv, mv: f(rv, pv, mv, cv, cpv),
                         _load_rows(rr, rows), _load_params(pr, params), [m[...] for m in mr])
        dv = [d[...].astype(F32) for d in dr]
        if pre is not None:
            dv = pre(dv, cv)
        g_rows, g_params, g_mods = vjp(tuple(dv))

        for ref, k in zip(drr, out_ix):
            _, W, _, pw = rows[k]
            g = g_rows[k]
            extra = ar[add_ix.index(k)] if k in add_ix else None
            if pw == W:
                if extra is not None:
                    g = g + extra[...].astype(F32)
                ref[...] = g.astype(ref.dtype)
            else:
                for q in range(W // pw):
                    gq = g[q]
                    if extra is not None:
                        gq = gq + extra[:, q * pw:(q + 1) * pw].astype(F32)
                    ref[:, q * pw:(q + 1) * pw] = gq.astype(ref.dtype)

        @pl.when(i == 0)
        def _():
            for ref in dpr:
                ref[...] = jnp.zeros_like(ref)

        for ref, (arr, split), g in zip(dpr, params, g_params):
            if split:
                for k in range(arr.shape[0]):
                    ref[k] += g[k]
            else:
                ref[...] += g

        @pl.when((i == 0) | (i == nct))
        def _():
            for ref in dmr:
                ref[...] = jnp.zeros_like(ref)

        for ref, g in zip(dmr, g_mods):
            ref[...] += g

    in_specs = ([_row_spec(R, W, cb) for (_, W, cb, _) in rows + consts]
                + [_full_spec(a.shape) for (a, _) in params + cparams]
                + [_mod_spec(nct, m.shape[2]) for m in mods]
                + [_row_spec(R, W, 0) for (_, W) in douts]
                + [_row_spec(R, rows[k][1], 0) for k in add_ix])
    args = ([a for (a, _, _, _) in rows + consts] + [a for (a, _) in params + cparams] + list(mods)
            + [a for (a, _) in douts] + [adds[k] for k in add_ix])
    out_specs = ([_row_spec(R, rows[k][1], 0) for k in out_ix]
                 + [_full_spec(a.shape) for (a, _) in params]
                 + [_mod_spec(nct, m.shape[2]) for m in mods])
    out_shape = ([jax.ShapeDtypeStruct((n, rows[k][1]), drow[k]) for k in out_ix]
                 + [jax.ShapeDtypeStruct(a.shape, F32) for (a, _) in params]
                 + [jax.ShapeDtypeStruct(m.shape, F32) for m in mods])
    res = pl.pallas_call(
        body, name=name, grid=(n // R,), in_specs=in_specs, out_specs=out_specs, out_shape=out_shape,
        compiler_params=_ARB1,
    )(*args)
    no = len(out_ix)
    return list(res[:no]), list(res[no:no + npar]), list(res[no + npar:])


def _rms(x, g):
    return x * lax.rsqrt(jnp.mean(x * x, axis=-1, keepdims=True) + EPS) * g


def _f_lnmod(rv, pv, mv, cv, cpv):
    (x,), (g,), (shift, scale) = rv, pv, mv
    return (_rms(x, g) * (1.0 + scale) + shift,)


def _f_resid(rv, pv, mv, cv, cpv):
    (x, y), (gate,) = rv, mv
    return (x + gate * y,)


def _f_merge(rv, pv, mv, cv, cpv):
    ga, gb, gc, ya, yb, yc = rv
    return (jax.nn.sigmoid(ga) * ya + jax.nn.sigmoid(gb) * yb + jax.nn.sigmoid(gc) * yc,)


def _f_qknorm(rv, pv, mv, cv, cpv):
    (q, k), (gq, gk), (bdq, eq, bdk, ek) = rv, pv, cpv
    qn = q * lax.rsqrt(_hdot(q * q, bdq) + EPS) * _hdot(gq, eq)
    kn = k * lax.rsqrt(_hdot(k * k, bdk) + EPS) * _hdot(gk, ek)
    return (qn, kn)


def _rope(x, cos, sin):
    w = x.shape[1]
    lane = lax.broadcasted_iota(jnp.int32, x.shape, 1)
    partner = jnp.where((lane & 31) < 16, pltpu.roll(x, w - 16, 1), pltpu.roll(x, 16, 1))
    return x * cos + partner * sin


def _qk_post(res, cv):
    (qn, kn), (cq, sq, ck, sk) = res, cv
    return (_rope(qn, cq, sq) * (HEAD_DIM ** -0.5), _rope(kn, ck, sk))


def _qk_pre(dv, cv):
    (dq, dk), (cq, sq, ck, sk) = dv, cv
    return (_rope(dq * (HEAD_DIM ** -0.5), cq, -sq), _rope(dk, ck, -sk))


def _f_gmlp(rv, pv, mv, cv, cpv):
    (u, vs), (ng, ws, bt) = rv, pv
    pieces = []
    for g in range(GMLP_GROUPS):
        vn = _rms(jax.nn.gelu(vs[g]), ng[g])
        pieces.append(_hdot(ws[g], vn) + bt[g])
    return (jax.nn.gelu(u) * jnp.concatenate(pieces, axis=1),)


def _f_glaout(rv, pv, mv, cv, cpv):
    (os_, rs), (gn,) = rv, pv
    pieces = [_rms(os_[h], gn[h]) * jax.nn.silu(rs[h]) for h in range(GLA_HEADS)]
    return (jnp.concatenate(pieces, axis=1),)


_CONV_CB = 1408


def _seq_masks(first_row, rows, tc, n):
    g = first_row + lax.broadcasted_iota(jnp.int32, (rows, 1), 0)
    not_start = jnp.where((g == 0) | (g == tc), 0.0, 1.0)
    not_end = jnp.where((g == tc - 1) | (g == n - 1), 0.0, 1.0)
    return not_start, not_end


def _conv_specs(R, n):
    nb8 = n // 8
    main = pl.BlockSpec((2, R, _CONV_CB), lambda j, i: (0, i, j))
    prev = pl.BlockSpec((2, 8, _CONV_CB), lambda j, i: (0, jnp.maximum(i * (R // 8) - 1, 0), j))
    nxt = pl.BlockSpec((2, 8, _CONV_CB), lambda j, i: (0, jnp.minimum((i + 1) * (R // 8), nb8 - 1), j))
    cw = pl.BlockSpec((2, 3, _CONV_CB), lambda j, i: (0, 0, j))
    cb = pl.BlockSpec((2, 1, _CONV_CB), lambda j, i: (0, 0, j))
    return main, prev, nxt, cw, cb


def _conv_act(ext_ref, h, first, count, cw_ref, cb_ref, not_start, not_end):
    prev = ext_ref[h, pl.ds(first - 1, count), :]
    mid = ext_ref[h, pl.ds(first, count), :]
    nxt = ext_ref[h, pl.ds(first + 1, count), :]
    return (cb_ref[h] + cw_ref[h, 1:2, :] * mid + cw_ref[h, 0:1, :] * (prev * not_start)
            + cw_ref[h, 2:3, :] * (nxt * not_end))


def _conv_gate(name, a2, cw, cb, R, tc):
    n = a2.shape[1]
    main, prev, nxt, cws, cbs = _conv_specs(R, n)

    def body(a_ref, p_ref, n_ref, cw_ref, cb_ref, o_ref, ext):
        i = pl.program_id(1)
        ext[:, 0:8, :] = jnp.zeros((2, 8, _CONV_CB), F32)
        ext[:, 8:16, :] = p_ref[...]
        ext[:, 16:16 + R, :] = a_ref[...]
        ext[:, 16 + R:24 + R, :] = n_ref[...]
        ext[:, 24 + R:32 + R, :] = jnp.zeros((2, 8, _CONV_CB), F32)
        ns, ne = _seq_masks(i * R, R, tc, n)
        g = _conv_act(ext, 0, 16, R, cw_ref, cb_ref, ns, ne)
        v = _conv_act(ext, 1, 16, R, cw_ref, cb_ref, ns, ne)
        o_ref[...] = (jax.nn.silu(g) * v).astype(o_ref.dtype)

    return pl.pallas_call(
        body, name=name, grid=(FFN_H // _CONV_CB, n // R),
        in_specs=[main, prev, nxt, cws, cbs],
        out_specs=pl.BlockSpec((R, _CONV_CB), lambda j, i: (i, j)),
        out_shape=jax.ShapeDtypeStruct((n, FFN_H), BF16),
        scratch_shapes=[pltpu.VMEM((2, R + 32, _CONV_CB), F32)],
        compiler_params=pltpu.CompilerParams(dimension_semantics=("arbitrary", "arbitrary"),
                                             vmem_limit_bytes=V7X_VMEM_LIMIT),
    )(a2, a2, a2, cw, cb)


def _conv_gate_bwd(name, a2, dgv, cw, cb, R, tc):
    n = a2.shape[1]
    nb8 = n // 8
    main, prev, nxt, cws, cbs = _conv_specs(R, n)
    d_main = pl.BlockSpec((R, _CONV_CB), lambda j, i: (i, j))
    d_prev = pl.BlockSpec((8, _CONV_CB), lambda j, i: (jnp.maximum(i * (R // 8) - 1, 0), j))
    d_next = pl.BlockSpec((8, _CONV_CB), lambda j, i: (jnp.minimum((i + 1) * (R // 8), nb8 - 1), j))
    RE = R + 16

    def body(a_ref, p_ref, n_ref, cw_ref, cb_ref, d_ref, dp_ref, dn_ref, da_ref, dcw_ref, dcb_ref, ext, dext):
        i = pl.program_id(1)
        zeros8 = jnp.zeros((2, 8, _CONV_CB), F32)
        ext[:, 0:8, :] = zeros8
        ext[:, 8:16, :] = p_ref[...]
        ext[:, 16:16 + R, :] = a_ref[...]
        ext[:, 16 + R:24 + R, :] = n_ref[...]
        ext[:, 24 + R:32 + R, :] = zeros8
        ns_e, ne_e = _seq_masks(i * R - 8, RE, tc, n)
        g = _conv_act(ext, 0, 8, RE, cw_ref, cb_ref, ns_e, ne_e)
        v = _conv_act(ext, 1, 8, RE, cw_ref, cb_ref, ns_e, ne_e)
        dout = jnp.concatenate([dp_ref[...], d_ref[...], dn_ref[...]], axis=0).astype(F32)
        sg = jax.nn.sigmoid(g)
        dext[:, 0:8, :] = zeros8
        dext[:, 24 + R:32 + R, :] = zeros8
        dext[0, 8:8 + RE, :] = dout * v * sg * (1.0 + g * (1.0 - sg))
        dext[1, 8:8 + RE, :] = dout * g * sg
        ns, ne = _seq_masks(i * R, R, tc, n)

        @pl.when(i == 0)
        def _():
            dcw_ref[...] = jnp.zeros_like(dcw_ref)
            dcb_ref[...] = jnp.zeros_like(dcb_ref)

        for h in range(2):
            da = dext[h, 16:16 + R, :]
            da_up = dext[h, 17:17 + R, :] * ne
            da_dn = dext[h, 15:15 + R, :] * ns
            da_ref[h] = (cw_ref[h, 1:2, :] * da + cw_ref[h, 0:1, :] * da_up + cw_ref[h, 2:3, :] * da_dn).astype(da_ref.dtype)
            a_prev = ext[h, 15:15 + R, :] * ns
            a_mid = ext[h, 16:16 + R, :]
            a_next = ext[h, 17:17 + R, :] * ne
            dcw_ref[h, 0:1, :] += jnp.sum(da * a_prev, axis=0, keepdims=True)
            dcw_ref[h, 1:2, :] += jnp.sum(da * a_mid, axis=0, keepdims=True)
            dcw_ref[h, 2:3, :] += jnp.sum(da * a_next, axis=0, keepdims=True)
            dcb_ref[h] += jnp.sum(da, axis=0, keepdims=True)

    return pl.pallas_call(
        body, name=name, grid=(FFN_H // _CONV_CB, n // R),
        in_specs=[main, prev, nxt, cws, cbs, d_main, d_prev, d_next],
        out_specs=[main, cws, cbs],
        out_shape=[jax.ShapeDtypeStruct((2, n, FFN_H), BF16), jax.ShapeDtypeStruct((2, 3, FFN_H), F32),
                   jax.ShapeDtypeStruct((2, 1, FFN_H), F32)],
        scratch_shapes=[pltpu.VMEM((2, R + 32, _CONV_CB), F32), pltpu.VMEM((2, R + 32, _CONV_CB), F32)],
        compiler_params=pltpu.CompilerParams(dimension_semantics=("arbitrary", "arbitrary"),
                                             vmem_limit_bytes=V7X_VMEM_LIMIT),
    )(a2, a2, a2, cw, cb, dgv, dgv, dgv)


def _flash_fwd(name, q8, k2, v2):
    nq, nk = q8.shape[1], k2.shape[1]
    tq = _pick(nq, (256, 128))
    tk = _pick(nk, (256, 128))
    rows = Q_GROUP * tq

    def body(q_ref, k_ref, v_ref, o_ref, lse_ref, m_ref, l_ref, acc_ref):
        q = q_ref[...].reshape(rows, HEAD_DIM)
        m_ref[...] = jnp.full((rows, 1), -1e30, F32)
        l_ref[...] = jnp.zeros((rows, 1), F32)
        acc_ref[...] = jnp.zeros((rows, HEAD_DIM), F32)

        def step(j, carry):
            sl = pl.ds(pl.multiple_of(j * tk, tk), tk)
            kj = k_ref[sl, :]
            vj = v_ref[sl, :]
            s = lax.dot_general(q, kj, _NT, preferred_element_type=F32)
            m_old = m_ref[...]
            m_new = jnp.maximum(m_old, jnp.max(s, axis=1, keepdims=True))
            alpha = jnp.exp(m_old - m_new)
            p = jnp.exp(s - m_new)
            l_ref[...] = alpha * l_ref[...] + jnp.sum(p, axis=1, keepdims=True)
            acc_ref[...] = alpha * acc_ref[...] + lax.dot_general(p.astype(BF16), vj, _NN, preferred_element_type=F32)
            m_ref[...] = m_new
            return carry

        lax.fori_loop(0, nk // tk, step, 0)
        l = l_ref[...]
        o_ref[...] = (acc_ref[...] / l).reshape(Q_GROUP, tq, HEAD_DIM).astype(o_ref.dtype)
        lse_ref[...] = (m_ref[...] + jnp.log(l)).reshape(Q_GROUP, tq, 1)

    qspec = pl.BlockSpec((Q_GROUP, tq, HEAD_DIM), lambda g, i: (g, i, 0))
    kspec = pl.BlockSpec((None, nk, HEAD_DIM), lambda g, i: (g, 0, 0))
    return pl.pallas_call(
        body, name=name, grid=(N_KV_HEADS, nq // tq),
        in_specs=[qspec, kspec, kspec],
        out_specs=[qspec, pl.BlockSpec((Q_GROUP, tq, 1), lambda g, i: (g, i, 0))],
        out_shape=[jax.ShapeDtypeStruct((N_Q_HEADS, nq, HEAD_DIM), BF16), jax.ShapeDtypeStruct((N_Q_HEADS, nq, 1), F32)],
        scratch_shapes=[pltpu.VMEM((rows, 1), F32), pltpu.VMEM((rows, 1), F32), pltpu.VMEM((rows, HEAD_DIM), F32)],
        compiler_params=pltpu.CompilerParams(dimension_semantics=("arbitrary", "arbitrary"),
                                             vmem_limit_bytes=V7X_VMEM_LIMIT),
    )(q8, k2, v2)


def _flash_dq(name, q8, k2, v2, o8, do8, lse):
    nq, nk = q8.shape[1], k2.shape[1]
    tq = _pick(nq, (256, 128))
    tk = _pick(nk, (256, 128))
    rows = Q_GROUP * tq

    def body(q_ref, k_ref, v_ref, o_ref, do_ref, lse_ref, dq_ref, dl_ref, acc_ref):
        q = q_ref[...].reshape(rows, HEAD_DIM)
        do = do_ref[...].reshape(rows, HEAD_DIM)
        delta = jnp.sum(do.astype(F32) * o_ref[...].reshape(rows, HEAD_DIM).astype(F32), axis=1, keepdims=True)
        lse_v = lse_ref[...].reshape(rows, 1)
        acc_ref[...] = jnp.zeros((rows, HEAD_DIM), F32)

        def step(j, carry):
            sl = pl.ds(pl.multiple_of(j * tk, tk), tk)
            kj = k_ref[sl, :]
            vj = v_ref[sl, :]
            s = lax.dot_general(q, kj, _NT, preferred_element_type=F32)
            p = jnp.exp(s - lse_v)
            dp = lax.dot_general(do, vj, _NT, preferred_element_type=F32)
            ds = p * (dp - delta)
            acc_ref[...] += lax.dot_general(ds.astype(BF16), kj, _NN, preferred_element_type=F32)
            return carry

        lax.fori_loop(0, nk // tk, step, 0)
        dq_ref[...] = acc_ref[...].reshape(Q_GROUP, tq, HEAD_DIM)
        dl_ref[...] = delta.reshape(Q_GROUP, tq, 1)

    qspec = pl.BlockSpec((Q_GROUP, tq, HEAD_DIM), lambda g, i: (g, i, 0))
    lspec = pl.BlockSpec((Q_GROUP, tq, 1), lambda g, i: (g, i, 0))
    kspec = pl.BlockSpec((None, nk, HEAD_DIM), lambda g, i: (g, 0, 0))
    return pl.pallas_call(
        body, name=name, grid=(N_KV_HEADS, nq // tq),
        in_specs=[qspec, kspec, kspec, qspec, qspec, lspec],
        out_specs=[qspec, lspec],
        out_shape=[jax.ShapeDtypeStruct((N_Q_HEADS, nq, HEAD_DIM), F32), jax.ShapeDtypeStruct((N_Q_HEADS, nq, 1), F32)],
        scratch_shapes=[pltpu.VMEM((rows, HEAD_DIM), F32)],
        compiler_params=pltpu.CompilerParams(dimension_semantics=("arbitrary", "arbitrary"),
                                             vmem_limit_bytes=V7X_VMEM_LIMIT),
    )(q8, k2, v2, o8, do8, lse)


def _flash_dkv(name, q8, k2, v2, do8, lse_t, delta_t):
    nq, nk = q8.shape[1], k2.shape[1]
    tq = _pick(nq, (512, 256, 128))
    tk = _pick(nk, (256, 128))

    def body(q_ref, k_ref, v_ref, do_ref, lse_ref, dl_ref, dk_ref, dv_ref, dk_acc, dv_acc):
        kj = k_ref[...]
        vj = v_ref[...]
        dk_acc[...] = jnp.zeros((tk, HEAD_DIM), F32)
        dv_acc[...] = jnp.zeros((tk, HEAD_DIM), F32)
        for h in range(Q_GROUP):
            def step(i, carry, h=h):
                sl = pl.ds(pl.multiple_of(i * tq, tq), tq)
                qh = q_ref[h, sl, :]
                doh = do_ref[h, sl, :]
                st = lax.dot_general(kj, qh, _NT, preferred_element_type=F32)
                pt = jnp.exp(st - lse_ref[h, :, sl])
                dv_acc[...] += lax.dot_general(pt.astype(BF16), doh, _NN, preferred_element_type=F32)
                dpt = lax.dot_general(vj, doh, _NT, preferred_element_type=F32)
                dst = pt * (dpt - dl_ref[h, :, sl])
                dk_acc[...] += lax.dot_general(dst.astype(BF16), qh, _NN, preferred_element_type=F32)
                return carry

            lax.fori_loop(0, nq // tq, step, 0)
        dk_ref[...] = dk_acc[...]
        dv_ref[...] = dv_acc[...]

    qspec = pl.BlockSpec((Q_GROUP, nq, HEAD_DIM), lambda g, j: (g, 0, 0))
    tspec = pl.BlockSpec((Q_GROUP, 1, nq), lambda g, j: (g, 0, 0))
    kspec = pl.BlockSpec((None, tk, HEAD_DIM), lambda g, j: (g, j, 0))
    return pl.pallas_call(
        body, name=name, grid=(N_KV_HEADS, nk // tk),
        in_specs=[qspec, kspec, kspec, qspec, tspec, tspec],
        out_specs=[kspec, kspec],
        out_shape=[jax.ShapeDtypeStruct((N_KV_HEADS, nk, HEAD_DIM), F32)] * 2,
        scratch_shapes=[pltpu.VMEM((tk, HEAD_DIM), F32), pltpu.VMEM((tk, HEAD_DIM), F32)],
        compiler_params=pltpu.CompilerParams(dimension_semantics=("arbitrary", "arbitrary"),
                                             vmem_limit_bytes=V7X_VMEM_LIMIT),
    )(q8, k2, v2, do8, lse_t, delta_t)


def _log_sigmoid(z):
    return jnp.minimum(z, 0.0) - jnp.log(1.0 + jnp.exp(-jnp.abs(z)))


def _gla_chunk(q, k, vs, a, w2, b2, state, *, rev):
    c = GLA_CHUNK
    r_i = lax.broadcasted_iota(jnp.int32, (c, c), 0)
    c_i = lax.broadcasted_iota(jnp.int32, (c, c), 1)
    tri = (c_i >= r_i) if rev else (c_i <= r_i)
    la = _log_sigmoid(_hdot(a, w2) + b2) * (1.0 / GLA_TAU)
    cum = _hdot(tri.astype(F32), la)
    tot = jnp.sum(la, axis=0, keepdims=True)
    tot_col = _hdot(la, jnp.ones((c, 1), F32), _TN)
    q_in = q * (GLA_DK ** -0.5) * jnp.exp(cum)
    k_in = k * jnp.exp(-cum)
    k_st = k * jnp.exp(tot - cum)
    lane = lax.broadcasted_iota(jnp.int32, (1, GLA_QK_W), 1)
    outs = []
    for h in range(GLA_HEADS):
        head = ((lane >= GLA_DK * h) & (lane < GLA_DK * (h + 1))).astype(F32)
        att = jnp.where(tri, _hdot(q_in * head, k_in, _NT), 0.0)
        outs.append(_hdot(att, vs[h]))
    o = jnp.concatenate(outs, axis=1) + _hdot(q_in, state)
    hr = lax.broadcasted_iota(jnp.int32, (GLA_QK_W, GLA_V_W), 0) // GLA_DK
    hc = lax.broadcasted_iota(jnp.int32, (GLA_QK_W, GLA_V_W), 1) // GLA_DV
    u = _hdot(k_st, jnp.concatenate(vs, axis=1), _TN) * (hr == hc).astype(F32)
    return o, jnp.exp(tot_col) * state + u


def _gla_tile_of(step, rev, nct, nt):
    if not rev:
        return step
    return jnp.where(step < nct, nct - 1 - step, nt - 1 - (step - nct))


def _gla_row_specs(R, tile):
    return [pl.BlockSpec((R, GLA_QK_W), lambda s: (tile(s), C_GQ // GLA_QK_W)),
            pl.BlockSpec((R, GLA_QK_W), lambda s: (tile(s), C_GK // GLA_QK_W)),
            pl.BlockSpec((R, GLA_V_W), lambda s: (tile(s), C_GV // GLA_V_W)),
            pl.BlockSpec((R, 128), lambda s: (tile(s), C_AF // 128))]


def _gla_dir(name, p, w2, b2, *, rev, R, nct, add=None):
    n = p.shape[0]
    nt = n // R
    nch = R // GLA_CHUNK
    tile = lambda s: _gla_tile_of(s, rev, nct, nt)
    order = range(nch - 1, -1, -1) if rev else range(nch)

    def body(q_ref, k_ref, v_ref, a_ref, w2_ref, b2_ref, *rest):
        if add is not None:
            add_ref, o_ref, ssave_ref, state = rest
        else:
            o_ref, ssave_ref, state = rest

        @pl.when(pl.program_id(0) == 0)
        def _():
            state[...] = jnp.zeros_like(state)

        for cix in order:
            rows = pl.ds(cix * GLA_CHUNK, GLA_CHUNK)
            vs = [v_ref[rows, GLA_DV * h:GLA_DV * (h + 1)] for h in range(GLA_HEADS)]
            s_in = state[...]
            ssave_ref[cix] = s_in
            o, s_out = _gla_chunk(q_ref[rows, :], k_ref[rows, :], vs, a_ref[rows, :], w2_ref[...], b2_ref[...], s_in, rev=rev)
            if add is not None:
                o = o + add_ref[rows, :]
            o_ref[rows, :] = o
            state[...] = s_out

    o_spec = pl.BlockSpec((R, GLA_V_W), lambda s: (tile(s), 0))
    in_specs = _gla_row_specs(R, tile) + [_full_spec(w2.shape), _full_spec(b2.shape)]
    args = [p, p, p, p, w2, b2]
    if add is not None:
        in_specs.append(o_spec)
        args.append(add)
    return pl.pallas_call(
        body, name=name, grid=(nt,), in_specs=in_specs,
        out_specs=[o_spec, pl.BlockSpec((nch, GLA_QK_W, GLA_V_W), lambda s: (tile(s), 0, 0))],
        out_shape=[jax.ShapeDtypeStruct((n, GLA_V_W), F32),
                   jax.ShapeDtypeStruct((n // GLA_CHUNK, GLA_QK_W, GLA_V_W), F32)],
        scratch_shapes=[pltpu.VMEM((GLA_QK_W, GLA_V_W), F32)],
        compiler_params=_ARB1,
    )(*args)


def _gla_dir_bwd(name, p, w2, b2, ssave, do, *, rev, R, nct, adds=None, out_dtype=F32):
    n = p.shape[0]
    nt = n // R
    nch = R // GLA_CHUNK
    tile = lambda s: _gla_tile_of(nt - 1 - s, rev, nct, nt)
    order = range(nch) if rev else range(nch - 1, -1, -1)
    widths = (GLA_QK_W, GLA_QK_W, GLA_V_W, 128)

    def body(q_ref, k_ref, v_ref, a_ref, w2_ref, b2_ref, ss_ref, do_ref, *rest):
        if adds is not None:
            add_refs, rest = rest[:4], rest[4:]
        dq_ref, dk_ref, dv_ref, da_ref, dw2_ref, db2_ref, dstate = rest

        @pl.when(pl.program_id(0) == 0)
        def _():
            dstate[...] = jnp.zeros_like(dstate)
            dw2_ref[...] = jnp.zeros_like(dw2_ref)
            db2_ref[...] = jnp.zeros_like(db2_ref)

        for cix in order:
            rows = pl.ds(cix * GLA_CHUNK, GLA_CHUNK)
            vs = [v_ref[rows, GLA_DV * h:GLA_DV * (h + 1)] for h in range(GLA_HEADS)]
            _, vjp = jax.vjp(functools.partial(_gla_chunk, rev=rev), q_ref[rows, :], k_ref[rows, :], vs, a_ref[rows, :],
                             w2_ref[...], b2_ref[...], ss_ref[cix])
            dq, dk, dvs, da, dw2, db2, ds = vjp((do_ref[rows, :], dstate[...]))
            dv = jnp.concatenate(dvs, axis=1)
            grads = [dq, dk, dv, da]
            if adds is not None:
                grads = [g + r[rows, :].astype(F32) for g, r in zip(grads, add_refs)]
            for ref, g in zip((dq_ref, dk_ref, dv_ref, da_ref), grads):
                ref[rows, :] = g.astype(ref.dtype)
            dw2_ref[...] += dw2
            db2_ref[...] += db2
            dstate[...] = ds

    d_specs = [pl.BlockSpec((R, w), lambda s: (tile(s), 0)) for w in widths]
    in_specs = (_gla_row_specs(R, tile) + [_full_spec(w2.shape), _full_spec(b2.shape),
                pl.BlockSpec((nch, GLA_QK_W, GLA_V_W), lambda s: (tile(s), 0, 0)),
                pl.BlockSpec((R, GLA_V_W), lambda s: (tile(s), 0))])
    args = [p, p, p, p, w2, b2, ssave, do]
    if adds is not None:
        in_specs += d_specs
        args += list(adds)
    return pl.pallas_call(
        body, name=name, grid=(nt,), in_specs=in_specs,
        out_specs=d_specs + [_full_spec(w2.shape), _full_spec(b2.shape)],
        out_shape=[jax.ShapeDtypeStruct((n, w), out_dtype) for w in widths]
        + [jax.ShapeDtypeStruct(w2.shape, F32), jax.ShapeDtypeStruct(b2.shape, F32)],
        scratch_shapes=[pltpu.VMEM((GLA_QK_W, GLA_V_W), F32)],
        compiler_params=_ARB1,
    )(*args)


def _final_loss(name, x, target, gf, R, nct):
    n = x.shape[0]

    def body(x_ref, t_ref, g_ref, loss_ref, dx_ref, dg_ref):
        i = pl.program_id(0)

        @pl.when(i == 0)
        def _():
            loss_ref[...] = jnp.zeros_like(loss_ref)
            dg_ref[...] = jnp.zeros_like(dg_ref)

        @pl.when(i < nct)
        def _():
            dx_ref[...] = jnp.zeros_like(dx_ref)

        @pl.when(i >= nct)
        def _():
            y, vjp = jax.vjp(_rms, x_ref[...], g_ref[...])
            err = y - t_ref[...]
            loss_ref[...] += jnp.sum(0.5 * jnp.mean(err * err, axis=-1, keepdims=True))
            dx, dg = vjp(err * (1.0 / D))
            dx_ref[...] = dx
            dg_ref[...] += dg

    return pl.pallas_call(
        body, name=name, grid=(n // R,),
        in_specs=[_row_spec(R, D, 0), pl.BlockSpec((R, D), lambda i: (jnp.maximum(i - nct, 0), 0)), _full_spec((1, D))],
        out_specs=[_full_spec((8, 128)), _row_spec(R, D, 0), _full_spec((1, D))],
        out_shape=[jax.ShapeDtypeStruct((8, 128), F32), jax.ShapeDtypeStruct((n, D), F32), jax.ShapeDtypeStruct((1, D), F32)],
        compiler_params=_ARB1,
    )(x, target, gf)


def _adamw(name, w, m, v, gparts):
    rows, cols = w.shape
    nparts = gparts.shape[0]
    tr = rows
    for cand in range(min(rows, 256), 15, -16):
        if rows % cand == 0:
            tr = cand
            break

    def body(w_ref, m_ref, v_ref, g_ref, go_ref, d_ref, mo_ref, vo_ref):
        g = g_ref[0].astype(F32)
        for k in range(1, nparts):
            g = g + g_ref[k].astype(F32)
        m_new = ADAM_B1 * m_ref[...] + (1.0 - ADAM_B1) * g
        v_new = ADAM_B2 * v_ref[...] + (1.0 - ADAM_B2) * (g * g)
        m_hat = m_new / (1.0 - ADAM_B1 ** ADAM_STEP)
        v_hat = v_new / (1.0 - ADAM_B2 ** ADAM_STEP)
        go_ref[...] = g
        d_ref[...] = -ADAM_LR * (m_hat / (jnp.sqrt(v_hat) + ADAM_EPS) + ADAM_WD * w_ref[...])
        mo_ref[...] = m_new
        vo_ref[...] = v_new

    spec = pl.BlockSpec((tr, cols), lambda i: (i, 0))
    return pl.pallas_call(
        body, name=name, grid=(rows // tr,),
        in_specs=[spec, spec, spec, pl.BlockSpec((nparts, tr, cols), lambda i: (0, i, 0))],
        out_specs=[spec] * 4, out_shape=[jax.ShapeDtypeStruct((rows, cols), F32)] * 4,
        compiler_params=_ARB1,
    )(w, m, v, gparts)


def _my_index():
    return 4 * lax.axis_index("x") + 2 * lax.axis_index("y") + lax.axis_index("c")


def _xor_peer(k):
    flip = lambda a, bit: (1 - a) if bit else a
    pos = (flip(lax.axis_index("x"), (k >> 2) & 1), flip(lax.axis_index("y"), (k >> 1) & 1), flip(lax.axis_index("c"), k & 1))
    return pos, 4 * pos[0] + 2 * pos[1] + pos[2]


def _exchange(name, x, *, gather):
    slab = x.shape if gather else x.shape[1:]

    def body(x_ref, o_ref, send_sems, recv_sems, local_sem):
        me = _my_index()
        mine = x_ref if gather else x_ref.at[me]
        local = pltpu.make_async_copy(mine, o_ref.at[me], local_sem)
        local.start()
        sends, recvs = [], []
        for k in range(1, N_DEV):
            pos, lin = _xor_peer(k)
            src = x_ref if gather else x_ref.at[lin]
            sends.append(pltpu.make_async_remote_copy(src_ref=src, dst_ref=o_ref.at[me], send_sem=send_sems.at[k - 1],
                                                      recv_sem=recv_sems.at[k - 1], device_id=pos, device_id_type=MESH_ID))
            recvs.append(pltpu.make_async_remote_copy(src_ref=src, dst_ref=o_ref.at[lin], send_sem=send_sems.at[k - 1],
                                                      recv_sem=recv_sems.at[k - 1], device_id=pos, device_id_type=MESH_ID))
        for cp in sends:
            cp.start()
        for cp in recvs:
            cp.wait_recv()
        for cp in sends:
            cp.wait_send()
        local.wait()

    hbm = pl.BlockSpec(memory_space=pltpu.HBM)
    return pl.pallas_call(
        body, name=name, in_specs=[hbm], out_specs=hbm,
        out_shape=jax.ShapeDtypeStruct((N_DEV,) + tuple(slab), x.dtype),
        scratch_shapes=[pltpu.SemaphoreType.DMA((N_DEV - 1,)), pltpu.SemaphoreType.DMA((N_DEV - 1,)), pltpu.SemaphoreType.DMA],
    )(x)


def _adaln_fwd(name, craw16, w_ada, b_cols):
    def body(c_ref, w_ref, b_ref, o_ref):
        cs = jax.nn.silu(c_ref[...]).astype(BF16)
        for l in range(2):
            o_ref[l] = lax.dot_general(cs, w_ref[l].astype(BF16), _NN, preferred_element_type=F32) + b_ref[l]

    return pl.pallas_call(
        body, name=name, out_shape=jax.ShapeDtypeStruct((2, 16, w_ada.shape[2]), F32),
        compiler_params=pltpu.CompilerParams(vmem_limit_bytes=V7X_VMEM_LIMIT),
    )(craw16, w_ada, b_cols)


def _adaln_bwd(name, craw16, w_ada, dm):
    def body(c_ref, w_ref, dm_ref, gw_ref, dc_ref):
        c = c_ref[...]
        sg = jax.nn.sigmoid(c)
        cs = c * sg
        row = lax.broadcasted_iota(jnp.int32, (8, 1), 0)
        dc = jnp.zeros((16, D), F32)
        for l in range(2):
            dmx = dm_ref[2 * l + 1]
            dmc = jnp.where(row == 0, jnp.sum(dm_ref[2 * l], axis=0, keepdims=True), 0.0)
            gw_ref[l] = _hdot(cs[0:8], dmx, _TN) + _hdot(cs[8:16], dmc, _TN)
            dc = dc + _hdot(jnp.concatenate([dmx, dmc], axis=0), w_ref[l], _NT)
        dc_ref[...] = dc * sg * (1.0 + c * (1.0 - sg))

    return pl.pallas_call(
        body, name=name,
        out_shape=[jax.ShapeDtypeStruct(w_ada.shape, F32), jax.ShapeDtypeStruct((16, D), F32)],
        compiler_params=pltpu.CompilerParams(vmem_limit_bytes=V7X_VMEM_LIMIT),
    )(craw16, w_ada, dm)


_IN_OFFS = [sum(IN_SPLITS[:k]) for k in range(len(IN_SPLITS) + 1)]
_MY_ORDER = (11, 12, 13, 0, 1, 2, 7, 10, 5, 6, 3, 4, 8, 9)


def _to_my_cols(w):
    parts = [w[..., _IN_OFFS[k]:_IN_OFFS[k + 1]] for k in _MY_ORDER]
    pad = jnp.zeros(w.shape[:-1] + (PW - IN_WIDTH,), w.dtype)
    return jnp.concatenate(parts + [pad], axis=-1)


def _from_my_cols(w):
    my_offs, pos = {}, 0
    for k in _MY_ORDER:
        my_offs[k] = pos
        pos += IN_SPLITS[k]
    return jnp.concatenate([w[..., my_offs[k]:my_offs[k] + IN_SPLITS[k]] for k in range(len(IN_SPLITS))], axis=-1)


def _unshard(g, axis):
    g = jnp.moveaxis(g, 0, axis)
    return g.reshape(g.shape[:axis] + (g.shape[axis] * g.shape[axis + 1],) + g.shape[axis + 2:])


def _shard(full, axis):
    s = full.shape
    g = full.reshape(s[:axis] + (N_DEV, s[axis] // N_DEV) + s[axis + 1:])
    return jnp.moveaxis(g, axis, 0)


def _pack(pieces, cols, dtype, row_mult):
    flat = jnp.concatenate([p.reshape(-1).astype(dtype) for p in pieces])
    per = cols * row_mult
    padn = (-flat.shape[0]) % per
    if padn:
        flat = jnp.concatenate([flat, jnp.zeros((padn,), dtype)])
    return flat.reshape(-1, cols)


def _pack8(pieces, cols, dtype, row_mult):
    flat = jnp.concatenate([p.reshape(N_DEV, -1).astype(dtype) for p in pieces], axis=1)
    per = cols * row_mult
    padn = (-flat.shape[1]) % per
    if padn:
        flat = jnp.concatenate([flat, jnp.zeros((N_DEV, padn), dtype)], axis=1)
    return flat.reshape(N_DEV, -1, cols)


def _unpack(flat, shapes):
    out, pos = [], 0
    for s in shapes:
        size = 1
        for d in s:
            size *= d
        out.append(flat[..., pos:pos + size].reshape(flat.shape[:-1] + tuple(s)))
        pos += size
    return out


def _heads_front(a, nh):
    return a.reshape(a.shape[0], nh, HEAD_DIM).transpose(1, 0, 2)


def _heads_back(a):
    return a.transpose(1, 0, 2).reshape(a.shape[1], a.shape[0] * HEAD_DIM)


def _rope_tables(t, tc):
    tok = jnp.arange(t, dtype=jnp.int32)
    inv_freq = ROPE_THETA ** (-jnp.arange(ROPE_FREQS, dtype=F32) / ROPE_FREQS)
    ang_r = (tok // GRID_W).astype(F32)[:, None] * inv_freq
    ang_c = (tok % GRID_W).astype(F32)[:, None] * inv_freq
    cos64 = jnp.concatenate([jnp.cos(ang_r), jnp.cos(ang_r), jnp.cos(ang_c), jnp.cos(ang_c)], axis=1)
    sin64 = jnp.concatenate([-jnp.sin(ang_r), jnp.sin(ang_r), -jnp.sin(ang_c), jnp.sin(ang_c)], axis=1)
    cos64 = jnp.concatenate([jnp.ones((tc, HEAD_DIM), F32), cos64], axis=0)
    sin64 = jnp.concatenate([jnp.zeros((tc, HEAD_DIM), F32), sin64], axis=0)
    return jnp.tile(cos64, (1, N_Q_HEADS)), jnp.tile(sin64, (1, N_Q_HEADS))


def _head_mean_matrix(width):
    i = jnp.arange(width) // HEAD_DIM
    return (i[:, None] == i[None, :]).astype(F32) / HEAD_DIM


def _head_tile_matrix(width):
    return (jnp.arange(HEAD_DIM)[:, None] == (jnp.arange(width) % HEAD_DIM)[None, :]).astype(F32)


def _attention_fwd(tag, qr, kr, vv, tc):
    q8, k2, v2 = _heads_front(qr, N_Q_HEADS), _heads_front(kr, N_KV_HEADS), _heads_front(vv, N_KV_HEADS)
    o_c, lse_c = _flash_fwd(tag + "_attn_ctx", q8[:, :tc], k2[:, :tc], v2[:, :tc])
    o_x, lse_x = _flash_fwd(tag + "_attn_lat", q8[:, tc:], k2, v2)
    o8 = jnp.concatenate([o_c, o_x], axis=1)
    lse = jnp.concatenate([lse_c, lse_x], axis=1)
    return _heads_back(o8), (q8, k2, v2, o8, lse)


def _attention_bwd(tag, saved, datt, tc):
    q8, k2, v2, o8, lse = saved
    do8 = _heads_front(datt.astype(BF16), N_Q_HEADS)
    parts = []
    for nm, qs, ks in (("ctx", slice(0, tc), slice(0, tc)), ("lat", slice(tc, None), slice(None))):
        q, k, v, o, do, ls = q8[:, qs], k2[:, ks], v2[:, ks], o8[:, qs], do8[:, qs], lse[:, qs]
        dq, delta = _flash_dq(tag + "_attn_dq_" + nm, q, k, v, o, do, ls)
        dk, dv = _flash_dkv(tag + "_attn_dkv_" + nm, q, k, v, do, ls.transpose(0, 2, 1), delta.transpose(0, 2, 1))
        parts.append((dq, dk, dv))
    (dq_c, dk_c, dv_c), (dq_x, dk_x, dv_x) = parts
    dq8 = jnp.concatenate([dq_c, dq_x], axis=1)
    dk2 = dk_x.at[:, :tc].add(dk_c)
    dv2 = dv_x.at[:, :tc].add(dv_c)
    return _heads_back(dq8), _heads_back(dk2), _heads_back(dv2)


def _layer_fwd(tag, x, w, modv, consts, R, nct, tc):
    sh1, sc1, g1, sh2, sc2, g2 = modv
    cosq, sinq, bdq, eq, bdk, ek = consts
    n = x.shape[0]
    (h1,) = _rowwise(tag + "_ln1", _f_lnmod, R, nct, [(x, D, 0, D)], [], [(w["norm1_g"], False)], [], [sh1, sc1], [(D, BF16)])
    p = _matmul(tag + "_in", h1, w["w_in"], "nn", F32)
    qk_rows = [(p, Q_W, C_Q // Q_W, Q_W), (p, KV_W, C_K // KV_W, KV_W)]
    qk_consts = [(cosq, Q_W, 0, Q_W), (sinq, Q_W, 0, Q_W), (cosq, KV_W, 0, KV_W), (sinq, KV_W, 0, KV_W)]
    qk_params = [(w["q_norm_g"], False), (w["k_norm_g"], False)]
    qk_cparams = [(bdq, False), (eq, False), (bdk, False), (ek, False)]
    qr, kr = _rowwise(tag + "_qk", _f_qknorm, R, nct, qk_rows, qk_consts, qk_params, qk_cparams, [],
                      [(Q_W, BF16), (KV_W, BF16)], post=_qk_post)
    vv = p[:, C_VV:C_VV + KV_W].astype(BF16)
    att, att_saved = _attention_fwd(tag, qr, kr, vv, tc)

    o_f, s_f = _gla_dir(tag + "_gla_f", p, w["w2p_f"], w["b2_f"], rev=False, R=R, nct=nct)
    o_fb, s_b = _gla_dir(tag + "_gla_b", p, w["w2p_b"], w["b2_b"], rev=True, R=R, nct=nct, add=o_f)
    go_rows = [(o_fb, GLA_V_W, 0, GLA_DV), (p, GLA_V_W, C_R // GLA_V_W, GLA_DV)]
    (gla,) = _rowwise(tag + "_glaout", _f_glaout, R, nct, go_rows, [], [(w["gla_norm_g"], True)], [], [], [(GLA_V_W, BF16)])

    rg = GMLP_CHUNK
    gm_rows = [(p, GMLP_W, C_U // GMLP_W, GMLP_W), (p, GMLP_W, C_V // GMLP_W, GMLP_W // GMLP_GROUPS)]
    gm_params = [(w["gmlp_norm_g"], True), (w["w_spatial"], True), (w["b_spatial_t"], True)]
    (gm,) = _rowwise(tag + "_gmlp", _f_gmlp, rg, tc // rg, gm_rows, [], gm_params, [], [], [(GMLP_W, BF16)])

    ya = _matmul(tag + "_br_a", gm, w["w_br_a"], "nn", F32)
    yb = _matmul(tag + "_br_b", att, w["w_br_b"], "nn", F32)
    yc = _matmul(tag + "_br_c", gla, w["w_br_c"], "nn", F32)
    mg_rows = [(p, D, C_GA // D, D), (p, D, C_GB // D, D), (p, D, C_GC // D, D), (ya, D, 0, D), (yb, D, 0, D), (yc, D, 0, D)]
    (merged,) = _rowwise(tag + "_merge", _f_merge, R, nct, mg_rows, [], [], [], [], [(D, BF16)])
    mix = _matmul(tag + "_out", merged, w["w_out"], "nn", F32)
    (x_mid,) = _rowwise(tag + "_res1", _f_resid, R, nct, [(x, D, 0, D), (mix, D, 0, D)], [], [], [], [g1], [(D, F32)])

    (h2,) = _rowwise(tag + "_ln2", _f_lnmod, R, nct, [(x_mid, D, 0, D)], [], [(w["norm2_g"], False)], [], [sh2, sc2], [(D, BF16)])
    a2 = _matmul(tag + "_up", h2, w["w_ffn_up"], "nn", F32, o_halves=True)
    gv = _conv_gate(tag + "_conv", a2, w["conv_w_h"], w["conv_b_h"], R, tc)
    ffn = _matmul(tag + "_down", gv, w["w_ffn_down"], "nn", F32)
    (x_next,) = _rowwise(tag + "_res2", _f_resid, R, nct, [(x_mid, D, 0, D), (ffn, D, 0, D)], [], [], [], [g2], [(D, F32)])
    saved = dict(x=x, h1=h1, p=p, att_saved=att_saved, att=att, o_fb=o_fb, s_f=s_f, s_b=s_b, gla=gla, gm=gm,
                 ya=ya, yb=yb, yc=yc, merged=merged, mix=mix, x_mid=x_mid, h2=h2, a2=a2, gv=gv, ffn=ffn,
                 qk=(qk_rows, qk_consts, qk_params, qk_cparams), go_rows=go_rows, gm_info=(gm_rows, gm_params),
                 mg_rows=mg_rows)
    return x_next, saved


def _layer_bwd(tag, dx_next, s, w, modv, R, nct, tc):
    sh1, sc1, g1, sh2, sc2, g2 = modv
    gw = {}
    (dffn,), _, (dg2,) = _rowwise_bwd(tag + "_res2_b", _f_resid, R, nct, [(s["ffn"], D, 0, D), (s["ffn"], D, 0, D)], [], [], [], [g2],
                                      [(dx_next, D)], [None, BF16])
    dgv = _matmul(tag + "_down_da", dffn, w["w_ffn_down"], "nt", F32)
    gw["w_ffn_down"] = _matmul(tag + "_down_dw", s["gv"], dffn, "tn", F32)
    da2, dcw, dcb = _conv_gate_bwd(tag + "_conv_b", s["a2"], dgv, w["conv_w_h"], w["conv_b_h"], R, tc)
    gw["conv_w_h"], gw["conv_b_h"] = dcw, dcb
    dh2 = _matmul(tag + "_up_da", da2, w["w_ffn_up"], "nt", F32, a_halves=True)
    gw["w_ffn_up"] = _matmul(tag + "_up_dw", s["h2"], da2, "tn", F32, b_halves=True)
    (dx_mid,), (gw["norm2_g"],), (dsh2, dsc2) = _rowwise_bwd(
        tag + "_ln2_b", _f_lnmod, R, nct, [(s["x_mid"], D, 0, D)], [], [(w["norm2_g"], False)], [], [sh2, sc2],
        [(dh2, D)], [F32], adds=[dx_next])
    (dmix,), _, (dg1,) = _rowwise_bwd(tag + "_res1_b", _f_resid, R, nct, [(s["mix"], D, 0, D), (s["mix"], D, 0, D)], [], [], [], [g1],
                                      [(dx_mid, D)], [None, BF16])
    dmerged = _matmul(tag + "_out_da", dmix, w["w_out"], "nt", F32)
    gw["w_out"] = _matmul(tag + "_out_dw", s["merged"], dmix, "tn", F32)
    (dga, dgb, dgc, dya, dyb, dyc), _, _ = _rowwise_bwd(tag + "_merge_b", _f_merge, R, nct, s["mg_rows"], [], [], [], [],
                                                        [(dmerged, D)], [BF16] * 6)
    dgm = _matmul(tag + "_br_a_da", dya, w["w_br_a"], "nt", F32)
    datt = _matmul(tag + "_br_b_da", dyb, w["w_br_b"], "nt", F32)
    dgla = _matmul(tag + "_br_c_da", dyc, w["w_br_c"], "nt", F32)
    gm_rows, gm_params = s["gm_info"]
    gw["w_br_a"] = _matmul(tag + "_br_a_dw", s["gm"], dya, "tn", F32)
    gw["w_br_b"] = _matmul(tag + "_br_b_dw", s["att"], dyb, "tn", F32)
    gw["w_br_c"] = _matmul(tag + "_br_c_dw", s["gla"], dyc, "tn", F32)
    rg = GMLP_CHUNK
    (du, dv_), (gw["gmlp_norm_g"], gw["w_spatial"], gw["b_spatial_t"]), _ = _rowwise_bwd(
        tag + "_gmlp_b", _f_gmlp, rg, tc // rg, gm_rows, [], gm_params, [], [], [(dgm, GMLP_W)], [BF16, BF16])
    (do, dr), (gw["gla_norm_g"],), _ = _rowwise_bwd(tag + "_glaout_b", _f_glaout, R, nct, s["go_rows"], [],
                                                    [(w["gla_norm_g"], True)], [], [], [(dgla, GLA_V_W)], [F32, BF16])
    p = s["p"]
    *d_b, gw["w2p_b"], gw["b2_b"] = _gla_dir_bwd(tag + "_gla_b_b", p, w["w2p_b"], w["b2_b"], s["s_b"], do, rev=True, R=R, nct=nct)
    dgq, dgk, dgv_, daf, gw["w2p_f"], gw["b2_f"] = _gla_dir_bwd(tag + "_gla_f_b", p, w["w2p_f"], w["b2_f"], s["s_f"], do,
                                                              rev=False, R=R, nct=nct, adds=d_b, out_dtype=BF16)
    dqr, dkr, dvv = _attention_bwd(tag, s["att_saved"], datt, tc)
    qk_rows, qk_consts, qk_params, qk_cparams = s["qk"]
    (dq, dk), (gw["q_norm_g"], gw["k_norm_g"]), _ = _rowwise_bwd(
        tag + "_qk_b", _f_qknorm, R, nct, qk_rows, qk_consts, qk_params, qk_cparams, [],
        [(dqr, Q_W), (dkr, KV_W)], [BF16, BF16], pre=_qk_pre)
    dp = jnp.concatenate([dga, dgb, dgc, du, dv_, dq, dgv_, dr, dgq, dgk, dk, dvv.astype(BF16), daf,
                          jnp.zeros((p.shape[0], PW - C_AF - 128), BF16)], axis=1)
    dh1 = _matmul(tag + "_in_da", dp, w["w_in"], "nt", F32)
    gw["w_in"] = _matmul(tag + "_in_dw", s["h1"], dp, "tn", F32)
    (dx,), (gw["norm1_g"],), (dsh1, dsc1) = _rowwise_bwd(
        tag + "_ln1_b", _f_lnmod, R, nct, [(s["x"], D, 0, D)], [], [(w["norm1_g"], False)], [], [sh1, sc1],
        [(dh1, D)], [F32], adds=[dx_mid])
    return dx, gw, (dsh1, dsc1, dg1, dsh2, dsc2, dg2)


_BIG = (("w_in", 1), ("w_br_a", 1), ("w_br_b", 1), ("w_br_c", 1), ("w_out", 0), ("w_ffn_up", 1), ("w_ffn_down", 0))
_SMALL_SHARDED = (("conv_w", 1), ("w_alpha2", 2), ("b_alpha", 1))
_REPLICATED = ("c_ctx", "b_ada", "norm1_g", "norm2_g", "q_norm_g", "k_norm_g", "gmlp_norm_g", "w_spatial", "b_spatial",
               "gla_norm_g", "conv_b", "final_norm_g")
_WEIGHTS = ("c_ctx", "w_ada", "b_ada", "norm1_g", "norm2_g", "w_in", "q_norm_g", "k_norm_g", "gmlp_norm_g", "w_spatial",
            "b_spatial", "w_alpha2", "b_alpha", "gla_norm_g", "w_br_a", "w_br_b", "w_br_c", "w_out", "w_ffn_up", "conv_w",
            "conv_b", "w_ffn_down", "final_norm_g")
_BIG_COLS = 1024
_SMALL_COLS = 128


def _decay_weights(w_alpha2_l, b_alpha_l):
    out = []
    for d in range(2):
        w2p = jnp.zeros((128, GLA_QK_W), F32).at[GLA_RANK * d:GLA_RANK * (d + 1)].set(w_alpha2_l[d])
        out += [w2p, b_alpha_l[d][None, :]]
    return out


def _step(inp, wts, moms, vels):
    x, c, ctx, loss_target = inp
    t, tc = x.shape[1], ctx.shape[1]
    n = t + tc
    R = min(256, tc)
    nct = tc // R
    me = _my_index()
    depth = wts["w_in"].shape[0]

    big_shapes = [wts[nm].shape[1:] for nm, _ in _BIG] * depth
    big_pack = _pack([wts[nm][l] for l in range(depth) for nm, _ in _BIG], _BIG_COLS, BF16, 16)
    big_all = _exchange("gather_big", big_pack, gather=True).reshape(N_DEV, -1)
    big_full = [_unshard(g, ax) for g, (_, ax) in zip(_unpack(big_all, big_shapes), _BIG * depth)]
    small_shapes = [wts[nm].shape[1:] for nm, _ in _SMALL_SHARDED] * depth
    small_pack = _pack([wts[nm][l] for l in range(depth) for nm, _ in _SMALL_SHARDED], _SMALL_COLS, F32, 8)
    small_all = _exchange("gather_small", small_pack, gather=True).reshape(N_DEV, -1)
    small_full = [_unshard(g, ax) for g, (_, ax) in zip(_unpack(small_all, small_shapes), _SMALL_SHARDED * depth)]

    layers = []
    for l in range(depth):
        w = dict(zip([nm for nm, _ in _BIG], big_full[l * len(_BIG):(l + 1) * len(_BIG)]))
        w["w_in"] = _to_my_cols(w["w_in"])
        conv_w, w_alpha2, b_alpha = small_full[l * 3:(l + 1) * 3]
        w["conv_w_h"] = conv_w.reshape(3, 2, FFN_H).transpose(1, 0, 2)
        w["conv_b_h"] = wts["conv_b"][l].reshape(2, 1, FFN_H)
        w["w2p_f"], w["b2_f"], w["w2p_b"], w["b2_b"] = _decay_weights(w_alpha2, b_alpha)
        w["norm1_g"] = wts["norm1_g"][l][None, :]
        w["norm2_g"] = wts["norm2_g"][l][None, :]
        w["q_norm_g"] = wts["q_norm_g"][l][None, :]
        w["k_norm_g"] = wts["k_norm_g"][l][None, :]
        w["gmlp_norm_g"] = wts["gmlp_norm_g"][l].reshape(GMLP_GROUPS, 1, GMLP_W // GMLP_GROUPS)
        w["w_spatial"] = wts["w_spatial"][l]
        w["b_spatial_t"] = wts["b_spatial"][l][:, :, None]
        w["gla_norm_g"] = wts["gla_norm_g"][l].reshape(GLA_HEADS, 1, GLA_DV)
        layers.append(w)

    c8 = jnp.concatenate([c, jnp.zeros((7, D), F32)], axis=0)
    c_all = _exchange("gather_c", c8, gather=True)[:, 0, :]
    craw16 = jnp.concatenate([c_all, wts["c_ctx"][None, :], jnp.zeros((7, D), F32)], axis=0)
    acols = wts["w_ada"].shape[2]
    b_cols = lax.dynamic_slice_in_dim(wts["b_ada"], me * acols, acols, axis=1)[:, None, :]
    mod_part = _adaln_fwd("adaln", craw16, wts["w_ada"], b_cols)
    send = jnp.stack([mod_part[:, 8, :][None].repeat(N_DEV, 0), mod_part[:, :8, :].transpose(1, 0, 2)], axis=2)
    send = jnp.concatenate([send.reshape(N_DEV, 2 * depth, acols), jnp.zeros((N_DEV, 8 - 2 * depth, acols), F32)], axis=1)
    got = _exchange("scatter_mod", send, gather=False)
    mod = got[:, :2 * depth, :].transpose(1, 0, 2).reshape(depth, 2, N_MOD, 1, D)
    modv = [[mod[l, :, k] for k in range(N_MOD)] for l in range(depth)]

    cosq, sinq = _rope_tables(t, tc)
    consts = (cosq, sinq, _head_mean_matrix(Q_W), _head_tile_matrix(Q_W), _head_mean_matrix(KV_W), _head_tile_matrix(KV_W))
    xs = jnp.concatenate([ctx[0], x[0]], axis=0)
    saved = []
    for l in range(depth):
        xs, sv = _layer_fwd("l%d" % l, xs, layers[l], modv[l], consts, R, nct, tc)
        saved.append(sv)
    loss_blk, dxs, dgf = _final_loss("final", xs, loss_target[0], wts["final_norm_g"][None, :], R, nct)
    loss = lax.psum(loss_blk[0, 0], ("x", "y", "c"))

    grads = [None] * depth
    dmods = [None] * depth
    for l in range(depth - 1, -1, -1):
        dxs, grads[l], dmods[l] = _layer_bwd("l%d" % l, dxs, saved[l], layers[l], modv[l], R, nct, tc)
    grad_x = dxs[tc:][None]

    dmod = jnp.stack([jnp.stack(dmods[l], axis=1) for l in range(depth)])
    dmod = dmod.reshape(depth, 2, N_DEV, acols).transpose(2, 0, 1, 3).reshape(N_DEV, 2 * depth, acols)
    dmod_send = jnp.concatenate([dmod, jnp.zeros((N_DEV, 8 - 2 * depth, acols), F32)], axis=1)
    dm_got = _exchange("scatter_dmod", dmod_send, gather=False)
    g_w_ada, dc16 = _adaln_bwd("adaln_b", craw16, wts["w_ada"], dm_got[:, :2 * depth].transpose(1, 0, 2))
    db_ada_part = jnp.stack([jnp.stack(dmods[l], axis=1) for l in range(depth)]).reshape(depth, 2, N_MOD * D).sum(axis=1)

    out = {}
    big_g = []
    for l in range(depth):
        g = dict(grads[l])
        g["w_in"] = _from_my_cols(g["w_in"])
        big_g += [_shard(g[nm], ax) for nm, ax in _BIG]
    g_parts = _exchange("scatter_big_grads", _pack8(big_g, _BIG_COLS, BF16, 16), gather=False)
    wpk = lambda src: _pack([src[nm][l] for l in range(depth) for nm, _ in _BIG], _BIG_COLS, F32, 16)
    res = _adamw("adamw_big", wpk(wts), wpk(moms), wpk(vels), g_parts)
    for kind, flat in zip(("grad", "delta", "new_m", "new_v"), res):
        pieces = _unpack(flat.reshape(-1), big_shapes)
        for k, (nm, _) in enumerate(_BIG):
            out[kind, nm] = jnp.stack([pieces[l * len(_BIG) + k] for l in range(depth)])

    small_g = []
    for l in range(depth):
        g = grads[l]
        conv_w_g = g["conv_w_h"].transpose(1, 0, 2).reshape(3, F2)
        w_alpha2_g = jnp.stack([g["w2p_f"][:GLA_RANK], g["w2p_b"][GLA_RANK:2 * GLA_RANK]])
        b_alpha_g = jnp.stack([g["b2_f"][0], g["b2_b"][0]])
        small_g += [_shard(conv_w_g, 1), _shard(w_alpha2_g, 2), _shard(b_alpha_g, 1)]
    sg_parts = _exchange("scatter_small_grads", _pack8(small_g, _SMALL_COLS, F32, 16), gather=False)
    spk = lambda src: _pack([src[nm][l] for l in range(depth) for nm, _ in _SMALL_SHARDED], _SMALL_COLS, F32, 16)
    res = _adamw("adamw_small", spk(wts), spk(moms), spk(vels), sg_parts)
    for kind, flat in zip(("grad", "delta", "new_m", "new_v"), res):
        pieces = _unpack(flat.reshape(-1), small_shapes)
        for k, (nm, _) in enumerate(_SMALL_SHARDED):
            out[kind, nm] = jnp.stack([pieces[l * 3 + k] for l in range(depth)])

    rep_g = dict(
        c_ctx=dc16[8], b_ada=db_ada_part, final_norm_g=dgf[0],
        norm1_g=jnp.stack([grads[l]["norm1_g"][0] for l in range(depth)]),
        norm2_g=jnp.stack([grads[l]["norm2_g"][0] for l in range(depth)]),
        q_norm_g=jnp.stack([grads[l]["q_norm_g"][0] for l in range(depth)]),
        k_norm_g=jnp.stack([grads[l]["k_norm_g"][0] for l in range(depth)]),
        gmlp_norm_g=jnp.stack([grads[l]["gmlp_norm_g"].reshape(GMLP_W) for l in range(depth)]),
        w_spatial=jnp.stack([grads[l]["w_spatial"] for l in range(depth)]),
        b_spatial=jnp.stack([grads[l]["b_spatial_t"][:, :, 0] for l in range(depth)]),
        gla_norm_g=jnp.stack([grads[l]["gla_norm_g"].reshape(GLA_V_W) for l in range(depth)]),
        conv_b=jnp.stack([grads[l]["conv_b_h"].reshape(F2) for l in range(depth)]),
    )
    rep_shapes = [wts[nm].shape for nm in _REPLICATED]
    rg_parts = _exchange("gather_rep_grads", _pack([rep_g[nm] for nm in _REPLICATED], _SMALL_COLS, F32, 16), gather=True)
    rpk = lambda src: _pack([src[nm] for nm in _REPLICATED], _SMALL_COLS, F32, 16)
    res = _adamw("adamw_rep", rpk(wts), rpk(moms), rpk(vels), rg_parts)
    for kind, flat in zip(("grad", "delta", "new_m", "new_v"), res):
        for nm, piece in zip(_REPLICATED, _unpack(flat.reshape(-1), rep_shapes)):
            out[kind, nm] = piece

    ada2 = lambda a: a.reshape(-1, acols)
    res = _adamw("adamw_ada", ada2(wts["w_ada"]), ada2(moms["w_ada"]), ada2(vels["w_ada"]), ada2(g_w_ada)[None])
    for kind, flat in zip(("grad", "delta", "new_m", "new_v"), res):
        out[kind, "w_ada"] = flat.reshape(wts["w_ada"].shape)

    return (loss, grad_x, *[out[kind, nm] for kind in ("grad", "delta", "new_m", "new_v") for nm in _WEIGHTS])


def kernel(x, c, ctx, c_ctx, w_ada, b_ada, norm1_g, norm2_g, w_in, q_norm_g, k_norm_g, gmlp_norm_g, w_spatial, b_spatial, w_alpha2, b_alpha, gla_norm_g, w_br_a, w_br_b, w_br_c, w_out, w_ffn_up, conv_w, conv_b, w_ffn_down, final_norm_g, loss_target, m_c_ctx, m_w_ada, m_b_ada, m_norm1_g, m_norm2_g, m_w_in, m_q_norm_g, m_k_norm_g, m_gmlp_norm_g, m_w_spatial, m_b_spatial, m_w_alpha2, m_b_alpha, m_gla_norm_g, m_w_br_a, m_w_br_b, m_w_br_c, m_w_out, m_w_ffn_up, m_conv_w, m_conv_b, m_w_ffn_down, m_final_norm_g, v_c_ctx, v_w_ada, v_b_ada, v_norm1_g, v_norm2_g, v_w_in, v_q_norm_g, v_k_norm_g, v_gmlp_norm_g, v_w_spatial, v_b_spatial, v_w_alpha2, v_b_alpha, v_gla_norm_g, v_w_br_a, v_w_br_b, v_w_br_c, v_w_out, v_w_ffn_up, v_conv_w, v_conv_b, v_w_ffn_down, v_final_norm_g):
    wts = dict(zip(_WEIGHTS, (c_ctx, w_ada, b_ada, norm1_g, norm2_g, w_in, q_norm_g, k_norm_g, gmlp_norm_g, w_spatial, b_spatial,
                              w_alpha2, b_alpha, gla_norm_g, w_br_a, w_br_b, w_br_c, w_out, w_ffn_up, conv_w, conv_b, w_ffn_down,
                              final_norm_g)))
    moms = dict(zip(_WEIGHTS, (m_c_ctx, m_w_ada, m_b_ada, m_norm1_g, m_norm2_g, m_w_in, m_q_norm_g, m_k_norm_g, m_gmlp_norm_g,
                               m_w_spatial, m_b_spatial, m_w_alpha2, m_b_alpha, m_gla_norm_g, m_w_br_a, m_w_br_b, m_w_br_c, m_w_out,
                               m_w_ffn_up, m_conv_w, m_conv_b, m_w_ffn_down, m_final_norm_g)))
    vels = dict(zip(_WEIGHTS, (v_c_ctx, v_w_ada, v_b_ada, v_norm1_g, v_norm2_g, v_w_in, v_q_norm_g, v_k_norm_g, v_gmlp_norm_g,
                               v_w_spatial, v_b_spatial, v_w_alpha2, v_b_alpha, v_gla_norm_g, v_w_br_a, v_w_br_b, v_w_br_c, v_w_out,
                               v_w_ffn_up, v_conv_w, v_conv_b, v_w_ffn_down, v_final_norm_g)))
    return _step((x, c, ctx, loss_target), wts, moms, vels)
```

```python
import functools

import jax
import jax.numpy as jnp
from jax import lax
from jax.experimental import pallas as pl
from jax.experimental.pallas import tpu as pltpu

F32 = jnp.float32
BF16 = jnp.bfloat16
HI = lax.Precision.HIGHEST
MESH_ID = pl.DeviceIdType.MESH

N_DEV = 8
EPS = 1e-6
D = 1024
N_MOD = 6
HEAD_DIM = 64
N_Q_HEADS = 8
N_KV_HEADS = 2
Q_GROUP = 4
Q_W = 512
KV_W = 128
GRID_W = 64
ROPE_THETA = 10000.0
ROPE_FREQS = 16
GMLP_CHUNK = 128
GMLP_GROUPS = 4
GMLP_W = 512
GLA_HEADS = 4
GLA_QK_W = 256
GLA_V_W = 512
GLA_DK = 64
GLA_DV = 128
GLA_RANK = 16
GLA_TAU = 16.0
GLA_CHUNK = 64
FFN_H = 2816
F2 = 2 * FFN_H
IN_SPLITS = (512, 512, 512, 128, 128, 256, 256, 512, 16, 16, 512, 1024, 1024, 1024)
IN_WIDTH = sum(IN_SPLITS)

C_GA, C_GB, C_GC = 0, 1024, 2048
C_U, C_V, C_Q, C_GV, C_R = 3072, 3584, 4096, 4608, 5120
C_GQ, C_GK = 5632, 5888
C_K, C_VV, C_AF = 6144, 6272, 6400
PW = 6656

ADAM_LR = 0.001
ADAM_B1 = 0.9
ADAM_B2 = 0.999
ADAM_EPS = 1e-08
ADAM_WD = 0.01
ADAM_STEP = 10

V7X_VMEM_LIMIT = 56 * 1024 * 1024

_ARB1 = pltpu.CompilerParams(dimension_semantics=("arbitrary",), vmem_limit_bytes=V7X_VMEM_LIMIT)


def _pick(dim, prefs):
    for p in prefs:
        if dim % p == 0:
            return p
    return dim


def _hdot(a, b, dims=(((1,), (0,)), ((), ()))):
    return lax.dot_general(a, b, dims, precision=HI, preferred_element_type=F32)


_NT = (((1,), (1,)), ((), ()))
_TN = (((0,), (0,)), ((), ()))
_NN = (((1,), (0,)), ((), ()))


def _matmul(name, a, b, mode, out_dtype, *, a_halves=False, b_halves=False, o_halves=False):
    def dims2(x, halves):
        return (x.shape[1], 2 * x.shape[2]) if halves else x.shape

    ar, ac = dims2(a, a_halves)
    br, bc = dims2(b, b_halves)
    if mode == "nn":
        M, K, N = ar, ac, bc
    elif mode == "nt":
        M, K, N = ar, ac, br
    else:
        M, K, N = ac, ar, bc
    row_prefs = (768, 512, 384, 256, 128)
    n_unit = N // 2 if (o_halves or (b_halves and mode != "nt")) else N
    k_unit = K // 2 if (a_halves and mode != "tn") else K
    if mode == "tn":
        tm = _pick(M, (1024, 1408, 512, 256, 128))
        tk = _pick(K, row_prefs)
    else:
        tm = _pick(M, row_prefs)
        tk = _pick(k_unit, (1024, 1408, 512, 256, 128))
    tn = _pick(n_unit, (512, 1408, 256, 128))
    nk = K // tk

    def spec(shape2, halves, blk, imap):
        if not halves:
            return pl.BlockSpec(blk, imap)
        nhalf = (shape2[1] // 2) // blk[1]

        def im(i, j, k):
            r, c = imap(i, j, k)
            return (c // nhalf, r, c % nhalf)
        return pl.BlockSpec((None,) + blk, im)

    if mode == "nn":
        a_spec = spec((ar, ac), a_halves, (tm, tk), lambda i, j, k: (i, k))
        b_spec = spec((br, bc), b_halves, (tk, tn), lambda i, j, k: (k, j))
        dn = _NN
    elif mode == "nt":
        a_spec = spec((ar, ac), a_halves, (tm, tk), lambda i, j, k: (i, k))
        b_spec = spec((br, bc), b_halves, (tn, tk), lambda i, j, k: (j, k))
        dn = _NT
    else:
        a_spec = spec((ar, ac), a_halves, (tk, tm), lambda i, j, k: (k, i))
        b_spec = spec((br, bc), b_halves, (tk, tn), lambda i, j, k: (k, j))
        dn = _TN
    o_spec = spec((M, N), o_halves, (tm, tn), lambda i, j, k: (i, j))
    o_shape = (2, M, N // 2) if o_halves else (M, N)

    def body(a_ref, b_ref, o_ref, acc_ref):
        k = pl.program_id(2)
        part = lax.dot_general(a_ref[...], b_ref[...], dn, preferred_element_type=F32)
        if nk == 1:
            o_ref[...] = part.astype(o_ref.dtype)
        else:
            @pl.when(k == 0)
            def _():
                acc_ref[...] = part

            @pl.when(k > 0)
            def _():
                acc_ref[...] += part

            @pl.when(k == nk - 1)
            def _():
                o_ref[...] = acc_ref[...].astype(o_ref.dtype)

    return pl.pallas_call(
        body, name=name, grid=(M // tm, N // tn, nk),
        in_specs=[a_spec, b_spec], out_specs=o_spec,
        out_shape=jax.ShapeDtypeStruct(o_shape, out_dtype),
        scratch_shapes=[pltpu.VMEM((tm, tn), F32)],
        compiler_params=pltpu.CompilerParams(dimension_semantics=("parallel", "parallel", "arbitrary"),
                                             vmem_limit_bytes=V7X_VMEM_LIMIT),
    )(a, b)


def _full_spec(shape):
    nd = len(shape)
    return pl.BlockSpec(tuple(shape), lambda i, _nd=nd: (0,) * _nd)


def _row_spec(R, W, cb):
    return pl.BlockSpec((R, W), lambda i, _cb=cb: (i, _cb))


def _load_rows(refs, specs):
    vals = []
    for ref, (_, W, _, pw) in zip(refs, specs):
        if pw == W:
            vals.append(ref[...].astype(F32))
        else:
            vals.append([ref[:, k * pw:(k + 1) * pw].astype(F32) for k in range(W // pw)])
    return vals


def _load_params(refs, specs):
    vals = []
    for ref, (arr, split) in zip(refs, specs):
        if split:
            vals.append([ref[k] for k in range(arr.shape[0])])
        else:
            vals.append(ref[...])
    return vals


def _mod_spec(nct, width):
    return pl.BlockSpec((None, 1, width), lambda i: (jnp.minimum(i // nct, 1), 0, 0))


def _rowwise(name, f, R, nct, rows, consts, params, cparams, mods, outs, post=None):
    n = rows[0][0].shape[0]
    nr, nc, npar, ncp, nm = len(rows), len(consts), len(params), len(cparams), len(mods)

    def body(*refs):
        pos = 0
        rr = refs[pos:pos + nr]; pos += nr
        cr = refs[pos:pos + nc]; pos += nc
        pr = refs[pos:pos + npar]; pos += npar
        cpr = refs[pos:pos + ncp]; pos += ncp
        mr = refs[pos:pos + nm]; pos += nm
        orefs = refs[pos:]
        res = f(_load_rows(rr, rows), _load_params(pr, params), [m[...] for m in mr],
                _load_rows(cr, consts), _load_params(cpr, cparams))
        if post is not None:
            res = post(res, _load_rows(cr, consts))
        for o_ref, r in zip(orefs, res):
            o_ref[...] = r.astype(o_ref.dtype)

    in_specs = ([_row_spec(R, W, cb) for (_, W, cb, _) in rows + consts]
                + [_full_spec(a.shape) for (a, _) in params + cparams]
                + [_mod_spec(nct, m.shape[2]) for m in mods])
    args = [a for (a, _, _, _) in rows + consts] + [a for (a, _) in params + cparams] + list(mods)
    return pl.pallas_call(
        body, name=name, grid=(n // R,), in_specs=in_specs,
        out_specs=[_row_spec(R, w, 0) for (w, _) in outs],
        out_shape=[jax.ShapeDtypeStruct((n, w), dt) for (w, dt) in outs],
        compiler_params=_ARB1,
    )(*args)


def _rowwise_bwd(name, f, R, nct, rows, consts, params, cparams, mods, douts, drow, adds=None, pre=None):
    n = rows[0][0].shape[0]
    adds = adds or [None] * len(rows)
    nr, nc, npar, ncp, nm, nd = len(rows), len(consts), len(params), len(cparams), len(mods), len(douts)
    add_ix = [k for k in range(nr) if adds[k] is not None]
    out_ix = [k for k in range(nr) if drow[k] is not None]

    def body(*refs):
        i = pl.program_id(0)
        pos = 0
        rr = refs[pos:pos + nr]; pos += nr
        cr = refs[pos:pos + nc]; pos += nc
        pr = refs[pos:pos + npar]; pos += npar
        cpr = refs[pos:pos + ncp]; pos += ncp
        mr = refs[pos:pos + nm]; pos += nm
        dr = refs[pos:pos + nd]; pos += nd
        ar = refs[pos:pos + len(add_ix)]; pos += len(add_ix)
        drr = refs[pos:pos + len(out_ix)]; pos += len(out_ix)
        dpr = refs[pos:pos + npar]; pos += npar
        dmr = refs[pos:pos + nm]; pos += nm

        cv = _load_rows(cr, consts)
        cpv = _load_params(cpr, cparams)
        _, vjp = jax.vjp(lambda rv, pv, mv: f(rv, pv, mv, cv, cpv),
                         _load_rows(rr, rows), _load_params(pr, params), [m[...] for m in mr])
        dv = [d[...].astype(F32) for d in dr]
        if pre is not None:
            dv = pre(dv, cv)
        g_rows, g_params, g_mods = vjp(tuple(dv))

        for ref, k in zip(drr, out_ix):
            _, W, _, pw = rows[k]
            g = g_rows[k]
            extra = ar[add_ix.index(k)] if k in add_ix else None
            if pw == W:
                if extra is not None:
                    g = g + extra[...].astype(F32)
                ref[...] = g.astype(ref.dtype)
            else:
                for q in range(W // pw):
                    gq = g[q]
                    if extra is not None:
                        gq = gq + extra[:, q * pw:(q + 1) * pw].astype(F32)
                    ref[:, q * pw:(q + 1) * pw] = gq.astype(ref.dtype)

        @pl.when(i == 0)
        def _():
            for ref in dpr:
                ref[...] = jnp.zeros_like(ref)

        for ref, (arr, split), g in zip(dpr, params, g_params):
            if split:
                for k in range(arr.shape[0]):
                    ref[k] += g[k]
            else:
                ref[...] += g

        @pl.when((i == 0) | (i == nct))
        def _():
            for ref in dmr:
                ref[...] = jnp.zeros_like(ref)

        for ref, g in zip(dmr, g_mods):
            ref[...] += g

    in_specs = ([_row_spec(R, W, cb) for (_, W, cb, _) in rows + consts]
                + [_full_spec(a.shape) for (a, _) in params + cparams]
                + [_mod_spec(nct, m.shape[2]) for m in mods]
                + [_row_spec(R, W, 0) for (_, W) in douts]
                + [_row_spec(R, rows[k][1], 0) for k in add_ix])
    args = ([a for (a, _, _, _) in rows + consts] + [a for (a, _) in params + cparams] + list(mods)
            + [a for (a, _) in douts] + [adds[k] for k in add_ix])
    out_specs = ([_row_spec(R, rows[k][1], 0) for k in out_ix]
                 + [_full_spec(a.shape) for (a, _) in params]
                 + [_mod_spec(nct, m.shape[2]) for m in mods])
    out_shape = ([jax.ShapeDtypeStruct((n, rows[k][1]), drow[k]) for k in out_ix]
                 + [jax.ShapeDtypeStruct(a.shape, F32) for (a, _) in params]
                 + [jax.ShapeDtypeStruct(m.shape, F32) for m in mods])
    res = pl.pallas_call(
        body, name=name, grid=(n // R,), in_specs=in_specs, out_specs=out_specs, out_shape=out_shape,
        compiler_params=_ARB1,
    )(*args)
    no = len(out_ix)
    return list(res[:no]), list(res[no:no + npar]), list(res[no + npar:])


def _rms(x, g):
    return x * lax.rsqrt(jnp.mean(x * x, axis=-1, keepdims=True) + EPS) * g


def _f_lnmod(rv, pv, mv, cv, cpv):
    (x,), (g,), (shift, scale) = rv, pv, mv
    return (_rms(x, g) * (1.0 + scale) + shift,)


def _f_resid(rv, pv, mv, cv, cpv):
    (x, y), (gate,) = rv, mv
    return (x + gate * y,)


def _f_merge(rv, pv, mv, cv, cpv):
    ga, gb, gc, ya, yb, yc = rv
    return (jax.nn.sigmoid(ga) * ya + jax.nn.sigmoid(gb) * yb + jax.nn.sigmoid(gc) * yc,)


def _f_qknorm(rv, pv, mv, cv, cpv):
    (q, k), (gq, gk), (bdq, eq, bdk, ek) = rv, pv, cpv
    qn = q * lax.rsqrt(_hdot(q * q, bdq) + EPS) * _hdot(gq, eq)
    kn = k * lax.rsqrt(_hdot(k * k, bdk) + EPS) * _hdot(gk, ek)
    return (qn, kn)


def _rope(x, cos, sin):
    w = x.shape[1]
    lane = lax.broadcasted_iota(jnp.int32, x.shape, 1)
    partner = jnp.where((lane & 31) < 16, pltpu.roll(x, w - 16, 1), pltpu.roll(x, 16, 1))
    return x * cos + partner * sin


def _qk_post(res, cv):
    (qn, kn), (cq, sq, ck, sk) = res, cv
    return (_rope(qn, cq, sq) * (HEAD_DIM ** -0.5), _rope(kn, ck, sk))


def _qk_pre(dv, cv):
    (dq, dk), (cq, sq, ck, sk) = dv, cv
    return (_rope(dq * (HEAD_DIM ** -0.5), cq, -sq), _rope(dk, ck, -sk))


def _f_gmlp(rv, pv, mv, cv, cpv):
    (u, vs), (ng, ws, bt) = rv, pv
    pieces = []
    for g in range(GMLP_GROUPS):
        vn = _rms(jax.nn.gelu(vs[g]), ng[g])
        pieces.append(_hdot(ws[g], vn) + bt[g])
    return (jax.nn.gelu(u) * jnp.concatenate(pieces, axis=1),)


def _f_glaout(rv, pv, mv, cv, cpv):
    (os_, rs), (gn,) = rv, pv
    pieces = [_rms(os_[h], gn[h]) * jax.nn.silu(rs[h]) for h in range(GLA_HEADS)]
    return (jnp.concatenate(pieces, axis=1),)


_CONV_CB = 1408


def _seq_masks(first_row, rows, tc, n):
    g = first_row + lax.broadcasted_iota(jnp.int32, (rows, 1), 0)
    not_start = jnp.where((g == 0) | (g == tc), 0.0, 1.0)
    not_end = jnp.where((g == tc - 1) | (g == n - 1), 0.0, 1.0)
    return not_start, not_end


def _conv_specs(R, n):
    nb8 = n // 8
    main = pl.BlockSpec((2, R, _CONV_CB), lambda j, i: (0, i, j))
    prev = pl.BlockSpec((2, 8, _CONV_CB), lambda j, i: (0, jnp.maximum(i * (R // 8) - 1, 0), j))
    nxt = pl.BlockSpec((2, 8, _CONV_CB), lambda j, i: (0, jnp.minimum((i + 1) * (R // 8), nb8 - 1), j))
    cw = pl.BlockSpec((2, 3, _CONV_CB), lambda j, i: (0, 0, j))
    cb = pl.BlockSpec((2, 1, _CONV_CB), lambda j, i: (0, 0, j))
    return main, prev, nxt, cw, cb


def _conv_act(ext_ref, h, first, count, cw_ref, cb_ref, not_start, not_end):
    prev = ext_ref[h, pl.ds(first - 1, count), :]
    mid = ext_ref[h, pl.ds(first, count), :]
    nxt = ext_ref[h, pl.ds(first + 1, count), :]
    return (cb_ref[h] + cw_ref[h, 1:2, :] * mid + cw_ref[h, 0:1, :] * (prev * not_start)
            + cw_ref[h, 2:3, :] * (nxt * not_end))


def _conv_gate(name, a2, cw, cb, R, tc):
    n = a2.shape[1]
    main, prev, nxt, cws, cbs = _conv_specs(R, n)

    def body(a_ref, p_ref, n_ref, cw_ref, cb_ref, o_ref, ext):
        i = pl.program_id(1)
        ext[:, 0:8, :] = jnp.zeros((2, 8, _CONV_CB), F32)
        ext[:, 8:16, :] = p_ref[...]
        ext[:, 16:16 + R, :] = a_ref[...]
        ext[:, 16 + R:24 + R, :] = n_ref[...]
        ext[:, 24 + R:32 + R, :] = jnp.zeros((2, 8, _CONV_CB), F32)
        ns, ne = _seq_masks(i * R, R, tc, n)
        g = _conv_act(ext, 0, 16, R, cw_ref, cb_ref, ns, ne)
        v = _conv_act(ext, 1, 16, R, cw_ref, cb_ref, ns, ne)
        o_ref[...] = (jax.nn.silu(g) * v).astype(o_ref.dtype)

    return pl.pallas_call(
        body, name=name, grid=(FFN_H // _CONV_CB, n // R),
        in_specs=[main, prev, nxt, cws, cbs],
        out_specs=pl.BlockSpec((R, _CONV_CB), lambda j, i: (i, j)),
        out_shape=jax.ShapeDtypeStruct((n, FFN_H), BF16),
        scratch_shapes=[pltpu.VMEM((2, R + 32, _CONV_CB), F32)],
        compiler_params=pltpu.CompilerParams(dimension_semantics=("arbitrary", "arbitrary"),
                                             vmem_limit_bytes=V7X_VMEM_LIMIT),
    )(a2, a2, a2, cw, cb)


def _conv_gate_bwd(name, a2, dgv, cw, cb, R, tc):
    n = a2.shape[1]
    nb8 = n // 8
    main, prev, nxt, cws, cbs = _conv_specs(R, n)
    d_main = pl.BlockSpec((R, _CONV_CB), lambda j, i: (i, j))
    d_prev = pl.BlockSpec((8, _CONV_CB), lambda j, i: (jnp.maximum(i * (R // 8) - 1, 0), j))
    d_next = pl.BlockSpec((8, _CONV_CB), lambda j, i: (jnp.minimum((i + 1) * (R // 8), nb8 - 1), j))
    RE = R + 16

    def body(a_ref, p_ref, n_ref, cw_ref, cb_ref, d_ref, dp_ref, dn_ref, da_ref, dcw_ref, dcb_ref, ext, dext):
        i = pl.program_id(1)
        zeros8 = jnp.zeros((2, 8, _CONV_CB), F32)
        ext[:, 0:8, :] = zeros8
        ext[:, 8:16, :] = p_ref[...]
        ext[:, 16:16 + R, :] = a_ref[...]
        ext[:, 16 + R:24 + R, :] = n_ref[...]
        ext[:, 24 + R:32 + R, :] = zeros8
        ns_e, ne_e = _seq_masks(i * R - 8, RE, tc, n)
        g = _conv_act(ext, 0, 8, RE, cw_ref, cb_ref, ns_e, ne_e)
        v = _conv_act(ext, 1, 8, RE, cw_ref, cb_ref, ns_e, ne_e)
        dout = jnp.concatenate([dp_ref[...], d_ref[...], dn_ref[...]], axis=0).astype(F32)
        sg = jax.nn.sigmoid(g)
        dext[:, 0:8, :] = zeros8
        dext[:, 24 + R:32 + R, :] = zeros8
        dext[0, 8:8 + RE, :] = dout * v * sg * (1.0 + g * (1.0 - sg))
        dext[1, 8:8 + RE, :] = dout * g * sg
        ns, ne = _seq_masks(i * R, R, tc, n)

        @pl.when(i == 0)
        def _():
            dcw_ref[...] = jnp.zeros_like(dcw_ref)
            dcb_ref[...] = jnp.zeros_like(dcb_ref)

        for h in range(2):
            da = dext[h, 16:16 + R, :]
            da_up = dext[h, 17:17 + R, :] * ne
            da_dn = dext[h, 15:15 + R, :] * ns
            da_ref[h] = (cw_ref[h, 1:2, :] * da + cw_ref[h, 0:1, :] * da_up + cw_ref[h, 2:3, :] * da_dn).astype(da_ref.dtype)
            a_prev = ext[h, 15:15 + R, :] * ns
            a_mid = ext[h, 16:16 + R, :]
            a_next = ext[h, 17:17 + R, :] * ne
            dcw_ref[h, 0:1, :] += jnp.sum(da * a_prev, axis=0, keepdims=True)
            dcw_ref[h, 1:2, :] += jnp.sum(da * a_mid, axis=0, keepdims=True)
            dcw_ref[h, 2:3, :] += jnp.sum(da * a_next, axis=0, keepdims=True)
            dcb_ref[h] += jnp.sum(da, axis=0, keepdims=True)

    return pl.pallas_call(
        body, name=name, grid=(FFN_H // _CONV_CB, n // R),
        in_specs=[main, prev, nxt, cws, cbs, d_main, d_prev, d_next],
        out_specs=[main, cws, cbs],
        out_shape=[jax.ShapeDtypeStruct((2, n, FFN_H), BF16), jax.ShapeDtypeStruct((2, 3, FFN_H), F32),
                   jax.ShapeDtypeStruct((2, 1, FFN_H), F32)],
        scratch_shapes=[pltpu.VMEM((2, R + 32, _CONV_CB), F32), pltpu.VMEM((2, R + 32, _CONV_CB), F32)],
        compiler_params=pltpu.CompilerParams(dimension_semantics=("arbitrary", "arbitrary"),
                                             vmem_limit_bytes=V7X_VMEM_LIMIT),
    )(a2, a2, a2, cw, cb, dgv, dgv, dgv)


_ARB2 = pltpu.CompilerParams(dimension_semantics=("arbitrary", "arbitrary"), vmem_limit_bytes=V7X_VMEM_LIMIT)


def _attn_fwd(name, qt8, k2, vt2):
    nq, nk = qt8.shape[2], k2.shape[1]
    tq = _pick(nq, (256, 128))
    tk = _pick(nk, (768, 256, 128))

    def body(qt_ref, k_ref, vt_ref, ot_ref, lse_ref, m_ref, l_ref, acc_ref):
        m_ref[...] = jnp.full((Q_GROUP, 1, tq), -1e30, F32)
        l_ref[...] = jnp.zeros((Q_GROUP, 1, tq), F32)
        acc_ref[...] = jnp.zeros((Q_GROUP, HEAD_DIM, tq), F32)

        def step(j, carry):
            sl = pl.ds(pl.multiple_of(j * tk, tk), tk)
            kj = k_ref[sl, :]
            vtj = vt_ref[:, sl]
            sts = [lax.dot_general(kj, qt_ref[h], _NN, preferred_element_type=F32) for h in range(Q_GROUP)]
            m_old = [m_ref[h] for h in range(Q_GROUP)]
            l_old = [l_ref[h] for h in range(Q_GROUP)]
            acc_old = [acc_ref[h] for h in range(Q_GROUP)]
            m_new = [jnp.maximum(m_old[h], jnp.max(sts[h], axis=0, keepdims=True)) for h in range(Q_GROUP)]
            pts = [jnp.exp(sts[h] - m_new[h]) for h in range(Q_GROUP)]
            pvs = [lax.dot_general(vtj, pts[h].astype(BF16), _NN, preferred_element_type=F32) for h in range(Q_GROUP)]
            for h in range(Q_GROUP):
                alpha = jnp.exp(m_old[h] - m_new[h])
                l_ref[h] = alpha * l_old[h] + jnp.sum(pts[h], axis=0, keepdims=True)
                acc_ref[h] = alpha * acc_old[h] + pvs[h]
                m_ref[h] = m_new[h]
            return carry

        lax.fori_loop(0, nk // tk, step, 0)
        for h in range(Q_GROUP):
            l = l_ref[h]
            ot_ref[h] = (acc_ref[h] / l).astype(ot_ref.dtype)
            lse_ref[h] = m_ref[h] + jnp.log(l)

    qspec = pl.BlockSpec((Q_GROUP, HEAD_DIM, tq), lambda g, i: (g, 0, i))
    lspec = pl.BlockSpec((Q_GROUP, 1, tq), lambda g, i: (g, 0, i))
    return pl.pallas_call(
        body, name=name, grid=(N_KV_HEADS, nq // tq),
        in_specs=[qspec, pl.BlockSpec((None, nk, HEAD_DIM), lambda g, i: (g, 0, 0)),
                  pl.BlockSpec((None, HEAD_DIM, nk), lambda g, i: (g, 0, 0))],
        out_specs=[qspec, lspec],
        out_shape=[jax.ShapeDtypeStruct((N_Q_HEADS, HEAD_DIM, nq), BF16), jax.ShapeDtypeStruct((N_Q_HEADS, 1, nq), F32)],
        scratch_shapes=[pltpu.VMEM((Q_GROUP, 1, tq), F32), pltpu.VMEM((Q_GROUP, 1, tq), F32),
                        pltpu.VMEM((Q_GROUP, HEAD_DIM, tq), F32)],
        compiler_params=_ARB2,
    )(qt8, k2, vt2)


def _attn_bwd(name, qt8, q8, k2, v2, kt2, ot8, dot8, do8, lse):
    nq, nk = qt8.shape[2], k2.shape[1]
    tq = _pick(nq, (256, 128))
    tk = _pick(nk, (768, 256, 128))

    def body(qt_ref, q_ref, k_ref, v_ref, kt_ref, ot_ref, dot_ref, do_ref, lse_ref, dqt_ref, dk_ref, dv_ref, dl_ref, dq_acc):
        @pl.when(pl.program_id(1) == 0)
        def _():
            dk_ref[...] = jnp.zeros_like(dk_ref)
            dv_ref[...] = jnp.zeros_like(dv_ref)

        for h in range(Q_GROUP):
            dl_ref[h] = jnp.sum(dot_ref[h].astype(F32) * ot_ref[h].astype(F32), axis=0, keepdims=True)
        dq_acc[...] = jnp.zeros((Q_GROUP, HEAD_DIM, tq), F32)

        def step(j, carry):
            sl = pl.ds(pl.multiple_of(j * tk, tk), tk)
            kj = k_ref[sl, :]
            vj = v_ref[sl, :]
            ktj = kt_ref[:, sl]
            heads = range(Q_GROUP)
            sts = [lax.dot_general(kj, qt_ref[h], _NN, preferred_element_type=F32) for h in heads]
            dpts = [lax.dot_general(vj, dot_ref[h], _NN, preferred_element_type=F32) for h in heads]
            pts = [jnp.exp(sts[h] - lse_ref[h]) for h in heads]
            dsts = [(pts[h] * (dpts[h] - dl_ref[h])).astype(BF16) for h in heads]
            dvs = [lax.dot_general(pts[h].astype(BF16), do_ref[h], _NN, preferred_element_type=F32) for h in heads]
            dks = [lax.dot_general(dsts[h], q_ref[h], _NN, preferred_element_type=F32) for h in heads]
            dqs = [lax.dot_general(ktj, dsts[h], _NN, preferred_element_type=F32) for h in heads]
            for h in heads:
                dq_acc[h] += dqs[h]
            dk_ref[sl, :] += (dks[0] + dks[1]) + (dks[2] + dks[3])
            dv_ref[sl, :] += (dvs[0] + dvs[1]) + (dvs[2] + dvs[3])
            return carry

        lax.fori_loop(0, nk // tk, step, 0)
        dqt_ref[...] = dq_acc[...]

    tspec = pl.BlockSpec((Q_GROUP, HEAD_DIM, tq), lambda g, i: (g, 0, i))
    rspec = pl.BlockSpec((Q_GROUP, tq, HEAD_DIM), lambda g, i: (g, i, 0))
    lspec = pl.BlockSpec((Q_GROUP, 1, tq), lambda g, i: (g, 0, i))
    kspec = pl.BlockSpec((None, nk, HEAD_DIM), lambda g, i: (g, 0, 0))
    ktspec = pl.BlockSpec((None, HEAD_DIM, nk), lambda g, i: (g, 0, 0))
    return pl.pallas_call(
        body, name=name, grid=(N_KV_HEADS, nq // tq),
        in_specs=[tspec, rspec, kspec, kspec, ktspec, tspec, tspec, rspec, lspec],
        out_specs=[tspec, kspec, kspec],
        out_shape=[jax.ShapeDtypeStruct((N_Q_HEADS, HEAD_DIM, nq), F32), jax.ShapeDtypeStruct((N_KV_HEADS, nk, HEAD_DIM), F32),
                   jax.ShapeDtypeStruct((N_KV_HEADS, nk, HEAD_DIM), F32)],
        scratch_shapes=[pltpu.VMEM((Q_GROUP, 1, tq), F32), pltpu.VMEM((Q_GROUP, HEAD_DIM, tq), F32)],
        compiler_params=_ARB2,
    )(qt8, q8, k2, v2, kt2, ot8, dot8, do8, lse)


def _log_sigmoid(z):
    return jnp.minimum(z, 0.0) - jnp.log(1.0 + jnp.exp(-jnp.abs(z)))


def _gla_chunk(q, k, vs, a, w2, b2, state, *, rev):
    c = GLA_CHUNK
    r_i = lax.broadcasted_iota(jnp.int32, (c, c), 0)
    c_i = lax.broadcasted_iota(jnp.int32, (c, c), 1)
    tri = (c_i >= r_i) if rev else (c_i <= r_i)
    la = _log_sigmoid(_hdot(a, w2) + b2) * (1.0 / GLA_TAU)
    cum = _hdot(tri.astype(F32), la)
    tot = jnp.sum(la, axis=0, keepdims=True)
    tot_col = _hdot(la, jnp.ones((c, 1), F32), _TN)
    q_in = q * (GLA_DK ** -0.5) * jnp.exp(cum)
    k_in = k * jnp.exp(-cum)
    k_st = k * jnp.exp(tot - cum)
    lane = lax.broadcasted_iota(jnp.int32, (1, GLA_QK_W), 1)
    outs = []
    for h in range(GLA_HEADS):
        head = ((lane >= GLA_DK * h) & (lane < GLA_DK * (h + 1))).astype(F32)
        att = jnp.where(tri, _hdot(q_in * head, k_in, _NT), 0.0)
        outs.append(_hdot(att, vs[h]))
    o = jnp.concatenate(outs, axis=1) + _hdot(q_in, state)
    hr = lax.broadcasted_iota(jnp.int32, (GLA_QK_W, GLA_V_W), 0) // GLA_DK
    hc = lax.broadcasted_iota(jnp.int32, (GLA_QK_W, GLA_V_W), 1) // GLA_DV
    u = _hdot(k_st, jnp.concatenate(vs, axis=1), _TN) * (hr == hc).astype(F32)
    return o, jnp.exp(tot_col) * state + u


def _gla_tile_of(step, rev, nct, nt):
    if not rev:
        return step
    return jnp.where(step < nct, nct - 1 - step, nt - 1 - (step - nct))


def _gla_row_specs(R, tile):
    return [pl.BlockSpec((R, GLA_QK_W), lambda s: (tile(s), C_GQ // GLA_QK_W)),
            pl.BlockSpec((R, GLA_QK_W), lambda s: (tile(s), C_GK // GLA_QK_W)),
            pl.BlockSpec((R, GLA_V_W), lambda s: (tile(s), C_GV // GLA_V_W)),
            pl.BlockSpec((R, 128), lambda s: (tile(s), C_AF // 128))]


def _gla_dir(name, p, w2, b2, *, rev, R, nct, add=None):
    n = p.shape[0]
    nt = n // R
    nch = R // GLA_CHUNK
    tile = lambda s: _gla_tile_of(s, rev, nct, nt)
    order = range(nch - 1, -1, -1) if rev else range(nch)

    def body(q_ref, k_ref, v_ref, a_ref, w2_ref, b2_ref, *rest):
        if add is not None:
            add_ref, o_ref, ssave_ref, state = rest
        else:
            o_ref, ssave_ref, state = rest

        @pl.when(pl.program_id(0) == 0)
        def _():
            state[...] = jnp.zeros_like(state)

        for cix in order:
            rows = pl.ds(cix * GLA_CHUNK, GLA_CHUNK)
            vs = [v_ref[rows, GLA_DV * h:GLA_DV * (h + 1)] for h in range(GLA_HEADS)]
            s_in = state[...]
            ssave_ref[cix] = s_in
            o, s_out = _gla_chunk(q_ref[rows, :], k_ref[rows, :], vs, a_ref[rows, :], w2_ref[...], b2_ref[...], s_in, rev=rev)
            if add is not None:
                o = o + add_ref[rows, :]
            o_ref[rows, :] = o
            state[...] = s_out

    o_spec = pl.BlockSpec((R, GLA_V_W), lambda s: (tile(s), 0))
    in_specs = _gla_row_specs(R, tile) + [_full_spec(w2.shape), _full_spec(b2.shape)]
    args = [p, p, p, p, w2, b2]
    if add is not None:
        in_specs.append(o_spec)
        args.append(add)
    return pl.pallas_call(
        body, name=name, grid=(nt,), in_specs=in_specs,
        out_specs=[o_spec, pl.BlockSpec((nch, GLA_QK_W, GLA_V_W), lambda s: (tile(s), 0, 0))],
        out_shape=[jax.ShapeDtypeStruct((n, GLA_V_W), F32),
                   jax.ShapeDtypeStruct((n // GLA_CHUNK, GLA_QK_W, GLA_V_W), F32)],
        scratch_shapes=[pltpu.VMEM((GLA_QK_W, GLA_V_W), F32)],
        compiler_params=_ARB1,
    )(*args)


def _gla_dir_bwd(name, p, w2, b2, ssave, do, *, rev, R, nct, adds=None, out_dtype=F32):
    n = p.shape[0]
    nt = n // R
    nch = R // GLA_CHUNK
    tile = lambda s: _gla_tile_of(nt - 1 - s, rev, nct, nt)
    order = range(nch) if rev else range(nch - 1, -1, -1)
    widths = (GLA_QK_W, GLA_QK_W, GLA_V_W, 128)

    def body(q_ref, k_ref, v_ref, a_ref, w2_ref, b2_ref, ss_ref, do_ref, *rest):
        if adds is not None:
            add_refs, rest = rest[:4], rest[4:]
        dq_ref, dk_ref, dv_ref, da_ref, dw2_ref, db2_ref, dstate = rest

        @pl.when(pl.program_id(0) == 0)
        def _():
            dstate[...] = jnp.zeros_like(dstate)
            dw2_ref[...] = jnp.zeros_like(dw2_ref)
            db2_ref[...] = jnp.zeros_like(db2_ref)

        for cix in order:
            rows = pl.ds(cix * GLA_CHUNK, GLA_CHUNK)
            vs = [v_ref[rows, GLA_DV * h:GLA_DV * (h + 1)] for h in range(GLA_HEADS)]
            _, vjp = jax.vjp(functools.partial(_gla_chunk, rev=rev), q_ref[rows, :], k_ref[rows, :], vs, a_ref[rows, :],
                             w2_ref[...], b2_ref[...], ss_ref[cix])
            dq, dk, dvs, da, dw2, db2, ds = vjp((do_ref[rows, :], dstate[...]))
            dv = jnp.concatenate(dvs, axis=1)
            grads = [dq, dk, dv, da]
            if adds is not None:
                grads = [g + r[rows, :].astype(F32) for g, r in zip(grads, add_refs)]
            for ref, g in zip((dq_ref, dk_ref, dv_ref, da_ref), grads):
                ref[rows, :] = g.astype(ref.dtype)
            dw2_ref[...] += dw2
            db2_ref[...] += db2
            dstate[...] = ds

    d_specs = [pl.BlockSpec((R, w), lambda s: (tile(s), 0)) for w in widths]
    in_specs = (_gla_row_specs(R, tile) + [_full_spec(w2.shape), _full_spec(b2.shape),
                pl.BlockSpec((nch, GLA_QK_W, GLA_V_W), lambda s: (tile(s), 0, 0)),
                pl.BlockSpec((R, GLA_V_W), lambda s: (tile(s), 0))])
    args = [p, p, p, p, w2, b2, ssave, do]
    if adds is not None:
        in_specs += d_specs
        args += list(adds)
    return pl.pallas_call(
        body, name=name, grid=(nt,), in_specs=in_specs,
        out_specs=d_specs + [_full_spec(w2.shape), _full_spec(b2.shape)],
        out_shape=[jax.ShapeDtypeStruct((n, w), out_dtype) for w in widths]
        + [jax.ShapeDtypeStruct(w2.shape, F32), jax.ShapeDtypeStruct(b2.shape, F32)],
        scratch_shapes=[pltpu.VMEM((GLA_QK_W, GLA_V_W), F32)],
        compiler_params=_ARB1,
    )(*args)


def _final_loss(name, x, target, gf, R, nct):
    n = x.shape[0]

    def body(x_ref, t_ref, g_ref, loss_ref, dx_ref, dg_ref):
        i = pl.program_id(0)

        @pl.when(i == 0)
        def _():
            loss_ref[...] = jnp.zeros_like(loss_ref)
            dg_ref[...] = jnp.zeros_like(dg_ref)

        @pl.when(i < nct)
        def _():
            dx_ref[...] = jnp.zeros_like(dx_ref)

        @pl.when(i >= nct)
        def _():
            y, vjp = jax.vjp(_rms, x_ref[...], g_ref[...])
            err = y - t_ref[...]
            loss_ref[...] += jnp.sum(0.5 * jnp.mean(err * err, axis=-1, keepdims=True))
            dx, dg = vjp(err * (1.0 / D))
            dx_ref[...] = dx
            dg_ref[...] += dg

    return pl.pallas_call(
        body, name=name, grid=(n // R,),
        in_specs=[_row_spec(R, D, 0), pl.BlockSpec((R, D), lambda i: (jnp.maximum(i - nct, 0), 0)), _full_spec((1, D))],
        out_specs=[_full_spec((8, 128)), _row_spec(R, D, 0), _full_spec((1, D))],
        out_shape=[jax.ShapeDtypeStruct((8, 128), F32), jax.ShapeDtypeStruct((n, D), F32), jax.ShapeDtypeStruct((1, D), F32)],
        compiler_params=_ARB1,
    )(x, target, gf)


def _adamw(name, w, m, v, gparts):
    rows, cols = w.shape
    nparts = gparts.shape[0]
    tr = rows
    for cand in range(min(rows, 256), 15, -16):
        if rows % cand == 0:
            tr = cand
            break

    def body(w_ref, m_ref, v_ref, g_ref, go_ref, d_ref, mo_ref, vo_ref):
        g = g_ref[0].astype(F32)
        for k in range(1, nparts):
            g = g + g_ref[k].astype(F32)
        m_new = ADAM_B1 * m_ref[...] + (1.0 - ADAM_B1) * g
        v_new = ADAM_B2 * v_ref[...] + (1.0 - ADAM_B2) * (g * g)
        m_hat = m_new / (1.0 - ADAM_B1 ** ADAM_STEP)
        v_hat = v_new / (1.0 - ADAM_B2 ** ADAM_STEP)
        go_ref[...] = g
        d_ref[...] = -ADAM_LR * (m_hat / (jnp.sqrt(v_hat) + ADAM_EPS) + ADAM_WD * w_ref[...])
        mo_ref[...] = m_new
        vo_ref[...] = v_new

    spec = pl.BlockSpec((tr, cols), lambda i: (i, 0))
    return pl.pallas_call(
        body, name=name, grid=(rows // tr,),
        in_specs=[spec, spec, spec, pl.BlockSpec((nparts, tr, cols), lambda i: (0, i, 0))],
        out_specs=[spec] * 4, out_shape=[jax.ShapeDtypeStruct((rows, cols), F32)] * 4,
        compiler_params=_ARB1,
    )(w, m, v, gparts)


def _my_index():
    return 4 * lax.axis_index("x") + 2 * lax.axis_index("y") + lax.axis_index("c")


def _xor_peer(k):
    flip = lambda a, bit: (1 - a) if bit else a
    pos = (flip(lax.axis_index("x"), (k >> 2) & 1), flip(lax.axis_index("y"), (k >> 1) & 1), flip(lax.axis_index("c"), k & 1))
    return pos, 4 * pos[0] + 2 * pos[1] + pos[2]


def _exchange(name, xs, *, gather):
    na = len(xs)
    npeer = N_DEV - 1

    def body(*refs):
        x_refs, o_refs = refs[:na], refs[na:2 * na]
        send_sems, recv_sems, local_sems = refs[2 * na:]
        me = _my_index()
        locals_, sends, recvs = [], [], []
        for a, (x_ref, o_ref) in enumerate(zip(x_refs, o_refs)):
            mine = x_ref if gather else x_ref.at[me]
            locals_.append(pltpu.make_async_copy(mine, o_ref.at[me], local_sems.at[a]))
            for k in range(1, N_DEV):
                pos, lin = _xor_peer(k)
                src = x_ref if gather else x_ref.at[lin]
                sem = a * npeer + k - 1
                sends.append(pltpu.make_async_remote_copy(src_ref=src, dst_ref=o_ref.at[me], send_sem=send_sems.at[sem],
                                                          recv_sem=recv_sems.at[sem], device_id=pos, device_id_type=MESH_ID))
                recvs.append(pltpu.make_async_remote_copy(src_ref=src, dst_ref=o_ref.at[lin], send_sem=send_sems.at[sem],
                                                          recv_sem=recv_sems.at[sem], device_id=pos, device_id_type=MESH_ID))
        for cp in locals_ + sends:
            cp.start()
        for cp in recvs:
            cp.wait_recv()
        for cp in sends:
            cp.wait_send()
        for cp in locals_:
            cp.wait()

    hbm = pl.BlockSpec(memory_space=pltpu.HBM)
    return pl.pallas_call(
        body, name=name, in_specs=[hbm] * na, out_specs=[hbm] * na,
        out_shape=[jax.ShapeDtypeStruct((N_DEV,) + tuple(x.shape if gather else x.shape[1:]), x.dtype) for x in xs],
        scratch_shapes=[pltpu.SemaphoreType.DMA((na * npeer,)), pltpu.SemaphoreType.DMA((na * npeer,)),
                        pltpu.SemaphoreType.DMA((na,))],
    )(*xs)


def _adaln_fwd(name, craw16, w_ada, b_cols):
    def body(c_ref, w_ref, b_ref, o_ref):
        cs = jax.nn.silu(c_ref[...]).astype(BF16)
        for l in range(2):
            o_ref[l] = lax.dot_general(cs, w_ref[l].astype(BF16), _NN, preferred_element_type=F32) + b_ref[l]

    return pl.pallas_call(
        body, name=name, out_shape=jax.ShapeDtypeStruct((2, 16, w_ada.shape[2]), F32),
        compiler_params=pltpu.CompilerParams(vmem_limit_bytes=V7X_VMEM_LIMIT),
    )(craw16, w_ada, b_cols)


def _adaln_bwd(name, craw16, w_ada, dm):
    def body(c_ref, w_ref, dm_ref, gw_ref, dc_ref):
        c = c_ref[...]
        sg = jax.nn.sigmoid(c)
        cs = c * sg
        row = lax.broadcasted_iota(jnp.int32, (8, 1), 0)
        dc = jnp.zeros((16, D), F32)
        for l in range(2):
            dmx = dm_ref[2 * l + 1]
            dmc = jnp.where(row == 0, jnp.sum(dm_ref[2 * l], axis=0, keepdims=True), 0.0)
            gw_ref[l] = _hdot(cs[0:8], dmx, _TN) + _hdot(cs[8:16], dmc, _TN)
            dc = dc + _hdot(jnp.concatenate([dmx, dmc], axis=0), w_ref[l], _NT)
        dc_ref[...] = dc * sg * (1.0 + c * (1.0 - sg))

    return pl.pallas_call(
        body, name=name,
        out_shape=[jax.ShapeDtypeStruct(w_ada.shape, F32), jax.ShapeDtypeStruct((16, D), F32)],
        compiler_params=pltpu.CompilerParams(vmem_limit_bytes=V7X_VMEM_LIMIT),
    )(craw16, w_ada, dm)


_IN_OFFS = [sum(IN_SPLITS[:k]) for k in range(len(IN_SPLITS) + 1)]
_MY_ORDER = (11, 12, 13, 0, 1, 2, 7, 10, 5, 6, 3, 4, 8, 9)


_IN_SHARD = IN_WIDTH // N_DEV


def _win_my_cols(pieces):
    parts = []
    for k in _MY_ORDER:
        a, b = _IN_OFFS[k], _IN_OFFS[k + 1]
        for s in range(a // _IN_SHARD, (b - 1) // _IN_SHARD + 1):
            lo, hi = max(a, s * _IN_SHARD), min(b, (s + 1) * _IN_SHARD)
            parts.append(pieces[s][:, lo - s * _IN_SHARD:hi - s * _IN_SHARD])
    parts.append(jnp.zeros((pieces[0].shape[0], PW - IN_WIDTH), pieces[0].dtype))
    return jnp.concatenate(parts, axis=1)


def _win_shards(wp):
    my_offs, pos = {}, 0
    for k in _MY_ORDER:
        my_offs[k] = pos
        pos += IN_SPLITS[k]
    shards = []
    for s in range(N_DEV):
        parts = []
        for k in range(len(IN_SPLITS)):
            lo, hi = max(_IN_OFFS[k], s * _IN_SHARD), min(_IN_OFFS[k + 1], (s + 1) * _IN_SHARD)
            if lo < hi:
                parts.append(wp[:, my_offs[k] + lo - _IN_OFFS[k]:my_offs[k] + hi - _IN_OFFS[k]])
        shards.append(jnp.concatenate(parts, axis=1))
    return jnp.stack(shards)


def _split_shards(full, axis):
    c = full.shape[axis] // N_DEV
    return jnp.stack([lax.slice_in_dim(full, s * c, (s + 1) * c, axis=axis) for s in range(N_DEV)])


def _pack_rows(pieces, row_mult):
    rows = jnp.concatenate([p.reshape(-1, 128) for p in pieces], axis=0)
    padn = (-rows.shape[0]) % row_mult
    if padn:
        rows = jnp.concatenate([rows, jnp.zeros((padn, 128), rows.dtype)], axis=0)
    return rows


def _unpack_rows(rows, shapes):
    out, pos = [], 0
    for s in shapes:
        size = 1
        for d in s:
            size *= d
        out.append(rows[pos:pos + size // 128].reshape(tuple(s)))
        pos += size // 128
    return out


def _heads_front(a, nh):
    return a.reshape(a.shape[0], nh, HEAD_DIM).transpose(1, 0, 2)


def _heads_back(a):
    return a.transpose(1, 0, 2).reshape(a.shape[1], a.shape[0] * HEAD_DIM)


def _rope_tables(t, tc):
    tok = jnp.arange(t, dtype=jnp.int32)
    inv_freq = ROPE_THETA ** (-jnp.arange(ROPE_FREQS, dtype=F32) / ROPE_FREQS)
    ang_r = (tok // GRID_W).astype(F32)[:, None] * inv_freq
    ang_c = (tok % GRID_W).astype(F32)[:, None] * inv_freq
    cos64 = jnp.concatenate([jnp.cos(ang_r), jnp.cos(ang_r), jnp.cos(ang_c), jnp.cos(ang_c)], axis=1)
    sin64 = jnp.concatenate([-jnp.sin(ang_r), jnp.sin(ang_r), -jnp.sin(ang_c), jnp.sin(ang_c)], axis=1)
    cos64 = jnp.concatenate([jnp.ones((tc, HEAD_DIM), F32), cos64], axis=0)
    sin64 = jnp.concatenate([jnp.zeros((tc, HEAD_DIM), F32), sin64], axis=0)
    return jnp.tile(cos64, (1, N_Q_HEADS)), jnp.tile(sin64, (1, N_Q_HEADS))


def _head_mean_matrix(width):
    i = jnp.arange(width) // HEAD_DIM
    return (i[:, None] == i[None, :]).astype(F32) / HEAD_DIM


def _head_tile_matrix(width):
    return (jnp.arange(HEAD_DIM)[:, None] == (jnp.arange(width) % HEAD_DIM)[None, :]).astype(F32)


def _heads_t(a, nh):
    return a.T.reshape(nh, HEAD_DIM, a.shape[0])


def _heads_t_back(a):
    return a.reshape(a.shape[0] * HEAD_DIM, a.shape[2]).T


def _attention_fwd(tag, qr, kr, vv, tc):
    qt8, k2, vt2 = _heads_t(qr, N_Q_HEADS), _heads_front(kr, N_KV_HEADS), _heads_t(vv, N_KV_HEADS)
    o_c, lse_c = _attn_fwd(tag + "_attn_ctx", qt8[:, :, :tc], k2[:, :tc], vt2[:, :, :tc])
    o_x, lse_x = _attn_fwd(tag + "_attn_lat", qt8[:, :, tc:], k2, vt2)
    ot8 = jnp.concatenate([o_c, o_x], axis=2)
    lse = jnp.concatenate([lse_c, lse_x], axis=2)
    return _heads_t_back(ot8), (qr, kr, vv, ot8, lse)


def _attention_bwd(tag, saved, datt, tc):
    qr, kr, vv, ot8, lse = saved
    datt = datt.astype(BF16)
    qt8, q8, dot8, do8 = _heads_t(qr, N_Q_HEADS), _heads_front(qr, N_Q_HEADS), _heads_t(datt, N_Q_HEADS), _heads_front(datt, N_Q_HEADS)
    k2, v2, kt2 = _heads_front(kr, N_KV_HEADS), _heads_front(vv, N_KV_HEADS), _heads_t(kr, N_KV_HEADS)
    parts = []
    for nm, qs, ks in (("ctx", slice(0, tc), slice(0, tc)), ("lat", slice(tc, None), slice(None))):
        parts.append(_attn_bwd(tag + "_attn_b_" + nm, qt8[:, :, qs], q8[:, qs], k2[:, ks], v2[:, ks], kt2[:, :, ks],
                               ot8[:, :, qs], dot8[:, :, qs], do8[:, qs], lse[:, :, qs]))
    (dq_c, dk_c, dv_c), (dq_x, dk_x, dv_x) = parts
    dqt8 = jnp.concatenate([dq_c, dq_x], axis=2)
    dk2 = dk_x.at[:, :tc].add(dk_c)
    dv2 = dv_x.at[:, :tc].add(dv_c)
    return _heads_t_back(dqt8), _heads_back(dk2), _heads_back(dv2)


def _layer_fwd(tag, x, w, modv, consts, R, nct, tc):
    sh1, sc1, g1, sh2, sc2, g2 = modv
    cosq, sinq, bdq, eq, bdk, ek = consts
    n = x.shape[0]
    (h1,) = _rowwise(tag + "_ln1", _f_lnmod, R, nct, [(x, D, 0, D)], [], [(w["norm1_g"], False)], [], [sh1, sc1], [(D, BF16)])
    p = _matmul(tag + "_in", h1, w["w_in"], "nn", F32)
    qk_rows = [(p, Q_W, C_Q // Q_W, Q_W), (p, KV_W, C_K // KV_W, KV_W)]
    qk_consts = [(cosq, Q_W, 0, Q_W), (sinq, Q_W, 0, Q_W), (cosq, KV_W, 0, KV_W), (sinq, KV_W, 0, KV_W)]
    qk_params = [(w["q_norm_g"], False), (w["k_norm_g"], False)]
    qk_cparams = [(bdq, False), (eq, False), (bdk, False), (ek, False)]
    qr, kr = _rowwise(tag + "_qk", _f_qknorm, R, nct, qk_rows, qk_consts, qk_params, qk_cparams, [],
                      [(Q_W, BF16), (KV_W, BF16)], post=_qk_post)
    vv = p[:, C_VV:C_VV + KV_W].astype(BF16)
    att, att_saved = _attention_fwd(tag, qr, kr, vv, tc)

    o_f, s_f = _gla_dir(tag + "_gla_f", p, w["w2p_f"], w["b2_f"], rev=False, R=R, nct=nct)
    o_fb, s_b = _gla_dir(tag + "_gla_b", p, w["w2p_b"], w["b2_b"], rev=True, R=R, nct=nct, add=o_f)
    go_rows = [(o_fb, GLA_V_W, 0, GLA_DV), (p, GLA_V_W, C_R // GLA_V_W, GLA_DV)]
    (gla,) = _rowwise(tag + "_glaout", _f_glaout, R, nct, go_rows, [], [(w["gla_norm_g"], True)], [], [], [(GLA_V_W, BF16)])

    rg = GMLP_CHUNK
    gm_rows = [(p, GMLP_W, C_U // GMLP_W, GMLP_W), (p, GMLP_W, C_V // GMLP_W, GMLP_W // GMLP_GROUPS)]
    gm_params = [(w["gmlp_norm_g"], True), (w["w_spatial"], True), (w["b_spatial_t"], True)]
    (gm,) = _rowwise(tag + "_gmlp", _f_gmlp, rg, tc // rg, gm_rows, [], gm_params, [], [], [(GMLP_W, BF16)])

    ya = _matmul(tag + "_br_a", gm, w["w_br_a"], "nn", F32)
    yb = _matmul(tag + "_br_b", att, w["w_br_b"], "nn", F32)
    yc = _matmul(tag + "_br_c", gla, w["w_br_c"], "nn", F32)
    mg_rows = [(p, D, C_GA // D, D), (p, D, C_GB // D, D), (p, D, C_GC // D, D), (ya, D, 0, D), (yb, D, 0, D), (yc, D, 0, D)]
    (merged,) = _rowwise(tag + "_merge", _f_merge, R, nct, mg_rows, [], [], [], [], [(D, BF16)])
    mix = _matmul(tag + "_out", merged, w["w_out"], "nn", F32)
    (x_mid,) = _rowwise(tag + "_res1", _f_resid, R, nct, [(x, D, 0, D), (mix, D, 0, D)], [], [], [], [g1], [(D, F32)])

    (h2,) = _rowwise(tag + "_ln2", _f_lnmod, R, nct, [(x_mid, D, 0, D)], [], [(w["norm2_g"], False)], [], [sh2, sc2], [(D, BF16)])
    a2 = _matmul(tag + "_up", h2, w["w_ffn_up"], "nn", F32, o_halves=True)
    gv = _conv_gate(tag + "_conv", a2, w["conv_w_h"], w["conv_b_h"], R, tc)
    ffn = _matmul(tag + "_down", gv, w["w_ffn_down"], "nn", F32)
    (x_next,) = _rowwise(tag + "_res2", _f_resid, R, nct, [(x_mid, D, 0, D), (ffn, D, 0, D)], [], [], [], [g2], [(D, F32)])
    saved = dict(x=x, h1=h1, p=p, att_saved=att_saved, att=att, o_fb=o_fb, s_f=s_f, s_b=s_b, gla=gla, gm=gm,
                 ya=ya, yb=yb, yc=yc, merged=merged, mix=mix, x_mid=x_mid, h2=h2, a2=a2, gv=gv, ffn=ffn,
                 qk=(qk_rows, qk_consts, qk_params, qk_cparams), go_rows=go_rows, gm_info=(gm_rows, gm_params),
                 mg_rows=mg_rows)
    return x_next, saved


def _layer_bwd(tag, dx_next, s, w, modv, R, nct, tc):
    sh1, sc1, g1, sh2, sc2, g2 = modv
    gw = {}
    (dffn,), _, (dg2,) = _rowwise_bwd(tag + "_res2_b", _f_resid, R, nct, [(s["ffn"], D, 0, D), (s["ffn"], D, 0, D)], [], [], [], [g2],
                                      [(dx_next, D)], [None, BF16])
    dgv = _matmul(tag + "_down_da", dffn, w["w_ffn_down"], "nt", F32)
    gw["w_ffn_down"] = _matmul(tag + "_down_dw", s["gv"], dffn, "tn", F32)
    da2, dcw, dcb = _conv_gate_bwd(tag + "_conv_b", s["a2"], dgv, w["conv_w_h"], w["conv_b_h"], R, tc)
    gw["conv_w_h"], gw["conv_b_h"] = dcw, dcb
    dh2 = _matmul(tag + "_up_da", da2, w["w_ffn_up"], "nt", F32, a_halves=True)
    gw["w_ffn_up"] = _matmul(tag + "_up_dw", s["h2"], da2, "tn", F32, b_halves=True)
    (dx_mid,), (gw["norm2_g"],), (dsh2, dsc2) = _rowwise_bwd(
        tag + "_ln2_b", _f_lnmod, R, nct, [(s["x_mid"], D, 0, D)], [], [(w["norm2_g"], False)], [], [sh2, sc2],
        [(dh2, D)], [F32], adds=[dx_next])
    (dmix,), _, (dg1,) = _rowwise_bwd(tag + "_res1_b", _f_resid, R, nct, [(s["mix"], D, 0, D), (s["mix"], D, 0, D)], [], [], [], [g1],
                                      [(dx_mid, D)], [None, BF16])
    dmerged = _matmul(tag + "_out_da", dmix, w["w_out"], "nt", F32)
    gw["w_out"] = _matmul(tag + "_out_dw", s["merged"], dmix, "tn", F32)
    (dga, dgb, dgc, dya, dyb, dyc), _, _ = _rowwise_bwd(tag + "_merge_b", _f_merge, R, nct, s["mg_rows"], [], [], [], [],
                                                        [(dmerged, D)], [BF16] * 6)
    dgm = _matmul(tag + "_br_a_da", dya, w["w_br_a"], "nt", F32)
    datt = _matmul(tag + "_br_b_da", dyb, w["w_br_b"], "nt", F32)
    dgla = _matmul(tag + "_br_c_da", dyc, w["w_br_c"], "nt", F32)
    gm_rows, gm_params = s["gm_info"]
    gw["w_br_a"] = _matmul(tag + "_br_a_dw", s["gm"], dya, "tn", F32)
    gw["w_br_b"] = _matmul(tag + "_br_b_dw", s["att"], dyb, "tn", F32)
    gw["w_br_c"] = _matmul(tag + "_br_c_dw", s["gla"], dyc, "tn", F32)
    rg = GMLP_CHUNK
    (du, dv_), (gw["gmlp_norm_g"], gw["w_spatial"], gw["b_spatial_t"]), _ = _rowwise_bwd(
        tag + "_gmlp_b", _f_gmlp, rg, tc // rg, gm_rows, [], gm_params, [], [], [(dgm, GMLP_W)], [BF16, BF16])
    (do, dr), (gw["gla_norm_g"],), _ = _rowwise_bwd(tag + "_glaout_b", _f_glaout, R, nct, s["go_rows"], [],
                                                    [(w["gla_norm_g"], True)], [], [], [(dgla, GLA_V_W)], [F32, BF16])
    p = s["p"]
    *d_b, gw["w2p_b"], gw["b2_b"] = _gla_dir_bwd(tag + "_gla_b_b", p, w["w2p_b"], w["b2_b"], s["s_b"], do, rev=True, R=R, nct=nct)
    dgq, dgk, dgv_, daf, gw["w2p_f"], gw["b2_f"] = _gla_dir_bwd(tag + "_gla_f_b", p, w["w2p_f"], w["b2_f"], s["s_f"], do,
                                                              rev=False, R=R, nct=nct, adds=d_b, out_dtype=BF16)
    dqr, dkr, dvv = _attention_bwd(tag, s["att_saved"], datt, tc)
    qk_rows, qk_consts, qk_params, qk_cparams = s["qk"]
    (dq, dk), (gw["q_norm_g"], gw["k_norm_g"]), _ = _rowwise_bwd(
        tag + "_qk_b", _f_qknorm, R, nct, qk_rows, qk_consts, qk_params, qk_cparams, [],
        [(dqr, Q_W), (dkr, KV_W)], [BF16, BF16], pre=_qk_pre)
    dp = jnp.concatenate([dga, dgb, dgc, du, dv_, dq, dgv_, dr, dgq, dgk, dk, dvv.astype(BF16), daf,
                          jnp.zeros((p.shape[0], PW - C_AF - 128), BF16)], axis=1)
    dh1 = _matmul(tag + "_in_da", dp, w["w_in"], "nt", F32)
    gw["w_in"] = _matmul(tag + "_in_dw", s["h1"], dp, "tn", F32)
    (dx,), (gw["norm1_g"],), (dsh1, dsc1) = _rowwise_bwd(
        tag + "_ln1_b", _f_lnmod, R, nct, [(s["x"], D, 0, D)], [], [(w["norm1_g"], False)], [], [sh1, sc1],
        [(dh1, D)], [F32], adds=[dx_mid])
    return dx, gw, (dsh1, dsc1, dg1, dsh2, dsc2, dg2)


_SHARDED = (("w_in", 1, True), ("w_br_a", 1, True), ("w_br_b", 1, True), ("w_br_c", 1, True), ("w_out", 0, True),
            ("w_ffn_up", 1, True), ("w_ffn_down", 0, True), ("conv_w", 1, False), ("w_alpha2", 2, False), ("b_alpha", 1, False))
_REPLICATED = ("c_ctx", "b_ada", "norm1_g", "norm2_g", "q_norm_g", "k_norm_g", "gmlp_norm_g", "w_spatial", "b_spatial",
               "gla_norm_g", "conv_b", "final_norm_g")
_WEIGHTS = ("c_ctx", "w_ada", "b_ada", "norm1_g", "norm2_g", "w_in", "q_norm_g", "k_norm_g", "gmlp_norm_g", "w_spatial",
            "b_spatial", "w_alpha2", "b_alpha", "gla_norm_g", "w_br_a", "w_br_b", "w_br_c", "w_out", "w_ffn_up", "conv_w",
            "conv_b", "w_ffn_down", "final_norm_g")


def _decay_weights(w_alpha2_l, b_alpha_l):
    out = []
    for d in range(2):
        w2p = jnp.zeros((128, GLA_QK_W), F32).at[GLA_RANK * d:GLA_RANK * (d + 1)].set(w_alpha2_l[d])
        out += [w2p, b_alpha_l[d][None, :]]
    return out


def _step(inp, wts, moms, vels):
    x, c, ctx, loss_target = inp
    t, tc = x.shape[1], ctx.shape[1]
    n = t + tc
    R = min(256, tc)
    nct = tc // R
    me = _my_index()
    depth = wts["w_in"].shape[0]

    c8 = jnp.concatenate([c, jnp.zeros((7, D), F32)], axis=0)
    gathered = _exchange("gather_weights", [wts[nm].astype(BF16) if half else wts[nm] for nm, _, half in _SHARDED] + [c8],
                         gather=True)
    c_all = gathered[-1][:, 0, :]
    gathered = dict(zip([nm for nm, _, _ in _SHARDED], gathered))

    layers = []
    for l in range(depth):
        w = {}
        for nm, ax, _ in _SHARDED:
            pieces = [gathered[nm][s, l] for s in range(N_DEV)]
            w[nm] = _win_my_cols(pieces) if nm == "w_in" else jnp.concatenate(pieces, axis=ax)
        conv_w, w_alpha2, b_alpha = w.pop("conv_w"), w.pop("w_alpha2"), w.pop("b_alpha")
        w["conv_w_h"] = conv_w.reshape(3, 2, FFN_H).transpose(1, 0, 2)
        w["conv_b_h"] = wts["conv_b"][l].reshape(2, 1, FFN_H)
        w["w2p_f"], w["b2_f"], w["w2p_b"], w["b2_b"] = _decay_weights(w_alpha2, b_alpha)
        w["norm1_g"] = wts["norm1_g"][l][None, :]
        w["norm2_g"] = wts["norm2_g"][l][None, :]
        w["q_norm_g"] = wts["q_norm_g"][l][None, :]
        w["k_norm_g"] = wts["k_norm_g"][l][None, :]
        w["gmlp_norm_g"] = wts["gmlp_norm_g"][l].reshape(GMLP_GROUPS, 1, GMLP_W // GMLP_GROUPS)
        w["w_spatial"] = wts["w_spatial"][l]
        w["b_spatial_t"] = wts["b_spatial"][l][:, :, None]
        w["gla_norm_g"] = wts["gla_norm_g"][l].reshape(GLA_HEADS, 1, GLA_DV)
        layers.append(w)

    craw16 =jnp.concatenate([c_all, wts["c_ctx"][None, :], jnp.zeros((7, D), F32)], axis=0)
    acols = wts["w_ada"].shape[2]
    b_cols = lax.dynamic_slice_in_dim(wts["b_ada"], me * acols, acols, axis=1)[:, None, :]
    mod_part = _adaln_fwd("adaln", craw16, wts["w_ada"], b_cols)
    send = jnp.stack([mod_part[:, 8, :][None].repeat(N_DEV, 0), mod_part[:, :8, :].transpose(1, 0, 2)], axis=2)
    send = jnp.concatenate([send.reshape(N_DEV, 2 * depth, acols), jnp.zeros((N_DEV, 8 - 2 * depth, acols), F32)], axis=1)
    (got,) = _exchange("scatter_mod", [send], gather=False)
    mod = got[:, :2 * depth, :].transpose(1, 0, 2).reshape(depth, 2, N_MOD, 1, D)
    modv = [[mod[l, :, k] for k in range(N_MOD)] for l in range(depth)]

    cosq, sinq = _rope_tables(t, tc)
    consts = (cosq, sinq, _head_mean_matrix(Q_W), _head_tile_matrix(Q_W), _head_mean_matrix(KV_W), _head_tile_matrix(KV_W))
    xs = jnp.concatenate([ctx[0], x[0]], axis=0)
    saved = []
    for l in range(depth):
        xs, sv = _layer_fwd("l%d" % l, xs, layers[l], modv[l], consts, R, nct, tc)
        saved.append(sv)
    loss_blk, dxs, dgf = _final_loss("final", xs, loss_target[0], wts["final_norm_g"][None, :], R, nct)
    loss = lax.psum(loss_blk[0, 0], ("x", "y", "c"))

    grads = [None] * depth
    dmods = [None] * depth
    for l in range(depth - 1, -1, -1):
        dxs, grads[l], dmods[l] = _layer_bwd("l%d" % l, dxs, saved[l], layers[l], modv[l], R, nct, tc)
    grad_x = dxs[tc:][None]

    dmod = jnp.stack([jnp.stack(dmods[l], axis=1) for l in range(depth)])
    dmod = dmod.reshape(depth, 2, N_DEV, acols).transpose(2, 0, 1, 3).reshape(N_DEV, 2 * depth, acols)
    dmod_send = jnp.concatenate([dmod, jnp.zeros((N_DEV, 8 - 2 * depth, acols), F32)], axis=1)
    (dm_got,) = _exchange("scatter_dmod", [dmod_send], gather=False)
    g_w_ada, dc16 = _adaln_bwd("adaln_b", craw16, wts["w_ada"], dm_got[:, :2 * depth].transpose(1, 0, 2))
    db_ada_part = jnp.stack([jnp.stack(dmods[l], axis=1) for l in range(depth)]).reshape(depth, 2, N_MOD * D).sum(axis=1)

    out = {}
    kinds = ("grad", "delta", "new_m", "new_v")
    to_send = {nm: [] for nm, _, _ in _SHARDED}
    for l in range(depth):
        g = dict(grads[l])
        g["conv_w"] = g["conv_w_h"].transpose(1, 0, 2).reshape(3, F2)
        g["w_alpha2"] = jnp.stack([g["w2p_f"][:GLA_RANK], g["w2p_b"][GLA_RANK:2 * GLA_RANK]])
        g["b_alpha"] = jnp.stack([g["b2_f"][0], g["b2_b"][0]])
        for nm, ax, half in _SHARDED:
            sh = _win_shards(g[nm]) if nm == "w_in" else _split_shards(g[nm], ax)
            to_send[nm].append(sh.astype(BF16) if half else sh)
    g_parts = _exchange("scatter_grads", [jnp.stack(to_send[nm], axis=1) for nm, _, _ in _SHARDED], gather=False)
    view2 = lambda a: a.reshape(-1, a.shape[-1])
    for (nm, _, _), parts in zip(_SHARDED, g_parts):
        res = _adamw("adamw_" + nm, view2(wts[nm]), view2(moms[nm]), view2(vels[nm]), parts.reshape(N_DEV, -1, parts.shape[-1]))
        for kind, flat in zip(kinds, res):
            out[kind, nm] = flat.reshape(wts[nm].shape)

    rep_g = dict(
        c_ctx=dc16[8], b_ada=db_ada_part, final_norm_g=dgf[0],
        norm1_g=jnp.stack([grads[l]["norm1_g"][0] for l in range(depth)]),
        norm2_g=jnp.stack([grads[l]["norm2_g"][0] for l in range(depth)]),
        q_norm_g=jnp.stack([grads[l]["q_norm_g"][0] for l in range(depth)]),
        k_norm_g=jnp.stack([grads[l]["k_norm_g"][0] for l in range(depth)]),
        gmlp_norm_g=jnp.stack([grads[l]["gmlp_norm_g"].reshape(GMLP_W) for l in range(depth)]),
        w_spatial=jnp.stack([grads[l]["w_spatial"] for l in range(depth)]),
        b_spatial=jnp.stack([grads[l]["b_spatial_t"][:, :, 0] for l in range(depth)]),
        gla_norm_g=jnp.stack([grads[l]["gla_norm_g"].reshape(GLA_V_W) for l in range(depth)]),
        conv_b=jnp.stack([grads[l]["conv_b_h"].reshape(F2) for l in range(depth)]),
    )
    rep_shapes = [wts[nm].shape for nm in _REPLICATED]
    (rg_parts,) = _exchange("gather_rep_grads", [_pack_rows([rep_g[nm] for nm in _REPLICATED], 16)], gather=True)
    rpk = lambda src: _pack_rows([src[nm] for nm in _REPLICATED], 16)
    res = _adamw("adamw_rep", rpk(wts), rpk(moms), rpk(vels), rg_parts)
    for kind, rows in zip(kinds, res):
        for nm, piece in zip(_REPLICATED, _unpack_rows(rows, rep_shapes)):
            out[kind, nm] = piece

    res = _adamw("adamw_ada", view2(wts["w_ada"]), view2(moms["w_ada"]), view2(vels["w_ada"]), view2(g_w_ada)[None])
    for kind, flat in zip(kinds, res):
        out[kind, "w_ada"] = flat.reshape(wts["w_ada"].shape)

    return (loss, grad_x, *[out[kind, nm] for kind in kinds for nm in _WEIGHTS])


def kernel(x, c, ctx, c_ctx, w_ada, b_ada, norm1_g, norm2_g, w_in, q_norm_g, k_norm_g, gmlp_norm_g, w_spatial, b_spatial, w_alpha2, b_alpha, gla_norm_g, w_br_a, w_br_b, w_br_c, w_out, w_ffn_up, conv_w, conv_b, w_ffn_down, final_norm_g, loss_target, m_c_ctx, m_w_ada, m_b_ada, m_norm1_g, m_norm2_g, m_w_in, m_q_norm_g, m_k_norm_g, m_gmlp_norm_g, m_w_spatial, m_b_spatial, m_w_alpha2, m_b_alpha, m_gla_norm_g, m_w_br_a, m_w_br_b, m_w_br_c, m_w_out, m_w_ffn_up, m_conv_w, m_conv_b, m_w_ffn_down, m_final_norm_g, v_c_ctx, v_w_ada, v_b_ada, v_norm1_g, v_norm2_g, v_w_in, v_q_norm_g, v_k_norm_g, v_gmlp_norm_g, v_w_spatial, v_b_spatial, v_w_alpha2, v_b_alpha, v_gla_norm_g, v_w_br_a, v_w_br_b, v_w_br_c, v_w_out, v_w_ffn_up, v_conv_w, v_conv_b, v_w_ffn_down, v_final_norm_g):
    wts = dict(zip(_WEIGHTS, (c_ctx, w_ada, b_ada, norm1_g, norm2_g, w_in, q_norm_g, k_norm_g, gmlp_norm_g, w_spatial, b_spatial,
                              w_alpha2, b_alpha, gla_norm_g, w_br_a, w_br_b, w_br_c, w_out, w_ffn_up, conv_w, conv_b, w_ffn_down,
                              final_norm_g)))
    moms = dict(zip(_WEIGHTS, (m_c_ctx, m_w_ada, m_b_ada, m_norm1_g, m_norm2_g, m_w_in, m_q_norm_g, m_k_norm_g, m_gmlp_norm_g,
                               m_w_spatial, m_b_spatial, m_w_alpha2, m_b_alpha, m_gla_norm_g, m_w_br_a, m_w_br_b, m_w_br_c, m_w_out,
                               m_w_ffn_up, m_conv_w, m_conv_b, m_w_ffn_down, m_final_norm_g)))
    vels = dict(zip(_WEIGHTS, (v_c_ctx, v_w_ada, v_b_ada, v_norm1_g, v_norm2_g, v_w_in, v_q_norm_g, v_k_norm_g, v_gmlp_norm_g,
                               v_w_spatial, v_b_spatial, v_w_alpha2, v_b_alpha, v_gla_norm_g, v_w_br_a, v_w_br_b, v_w_br_c, v_w_out,
                               v_w_ffn_up, v_conv_w, v_conv_b, v_w_ffn_down, v_final_norm_g)))
    return _step((x, c, ctx, loss_target), wts, moms, vels)
```

```python
import functools

import jax
import jax.numpy as jnp
from jax import lax
from jax.experimental import pallas as pl
from jax.experimental.pallas import tpu as pltpu

F32 = jnp.float32
BF16 = jnp.bfloat16
HI = lax.Precision.HIGHEST
MESH_ID = pl.DeviceIdType.MESH

N_DEV = 8
EPS = 1e-6
D = 1024
N_MOD = 6
HEAD_DIM = 64
N_Q_HEADS = 8
N_KV_HEADS = 2
Q_GROUP = 4
Q_W = 512
KV_W = 128
GRID_W = 64
ROPE_THETA = 10000.0
ROPE_FREQS = 16
GMLP_CHUNK = 128
GMLP_GROUPS = 4
GMLP_W = 512
GLA_HEADS = 4
GLA_QK_W = 256
GLA_V_W = 512
GLA_DK = 64
GLA_DV = 128
GLA_RANK = 16
GLA_TAU = 16.0
GLA_CHUNK = 64
FFN_H = 2816
F2 = 2 * FFN_H
IN_SPLITS = (512, 512, 512, 128, 128, 256, 256, 512, 16, 16, 512, 1024, 1024, 1024)
IN_WIDTH = sum(IN_SPLITS)

C_GA, C_GB, C_GC = 0, 1024, 2048
C_U, C_V, C_Q, C_GV, C_R = 3072, 3584, 4096, 4608, 5120
C_GQ, C_GK = 5632, 5888
C_K, C_VV, C_AF = 6144, 6272, 6400
PW = 6656

ADAM_LR = 0.001
ADAM_B1 = 0.9
ADAM_B2 = 0.999
ADAM_EPS = 1e-08
ADAM_WD = 0.01
ADAM_STEP = 10

V7X_VMEM_LIMIT = 56 * 1024 * 1024

_ARB1 = pltpu.CompilerParams(dimension_semantics=("arbitrary",), vmem_limit_bytes=V7X_VMEM_LIMIT)


def _pick(dim, prefs):
    for p in prefs:
        if dim % p == 0:
            return p
    return dim


def _hdot(a, b, dims=(((1,), (0,)), ((), ()))):
    return lax.dot_general(a, b, dims, precision=HI, preferred_element_type=F32)


_NT = (((1,), (1,)), ((), ()))
_TN = (((0,), (0,)), ((), ()))
_NN = (((1,), (0,)), ((), ()))


def _matmul(name, a, b, mode, out_dtype, *, a_halves=False, b_halves=False, o_halves=False):
    def dims2(x, halves):
        return (x.shape[1], 2 * x.shape[2]) if halves else x.shape

    ar, ac = dims2(a, a_halves)
    br, bc = dims2(b, b_halves)
    if mode == "nn":
        M, K, N = ar, ac, bc
    elif mode == "nt":
        M, K, N = ar, ac, br
    else:
        M, K, N = ac, ar, bc
    row_prefs = (768, 512, 384, 256, 128)
    n_unit = N // 2 if (o_halves or (b_halves and mode != "nt")) else N
    k_unit = K // 2 if (a_halves and mode != "tn") else K
    if mode == "tn":
        tm = _pick(M, (1024, 1408, 512, 256, 128))
        tk = _pick(K, row_prefs)
    else:
        tm = _pick(M, row_prefs)
        tk = _pick(k_unit, (1664, 1408, 1024, 512, 256, 128))
    tn = _pick(n_unit, (1664, 1408, 1024, 512, 256, 128))
    nk = K // tk

    def spec(shape2, halves, blk, imap):
        if not halves:
            return pl.BlockSpec(blk, imap)
        nhalf = (shape2[1] // 2) // blk[1]

        def im(i, j, k):
            r, c = imap(i, j, k)
            return (c // nhalf, r, c % nhalf)
        return pl.BlockSpec((None,) + blk, im)

    if mode == "nn":
        a_spec = spec((ar, ac), a_halves, (tm, tk), lambda i, j, k: (i, k))
        b_spec = spec((br, bc), b_halves, (tk, tn), lambda i, j, k: (k, j))
        dn = _NN
    elif mode == "nt":
        a_spec = spec((ar, ac), a_halves, (tm, tk), lambda i, j, k: (i, k))
        b_spec = spec((br, bc), b_halves, (tn, tk), lambda i, j, k: (j, k))
        dn = _NT
    else:
        a_spec = spec((ar, ac), a_halves, (tk, tm), lambda i, j, k: (k, i))
        b_spec = spec((br, bc), b_halves, (tk, tn), lambda i, j, k: (k, j))
        dn = _TN
    o_spec = spec((M, N), o_halves, (tm, tn), lambda i, j, k: (i, j))
    o_shape = (2, M, N // 2) if o_halves else (M, N)

    def body(a_ref, b_ref, o_ref, acc_ref):
        k = pl.program_id(2)
        part = lax.dot_general(a_ref[...], b_ref[...], dn, preferred_element_type=F32)
        if nk == 1:
            o_ref[...] = part.astype(o_ref.dtype)
        else:
            @pl.when(k == 0)
            def _():
                acc_ref[...] = part

            @pl.when(k > 0)
            def _():
                acc_ref[...] += part

            @pl.when(k == nk - 1)
            def _():
                o_ref[...] = acc_ref[...].astype(o_ref.dtype)

    return pl.pallas_call(
        body, name=name, grid=(M // tm, N // tn, nk),
        in_specs=[a_spec, b_spec], out_specs=o_spec,
        out_shape=jax.ShapeDtypeStruct(o_shape, out_dtype),
        scratch_shapes=[pltpu.VMEM((tm, tn), F32)],
        compiler_params=pltpu.CompilerParams(dimension_semantics=("parallel", "parallel", "arbitrary"),
                                             vmem_limit_bytes=V7X_VMEM_LIMIT),
    )(a, b)


def _full_spec(shape):
    nd = len(shape)
    return pl.BlockSpec(tuple(shape), lambda i, _nd=nd: (0,) * _nd)


def _row_spec(R, W, cb):
    return pl.BlockSpec((R, W), lambda i, _cb=cb: (i, _cb))


def _load_rows(refs, specs):
    vals = []
    for ref, (_, W, _, pw) in zip(refs, specs):
        if pw == W:
            vals.append(ref[...].astype(F32))
        else:
            vals.append([ref[:, k * pw:(k + 1) * pw].astype(F32) for k in range(W // pw)])
    return vals


def _load_params(refs, specs):
    vals = []
    for ref, (arr, split) in zip(refs, specs):
        if split:
            vals.append([ref[k] for k in range(arr.shape[0])])
        else:
            vals.append(ref[...])
    return vals


def _mod_spec(nct, width):
    return pl.BlockSpec((None, 1, width), lambda i: (jnp.minimum(i // nct, 1), 0, 0))


def _rowwise(name, f, R, nct, rows, consts, params, cparams, mods, outs, post=None):
    n = rows[0][0].shape[0]
    nr, nc, npar, ncp, nm = len(rows), len(consts), len(params), len(cparams), len(mods)

    def body(*refs):
        pos = 0
        rr = refs[pos:pos + nr]; pos += nr
        cr = refs[pos:pos + nc]; pos += nc
        pr = refs[pos:pos + npar]; pos += npar
        cpr = refs[pos:pos + ncp]; pos += ncp
        mr = refs[pos:pos + nm]; pos += nm
        orefs = refs[pos:]
        res = f(_load_rows(rr, rows), _load_params(pr, params), [m[...] for m in mr],
                _load_rows(cr, consts), _load_params(cpr, cparams))
        if post is not None:
            res = post(res, _load_rows(cr, consts))
        for o_ref, r in zip(orefs, res):
            o_ref[...] = r.astype(o_ref.dtype)

    in_specs = ([_row_spec(R, W, cb) for (_, W, cb, _) in rows + consts]
                + [_full_spec(a.shape) for (a, _) in params + cparams]
                + [_mod_spec(nct, m.shape[2]) for m in mods])
    args = [a for (a, _, _, _) in rows + consts] + [a for (a, _) in params + cparams] + list(mods)
    return pl.pallas_call(
        body, name=name, grid=(n // R,), in_specs=in_specs,
        out_specs=[_row_spec(R, w, 0) for (w, _) in outs],
        out_shape=[jax.ShapeDtypeStruct((n, w), dt) for (w, dt) in outs],
        compiler_params=_ARB1,
    )(*args)


def _rowwise_bwd(name, f, R, nct, rows, consts, params, cparams, mods, douts, drow, adds=None, pre=None):
    n = rows[0][0].shape[0]
    adds = adds or [None] * len(rows)
    nr, nc, npar, ncp, nm, nd = len(rows), len(consts), len(params), len(cparams), len(mods), len(douts)
    add_ix = [k for k in range(nr) if adds[k] is not None]
    out_ix = [k for k in range(nr) if drow[k] is not None]

    def body(*refs):
        i = pl.program_id(0)
        pos = 0
        rr = refs[pos:pos + nr]; pos += nr
        cr = refs[pos:pos + nc]; pos += nc
        pr = refs[pos:pos + npar]; pos += npar
        cpr = refs[pos:pos + ncp]; pos += ncp
        mr = refs[pos:pos + nm]; pos += nm
        dr = refs[pos:pos + nd]; pos += nd
        ar = refs[pos:pos + len(add_ix)]; pos += len(add_ix)
        drr = refs[pos:pos + len(out_ix)]; pos += len(out_ix)
        dpr = refs[pos:pos + npar]; pos += npar
        dmr = refs[pos:pos + nm]; pos += nm

        cv = _load_rows(cr, consts)
        cpv = _load_params(cpr, cparams)
        _, vjp = jax.vjp(lambda rv, pv, mv: f(rv, pv, mv, cv, cpv),
                         _load_rows(rr, rows), _load_params(pr, params), [m[...] for m in mr])
        dv = [d[...].astype(F32) for d in dr]
        if pre is not None:
            dv = pre(dv, cv)
        g_rows, g_params, g_mods = vjp(tuple(dv))

        for ref, k in zip(drr, out_ix):
            _, W, _, pw = rows[k]
            g = g_rows[k]
            extra = ar[add_ix.index(k)] if k in add_ix else None
            if pw == W:
                if extra is not None:
                    g = g + extra[...].astype(F32)
                ref[...] = g.astype(ref.dtype)
            else:
                for q in range(W // pw):
                    gq = g[q]
                    if extra is not None:
                        gq = gq + extra[:, q * pw:(q + 1) * pw].astype(F32)
                    ref[:, q * pw:(q + 1) * pw] = gq.astype(ref.dtype)

        @pl.when(i == 0)
        def _():
            for ref in dpr:
                ref[...] = jnp.zeros_like(ref)

        for ref, (arr, split), g in zip(dpr, params, g_params):
            if split:
                for k in range(arr.shape[0]):
                    ref[k] += g[k]
            else:
                ref[...] += g

        @pl.when((i == 0) | (i == nct))
        def _():
            for ref in dmr:
                ref[...] = jnp.zeros_like(ref)

        for ref, g in zip(dmr, g_mods):
            ref[...] += g

    in_specs = ([_row_spec(R, W, cb) for (_, W, cb, _) in rows + consts]
                + [_full_spec(a.shape) for (a, _) in params + cparams]
                + [_mod_spec(nct, m.shape[2]) for m in mods]
                + [_row_spec(R, W, 0) for (_, W) in douts]
                + [_row_spec(R, rows[k][1], 0) for k in add_ix])
    args = ([a for (a, _, _, _) in rows + consts] + [a for (a, _) in params + cparams] + list(mods)
            + [a for (a, _) in douts] + [adds[k] for k in add_ix])
    out_specs = ([_row_spec(R, rows[k][1], 0) for k in out_ix]
                 + [_full_spec(a.shape) for (a, _) in params]
                 + [_mod_spec(nct, m.shape[2]) for m in mods])
    out_shape = ([jax.ShapeDtypeStruct((n, rows[k][1]), drow[k]) for k in out_ix]
                 + [jax.ShapeDtypeStruct(a.shape, F32) for (a, _) in params]
                 + [jax.ShapeDtypeStruct(m.shape, F32) for m in mods])
    res = pl.pallas_call(
        body, name=name, grid=(n // R,), in_specs=in_specs, out_specs=out_specs, out_shape=out_shape,
        compiler_params=_ARB1,
    )(*args)
    no = len(out_ix)
    return list(res[:no]), list(res[no:no + npar]), list(res[no + npar:])


def _rms(x, g):
    return x * lax.rsqrt(jnp.mean(x * x, axis=-1, keepdims=True) + EPS) * g


def _f_lnmod(rv, pv, mv, cv, cpv):
    (x,), (g,), (shift, scale) = rv, pv, mv
    return (_rms(x, g) * (1.0 + scale) + shift,)


def _f_resid(rv, pv, mv, cv, cpv):
    (x, y), (gate,) = rv, mv
    return (x + gate * y,)


def _f_merge(rv, pv, mv, cv, cpv):
    ga, gb, gc, ya, yb, yc = rv
    return (_sigmoid(ga) * ya + _sigmoid(gb) * yb + _sigmoid(gc) * yc,)


def _f_qknorm(rv, pv, mv, cv, cpv):
    (q, k), (gq, gk), (bdq, eq, bdk, ek) = rv, pv, cpv
    qn = q * lax.rsqrt(_hdot(q * q, bdq) + EPS) * _hdot(gq, eq)
    kn = k * lax.rsqrt(_hdot(k * k, bdk) + EPS) * _hdot(gk, ek)
    return (qn, kn)


def _rope(x, cos, sin):
    w = x.shape[1]
    lane = lax.broadcasted_iota(jnp.int32, x.shape, 1)
    partner = jnp.where((lane & 31) < 16, pltpu.roll(x, w - 16, 1), pltpu.roll(x, 16, 1))
    return x * cos + partner * sin


def _qk_post(res, cv):
    (qn, kn), (cq, sq, ck, sk) = res, cv
    return (_rope(qn, cq, sq) * (HEAD_DIM ** -0.5), _rope(kn, ck, sk))


def _qk_pre(dv, cv):
    (dq, dk), (cq, sq, ck, sk) = dv, cv
    return (_rope(dq * (HEAD_DIM ** -0.5), cq, -sq), _rope(dk, ck, -sk))


def _f_gmlp(rv, pv, mv, cv, cpv):
    (u, vs), (ng, ws, bt) = rv, pv
    pieces = []
    for g in range(GMLP_GROUPS):
        vn = _rms(jax.nn.gelu(vs[g]), ng[g])
        pieces.append(_hdot(ws[g], vn) + bt[g])
    return (jax.nn.gelu(u) * jnp.concatenate(pieces, axis=1),)


def _f_glaout(rv, pv, mv, cv, cpv):
    (os_, rs), (gn,) = rv, pv
    pieces = [_rms(os_[h], gn[h]) * (rs[h] * _sigmoid(rs[h])) for h in range(GLA_HEADS)]
    return (jnp.concatenate(pieces, axis=1),)


_CONV_CB = 1408


def _sigmoid(x):
    return 0.5 * jnp.tanh(0.5 * x) + 0.5


def _halo_keep(i, R, tc, n):
    first, end = i * R, (i + 1) * R
    keep_prev = jnp.where((first == 0) | (first == tc), 0.0, 1.0)
    keep_next = jnp.where((end == tc) | (end == n), 0.0, 1.0)
    return keep_prev, keep_next


def _conv_specs(R, n):
    nb8 = n // 8
    main = pl.BlockSpec((2, R, _CONV_CB), lambda j, i: (0, i, j))
    prev = pl.BlockSpec((2, 8, _CONV_CB), lambda j, i: (0, jnp.maximum(i * (R // 8) - 1, 0), j))
    nxt = pl.BlockSpec((2, 8, _CONV_CB), lambda j, i: (0, jnp.minimum((i + 1) * (R // 8), nb8 - 1), j))
    cw = pl.BlockSpec((2, 3, _CONV_CB), lambda j, i: (0, 0, j))
    cb = pl.BlockSpec((2, 1, _CONV_CB), lambda j, i: (0, 0, j))
    return main, prev, nxt, cw, cb


def _conv_act(ext_ref, h, first, count, cw_ref, cb_ref):
    prev = ext_ref[h, pl.ds(first - 1, count), :]
    mid = ext_ref[h, pl.ds(first, count), :]
    nxt = ext_ref[h, pl.ds(first + 1, count), :]
    return cb_ref[h] + cw_ref[h, 1:2, :] * mid + cw_ref[h, 0:1, :] * prev + cw_ref[h, 2:3, :] * nxt


def _conv_gate(name, a2, cw, cb, R, tc):
    n = a2.shape[1]
    main, prev, nxt, cws, cbs = _conv_specs(R, n)

    def body(a_ref, p_ref, n_ref, cw_ref, cb_ref, o_ref, ext):
        i = pl.program_id(1)
        keep_prev, keep_next = _halo_keep(i, R, tc, n)
        ext[:, 0:8, :] = jnp.zeros((2, 8, _CONV_CB), F32)
        ext[:, 8:16, :] = p_ref[...] * keep_prev
        ext[:, 16:16 + R, :] = a_ref[...]
        ext[:, 16 + R:24 + R, :] = n_ref[...] * keep_next
        ext[:, 24 + R:32 + R, :] = jnp.zeros((2, 8, _CONV_CB), F32)
        g = _conv_act(ext, 0, 16, R, cw_ref, cb_ref)
        v = _conv_act(ext, 1, 16, R, cw_ref, cb_ref)
        o_ref[...] = (g * _sigmoid(g) * v).astype(o_ref.dtype)

    return pl.pallas_call(
        body, name=name, grid=(FFN_H // _CONV_CB, n // R),
        in_specs=[main, prev, nxt, cws, cbs],
        out_specs=pl.BlockSpec((R, _CONV_CB), lambda j, i: (i, j)),
        out_shape=jax.ShapeDtypeStruct((n, FFN_H), BF16),
        scratch_shapes=[pltpu.VMEM((2, R + 32, _CONV_CB), F32)],
        compiler_params=pltpu.CompilerParams(dimension_semantics=("arbitrary", "arbitrary"),
                                             vmem_limit_bytes=V7X_VMEM_LIMIT),
    )(a2, a2, a2, cw, cb)


def _conv_gate_bwd(name, a2, dgv, cw, cb, R, tc):
    n = a2.shape[1]
    nb8 = n // 8
    main, prev, nxt, cws, cbs = _conv_specs(R, n)
    d_main = pl.BlockSpec((R, _CONV_CB), lambda j, i: (i, j))
    d_prev = pl.BlockSpec((8, _CONV_CB), lambda j, i: (jnp.maximum(i * (R // 8) - 1, 0), j))
    d_next = pl.BlockSpec((8, _CONV_CB), lambda j, i: (jnp.minimum((i + 1) * (R // 8), nb8 - 1), j))
    RE = R + 16

    def body(a_ref, p_ref, n_ref, cw_ref, cb_ref, d_ref, dp_ref, dn_ref, da_ref, dcw_ref, dcb_ref, ext, dext):
        i = pl.program_id(1)
        keep_prev, keep_next = _halo_keep(i, R, tc, n)
        zeros8 = jnp.zeros((2, 8, _CONV_CB), F32)
        ext[:, 0:8, :] = zeros8
        ext[:, 8:16, :] = p_ref[...] * keep_prev
        ext[:, 16:16 + R, :] = a_ref[...]
        ext[:, 16 + R:24 + R, :] = n_ref[...] * keep_next
        ext[:, 24 + R:32 + R, :] = zeros8
        g = _conv_act(ext, 0, 8, RE, cw_ref, cb_ref)
        v = _conv_act(ext, 1, 8, RE, cw_ref, cb_ref)
        dout = jnp.concatenate([dp_ref[...].astype(F32) * keep_prev, d_ref[...].astype(F32),
                                dn_ref[...].astype(F32) * keep_next], axis=0)
        sg = _sigmoid(g)
        dext[:, 0:8, :] = zeros8
        dext[:, 24 + R:32 + R, :] = zeros8
        dext[0, 8:8 + RE, :] = dout * v * sg * (1.0 + g * (1.0 - sg))
        dext[1, 8:8 + RE, :] = dout * g * sg

        @pl.when(i == 0)
        def _():
            dcw_ref[...] = jnp.zeros_like(dcw_ref)
            dcb_ref[...] = jnp.zeros_like(dcb_ref)

        for h in range(2):
            da = dext[h, 16:16 + R, :]
            da_up = dext[h, 17:17 + R, :]
            da_dn = dext[h, 15:15 + R, :]
            da_ref[h] = (cw_ref[h, 1:2, :] * da + cw_ref[h, 0:1, :] * da_up + cw_ref[h, 2:3, :] * da_dn).astype(da_ref.dtype)
            a_prev = ext[h, 15:15 + R, :]
            a_mid = ext[h, 16:16 + R, :]
            a_next = ext[h, 17:17 + R, :]
            dcw_ref[h, 0:1, :] += jnp.sum(da * a_prev, axis=0, keepdims=True)
            dcw_ref[h, 1:2, :] += jnp.sum(da * a_mid, axis=0, keepdims=True)
            dcw_ref[h, 2:3, :] += jnp.sum(da * a_next, axis=0, keepdims=True)
            dcb_ref[h] += jnp.sum(da, axis=0, keepdims=True)

    return pl.pallas_call(
        body, name=name, grid=(FFN_H // _CONV_CB, n // R),
        in_specs=[main, prev, nxt, cws, cbs, d_main, d_prev, d_next],
        out_specs=[main, cws, cbs],
        out_shape=[jax.ShapeDtypeStruct((2, n, FFN_H), BF16), jax.ShapeDtypeStruct((2, 3, FFN_H), F32),
                   jax.ShapeDtypeStruct((2, 1, FFN_H), F32)],
        scratch_shapes=[pltpu.VMEM((2, R + 32, _CONV_CB), F32), pltpu.VMEM((2, R + 32, _CONV_CB), F32)],
        compiler_params=pltpu.CompilerParams(dimension_semantics=("arbitrary", "arbitrary"),
                                             vmem_limit_bytes=V7X_VMEM_LIMIT),
    )(a2, a2, a2, cw, cb, dgv, dgv, dgv)


_ARB2 = pltpu.CompilerParams(dimension_semantics=("arbitrary", "arbitrary"), vmem_limit_bytes=V7X_VMEM_LIMIT)


def _with_exchange(body, n_in, n_out, grid, comm):
    if comm is None:
        return body, [], [], []
    xs, gather = comm
    na = len(xs)
    specs, shapes, sems = _exchange_io(xs, gather)

    def wrapped(*refs):
        ins, x_refs = refs[:n_in], refs[n_in:n_in + na]
        outs, o_refs = refs[n_in + na:n_in + na + n_out], refs[n_in + na + n_out:n_in + 2 * na + n_out]
        scratch, sem_refs = refs[n_in + 2 * na + n_out:-3], refs[-3:]
        start, wait = _exchange_plan(x_refs, o_refs, *sem_refs, gather)
        ids = [pl.program_id(d) for d in range(len(grid))]
        first, last = ids[0] == 0, ids[0] == grid[0] - 1
        for d in range(1, len(grid)):
            first, last = first & (ids[d] == 0), last & (ids[d] == grid[d] - 1)

        @pl.when(first)
        def _():
            start()

        body(*ins, *outs, *scratch)

        @pl.when(last)
        def _():
            wait()

    return wrapped, specs, shapes, sems


def _attn_fwd(name, qt8, k2, vt2, comm=None):
    nq, nk = qt8.shape[2], k2.shape[1]
    tq = _pick(nq, (256, 128))
    tk = _pick(nk, (768, 256, 128))

    def body(qt_ref, k_ref, vt_ref, ot_ref, lse_ref, m_ref, l_ref, acc_ref):
        m_ref[...] = jnp.full((Q_GROUP, 1, tq), -1e30, F32)
        l_ref[...] = jnp.zeros((Q_GROUP, 1, tq), F32)
        acc_ref[...] = jnp.zeros((Q_GROUP, HEAD_DIM, tq), F32)

        def step(j, carry):
            sl = pl.ds(pl.multiple_of(j * tk, tk), tk)
            kj = k_ref[sl, :]
            vtj = vt_ref[:, sl]
            sts = [lax.dot_general(kj, qt_ref[h], _NN, preferred_element_type=F32) for h in range(Q_GROUP)]
            m_old = [m_ref[h] for h in range(Q_GROUP)]
            l_old = [l_ref[h] for h in range(Q_GROUP)]
            acc_old = [acc_ref[h] for h in range(Q_GROUP)]
            m_new = [jnp.maximum(m_old[h], jnp.max(sts[h], axis=0, keepdims=True)) for h in range(Q_GROUP)]
            pts = [jnp.exp(sts[h] - m_new[h]) for h in range(Q_GROUP)]
            pvs = [lax.dot_general(vtj, pts[h].astype(BF16), _NN, preferred_element_type=F32) for h in range(Q_GROUP)]
            for h in range(Q_GROUP):
                alpha = jnp.exp(m_old[h] - m_new[h])
                l_ref[h] = alpha * l_old[h] + jnp.sum(pts[h], axis=0, keepdims=True)
                acc_ref[h] = alpha * acc_old[h] + pvs[h]
                m_ref[h] = m_new[h]
            return carry

        lax.fori_loop(0, nk // tk, step, 0)
        for h in range(Q_GROUP):
            l = l_ref[h]
            ot_ref[h] = (acc_ref[h] / l).astype(ot_ref.dtype)
            lse_ref[h] = m_ref[h] + jnp.log(l)

    qspec = pl.BlockSpec((Q_GROUP, HEAD_DIM, tq), lambda g, i: (g, 0, i))
    lspec = pl.BlockSpec((Q_GROUP, 1, tq), lambda g, i: (g, 0, i))
    grid = (N_KV_HEADS, nq // tq)
    body, xspecs, xshapes, xsems = _with_exchange(body, 3, 2, grid, comm)
    res = pl.pallas_call(
        body, name=name, grid=grid,
        in_specs=[qspec, pl.BlockSpec((None, nk, HEAD_DIM), lambda g, i: (g, 0, 0)),
                  pl.BlockSpec((None, HEAD_DIM, nk), lambda g, i: (g, 0, 0))] + xspecs,
        out_specs=[qspec, lspec] + xspecs,
        out_shape=[jax.ShapeDtypeStruct((N_Q_HEADS, HEAD_DIM, nq), BF16), jax.ShapeDtypeStruct((N_Q_HEADS, 1, nq), F32)] + xshapes,
        scratch_shapes=[pltpu.VMEM((Q_GROUP, 1, tq), F32), pltpu.VMEM((Q_GROUP, 1, tq), F32),
                        pltpu.VMEM((Q_GROUP, HEAD_DIM, tq), F32)] + xsems,
        compiler_params=_ARB2,
    )(qt8, k2, vt2, *(comm[0] if comm else []))
    return res[0], res[1], list(res[2:])


def _attn_bwd(name, qt8, q8, k2, v2, kt2, ot8, dot8, do8, lse, comm=None):
    nq, nk = qt8.shape[2], k2.shape[1]
    tq = _pick(nq, (256, 128))
    tk = _pick(nk, (768, 256, 128))

    def body(qt_ref, q_ref, k_ref, v_ref, kt_ref, ot_ref, dot_ref, do_ref, lse_ref, dqt_ref, dk_ref, dv_ref, dl_ref, dq_acc):
        @pl.when(pl.program_id(1) == 0)
        def _():
            dk_ref[...] = jnp.zeros_like(dk_ref)
            dv_ref[...] = jnp.zeros_like(dv_ref)

        for h in range(Q_GROUP):
            dl_ref[h] = jnp.sum(dot_ref[h].astype(F32) * ot_ref[h].astype(F32), axis=0, keepdims=True)
        dq_acc[...] = jnp.zeros((Q_GROUP, HEAD_DIM, tq), F32)

        def step(j, carry):
            sl = pl.ds(pl.multiple_of(j * tk, tk), tk)
            kj = k_ref[sl, :]
            vj = v_ref[sl, :]
            ktj = kt_ref[:, sl]
            heads = range(Q_GROUP)
            sts = [lax.dot_general(kj, qt_ref[h], _NN, preferred_element_type=F32) for h in heads]
            dpts = [lax.dot_general(vj, dot_ref[h], _NN, preferred_element_type=F32) for h in heads]
            pts = [jnp.exp(sts[h] - lse_ref[h]) for h in heads]
            dsts = [(pts[h] * (dpts[h] - dl_ref[h])).astype(BF16) for h in heads]
            dvs = [lax.dot_general(pts[h].astype(BF16), do_ref[h], _NN, preferred_element_type=F32) for h in heads]
            dks = [lax.dot_general(dsts[h], q_ref[h], _NN, preferred_element_type=F32) for h in heads]
            dqs = [lax.dot_general(ktj, dsts[h], _NN, preferred_element_type=F32) for h in heads]
            for h in heads:
                dq_acc[h] += dqs[h]
            dk_ref[sl, :] += (dks[0] + dks[1]) + (dks[2] + dks[3])
            dv_ref[sl, :] += (dvs[0] + dvs[1]) + (dvs[2] + dvs[3])
            return carry

        lax.fori_loop(0, nk // tk, step, 0)
        dqt_ref[...] = dq_acc[...]

    tspec = pl.BlockSpec((Q_GROUP, HEAD_DIM, tq), lambda g, i: (g, 0, i))
    rspec = pl.BlockSpec((Q_GROUP, tq, HEAD_DIM), lambda g, i: (g, i, 0))
    lspec = pl.BlockSpec((Q_GROUP, 1, tq), lambda g, i: (g, 0, i))
    kspec = pl.BlockSpec((None, nk, HEAD_DIM), lambda g, i: (g, 0, 0))
    ktspec = pl.BlockSpec((None, HEAD_DIM, nk), lambda g, i: (g, 0, 0))
    grid = (N_KV_HEADS, nq // tq)
    body, xspecs, xshapes, xsems = _with_exchange(body, 9, 3, grid, comm)
    res = pl.pallas_call(
        body, name=name, grid=grid,
        in_specs=[tspec, rspec, kspec, kspec, ktspec, tspec, tspec, rspec, lspec] + xspecs,
        out_specs=[tspec, kspec, kspec] + xspecs,
        out_shape=[jax.ShapeDtypeStruct((N_Q_HEADS, HEAD_DIM, nq), F32), jax.ShapeDtypeStruct((N_KV_HEADS, nk, HEAD_DIM), F32),
                   jax.ShapeDtypeStruct((N_KV_HEADS, nk, HEAD_DIM), F32)] + xshapes,
        scratch_shapes=[pltpu.VMEM((Q_GROUP, 1, tq), F32), pltpu.VMEM((Q_GROUP, HEAD_DIM, tq), F32)] + xsems,
        compiler_params=_ARB2,
    )(qt8, q8, k2, v2, kt2, ot8, dot8, do8, lse, *(comm[0] if comm else []))
    return res[0], res[1], res[2], list(res[3:])


def _log_sigmoid(z):
    return jnp.minimum(z, 0.0) - jnp.log(1.0 + jnp.exp(-jnp.abs(z)))


_BDOT_DIMS = {"nn": _NN, "nt": _NT, "tn": _TN}
_BDOT_BWD = {"nn": (("nt", "gb"), ("tn", "ag")), "nt": (("nn", "gb"), ("tn", "ga")), "tn": (("nt", "bg"), ("nn", "ag"))}


def _bdot_raw(a, b, mode):
    return lax.dot_general(a.astype(BF16), b.astype(BF16), _BDOT_DIMS[mode], preferred_element_type=F32)


@functools.partial(jax.custom_vjp, nondiff_argnums=(2,))
def _bdot(a, b, mode):
    return _bdot_raw(a, b, mode)


def _bdot_fwd(a, b, mode):
    return _bdot_raw(a, b, mode), (a.astype(BF16), b.astype(BF16))


def _bdot_bwd(mode, res, g):
    ops = {"a": res[0], "b": res[1], "g": g}
    (ma, oa), (mb, ob) = _BDOT_BWD[mode]
    return _bdot_raw(ops[oa[0]], ops[oa[1]], ma), _bdot_raw(ops[ob[0]], ops[ob[1]], mb)


_bdot.defvjp(_bdot_fwd, _bdot_bwd)


def _tile_tri(rev, rows):
    r_i = lax.broadcasted_iota(jnp.int32, (rows, rows), 0)
    c_i = lax.broadcasted_iota(jnp.int32, (rows, rows), 1)
    same = (r_i // GLA_CHUNK) == (c_i // GLA_CHUNK)
    return same & ((c_i >= r_i) if rev else (c_i <= r_i))


def _tri_dot(rev, x):
    tri = _tile_tri(rev, x.shape[0]).astype(BF16)
    hi = x.astype(BF16)
    rest = x - hi.astype(F32)
    mid = rest.astype(BF16)
    lo = (rest - mid.astype(F32)).astype(BF16)
    d = lambda piece: lax.dot_general(tri, piece, _NN, preferred_element_type=F32)
    return d(hi) + d(mid) + d(lo)


@functools.partial(jax.custom_vjp, nondiff_argnums=(1,))
def _chunk_cumsum(x, rev):
    return _tri_dot(rev, x)


_chunk_cumsum.defvjp(lambda x, rev: (_tri_dot(rev, x), None), lambda rev, _, g: (_tri_dot(not rev, g),))


def _gla_tile(q, k, vs, a, w2, b2, state_t, *, rev):
    rows = q.shape[0]
    nch = rows // GLA_CHUNK
    tri = _tile_tri(rev, rows)
    chunk_of_row = lax.broadcasted_iota(jnp.int32, (rows, 1), 0) // GLA_CHUNK
    in_chunk = [(chunk_of_row == c).astype(F32) for c in range(nch)]
    la = _log_sigmoid(_bdot(a, w2, "nn") + b2) * (1.0 / GLA_TAU)
    cum = _chunk_cumsum(la, rev)
    tots = [jnp.sum(la * in_chunk[c], axis=0, keepdims=True) for c in range(nch)]
    tot_rows = sum(in_chunk[c] * tots[c] for c in range(nch))
    q_in = q * (GLA_DK ** -0.5) * jnp.exp(cum)
    k_in = k * jnp.exp(-cum)
    k_st = k * jnp.exp(tot_rows - cum)
    lane = lax.broadcasted_iota(jnp.int32, (1, GLA_QK_W), 1)
    outs = []
    for h in range(GLA_HEADS):
        head = ((lane >= GLA_DK * h) & (lane < GLA_DK * (h + 1))).astype(F32)
        att = jnp.where(tri, _bdot(q_in * head, k_in, "nt"), 0.0)
        outs.append(_bdot(att, vs[h], "nn"))
    o = jnp.concatenate(outs, axis=1)
    hr = lax.broadcasted_iota(jnp.int32, (GLA_V_W, GLA_QK_W), 0) // GLA_DV
    hc = lax.broadcasted_iota(jnp.int32, (GLA_V_W, GLA_QK_W), 1) // GLA_DK
    same_head = (hr == hc).astype(F32)
    v_all = jnp.concatenate(vs, axis=1)
    for c in (range(nch - 1, -1, -1) if rev else range(nch)):
        o = o + _bdot(q_in * in_chunk[c], state_t, "nt")
        state_t = jnp.exp(tots[c]) * state_t + _bdot(v_all, k_st * in_chunk[c], "tn") * same_head
    return o, state_t


def _gla_tile_of(step, rev, nct, nt):
    if not rev:
        return step
    return jnp.where(step < nct, nct - 1 - step, nt - 1 - (step - nct))


def _gla_row_specs(R, tile):
    return [pl.BlockSpec((R, GLA_QK_W), lambda s: (tile(s), C_GQ // GLA_QK_W)),
            pl.BlockSpec((R, GLA_QK_W), lambda s: (tile(s), C_GK // GLA_QK_W)),
            pl.BlockSpec((R, GLA_V_W), lambda s: (tile(s), C_GV // GLA_V_W)),
            pl.BlockSpec((R, 128), lambda s: (tile(s), C_AF // 128))]


def _gla_dir(name, p, w2, b2, *, rev, R, nct, add=None):
    n = p.shape[0]
    nt = n // R
    tile = lambda s: _gla_tile_of(s, rev, nct, nt)

    def body(q_ref, k_ref, v_ref, a_ref, w2_ref, b2_ref, *rest):
        if add is not None:
            add_ref, o_ref, ssave_ref, state = rest
        else:
            o_ref, ssave_ref, state = rest

        @pl.when(pl.program_id(0) == 0)
        def _():
            state[...] = jnp.zeros_like(state)

        vs = [v_ref[:, GLA_DV * h:GLA_DV * (h + 1)] for h in range(GLA_HEADS)]
        s_in = state[...]
        ssave_ref[...] = s_in
        o, s_out = _gla_tile(q_ref[...], k_ref[...], vs, a_ref[...], w2_ref[...], b2_ref[...], s_in, rev=rev)
        if add is not None:
            o = o + add_ref[...]
        o_ref[...] = o
        state[...] = s_out

    o_spec = pl.BlockSpec((R, GLA_V_W), lambda s: (tile(s), 0))
    in_specs = _gla_row_specs(R, tile) + [_full_spec(w2.shape), _full_spec(b2.shape)]
    args = [p, p, p, p, w2, b2]
    if add is not None:
        in_specs.append(o_spec)
        args.append(add)
    return pl.pallas_call(
        body, name=name, grid=(nt,), in_specs=in_specs,
        out_specs=[o_spec, pl.BlockSpec((None, GLA_V_W, GLA_QK_W), lambda s: (tile(s), 0, 0))],
        out_shape=[jax.ShapeDtypeStruct((n, GLA_V_W), F32), jax.ShapeDtypeStruct((nt, GLA_V_W, GLA_QK_W), F32)],
        scratch_shapes=[pltpu.VMEM((GLA_V_W, GLA_QK_W), F32)],
        compiler_params=_ARB1,
    )(*args)


def _gla_dir_bwd(name, p, w2, b2, ssave, do, *, rev, R, nct, adds=None, out_dtype=F32):
    n = p.shape[0]
    nt = n // R
    tile = lambda s: _gla_tile_of(nt - 1 - s, rev, nct, nt)
    widths = (GLA_QK_W, GLA_QK_W, GLA_V_W, 128)

    def body(q_ref, k_ref, v_ref, a_ref, w2_ref, b2_ref, ss_ref, do_ref, *rest):
        if adds is not None:
            add_refs, rest = rest[:4], rest[4:]
        dq_ref, dk_ref, dv_ref, da_ref, dw2_ref, db2_ref, dstate = rest

        @pl.when(pl.program_id(0) == 0)
        def _():
            dstate[...] = jnp.zeros_like(dstate)
            dw2_ref[...] = jnp.zeros_like(dw2_ref)
            db2_ref[...] = jnp.zeros_like(db2_ref)

        vs = [v_ref[:, GLA_DV * h:GLA_DV * (h + 1)] for h in range(GLA_HEADS)]
        _, vjp = jax.vjp(functools.partial(_gla_tile, rev=rev), q_ref[...], k_ref[...], vs, a_ref[...],
                         w2_ref[...], b2_ref[...], ss_ref[...])
        dq, dk, dvs, da, dw2, db2, ds = vjp((do_ref[...], dstate[...]))
        grads = [dq, dk, jnp.concatenate(dvs, axis=1), da]
        if adds is not None:
            grads = [g + r[...].astype(F32) for g, r in zip(grads, add_refs)]
        for ref, g in zip((dq_ref, dk_ref, dv_ref, da_ref), grads):
            ref[...] = g.astype(ref.dtype)
        dw2_ref[...] += dw2
        db2_ref[...] += db2
        dstate[...] = ds

    d_specs = [pl.BlockSpec((R, w), lambda s: (tile(s), 0)) for w in widths]
    in_specs = (_gla_row_specs(R, tile) + [_full_spec(w2.shape), _full_spec(b2.shape),
                pl.BlockSpec((None, GLA_V_W, GLA_QK_W), lambda s: (tile(s), 0, 0)),
                pl.BlockSpec((R, GLA_V_W), lambda s: (tile(s), 0))])
    args = [p, p, p, p, w2, b2, ssave, do]
    if adds is not None:
        in_specs += d_specs
        args += list(adds)
    return pl.pallas_call(
        body, name=name, grid=(nt,), in_specs=in_specs,
        out_specs=d_specs + [_full_spec(w2.shape), _full_spec(b2.shape)],
        out_shape=[jax.ShapeDtypeStruct((n, w), out_dtype) for w in widths]
        + [jax.ShapeDtypeStruct(w2.shape, F32), jax.ShapeDtypeStruct(b2.shape, F32)],
        scratch_shapes=[pltpu.VMEM((GLA_V_W, GLA_QK_W), F32)],
        compiler_params=_ARB1,
    )(*args)


def _final_loss(name, x, target, gf, R, nct):
    n = x.shape[0]

    def body(x_ref, t_ref, g_ref, loss_ref, dx_ref, dg_ref):
        i = pl.program_id(0)

        @pl.when(i == 0)
        def _():
            loss_ref[...] = jnp.zeros_like(loss_ref)
            dg_ref[...] = jnp.zeros_like(dg_ref)

        @pl.when(i < nct)
        def _():
            dx_ref[...] = jnp.zeros_like(dx_ref)

        @pl.when(i >= nct)
        def _():
            y, vjp = jax.vjp(_rms, x_ref[...], g_ref[...])
            err = y - t_ref[...]
            loss_ref[...] += jnp.sum(0.5 * jnp.mean(err * err, axis=-1, keepdims=True))
            dx, dg = vjp(err * (1.0 / D))
            dx_ref[...] = dx
            dg_ref[...] += dg

    return pl.pallas_call(
        body, name=name, grid=(n // R,),
        in_specs=[_row_spec(R, D, 0), pl.BlockSpec((R, D), lambda i: (jnp.maximum(i - nct, 0), 0)), _full_spec((1, D))],
        out_specs=[_full_spec((8, 128)), _row_spec(R, D, 0), _full_spec((1, D))],
        out_shape=[jax.ShapeDtypeStruct((8, 128), F32), jax.ShapeDtypeStruct((n, D), F32), jax.ShapeDtypeStruct((1, D), F32)],
        compiler_params=_ARB1,
    )(x, target, gf)


def _adamw(name, w, m, v, gparts):
    rows, cols = w.shape
    nparts = gparts.shape[0]
    tr = rows
    for cand in range(min(rows, 256), 15, -16):
        if rows % cand == 0:
            tr = cand
            break

    def body(w_ref, m_ref, v_ref, g_ref, go_ref, d_ref, mo_ref, vo_ref):
        g = g_ref[0].astype(F32)
        for k in range(1, nparts):
            g = g + g_ref[k].astype(F32)
        m_new = ADAM_B1 * m_ref[...] + (1.0 - ADAM_B1) * g
        v_new = ADAM_B2 * v_ref[...] + (1.0 - ADAM_B2) * (g * g)
        m_hat = m_new / (1.0 - ADAM_B1 ** ADAM_STEP)
        v_hat = v_new / (1.0 - ADAM_B2 ** ADAM_STEP)
        go_ref[...] = g
        d_ref[...] = -ADAM_LR * (m_hat / (jnp.sqrt(v_hat) + ADAM_EPS) + ADAM_WD * w_ref[...])
        mo_ref[...] = m_new
        vo_ref[...] = v_new

    spec = pl.BlockSpec((tr, cols), lambda i: (i, 0))
    return pl.pallas_call(
        body, name=name, grid=(rows // tr,),
        in_specs=[spec, spec, spec, pl.BlockSpec((nparts, tr, cols), lambda i: (0, i, 0))],
        out_specs=[spec] * 4, out_shape=[jax.ShapeDtypeStruct((rows, cols), F32)] * 4,
        compiler_params=_ARB1,
    )(w, m, v, gparts)


def _my_index():
    return 4 * lax.axis_index("x") + 2 * lax.axis_index("y") + lax.axis_index("c")


def _xor_peer(k):
    flip = lambda a, bit: (1 - a) if bit else a
    pos = (flip(lax.axis_index("x"), (k >> 2) & 1), flip(lax.axis_index("y"), (k >> 1) & 1), flip(lax.axis_index("c"), k & 1))
    return pos, 4 * pos[0] + 2 * pos[1] + pos[2]


def _exchange_plan(x_refs, o_refs, send_sems, recv_sems, local_sems, gather):
    npeer = N_DEV - 1
    me = _my_index()
    locals_, sends, recvs = [], [], []
    for a, (x_ref, o_ref) in enumerate(zip(x_refs, o_refs)):
        mine = x_ref if gather else x_ref.at[me]
        locals_.append(pltpu.make_async_copy(mine, o_ref.at[me], local_sems.at[a]))
        for k in range(1, N_DEV):
            pos, lin = _xor_peer(k)
            src = x_ref if gather else x_ref.at[lin]
            sem = a * npeer + k - 1
            sends.append(pltpu.make_async_remote_copy(src_ref=src, dst_ref=o_ref.at[me], send_sem=send_sems.at[sem],
                                                      recv_sem=recv_sems.at[sem], device_id=pos, device_id_type=MESH_ID))
            recvs.append(pltpu.make_async_remote_copy(src_ref=src, dst_ref=o_ref.at[lin], send_sem=send_sems.at[sem],
                                                      recv_sem=recv_sems.at[sem], device_id=pos, device_id_type=MESH_ID))

    def start():
        for cp in locals_ + sends:
            cp.start()

    def wait():
        for cp in recvs:
            cp.wait_recv()
        for cp in sends:
            cp.wait_send()
        for cp in locals_:
            cp.wait()

    return start, wait


def _exchange_io(xs, gather):
    na = len(xs)
    hbm = pl.BlockSpec(memory_space=pltpu.HBM)
    shapes = [jax.ShapeDtypeStruct((N_DEV,) + tuple(x.shape if gather else x.shape[1:]), x.dtype) for x in xs]
    sems = [pltpu.SemaphoreType.DMA((na * (N_DEV - 1),)), pltpu.SemaphoreType.DMA((na * (N_DEV - 1),)),
            pltpu.SemaphoreType.DMA((na,))]
    return [hbm] * na, shapes, sems


def _exchange(name, xs, *, gather):
    na = len(xs)
    specs, shapes, sems = _exchange_io(xs, gather)

    def body(*refs):
        start, wait = _exchange_plan(refs[:na], refs[na:2 * na], *refs[2 * na:], gather)
        start()
        wait()

    return list(pl.pallas_call(body, name=name, in_specs=specs, out_specs=specs, out_shape=shapes, scratch_shapes=sems)(*xs))


def _adaln_fwd(name, craw16, w_ada, b_cols):
    def body(c_ref, w_ref, b_ref, o_ref):
        cs = jax.nn.silu(c_ref[...]).astype(BF16)
        for l in range(2):
            o_ref[l] = lax.dot_general(cs, w_ref[l].astype(BF16), _NN, preferred_element_type=F32) + b_ref[l]

    return pl.pallas_call(
        body, name=name, out_shape=jax.ShapeDtypeStruct((2, 16, w_ada.shape[2]), F32),
        compiler_params=pltpu.CompilerParams(vmem_limit_bytes=V7X_VMEM_LIMIT),
    )(craw16, w_ada, b_cols)


def _adaln_bwd(name, craw16, w_ada, dm):
    def body(c_ref, w_ref, dm_ref, gw_ref, dc_ref):
        c = c_ref[...]
        sg = jax.nn.sigmoid(c)
        cs = c * sg
        row = lax.broadcasted_iota(jnp.int32, (8, 1), 0)
        dc = jnp.zeros((16, D), F32)
        for l in range(2):
            dmx = dm_ref[2 * l + 1]
            dmc = jnp.where(row == 0, jnp.sum(dm_ref[2 * l], axis=0, keepdims=True), 0.0)
            gw_ref[l] = _hdot(cs[0:8], dmx, _TN) + _hdot(cs[8:16], dmc, _TN)
            dc = dc + _hdot(jnp.concatenate([dmx, dmc], axis=0), w_ref[l], _NT)
        dc_ref[...] = dc * sg * (1.0 + c * (1.0 - sg))

    return pl.pallas_call(
        body, name=name,
        out_shape=[jax.ShapeDtypeStruct(w_ada.shape, F32), jax.ShapeDtypeStruct((16, D), F32)],
        compiler_params=pltpu.CompilerParams(vmem_limit_bytes=V7X_VMEM_LIMIT),
    )(craw16, w_ada, dm)


_IN_OFFS = [sum(IN_SPLITS[:k]) for k in range(len(IN_SPLITS) + 1)]
_MY_ORDER = (11, 12, 13, 0, 1, 2, 7, 10, 5, 6, 3, 4, 8, 9)


_IN_SHARD = IN_WIDTH // N_DEV


def _win_my_cols(pieces):
    parts = []
    for k in _MY_ORDER:
        a, b = _IN_OFFS[k], _IN_OFFS[k + 1]
        for s in range(a // _IN_SHARD, (b - 1) // _IN_SHARD + 1):
            lo, hi = max(a, s * _IN_SHARD), min(b, (s + 1) * _IN_SHARD)
            parts.append(pieces[s][:, lo - s * _IN_SHARD:hi - s * _IN_SHARD])
    parts.append(jnp.zeros((pieces[0].shape[0], PW - IN_WIDTH), pieces[0].dtype))
    return jnp.concatenate(parts, axis=1)


def _win_shards(wp):
    my_offs, pos = {}, 0
    for k in _MY_ORDER:
        my_offs[k] = pos
        pos += IN_SPLITS[k]
    shards = []
    for s in range(N_DEV):
        parts = []
        for k in range(len(IN_SPLITS)):
            lo, hi = max(_IN_OFFS[k], s * _IN_SHARD), min(_IN_OFFS[k + 1], (s + 1) * _IN_SHARD)
            if lo < hi:
                parts.append(wp[:, my_offs[k] + lo - _IN_OFFS[k]:my_offs[k] + hi - _IN_OFFS[k]])
        shards.append(jnp.concatenate(parts, axis=1))
    return jnp.stack(shards)


def _split_shards(full, axis):
    c = full.shape[axis] // N_DEV
    return jnp.stack([lax.slice_in_dim(full, s * c, (s + 1) * c, axis=axis) for s in range(N_DEV)])


def _pack_rows(pieces, row_mult):
    rows = jnp.concatenate([p.reshape(-1, 128) for p in pieces], axis=0)
    padn = (-rows.shape[0]) % row_mult
    if padn:
        rows = jnp.concatenate([rows, jnp.zeros((padn, 128), rows.dtype)], axis=0)
    return rows


def _unpack_rows(rows, shapes):
    out, pos = [], 0
    for s in shapes:
        size = 1
        for d in s:
            size *= d
        out.append(rows[pos:pos + size // 128].reshape(tuple(s)))
        pos += size // 128
    return out


def _heads_front(a, nh):
    return a.reshape(a.shape[0], nh, HEAD_DIM).transpose(1, 0, 2)


def _heads_back(a):
    return a.transpose(1, 0, 2).reshape(a.shape[1], a.shape[0] * HEAD_DIM)


def _rope_tables(t, tc):
    tok = jnp.arange(t, dtype=jnp.int32)
    inv_freq = ROPE_THETA ** (-jnp.arange(ROPE_FREQS, dtype=F32) / ROPE_FREQS)
    ang_r = (tok // GRID_W).astype(F32)[:, None] * inv_freq
    ang_c = (tok % GRID_W).astype(F32)[:, None] * inv_freq
    cos64 = jnp.concatenate([jnp.cos(ang_r), jnp.cos(ang_r), jnp.cos(ang_c), jnp.cos(ang_c)], axis=1)
    sin64 = jnp.concatenate([-jnp.sin(ang_r), jnp.sin(ang_r), -jnp.sin(ang_c), jnp.sin(ang_c)], axis=1)
    cos64 = jnp.concatenate([jnp.ones((tc, HEAD_DIM), F32), cos64], axis=0)
    sin64 = jnp.concatenate([jnp.zeros((tc, HEAD_DIM), F32), sin64], axis=0)
    return jnp.tile(cos64, (1, N_Q_HEADS)), jnp.tile(sin64, (1, N_Q_HEADS))


def _head_mean_matrix(width):
    i = jnp.arange(width) // HEAD_DIM
    return (i[:, None] == i[None, :]).astype(F32) / HEAD_DIM


def _head_tile_matrix(width):
    return (jnp.arange(HEAD_DIM)[:, None] == (jnp.arange(width) % HEAD_DIM)[None, :]).astype(F32)


def _heads_t(a, nh):
    return a.T.reshape(nh, HEAD_DIM, a.shape[0])


def _heads_t_back(a):
    return a.reshape(a.shape[0] * HEAD_DIM, a.shape[2]).T


def _attention_fwd(tag, qr, kr, vv, tc, comm):
    qt8, k2, vt2 = _heads_t(qr, N_Q_HEADS), _heads_front(kr, N_KV_HEADS), _heads_t(vv, N_KV_HEADS)
    o_c, lse_c, _ = _attn_fwd(tag + "_attn_ctx", qt8[:, :, :tc], k2[:, :tc], vt2[:, :, :tc])
    o_x, lse_x, comm_out = _attn_fwd(tag + "_attn_lat", qt8[:, :, tc:], k2, vt2, comm)
    ot8 = jnp.concatenate([o_c, o_x], axis=2)
    lse = jnp.concatenate([lse_c, lse_x], axis=2)
    return _heads_t_back(ot8), (qr, kr, vv, ot8, lse), comm_out


def _attention_bwd(tag, saved, datt, tc, comm):
    qr, kr, vv, ot8, lse = saved
    datt = datt.astype(BF16)
    qt8, q8, dot8, do8 = _heads_t(qr, N_Q_HEADS), _heads_front(qr, N_Q_HEADS), _heads_t(datt, N_Q_HEADS), _heads_front(datt, N_Q_HEADS)
    k2, v2, kt2 = _heads_front(kr, N_KV_HEADS), _heads_front(vv, N_KV_HEADS), _heads_t(kr, N_KV_HEADS)
    dq_c, dk_c, dv_c, _ = _attn_bwd(tag + "_attn_b_ctx", qt8[:, :, :tc], q8[:, :tc], k2[:, :tc], v2[:, :tc], kt2[:, :, :tc],
                                    ot8[:, :, :tc], dot8[:, :, :tc], do8[:, :tc], lse[:, :, :tc])
    dq_x, dk_x, dv_x, comm_out = _attn_bwd(tag + "_attn_b_lat", qt8[:, :, tc:], q8[:, tc:], k2, v2, kt2,
                                           ot8[:, :, tc:], dot8[:, :, tc:], do8[:, tc:], lse[:, :, tc:], comm)
    dqt8 = jnp.concatenate([dq_c, dq_x], axis=2)
    dk2 = dk_x.at[:, :tc].add(dk_c)
    dv2 = dv_x.at[:, :tc].add(dv_c)
    return _heads_t_back(dqt8), _heads_back(dk2), _heads_back(dv2), comm_out


def _layer_fwd(tag, x, w, modv, consts, R, nct, tc, comm=None):
    sh1, sc1, g1, sh2, sc2, g2 = modv
    cosq, sinq, bdq, eq, bdk, ek = consts
    n = x.shape[0]
    (h1,) = _rowwise(tag + "_ln1", _f_lnmod, R, nct, [(x, D, 0, D)], [], [(w["norm1_g"], False)], [], [sh1, sc1], [(D, BF16)])
    p = _matmul(tag + "_in", h1, w["w_in"], "nn", F32)
    qk_rows = [(p, Q_W, C_Q // Q_W, Q_W), (p, KV_W, C_K // KV_W, KV_W)]
    qk_consts = [(cosq, Q_W, 0, Q_W), (sinq, Q_W, 0, Q_W), (cosq, KV_W, 0, KV_W), (sinq, KV_W, 0, KV_W)]
    qk_params = [(w["q_norm_g"], False), (w["k_norm_g"], False)]
    qk_cparams = [(bdq, False), (eq, False), (bdk, False), (ek, False)]
    qr, kr = _rowwise(tag + "_qk", _f_qknorm, R, nct, qk_rows, qk_consts, qk_params, qk_cparams, [],
                      [(Q_W, BF16), (KV_W, BF16)], post=_qk_post)
    vv = p[:, C_VV:C_VV + KV_W].astype(BF16)
    att, att_saved, comm_out = _attention_fwd(tag, qr, kr, vv, tc, comm)

    o_f, s_f = _gla_dir(tag + "_gla_f", p, w["w2p_f"], w["b2_f"], rev=False, R=R, nct=nct)
    o_fb, s_b = _gla_dir(tag + "_gla_b", p, w["w2p_b"], w["b2_b"], rev=True, R=R, nct=nct, add=o_f)
    go_rows = [(o_fb, GLA_V_W, 0, GLA_DV), (p, GLA_V_W, C_R // GLA_V_W, GLA_DV)]
    (gla,) = _rowwise(tag + "_glaout", _f_glaout, R, nct, go_rows, [], [(w["gla_norm_g"], True)], [], [], [(GLA_V_W, BF16)])

    rg = GMLP_CHUNK
    gm_rows = [(p, GMLP_W, C_U // GMLP_W, GMLP_W), (p, GMLP_W, C_V // GMLP_W, GMLP_W // GMLP_GROUPS)]
    gm_params = [(w["gmlp_norm_g"], True), (w["w_spatial"], True), (w["b_spatial_t"], True)]
    (gm,) = _rowwise(tag + "_gmlp", _f_gmlp, rg, tc // rg, gm_rows, [], gm_params, [], [], [(GMLP_W, BF16)])

    ya = _matmul(tag + "_br_a", gm, w["w_br_a"], "nn", F32)
    yb = _matmul(tag + "_br_b", att, w["w_br_b"], "nn", F32)
    yc = _matmul(tag + "_br_c", gla, w["w_br_c"], "nn", F32)
    mg_rows = [(p, D, C_GA // D, D), (p, D, C_GB // D, D), (p, D, C_GC // D, D), (ya, D, 0, D), (yb, D, 0, D), (yc, D, 0, D)]
    (merged,) = _rowwise(tag + "_merge", _f_merge, R, nct, mg_rows, [], [], [], [], [(D, BF16)])
    mix = _matmul(tag + "_out", merged, w["w_out"], "nn", F32)
    (x_mid,) = _rowwise(tag + "_res1", _f_resid, R, nct, [(x, D, 0, D), (mix, D, 0, D)], [], [], [], [g1], [(D, F32)])

    (h2,) = _rowwise(tag + "_ln2", _f_lnmod, R, nct, [(x_mid, D, 0, D)], [], [(w["norm2_g"], False)], [], [sh2, sc2], [(D, BF16)])
    a2 = _matmul(tag + "_up", h2, w["w_ffn_up"], "nn", F32, o_halves=True)
    gv = _conv_gate(tag + "_conv", a2, w["conv_w_h"], w["conv_b_h"], R, tc)
    ffn = _matmul(tag + "_down", gv, w["w_ffn_down"], "nn", F32)
    (x_next,) = _rowwise(tag + "_res2", _f_resid, R, nct, [(x_mid, D, 0, D), (ffn, D, 0, D)], [], [], [], [g2], [(D, F32)])
    saved = dict(x=x, h1=h1, p=p, att_saved=att_saved, att=att, o_fb=o_fb, s_f=s_f, s_b=s_b, gla=gla, gm=gm,
                 ya=ya, yb=yb, yc=yc, merged=merged, mix=mix, x_mid=x_mid, h2=h2, a2=a2, gv=gv, ffn=ffn,
                 qk=(qk_rows, qk_consts, qk_params, qk_cparams), go_rows=go_rows, gm_info=(gm_rows, gm_params),
                 mg_rows=mg_rows)
    return x_next, saved, comm_out


def _layer_bwd(tag, dx_next, s, w, modv, R, nct, tc, comm=None):
    sh1, sc1, g1, sh2, sc2, g2 = modv
    gw = {}
    (dffn,), _, (dg2,) = _rowwise_bwd(tag + "_res2_b", _f_resid, R, nct, [(s["ffn"], D, 0, D), (s["ffn"], D, 0, D)], [], [], [], [g2],
                                      [(dx_next, D)], [None, BF16])
    dgv = _matmul(tag + "_down_da", dffn, w["w_ffn_down"], "nt", F32)
    gw["w_ffn_down"] = _matmul(tag + "_down_dw", s["gv"], dffn, "tn", F32)
    da2, dcw, dcb = _conv_gate_bwd(tag + "_conv_b", s["a2"], dgv, w["conv_w_h"], w["conv_b_h"], R, tc)
    gw["conv_w_h"], gw["conv_b_h"] = dcw, dcb
    dh2 = _matmul(tag + "_up_da", da2, w["w_ffn_up"], "nt", F32, a_halves=True)
    gw["w_ffn_up"] = _matmul(tag + "_up_dw", s["h2"], da2, "tn", F32, b_halves=True)
    (dx_mid,), (gw["norm2_g"],), (dsh2, dsc2) = _rowwise_bwd(
        tag + "_ln2_b", _f_lnmod, R, nct, [(s["x_mid"], D, 0, D)], [], [(w["norm2_g"], False)], [], [sh2, sc2],
        [(dh2, D)], [F32], adds=[dx_next])
    (dmix,), _, (dg1,) = _rowwise_bwd(tag + "_res1_b", _f_resid, R, nct, [(s["mix"], D, 0, D), (s["mix"], D, 0, D)], [], [], [], [g1],
                                      [(dx_mid, D)], [None, BF16])
    dmerged = _matmul(tag + "_out_da", dmix, w["w_out"], "nt", F32)
    gw["w_out"] = _matmul(tag + "_out_dw", s["merged"], dmix, "tn", F32)
    (dga, dgb, dgc, dya, dyb, dyc), _, _ = _rowwise_bwd(tag + "_merge_b", _f_merge, R, nct, s["mg_rows"], [], [], [], [],
                                                        [(dmerged, D)], [BF16] * 6)
    dgm = _matmul(tag + "_br_a_da", dya, w["w_br_a"], "nt", F32)
    datt = _matmul(tag + "_br_b_da", dyb, w["w_br_b"], "nt", F32)
    dgla = _matmul(tag + "_br_c_da", dyc, w["w_br_c"], "nt", F32)
    gm_rows, gm_params = s["gm_info"]
    gw["w_br_a"] = _matmul(tag + "_br_a_dw", s["gm"], dya, "tn", F32)
    gw["w_br_b"] = _matmul(tag + "_br_b_dw", s["att"], dyb, "tn", F32)
    gw["w_br_c"] = _matmul(tag + "_br_c_dw", s["gla"], dyc, "tn", F32)
    rg = GMLP_CHUNK
    (du, dv_), (gw["gmlp_norm_g"], gw["w_spatial"], gw["b_spatial_t"]), _ = _rowwise_bwd(
        tag + "_gmlp_b", _f_gmlp, rg, tc // rg, gm_rows, [], gm_params, [], [], [(dgm, GMLP_W)], [BF16, BF16])
    (do, dr), (gw["gla_norm_g"],), _ = _rowwise_bwd(tag + "_glaout_b", _f_glaout, R, nct, s["go_rows"], [],
                                                    [(w["gla_norm_g"], True)], [], [], [(dgla, GLA_V_W)], [F32, BF16])
    p = s["p"]
    *d_b, gw["w2p_b"], gw["b2_b"] = _gla_dir_bwd(tag + "_gla_b_b", p, w["w2p_b"], w["b2_b"], s["s_b"], do, rev=True, R=R, nct=nct)
    dgq, dgk, dgv_, daf, gw["w2p_f"], gw["b2_f"] = _gla_dir_bwd(tag + "_gla_f_b", p, w["w2p_f"], w["b2_f"], s["s_f"], do,
                                                              rev=False, R=R, nct=nct, adds=d_b, out_dtype=BF16)
    dqr, dkr, dvv, comm_out = _attention_bwd(tag, s["att_saved"], datt, tc, comm)
    qk_rows, qk_consts, qk_params, qk_cparams = s["qk"]
    (dq, dk), (gw["q_norm_g"], gw["k_norm_g"]), _ = _rowwise_bwd(
        tag + "_qk_b", _f_qknorm, R, nct, qk_rows, qk_consts, qk_params, qk_cparams, [],
        [(dqr, Q_W), (dkr, KV_W)], [BF16, BF16], pre=_qk_pre)
    dp = jnp.concatenate([dga, dgb, dgc, du, dv_, dq, dgv_, dr, dgq, dgk, dk, dvv.astype(BF16), daf,
                          jnp.zeros((p.shape[0], PW - C_AF - 128), BF16)], axis=1)
    dh1 = _matmul(tag + "_in_da", dp, w["w_in"], "nt", F32)
    gw["w_in"] = _matmul(tag + "_in_dw", s["h1"], dp, "tn", F32)
    (dx,), (gw["norm1_g"],), (dsh1, dsc1) = _rowwise_bwd(
        tag + "_ln1_b", _f_lnmod, R, nct, [(s["x"], D, 0, D)], [], [(w["norm1_g"], False)], [], [sh1, sc1],
        [(dh1, D)], [F32], adds=[dx_mid])
    return dx, gw, (dsh1, dsc1, dg1, dsh2, dsc2, dg2), comm_out


_SHARDED = (("w_in", 1, True), ("w_br_a", 1, True), ("w_br_b", 1, True), ("w_br_c", 1, True), ("w_out", 0, True),
            ("w_ffn_up", 1, True), ("w_ffn_down", 0, True), ("conv_w", 1, False), ("w_alpha2", 2, False), ("b_alpha", 1, False))
_REPLICATED = ("c_ctx", "b_ada", "norm1_g", "norm2_g", "q_norm_g", "k_norm_g", "gmlp_norm_g", "w_spatial", "b_spatial",
               "gla_norm_g", "conv_b", "final_norm_g")
_WEIGHTS = ("c_ctx", "w_ada", "b_ada", "norm1_g", "norm2_g", "w_in", "q_norm_g", "k_norm_g", "gmlp_norm_g", "w_spatial",
            "b_spatial", "w_alpha2", "b_alpha", "gla_norm_g", "w_br_a", "w_br_b", "w_br_c", "w_out", "w_ffn_up", "conv_w",
            "conv_b", "w_ffn_down", "final_norm_g")


def _decay_weights(w_alpha2_l, b_alpha_l):
    out = []
    for d in range(2):
        w2p = jnp.zeros((128, GLA_QK_W), F32).at[GLA_RANK * d:GLA_RANK * (d + 1)].set(w_alpha2_l[d])
        out += [w2p, b_alpha_l[d][None, :]]
    return out


def _step(inp, wts, moms, vels):
    x, c, ctx, loss_target = inp
    t, tc = x.shape[1], ctx.shape[1]
    n = t + tc
    R = min(256, tc)
    nct = tc // R
    me = _my_index()
    depth = wts["w_in"].shape[0]

    big = [(nm, ax) for nm, ax, half in _SHARDED if half]
    small = [(nm, ax) for nm, ax, half in _SHARDED if not half]
    big_shards = lambda l: [wts[nm][l].astype(BF16) for nm, _ in big]
    c8 = jnp.concatenate([c, jnp.zeros((7, D), F32)], axis=0)
    first = _exchange("gather_first", big_shards(0) + [wts[nm] for nm, _ in small] + [c8], gather=True)
    c_all = first[-1][:, 0, :]
    small_all = dict(zip([nm for nm, _ in small], first[len(big):-1]))

    def layer_weights(l, big_all):
        w = {}
        for (nm, ax), g in zip(big, big_all):
            pieces = [g[s] for s in range(N_DEV)]
            w[nm] = _win_my_cols(pieces) if nm == "w_in" else jnp.concatenate(pieces, axis=ax)
        conv_w, w_alpha2, b_alpha = [jnp.concatenate([small_all[nm][s, l] for s in range(N_DEV)], axis=ax) for nm, ax in small]
        w["conv_w_h"] = conv_w.reshape(3, 2, FFN_H).transpose(1, 0, 2)
        w["conv_b_h"] = wts["conv_b"][l].reshape(2, 1, FFN_H)
        w["w2p_f"], w["b2_f"], w["w2p_b"], w["b2_b"] = _decay_weights(w_alpha2, b_alpha)
        w["norm1_g"] = wts["norm1_g"][l][None, :]
        w["norm2_g"] = wts["norm2_g"][l][None, :]
        w["q_norm_g"] = wts["q_norm_g"][l][None, :]
        w["k_norm_g"] = wts["k_norm_g"][l][None, :]
        w["gmlp_norm_g"] = wts["gmlp_norm_g"][l].reshape(GMLP_GROUPS, 1, GMLP_W // GMLP_GROUPS)
        w["w_spatial"] = wts["w_spatial"][l]
        w["b_spatial_t"] = wts["b_spatial"][l][:, :, None]
        w["gla_norm_g"] = wts["gla_norm_g"][l].reshape(GLA_HEADS, 1, GLA_DV)
        return w

    craw16 =jnp.concatenate([c_all, wts["c_ctx"][None, :], jnp.zeros((7, D), F32)], axis=0)
    acols = wts["w_ada"].shape[2]
    b_cols = lax.dynamic_slice_in_dim(wts["b_ada"], me * acols, acols, axis=1)[:, None, :]
    mod_part = _adaln_fwd("adaln", craw16, wts["w_ada"], b_cols)
    send = jnp.stack([mod_part[:, 8, :][None].repeat(N_DEV, 0), mod_part[:, :8, :].transpose(1, 0, 2)], axis=2)
    send = jnp.concatenate([send.reshape(N_DEV, 2 * depth, acols), jnp.zeros((N_DEV, 8 - 2 * depth, acols), F32)], axis=1)
    (got,) = _exchange("scatter_mod", [send], gather=False)
    mod = got[:, :2 * depth, :].transpose(1, 0, 2).reshape(depth, 2, N_MOD, 1, D)
    modv = [[mod[l, :, k] for k in range(N_MOD)] for l in range(depth)]

    cosq, sinq = _rope_tables(t, tc)
    consts = (cosq, sinq, _head_mean_matrix(Q_W), _head_tile_matrix(Q_W), _head_mean_matrix(KV_W), _head_tile_matrix(KV_W))
    xs = jnp.concatenate([ctx[0], x[0]], axis=0)
    saved, layers = [], []
    big_all = first[:len(big)]
    for l in range(depth):
        layers.append(layer_weights(l, big_all))
        comm = (big_shards(l + 1), True) if l + 1 < depth else None
        xs, sv, big_all = _layer_fwd("l%d" % l, xs, layers[l], modv[l], consts, R, nct, tc, comm)
        saved.append(sv)
    loss_blk, dxs, dgf = _final_loss("final", xs, loss_target[0], wts["final_norm_g"][None, :], R, nct)
    loss = lax.psum(loss_blk[0, 0], ("x", "y", "c"))

    def big_grad_shards(g):
        return [(_win_shards(g[nm]) if nm == "w_in" else _split_shards(g[nm], ax)).astype(BF16) for nm, ax in big]

    grads = [None] * depth
    dmods = [None] * depth
    big_parts = [None] * depth
    for l in range(depth - 1, -1, -1):
        comm = (big_grad_shards(grads[l + 1]), False) if l + 1 < depth else None
        dxs, grads[l], dmods[l], got = _layer_bwd("l%d" % l, dxs, saved[l], layers[l], modv[l], R, nct, tc, comm)
        if comm is not None:
            big_parts[l + 1] = got
    grad_x = dxs[tc:][None]

    dmod = jnp.stack([jnp.stack(dmods[l], axis=1) for l in range(depth)])
    dmod = dmod.reshape(depth, 2, N_DEV, acols).transpose(2, 0, 1, 3).reshape(N_DEV, 2 * depth, acols)
    dmod_send = jnp.concatenate([dmod, jnp.zeros((N_DEV, 8 - 2 * depth, acols), F32)], axis=1)
    (dm_got,) = _exchange("scatter_dmod", [dmod_send], gather=False)
    g_w_ada, dc16 = _adaln_bwd("adaln_b", craw16, wts["w_ada"], dm_got[:, :2 * depth].transpose(1, 0, 2))
    db_ada_part = jnp.stack([jnp.stack(dmods[l], axis=1) for l in range(depth)]).reshape(depth, 2, N_MOD * D).sum(axis=1)

    out = {}
    kinds = ("grad", "delta", "new_m", "new_v")
    small_send = {nm: [] for nm, _ in small}
    for l in range(depth):
        g = grads[l]
        small_g = dict(conv_w=g["conv_w_h"].transpose(1, 0, 2).reshape(3, F2),
                       w_alpha2=jnp.stack([g["w2p_f"][:GLA_RANK], g["w2p_b"][GLA_RANK:2 * GLA_RANK]]),
                       b_alpha=jnp.stack([g["b2_f"][0], g["b2_b"][0]]))
        for nm, ax in small:
            small_send[nm].append(_split_shards(small_g[nm], ax))
    last = _exchange("scatter_last", big_grad_shards(grads[0]) + [jnp.stack(small_send[nm], axis=1) for nm, _ in small],
                     gather=False)
    big_parts[0] = last[:len(big)]
    view2 = lambda a: a.reshape(-1, a.shape[-1])
    sharded_parts = [jnp.stack([big_parts[l][k] for l in range(depth)], axis=1) for k in range(len(big))] + last[len(big):]
    for (nm, _), parts in zip(big + small, sharded_parts):
        res = _adamw("adamw_" + nm, view2(wts[nm]), view2(moms[nm]), view2(vels[nm]), parts.reshape(N_DEV, -1, parts.shape[-1]))
        for kind, flat in zip(kinds, res):
            out[kind, nm] = flat.reshape(wts[nm].shape)

    rep_g = dict(
        c_ctx=dc16[8], b_ada=db_ada_part, final_norm_g=dgf[0],
        norm1_g=jnp.stack([grads[l]["norm1_g"][0] for l in range(depth)]),
        norm2_g=jnp.stack([grads[l]["norm2_g"][0] for l in range(depth)]),
        q_norm_g=jnp.stack([grads[l]["q_norm_g"][0] for l in range(depth)]),
        k_norm_g=jnp.stack([grads[l]["k_norm_g"][0] for l in range(depth)]),
        gmlp_norm_g=jnp.stack([grads[l]["gmlp_norm_g"].reshape(GMLP_W) for l in range(depth)]),
        w_spatial=jnp.stack([grads[l]["w_spatial"] for l in range(depth)]),
        b_spatial=jnp.stack([grads[l]["b_spatial_t"][:, :, 0] for l in range(depth)]),
        gla_norm_g=jnp.stack([grads[l]["gla_norm_g"].reshape(GLA_V_W) for l in range(depth)]),
        conv_b=jnp.stack([grads[l]["conv_b_h"].reshape(F2) for l in range(depth)]),
    )
    rep_shapes = [wts[nm].shape for nm in _REPLICATED]
    (rg_parts,) = _exchange("gather_rep_grads", [_pack_rows([rep_g[nm] for nm in _REPLICATED], 16)], gather=True)
    rpk = lambda src: _pack_rows([src[nm] for nm in _REPLICATED], 16)
    res = _adamw("adamw_rep", rpk(wts), rpk(moms), rpk(vels), rg_parts)
    for kind, rows in zip(kinds, res):
        for nm, piece in zip(_REPLICATED, _unpack_rows(rows, rep_shapes)):
            out[kind, nm] = piece

    res = _adamw("adamw_ada", view2(wts["w_ada"]), view2(moms["w_ada"]), view2(vels["w_ada"]), view2(g_w_ada)[None])
    for kind, flat in zip(kinds, res):
        out[kind, "w_ada"] = flat.reshape(wts["w_ada"].shape)

    return (loss, grad_x, *[out[kind, nm] for kind in kinds for nm in _WEIGHTS])


def kernel(x, c, ctx, c_ctx, w_ada, b_ada, norm1_g, norm2_g, w_in, q_norm_g, k_norm_g, gmlp_norm_g, w_spatial, b_spatial, w_alpha2, b_alpha, gla_norm_g, w_br_a, w_br_b, w_br_c, w_out, w_ffn_up, conv_w, conv_b, w_ffn_down, final_norm_g, loss_target, m_c_ctx, m_w_ada, m_b_ada, m_norm1_g, m_norm2_g, m_w_in, m_q_norm_g, m_k_norm_g, m_gmlp_norm_g, m_w_spatial, m_b_spatial, m_w_alpha2, m_b_alpha, m_gla_norm_g, m_w_br_a, m_w_br_b, m_w_br_c, m_w_out, m_w_ffn_up, m_conv_w, m_conv_b, m_w_ffn_down, m_final_norm_g, v_c_ctx, v_w_ada, v_b_ada, v_norm1_g, v_norm2_g, v_w_in, v_q_norm_g, v_k_norm_g, v_gmlp_norm_g, v_w_spatial, v_b_spatial, v_w_alpha2, v_b_alpha, v_gla_norm_g, v_w_br_a, v_w_br_b, v_w_br_c, v_w_out, v_w_ffn_up, v_conv_w, v_conv_b, v_w_ffn_down, v_final_norm_g):
    wts = dict(zip(_WEIGHTS, (c_ctx, w_ada, b_ada, norm1_g, norm2_g, w_in, q_norm_g, k_norm_g, gmlp_norm_g, w_spatial, b_spatial,
                              w_alpha2, b_alpha, gla_norm_g, w_br_a, w_br_b, w_br_c, w_out, w_ffn_up, conv_w, conv_b, w_ffn_down,
                              final_norm_g)))
    moms = dict(zip(_WEIGHTS, (m_c_ctx, m_w_ada, m_b_ada, m_norm1_g, m_norm2_g, m_w_in, m_q_norm_g, m_k_norm_g, m_gmlp_norm_g,
                               m_w_spatial, m_b_spatial, m_w_alpha2, m_b_alpha, m_gla_norm_g, m_w_br_a, m_w_br_b, m_w_br_c, m_w_out,
                               m_w_ffn_up, m_conv_w, m_conv_b, m_w_ffn_down, m_final_norm_g)))
    vels = dict(zip(_WEIGHTS, (v_c_ctx, v_w_ada, v_b_ada, v_norm1_g, v_norm2_g, v_w_in, v_q_norm_g, v_k_norm_g, v_gmlp_norm_g,
                               v_w_spatial, v_b_spatial, v_w_alpha2, v_b_alpha, v_gla_norm_g, v_w_br_a, v_w_br_b, v_w_br_c, v_w_out,
                               v_w_ffn_up, v_conv_w, v_conv_b, v_w_ffn_down, v_final_norm_g)))
    return _step((x, c, ctx, loss_target), wts, moms, vels)
```

```python
import functools

import jax
import jax.numpy as jnp
from jax import lax
from jax.experimental import pallas as pl
from jax.experimental.pallas import tpu as pltpu

F32 = jnp.float32
BF16 = jnp.bfloat16
HI = lax.Precision.HIGHEST
MESH_ID = pl.DeviceIdType.MESH

N_DEV = 8
EPS = 1e-6
D = 1024
N_MOD = 6
HEAD_DIM = 64
N_Q_HEADS = 8
N_KV_HEADS = 2
Q_GROUP = 4
Q_W = 512
KV_W = 128
GRID_W = 64
ROPE_THETA = 10000.0
ROPE_FREQS = 16
GMLP_CHUNK = 128
GMLP_GROUPS = 4
GMLP_W = 512
GLA_HEADS = 4
GLA_QK_W = 256
GLA_V_W = 512
GLA_DK = 64
GLA_DV = 128
GLA_RANK = 16
GLA_TAU = 16.0
GLA_CHUNK = 64
FFN_H = 2816
F2 = 2 * FFN_H
IN_SPLITS = (512, 512, 512, 128, 128, 256, 256, 512, 16, 16, 512, 1024, 1024, 1024)
IN_WIDTH = sum(IN_SPLITS)

C_GA, C_GB, C_GC = 0, 1024, 2048
C_U, C_V, C_Q, C_GV, C_R = 3072, 3584, 4096, 4608, 5120
C_GQ, C_GK = 5632, 5888
C_K, C_VV, C_AF = 6144, 6272, 6400
PW = 6656

ADAM_LR = 0.001
ADAM_B1 = 0.9
ADAM_B2 = 0.999
ADAM_EPS = 1e-08
ADAM_WD = 0.01
ADAM_STEP = 10

V7X_VMEM_LIMIT = 56 * 1024 * 1024

_ARB1 = pltpu.CompilerParams(dimension_semantics=("arbitrary",), vmem_limit_bytes=V7X_VMEM_LIMIT)


def _pick(dim, prefs):
    for p in prefs:
        if dim % p == 0:
            return p
    return dim


def _hdot(a, b, dims=(((1,), (0,)), ((), ()))):
    return lax.dot_general(a, b, dims, precision=HI, preferred_element_type=F32)


_NT = (((1,), (1,)), ((), ()))
_TN = (((0,), (0,)), ((), ()))
_NN = (((1,), (0,)), ((), ()))


def _matmul(name, a, b, mode, out_dtype, *, a_halves=False, b_halves=False, o_halves=False):
    def dims2(x, halves):
        return (x.shape[1], 2 * x.shape[2]) if halves else x.shape

    ar, ac = dims2(a, a_halves)
    br, bc = dims2(b, b_halves)
    if mode == "nn":
        M, K, N = ar, ac, bc
    elif mode == "nt":
        M, K, N = ar, ac, br
    else:
        M, K, N = ac, ar, bc
    row_prefs = (768, 512, 384, 256, 128)
    n_unit = N // 2 if (o_halves or (b_halves and mode != "nt")) else N
    k_unit = K // 2 if (a_halves and mode != "tn") else K
    if mode == "tn":
        tm = _pick(M, (1024, 1408, 512, 256, 128))
        tk = _pick(K, row_prefs)
    else:
        tm = _pick(M, row_prefs)
        tk = _pick(k_unit, (1664, 1408, 1024, 512, 256, 128))
    tn = _pick(n_unit, (1664, 1408, 1024, 512, 256, 128))
    nk = K // tk

    def spec(shape2, halves, blk, imap):
        if not halves:
            return pl.BlockSpec(blk, imap)
        nhalf = (shape2[1] // 2) // blk[1]

        def im(i, j, k):
            r, c = imap(i, j, k)
            return (c // nhalf, r, c % nhalf)
        return pl.BlockSpec((None,) + blk, im)

    if mode == "nn":
        a_spec = spec((ar, ac), a_halves, (tm, tk), lambda i, j, k: (i, k))
        b_spec = spec((br, bc), b_halves, (tk, tn), lambda i, j, k: (k, j))
        dn = _NN
    elif mode == "nt":
        a_spec = spec((ar, ac), a_halves, (tm, tk), lambda i, j, k: (i, k))
        b_spec = spec((br, bc), b_halves, (tn, tk), lambda i, j, k: (j, k))
        dn = _NT
    else:
        a_spec = spec((ar, ac), a_halves, (tk, tm), lambda i, j, k: (k, i))
        b_spec = spec((br, bc), b_halves, (tk, tn), lambda i, j, k: (k, j))
        dn = _TN
    o_spec = spec((M, N), o_halves, (tm, tn), lambda i, j, k: (i, j))
    o_shape = (2, M, N // 2) if o_halves else (M, N)

    def body(a_ref, b_ref, o_ref, acc_ref):
        k = pl.program_id(2)
        part = lax.dot_general(a_ref[...], b_ref[...], dn, preferred_element_type=F32)
        if nk == 1:
            o_ref[...] = part.astype(o_ref.dtype)
        else:
            @pl.when(k == 0)
            def _():
                acc_ref[...] = part

            @pl.when(k > 0)
            def _():
                acc_ref[...] += part

            @pl.when(k == nk - 1)
            def _():
                o_ref[...] = acc_ref[...].astype(o_ref.dtype)

    return pl.pallas_call(
        body, name=name, grid=(M // tm, N // tn, nk),
        in_specs=[a_spec, b_spec], out_specs=o_spec,
        out_shape=jax.ShapeDtypeStruct(o_shape, out_dtype),
        scratch_shapes=[pltpu.VMEM((tm, tn), F32)],
        compiler_params=pltpu.CompilerParams(dimension_semantics=("parallel", "parallel", "arbitrary"),
                                             vmem_limit_bytes=V7X_VMEM_LIMIT),
    )(a, b)


def _full_spec(shape):
    nd = len(shape)
    return pl.BlockSpec(tuple(shape), lambda i, _nd=nd: (0,) * _nd)


def _row_spec(R, W, cb):
    return pl.BlockSpec((R, W), lambda i, _cb=cb: (i, _cb))


def _load_rows(refs, specs):
    vals = []
    for ref, (_, W, _, pw) in zip(refs, specs):
        if pw == W:
            vals.append(ref[...].astype(F32))
        else:
            vals.append([ref[:, k * pw:(k + 1) * pw].astype(F32) for k in range(W // pw)])
    return vals


def _load_params(refs, specs):
    vals = []
    for ref, (arr, split) in zip(refs, specs):
        if split:
            vals.append([ref[k] for k in range(arr.shape[0])])
        else:
            vals.append(ref[...])
    return vals


def _mod_spec(nct, width):
    return pl.BlockSpec((None, 1, width), lambda i: (jnp.minimum(i // nct, 1), 0, 0))


def _rowwise(name, f, R, nct, rows, consts, params, cparams, mods, outs, post=None):
    n = rows[0][0].shape[0]
    nr, nc, npar, ncp, nm = len(rows), len(consts), len(params), len(cparams), len(mods)

    def body(*refs):
        pos = 0
        rr = refs[pos:pos + nr]; pos += nr
        cr = refs[pos:pos + nc]; pos += nc
        pr = refs[pos:pos + npar]; pos += npar
        cpr = refs[pos:pos + ncp]; pos += ncp
        mr = refs[pos:pos + nm]; pos += nm
        orefs = refs[pos:]
        res = f(_load_rows(rr, rows), _load_params(pr, params), [m[...] for m in mr],
                _load_rows(cr, consts), _load_params(cpr, cparams))
        if post is not None:
            res = post(res, _load_rows(cr, consts))
        for o_ref, r in zip(orefs, res):
            o_ref[...] = r.astype(o_ref.dtype)

    in_specs = ([_row_spec(R, W, cb) for (_, W, cb, _) in rows + consts]
                + [_full_spec(a.shape) for (a, _) in params + cparams]
                + [_mod_spec(nct, m.shape[2]) for m in mods])
    args = [a for (a, _, _, _) in rows + consts] + [a for (a, _) in params + cparams] + list(mods)
    return pl.pallas_call(
        body, name=name, grid=(n // R,), in_specs=in_specs,
        out_specs=[_row_spec(R, w, 0) for (w, _) in outs],
        out_shape=[jax.ShapeDtypeStruct((n, w), dt) for (w, dt) in outs],
        compiler_params=_ARB1,
    )(*args)


def _rowwise_bwd(name, f, R, nct, rows, consts, params, cparams, mods, douts, drow, adds=None, pre=None):
    n = rows[0][0].shape[0]
    adds = adds or [None] * len(rows)
    nr, nc, npar, ncp, nm, nd = len(rows), len(consts), len(params), len(cparams), len(mods), len(douts)
    add_ix = [k for k in range(nr) if adds[k] is not None]
    out_ix = [k for k in range(nr) if drow[k] is not None]

    def body(*refs):
        i = pl.program_id(0)
        pos = 0
        rr = refs[pos:pos + nr]; pos += nr
        cr = refs[pos:pos + nc]; pos += nc
        pr = refs[pos:pos + npar]; pos += npar
        cpr = refs[pos:pos + ncp]; pos += ncp
        mr = refs[pos:pos + nm]; pos += nm
        dr = refs[pos:pos + nd]; pos += nd
        ar = refs[pos:pos + len(add_ix)]; pos += len(add_ix)
        drr = refs[pos:pos + len(out_ix)]; pos += len(out_ix)
        dpr = refs[pos:pos + npar]; pos += npar
        dmr = refs[pos:pos + nm]; pos += nm

        cv = _load_rows(cr, consts)
        cpv = _load_params(cpr, cparams)
        _, vjp = jax.vjp(lambda rv, pv, mv: f(rv, pv, mv, cv, cpv),
                         _load_rows(rr, rows), _load_params(pr, params), [m[...] for m in mr])
        dv = [d[...].astype(F32) for d in dr]
        if pre is not None:
            dv = pre(dv, cv)
        g_rows, g_params, g_mods = vjp(tuple(dv))

        for ref, k in zip(drr, out_ix):
            _, W, _, pw = rows[k]
            g = g_rows[k]
            extra = ar[add_ix.index(k)] if k in add_ix else None
            if pw == W:
                if extra is not None:
                    g = g + extra[...].astype(F32)
                ref[...] = g.astype(ref.dtype)
            else:
                for q in range(W // pw):
                    gq = g[q]
                    if extra is not None:
                        gq = gq + extra[:, q * pw:(q + 1) * pw].astype(F32)
                    ref[:, q * pw:(q + 1) * pw] = gq.astype(ref.dtype)

        @pl.when(i == 0)
        def _():
            for ref in dpr:
                ref[...] = jnp.zeros_like(ref)

        for ref, (arr, split), g in zip(dpr, params, g_params):
            if split:
                for k in range(arr.shape[0]):
                    ref[k] += g[k]
            else:
                ref[...] += g

        @pl.when((i == 0) | (i == nct))
        def _():
            for ref in dmr:
                ref[...] = jnp.zeros_like(ref)

        for ref, g in zip(dmr, g_mods):
            ref[...] += g

    in_specs = ([_row_spec(R, W, cb) for (_, W, cb, _) in rows + consts]
                + [_full_spec(a.shape) for (a, _) in params + cparams]
                + [_mod_spec(nct, m.shape[2]) for m in mods]
                + [_row_spec(R, W, 0) for (_, W) in douts]
                + [_row_spec(R, rows[k][1], 0) for k in add_ix])
    args = ([a for (a, _, _, _) in rows + consts] + [a for (a, _) in params + cparams] + list(mods)
            + [a for (a, _) in douts] + [adds[k] for k in add_ix])
    out_specs = ([_row_spec(R, rows[k][1], 0) for k in out_ix]
                 + [_full_spec(a.shape) for (a, _) in params]
                 + [_mod_spec(nct, m.shape[2]) for m in mods])
    out_shape = ([jax.ShapeDtypeStruct((n, rows[k][1]), drow[k]) for k in out_ix]
                 + [jax.ShapeDtypeStruct(a.shape, F32) for (a, _) in params]
                 + [jax.ShapeDtypeStruct(m.shape, F32) for m in mods])
    res = pl.pallas_call(
        body, name=name, grid=(n // R,), in_specs=in_specs, out_specs=out_specs, out_shape=out_shape,
        compiler_params=_ARB1,
    )(*args)
    no = len(out_ix)
    return list(res[:no]), list(res[no:no + npar]), list(res[no + npar:])


def _rms(x, g):
    return x * lax.rsqrt(jnp.mean(x * x, axis=-1, keepdims=True) + EPS) * g


def _f_lnmod(rv, pv, mv, cv, cpv):
    (x,), (g,), (shift, scale) = rv, pv, mv
    return (_rms(x, g) * (1.0 + scale) + shift,)


def _f_resid(rv, pv, mv, cv, cpv):
    (x, y), (gate,) = rv, mv
    return (x + gate * y,)


def _f_merge(rv, pv, mv, cv, cpv):
    ga, gb, gc, ya, yb, yc = rv
    return (_sigmoid(ga) * ya + _sigmoid(gb) * yb + _sigmoid(gc) * yc,)


def _f_qknorm(rv, pv, mv, cv, cpv):
    (q, k), (gq, gk), (bdq, eq, bdk, ek) = rv, pv, cpv
    qn = q * lax.rsqrt(_hdot(q * q, bdq) + EPS) * _hdot(gq, eq)
    kn = k * lax.rsqrt(_hdot(k * k, bdk) + EPS) * _hdot(gk, ek)
    return (qn, kn)


def _rope(x, cos, sin):
    w = x.shape[1]
    lane = lax.broadcasted_iota(jnp.int32, x.shape, 1)
    partner = jnp.where((lane & 31) < 16, pltpu.roll(x, w - 16, 1), pltpu.roll(x, 16, 1))
    return x * cos + partner * sin


def _qk_post(res, cv):
    (qn, kn), (cq, sq, ck, sk) = res, cv
    return (_rope(qn, cq, sq) * (HEAD_DIM ** -0.5), _rope(kn, ck, sk))


def _qk_pre(dv, cv):
    (dq, dk), (cq, sq, ck, sk) = dv, cv
    return (_rope(dq * (HEAD_DIM ** -0.5), cq, -sq), _rope(dk, ck, -sk))


def _f_gmlp(rv, pv, mv, cv, cpv):
    (u, vs), (ng, ws, bt) = rv, pv
    pieces = []
    for g in range(GMLP_GROUPS):
        vn = _rms(jax.nn.gelu(vs[g]), ng[g])
        pieces.append(_hdot(ws[g], vn) + bt[g])
    return (jax.nn.gelu(u) * jnp.concatenate(pieces, axis=1),)


def _f_glaout(rv, pv, mv, cv, cpv):
    (os_, rs), (gn,) = rv, pv
    pieces = [_rms(os_[h], gn[h]) * (rs[h] * _sigmoid(rs[h])) for h in range(GLA_HEADS)]
    return (jnp.concatenate(pieces, axis=1),)


_CONV_CB = 1408


def _sigmoid(x):
    return 0.5 * jnp.tanh(0.5 * x) + 0.5


def _halo_keep(i, R, tc, n):
    first, end = i * R, (i + 1) * R
    keep_prev = jnp.where((first == 0) | (first == tc), 0.0, 1.0)
    keep_next = jnp.where((end == tc) | (end == n), 0.0, 1.0)
    return keep_prev, keep_next


def _conv_specs(R, n):
    nb8 = n // 8
    main = pl.BlockSpec((2, R, _CONV_CB), lambda j, i: (0, i, j))
    prev = pl.BlockSpec((2, 8, _CONV_CB), lambda j, i: (0, jnp.maximum(i * (R // 8) - 1, 0), j))
    nxt = pl.BlockSpec((2, 8, _CONV_CB), lambda j, i: (0, jnp.minimum((i + 1) * (R // 8), nb8 - 1), j))
    cw = pl.BlockSpec((2, 3, _CONV_CB), lambda j, i: (0, 0, j))
    cb = pl.BlockSpec((2, 1, _CONV_CB), lambda j, i: (0, 0, j))
    return main, prev, nxt, cw, cb


def _conv_act(ext_ref, h, first, count, cw_ref, cb_ref):
    prev = ext_ref[h, pl.ds(first - 1, count), :]
    mid = ext_ref[h, pl.ds(first, count), :]
    nxt = ext_ref[h, pl.ds(first + 1, count), :]
    return cb_ref[h] + cw_ref[h, 1:2, :] * mid + cw_ref[h, 0:1, :] * prev + cw_ref[h, 2:3, :] * nxt


def _conv_gate(name, a2, cw, cb, R, tc):
    n = a2.shape[1]
    main, prev, nxt, cws, cbs = _conv_specs(R, n)

    def body(a_ref, p_ref, n_ref, cw_ref, cb_ref, o_ref, ext):
        i = pl.program_id(1)
        keep_prev, keep_next = _halo_keep(i, R, tc, n)
        ext[:, 0:8, :] = jnp.zeros((2, 8, _CONV_CB), F32)
        ext[:, 8:16, :] = p_ref[...] * keep_prev
        ext[:, 16:16 + R, :] = a_ref[...]
        ext[:, 16 + R:24 + R, :] = n_ref[...] * keep_next
        ext[:, 24 + R:32 + R, :] = jnp.zeros((2, 8, _CONV_CB), F32)
        g = _conv_act(ext, 0, 16, R, cw_ref, cb_ref)
        v = _conv_act(ext, 1, 16, R, cw_ref, cb_ref)
        o_ref[...] = (g * _sigmoid(g) * v).astype(o_ref.dtype)

    return pl.pallas_call(
        body, name=name, grid=(FFN_H // _CONV_CB, n // R),
        in_specs=[main, prev, nxt, cws, cbs],
        out_specs=pl.BlockSpec((R, _CONV_CB), lambda j, i: (i, j)),
        out_shape=jax.ShapeDtypeStruct((n, FFN_H), BF16),
        scratch_shapes=[pltpu.VMEM((2, R + 32, _CONV_CB), F32)],
        compiler_params=pltpu.CompilerParams(dimension_semantics=("arbitrary", "arbitrary"),
                                             vmem_limit_bytes=V7X_VMEM_LIMIT),
    )(a2, a2, a2, cw, cb)


def _conv_gate_bwd(name, a2, dgv, cw, cb, R, tc):
    n = a2.shape[1]
    nb8 = n // 8
    main, prev, nxt, cws, cbs = _conv_specs(R, n)
    d_main = pl.BlockSpec((R, _CONV_CB), lambda j, i: (i, j))
    d_prev = pl.BlockSpec((8, _CONV_CB), lambda j, i: (jnp.maximum(i * (R // 8) - 1, 0), j))
    d_next = pl.BlockSpec((8, _CONV_CB), lambda j, i: (jnp.minimum((i + 1) * (R // 8), nb8 - 1), j))
    RE = R + 16

    def body(a_ref, p_ref, n_ref, cw_ref, cb_ref, d_ref, dp_ref, dn_ref, da_ref, dcw_ref, dcb_ref, ext, dext):
        i = pl.program_id(1)
        keep_prev, keep_next = _halo_keep(i, R, tc, n)
        zeros8 = jnp.zeros((2, 8, _CONV_CB), F32)
        ext[:, 0:8, :] = zeros8
        ext[:, 8:16, :] = p_ref[...] * keep_prev
        ext[:, 16:16 + R, :] = a_ref[...]
        ext[:, 16 + R:24 + R, :] = n_ref[...] * keep_next
        ext[:, 24 + R:32 + R, :] = zeros8
        g = _conv_act(ext, 0, 8, RE, cw_ref, cb_ref)
        v = _conv_act(ext, 1, 8, RE, cw_ref, cb_ref)
        dout = jnp.concatenate([dp_ref[...].astype(F32) * keep_prev, d_ref[...].astype(F32),
                                dn_ref[...].astype(F32) * keep_next], axis=0)
        sg = _sigmoid(g)
        dext[:, 0:8, :] = zeros8
        dext[:, 24 + R:32 + R, :] = zeros8
        dext[0, 8:8 + RE, :] = dout * v * sg * (1.0 + g * (1.0 - sg))
        dext[1, 8:8 + RE, :] = dout * g * sg

        @pl.when(i == 0)
        def _():
            dcw_ref[...] = jnp.zeros_like(dcw_ref)
            dcb_ref[...] = jnp.zeros_like(dcb_ref)

        for h in range(2):
            da = dext[h, 16:16 + R, :]
            da_up = dext[h, 17:17 + R, :]
            da_dn = dext[h, 15:15 + R, :]
            da_ref[h] = (cw_ref[h, 1:2, :] * da + cw_ref[h, 0:1, :] * da_up + cw_ref[h, 2:3, :] * da_dn).astype(da_ref.dtype)
            a_prev = ext[h, 15:15 + R, :]
            a_mid = ext[h, 16:16 + R, :]
            a_next = ext[h, 17:17 + R, :]
            dcw_ref[h, 0:1, :] += jnp.sum(da * a_prev, axis=0, keepdims=True)
            dcw_ref[h, 1:2, :] += jnp.sum(da * a_mid, axis=0, keepdims=True)
            dcw_ref[h, 2:3, :] += jnp.sum(da * a_next, axis=0, keepdims=True)
            dcb_ref[h] += jnp.sum(da, axis=0, keepdims=True)

    return pl.pallas_call(
        body, name=name, grid=(FFN_H // _CONV_CB, n // R),
        in_specs=[main, prev, nxt, cws, cbs, d_main, d_prev, d_next],
        out_specs=[main, cws, cbs],
        out_shape=[jax.ShapeDtypeStruct((2, n, FFN_H), BF16), jax.ShapeDtypeStruct((2, 3, FFN_H), F32),
                   jax.ShapeDtypeStruct((2, 1, FFN_H), F32)],
        scratch_shapes=[pltpu.VMEM((2, R + 32, _CONV_CB), F32), pltpu.VMEM((2, R + 32, _CONV_CB), F32)],
        compiler_params=pltpu.CompilerParams(dimension_semantics=("arbitrary", "arbitrary"),
                                             vmem_limit_bytes=V7X_VMEM_LIMIT),
    )(a2, a2, a2, cw, cb, dgv, dgv, dgv)


_ARB2 = pltpu.CompilerParams(dimension_semantics=("arbitrary", "arbitrary"), vmem_limit_bytes=V7X_VMEM_LIMIT)


def _with_exchange(body, n_in, n_out, grid, comm):
    if comm is None:
        return body, [], [], []
    xs, gather = comm
    na = len(xs)
    specs, shapes, sems = _exchange_io(xs, gather)

    def wrapped(*refs):
        ins, x_refs = refs[:n_in], refs[n_in:n_in + na]
        outs, o_refs = refs[n_in + na:n_in + na + n_out], refs[n_in + na + n_out:n_in + 2 * na + n_out]
        scratch, sem_refs = refs[n_in + 2 * na + n_out:-3], refs[-3:]
        start, wait = _exchange_plan(x_refs, o_refs, *sem_refs, gather)
        ids = [pl.program_id(d) for d in range(len(grid))]
        first, last = ids[0] == 0, ids[0] == grid[0] - 1
        for d in range(1, len(grid)):
            first, last = first & (ids[d] == 0), last & (ids[d] == grid[d] - 1)

        @pl.when(first)
        def _():
            start()

        body(*ins, *outs, *scratch)

        @pl.when(last)
        def _():
            wait()

    return wrapped, specs, shapes, sems


def _attn_fwd(name, qt8, k2, vt2, comm=None):
    nq, nk = qt8.shape[2], k2.shape[1]
    tq = _pick(nq, (256, 128))
    tk = _pick(nk, (768, 256, 128))

    va = vt2.shape[1]

    def body(qt_ref, k_ref, vt_ref, ot_ref, lse_ref, m_ref, acc_ref):
        m_ref[...] = jnp.full((Q_GROUP, 1, tq), -1e30, F32)
        acc_ref[...] = jnp.zeros((Q_GROUP, va, tq), F32)

        def step(j, carry):
            sl = pl.ds(pl.multiple_of(j * tk, tk), tk)
            kj = k_ref[sl, :]
            vtj = vt_ref[:, sl]
            sts = [lax.dot_general(kj, qt_ref[h], _NN, preferred_element_type=F32) for h in range(Q_GROUP)]
            m_old = [m_ref[h] for h in range(Q_GROUP)]
            acc_old = [acc_ref[h] for h in range(Q_GROUP)]
            m_new = [jnp.maximum(m_old[h], jnp.max(sts[h], axis=0, keepdims=True)) for h in range(Q_GROUP)]
            pts = [jnp.exp(sts[h] - m_new[h]).astype(BF16) for h in range(Q_GROUP)]
            pvs = [lax.dot_general(vtj, pts[h], _NN, preferred_element_type=F32) for h in range(Q_GROUP)]
            for h in range(Q_GROUP):
                acc_ref[h] = jnp.exp(m_old[h] - m_new[h]) * acc_old[h] + pvs[h]
                m_ref[h] = m_new[h]
            return carry

        lax.fori_loop(0, nk // tk, step, 0, unroll=2)
        for h in range(Q_GROUP):
            l = acc_ref[h, HEAD_DIM:HEAD_DIM + 1, :]
            ot_ref[h] = (acc_ref[h, 0:HEAD_DIM, :] / l).astype(ot_ref.dtype)
            lse_ref[h] = m_ref[h] + jnp.log(l)

    qspec = pl.BlockSpec((Q_GROUP, HEAD_DIM, tq), lambda g, i: (g, 0, i))
    lspec = pl.BlockSpec((Q_GROUP, 1, tq), lambda g, i: (g, 0, i))
    grid = (N_KV_HEADS, nq // tq)
    body, xspecs, xshapes, xsems = _with_exchange(body, 3, 2, grid, comm)
    res = pl.pallas_call(
        body, name=name, grid=grid,
        in_specs=[qspec, pl.BlockSpec((None, nk, HEAD_DIM), lambda g, i: (g, 0, 0)),
                  pl.BlockSpec((None, va, nk), lambda g, i: (g, 0, 0))] + xspecs,
        out_specs=[qspec, lspec] + xspecs,
        out_shape=[jax.ShapeDtypeStruct((N_Q_HEADS, HEAD_DIM, nq), BF16), jax.ShapeDtypeStruct((N_Q_HEADS, 1, nq), F32)] + xshapes,
        scratch_shapes=[pltpu.VMEM((Q_GROUP, 1, tq), F32), pltpu.VMEM((Q_GROUP, va, tq), F32)] + xsems,
        compiler_params=_ARB2,
    )(qt8, k2, vt2, *(comm[0] if comm else []))
    return res[0], res[1], list(res[2:])


def _attn_bwd(name, qt8, k2, v2, kt2, ot8, dot8, lse, comm=None):
    nq, nk = qt8.shape[2], k2.shape[1]
    tq = _pick(nq, (256, 128))
    tk = _pick(nk, (768, 256, 128))
    heads = range(Q_GROUP)

    def body(qt_ref, k_ref, v_ref, kt_ref, ot_ref, dot_ref, lse_ref, dqt_ref, dkt_ref, dvt_ref, dl_ref, dq_acc):
        @pl.when(pl.program_id(1) == 0)
        def _():
            dkt_ref[...] = jnp.zeros_like(dkt_ref)
            dvt_ref[...] = jnp.zeros_like(dvt_ref)

        for h in heads:
            dl_ref[h] = jnp.sum(dot_ref[h].astype(F32) * ot_ref[h].astype(F32), axis=0, keepdims=True)
        dq_acc[...] = jnp.zeros((Q_GROUP, HEAD_DIM, tq), F32)

        def step(j, carry):
            sl = pl.ds(pl.multiple_of(j * tk, tk), tk)
            kj = k_ref[sl, :]
            vj = v_ref[sl, :]
            ktj = kt_ref[:, sl]
            sts = [lax.dot_general(kj, qt_ref[h], _NN, preferred_element_type=F32) for h in heads]
            dpts = [lax.dot_general(vj, dot_ref[h], _NN, preferred_element_type=F32) for h in heads]
            dvt_j = jnp.zeros((HEAD_DIM, tk), F32)
            dkt_j = jnp.zeros((HEAD_DIM, tk), F32)
            for h in heads:
                pt = jnp.exp(sts[h] - lse_ref[h])
                dst = (pt * (dpts[h] - dl_ref[h])).astype(BF16)
                dq_acc[h] += lax.dot_general(ktj, dst, _NN, preferred_element_type=F32)
                dvt_j = dvt_j + lax.dot_general(dot_ref[h], pt.astype(BF16), _NT, preferred_element_type=F32)
                dkt_j = dkt_j + lax.dot_general(qt_ref[h], dst, _NT, preferred_element_type=F32)
            dvt_ref[:, sl] += dvt_j
            dkt_ref[:, sl] += dkt_j
            return carry

        lax.fori_loop(0, nk // tk, step, 0)
        dqt_ref[...] = dq_acc[...]

    tspec = pl.BlockSpec((Q_GROUP, HEAD_DIM, tq), lambda g, i: (g, 0, i))
    lspec = pl.BlockSpec((Q_GROUP, 1, tq), lambda g, i: (g, 0, i))
    kspec = pl.BlockSpec((None, nk, HEAD_DIM), lambda g, i: (g, 0, 0))
    ktspec = pl.BlockSpec((None, HEAD_DIM, nk), lambda g, i: (g, 0, 0))
    grid = (N_KV_HEADS, nq // tq)
    body, xspecs, xshapes, xsems = _with_exchange(body, 7, 3, grid, comm)
    res = pl.pallas_call(
        body, name=name, grid=grid,
        in_specs=[tspec, kspec, kspec, ktspec, tspec, tspec, lspec] + xspecs,
        out_specs=[tspec, ktspec, ktspec] + xspecs,
        out_shape=[jax.ShapeDtypeStruct((N_Q_HEADS, HEAD_DIM, nq), F32), jax.ShapeDtypeStruct((N_KV_HEADS, HEAD_DIM, nk), F32),
                   jax.ShapeDtypeStruct((N_KV_HEADS, HEAD_DIM, nk), F32)] + xshapes,
        scratch_shapes=[pltpu.VMEM((Q_GROUP, 1, tq), F32), pltpu.VMEM((Q_GROUP, HEAD_DIM, tq), F32)] + xsems,
        compiler_params=_ARB2,
    )(qt8, k2, v2, kt2, ot8, dot8, lse, *(comm[0] if comm else []))
    return res[0], res[1], res[2], list(res[3:])


def _log_sigmoid(z):
    return jnp.minimum(z, 0.0) - jnp.log(1.0 + jnp.exp(-jnp.abs(z)))


_BDOT_DIMS = {"nn": _NN, "nt": _NT, "tn": _TN}
_BDOT_BWD = {"nn": (("nt", "gb"), ("tn", "ag")), "nt": (("nn", "gb"), ("tn", "ga")), "tn": (("nt", "bg"), ("nn", "ag"))}


def _bdot_raw(a, b, mode):
    return lax.dot_general(a.astype(BF16), b.astype(BF16), _BDOT_DIMS[mode], preferred_element_type=F32)


@functools.partial(jax.custom_vjp, nondiff_argnums=(2,))
def _bdot(a, b, mode):
    return _bdot_raw(a, b, mode)


def _bdot_fwd(a, b, mode):
    return _bdot_raw(a, b, mode), (a.astype(BF16), b.astype(BF16))


def _bdot_bwd(mode, res, g):
    ops = {"a": res[0], "b": res[1], "g": g}
    (ma, oa), (mb, ob) = _BDOT_BWD[mode]
    return _bdot_raw(ops[oa[0]], ops[oa[1]], ma), _bdot_raw(ops[ob[0]], ops[ob[1]], mb)


_bdot.defvjp(_bdot_fwd, _bdot_bwd)


def _tile_tri(rev, rows):
    r_i = lax.broadcasted_iota(jnp.int32, (rows, rows), 0)
    c_i = lax.broadcasted_iota(jnp.int32, (rows, rows), 1)
    same = (r_i // GLA_CHUNK) == (c_i // GLA_CHUNK)
    return same & ((c_i >= r_i) if rev else (c_i <= r_i))


def _tri_dot(rev, x):
    tri = _tile_tri(rev, x.shape[0]).astype(BF16)
    hi = x.astype(BF16)
    rest = x - hi.astype(F32)
    mid = rest.astype(BF16)
    lo = (rest - mid.astype(F32)).astype(BF16)
    d = lambda piece: lax.dot_general(tri, piece, _NN, preferred_element_type=F32)
    return d(hi) + d(mid) + d(lo)


@functools.partial(jax.custom_vjp, nondiff_argnums=(1,))
def _chunk_cumsum(x, rev):
    return _tri_dot(rev, x)


_chunk_cumsum.defvjp(lambda x, rev: (_tri_dot(rev, x), None), lambda rev, _, g: (_tri_dot(not rev, g),))


def _gla_tile(q, k, vs, a, w2, b2, state_t, *, rev):
    rows = q.shape[0]
    nch = rows // GLA_CHUNK
    tri = _tile_tri(rev, rows)
    chunk_of_row = lax.broadcasted_iota(jnp.int32, (rows, 1), 0) // GLA_CHUNK
    in_chunk = [(chunk_of_row == c).astype(F32) for c in range(nch)]
    la = _log_sigmoid(_bdot(a, w2, "nn") + b2) * (1.0 / GLA_TAU)
    cum = _chunk_cumsum(la, rev)
    tots = [jnp.sum(la * in_chunk[c], axis=0, keepdims=True) for c in range(nch)]
    tot_rows = sum(in_chunk[c] * tots[c] for c in range(nch))
    q_in = q * (GLA_DK ** -0.5) * jnp.exp(cum)
    k_in = k * jnp.exp(-cum)
    k_st = k * jnp.exp(tot_rows - cum)
    lane = lax.broadcasted_iota(jnp.int32, (1, GLA_QK_W), 1)
    outs = []
    for h in range(GLA_HEADS):
        head = ((lane >= GLA_DK * h) & (lane < GLA_DK * (h + 1))).astype(F32)
        att = jnp.where(tri, _bdot(q_in * head, k_in, "nt"), 0.0)
        outs.append(_bdot(att, vs[h], "nn"))
    o = jnp.concatenate(outs, axis=1)
    hr = lax.broadcasted_iota(jnp.int32, (GLA_V_W, GLA_QK_W), 0) // GLA_DV
    hc = lax.broadcasted_iota(jnp.int32, (GLA_V_W, GLA_QK_W), 1) // GLA_DK
    same_head = (hr == hc).astype(F32)
    v_all = jnp.concatenate(vs, axis=1)
    for c in (range(nch - 1, -1, -1) if rev else range(nch)):
        o = o + _bdot(q_in * in_chunk[c], state_t, "nt")
        state_t = jnp.exp(tots[c]) * state_t + _bdot(v_all, k_st * in_chunk[c], "tn") * same_head
    return o, state_t


def _gla_tile_of(step, rev, nct, nt):
    if not rev:
        return step
    return jnp.where(step < nct, nct - 1 - step, nt - 1 - (step - nct))


def _gla_row_specs(R, tile):
    return [pl.BlockSpec((R, GLA_QK_W), lambda s: (tile(s), C_GQ // GLA_QK_W)),
            pl.BlockSpec((R, GLA_QK_W), lambda s: (tile(s), C_GK // GLA_QK_W)),
            pl.BlockSpec((R, GLA_V_W), lambda s: (tile(s), C_GV // GLA_V_W)),
            pl.BlockSpec((R, 128), lambda s: (tile(s), C_AF // 128))]


def _gla_dir(name, p, w2, b2, *, rev, R, nct, add=None):
    n = p.shape[0]
    nt = n // R
    tile = lambda s: _gla_tile_of(s, rev, nct, nt)

    def body(q_ref, k_ref, v_ref, a_ref, w2_ref, b2_ref, *rest):
        if add is not None:
            add_ref, o_ref, ssave_ref, state = rest
        else:
            o_ref, ssave_ref, state = rest

        @pl.when(pl.program_id(0) == 0)
        def _():
            state[...] = jnp.zeros_like(state)

        vs = [v_ref[:, GLA_DV * h:GLA_DV * (h + 1)] for h in range(GLA_HEADS)]
        s_in = state[...]
        ssave_ref[...] = s_in
        o, s_out = _gla_tile(q_ref[...], k_ref[...], vs, a_ref[...], w2_ref[...], b2_ref[...], s_in, rev=rev)
        if add is not None:
            o = o + add_ref[...]
        o_ref[...] = o
        state[...] = s_out

    o_spec = pl.BlockSpec((R, GLA_V_W), lambda s: (tile(s), 0))
    in_specs = _gla_row_specs(R, tile) + [_full_spec(w2.shape), _full_spec(b2.shape)]
    args = [p, p, p, p, w2, b2]
    if add is not None:
        in_specs.append(o_spec)
        args.append(add)
    return pl.pallas_call(
        body, name=name, grid=(nt,), in_specs=in_specs,
        out_specs=[o_spec, pl.BlockSpec((None, GLA_V_W, GLA_QK_W), lambda s: (tile(s), 0, 0))],
        out_shape=[jax.ShapeDtypeStruct((n, GLA_V_W), F32), jax.ShapeDtypeStruct((nt, GLA_V_W, GLA_QK_W), F32)],
        scratch_shapes=[pltpu.VMEM((GLA_V_W, GLA_QK_W), F32)],
        compiler_params=_ARB1,
    )(*args)


def _gla_dir_bwd(name, p, w2, b2, ssave, do, *, rev, R, nct, adds=None, out_dtype=F32):
    n = p.shape[0]
    nt = n // R
    tile = lambda s: _gla_tile_of(nt - 1 - s, rev, nct, nt)
    widths = (GLA_QK_W, GLA_QK_W, GLA_V_W, 128)

    def body(q_ref, k_ref, v_ref, a_ref, w2_ref, b2_ref, ss_ref, do_ref, *rest):
        if adds is not None:
            add_refs, rest = rest[:4], rest[4:]
        dq_ref, dk_ref, dv_ref, da_ref, dw2_ref, db2_ref, dstate = rest

        @pl.when(pl.program_id(0) == 0)
        def _():
            dstate[...] = jnp.zeros_like(dstate)
            dw2_ref[...] = jnp.zeros_like(dw2_ref)
            db2_ref[...] = jnp.zeros_like(db2_ref)

        vs = [v_ref[:, GLA_DV * h:GLA_DV * (h + 1)] for h in range(GLA_HEADS)]
        _, vjp = jax.vjp(functools.partial(_gla_tile, rev=rev), q_ref[...], k_ref[...], vs, a_ref[...],
                         w2_ref[...], b2_ref[...], ss_ref[...])
        dq, dk, dvs, da, dw2, db2, ds = vjp((do_ref[...], dstate[...]))
        grads = [dq, dk, jnp.concatenate(dvs, axis=1), da]
        if adds is not None:
            grads = [g + r[...].astype(F32) for g, r in zip(grads, add_refs)]
        for ref, g in zip((dq_ref, dk_ref, dv_ref, da_ref), grads):
            ref[...] = g.astype(ref.dtype)
        dw2_ref[...] += dw2
        db2_ref[...] += db2
        dstate[...] = ds

    d_specs = [pl.BlockSpec((R, w), lambda s: (tile(s), 0)) for w in widths]
    in_specs = (_gla_row_specs(R, tile) + [_full_spec(w2.shape), _full_spec(b2.shape),
                pl.BlockSpec((None, GLA_V_W, GLA_QK_W), lambda s: (tile(s), 0, 0)),
                pl.BlockSpec((R, GLA_V_W), lambda s: (tile(s), 0))])
    args = [p, p, p, p, w2, b2, ssave, do]
    if adds is not None:
        in_specs += d_specs
        args += list(adds)
    return pl.pallas_call(
        body, name=name, grid=(nt,), in_specs=in_specs,
        out_specs=d_specs + [_full_spec(w2.shape), _full_spec(b2.shape)],
        out_shape=[jax.ShapeDtypeStruct((n, w), out_dtype) for w in widths]
        + [jax.ShapeDtypeStruct(w2.shape, F32), jax.ShapeDtypeStruct(b2.shape, F32)],
        scratch_shapes=[pltpu.VMEM((GLA_V_W, GLA_QK_W), F32)],
        compiler_params=_ARB1,
    )(*args)


def _final_loss(name, x, target, gf, R, nct):
    n = x.shape[0]

    def body(x_ref, t_ref, g_ref, loss_ref, dx_ref, dg_ref):
        i = pl.program_id(0)

        @pl.when(i == 0)
        def _():
            loss_ref[...] = jnp.zeros_like(loss_ref)
            dg_ref[...] = jnp.zeros_like(dg_ref)

        @pl.when(i < nct)
        def _():
            dx_ref[...] = jnp.zeros_like(dx_ref)

        @pl.when(i >= nct)
        def _():
            y, vjp = jax.vjp(_rms, x_ref[...], g_ref[...])
            err = y - t_ref[...]
            loss_ref[...] += jnp.sum(0.5 * jnp.mean(err * err, axis=-1, keepdims=True))
            dx, dg = vjp(err * (1.0 / D))
            dx_ref[...] = dx
            dg_ref[...] += dg

    return pl.pallas_call(
        body, name=name, grid=(n // R,),
        in_specs=[_row_spec(R, D, 0), pl.BlockSpec((R, D), lambda i: (jnp.maximum(i - nct, 0), 0)), _full_spec((1, D))],
        out_specs=[_full_spec((8, 128)), _row_spec(R, D, 0), _full_spec((1, D))],
        out_shape=[jax.ShapeDtypeStruct((8, 128), F32), jax.ShapeDtypeStruct((n, D), F32), jax.ShapeDtypeStruct((1, D), F32)],
        compiler_params=_ARB1,
    )(x, target, gf)


def _adamw(name, w, m, v, gparts):
    rows, cols = w.shape
    nparts = gparts.shape[0]
    tr = rows
    for cand in range(min(rows, 256), 15, -16):
        if rows % cand == 0:
            tr = cand
            break

    def body(w_ref, m_ref, v_ref, g_ref, go_ref, d_ref, mo_ref, vo_ref):
        g = g_ref[0].astype(F32)
        for k in range(1, nparts):
            g = g + g_ref[k].astype(F32)
        m_new = ADAM_B1 * m_ref[...] + (1.0 - ADAM_B1) * g
        v_new = ADAM_B2 * v_ref[...] + (1.0 - ADAM_B2) * (g * g)
        m_hat = m_new / (1.0 - ADAM_B1 ** ADAM_STEP)
        v_hat = v_new / (1.0 - ADAM_B2 ** ADAM_STEP)
        go_ref[...] = g
        d_ref[...] = -ADAM_LR * (m_hat / (jnp.sqrt(v_hat) + ADAM_EPS) + ADAM_WD * w_ref[...])
        mo_ref[...] = m_new
        vo_ref[...] = v_new

    spec = pl.BlockSpec((tr, cols), lambda i: (i, 0))
    return pl.pallas_call(
        body, name=name, grid=(rows // tr,),
        in_specs=[spec, spec, spec, pl.BlockSpec((nparts, tr, cols), lambda i: (0, i, 0))],
        out_specs=[spec] * 4, out_shape=[jax.ShapeDtypeStruct((rows, cols), F32)] * 4,
        compiler_params=_ARB1,
    )(w, m, v, gparts)


def _my_index():
    return 4 * lax.axis_index("x") + 2 * lax.axis_index("y") + lax.axis_index("c")


def _xor_peer(k):
    flip = lambda a, bit: (1 - a) if bit else a
    pos = (flip(lax.axis_index("x"), (k >> 2) & 1), flip(lax.axis_index("y"), (k >> 1) & 1), flip(lax.axis_index("c"), k & 1))
    return pos, 4 * pos[0] + 2 * pos[1] + pos[2]


def _exchange_plan(x_refs, o_refs, send_sems, recv_sems, local_sems, gather):
    npeer = N_DEV - 1
    me = _my_index()
    locals_, sends, recvs = [], [], []
    for a, (x_ref, o_ref) in enumerate(zip(x_refs, o_refs)):
        mine = x_ref if gather else x_ref.at[me]
        locals_.append(pltpu.make_async_copy(mine, o_ref.at[me], local_sems.at[a]))
        for k in range(1, N_DEV):
            pos, lin = _xor_peer(k)
            src = x_ref if gather else x_ref.at[lin]
            sem = a * npeer + k - 1
            sends.append(pltpu.make_async_remote_copy(src_ref=src, dst_ref=o_ref.at[me], send_sem=send_sems.at[sem],
                                                      recv_sem=recv_sems.at[sem], device_id=pos, device_id_type=MESH_ID))
            recvs.append(pltpu.make_async_remote_copy(src_ref=src, dst_ref=o_ref.at[lin], send_sem=send_sems.at[sem],
                                                      recv_sem=recv_sems.at[sem], device_id=pos, device_id_type=MESH_ID))

    def start():
        for cp in locals_ + sends:
            cp.start()

    def wait():
        for cp in recvs:
            cp.wait_recv()
        for cp in sends:
            cp.wait_send()
        for cp in locals_:
            cp.wait()

    return start, wait


def _exchange_io(xs, gather):
    na = len(xs)
    hbm = pl.BlockSpec(memory_space=pltpu.HBM)
    shapes = [jax.ShapeDtypeStruct((N_DEV,) + tuple(x.shape if gather else x.shape[1:]), x.dtype) for x in xs]
    sems = [pltpu.SemaphoreType.DMA((na * (N_DEV - 1),)), pltpu.SemaphoreType.DMA((na * (N_DEV - 1),)),
            pltpu.SemaphoreType.DMA((na,))]
    return [hbm] * na, shapes, sems


def _exchange(name, xs, *, gather):
    na = len(xs)
    specs, shapes, sems = _exchange_io(xs, gather)

    def body(*refs):
        start, wait = _exchange_plan(refs[:na], refs[na:2 * na], *refs[2 * na:], gather)
        start()
        wait()

    return list(pl.pallas_call(body, name=name, in_specs=specs, out_specs=specs, out_shape=shapes, scratch_shapes=sems)(*xs))


def _adaln_fwd(name, craw16, w_ada, b_cols):
    def body(c_ref, w_ref, b_ref, o_ref):
        cs = jax.nn.silu(c_ref[...]).astype(BF16)
        for l in range(2):
            o_ref[l] = lax.dot_general(cs, w_ref[l].astype(BF16), _NN, preferred_element_type=F32) + b_ref[l]

    return pl.pallas_call(
        body, name=name, out_shape=jax.ShapeDtypeStruct((2, 16, w_ada.shape[2]), F32),
        compiler_params=pltpu.CompilerParams(vmem_limit_bytes=V7X_VMEM_LIMIT),
    )(craw16, w_ada, b_cols)


def _adaln_bwd(name, craw16, w_ada, dm):
    def body(c_ref, w_ref, dm_ref, gw_ref, dc_ref):
        c = c_ref[...]
        sg = jax.nn.sigmoid(c)
        cs = c * sg
        row = lax.broadcasted_iota(jnp.int32, (8, 1), 0)
        dc = jnp.zeros((16, D), F32)
        for l in range(2):
            dmx = dm_ref[2 * l + 1]
            dmc = jnp.where(row == 0, jnp.sum(dm_ref[2 * l], axis=0, keepdims=True), 0.0)
            gw_ref[l] = _hdot(cs[0:8], dmx, _TN) + _hdot(cs[8:16], dmc, _TN)
            dc = dc + _hdot(jnp.concatenate([dmx, dmc], axis=0), w_ref[l], _NT)
        dc_ref[...] = dc * sg * (1.0 + c * (1.0 - sg))

    return pl.pallas_call(
        body, name=name,
        out_shape=[jax.ShapeDtypeStruct(w_ada.shape, F32), jax.ShapeDtypeStruct((16, D), F32)],
        compiler_params=pltpu.CompilerParams(vmem_limit_bytes=V7X_VMEM_LIMIT),
    )(craw16, w_ada, dm)


_IN_OFFS = [sum(IN_SPLITS[:k]) for k in range(len(IN_SPLITS) + 1)]
_MY_ORDER = (11, 12, 13, 0, 1, 2, 7, 10, 5, 6, 3, 4, 8, 9)


_IN_SHARD = IN_WIDTH // N_DEV


def _win_my_cols(pieces):
    parts = []
    for k in _MY_ORDER:
        a, b = _IN_OFFS[k], _IN_OFFS[k + 1]
        for s in range(a // _IN_SHARD, (b - 1) // _IN_SHARD + 1):
            lo, hi = max(a, s * _IN_SHARD), min(b, (s + 1) * _IN_SHARD)
            parts.append(pieces[s][:, lo - s * _IN_SHARD:hi - s * _IN_SHARD])
    parts.append(jnp.zeros((pieces[0].shape[0], PW - IN_WIDTH), pieces[0].dtype))
    return jnp.concatenate(parts, axis=1)


def _win_shards(wp):
    my_offs, pos = {}, 0
    for k in _MY_ORDER:
        my_offs[k] = pos
        pos += IN_SPLITS[k]
    shards = []
    for s in range(N_DEV):
        parts = []
        for k in range(len(IN_SPLITS)):
            lo, hi = max(_IN_OFFS[k], s * _IN_SHARD), min(_IN_OFFS[k + 1], (s + 1) * _IN_SHARD)
            if lo < hi:
                parts.append(wp[:, my_offs[k] + lo - _IN_OFFS[k]:my_offs[k] + hi - _IN_OFFS[k]])
        shards.append(jnp.concatenate(parts, axis=1))
    return jnp.stack(shards)


def _split_shards(full, axis):
    c = full.shape[axis] // N_DEV
    return jnp.stack([lax.slice_in_dim(full, s * c, (s + 1) * c, axis=axis) for s in range(N_DEV)])


def _pack_rows(pieces, row_mult):
    rows = jnp.concatenate([p.reshape(-1, 128) for p in pieces], axis=0)
    padn = (-rows.shape[0]) % row_mult
    if padn:
        rows = jnp.concatenate([rows, jnp.zeros((padn, 128), rows.dtype)], axis=0)
    return rows


def _unpack_rows(rows, shapes):
    out, pos = [], 0
    for s in shapes:
        size = 1
        for d in s:
            size *= d
        out.append(rows[pos:pos + size // 128].reshape(tuple(s)))
        pos += size // 128
    return out


def _heads_front(a, nh):
    return a.reshape(a.shape[0], nh, HEAD_DIM).transpose(1, 0, 2)


def _heads_back(a):
    return a.transpose(1, 0, 2).reshape(a.shape[1], a.shape[0] * HEAD_DIM)


def _rope_tables(t, tc):
    tok = jnp.arange(t, dtype=jnp.int32)
    inv_freq = ROPE_THETA ** (-jnp.arange(ROPE_FREQS, dtype=F32) / ROPE_FREQS)
    ang_r = (tok // GRID_W).astype(F32)[:, None] * inv_freq
    ang_c = (tok % GRID_W).astype(F32)[:, None] * inv_freq
    cos64 = jnp.concatenate([jnp.cos(ang_r), jnp.cos(ang_r), jnp.cos(ang_c), jnp.cos(ang_c)], axis=1)
    sin64 = jnp.concatenate([-jnp.sin(ang_r), jnp.sin(ang_r), -jnp.sin(ang_c), jnp.sin(ang_c)], axis=1)
    cos64 = jnp.concatenate([jnp.ones((tc, HEAD_DIM), F32), cos64], axis=0)
    sin64 = jnp.concatenate([jnp.zeros((tc, HEAD_DIM), F32), sin64], axis=0)
    return jnp.tile(cos64, (1, N_Q_HEADS)), jnp.tile(sin64, (1, N_Q_HEADS))


def _head_mean_matrix(width):
    i = jnp.arange(width) // HEAD_DIM
    return (i[:, None] == i[None, :]).astype(F32) / HEAD_DIM


def _head_tile_matrix(width):
    return (jnp.arange(HEAD_DIM)[:, None] == (jnp.arange(width) % HEAD_DIM)[None, :]).astype(F32)


def _heads_t(a, nh):
    return a.T.reshape(nh, HEAD_DIM, a.shape[0])


def _heads_t_back(a):
    return a.reshape(a.shape[0] * HEAD_DIM, a.shape[2]).T


def _attention_fwd(tag, qr, kr, vv, tc, comm):
    qt8, k2, vt2 = _heads_t(qr, N_Q_HEADS), _heads_front(kr, N_KV_HEADS), _heads_t(vv, N_KV_HEADS)
    vt2 = jnp.concatenate([vt2, jnp.ones((N_KV_HEADS, 8, vt2.shape[2]), BF16)], axis=1)
    o_c, lse_c, _ = _attn_fwd(tag + "_attn_ctx", qt8[:, :, :tc], k2[:, :tc], vt2[:, :, :tc])
    o_x, lse_x, comm_out = _attn_fwd(tag + "_attn_lat", qt8[:, :, tc:], k2, vt2, comm)
    ot8 = jnp.concatenate([o_c, o_x], axis=2)
    lse = jnp.concatenate([lse_c, lse_x], axis=2)
    return _heads_t_back(ot8), (qr, kr, vv, ot8, lse), comm_out


def _attention_bwd(tag, saved, datt, tc, comm):
    qr, kr, vv, ot8, lse = saved
    datt = datt.astype(BF16)
    qt8, dot8 = _heads_t(qr, N_Q_HEADS), _heads_t(datt, N_Q_HEADS)
    k2, v2, kt2 = _heads_front(kr, N_KV_HEADS), _heads_front(vv, N_KV_HEADS), _heads_t(kr, N_KV_HEADS)
    dq_c, dk_c, dv_c, _ = _attn_bwd(tag + "_attn_b_ctx", qt8[:, :, :tc], k2[:, :tc], v2[:, :tc], kt2[:, :, :tc],
                                    ot8[:, :, :tc], dot8[:, :, :tc], lse[:, :, :tc])
    dq_x, dk_x, dv_x, comm_out = _attn_bwd(tag + "_attn_b_lat", qt8[:, :, tc:], k2, v2, kt2,
                                           ot8[:, :, tc:], dot8[:, :, tc:], lse[:, :, tc:], comm)
    dqt8 = jnp.concatenate([dq_c, dq_x], axis=2)
    dkt2 = dk_x.at[:, :, :tc].add(dk_c)
    dvt2 = dv_x.at[:, :, :tc].add(dv_c)
    return _heads_t_back(dqt8), _heads_t_back(dkt2), _heads_t_back(dvt2), comm_out


def _layer_fwd(tag, x, w, modv, consts, R, nct, tc, comm, late_weights):
    sh1, sc1, g1, sh2, sc2, g2 = modv
    cosq, sinq, bdq, eq, bdk, ek = consts
    n = x.shape[0]
    (h1,) = _rowwise(tag + "_ln1", _f_lnmod, R, nct, [(x, D, 0, D)], [], [(w["norm1_g"], False)], [], [sh1, sc1], [(D, BF16)])
    p = _matmul(tag + "_in", h1, w["w_in"], "nn", F32)
    qk_rows = [(p, Q_W, C_Q // Q_W, Q_W), (p, KV_W, C_K // KV_W, KV_W)]
    qk_consts = [(cosq, Q_W, 0, Q_W), (sinq, Q_W, 0, Q_W), (cosq, KV_W, 0, KV_W), (sinq, KV_W, 0, KV_W)]
    qk_params = [(w["q_norm_g"], False), (w["k_norm_g"], False)]
    qk_cparams = [(bdq, False), (eq, False), (bdk, False), (ek, False)]
    qr, kr = _rowwise(tag + "_qk", _f_qknorm, R, nct, qk_rows, qk_consts, qk_params, qk_cparams, [],
                      [(Q_W, BF16), (KV_W, BF16)], post=_qk_post)
    vv = p[:, C_VV:C_VV + KV_W].astype(BF16)
    att, att_saved, comm_out = _attention_fwd(tag, qr, kr, vv, tc, comm)
    w.update(late_weights(comm_out))

    o_f, s_f = _gla_dir(tag + "_gla_f", p, w["w2p_f"], w["b2_f"], rev=False, R=R, nct=nct)
    o_fb, s_b = _gla_dir(tag + "_gla_b", p, w["w2p_b"], w["b2_b"], rev=True, R=R, nct=nct, add=o_f)
    go_rows = [(o_fb, GLA_V_W, 0, GLA_DV), (p, GLA_V_W, C_R // GLA_V_W, GLA_DV)]
    (gla,) = _rowwise(tag + "_glaout", _f_glaout, R, nct, go_rows, [], [(w["gla_norm_g"], True)], [], [], [(GLA_V_W, BF16)])

    rg = GMLP_CHUNK
    gm_rows = [(p, GMLP_W, C_U // GMLP_W, GMLP_W), (p, GMLP_W, C_V // GMLP_W, GMLP_W // GMLP_GROUPS)]
    gm_params = [(w["gmlp_norm_g"], True), (w["w_spatial"], True), (w["b_spatial_t"], True)]
    (gm,) = _rowwise(tag + "_gmlp", _f_gmlp, rg, tc // rg, gm_rows, [], gm_params, [], [], [(GMLP_W, BF16)])

    ya = _matmul(tag + "_br_a", gm, w["w_br_a"], "nn", F32)
    yb = _matmul(tag + "_br_b", att, w["w_br_b"], "nn", F32)
    yc = _matmul(tag + "_br_c", gla, w["w_br_c"], "nn", F32)
    mg_rows = [(p, D, C_GA // D, D), (p, D, C_GB // D, D), (p, D, C_GC // D, D), (ya, D, 0, D), (yb, D, 0, D), (yc, D, 0, D)]
    (merged,) = _rowwise(tag + "_merge", _f_merge, R, nct, mg_rows, [], [], [], [], [(D, BF16)])
    mix = _matmul(tag + "_out", merged, w["w_out"], "nn", F32)
    (x_mid,) = _rowwise(tag + "_res1", _f_resid, R, nct, [(x, D, 0, D), (mix, D, 0, D)], [], [], [], [g1], [(D, F32)])

    (h2,) = _rowwise(tag + "_ln2", _f_lnmod, R, nct, [(x_mid, D, 0, D)], [], [(w["norm2_g"], False)], [], [sh2, sc2], [(D, BF16)])
    a2 = _matmul(tag + "_up", h2, w["w_ffn_up"], "nn", F32, o_halves=True)
    gv = _conv_gate(tag + "_conv", a2, w["conv_w_h"], w["conv_b_h"], R, tc)
    ffn = _matmul(tag + "_down", gv, w["w_ffn_down"], "nn", F32)
    (x_next,) = _rowwise(tag + "_res2", _f_resid, R, nct, [(x_mid, D, 0, D), (ffn, D, 0, D)], [], [], [], [g2], [(D, F32)])
    saved = dict(x=x, h1=h1, p=p, att_saved=att_saved, att=att, o_fb=o_fb, s_f=s_f, s_b=s_b, gla=gla, gm=gm,
                 ya=ya, yb=yb, yc=yc, merged=merged, mix=mix, x_mid=x_mid, h2=h2, a2=a2, gv=gv, ffn=ffn,
                 qk=(qk_rows, qk_consts, qk_params, qk_cparams), go_rows=go_rows, gm_info=(gm_rows, gm_params),
                 mg_rows=mg_rows)
    return x_next, saved, comm_out


def _layer_bwd(tag, dx_next, s, w, modv, R, nct, tc, make_comm):
    sh1, sc1, g1, sh2, sc2, g2 = modv
    gw = {}
    (dffn,), _, (dg2,) = _rowwise_bwd(tag + "_res2_b", _f_resid, R, nct, [(s["ffn"], D, 0, D), (s["ffn"], D, 0, D)], [], [], [], [g2],
                                      [(dx_next, D)], [None, BF16])
    dgv = _matmul(tag + "_down_da", dffn, w["w_ffn_down"], "nt", F32)
    gw["w_ffn_down"] = _matmul(tag + "_down_dw", s["gv"], dffn, "tn", F32)
    da2, dcw, dcb = _conv_gate_bwd(tag + "_conv_b", s["a2"], dgv, w["conv_w_h"], w["conv_b_h"], R, tc)
    gw["conv_w_h"], gw["conv_b_h"] = dcw, dcb
    dh2 = _matmul(tag + "_up_da", da2, w["w_ffn_up"], "nt", F32, a_halves=True)
    gw["w_ffn_up"] = _matmul(tag + "_up_dw", s["h2"], da2, "tn", F32, b_halves=True)
    (dx_mid,), (gw["norm2_g"],), (dsh2, dsc2) = _rowwise_bwd(
        tag + "_ln2_b", _f_lnmod, R, nct, [(s["x_mid"], D, 0, D)], [], [(w["norm2_g"], False)], [], [sh2, sc2],
        [(dh2, D)], [F32], adds=[dx_next])
    (dmix,), _, (dg1,) = _rowwise_bwd(tag + "_res1_b", _f_resid, R, nct, [(s["mix"], D, 0, D), (s["mix"], D, 0, D)], [], [], [], [g1],
                                      [(dx_mid, D)], [None, BF16])
    dmerged = _matmul(tag + "_out_da", dmix, w["w_out"], "nt", F32)
    gw["w_out"] = _matmul(tag + "_out_dw", s["merged"], dmix, "tn", F32)
    (dga, dgb, dgc, dya, dyb, dyc), _, _ = _rowwise_bwd(tag + "_merge_b", _f_merge, R, nct, s["mg_rows"], [], [], [], [],
                                                        [(dmerged, D)], [BF16] * 6)
    dgm = _matmul(tag + "_br_a_da", dya, w["w_br_a"], "nt", F32)
    datt = _matmul(tag + "_br_b_da", dyb, w["w_br_b"], "nt", F32)
    dgla = _matmul(tag + "_br_c_da", dyc, w["w_br_c"], "nt", F32)
    gm_rows, gm_params = s["gm_info"]
    gw["w_br_a"] = _matmul(tag + "_br_a_dw", s["gm"], dya, "tn", F32)
    gw["w_br_b"] = _matmul(tag + "_br_b_dw", s["att"], dyb, "tn", F32)
    gw["w_br_c"] = _matmul(tag + "_br_c_dw", s["gla"], dyc, "tn", F32)
    rg = GMLP_CHUNK
    (du, dv_), (gw["gmlp_norm_g"], gw["w_spatial"], gw["b_spatial_t"]), _ = _rowwise_bwd(
        tag + "_gmlp_b", _f_gmlp, rg, tc // rg, gm_rows, [], gm_params, [], [], [(dgm, GMLP_W)], [BF16, BF16])
    (do, dr), (gw["gla_norm_g"],), _ = _rowwise_bwd(tag + "_glaout_b", _f_glaout, R, nct, s["go_rows"], [],
                                                    [(w["gla_norm_g"], True)], [], [], [(dgla, GLA_V_W)], [F32, BF16])
    p = s["p"]
    *d_b, gw["w2p_b"], gw["b2_b"] = _gla_dir_bwd(tag + "_gla_b_b", p, w["w2p_b"], w["b2_b"], s["s_b"], do, rev=True, R=R, nct=nct)
    dgq, dgk, dgv_, daf, gw["w2p_f"], gw["b2_f"] = _gla_dir_bwd(tag + "_gla_f_b", p, w["w2p_f"], w["b2_f"], s["s_f"], do,
                                                              rev=False, R=R, nct=nct, adds=d_b, out_dtype=BF16)
    dqr, dkr, dvv, comm_out = _attention_bwd(tag, s["att_saved"], datt, tc, make_comm(gw))
    qk_rows, qk_consts, qk_params, qk_cparams = s["qk"]
    (dq, dk), (gw["q_norm_g"], gw["k_norm_g"]), _ = _rowwise_bwd(
        tag + "_qk_b", _f_qknorm, R, nct, qk_rows, qk_consts, qk_params, qk_cparams, [],
        [(dqr, Q_W), (dkr, KV_W)], [BF16, BF16], pre=_qk_pre)
    dp = jnp.concatenate([dga, dgb, dgc, du, dv_, dq, dgv_, dr, dgq, dgk, dk, dvv.astype(BF16), daf,
                          jnp.zeros((p.shape[0], PW - C_AF - 128), BF16)], axis=1)
    dh1 = _matmul(tag + "_in_da", dp, w["w_in"], "nt", F32)
    gw["w_in"] = _matmul(tag + "_in_dw", s["h1"], dp, "tn", F32)
    (dx,), (gw["norm1_g"],), (dsh1, dsc1) = _rowwise_bwd(
        tag + "_ln1_b", _f_lnmod, R, nct, [(s["x"], D, 0, D)], [], [(w["norm1_g"], False)], [], [sh1, sc1],
        [(dh1, D)], [F32], adds=[dx_mid])
    return dx, gw, (dsh1, dsc1, dg1, dsh2, dsc2, dg2), comm_out


_SHARDED = (("w_in", 1, True), ("w_br_a", 1, True), ("w_br_b", 1, True), ("w_br_c", 1, True), ("w_out", 0, True),
            ("w_ffn_up", 1, True), ("w_ffn_down", 0, True), ("conv_w", 1, False), ("w_alpha2", 2, False), ("b_alpha", 1, False))
_REPLICATED = ("c_ctx", "b_ada", "norm1_g", "norm2_g", "q_norm_g", "k_norm_g", "gmlp_norm_g", "w_spatial", "b_spatial",
               "gla_norm_g", "conv_b", "final_norm_g")
_WEIGHTS = ("c_ctx", "w_ada", "b_ada", "norm1_g", "norm2_g", "w_in", "q_norm_g", "k_norm_g", "gmlp_norm_g", "w_spatial",
            "b_spatial", "w_alpha2", "b_alpha", "gla_norm_g", "w_br_a", "w_br_b", "w_br_c", "w_out", "w_ffn_up", "conv_w",
            "conv_b", "w_ffn_down", "final_norm_g")


def _decay_weights(w_alpha2_l, b_alpha_l):
    out = []
    for d in range(2):
        w2p = jnp.zeros((128, GLA_QK_W), F32).at[GLA_RANK * d:GLA_RANK * (d + 1)].set(w_alpha2_l[d])
        out += [w2p, b_alpha_l[d][None, :]]
    return out


def _step(inp, wts, moms, vels):
    x, c, ctx, loss_target = inp
    t, tc = x.shape[1], ctx.shape[1]
    n = t + tc
    R = min(256, tc)
    nct = tc // R
    me = _my_index()
    depth = wts["w_in"].shape[0]

    late = [(nm, ax) for nm, ax, half in _SHARDED if half and nm != "w_in"]
    small = [(nm, ax) for nm, ax, half in _SHARDED if not half]
    w_in_shard = lambda l: wts["w_in"][l].astype(BF16)
    c8 = jnp.concatenate([c, jnp.zeros((7, D), F32)], axis=0)
    first = _exchange("gather_first", [w_in_shard(0)] + [wts[nm] for nm, _ in small] + [c8], gather=True)
    c_all = first[-1][:, 0, :]
    small_all = dict(zip([nm for nm, _ in small], first[1:-1]))

    def early_weights(l, w_in_all):
        w = {"w_in": _win_my_cols([w_in_all[s] for s in range(N_DEV)])}
        conv_w, w_alpha2, b_alpha = [jnp.concatenate([small_all[nm][s, l] for s in range(N_DEV)], axis=ax) for nm, ax in small]
        w["conv_w_h"] = conv_w.reshape(3, 2, FFN_H).transpose(1, 0, 2)
        w["conv_b_h"] = wts["conv_b"][l].reshape(2, 1, FFN_H)
        w["w2p_f"], w["b2_f"], w["w2p_b"], w["b2_b"] = _decay_weights(w_alpha2, b_alpha)
        w["norm1_g"] = wts["norm1_g"][l][None, :]
        w["norm2_g"] = wts["norm2_g"][l][None, :]
        w["q_norm_g"] = wts["q_norm_g"][l][None, :]
        w["k_norm_g"] = wts["k_norm_g"][l][None, :]
        w["gmlp_norm_g"] = wts["gmlp_norm_g"][l].reshape(GMLP_GROUPS, 1, GMLP_W // GMLP_GROUPS)
        w["w_spatial"] = wts["w_spatial"][l]
        w["b_spatial_t"] = wts["b_spatial"][l][:, :, None]
        w["gla_norm_g"] = wts["gla_norm_g"][l].reshape(GLA_HEADS, 1, GLA_DV)
        return w

    craw16 =jnp.concatenate([c_all, wts["c_ctx"][None, :], jnp.zeros((7, D), F32)], axis=0)
    acols = wts["w_ada"].shape[2]
    b_cols = lax.dynamic_slice_in_dim(wts["b_ada"], me * acols, acols, axis=1)[:, None, :]
    mod_part = _adaln_fwd("adaln", craw16, wts["w_ada"], b_cols)
    send = jnp.stack([mod_part[:, 8, :][None].repeat(N_DEV, 0), mod_part[:, :8, :].transpose(1, 0, 2)], axis=2)
    send = jnp.concatenate([send.reshape(N_DEV, 2 * depth, acols), jnp.zeros((N_DEV, 8 - 2 * depth, acols), F32)], axis=1)
    (got,) = _exchange("scatter_mod", [send], gather=False)
    mod = got[:, :2 * depth, :].transpose(1, 0, 2).reshape(depth, 2, N_MOD, 1, D)
    modv = [[mod[l, :, k] for k in range(N_MOD)] for l in range(depth)]

    cosq, sinq = _rope_tables(t, tc)
    consts = (cosq, sinq, _head_mean_matrix(Q_W), _head_tile_matrix(Q_W), _head_mean_matrix(KV_W), _head_tile_matrix(KV_W))
    xs = jnp.concatenate([ctx[0], x[0]], axis=0)
    saved, layers = [], []
    w_in_all = first[0]

    def late_weights(got):
        return {nm: jnp.concatenate([g[s] for s in range(N_DEV)], axis=ax) for (nm, ax), g in zip(late, got)}

    for l in range(depth):
        layers.append(early_weights(l, w_in_all))
        sending = [wts[nm][l].astype(BF16) for nm, _ in late] + ([w_in_shard(l + 1)] if l + 1 < depth else [])
        xs, sv, got = _layer_fwd("l%d" % l, xs, layers[l], modv[l], consts, R, nct, tc, (sending, True), late_weights)
        if l + 1 < depth:
            w_in_all = got[len(late)]
        saved.append(sv)
    loss_blk, dxs, dgf = _final_loss("final", xs, loss_target[0], wts["final_norm_g"][None, :], R, nct)
    loss = lax.psum(loss_blk[0, 0], ("x", "y", "c"))

    grads = [None] * depth
    dmods = [None] * depth
    late_parts = [None] * depth
    w_in_parts = [None] * depth
    w_in_grad_shards = lambda l: _win_shards(grads[l]["w_in"]).astype(BF16)
    for l in range(depth - 1, -1, -1):
        def make_comm(gw, l=l):
            sending = [_split_shards(gw[nm], ax).astype(BF16) for nm, ax in late]
            return sending + ([w_in_grad_shards(l + 1)] if l + 1 < depth else []), False

        dxs, grads[l], dmods[l], got = _layer_bwd("l%d" % l, dxs, saved[l], layers[l], modv[l], R, nct, tc, make_comm)
        late_parts[l] = got[:len(late)]
        if l + 1 < depth:
            w_in_parts[l + 1] = got[len(late)]
    grad_x = dxs[tc:][None]

    dmod = jnp.stack([jnp.stack(dmods[l], axis=1) for l in range(depth)])
    dmod = dmod.reshape(depth, 2, N_DEV, acols).transpose(2, 0, 1, 3).reshape(N_DEV, 2 * depth, acols)
    dmod_send = jnp.concatenate([dmod, jnp.zeros((N_DEV, 8 - 2 * depth, acols), F32)], axis=1)
    (dm_got,) = _exchange("scatter_dmod", [dmod_send], gather=False)
    g_w_ada, dc16 = _adaln_bwd("adaln_b", craw16, wts["w_ada"], dm_got[:, :2 * depth].transpose(1, 0, 2))
    db_ada_part = jnp.stack([jnp.stack(dmods[l], axis=1) for l in range(depth)]).reshape(depth, 2, N_MOD * D).sum(axis=1)

    out = {}
    kinds = ("grad", "delta", "new_m", "new_v")
    small_send = {nm: [] for nm, _ in small}
    for l in range(depth):
        g = grads[l]
        small_g = dict(conv_w=g["conv_w_h"].transpose(1, 0, 2).reshape(3, F2),
                       w_alpha2=jnp.stack([g["w2p_f"][:GLA_RANK], g["w2p_b"][GLA_RANK:2 * GLA_RANK]]),
                       b_alpha=jnp.stack([g["b2_f"][0], g["b2_b"][0]]))
        for nm, ax in small:
            small_send[nm].append(_split_shards(small_g[nm], ax))
    last = _exchange("scatter_last", [w_in_grad_shards(0)] + [jnp.stack(small_send[nm], axis=1) for nm, _ in small], gather=False)
    w_in_parts[0] = last[0]
    view2 = lambda a: a.reshape(-1, a.shape[-1])
    sharded_parts = ([jnp.stack(w_in_parts, axis=1)]
                     + [jnp.stack([late_parts[l][k] for l in range(depth)], axis=1) for k in range(len(late))] + last[1:])
    for (nm, _), parts in zip([("w_in", 1)] + late + small, sharded_parts):
        res = _adamw("adamw_" + nm, view2(wts[nm]), view2(moms[nm]), view2(vels[nm]), parts.reshape(N_DEV, -1, parts.shape[-1]))
        for kind, flat in zip(kinds, res):
            out[kind, nm] = flat.reshape(wts[nm].shape)

    rep_g = dict(
        c_ctx=dc16[8], b_ada=db_ada_part, final_norm_g=dgf[0],
        norm1_g=jnp.stack([grads[l]["norm1_g"][0] for l in range(depth)]),
        norm2_g=jnp.stack([grads[l]["norm2_g"][0] for l in range(depth)]),
        q_norm_g=jnp.stack([grads[l]["q_norm_g"][0] for l in range(depth)]),
        k_norm_g=jnp.stack([grads[l]["k_norm_g"][0] for l in range(depth)]),
        gmlp_norm_g=jnp.stack([grads[l]["gmlp_norm_g"].reshape(GMLP_W) for l in range(depth)]),
        w_spatial=jnp.stack([grads[l]["w_spatial"] for l in range(depth)]),
        b_spatial=jnp.stack([grads[l]["b_spatial_t"][:, :, 0] for l in range(depth)]),
        gla_norm_g=jnp.stack([grads[l]["gla_norm_g"].reshape(GLA_V_W) for l in range(depth)]),
        conv_b=jnp.stack([grads[l]["conv_b_h"].reshape(F2) for l in range(depth)]),
    )
    rep_shapes = [wts[nm].shape for nm in _REPLICATED]
    (rg_parts,) = _exchange("gather_rep_grads", [_pack_rows([rep_g[nm] for nm in _REPLICATED], 16)], gather=True)
    rpk = lambda src: _pack_rows([src[nm] for nm in _REPLICATED], 16)
    res = _adamw("adamw_rep", rpk(wts), rpk(moms), rpk(vels), rg_parts)
    for kind, rows in zip(kinds, res):
        for nm, piece in zip(_REPLICATED, _unpack_rows(rows, rep_shapes)):
            out[kind, nm] = piece

    res = _adamw("adamw_ada", view2(wts["w_ada"]), view2(moms["w_ada"]), view2(vels["w_ada"]), view2(g_w_ada)[None])
    for kind, flat in zip(kinds, res):
        out[kind, "w_ada"] = flat.reshape(wts["w_ada"].shape)

    return (loss, grad_x, *[out[kind, nm] for kind in kinds for nm in _WEIGHTS])


def kernel(x, c, ctx, c_ctx, w_ada, b_ada, norm1_g, norm2_g, w_in, q_norm_g, k_norm_g, gmlp_norm_g, w_spatial, b_spatial, w_alpha2, b_alpha, gla_norm_g, w_br_a, w_br_b, w_br_c, w_out, w_ffn_up, conv_w, conv_b, w_ffn_down, final_norm_g, loss_target, m_c_ctx, m_w_ada, m_b_ada, m_norm1_g, m_norm2_g, m_w_in, m_q_norm_g, m_k_norm_g, m_gmlp_norm_g, m_w_spatial, m_b_spatial, m_w_alpha2, m_b_alpha, m_gla_norm_g, m_w_br_a, m_w_br_b, m_w_br_c, m_w_out, m_w_ffn_up, m_conv_w, m_conv_b, m_w_ffn_down, m_final_norm_g, v_c_ctx, v_w_ada, v_b_ada, v_norm1_g, v_norm2_g, v_w_in, v_q_norm_g, v_k_norm_g, v_gmlp_norm_g, v_w_spatial, v_b_spatial, v_w_alpha2, v_b_alpha, v_gla_norm_g, v_w_br_a, v_w_br_b, v_w_br_c, v_w_out, v_w_ffn_up, v_conv_w, v_conv_b, v_w_ffn_down, v_final_norm_g):
    wts = dict(zip(_WEIGHTS, (c_ctx, w_ada, b_ada, norm1_g, norm2_g, w_in, q_norm_g, k_norm_g, gmlp_norm_g, w_spatial, b_spatial,
                              w_alpha2, b_alpha, gla_norm_g, w_br_a, w_br_b, w_br_c, w_out, w_ffn_up, conv_w, conv_b, w_ffn_down,
                              final_norm_g)))
    moms = dict(zip(_WEIGHTS, (m_c_ctx, m_w_ada, m_b_ada, m_norm1_g, m_norm2_g, m_w_in, m_q_norm_g, m_k_norm_g, m_gmlp_norm_g,
                               m_w_spatial, m_b_spatial, m_w_alpha2, m_b_alpha, m_gla_norm_g, m_w_br_a, m_w_br_b, m_w_br_c, m_w_out,
                               m_w_ffn_up, m_conv_w, m_conv_b, m_w_ffn_down, m_final_norm_g)))
    vels = dict(zip(_WEIGHTS, (v_c_ctx, v_w_ada, v_b_ada, v_norm1_g, v_norm2_g, v_w_in, v_q_norm_g, v_k_norm_g, v_gmlp_norm_g,
                               v_w_spatial, v_b_spatial, v_w_alpha2, v_b_alpha, v_gla_norm_g, v_w_br_a, v_w_br_b, v_w_br_c, v_w_out,
                               v_w_ffn_up, v_conv_w, v_conv_b, v_w_ffn_down, v_final_norm_g)))
    return _step((x, c, ctx, loss_target), wts, moms, vels)
```

```python
import functools

import jax
import jax.numpy as jnp
from jax import lax
from jax.experimental import pallas as pl
from jax.experimental.pallas import tpu as pltpu

F32 = jnp.float32
BF16 = jnp.bfloat16
HI = lax.Precision.HIGHEST
MESH_ID = pl.DeviceIdType.MESH

N_DEV = 8
EPS = 1e-6
D = 1024
N_MOD = 6
HEAD_DIM = 64
N_Q_HEADS = 8
N_KV_HEADS = 2
Q_GROUP = 4
Q_W = 512
KV_W = 128
GRID_W = 64
ROPE_THETA = 10000.0
ROPE_FREQS = 16
GMLP_CHUNK = 128
GMLP_GROUPS = 4
GMLP_W = 512
GLA_HEADS = 4
GLA_QK_W = 256
GLA_V_W = 512
GLA_DK = 64
GLA_DV = 128
GLA_RANK = 16
GLA_TAU = 16.0
GLA_CHUNK = 64
FFN_H = 2816
F2 = 2 * FFN_H
IN_SPLITS = (512, 512, 512, 128, 128, 256, 256, 512, 16, 16, 512, 1024, 1024, 1024)
IN_WIDTH = sum(IN_SPLITS)

C_GA, C_GB, C_GC = 0, 1024, 2048
C_U, C_V, C_Q, C_GV, C_R = 3072, 3584, 4096, 4608, 5120
C_GQ, C_GK = 5632, 5888
C_K, C_VV, C_AF = 6144, 6272, 6400
PW = 6656

ADAM_LR = 0.001
ADAM_B1 = 0.9
ADAM_B2 = 0.999
ADAM_EPS = 1e-08
ADAM_WD = 0.01
ADAM_STEP = 10

V7X_VMEM_LIMIT = 56 * 1024 * 1024

_ARB1 = pltpu.CompilerParams(dimension_semantics=("arbitrary",), vmem_limit_bytes=V7X_VMEM_LIMIT)


def _pick(dim, prefs):
    for p in prefs:
        if dim % p == 0:
            return p
    return dim


def _hdot(a, b, dims=(((1,), (0,)), ((), ()))):
    return lax.dot_general(a, b, dims, precision=HI, preferred_element_type=F32)


_NT = (((1,), (1,)), ((), ()))
_TN = (((0,), (0,)), ((), ()))
_NN = (((1,), (0,)), ((), ()))


def _matmul(name, a, b, mode, out_dtype, *, a_halves=False, b_halves=False, o_halves=False):
    def dims2(x, halves):
        return (x.shape[1], 2 * x.shape[2]) if halves else x.shape

    ar, ac = dims2(a, a_halves)
    br, bc = dims2(b, b_halves)
    if mode == "nn":
        M, K, N = ar, ac, bc
    elif mode == "nt":
        M, K, N = ar, ac, br
    else:
        M, K, N = ac, ar, bc
    row_prefs = (768, 512, 384, 256, 128)
    n_unit = N // 2 if (o_halves or (b_halves and mode != "nt")) else N
    k_unit = K // 2 if (a_halves and mode != "tn") else K
    if mode == "tn":
        tm = _pick(M, (1024, 1408, 512, 256, 128))
        tk = _pick(K, (1408,) + row_prefs)
    else:
        tm = _pick(M, row_prefs)
        tk = _pick(k_unit, (1664, 1408, 1024, 512, 256, 128))
    tn = _pick(n_unit, (1664, 1408, 1024, 512, 256, 128))
    nk = K // tk

    def spec(shape2, halves, blk, imap):
        if not halves:
            return pl.BlockSpec(blk, imap)
        nhalf = (shape2[1] // 2) // blk[1]

        def im(i, j, k):
            r, c = imap(i, j, k)
            return (c // nhalf, r, c % nhalf)
        return pl.BlockSpec((None,) + blk, im)

    if mode == "nn":
        a_spec = spec((ar, ac), a_halves, (tm, tk), lambda i, j, k: (i, k))
        b_spec = spec((br, bc), b_halves, (tk, tn), lambda i, j, k: (k, j))
        dn = _NN
    elif mode == "nt":
        a_spec = spec((ar, ac), a_halves, (tm, tk), lambda i, j, k: (i, k))
        b_spec = spec((br, bc), b_halves, (tn, tk), lambda i, j, k: (j, k))
        dn = _NT
    else:
        a_spec = spec((ar, ac), a_halves, (tk, tm), lambda i, j, k: (k, i))
        b_spec = spec((br, bc), b_halves, (tk, tn), lambda i, j, k: (k, j))
        dn = _TN
    o_spec = spec((M, N), o_halves, (tm, tn), lambda i, j, k: (i, j))
    o_shape = (2, M, N // 2) if o_halves else (M, N)

    def body(a_ref, b_ref, o_ref, acc_ref):
        k = pl.program_id(2)
        part = lax.dot_general(a_ref[...], b_ref[...], dn, preferred_element_type=F32)
        if nk == 1:
            o_ref[...] = part.astype(o_ref.dtype)
        else:
            @pl.when(k == 0)
            def _():
                acc_ref[...] = part

            @pl.when(k > 0)
            def _():
                acc_ref[...] += part

            @pl.when(k == nk - 1)
            def _():
                o_ref[...] = acc_ref[...].astype(o_ref.dtype)

    return pl.pallas_call(
        body, name=name, grid=(M // tm, N // tn, nk),
        in_specs=[a_spec, b_spec], out_specs=o_spec,
        out_shape=jax.ShapeDtypeStruct(o_shape, out_dtype),
        scratch_shapes=[pltpu.VMEM((tm, tn), F32)],
        compiler_params=pltpu.CompilerParams(dimension_semantics=("parallel", "parallel", "arbitrary"),
                                             vmem_limit_bytes=V7X_VMEM_LIMIT),
    )(a, b)


def _full_spec(shape):
    nd = len(shape)
    return pl.BlockSpec(tuple(shape), lambda i, _nd=nd: (0,) * _nd)


def _row_spec(R, W, cb):
    return pl.BlockSpec((R, W), lambda i, _cb=cb: (i, _cb))


def _load_rows(refs, specs):
    vals = []
    for ref, (_, W, _, pw) in zip(refs, specs):
        if pw == W:
            vals.append(ref[...].astype(F32))
        else:
            vals.append([ref[:, k * pw:(k + 1) * pw].astype(F32) for k in range(W // pw)])
    return vals


def _load_params(refs, specs):
    vals = []
    for ref, (arr, split) in zip(refs, specs):
        if split:
            vals.append([ref[k] for k in range(arr.shape[0])])
        else:
            vals.append(ref[...])
    return vals


def _mod_spec(nct, width):
    return pl.BlockSpec((None, 1, width), lambda i: (jnp.minimum(i // nct, 1), 0, 0))


def _rowwise(name, f, R, nct, rows, consts, params, cparams, mods, outs, post=None):
    n = rows[0][0].shape[0]
    nr, nc, npar, ncp, nm = len(rows), len(consts), len(params), len(cparams), len(mods)

    def body(*refs):
        pos = 0
        rr = refs[pos:pos + nr]; pos += nr
        cr = refs[pos:pos + nc]; pos += nc
        pr = refs[pos:pos + npar]; pos += npar
        cpr = refs[pos:pos + ncp]; pos += ncp
        mr = refs[pos:pos + nm]; pos += nm
        orefs = refs[pos:]
        res = f(_load_rows(rr, rows), _load_params(pr, params), [m[...] for m in mr],
                _load_rows(cr, consts), _load_params(cpr, cparams))
        if post is not None:
            res = post(res, _load_rows(cr, consts))
        for o_ref, r in zip(orefs, res):
            o_ref[...] = r.astype(o_ref.dtype)

    in_specs = ([_row_spec(R, W, cb) for (_, W, cb, _) in rows + consts]
                + [_full_spec(a.shape) for (a, _) in params + cparams]
                + [_mod_spec(nct, m.shape[2]) for m in mods])
    args = [a for (a, _, _, _) in rows + consts] + [a for (a, _) in params + cparams] + list(mods)
    return pl.pallas_call(
        body, name=name, grid=(n // R,), in_specs=in_specs,
        out_specs=[_row_spec(R, w, 0) for (w, _) in outs],
        out_shape=[jax.ShapeDtypeStruct((n, w), dt) for (w, dt) in outs],
        compiler_params=_ARB1,
    )(*args)


def _rowwise_bwd(name, f, R, nct, rows, consts, params, cparams, mods, douts, drow, adds=None, pre=None):
    n = rows[0][0].shape[0]
    adds = adds or [None] * len(rows)
    nr, nc, npar, ncp, nm, nd = len(rows), len(consts), len(params), len(cparams), len(mods), len(douts)
    add_ix = [k for k in range(nr) if adds[k] is not None]
    out_ix = [k for k in range(nr) if drow[k] is not None]

    def body(*refs):
        i = pl.program_id(0)
        pos = 0
        rr = refs[pos:pos + nr]; pos += nr
        cr = refs[pos:pos + nc]; pos += nc
        pr = refs[pos:pos + npar]; pos += npar
        cpr = refs[pos:pos + ncp]; pos += ncp
        mr = refs[pos:pos + nm]; pos += nm
        dr = refs[pos:pos + nd]; pos += nd
        ar = refs[pos:pos + len(add_ix)]; pos += len(add_ix)
        drr = refs[pos:pos + len(out_ix)]; pos += len(out_ix)
        dpr = refs[pos:pos + npar]; pos += npar
        dmr = refs[pos:pos + nm]; pos += nm

        cv = _load_rows(cr, consts)
        cpv = _load_params(cpr, cparams)
        _, vjp = jax.vjp(lambda rv, pv, mv: f(rv, pv, mv, cv, cpv),
                         _load_rows(rr, rows), _load_params(pr, params), [m[...] for m in mr])
        dv = [d[...].astype(F32) for d in dr]
        if pre is not None:
            dv = pre(dv, cv)
        g_rows, g_params, g_mods = vjp(tuple(dv))

        for ref, k in zip(drr, out_ix):
            _, W, _, pw = rows[k]
            g = g_rows[k]
            extra = ar[add_ix.index(k)] if k in add_ix else None
            if pw == W:
                if extra is not None:
                    g = g + extra[...].astype(F32)
                ref[...] = g.astype(ref.dtype)
            else:
                for q in range(W // pw):
                    gq = g[q]
                    if extra is not None:
                        gq = gq + extra[:, q * pw:(q + 1) * pw].astype(F32)
                    ref[:, q * pw:(q + 1) * pw] = gq.astype(ref.dtype)

        @pl.when(i == 0)
        def _():
            for ref in dpr:
                ref[...] = jnp.zeros_like(ref)

        for ref, (arr, split), g in zip(dpr, params, g_params):
            if split:
                for k in range(arr.shape[0]):
                    ref[k] += g[k]
            else:
                ref[...] += g

        @pl.when((i == 0) | (i == nct))
        def _():
            for ref in dmr:
                ref[...] = jnp.zeros_like(ref)

        for ref, g in zip(dmr, g_mods):
            ref[...] += g

    in_specs = ([_row_spec(R, W, cb) for (_, W, cb, _) in rows + consts]
                + [_full_spec(a.shape) for (a, _) in params + cparams]
                + [_mod_spec(nct, m.shape[2]) for m in mods]
                + [_row_spec(R, W, 0) for (_, W) in douts]
                + [_row_spec(R, rows[k][1], 0) for k in add_ix])
    args = ([a for (a, _, _, _) in rows + consts] + [a for (a, _) in params + cparams] + list(mods)
            + [a for (a, _) in douts] + [adds[k] for k in add_ix])
    out_specs = ([_row_spec(R, rows[k][1], 0) for k in out_ix]
                 + [_full_spec(a.shape) for (a, _) in params]
                 + [_mod_spec(nct, m.shape[2]) for m in mods])
    out_shape = ([jax.ShapeDtypeStruct((n, rows[k][1]), drow[k]) for k in out_ix]
                 + [jax.ShapeDtypeStruct(a.shape, F32) for (a, _) in params]
                 + [jax.ShapeDtypeStruct(m.shape, F32) for m in mods])
    res = pl.pallas_call(
        body, name=name, grid=(n // R,), in_specs=in_specs, out_specs=out_specs, out_shape=out_shape,
        compiler_params=_ARB1,
    )(*args)
    no = len(out_ix)
    return list(res[:no]), list(res[no:no + npar]), list(res[no + npar:])


def _rms(x, g):
    return x * lax.rsqrt(jnp.mean(x * x, axis=-1, keepdims=True) + EPS) * g


def _f_lnmod(rv, pv, mv, cv, cpv):
    (x,), (g,), (shift, scale) = rv, pv, mv
    return (_rms(x, g) * (1.0 + scale) + shift,)


def _f_resid(rv, pv, mv, cv, cpv):
    (x, y), (gate,) = rv, mv
    return (x + gate * y,)


def _f_merge(rv, pv, mv, cv, cpv):
    ga, gb, gc, ya, yb, yc = rv
    return (_sigmoid(ga) * ya + _sigmoid(gb) * yb + _sigmoid(gc) * yc,)


def _split3(x):
    hi = x.astype(BF16)
    rest = x - hi.astype(F32)
    mid = rest.astype(BF16)
    return hi, mid, (rest - mid.astype(F32)).astype(BF16)


def _dot3_right(x, m):
    mb = m.astype(BF16)
    return sum(lax.dot_general(piece, mb, _NN, preferred_element_type=F32) for piece in _split3(x))


@jax.custom_vjp
def _sym_dot(x, m):
    return _dot3_right(x, m)


_sym_dot.defvjp(lambda x, m: (_dot3_right(x, m), m), lambda m, g: (_dot3_right(g, m), jnp.zeros_like(m)))


def _f_qknorm(rv, pv, mv, cv, cpv):
    (q, k), (gq, gk), (bdq, eq, bdk, ek) = rv, pv, cpv
    qn = q * lax.rsqrt(_sym_dot(q * q, bdq) + EPS) * _hdot(gq, eq)
    kn = k * lax.rsqrt(_sym_dot(k * k, bdk) + EPS) * _hdot(gk, ek)
    return (qn, kn)


def _rope(x, cos, sin):
    w = x.shape[1]
    lane = lax.broadcasted_iota(jnp.int32, x.shape, 1)
    partner = jnp.where((lane & 31) < 16, pltpu.roll(x, w - 16, 1), pltpu.roll(x, 16, 1))
    return x * cos + partner * sin


def _qk_post(res, cv):
    (qn, kn), (cq, sq, ck, sk) = res, cv
    return (_rope(qn, cq, sq) * (HEAD_DIM ** -0.5), _rope(kn, ck, sk))


def _qk_pre(dv, cv):
    (dq, dk), (cq, sq, ck, sk) = dv, cv
    return (_rope(dq * (HEAD_DIM ** -0.5), cq, -sq), _rope(dk, ck, -sk))


def _f_gmlp(rv, pv, mv, cv, cpv):
    (u, vs), (ng, ws, bt) = rv, pv
    pieces = []
    for g in range(GMLP_GROUPS):
        vn = _rms(jax.nn.gelu(vs[g]), ng[g])
        pieces.append(_bdot(ws[g], vn, "nn") + bt[g])
    return (jax.nn.gelu(u) * jnp.concatenate(pieces, axis=1),)


def _f_glaout(rv, pv, mv, cv, cpv):
    (os_, rs), (gn,) = rv, pv
    pieces = [_rms(os_[h], gn[h]) * (rs[h] * _sigmoid(rs[h])) for h in range(GLA_HEADS)]
    return (jnp.concatenate(pieces, axis=1),)


_CONV_CB = 1408


def _sigmoid(x):
    return 0.5 * jnp.tanh(0.5 * x) + 0.5


def _halo_keep(i, R, tc, n):
    first, end = i * R, (i + 1) * R
    keep_prev = jnp.where((first == 0) | (first == tc), 0.0, 1.0)
    keep_next = jnp.where((end == tc) | (end == n), 0.0, 1.0)
    return keep_prev, keep_next


def _conv_specs(R, n):
    nb8 = n // 8
    main = pl.BlockSpec((2, R, _CONV_CB), lambda j, i: (0, i, j))
    prev = pl.BlockSpec((2, 8, _CONV_CB), lambda j, i: (0, jnp.maximum(i * (R // 8) - 1, 0), j))
    nxt = pl.BlockSpec((2, 8, _CONV_CB), lambda j, i: (0, jnp.minimum((i + 1) * (R // 8), nb8 - 1), j))
    cw = pl.BlockSpec((2, 3, _CONV_CB), lambda j, i: (0, 0, j))
    cb = pl.BlockSpec((2, 1, _CONV_CB), lambda j, i: (0, 0, j))
    return main, prev, nxt, cw, cb


def _conv_act(ext_ref, h, first, count, cw_ref, cb_ref):
    prev = ext_ref[h, pl.ds(first - 1, count), :]
    mid = ext_ref[h, pl.ds(first, count), :]
    nxt = ext_ref[h, pl.ds(first + 1, count), :]
    return cb_ref[h] + cw_ref[h, 1:2, :] * mid + cw_ref[h, 0:1, :] * prev + cw_ref[h, 2:3, :] * nxt


def _conv_gate(name, a2, cw, cb, R, tc):
    n = a2.shape[1]
    main, prev, nxt, cws, cbs = _conv_specs(R, n)

    def body(a_ref, p_ref, n_ref, cw_ref, cb_ref, o_ref, ext):
        i = pl.program_id(1)
        keep_prev, keep_next = _halo_keep(i, R, tc, n)
        ext[:, 0:8, :] = jnp.zeros((2, 8, _CONV_CB), F32)
        ext[:, 8:16, :] = p_ref[...] * keep_prev
        ext[:, 16:16 + R, :] = a_ref[...]
        ext[:, 16 + R:24 + R, :] = n_ref[...] * keep_next
        ext[:, 24 + R:32 + R, :] = jnp.zeros((2, 8, _CONV_CB), F32)
        g = _conv_act(ext, 0, 16, R, cw_ref, cb_ref)
        v = _conv_act(ext, 1, 16, R, cw_ref, cb_ref)
        o_ref[...] = (g * _sigmoid(g) * v).astype(o_ref.dtype)

    return pl.pallas_call(
        body, name=name, grid=(FFN_H // _CONV_CB, n // R),
        in_specs=[main, prev, nxt, cws, cbs],
        out_specs=pl.BlockSpec((R, _CONV_CB), lambda j, i: (i, j)),
        out_shape=jax.ShapeDtypeStruct((n, FFN_H), BF16),
        scratch_shapes=[pltpu.VMEM((2, R + 32, _CONV_CB), F32)],
        compiler_params=pltpu.CompilerParams(dimension_semantics=("arbitrary", "arbitrary"),
                                             vmem_limit_bytes=V7X_VMEM_LIMIT),
    )(a2, a2, a2, cw, cb)


def _conv_gate_bwd(name, a2, dgv, cw, cb, R, tc):
    n = a2.shape[1]
    nb8 = n // 8
    main, prev, nxt, cws, cbs = _conv_specs(R, n)
    d_main = pl.BlockSpec((R, _CONV_CB), lambda j, i: (i, j))
    d_prev = pl.BlockSpec((8, _CONV_CB), lambda j, i: (jnp.maximum(i * (R // 8) - 1, 0), j))
    d_next = pl.BlockSpec((8, _CONV_CB), lambda j, i: (jnp.minimum((i + 1) * (R // 8), nb8 - 1), j))
    RE = R + 16

    def body(a_ref, p_ref, n_ref, cw_ref, cb_ref, d_ref, dp_ref, dn_ref, da_ref, dcw_ref, dcb_ref, ext, dext):
        i = pl.program_id(1)
        keep_prev, keep_next = _halo_keep(i, R, tc, n)
        zeros8 = jnp.zeros((2, 8, _CONV_CB), F32)
        ext[:, 0:8, :] = zeros8
        ext[:, 8:16, :] = p_ref[...] * keep_prev
        ext[:, 16:16 + R, :] = a_ref[...]
        ext[:, 16 + R:24 + R, :] = n_ref[...] * keep_next
        ext[:, 24 + R:32 + R, :] = zeros8
        g = _conv_act(ext, 0, 8, RE, cw_ref, cb_ref)
        v = _conv_act(ext, 1, 8, RE, cw_ref, cb_ref)
        dout = jnp.concatenate([dp_ref[...].astype(F32) * keep_prev, d_ref[...].astype(F32),
                                dn_ref[...].astype(F32) * keep_next], axis=0)
        sg = _sigmoid(g)
        dext[:, 0:8, :] = zeros8
        dext[:, 24 + R:32 + R, :] = zeros8
        dext[0, 8:8 + RE, :] = dout * v * sg * (1.0 + g * (1.0 - sg))
        dext[1, 8:8 + RE, :] = dout * g * sg

        @pl.when(i == 0)
        def _():
            dcw_ref[...] = jnp.zeros_like(dcw_ref)
            dcb_ref[...] = jnp.zeros_like(dcb_ref)

        for h in range(2):
            da = dext[h, 16:16 + R, :]
            da_up = dext[h, 17:17 + R, :]
            da_dn = dext[h, 15:15 + R, :]
            da_ref[h] = (cw_ref[h, 1:2, :] * da + cw_ref[h, 0:1, :] * da_up + cw_ref[h, 2:3, :] * da_dn).astype(da_ref.dtype)
            a_prev = ext[h, 15:15 + R, :]
            a_mid = ext[h, 16:16 + R, :]
            a_next = ext[h, 17:17 + R, :]
            dcw_ref[h, 0:1, :] += jnp.sum(da * a_prev, axis=0, keepdims=True)
            dcw_ref[h, 1:2, :] += jnp.sum(da * a_mid, axis=0, keepdims=True)
            dcw_ref[h, 2:3, :] += jnp.sum(da * a_next, axis=0, keepdims=True)
            dcb_ref[h] += jnp.sum(da, axis=0, keepdims=True)

    return pl.pallas_call(
        body, name=name, grid=(FFN_H // _CONV_CB, n // R),
        in_specs=[main, prev, nxt, cws, cbs, d_main, d_prev, d_next],
        out_specs=[main, cws, cbs],
        out_shape=[jax.ShapeDtypeStruct((2, n, FFN_H), BF16), jax.ShapeDtypeStruct((2, 3, FFN_H), F32),
                   jax.ShapeDtypeStruct((2, 1, FFN_H), F32)],
        scratch_shapes=[pltpu.VMEM((2, R + 32, _CONV_CB), F32), pltpu.VMEM((2, R + 32, _CONV_CB), F32)],
        compiler_params=pltpu.CompilerParams(dimension_semantics=("arbitrary", "arbitrary"),
                                             vmem_limit_bytes=V7X_VMEM_LIMIT),
    )(a2, a2, a2, cw, cb, dgv, dgv, dgv)


_ARB2 = pltpu.CompilerParams(dimension_semantics=("arbitrary", "arbitrary"), vmem_limit_bytes=V7X_VMEM_LIMIT)


def _with_exchange(body, n_in, n_out, grid, comm):
    if comm is None:
        return body, [], [], []
    xs, gather = comm
    na = len(xs)
    specs, shapes, sems = _exchange_io(xs, gather)

    def wrapped(*refs):
        ins, x_refs = refs[:n_in], refs[n_in:n_in + na]
        outs, o_refs = refs[n_in + na:n_in + na + n_out], refs[n_in + na + n_out:n_in + 2 * na + n_out]
        scratch, sem_refs = refs[n_in + 2 * na + n_out:-3], refs[-3:]
        start, wait = _exchange_plan(x_refs, o_refs, *sem_refs, gather)
        ids = [pl.program_id(d) for d in range(len(grid))]
        first, last = ids[0] == 0, ids[0] == grid[0] - 1
        for d in range(1, len(grid)):
            first, last = first & (ids[d] == 0), last & (ids[d] == grid[d] - 1)

        @pl.when(first)
        def _():
            start()

        body(*ins, *outs, *scratch)

        @pl.when(last)
        def _():
            wait()

    return wrapped, specs, shapes, sems


def _attn_fwd(name, qt8, k2, vt2, comm=None):
    nq, nk = qt8.shape[2], k2.shape[1]
    tq = _pick(nq, (256, 128))
    tk = _pick(nk, (768, 256, 128))

    va = vt2.shape[1]

    def body(qt_ref, k_ref, vt_ref, ot_ref, lse_ref, m_ref, acc_ref):
        m_ref[...] = jnp.full((Q_GROUP, 1, tq), -1e30, F32)
        acc_ref[...] = jnp.zeros((Q_GROUP, va, tq), F32)

        def step(j, carry):
            sl = pl.ds(pl.multiple_of(j * tk, tk), tk)
            kj = k_ref[sl, :]
            vtj = vt_ref[:, sl]
            sts = [lax.dot_general(kj, qt_ref[h], _NN, preferred_element_type=F32) for h in range(Q_GROUP)]
            m_old = [m_ref[h] for h in range(Q_GROUP)]
            acc_old = [acc_ref[h] for h in range(Q_GROUP)]
            m_new = [jnp.maximum(m_old[h], jnp.max(sts[h], axis=0, keepdims=True)) for h in range(Q_GROUP)]
            pts = [jnp.exp(sts[h] - m_new[h]).astype(BF16) for h in range(Q_GROUP)]
            pvs = [lax.dot_general(vtj, pts[h], _NN, preferred_element_type=F32) for h in range(Q_GROUP)]
            for h in range(Q_GROUP):
                acc_ref[h] = jnp.exp(m_old[h] - m_new[h]) * acc_old[h] + pvs[h]
                m_ref[h] = m_new[h]
            return carry

        lax.fori_loop(0, nk // tk, step, 0, unroll=2)
        for h in range(Q_GROUP):
            l = acc_ref[h, HEAD_DIM:HEAD_DIM + 1, :]
            ot_ref[h] = (acc_ref[h, 0:HEAD_DIM, :] / l).astype(ot_ref.dtype)
            lse_ref[h] = m_ref[h] + jnp.log(l)

    qspec = pl.BlockSpec((Q_GROUP, HEAD_DIM, tq), lambda g, i: (g, 0, i))
    lspec = pl.BlockSpec((Q_GROUP, 1, tq), lambda g, i: (g, 0, i))
    grid = (N_KV_HEADS, nq // tq)
    body, xspecs, xshapes, xsems = _with_exchange(body, 3, 2, grid, comm)
    res = pl.pallas_call(
        body, name=name, grid=grid,
        in_specs=[qspec, pl.BlockSpec((None, nk, HEAD_DIM), lambda g, i: (g, 0, 0)),
                  pl.BlockSpec((None, va, nk), lambda g, i: (g, 0, 0))] + xspecs,
        out_specs=[qspec, lspec] + xspecs,
        out_shape=[jax.ShapeDtypeStruct((N_Q_HEADS, HEAD_DIM, nq), BF16), jax.ShapeDtypeStruct((N_Q_HEADS, 1, nq), F32)] + xshapes,
        scratch_shapes=[pltpu.VMEM((Q_GROUP, 1, tq), F32), pltpu.VMEM((Q_GROUP, va, tq), F32)] + xsems,
        compiler_params=_ARB2,
    )(qt8, k2, vt2, *(comm[0] if comm else []))
    return res[0], res[1], list(res[2:])


def _attn_bwd(name, qt8, k2, v2, kt2, ot8, dot8, lse, comm=None):
    nq, nk = qt8.shape[2], k2.shape[1]
    tq = _pick(nq, (256, 128))
    tk = _pick(nk, (768, 256, 128))
    ts = tk
    heads = range(Q_GROUP)

    def body(qt_ref, k_ref, v_ref, kt_ref, ot_ref, dot_ref, lse_ref, dqt_ref, dkt_ref, dvt_ref, dl_ref, dq_acc):
        @pl.when(pl.program_id(1) == 0)
        def _():
            dkt_ref[...] = jnp.zeros_like(dkt_ref)
            dvt_ref[...] = jnp.zeros_like(dvt_ref)

        for h in heads:
            dl_ref[h] = jnp.sum(dot_ref[h].astype(F32) * ot_ref[h].astype(F32), axis=0, keepdims=True)
        dq_acc[...] = jnp.zeros((Q_GROUP, HEAD_DIM, tq), F32)

        items = [(s, h) for s in range(tk // ts) for h in heads]

        def step(j, carry):
            def keys(s):
                return pl.ds(pl.multiple_of(j * tk + s * ts, ts), ts)

            def scores(item):
                s, h = item
                return (lax.dot_general(k_ref[keys(s), :], qt_ref[h], _NN, preferred_element_type=F32),
                        lax.dot_general(v_ref[keys(s), :], dot_ref[h], _NN, preferred_element_type=F32))

            nxt = scores(items[0])
            for n, (s, h) in enumerate(items):
                st, dpt = nxt
                if n + 1 < len(items):
                    nxt = scores(items[n + 1])
                pt = jnp.exp(st - lse_ref[h])
                dst = (pt * (dpt - dl_ref[h])).astype(BF16)
                dq_acc[h] += lax.dot_general(kt_ref[:, keys(s)], dst, _NN, preferred_element_type=F32)
                dv_h = lax.dot_general(dot_ref[h], pt.astype(BF16), _NT, preferred_element_type=F32)
                dk_h = lax.dot_general(qt_ref[h], dst, _NT, preferred_element_type=F32)
                dvt_s, dkt_s = (dv_h, dk_h) if h == 0 else (dvt_s + dv_h, dkt_s + dk_h)
                if h == Q_GROUP - 1:
                    dvt_ref[:, keys(s)] += dvt_s
                    dkt_ref[:, keys(s)] += dkt_s
            return carry

        lax.fori_loop(0, nk // tk, step, 0)
        dqt_ref[...] = dq_acc[...]

    tspec = pl.BlockSpec((Q_GROUP, HEAD_DIM, tq), lambda g, i: (g, 0, i))
    lspec = pl.BlockSpec((Q_GROUP, 1, tq), lambda g, i: (g, 0, i))
    kspec = pl.BlockSpec((None, nk, HEAD_DIM), lambda g, i: (g, 0, 0))
    ktspec = pl.BlockSpec((None, HEAD_DIM, nk), lambda g, i: (g, 0, 0))
    grid = (N_KV_HEADS, nq // tq)
    body, xspecs, xshapes, xsems = _with_exchange(body, 7, 3, grid, comm)
    res = pl.pallas_call(
        body, name=name, grid=grid,
        in_specs=[tspec, kspec, kspec, ktspec, tspec, tspec, lspec] + xspecs,
        out_specs=[tspec, ktspec, ktspec] + xspecs,
        out_shape=[jax.ShapeDtypeStruct((N_Q_HEADS, HEAD_DIM, nq), F32), jax.ShapeDtypeStruct((N_KV_HEADS, HEAD_DIM, nk), F32),
                   jax.ShapeDtypeStruct((N_KV_HEADS, HEAD_DIM, nk), F32)] + xshapes,
        scratch_shapes=[pltpu.VMEM((Q_GROUP, 1, tq), F32), pltpu.VMEM((Q_GROUP, HEAD_DIM, tq), F32)] + xsems,
        compiler_params=_ARB2,
    )(qt8, k2, v2, kt2, ot8, dot8, lse, *(comm[0] if comm else []))
    return res[0], res[1], res[2], list(res[3:])


def _log_sigmoid(z):
    return jnp.minimum(z, 0.0) - jnp.log(1.0 + jnp.exp(-jnp.abs(z)))


_BDOT_DIMS = {"nn": _NN, "nt": _NT, "tn": _TN}
_BDOT_BWD = {"nn": (("nt", "gb"), ("tn", "ag")), "nt": (("nn", "gb"), ("tn", "ga")), "tn": (("nt", "bg"), ("nn", "ag"))}


def _bdot_raw(a, b, mode):
    return lax.dot_general(a.astype(BF16), b.astype(BF16), _BDOT_DIMS[mode], preferred_element_type=F32)


@functools.partial(jax.custom_vjp, nondiff_argnums=(2,))
def _bdot(a, b, mode):
    return _bdot_raw(a, b, mode)


def _bdot_fwd(a, b, mode):
    return _bdot_raw(a, b, mode), (a.astype(BF16), b.astype(BF16))


def _bdot_bwd(mode, res, g):
    ops = {"a": res[0], "b": res[1], "g": g}
    (ma, oa), (mb, ob) = _BDOT_BWD[mode]
    return _bdot_raw(ops[oa[0]], ops[oa[1]], ma), _bdot_raw(ops[ob[0]], ops[ob[1]], mb)


_bdot.defvjp(_bdot_fwd, _bdot_bwd)


def _tile_tri(rev, rows):
    r_i = lax.broadcasted_iota(jnp.int32, (rows, rows), 0)
    c_i = lax.broadcasted_iota(jnp.int32, (rows, rows), 1)
    same = (r_i // GLA_CHUNK) == (c_i // GLA_CHUNK)
    return same & ((c_i >= r_i) if rev else (c_i <= r_i))


def _tri_dot(rev, x):
    tri = _tile_tri(rev, x.shape[0]).astype(BF16)
    return sum(lax.dot_general(tri, piece, _NN, preferred_element_type=F32) for piece in _split3(x))


@functools.partial(jax.custom_vjp, nondiff_argnums=(1,))
def _chunk_cumsum(x, rev):
    return _tri_dot(rev, x)


_chunk_cumsum.defvjp(lambda x, rev: (_tri_dot(rev, x), None), lambda rev, _, g: (_tri_dot(not rev, g),))


def _gla_tile(q, k, vs, a, w2, b2, state_t, *, rev):
    rows = q.shape[0]
    nch = rows // GLA_CHUNK
    tri = _tile_tri(rev, rows)
    chunk_of_row = lax.broadcasted_iota(jnp.int32, (rows, 1), 0) // GLA_CHUNK
    in_chunk = [(chunk_of_row == c).astype(F32) for c in range(nch)]
    la = _log_sigmoid(_bdot(a, w2, "nn") + b2) * (1.0 / GLA_TAU)
    cum = _chunk_cumsum(la, rev)
    tots = [jnp.sum(la * in_chunk[c], axis=0, keepdims=True) for c in range(nch)]
    tot_rows = sum(in_chunk[c] * tots[c] for c in range(nch))
    q_in = q * (GLA_DK ** -0.5) * jnp.exp(cum)
    k_in = k * jnp.exp(-cum)
    k_st = k * jnp.exp(tot_rows - cum)
    lane = lax.broadcasted_iota(jnp.int32, (1, GLA_QK_W), 1)
    outs = []
    for h in range(GLA_HEADS):
        head = ((lane >= GLA_DK * h) & (lane < GLA_DK * (h + 1))).astype(F32)
        att = jnp.where(tri, _bdot(q_in * head, k_in, "nt"), 0.0)
        outs.append(_bdot(att, vs[h], "nn"))
    o = jnp.concatenate(outs, axis=1)
    hr = lax.broadcasted_iota(jnp.int32, (GLA_V_W, GLA_QK_W), 0) // GLA_DV
    hc = lax.broadcasted_iota(jnp.int32, (GLA_V_W, GLA_QK_W), 1) // GLA_DK
    same_head = (hr == hc).astype(F32)
    v_all = jnp.concatenate(vs, axis=1)
    for c in (range(nch - 1, -1, -1) if rev else range(nch)):
        o = o + _bdot(q_in * in_chunk[c], state_t, "nt")
        state_t = jnp.exp(tots[c]) * state_t + _bdot(v_all, k_st * in_chunk[c], "tn") * same_head
    return o, state_t


def _gla_tile_of(step, rev, nct, nt):
    if not rev:
        return step
    return jnp.where(step < nct, nct - 1 - step, nt - 1 - (step - nct))


def _gla_row_specs(R, tile):
    return [pl.BlockSpec((R, GLA_QK_W), lambda s: (tile(s), C_GQ // GLA_QK_W)),
            pl.BlockSpec((R, GLA_QK_W), lambda s: (tile(s), C_GK // GLA_QK_W)),
            pl.BlockSpec((R, GLA_V_W), lambda s: (tile(s), C_GV // GLA_V_W)),
            pl.BlockSpec((R, 128), lambda s: (tile(s), C_AF // 128))]


def _gla_dir(name, p, w2, b2, *, rev, R, nct, add=None):
    n = p.shape[0]
    nt = n // R
    tile = lambda s: _gla_tile_of(s, rev, nct, nt)

    def body(q_ref, k_ref, v_ref, a_ref, w2_ref, b2_ref, *rest):
        if add is not None:
            add_ref, o_ref, ssave_ref, state = rest
        else:
            o_ref, ssave_ref, state = rest

        @pl.when(pl.program_id(0) == 0)
        def _():
            state[...] = jnp.zeros_like(state)

        vs = [v_ref[:, GLA_DV * h:GLA_DV * (h + 1)] for h in range(GLA_HEADS)]
        s_in = state[...]
        ssave_ref[...] = s_in
        o, s_out = _gla_tile(q_ref[...], k_ref[...], vs, a_ref[...], w2_ref[...], b2_ref[...], s_in, rev=rev)
        if add is not None:
            o = o + add_ref[...]
        o_ref[...] = o
        state[...] = s_out

    o_spec = pl.BlockSpec((R, GLA_V_W), lambda s: (tile(s), 0))
    in_specs = _gla_row_specs(R, tile) + [_full_spec(w2.shape), _full_spec(b2.shape)]
    args = [p, p, p, p, w2, b2]
    if add is not None:
        in_specs.append(o_spec)
        args.append(add)
    return pl.pallas_call(
        body, name=name, grid=(nt,), in_specs=in_specs,
        out_specs=[o_spec, pl.BlockSpec((None, GLA_V_W, GLA_QK_W), lambda s: (tile(s), 0, 0))],
        out_shape=[jax.ShapeDtypeStruct((n, GLA_V_W), F32), jax.ShapeDtypeStruct((nt, GLA_V_W, GLA_QK_W), F32)],
        scratch_shapes=[pltpu.VMEM((GLA_V_W, GLA_QK_W), F32)],
        compiler_params=_ARB1,
    )(*args)


def _gla_dir_bwd(name, p, w2, b2, ssave, do, *, rev, R, nct, adds=None, out_dtype=F32):
    n = p.shape[0]
    nt = n // R
    tile = lambda s: _gla_tile_of(nt - 1 - s, rev, nct, nt)
    widths = (GLA_QK_W, GLA_QK_W, GLA_V_W, 128)

    def body(q_ref, k_ref, v_ref, a_ref, w2_ref, b2_ref, ss_ref, do_ref, *rest):
        if adds is not None:
            add_refs, rest = rest[:4], rest[4:]
        dq_ref, dk_ref, dv_ref, da_ref, dw2_ref, db2_ref, dstate = rest

        @pl.when(pl.program_id(0) == 0)
        def _():
            dstate[...] = jnp.zeros_like(dstate)
            dw2_ref[...] = jnp.zeros_like(dw2_ref)
            db2_ref[...] = jnp.zeros_like(db2_ref)

        vs = [v_ref[:, GLA_DV * h:GLA_DV * (h + 1)] for h in range(GLA_HEADS)]
        _, vjp = jax.vjp(functools.partial(_gla_tile, rev=rev), q_ref[...], k_ref[...], vs, a_ref[...],
                         w2_ref[...], b2_ref[...], ss_ref[...])
        dq, dk, dvs, da, dw2, db2, ds = vjp((do_ref[...], dstate[...]))
        grads = [dq, dk, jnp.concatenate(dvs, axis=1), da]
        if adds is not None:
            grads = [g + r[...].astype(F32) for g, r in zip(grads, add_refs)]
        for ref, g in zip((dq_ref, dk_ref, dv_ref, da_ref), grads):
            ref[...] = g.astype(ref.dtype)
        dw2_ref[...] += dw2
        db2_ref[...] += db2
        dstate[...] = ds

    d_specs = [pl.BlockSpec((R, w), lambda s: (tile(s), 0)) for w in widths]
    in_specs = (_gla_row_specs(R, tile) + [_full_spec(w2.shape), _full_spec(b2.shape),
                pl.BlockSpec((None, GLA_V_W, GLA_QK_W), lambda s: (tile(s), 0, 0)),
                pl.BlockSpec((R, GLA_V_W), lambda s: (tile(s), 0))])
    args = [p, p, p, p, w2, b2, ssave, do]
    if adds is not None:
        in_specs += d_specs
        args += list(adds)
    return pl.pallas_call(
        body, name=name, grid=(nt,), in_specs=in_specs,
        out_specs=d_specs + [_full_spec(w2.shape), _full_spec(b2.shape)],
        out_shape=[jax.ShapeDtypeStruct((n, w), out_dtype) for w in widths]
        + [jax.ShapeDtypeStruct(w2.shape, F32), jax.ShapeDtypeStruct(b2.shape, F32)],
        scratch_shapes=[pltpu.VMEM((GLA_V_W, GLA_QK_W), F32)],
        compiler_params=_ARB1,
    )(*args)


def _final_loss(name, x, target, gf, R, nct):
    n = x.shape[0]

    def body(x_ref, t_ref, g_ref, loss_ref, dx_ref, dg_ref):
        i = pl.program_id(0)

        @pl.when(i == 0)
        def _():
            loss_ref[...] = jnp.zeros_like(loss_ref)
            dg_ref[...] = jnp.zeros_like(dg_ref)

        @pl.when(i < nct)
        def _():
            dx_ref[...] = jnp.zeros_like(dx_ref)

        @pl.when(i >= nct)
        def _():
            y, vjp = jax.vjp(_rms, x_ref[...], g_ref[...])
            err = y - t_ref[...]
            loss_ref[...] += jnp.sum(0.5 * jnp.mean(err * err, axis=-1, keepdims=True))
            dx, dg = vjp(err * (1.0 / D))
            dx_ref[...] = dx
            dg_ref[...] += dg

    return pl.pallas_call(
        body, name=name, grid=(n // R,),
        in_specs=[_row_spec(R, D, 0), pl.BlockSpec((R, D), lambda i: (jnp.maximum(i - nct, 0), 0)), _full_spec((1, D))],
        out_specs=[_full_spec((8, 128)), _row_spec(R, D, 0), _full_spec((1, D))],
        out_shape=[jax.ShapeDtypeStruct((8, 128), F32), jax.ShapeDtypeStruct((n, D), F32), jax.ShapeDtypeStruct((1, D), F32)],
        compiler_params=_ARB1,
    )(x, target, gf)


def _adamw(name, w, m, v, gparts):
    rows, cols = w.shape
    nparts = gparts.shape[0]
    tr = rows
    for cand in range(min(rows, 256), 15, -16):
        if rows % cand == 0:
            tr = cand
            break

    def body(w_ref, m_ref, v_ref, g_ref, go_ref, d_ref, mo_ref, vo_ref):
        g = g_ref[0].astype(F32)
        for k in range(1, nparts):
            g = g + g_ref[k].astype(F32)
        m_new = ADAM_B1 * m_ref[...] + (1.0 - ADAM_B1) * g
        v_new = ADAM_B2 * v_ref[...] + (1.0 - ADAM_B2) * (g * g)
        m_hat = m_new / (1.0 - ADAM_B1 ** ADAM_STEP)
        v_hat = v_new / (1.0 - ADAM_B2 ** ADAM_STEP)
        go_ref[...] = g
        d_ref[...] = -ADAM_LR * (m_hat / (jnp.sqrt(v_hat) + ADAM_EPS) + ADAM_WD * w_ref[...])
        mo_ref[...] = m_new
        vo_ref[...] = v_new

    spec = pl.BlockSpec((tr, cols), lambda i: (i, 0))
    return pl.pallas_call(
        body, name=name, grid=(rows // tr,),
        in_specs=[spec, spec, spec, pl.BlockSpec((nparts, tr, cols), lambda i: (0, i, 0))],
        out_specs=[spec] * 4, out_shape=[jax.ShapeDtypeStruct((rows, cols), F32)] * 4,
        compiler_params=_ARB1,
    )(w, m, v, gparts)


def _my_index():
    return 4 * lax.axis_index("x") + 2 * lax.axis_index("y") + lax.axis_index("c")


def _xor_peer(k):
    flip = lambda a, bit: (1 - a) if bit else a
    pos = (flip(lax.axis_index("x"), (k >> 2) & 1), flip(lax.axis_index("y"), (k >> 1) & 1), flip(lax.axis_index("c"), k & 1))
    return pos, 4 * pos[0] + 2 * pos[1] + pos[2]


def _exchange_plan(x_refs, o_refs, send_sems, recv_sems, local_sems, gather):
    npeer = N_DEV - 1
    me = _my_index()
    locals_, sends, recvs = [], [], []
    for a, (x_ref, o_ref) in enumerate(zip(x_refs, o_refs)):
        mine = x_ref if gather else x_ref.at[me]
        locals_.append(pltpu.make_async_copy(mine, o_ref.at[me], local_sems.at[a]))
        for k in range(1, N_DEV):
            pos, lin = _xor_peer(k)
            src = x_ref if gather else x_ref.at[lin]
            sem = a * npeer + k - 1
            sends.append(pltpu.make_async_remote_copy(src_ref=src, dst_ref=o_ref.at[me], send_sem=send_sems.at[sem],
                                                      recv_sem=recv_sems.at[sem], device_id=pos, device_id_type=MESH_ID))
            recvs.append(pltpu.make_async_remote_copy(src_ref=src, dst_ref=o_ref.at[lin], send_sem=send_sems.at[sem],
                                                      recv_sem=recv_sems.at[sem], device_id=pos, device_id_type=MESH_ID))

    def start():
        for cp in locals_ + sends:
            cp.start()

    def wait():
        for cp in recvs:
            cp.wait_recv()
        for cp in sends:
            cp.wait_send()
        for cp in locals_:
            cp.wait()

    return start, wait


def _exchange_io(xs, gather):
    na = len(xs)
    hbm = pl.BlockSpec(memory_space=pltpu.HBM)
    shapes = [jax.ShapeDtypeStruct((N_DEV,) + tuple(x.shape if gather else x.shape[1:]), x.dtype) for x in xs]
    sems = [pltpu.SemaphoreType.DMA((na * (N_DEV - 1),)), pltpu.SemaphoreType.DMA((na * (N_DEV - 1),)),
            pltpu.SemaphoreType.DMA((na,))]
    return [hbm] * na, shapes, sems


def _exchange(name, xs, *, gather):
    na = len(xs)
    specs, shapes, sems = _exchange_io(xs, gather)

    def body(*refs):
        start, wait = _exchange_plan(refs[:na], refs[na:2 * na], *refs[2 * na:], gather)
        start()
        wait()

    return list(pl.pallas_call(body, name=name, in_specs=specs, out_specs=specs, out_shape=shapes, scratch_shapes=sems)(*xs))


def _adaln_fwd(name, craw16, w_ada, b_cols):
    def body(c_ref, w_ref, b_ref, o_ref):
        cs = jax.nn.silu(c_ref[...]).astype(BF16)
        for l in range(2):
            o_ref[l] = lax.dot_general(cs, w_ref[l].astype(BF16), _NN, preferred_element_type=F32) + b_ref[l]

    return pl.pallas_call(
        body, name=name, out_shape=jax.ShapeDtypeStruct((2, 16, w_ada.shape[2]), F32),
        compiler_params=pltpu.CompilerParams(vmem_limit_bytes=V7X_VMEM_LIMIT),
    )(craw16, w_ada, b_cols)


def _adaln_bwd(name, craw16, w_ada, dm):
    def body(c_ref, w_ref, dm_ref, gw_ref, dc_ref):
        c = c_ref[...]
        sg = jax.nn.sigmoid(c)
        cs = c * sg
        row = lax.broadcasted_iota(jnp.int32, (8, 1), 0)
        dc = jnp.zeros((16, D), F32)
        for l in range(2):
            dmx = dm_ref[2 * l + 1]
            dmc = jnp.where(row == 0, jnp.sum(dm_ref[2 * l], axis=0, keepdims=True), 0.0)
            gw_ref[l] = _hdot(cs[0:8], dmx, _TN) + _hdot(cs[8:16], dmc, _TN)
            dc = dc + _hdot(jnp.concatenate([dmx, dmc], axis=0), w_ref[l], _NT)
        dc_ref[...] = dc * sg * (1.0 + c * (1.0 - sg))

    return pl.pallas_call(
        body, name=name,
        out_shape=[jax.ShapeDtypeStruct(w_ada.shape, F32), jax.ShapeDtypeStruct((16, D), F32)],
        compiler_params=pltpu.CompilerParams(vmem_limit_bytes=V7X_VMEM_LIMIT),
    )(craw16, w_ada, dm)


_IN_OFFS = [sum(IN_SPLITS[:k]) for k in range(len(IN_SPLITS) + 1)]
_MY_ORDER = (11, 12, 13, 0, 1, 2, 7, 10, 5, 6, 3, 4, 8, 9)


_IN_SHARD = IN_WIDTH // N_DEV


def _win_my_cols(pieces):
    parts = []
    for k in _MY_ORDER:
        a, b = _IN_OFFS[k], _IN_OFFS[k + 1]
        for s in range(a // _IN_SHARD, (b - 1) // _IN_SHARD + 1):
            lo, hi = max(a, s * _IN_SHARD), min(b, (s + 1) * _IN_SHARD)
            parts.append(pieces[s][:, lo - s * _IN_SHARD:hi - s * _IN_SHARD])
    parts.append(jnp.zeros((pieces[0].shape[0], PW - IN_WIDTH), pieces[0].dtype))
    return jnp.concatenate(parts, axis=1)


def _win_shards(wp):
    my_offs, pos = {}, 0
    for k in _MY_ORDER:
        my_offs[k] = pos
        pos += IN_SPLITS[k]
    shards = []
    for s in range(N_DEV):
        parts = []
        for k in range(len(IN_SPLITS)):
            lo, hi = max(_IN_OFFS[k], s * _IN_SHARD), min(_IN_OFFS[k + 1], (s + 1) * _IN_SHARD)
            if lo < hi:
                parts.append(wp[:, my_offs[k] + lo - _IN_OFFS[k]:my_offs[k] + hi - _IN_OFFS[k]])
        shards.append(jnp.concatenate(parts, axis=1))
    return jnp.stack(shards)


def _split_shards(full, axis):
    c = full.shape[axis] // N_DEV
    return jnp.stack([lax.slice_in_dim(full, s * c, (s + 1) * c, axis=axis) for s in range(N_DEV)])


def _pack_rows(pieces, row_mult):
    rows = jnp.concatenate([p.reshape(-1, 128) for p in pieces], axis=0)
    padn = (-rows.shape[0]) % row_mult
    if padn:
        rows = jnp.concatenate([rows, jnp.zeros((padn, 128), rows.dtype)], axis=0)
    return rows


def _unpack_rows(rows, shapes):
    out, pos = [], 0
    for s in shapes:
        size = 1
        for d in s:
            size *= d
        out.append(rows[pos:pos + size // 128].reshape(tuple(s)))
        pos += size // 128
    return out


def _heads_front(a, nh):
    return a.reshape(a.shape[0], nh, HEAD_DIM).transpose(1, 0, 2)


def _heads_back(a):
    return a.transpose(1, 0, 2).reshape(a.shape[1], a.shape[0] * HEAD_DIM)


def _rope_tables(t, tc):
    tok = jnp.arange(t, dtype=jnp.int32)
    inv_freq = ROPE_THETA ** (-jnp.arange(ROPE_FREQS, dtype=F32) / ROPE_FREQS)
    ang_r = (tok // GRID_W).astype(F32)[:, None] * inv_freq
    ang_c = (tok % GRID_W).astype(F32)[:, None] * inv_freq
    cos64 = jnp.concatenate([jnp.cos(ang_r), jnp.cos(ang_r), jnp.cos(ang_c), jnp.cos(ang_c)], axis=1)
    sin64 = jnp.concatenate([-jnp.sin(ang_r), jnp.sin(ang_r), -jnp.sin(ang_c), jnp.sin(ang_c)], axis=1)
    cos64 = jnp.concatenate([jnp.ones((tc, HEAD_DIM), F32), cos64], axis=0)
    sin64 = jnp.concatenate([jnp.zeros((tc, HEAD_DIM), F32), sin64], axis=0)
    return jnp.tile(cos64, (1, N_Q_HEADS)), jnp.tile(sin64, (1, N_Q_HEADS))


def _head_mean_matrix(width):
    i = jnp.arange(width) // HEAD_DIM
    return (i[:, None] == i[None, :]).astype(F32) / HEAD_DIM


def _head_tile_matrix(width):
    return (jnp.arange(HEAD_DIM)[:, None] == (jnp.arange(width) % HEAD_DIM)[None, :]).astype(F32)


def _heads_t(a, nh):
    return a.T.reshape(nh, HEAD_DIM, a.shape[0])


def _heads_t_back(a):
    return a.reshape(a.shape[0] * HEAD_DIM, a.shape[2]).T


def _attention_fwd(tag, qr, kr, vv, tc, comm):
    qt8, k2, vt2 = _heads_t(qr, N_Q_HEADS), _heads_front(kr, N_KV_HEADS), _heads_t(vv, N_KV_HEADS)
    vt2 = jnp.concatenate([vt2, jnp.ones((N_KV_HEADS, 8, vt2.shape[2]), BF16)], axis=1)
    o_c, lse_c, _ = _attn_fwd(tag + "_attn_ctx", qt8[:, :, :tc], k2[:, :tc], vt2[:, :, :tc])
    o_x, lse_x, comm_out = _attn_fwd(tag + "_attn_lat", qt8[:, :, tc:], k2, vt2, comm)
    ot8 = jnp.concatenate([o_c, o_x], axis=2)
    lse = jnp.concatenate([lse_c, lse_x], axis=2)
    return _heads_t_back(ot8), (qr, kr, vv, ot8, lse), comm_out


def _attention_bwd(tag, saved, datt, tc, comm):
    qr, kr, vv, ot8, lse = saved
    datt = datt.astype(BF16)
    qt8, dot8 = _heads_t(qr, N_Q_HEADS), _heads_t(datt, N_Q_HEADS)
    k2, v2, kt2 = _heads_front(kr, N_KV_HEADS), _heads_front(vv, N_KV_HEADS), _heads_t(kr, N_KV_HEADS)
    dq_c, dk_c, dv_c, _ = _attn_bwd(tag + "_attn_b_ctx", qt8[:, :, :tc], k2[:, :tc], v2[:, :tc], kt2[:, :, :tc],
                                    ot8[:, :, :tc], dot8[:, :, :tc], lse[:, :, :tc])
    dq_x, dk_x, dv_x, comm_out = _attn_bwd(tag + "_attn_b_lat", qt8[:, :, tc:], k2, v2, kt2,
                                           ot8[:, :, tc:], dot8[:, :, tc:], lse[:, :, tc:], comm)
    dqt8 = jnp.concatenate([dq_c, dq_x], axis=2)
    dkt2 = dk_x.at[:, :, :tc].add(dk_c)
    dvt2 = dv_x.at[:, :, :tc].add(dv_c)
    return _heads_t_back(dqt8), _heads_t_back(dkt2), _heads_t_back(dvt2), comm_out


def _layer_fwd(tag, x, w, modv, consts, R, nct, tc, comm, late_weights):
    sh1, sc1, g1, sh2, sc2, g2 = modv
    cosq, sinq, bdq, eq, bdk, ek = consts
    n = x.shape[0]
    (h1,) = _rowwise(tag + "_ln1", _f_lnmod, R, nct, [(x, D, 0, D)], [], [(w["norm1_g"], False)], [], [sh1, sc1], [(D, BF16)])
    p = _matmul(tag + "_in", h1, w["w_in"], "nn", F32)
    qk_rows = [(p, Q_W, C_Q // Q_W, Q_W), (p, KV_W, C_K // KV_W, KV_W)]
    qk_consts = [(cosq, Q_W, 0, Q_W), (sinq, Q_W, 0, Q_W), (cosq, KV_W, 0, KV_W), (sinq, KV_W, 0, KV_W)]
    qk_params = [(w["q_norm_g"], False), (w["k_norm_g"], False)]
    qk_cparams = [(bdq, False), (eq, False), (bdk, False), (ek, False)]
    qr, kr = _rowwise(tag + "_qk", _f_qknorm, R, nct, qk_rows, qk_consts, qk_params, qk_cparams, [],
                      [(Q_W, BF16), (KV_W, BF16)], post=_qk_post)
    vv = p[:, C_VV:C_VV + KV_W].astype(BF16)
    att, att_saved, comm_out = _attention_fwd(tag, qr, kr, vv, tc, comm)
    w.update(late_weights(comm_out))

    o_f, s_f = _gla_dir(tag + "_gla_f", p, w["w2p_f"], w["b2_f"], rev=False, R=R, nct=nct)
    o_fb, s_b = _gla_dir(tag + "_gla_b", p, w["w2p_b"], w["b2_b"], rev=True, R=R, nct=nct, add=o_f)
    go_rows = [(o_fb, GLA_V_W, 0, GLA_DV), (p, GLA_V_W, C_R // GLA_V_W, GLA_DV)]
    (gla,) = _rowwise(tag + "_glaout", _f_glaout, R, nct, go_rows, [], [(w["gla_norm_g"], True)], [], [], [(GLA_V_W, BF16)])

    rg = GMLP_CHUNK
    gm_rows = [(p, GMLP_W, C_U // GMLP_W, GMLP_W), (p, GMLP_W, C_V // GMLP_W, GMLP_W // GMLP_GROUPS)]
    gm_params = [(w["gmlp_norm_g"], True), (w["w_spatial"], True), (w["b_spatial_t"], True)]
    (gm,) = _rowwise(tag + "_gmlp", _f_gmlp, rg, tc // rg, gm_rows, [], gm_params, [], [], [(GMLP_W, BF16)])

    ya = _matmul(tag + "_br_a", gm, w["w_br_a"], "nn", F32)
    yb = _matmul(tag + "_br_b", att, w["w_br_b"], "nn", F32)
    yc = _matmul(tag + "_br_c", gla, w["w_br_c"], "nn", F32)
    mg_rows = [(p, D, C_GA // D, D), (p, D, C_GB // D, D), (p, D, C_GC // D, D), (ya, D, 0, D), (yb, D, 0, D), (yc, D, 0, D)]
    (merged,) = _rowwise(tag + "_merge", _f_merge, R, nct, mg_rows, [], [], [], [], [(D, BF16)])
    mix = _matmul(tag + "_out", merged, w["w_out"], "nn", F32)
    (x_mid,) = _rowwise(tag + "_res1", _f_resid, R, nct, [(x, D, 0, D), (mix, D, 0, D)], [], [], [], [g1], [(D, F32)])

    (h2,) = _rowwise(tag + "_ln2", _f_lnmod, R, nct, [(x_mid, D, 0, D)], [], [(w["norm2_g"], False)], [], [sh2, sc2], [(D, BF16)])
    a2 = _matmul(tag + "_up", h2, w["w_ffn_up"], "nn", F32, o_halves=True)
    gv = _conv_gate(tag + "_conv", a2, w["conv_w_h"], w["conv_b_h"], R, tc)
    ffn = _matmul(tag + "_down", gv, w["w_ffn_down"], "nn", F32)
    (x_next,) = _rowwise(tag + "_res2", _f_resid, R, nct, [(x_mid, D, 0, D), (ffn, D, 0, D)], [], [], [], [g2], [(D, F32)])
    saved = dict(x=x, h1=h1, p=p, att_saved=att_saved, att=att, o_fb=o_fb, s_f=s_f, s_b=s_b, gla=gla, gm=gm,
                 ya=ya, yb=yb, yc=yc, merged=merged, mix=mix, x_mid=x_mid, h2=h2, a2=a2, gv=gv, ffn=ffn,
                 qk=(qk_rows, qk_consts, qk_params, qk_cparams), go_rows=go_rows, gm_info=(gm_rows, gm_params),
                 mg_rows=mg_rows)
    return x_next, saved, comm_out


def _layer_bwd(tag, dx_next, s, w, modv, R, nct, tc, make_comm):
    sh1, sc1, g1, sh2, sc2, g2 = modv
    gw = {}
    (dffn,), _, (dg2,) = _rowwise_bwd(tag + "_res2_b", _f_resid, R, nct, [(s["ffn"], D, 0, D), (s["ffn"], D, 0, D)], [], [], [], [g2],
                                      [(dx_next, D)], [None, BF16])
    dgv = _matmul(tag + "_down_da", dffn, w["w_ffn_down"], "nt", F32)
    gw["w_ffn_down"] = _matmul(tag + "_down_dw", s["gv"], dffn, "tn", F32)
    da2, dcw, dcb = _conv_gate_bwd(tag + "_conv_b", s["a2"], dgv, w["conv_w_h"], w["conv_b_h"], R, tc)
    gw["conv_w_h"], gw["conv_b_h"] = dcw, dcb
    dh2 = _matmul(tag + "_up_da", da2, w["w_ffn_up"], "nt", F32, a_halves=True)
    gw["w_ffn_up"] = _matmul(tag + "_up_dw", s["h2"], da2, "tn", F32, b_halves=True)
    (dx_mid,), (gw["norm2_g"],), (dsh2, dsc2) = _rowwise_bwd(
        tag + "_ln2_b", _f_lnmod, R, nct, [(s["x_mid"], D, 0, D)], [], [(w["norm2_g"], False)], [], [sh2, sc2],
        [(dh2, D)], [F32], adds=[dx_next])
    (dmix,), _, (dg1,) = _rowwise_bwd(tag + "_res1_b", _f_resid, R, nct, [(s["mix"], D, 0, D), (s["mix"], D, 0, D)], [], [], [], [g1],
                                      [(dx_mid, D)], [None, BF16])
    dmerged = _matmul(tag + "_out_da", dmix, w["w_out"], "nt", F32)
    gw["w_out"] = _matmul(tag + "_out_dw", s["merged"], dmix, "tn", F32)
    (dga, dgb, dgc, dya, dyb, dyc), _, _ = _rowwise_bwd(tag + "_merge_b", _f_merge, R, nct, s["mg_rows"], [], [], [], [],
                                                        [(dmerged, D)], [BF16] * 6)
    dgm = _matmul(tag + "_br_a_da", dya, w["w_br_a"], "nt", F32)
    datt = _matmul(tag + "_br_b_da", dyb, w["w_br_b"], "nt", F32)
    dgla = _matmul(tag + "_br_c_da", dyc, w["w_br_c"], "nt", F32)
    gm_rows, gm_params = s["gm_info"]
    gw["w_br_a"] = _matmul(tag + "_br_a_dw", s["gm"], dya, "tn", F32)
    gw["w_br_b"] = _matmul(tag + "_br_b_dw", s["att"], dyb, "tn", F32)
    gw["w_br_c"] = _matmul(tag + "_br_c_dw", s["gla"], dyc, "tn", F32)
    rg = GMLP_CHUNK
    (du, dv_), (gw["gmlp_norm_g"], gw["w_spatial"], gw["b_spatial_t"]), _ = _rowwise_bwd(
        tag + "_gmlp_b", _f_gmlp, rg, tc // rg, gm_rows, [], gm_params, [], [], [(dgm, GMLP_W)], [BF16, BF16])
    (do, dr), (gw["gla_norm_g"],), _ = _rowwise_bwd(tag + "_glaout_b", _f_glaout, R, nct, s["go_rows"], [],
                                                    [(w["gla_norm_g"], True)], [], [], [(dgla, GLA_V_W)], [F32, BF16])
    p = s["p"]
    *d_b, gw["w2p_b"], gw["b2_b"] = _gla_dir_bwd(tag + "_gla_b_b", p, w["w2p_b"], w["b2_b"], s["s_b"], do, rev=True, R=R, nct=nct)
    dgq, dgk, dgv_, daf, gw["w2p_f"], gw["b2_f"] = _gla_dir_bwd(tag + "_gla_f_b", p, w["w2p_f"], w["b2_f"], s["s_f"], do,
                                                              rev=False, R=R, nct=nct, adds=d_b, out_dtype=BF16)
    dqr, dkr, dvv, comm_out = _attention_bwd(tag, s["att_saved"], datt, tc, make_comm(gw))
    qk_rows, qk_consts, qk_params, qk_cparams = s["qk"]
    (dq, dk), (gw["q_norm_g"], gw["k_norm_g"]), _ = _rowwise_bwd(
        tag + "_qk_b", _f_qknorm, R, nct, qk_rows, qk_consts, qk_params, qk_cparams, [],
        [(dqr, Q_W), (dkr, KV_W)], [BF16, BF16], pre=_qk_pre)
    dp = jnp.concatenate([dga, dgb, dgc, du, dv_, dq, dgv_, dr, dgq, dgk, dk, dvv.astype(BF16), daf,
                          jnp.zeros((p.shape[0], PW - C_AF - 128), BF16)], axis=1)
    dh1 = _matmul(tag + "_in_da", dp, w["w_in"], "nt", F32)
    gw["w_in"] = _matmul(tag + "_in_dw", s["h1"], dp, "tn", F32)
    (dx,), (gw["norm1_g"],), (dsh1, dsc1) = _rowwise_bwd(
        tag + "_ln1_b", _f_lnmod, R, nct, [(s["x"], D, 0, D)], [], [(w["norm1_g"], False)], [], [sh1, sc1],
        [(dh1, D)], [F32], adds=[dx_mid])
    return dx, gw, (dsh1, dsc1, dg1, dsh2, dsc2, dg2), comm_out


_SHARDED = (("w_in", 1, True), ("w_br_a", 1, True), ("w_br_b", 1, True), ("w_br_c", 1, True), ("w_out", 0, True),
            ("w_ffn_up", 1, True), ("w_ffn_down", 0, True), ("conv_w", 1, False), ("w_alpha2", 2, False), ("b_alpha", 1, False))
_REPLICATED = ("c_ctx", "b_ada", "norm1_g", "norm2_g", "q_norm_g", "k_norm_g", "gmlp_norm_g", "w_spatial", "b_spatial",
               "gla_norm_g", "conv_b", "final_norm_g")
_WEIGHTS = ("c_ctx", "w_ada", "b_ada", "norm1_g", "norm2_g", "w_in", "q_norm_g", "k_norm_g", "gmlp_norm_g", "w_spatial",
            "b_spatial", "w_alpha2", "b_alpha", "gla_norm_g", "w_br_a", "w_br_b", "w_br_c", "w_out", "w_ffn_up", "conv_w",
            "conv_b", "w_ffn_down", "final_norm_g")


def _decay_weights(w_alpha2_l, b_alpha_l):
    out = []
    for d in range(2):
        w2p = jnp.zeros((128, GLA_QK_W), F32).at[GLA_RANK * d:GLA_RANK * (d + 1)].set(w_alpha2_l[d])
        out += [w2p, b_alpha_l[d][None, :]]
    return out


def _step(inp, wts, moms, vels):
    x, c, ctx, loss_target = inp
    t, tc = x.shape[1], ctx.shape[1]
    n = t + tc
    R = min(256, tc)
    nct = tc // R
    me = _my_index()
    depth = wts["w_in"].shape[0]

    late = [(nm, ax) for nm, ax, half in _SHARDED if half and nm != "w_in"]
    small = [(nm, ax) for nm, ax, half in _SHARDED if not half]
    w_in_shard = lambda l: wts["w_in"][l].astype(BF16)
    c8 = jnp.concatenate([c, jnp.zeros((7, D), F32)], axis=0)
    first = _exchange("gather_first", [w_in_shard(0)] + [wts[nm] for nm, _ in small] + [c8], gather=True)
    c_all = first[-1][:, 0, :]
    small_all = dict(zip([nm for nm, _ in small], first[1:-1]))

    def early_weights(l, w_in_all):
        w = {"w_in": _win_my_cols([w_in_all[s] for s in range(N_DEV)])}
        conv_w, w_alpha2, b_alpha = [jnp.concatenate([small_all[nm][s, l] for s in range(N_DEV)], axis=ax) for nm, ax in small]
        w["conv_w_h"] = conv_w.reshape(3, 2, FFN_H).transpose(1, 0, 2)
        w["conv_b_h"] = wts["conv_b"][l].reshape(2, 1, FFN_H)
        w["w2p_f"], w["b2_f"], w["w2p_b"], w["b2_b"] = _decay_weights(w_alpha2, b_alpha)
        w["norm1_g"] = wts["norm1_g"][l][None, :]
        w["norm2_g"] = wts["norm2_g"][l][None, :]
        w["q_norm_g"] = wts["q_norm_g"][l][None, :]
        w["k_norm_g"] = wts["k_norm_g"][l][None, :]
        w["gmlp_norm_g"] = wts["gmlp_norm_g"][l].reshape(GMLP_GROUPS, 1, GMLP_W // GMLP_GROUPS)
        w["w_spatial"] = wts["w_spatial"][l]
        w["b_spatial_t"] = wts["b_spatial"][l][:, :, None]
        w["gla_norm_g"] = wts["gla_norm_g"][l].reshape(GLA_HEADS, 1, GLA_DV)
        return w

    craw16 =jnp.concatenate([c_all, wts["c_ctx"][None, :], jnp.zeros((7, D), F32)], axis=0)
    acols = wts["w_ada"].shape[2]
    b_cols = lax.dynamic_slice_in_dim(wts["b_ada"], me * acols, acols, axis=1)[:, None, :]
    mod_part = _adaln_fwd("adaln", craw16, wts["w_ada"], b_cols)
    send = jnp.stack([mod_part[:, 8, :][None].repeat(N_DEV, 0), mod_part[:, :8, :].transpose(1, 0, 2)], axis=2)
    send = jnp.concatenate([send.reshape(N_DEV, 2 * depth, acols), jnp.zeros((N_DEV, 8 - 2 * depth, acols), F32)], axis=1)
    (got,) = _exchange("scatter_mod", [send], gather=False)
    mod = got[:, :2 * depth, :].transpose(1, 0, 2).reshape(depth, 2, N_MOD, 1, D)
    modv = [[mod[l, :, k] for k in range(N_MOD)] for l in range(depth)]

    cosq, sinq = _rope_tables(t, tc)
    consts = (cosq, sinq, _head_mean_matrix(Q_W), _head_tile_matrix(Q_W), _head_mean_matrix(KV_W), _head_tile_matrix(KV_W))
    xs = jnp.concatenate([ctx[0], x[0]], axis=0)
    saved, layers = [], []
    w_in_all = first[0]

    def late_weights(got):
        return {nm: jnp.concatenate([g[s] for s in range(N_DEV)], axis=ax) for (nm, ax), g in zip(late, got)}

    for l in range(depth):
        layers.append(early_weights(l, w_in_all))
        sending = [wts[nm][l].astype(BF16) for nm, _ in late] + ([w_in_shard(l + 1)] if l + 1 < depth else [])
        xs, sv, got = _layer_fwd("l%d" % l, xs, layers[l], modv[l], consts, R, nct, tc, (sending, True), late_weights)
        if l + 1 < depth:
            w_in_all = got[len(late)]
        saved.append(sv)
    loss_blk, dxs, dgf = _final_loss("final", xs, loss_target[0], wts["final_norm_g"][None, :], R, nct)
    loss = lax.psum(loss_blk[0, 0], ("x", "y", "c"))

    grads = [None] * depth
    dmods = [None] * depth
    late_parts = [None] * depth
    w_in_parts = [None] * depth
    w_in_grad_shards = lambda l: _win_shards(grads[l]["w_in"]).astype(BF16)
    for l in range(depth - 1, -1, -1):
        def make_comm(gw, l=l):
            sending = [_split_shards(gw[nm], ax).astype(BF16) for nm, ax in late]
            return sending + ([w_in_grad_shards(l + 1)] if l + 1 < depth else []), False

        dxs, grads[l], dmods[l], got = _layer_bwd("l%d" % l, dxs, saved[l], layers[l], modv[l], R, nct, tc, make_comm)
        late_parts[l] = got[:len(late)]
        if l + 1 < depth:
            w_in_parts[l + 1] = got[len(late)]
    grad_x = dxs[tc:][None]

    dmod = jnp.stack([jnp.stack(dmods[l], axis=1) for l in range(depth)])
    dmod = dmod.reshape(depth, 2, N_DEV, acols).transpose(2, 0, 1, 3).reshape(N_DEV, 2 * depth, acols)
    dmod_send = jnp.concatenate([dmod, jnp.zeros((N_DEV, 8 - 2 * depth, acols), F32)], axis=1)
    (dm_got,) = _exchange("scatter_dmod", [dmod_send], gather=False)
    g_w_ada, dc16 = _adaln_bwd("adaln_b", craw16, wts["w_ada"], dm_got[:, :2 * depth].transpose(1, 0, 2))
    db_ada_part = jnp.stack([jnp.stack(dmods[l], axis=1) for l in range(depth)]).reshape(depth, 2, N_MOD * D).sum(axis=1)

    out = {}
    kinds = ("grad", "delta", "new_m", "new_v")
    small_send = {nm: [] for nm, _ in small}
    for l in range(depth):
        g = grads[l]
        small_g = dict(conv_w=g["conv_w_h"].transpose(1, 0, 2).reshape(3, F2),
                       w_alpha2=jnp.stack([g["w2p_f"][:GLA_RANK], g["w2p_b"][GLA_RANK:2 * GLA_RANK]]),
                       b_alpha=jnp.stack([g["b2_f"][0], g["b2_b"][0]]))
        for nm, ax in small:
            small_send[nm].append(_split_shards(small_g[nm], ax))
    last = _exchange("scatter_last", [w_in_grad_shards(0)] + [jnp.stack(small_send[nm], axis=1) for nm, _ in small], gather=False)
    w_in_parts[0] = last[0]
    view2 = lambda a: a.reshape(-1, a.shape[-1])
    sharded_parts = ([jnp.stack(w_in_parts, axis=1)]
                     + [jnp.stack([late_parts[l][k] for l in range(depth)], axis=1) for k in range(len(late))] + last[1:])
    for (nm, _), parts in zip([("w_in", 1)] + late + small, sharded_parts):
        res = _adamw("adamw_" + nm, view2(wts[nm]), view2(moms[nm]), view2(vels[nm]), parts.reshape(N_DEV, -1, parts.shape[-1]))
        for kind, flat in zip(kinds, res):
            out[kind, nm] = flat.reshape(wts[nm].shape)

    rep_g = dict(
        c_ctx=dc16[8], b_ada=db_ada_part, final_norm_g=dgf[0],
        norm1_g=jnp.stack([grads[l]["norm1_g"][0] for l in range(depth)]),
        norm2_g=jnp.stack([grads[l]["norm2_g"][0] for l in range(depth)]),
        q_norm_g=jnp.stack([grads[l]["q_norm_g"][0] for l in range(depth)]),
        k_norm_g=jnp.stack([grads[l]["k_norm_g"][0] for l in range(depth)]),
        gmlp_norm_g=jnp.stack([grads[l]["gmlp_norm_g"].reshape(GMLP_W) for l in range(depth)]),
        w_spatial=jnp.stack([grads[l]["w_spatial"] for l in range(depth)]),
        b_spatial=jnp.stack([grads[l]["b_spatial_t"][:, :, 0] for l in range(depth)]),
        gla_norm_g=jnp.stack([grads[l]["gla_norm_g"].reshape(GLA_V_W) for l in range(depth)]),
        conv_b=jnp.stack([grads[l]["conv_b_h"].reshape(F2) for l in range(depth)]),
    )
    rep_shapes = [wts[nm].shape for nm in _REPLICATED]
    (rg_parts,) = _exchange("gather_rep_grads", [_pack_rows([rep_g[nm] for nm in _REPLICATED], 16)], gather=True)
    rpk = lambda src: _pack_rows([src[nm] for nm in _REPLICATED], 16)
    res = _adamw("adamw_rep", rpk(wts), rpk(moms), rpk(vels), rg_parts)
    for kind, rows in zip(kinds, res):
        for nm, piece in zip(_REPLICATED, _unpack_rows(rows, rep_shapes)):
            out[kind, nm] = piece

    res = _adamw("adamw_ada", view2(wts["w_ada"]), view2(moms["w_ada"]), view2(vels["w_ada"]), view2(g_w_ada)[None])
    for kind, flat in zip(kinds, res):
        out[kind, "w_ada"] = flat.reshape(wts["w_ada"].shape)

    return (loss, grad_x, *[out[kind, nm] for kind in kinds for nm in _WEIGHTS])


def kernel(x, c, ctx, c_ctx, w_ada, b_ada, norm1_g, norm2_g, w_in, q_norm_g, k_norm_g, gmlp_norm_g, w_spatial, b_spatial, w_alpha2, b_alpha, gla_norm_g, w_br_a, w_br_b, w_br_c, w_out, w_ffn_up, conv_w, conv_b, w_ffn_down, final_norm_g, loss_target, m_c_ctx, m_w_ada, m_b_ada, m_norm1_g, m_norm2_g, m_w_in, m_q_norm_g, m_k_norm_g, m_gmlp_norm_g, m_w_spatial, m_b_spatial, m_w_alpha2, m_b_alpha, m_gla_norm_g, m_w_br_a, m_w_br_b, m_w_br_c, m_w_out, m_w_ffn_up, m_conv_w, m_conv_b, m_w_ffn_down, m_final_norm_g, v_c_ctx, v_w_ada, v_b_ada, v_norm1_g, v_norm2_g, v_w_in, v_q_norm_g, v_k_norm_g, v_gmlp_norm_g, v_w_spatial, v_b_spatial, v_w_alpha2, v_b_alpha, v_gla_norm_g, v_w_br_a, v_w_br_b, v_w_br_c, v_w_out, v_w_ffn_up, v_conv_w, v_conv_b, v_w_ffn_down, v_final_norm_g):
    wts = dict(zip(_WEIGHTS, (c_ctx, w_ada, b_ada, norm1_g, norm2_g, w_in, q_norm_g, k_norm_g, gmlp_norm_g, w_spatial, b_spatial,
                              w_alpha2, b_alpha, gla_norm_g, w_br_a, w_br_b, w_br_c, w_out, w_ffn_up, conv_w, conv_b, w_ffn_down,
                              final_norm_g)))
    moms = dict(zip(_WEIGHTS, (m_c_ctx, m_w_ada, m_b_ada, m_norm1_g, m_norm2_g, m_w_in, m_q_norm_g, m_k_norm_g, m_gmlp_norm_g,
                               m_w_spatial, m_b_spatial, m_w_alpha2, m_b_alpha, m_gla_norm_g, m_w_br_a, m_w_br_b, m_w_br_c, m_w_out,
                               m_w_ffn_up, m_conv_w, m_conv_b, m_w_ffn_down, m_final_norm_g)))
    vels = dict(zip(_WEIGHTS, (v_c_ctx, v_w_ada, v_b_ada, v_norm1_g, v_norm2_g, v_w_in, v_q_norm_g, v_k_norm_g, v_gmlp_norm_g,
                               v_w_spatial, v_b_spatial, v_w_alpha2, v_b_alpha, v_gla_norm_g, v_w_br_a, v_w_br_b, v_w_br_c, v_w_out,
                               v_w_ffn_up, v_conv_w, v_conv_b, v_w_ffn_down, v_final_norm_g)))
    return _step((x, c, ctx, loss_target), wts, moms, vels)
```

```python
import functools

import jax
import jax.numpy as jnp
from jax import lax
from jax.experimental import pallas as pl
from jax.experimental.pallas import tpu as pltpu

F32 = jnp.float32
BF16 = jnp.bfloat16
HI = lax.Precision.HIGHEST
MESH_ID = pl.DeviceIdType.MESH

N_DEV = 8
EPS = 1e-6
D = 1024
N_MOD = 6
HEAD_DIM = 64
N_Q_HEADS = 8
N_KV_HEADS = 2
Q_GROUP = 4
Q_W = 512
KV_W = 128
GRID_W = 64
ROPE_THETA = 10000.0
ROPE_FREQS = 16
GMLP_CHUNK = 128
GMLP_GROUPS = 4
GMLP_W = 512
GLA_HEADS = 4
GLA_QK_W = 256
GLA_V_W = 512
GLA_DK = 64
GLA_DV = 128
GLA_RANK = 16
GLA_TAU = 16.0
GLA_CHUNK = 64
FFN_H = 2816
F2 = 2 * FFN_H
IN_SPLITS = (512, 512, 512, 128, 128, 256, 256, 512, 16, 16, 512, 1024, 1024, 1024)
IN_WIDTH = sum(IN_SPLITS)

C_GA, C_GB, C_GC = 0, 1024, 2048
C_U, C_V, C_Q, C_GV, C_R = 3072, 3584, 4096, 4608, 5120
C_GQ, C_GK = 5632, 5888
C_K, C_VV, C_AF = 6144, 6272, 6400
PW = 6656

ADAM_LR = 0.001
ADAM_B1 = 0.9
ADAM_B2 = 0.999
ADAM_EPS = 1e-08
ADAM_WD = 0.01
ADAM_STEP = 10

V7X_VMEM_LIMIT = 56 * 1024 * 1024

_ARB1 = pltpu.CompilerParams(dimension_semantics=("arbitrary",), vmem_limit_bytes=V7X_VMEM_LIMIT)


def _pick(dim, prefs):
    for p in prefs:
        if dim % p == 0:
            return p
    return dim


def _hdot(a, b, dims=(((1,), (0,)), ((), ()))):
    return lax.dot_general(a, b, dims, precision=HI, preferred_element_type=F32)


_NT = (((1,), (1,)), ((), ()))
_TN = (((0,), (0,)), ((), ()))
_NN = (((1,), (0,)), ((), ()))


def _matmul(name, a, b, mode, out_dtype, *, a_halves=False, b_halves=False, o_halves=False, comm=None):
    def dims2(x, halves):
        return (x.shape[1], 2 * x.shape[2]) if halves else x.shape

    ar, ac = dims2(a, a_halves)
    br, bc = dims2(b, b_halves)
    if mode == "nn":
        M, K, N = ar, ac, bc
    elif mode == "nt":
        M, K, N = ar, ac, br
    else:
        M, K, N = ac, ar, bc
    row_prefs = (768, 512, 384, 256, 128)
    n_unit = N // 2 if (o_halves or (b_halves and mode != "nt")) else N
    k_unit = K // 2 if (a_halves and mode != "tn") else K
    if mode == "tn":
        tm = _pick(M, (1024, 1408, 512, 256, 128))
        tk = _pick(K, (1408,) + row_prefs)
    else:
        tm = _pick(M, row_prefs)
        tk = _pick(k_unit, (1664, 1408, 1024, 512, 256, 128))
    tn = _pick(n_unit, (1664, 1408, 1024, 512, 256, 128))
    nk = K // tk

    def spec(shape2, halves, blk, imap):
        if not halves:
            return pl.BlockSpec(blk, imap)
        nhalf = (shape2[1] // 2) // blk[1]

        def im(i, j, k):
            r, c = imap(i, j, k)
            return (c // nhalf, r, c % nhalf)
        return pl.BlockSpec((None,) + blk, im)

    if mode == "nn":
        a_spec = spec((ar, ac), a_halves, (tm, tk), lambda i, j, k: (i, k))
        b_spec = spec((br, bc), b_halves, (tk, tn), lambda i, j, k: (k, j))
        dn = _NN
    elif mode == "nt":
        a_spec = spec((ar, ac), a_halves, (tm, tk), lambda i, j, k: (i, k))
        b_spec = spec((br, bc), b_halves, (tn, tk), lambda i, j, k: (j, k))
        dn = _NT
    else:
        a_spec = spec((ar, ac), a_halves, (tk, tm), lambda i, j, k: (k, i))
        b_spec = spec((br, bc), b_halves, (tk, tn), lambda i, j, k: (k, j))
        dn = _TN
    o_spec = spec((M, N), o_halves, (tm, tn), lambda i, j, k: (i, j))
    o_shape = (2, M, N // 2) if o_halves else (M, N)

    def body(a_ref, b_ref, o_ref, acc_ref):
        k = pl.program_id(2)
        part = lax.dot_general(a_ref[...], b_ref[...], dn, preferred_element_type=F32)
        if nk == 1:
            o_ref[...] = part.astype(o_ref.dtype)
        else:
            @pl.when(k == 0)
            def _():
                acc_ref[...] = part

            @pl.when(k > 0)
            def _():
                acc_ref[...] += part

            @pl.when(k == nk - 1)
            def _():
                o_ref[...] = acc_ref[...].astype(o_ref.dtype)

    grid = (M // tm, N // tn, nk)
    body, xspecs, xshapes, xsems = _with_exchange(body, 2, 1, grid, comm)
    res = pl.pallas_call(
        body, name=name, grid=grid,
        in_specs=[a_spec, b_spec] + xspecs, out_specs=[o_spec] + xspecs,
        out_shape=[jax.ShapeDtypeStruct(o_shape, out_dtype)] + xshapes,
        scratch_shapes=[pltpu.VMEM((tm, tn), F32)] + xsems,
        compiler_params=pltpu.CompilerParams(dimension_semantics=("arbitrary", "arbitrary", "arbitrary"),
                                             vmem_limit_bytes=V7X_VMEM_LIMIT),
    )(a, b, *(comm[0] if comm else []))
    return res[0] if comm is None else (res[0], list(res[1:]))


def _full_spec(shape):
    nd = len(shape)
    return pl.BlockSpec(tuple(shape), lambda i, _nd=nd: (0,) * _nd)


def _row_spec(R, W, cb):
    return pl.BlockSpec((R, W), lambda i, _cb=cb: (i, _cb))


def _load_rows(refs, specs):
    vals = []
    for ref, (_, W, _, pw) in zip(refs, specs):
        if pw == W:
            vals.append(ref[...].astype(F32))
        else:
            vals.append([ref[:, k * pw:(k + 1) * pw].astype(F32) for k in range(W // pw)])
    return vals


def _load_params(refs, specs):
    vals = []
    for ref, (arr, split) in zip(refs, specs):
        if split:
            vals.append([ref[k] for k in range(arr.shape[0])])
        else:
            vals.append(ref[...])
    return vals


def _mod_spec(nct, width):
    return pl.BlockSpec((None, 1, width), lambda i: (jnp.minimum(i // nct, 1), 0, 0))


def _rowwise(name, f, R, nct, rows, consts, params, cparams, mods, outs, post=None):
    n = rows[0][0].shape[0]
    nr, nc, npar, ncp, nm = len(rows), len(consts), len(params), len(cparams), len(mods)

    def body(*refs):
        pos = 0
        rr = refs[pos:pos + nr]; pos += nr
        cr = refs[pos:pos + nc]; pos += nc
        pr = refs[pos:pos + npar]; pos += npar
        cpr = refs[pos:pos + ncp]; pos += ncp
        mr = refs[pos:pos + nm]; pos += nm
        orefs = refs[pos:]
        res = f(_load_rows(rr, rows), _load_params(pr, params), [m[...] for m in mr],
                _load_rows(cr, consts), _load_params(cpr, cparams))
        if post is not None:
            res = post(res, _load_rows(cr, consts))
        for o_ref, r in zip(orefs, res):
            o_ref[...] = r.astype(o_ref.dtype)

    in_specs = ([_row_spec(R, W, cb) for (_, W, cb, _) in rows + consts]
                + [_full_spec(a.shape) for (a, _) in params + cparams]
                + [_mod_spec(nct, m.shape[2]) for m in mods])
    args = [a for (a, _, _, _) in rows + consts] + [a for (a, _) in params + cparams] + list(mods)
    return pl.pallas_call(
        body, name=name, grid=(n // R,), in_specs=in_specs,
        out_specs=[_row_spec(R, w, 0) for (w, _) in outs],
        out_shape=[jax.ShapeDtypeStruct((n, w), dt) for (w, dt) in outs],
        compiler_params=_ARB1,
    )(*args)


def _rowwise_bwd(name, f, R, nct, rows, consts, params, cparams, mods, douts, drow, adds=None, pre=None):
    n = rows[0][0].shape[0]
    adds = adds or [None] * len(rows)
    nr, nc, npar, ncp, nm, nd = len(rows), len(consts), len(params), len(cparams), len(mods), len(douts)
    add_ix = [k for k in range(nr) if adds[k] is not None]
    out_ix = [k for k in range(nr) if drow[k] is not None]

    def body(*refs):
        i = pl.program_id(0)
        pos = 0
        rr = refs[pos:pos + nr]; pos += nr
        cr = refs[pos:pos + nc]; pos += nc
        pr = refs[pos:pos + npar]; pos += npar
        cpr = refs[pos:pos + ncp]; pos += ncp
        mr = refs[pos:pos + nm]; pos += nm
        dr = refs[pos:pos + nd]; pos += nd
        ar = refs[pos:pos + len(add_ix)]; pos += len(add_ix)
        drr = refs[pos:pos + len(out_ix)]; pos += len(out_ix)
        dpr = refs[pos:pos + npar]; pos += npar
        dmr = refs[pos:pos + nm]; pos += nm

        cv = _load_rows(cr, consts)
        cpv = _load_params(cpr, cparams)
        _, vjp = jax.vjp(lambda rv, pv, mv: f(rv, pv, mv, cv, cpv),
                         _load_rows(rr, rows), _load_params(pr, params), [m[...] for m in mr])
        dv = [d[...].astype(F32) for d in dr]
        if pre is not None:
            dv = pre(dv, cv)
        g_rows, g_params, g_mods = vjp(tuple(dv))

        for ref, k in zip(drr, out_ix):
            _, W, _, pw = rows[k]
            g = g_rows[k]
            extra = ar[add_ix.index(k)] if k in add_ix else None
            if pw == W:
                if extra is not None:
                    g = g + extra[...].astype(F32)
                ref[...] = g.astype(ref.dtype)
            else:
                for q in range(W // pw):
                    gq = g[q]
                    if extra is not None:
                        gq = gq + extra[:, q * pw:(q + 1) * pw].astype(F32)
                    ref[:, q * pw:(q + 1) * pw] = gq.astype(ref.dtype)

        @pl.when(i == 0)
        def _():
            for ref in dpr:
                ref[...] = jnp.zeros_like(ref)

        for ref, (arr, split), g in zip(dpr, params, g_params):
            if split:
                for k in range(arr.shape[0]):
                    ref[k] += g[k]
            else:
                ref[...] += g

        @pl.when((i == 0) | (i == nct))
        def _():
            for ref in dmr:
                ref[...] = jnp.zeros_like(ref)

        for ref, g in zip(dmr, g_mods):
            ref[...] += g

    in_specs = ([_row_spec(R, W, cb) for (_, W, cb, _) in rows + consts]
                + [_full_spec(a.shape) for (a, _) in params + cparams]
                + [_mod_spec(nct, m.shape[2]) for m in mods]
                + [_row_spec(R, W, 0) for (_, W) in douts]
                + [_row_spec(R, rows[k][1], 0) for k in add_ix])
    args = ([a for (a, _, _, _) in rows + consts] + [a for (a, _) in params + cparams] + list(mods)
            + [a for (a, _) in douts] + [adds[k] for k in add_ix])
    out_specs = ([_row_spec(R, rows[k][1], 0) for k in out_ix]
                 + [_full_spec(a.shape) for (a, _) in params]
                 + [_mod_spec(nct, m.shape[2]) for m in mods])
    out_shape = ([jax.ShapeDtypeStruct((n, rows[k][1]), drow[k]) for k in out_ix]
                 + [jax.ShapeDtypeStruct(a.shape, F32) for (a, _) in params]
                 + [jax.ShapeDtypeStruct(m.shape, F32) for m in mods])
    res = pl.pallas_call(
        body, name=name, grid=(n // R,), in_specs=in_specs, out_specs=out_specs, out_shape=out_shape,
        compiler_params=_ARB1,
    )(*args)
    no = len(out_ix)
    return list(res[:no]), list(res[no:no + npar]), list(res[no + npar:])


def _rms(x, g):
    return x * lax.rsqrt(jnp.mean(x * x, axis=-1, keepdims=True) + EPS) * g


def _f_lnmod(rv, pv, mv, cv, cpv):
    (x,), (g,), (shift, scale) = rv, pv, mv
    return (_rms(x, g) * (1.0 + scale) + shift,)


def _f_resid(rv, pv, mv, cv, cpv):
    (x, y), (gate,) = rv, mv
    return (x + gate * y,)


def _f_merge(rv, pv, mv, cv, cpv):
    ga, gb, gc, ya, yb, yc = rv
    return (_sigmoid(ga) * ya + _sigmoid(gb) * yb + _sigmoid(gc) * yc,)


def _split3(x):
    hi = x.astype(BF16)
    rest = x - hi.astype(F32)
    mid = rest.astype(BF16)
    return hi, mid, (rest - mid.astype(F32)).astype(BF16)


def _dot3_right(x, m):
    mb = m.astype(BF16)
    return sum(lax.dot_general(piece, mb, _NN, preferred_element_type=F32) for piece in _split3(x))


@jax.custom_vjp
def _sym_dot(x, m):
    return _dot3_right(x, m)


_sym_dot.defvjp(lambda x, m: (_dot3_right(x, m), m), lambda m, g: (_dot3_right(g, m), jnp.zeros_like(m)))


def _f_qknorm(rv, pv, mv, cv, cpv):
    (q, k), (gq, gk), (bdq, eq, bdk, ek) = rv, pv, cpv
    qn = q * lax.rsqrt(_sym_dot(q * q, bdq) + EPS) * _hdot(gq, eq)
    kn = k * lax.rsqrt(_sym_dot(k * k, bdk) + EPS) * _hdot(gk, ek)
    return (qn, kn)


def _rope(x, cos, sin):
    w = x.shape[1]
    lane = lax.broadcasted_iota(jnp.int32, x.shape, 1)
    partner = jnp.where((lane & 31) < 16, pltpu.roll(x, w - 16, 1), pltpu.roll(x, 16, 1))
    return x * cos + partner * sin


_LOG2E = 1.4426950408889634
_LN2 = 0.6931471805599453


def _qk_post(res, cv):
    (qn, kn), (cq, sq, ck, sk) = res, cv
    return (_rope(qn, cq, sq) * (HEAD_DIM ** -0.5 * _LOG2E), _rope(kn, ck, sk))


def _qk_pre(dv, cv):
    (dq, dk), (cq, sq, ck, sk) = dv, cv
    return (_rope(dq * (HEAD_DIM ** -0.5), cq, -sq), _rope(dk * _LN2, ck, -sk))


def _f_gmlp(rv, pv, mv, cv, cpv):
    (u, vs), (ng, ws, bt) = rv, pv
    pieces = []
    for g in range(GMLP_GROUPS):
        vn = _rms(jax.nn.gelu(vs[g]), ng[g])
        pieces.append(_bdot(ws[g], vn, "nn") + bt[g])
    return (jax.nn.gelu(u) * jnp.concatenate(pieces, axis=1),)


def _f_glaout(rv, pv, mv, cv, cpv):
    (os_, rs), (gn,) = rv, pv
    pieces = [_rms(os_[h], gn[h]) * (rs[h] * _sigmoid(rs[h])) for h in range(GLA_HEADS)]
    return (jnp.concatenate(pieces, axis=1),)


_CONV_CB = 1408


def _sigmoid(x):
    return 0.5 * jnp.tanh(0.5 * x) + 0.5


def _halo_keep(i, R, tc, n):
    first, end = i * R, (i + 1) * R
    keep_prev = jnp.where((first == 0) | (first == tc), 0.0, 1.0)
    keep_next = jnp.where((end == tc) | (end == n), 0.0, 1.0)
    return keep_prev, keep_next


def _conv_specs(R, n):
    nb8 = n // 8
    main = pl.BlockSpec((2, R, _CONV_CB), lambda j, i: (0, i, j))
    prev = pl.BlockSpec((2, 8, _CONV_CB), lambda j, i: (0, jnp.maximum(i * (R // 8) - 1, 0), j))
    nxt = pl.BlockSpec((2, 8, _CONV_CB), lambda j, i: (0, jnp.minimum((i + 1) * (R // 8), nb8 - 1), j))
    cw = pl.BlockSpec((2, 3, _CONV_CB), lambda j, i: (0, 0, j))
    cb = pl.BlockSpec((2, 1, _CONV_CB), lambda j, i: (0, 0, j))
    return main, prev, nxt, cw, cb


def _conv_act(ext_ref, h, first, count, cw_ref, cb_ref):
    prev = ext_ref[h, pl.ds(first - 1, count), :]
    mid = ext_ref[h, pl.ds(first, count), :]
    nxt = ext_ref[h, pl.ds(first + 1, count), :]
    return cb_ref[h] + cw_ref[h, 1:2, :] * mid + cw_ref[h, 0:1, :] * prev + cw_ref[h, 2:3, :] * nxt


def _conv_gate(name, a2, cw, cb, R, tc):
    n = a2.shape[1]
    main, prev, nxt, cws, cbs = _conv_specs(R, n)

    def body(a_ref, p_ref, n_ref, cw_ref, cb_ref, o_ref, ext):
        i = pl.program_id(1)
        keep_prev, keep_next = _halo_keep(i, R, tc, n)
        ext[:, 0:8, :] = jnp.zeros((2, 8, _CONV_CB), F32)
        ext[:, 8:16, :] = p_ref[...] * keep_prev
        ext[:, 16:16 + R, :] = a_ref[...]
        ext[:, 16 + R:24 + R, :] = n_ref[...] * keep_next
        ext[:, 24 + R:32 + R, :] = jnp.zeros((2, 8, _CONV_CB), F32)
        g = _conv_act(ext, 0, 16, R, cw_ref, cb_ref)
        v = _conv_act(ext, 1, 16, R, cw_ref, cb_ref)
        o_ref[...] = (g * _sigmoid(g) * v).astype(o_ref.dtype)

    return pl.pallas_call(
        body, name=name, grid=(FFN_H // _CONV_CB, n // R),
        in_specs=[main, prev, nxt, cws, cbs],
        out_specs=pl.BlockSpec((R, _CONV_CB), lambda j, i: (i, j)),
        out_shape=jax.ShapeDtypeStruct((n, FFN_H), BF16),
        scratch_shapes=[pltpu.VMEM((2, R + 32, _CONV_CB), F32)],
        compiler_params=pltpu.CompilerParams(dimension_semantics=("arbitrary", "arbitrary"),
                                             vmem_limit_bytes=V7X_VMEM_LIMIT),
    )(a2, a2, a2, cw, cb)


def _conv_gate_bwd(name, a2, dgv, cw, cb, R, tc):
    n = a2.shape[1]
    nb8 = n // 8
    main, prev, nxt, cws, cbs = _conv_specs(R, n)
    d_main = pl.BlockSpec((R, _CONV_CB), lambda j, i: (i, j))
    d_prev = pl.BlockSpec((8, _CONV_CB), lambda j, i: (jnp.maximum(i * (R // 8) - 1, 0), j))
    d_next = pl.BlockSpec((8, _CONV_CB), lambda j, i: (jnp.minimum((i + 1) * (R // 8), nb8 - 1), j))
    RE = R + 16

    def body(a_ref, p_ref, n_ref, cw_ref, cb_ref, d_ref, dp_ref, dn_ref, da_ref, dcw_ref, dcb_ref, ext, dext):
        i = pl.program_id(1)
        keep_prev, keep_next = _halo_keep(i, R, tc, n)
        zeros8 = jnp.zeros((2, 8, _CONV_CB), F32)
        ext[:, 0:8, :] = zeros8
        ext[:, 8:16, :] = p_ref[...] * keep_prev
        ext[:, 16:16 + R, :] = a_ref[...]
        ext[:, 16 + R:24 + R, :] = n_ref[...] * keep_next
        ext[:, 24 + R:32 + R, :] = zeros8
        g = _conv_act(ext, 0, 8, RE, cw_ref, cb_ref)
        v = _conv_act(ext, 1, 8, RE, cw_ref, cb_ref)
        dout = jnp.concatenate([dp_ref[...].astype(F32) * keep_prev, d_ref[...].astype(F32),
                                dn_ref[...].astype(F32) * keep_next], axis=0)
        sg = _sigmoid(g)
        dext[:, 0:8, :] = zeros8
        dext[:, 24 + R:32 + R, :] = zeros8
        dext[0, 8:8 + RE, :] = dout * v * sg * (1.0 + g * (1.0 - sg))
        dext[1, 8:8 + RE, :] = dout * g * sg

        @pl.when(i == 0)
        def _():
            dcw_ref[...] = jnp.zeros_like(dcw_ref)
            dcb_ref[...] = jnp.zeros_like(dcb_ref)

        for h in range(2):
            da = dext[h, 16:16 + R, :]
            da_up = dext[h, 17:17 + R, :]
            da_dn = dext[h, 15:15 + R, :]
            da_ref[h] = (cw_ref[h, 1:2, :] * da + cw_ref[h, 0:1, :] * da_up + cw_ref[h, 2:3, :] * da_dn).astype(da_ref.dtype)
            a_prev = ext[h, 15:15 + R, :]
            a_mid = ext[h, 16:16 + R, :]
            a_next = ext[h, 17:17 + R, :]
            dcw_ref[h, 0:1, :] += jnp.sum(da * a_prev, axis=0, keepdims=True)
            dcw_ref[h, 1:2, :] += jnp.sum(da * a_mid, axis=0, keepdims=True)
            dcw_ref[h, 2:3, :] += jnp.sum(da * a_next, axis=0, keepdims=True)
            dcb_ref[h] += jnp.sum(da, axis=0, keepdims=True)

    return pl.pallas_call(
        body, name=name, grid=(FFN_H // _CONV_CB, n // R),
        in_specs=[main, prev, nxt, cws, cbs, d_main, d_prev, d_next],
        out_specs=[main, cws, cbs],
        out_shape=[jax.ShapeDtypeStruct((2, n, FFN_H), BF16), jax.ShapeDtypeStruct((2, 3, FFN_H), F32),
                   jax.ShapeDtypeStruct((2, 1, FFN_H), F32)],
        scratch_shapes=[pltpu.VMEM((2, R + 32, _CONV_CB), F32), pltpu.VMEM((2, R + 32, _CONV_CB), F32)],
        compiler_params=pltpu.CompilerParams(dimension_semantics=("arbitrary", "arbitrary"),
                                             vmem_limit_bytes=V7X_VMEM_LIMIT),
    )(a2, a2, a2, cw, cb, dgv, dgv, dgv)


_ARB2 = pltpu.CompilerParams(dimension_semantics=("arbitrary", "arbitrary"), vmem_limit_bytes=V7X_VMEM_LIMIT)


def _with_exchange(body, n_in, n_out, grid, comm):
    if comm is None:
        return body, [], [], []
    xs, gather = comm
    na = len(xs)
    specs, shapes, sems = _exchange_io(xs, gather)

    def wrapped(*refs):
        ins, x_refs = refs[:n_in], refs[n_in:n_in + na]
        outs, o_refs = refs[n_in + na:n_in + na + n_out], refs[n_in + na + n_out:n_in + 2 * na + n_out]
        scratch, sem_refs = refs[n_in + 2 * na + n_out:-3], refs[-3:]
        start, wait = _exchange_plan(x_refs, o_refs, *sem_refs, gather)
        ids = [pl.program_id(d) for d in range(len(grid))]
        first, last = ids[0] == 0, ids[0] == grid[0] - 1
        for d in range(1, len(grid)):
            first, last = first & (ids[d] == 0), last & (ids[d] == grid[d] - 1)

        @pl.when(first)
        def _():
            start()

        body(*ins, *outs, *scratch)

        @pl.when(last)
        def _():
            wait()

    return wrapped, specs, shapes, sems


def _attn_fwd(name, qt8, k2, vt2, comm=None):
    nq, nk = qt8.shape[2], k2.shape[1]
    tq = _pick(nq, (256, 128))
    tk = _pick(nk, (768, 256, 128))

    va = vt2.shape[1]

    def body(qt_ref, k_ref, vt_ref, ot_ref, lse_ref, m_ref, acc_ref):
        m_ref[...] = jnp.full((Q_GROUP, 1, tq), -1e30, F32)
        acc_ref[...] = jnp.zeros((Q_GROUP, va, tq), F32)

        def step(j, carry):
            sl = pl.ds(pl.multiple_of(j * tk, tk), tk)
            kj = k_ref[sl, :]
            vtj = vt_ref[:, sl]
            sts = [lax.dot_general(kj, qt_ref[h], _NN, preferred_element_type=F32) for h in range(Q_GROUP)]
            m_old = [m_ref[h] for h in range(Q_GROUP)]
            acc_old = [acc_ref[h] for h in range(Q_GROUP)]
            m_new = [jnp.maximum(m_old[h], jnp.max(sts[h], axis=0, keepdims=True)) for h in range(Q_GROUP)]
            pts = [jnp.exp2(sts[h] - m_new[h]).astype(BF16) for h in range(Q_GROUP)]
            pvs = [lax.dot_general(vtj, pts[h], _NN, preferred_element_type=F32) for h in range(Q_GROUP)]
            for h in range(Q_GROUP):
                acc_ref[h] = jnp.exp2(m_old[h] - m_new[h]) * acc_old[h] + pvs[h]
                m_ref[h] = m_new[h]
            return carry

        lax.fori_loop(0, nk // tk, step, 0, unroll=2)
        for h in range(Q_GROUP):
            l = acc_ref[h, HEAD_DIM:HEAD_DIM + 1, :]
            ot_ref[h] = (acc_ref[h, 0:HEAD_DIM, :] / l).astype(ot_ref.dtype)
            lse_ref[h] = m_ref[h] + jnp.log2(l)

    qspec = pl.BlockSpec((Q_GROUP, HEAD_DIM, tq), lambda g, i: (g, 0, i))
    lspec = pl.BlockSpec((Q_GROUP, 1, tq), lambda g, i: (g, 0, i))
    grid = (N_KV_HEADS, nq // tq)
    body, xspecs, xshapes, xsems = _with_exchange(body, 3, 2, grid, comm)
    res = pl.pallas_call(
        body, name=name, grid=grid,
        in_specs=[qspec, pl.BlockSpec((None, nk, HEAD_DIM), lambda g, i: (g, 0, 0)),
                  pl.BlockSpec((None, va, nk), lambda g, i: (g, 0, 0))] + xspecs,
        out_specs=[qspec, lspec] + xspecs,
        out_shape=[jax.ShapeDtypeStruct((N_Q_HEADS, HEAD_DIM, nq), BF16), jax.ShapeDtypeStruct((N_Q_HEADS, 1, nq), F32)] + xshapes,
        scratch_shapes=[pltpu.VMEM((Q_GROUP, 1, tq), F32), pltpu.VMEM((Q_GROUP, va, tq), F32)] + xsems,
        compiler_params=_ARB2,
    )(qt8, k2, vt2, *(comm[0] if comm else []))
    return res[0], res[1], list(res[2:])


def _attn_bwd(name, qt8, k2, v2, kt2, ot8, dot8, lse, comm=None):
    nq, nk = qt8.shape[2], k2.shape[1]
    tq = _pick(nq, (256, 128))
    tk = _pick(nk, (768, 256, 128))
    ts = tk
    heads = range(Q_GROUP)

    def body(qt_ref, k_ref, v_ref, kt_ref, ot_ref, dot_ref, lse_ref, dqt_ref, dkt_ref, dvt_ref, dl_ref, dq_acc):
        @pl.when(pl.program_id(1) == 0)
        def _():
            dkt_ref[...] = jnp.zeros_like(dkt_ref)
            dvt_ref[...] = jnp.zeros_like(dvt_ref)

        for h in heads:
            dl_ref[h] = jnp.sum(dot_ref[h].astype(F32) * ot_ref[h].astype(F32), axis=0, keepdims=True)
        dq_acc[...] = jnp.zeros((Q_GROUP, HEAD_DIM, tq), F32)

        items = [(s, h) for s in range(tk // ts) for h in heads]

        def step(j, carry):
            def keys(s):
                return pl.ds(pl.multiple_of(j * tk + s * ts, ts), ts)

            def scores(item):
                s, h = item
                return (lax.dot_general(k_ref[keys(s), :], qt_ref[h], _NN, preferred_element_type=F32),
                        lax.dot_general(v_ref[keys(s), :], dot_ref[h], _NN, preferred_element_type=F32))

            nxt = scores(items[0])
            for n, (s, h) in enumerate(items):
                st, dpt = nxt
                if n + 1 < len(items):
                    nxt = scores(items[n + 1])
                pt = jnp.exp2(st - lse_ref[h])
                dst = (pt * (dpt - dl_ref[h])).astype(BF16)
                dq_acc[h] += lax.dot_general(kt_ref[:, keys(s)], dst, _NN, preferred_element_type=F32)
                dv_h = lax.dot_general(dot_ref[h], pt.astype(BF16), _NT, preferred_element_type=F32)
                dk_h = lax.dot_general(qt_ref[h], dst, _NT, preferred_element_type=F32)
                dvt_s, dkt_s = (dv_h, dk_h) if h == 0 else (dvt_s + dv_h, dkt_s + dk_h)
                if h == Q_GROUP - 1:
                    dvt_ref[:, keys(s)] += dvt_s
                    dkt_ref[:, keys(s)] += dkt_s
            return carry

        lax.fori_loop(0, nk // tk, step, 0)
        dqt_ref[...] = dq_acc[...]

    tspec = pl.BlockSpec((Q_GROUP, HEAD_DIM, tq), lambda g, i: (g, 0, i))
    lspec = pl.BlockSpec((Q_GROUP, 1, tq), lambda g, i: (g, 0, i))
    kspec = pl.BlockSpec((None, nk, HEAD_DIM), lambda g, i: (g, 0, 0))
    ktspec = pl.BlockSpec((None, HEAD_DIM, nk), lambda g, i: (g, 0, 0))
    grid = (N_KV_HEADS, nq // tq)
    body, xspecs, xshapes, xsems = _with_exchange(body, 7, 3, grid, comm)
    res = pl.pallas_call(
        body, name=name, grid=grid,
        in_specs=[tspec, kspec, kspec, ktspec, tspec, tspec, lspec] + xspecs,
        out_specs=[tspec, ktspec, ktspec] + xspecs,
        out_shape=[jax.ShapeDtypeStruct((N_Q_HEADS, HEAD_DIM, nq), F32), jax.ShapeDtypeStruct((N_KV_HEADS, HEAD_DIM, nk), F32),
                   jax.ShapeDtypeStruct((N_KV_HEADS, HEAD_DIM, nk), F32)] + xshapes,
        scratch_shapes=[pltpu.VMEM((Q_GROUP, 1, tq), F32), pltpu.VMEM((Q_GROUP, HEAD_DIM, tq), F32)] + xsems,
        compiler_params=_ARB2,
    )(qt8, k2, v2, kt2, ot8, dot8, lse, *(comm[0] if comm else []))
    return res[0], res[1], res[2], list(res[3:])


def _log_sigmoid(z):
    return jnp.minimum(z, 0.0) - jnp.log(1.0 + jnp.exp(-jnp.abs(z)))


_BDOT_DIMS = {"nn": _NN, "nt": _NT, "tn": _TN}
_BDOT_BWD = {"nn": (("nt", "gb"), ("tn", "ag")), "nt": (("nn", "gb"), ("tn", "ga")), "tn": (("nt", "bg"), ("nn", "ag"))}


def _bdot_raw(a, b, mode):
    return lax.dot_general(a.astype(BF16), b.astype(BF16), _BDOT_DIMS[mode], preferred_element_type=F32)


@functools.partial(jax.custom_vjp, nondiff_argnums=(2,))
def _bdot(a, b, mode):
    return _bdot_raw(a, b, mode)


def _bdot_fwd(a, b, mode):
    return _bdot_raw(a, b, mode), (a.astype(BF16), b.astype(BF16))


def _bdot_bwd(mode, res, g):
    ops = {"a": res[0], "b": res[1], "g": g}
    (ma, oa), (mb, ob) = _BDOT_BWD[mode]
    return _bdot_raw(ops[oa[0]], ops[oa[1]], ma), _bdot_raw(ops[ob[0]], ops[ob[1]], mb)


_bdot.defvjp(_bdot_fwd, _bdot_bwd)


def _tile_tri(rev, rows):
    r_i = lax.broadcasted_iota(jnp.int32, (rows, rows), 0)
    c_i = lax.broadcasted_iota(jnp.int32, (rows, rows), 1)
    same = (r_i // GLA_CHUNK) == (c_i // GLA_CHUNK)
    return same & ((c_i >= r_i) if rev else (c_i <= r_i))


def _tri_dot(rev, x):
    tri = _tile_tri(rev, x.shape[0]).astype(BF16)
    return sum(lax.dot_general(tri, piece, _NN, preferred_element_type=F32) for piece in _split3(x))


@functools.partial(jax.custom_vjp, nondiff_argnums=(1,))
def _chunk_cumsum(x, rev):
    return _tri_dot(rev, x)


_chunk_cumsum.defvjp(lambda x, rev: (_tri_dot(rev, x), None), lambda rev, _, g: (_tri_dot(not rev, g),))


def _gla_tile(q, k, vs, a, w2, b2, state_t, *, rev):
    rows = q.shape[0]
    nch = rows // GLA_CHUNK
    tri = _tile_tri(rev, rows)
    chunk_of_row = lax.broadcasted_iota(jnp.int32, (rows, 1), 0) // GLA_CHUNK
    in_chunk = [(chunk_of_row == c).astype(F32) for c in range(nch)]
    la = _log_sigmoid(_bdot(a, w2, "nn") + b2) * (1.0 / GLA_TAU)
    cum = _chunk_cumsum(la, rev)
    tots = [jnp.sum(la * in_chunk[c], axis=0, keepdims=True) for c in range(nch)]
    tot_rows = sum(in_chunk[c] * tots[c] for c in range(nch))
    q_in = q * (GLA_DK ** -0.5) * jnp.exp(cum)
    k_in = k * jnp.exp(-cum)
    k_st = k * jnp.exp(tot_rows - cum)
    lane = lax.broadcasted_iota(jnp.int32, (1, GLA_QK_W), 1)
    outs = []
    for h in range(GLA_HEADS):
        head = ((lane >= GLA_DK * h) & (lane < GLA_DK * (h + 1))).astype(F32)
        att = jnp.where(tri, _bdot(q_in * head, k_in, "nt"), 0.0)
        outs.append(_bdot(att, vs[h], "nn"))
    o = jnp.concatenate(outs, axis=1)
    hr = lax.broadcasted_iota(jnp.int32, (GLA_V_W, GLA_QK_W), 0) // GLA_DV
    hc = lax.broadcasted_iota(jnp.int32, (GLA_V_W, GLA_QK_W), 1) // GLA_DK
    same_head = (hr == hc).astype(F32)
    v_all = jnp.concatenate(vs, axis=1)
    for c in (range(nch - 1, -1, -1) if rev else range(nch)):
        o = o + _bdot(q_in * in_chunk[c], state_t, "nt")
        state_t = jnp.exp(tots[c]) * state_t + _bdot(v_all, k_st * in_chunk[c], "tn") * same_head
    return o, state_t


def _gla_tile_of(step, rev, nct, nt):
    if not rev:
        return step
    return jnp.where(step < nct, nct - 1 - step, nt - 1 - (step - nct))


def _gla_row_specs(R, tile):
    return [pl.BlockSpec((R, GLA_QK_W), lambda s: (tile(s), C_GQ // GLA_QK_W)),
            pl.BlockSpec((R, GLA_QK_W), lambda s: (tile(s), C_GK // GLA_QK_W)),
            pl.BlockSpec((R, GLA_V_W), lambda s: (tile(s), C_GV // GLA_V_W)),
            pl.BlockSpec((R, 128), lambda s: (tile(s), C_AF // 128))]


def _gla_dir(name, p, w2, b2, *, rev, R, nct, add=None):
    n = p.shape[0]
    nt = n // R
    tile = lambda s: _gla_tile_of(s, rev, nct, nt)

    def body(q_ref, k_ref, v_ref, a_ref, w2_ref, b2_ref, *rest):
        if add is not None:
            add_ref, o_ref, ssave_ref, state = rest
        else:
            o_ref, ssave_ref, state = rest

        @pl.when(pl.program_id(0) == 0)
        def _():
            state[...] = jnp.zeros_like(state)

        vs = [v_ref[:, GLA_DV * h:GLA_DV * (h + 1)] for h in range(GLA_HEADS)]
        s_in = state[...]
        ssave_ref[...] = s_in
        o, s_out = _gla_tile(q_ref[...], k_ref[...], vs, a_ref[...], w2_ref[...], b2_ref[...], s_in, rev=rev)
        if add is not None:
            o = o + add_ref[...]
        o_ref[...] = o
        state[...] = s_out

    o_spec = pl.BlockSpec((R, GLA_V_W), lambda s: (tile(s), 0))
    in_specs = _gla_row_specs(R, tile) + [_full_spec(w2.shape), _full_spec(b2.shape)]
    args = [p, p, p, p, w2, b2]
    if add is not None:
        in_specs.append(o_spec)
        args.append(add)
    return pl.pallas_call(
        body, name=name, grid=(nt,), in_specs=in_specs,
        out_specs=[o_spec, pl.BlockSpec((None, GLA_V_W, GLA_QK_W), lambda s: (tile(s), 0, 0))],
        out_shape=[jax.ShapeDtypeStruct((n, GLA_V_W), F32), jax.ShapeDtypeStruct((nt, GLA_V_W, GLA_QK_W), F32)],
        scratch_shapes=[pltpu.VMEM((GLA_V_W, GLA_QK_W), F32)],
        compiler_params=_ARB1,
    )(*args)


def _gla_dir_bwd(name, p, w2, b2, ssave, do, *, rev, R, nct, adds=None, out_dtype=F32):
    n = p.shape[0]
    nt = n // R
    tile = lambda s: _gla_tile_of(nt - 1 - s, rev, nct, nt)
    widths = (GLA_QK_W, GLA_QK_W, GLA_V_W, 128)

    def body(q_ref, k_ref, v_ref, a_ref, w2_ref, b2_ref, ss_ref, do_ref, *rest):
        if adds is not None:
            add_refs, rest = rest[:4], rest[4:]
        dq_ref, dk_ref, dv_ref, da_ref, dw2_ref, db2_ref, dstate = rest

        @pl.when(pl.program_id(0) == 0)
        def _():
            dstate[...] = jnp.zeros_like(dstate)
            dw2_ref[...] = jnp.zeros_like(dw2_ref)
            db2_ref[...] = jnp.zeros_like(db2_ref)

        vs = [v_ref[:, GLA_DV * h:GLA_DV * (h + 1)] for h in range(GLA_HEADS)]
        _, vjp = jax.vjp(functools.partial(_gla_tile, rev=rev), q_ref[...], k_ref[...], vs, a_ref[...],
                         w2_ref[...], b2_ref[...], ss_ref[...])
        dq, dk, dvs, da, dw2, db2, ds = vjp((do_ref[...], dstate[...]))
        grads = [dq, dk, jnp.concatenate(dvs, axis=1), da]
        if adds is not None:
            grads = [g + r[...].astype(F32) for g, r in zip(grads, add_refs)]
        for ref, g in zip((dq_ref, dk_ref, dv_ref, da_ref), grads):
            ref[...] = g.astype(ref.dtype)
        dw2_ref[...] += dw2
        db2_ref[...] += db2
        dstate[...] = ds

    d_specs = [pl.BlockSpec((R, w), lambda s: (tile(s), 0)) for w in widths]
    in_specs = (_gla_row_specs(R, tile) + [_full_spec(w2.shape), _full_spec(b2.shape),
                pl.BlockSpec((None, GLA_V_W, GLA_QK_W), lambda s: (tile(s), 0, 0)),
                pl.BlockSpec((R, GLA_V_W), lambda s: (tile(s), 0))])
    args = [p, p, p, p, w2, b2, ssave, do]
    if adds is not None:
        in_specs += d_specs
        args += list(adds)
    return pl.pallas_call(
        body, name=name, grid=(nt,), in_specs=in_specs,
        out_specs=d_specs + [_full_spec(w2.shape), _full_spec(b2.shape)],
        out_shape=[jax.ShapeDtypeStruct((n, w), out_dtype) for w in widths]
        + [jax.ShapeDtypeStruct(w2.shape, F32), jax.ShapeDtypeStruct(b2.shape, F32)],
        scratch_shapes=[pltpu.VMEM((GLA_V_W, GLA_QK_W), F32)],
        compiler_params=_ARB1,
    )(*args)


def _final_loss(name, x, target, gf, R, nct):
    n = x.shape[0]

    def body(x_ref, t_ref, g_ref, loss_ref, dx_ref, dg_ref):
        i = pl.program_id(0)

        @pl.when(i == 0)
        def _():
            loss_ref[...] = jnp.zeros_like(loss_ref)
            dg_ref[...] = jnp.zeros_like(dg_ref)

        @pl.when(i < nct)
        def _():
            dx_ref[...] = jnp.zeros_like(dx_ref)

        @pl.when(i >= nct)
        def _():
            y, vjp = jax.vjp(_rms, x_ref[...], g_ref[...])
            err = y - t_ref[...]
            loss_ref[...] += jnp.sum(0.5 * jnp.mean(err * err, axis=-1, keepdims=True))
            dx, dg = vjp(err * (1.0 / D))
            dx_ref[...] = dx
            dg_ref[...] += dg

    return pl.pallas_call(
        body, name=name, grid=(n // R,),
        in_specs=[_row_spec(R, D, 0), pl.BlockSpec((R, D), lambda i: (jnp.maximum(i - nct, 0), 0)), _full_spec((1, D))],
        out_specs=[_full_spec((8, 128)), _row_spec(R, D, 0), _full_spec((1, D))],
        out_shape=[jax.ShapeDtypeStruct((8, 128), F32), jax.ShapeDtypeStruct((n, D), F32), jax.ShapeDtypeStruct((1, D), F32)],
        compiler_params=_ARB1,
    )(x, target, gf)


def _adamw(name, w, m, v, gparts):
    rows, cols = w.shape
    nparts = gparts.shape[0]
    tr = rows
    for cand in range(min(rows, 256), 15, -16):
        if rows % cand == 0:
            tr = cand
            break

    def body(w_ref, m_ref, v_ref, g_ref, go_ref, d_ref, mo_ref, vo_ref):
        g = g_ref[0].astype(F32)
        for k in range(1, nparts):
            g = g + g_ref[k].astype(F32)
        m_new = ADAM_B1 * m_ref[...] + (1.0 - ADAM_B1) * g
        v_new = ADAM_B2 * v_ref[...] + (1.0 - ADAM_B2) * (g * g)
        m_hat = m_new / (1.0 - ADAM_B1 ** ADAM_STEP)
        v_hat = v_new / (1.0 - ADAM_B2 ** ADAM_STEP)
        go_ref[...] = g
        d_ref[...] = -ADAM_LR * (m_hat / (jnp.sqrt(v_hat) + ADAM_EPS) + ADAM_WD * w_ref[...])
        mo_ref[...] = m_new
        vo_ref[...] = v_new

    spec = pl.BlockSpec((tr, cols), lambda i: (i, 0))
    return pl.pallas_call(
        body, name=name, grid=(rows // tr,),
        in_specs=[spec, spec, spec, pl.BlockSpec((nparts, tr, cols), lambda i: (0, i, 0))],
        out_specs=[spec] * 4, out_shape=[jax.ShapeDtypeStruct((rows, cols), F32)] * 4,
        compiler_params=_ARB1,
    )(w, m, v, gparts)


def _my_index():
    return 4 * lax.axis_index("x") + 2 * lax.axis_index("y") + lax.axis_index("c")


def _xor_peer(k):
    flip = lambda a, bit: (1 - a) if bit else a
    pos = (flip(lax.axis_index("x"), (k >> 2) & 1), flip(lax.axis_index("y"), (k >> 1) & 1), flip(lax.axis_index("c"), k & 1))
    return pos, 4 * pos[0] + 2 * pos[1] + pos[2]


def _exchange_plan(x_refs, o_refs, send_sems, recv_sems, local_sems, gather):
    npeer = N_DEV - 1
    me = _my_index()
    locals_, sends, recvs = [], [], []
    for a, (x_ref, o_ref) in enumerate(zip(x_refs, o_refs)):
        mine = x_ref if gather else x_ref.at[me]
        locals_.append(pltpu.make_async_copy(mine, o_ref.at[me], local_sems.at[a]))
        for k in range(1, N_DEV):
            pos, lin = _xor_peer(k)
            src = x_ref if gather else x_ref.at[lin]
            sem = a * npeer + k - 1
            sends.append(pltpu.make_async_remote_copy(src_ref=src, dst_ref=o_ref.at[me], send_sem=send_sems.at[sem],
                                                      recv_sem=recv_sems.at[sem], device_id=pos, device_id_type=MESH_ID))
            recvs.append(pltpu.make_async_remote_copy(src_ref=src, dst_ref=o_ref.at[lin], send_sem=send_sems.at[sem],
                                                      recv_sem=recv_sems.at[sem], device_id=pos, device_id_type=MESH_ID))

    def start():
        for cp in locals_ + sends:
            cp.start()

    def wait():
        for cp in recvs:
            cp.wait_recv()
        for cp in sends:
            cp.wait_send()
        for cp in locals_:
            cp.wait()

    return start, wait


def _exchange_io(xs, gather):
    na = len(xs)
    hbm = pl.BlockSpec(memory_space=pltpu.HBM)
    shapes = [jax.ShapeDtypeStruct((N_DEV,) + tuple(x.shape if gather else x.shape[1:]), x.dtype) for x in xs]
    sems = [pltpu.SemaphoreType.DMA((na * (N_DEV - 1),)), pltpu.SemaphoreType.DMA((na * (N_DEV - 1),)),
            pltpu.SemaphoreType.DMA((na,))]
    return [hbm] * na, shapes, sems


def _exchange(name, xs, *, gather):
    na = len(xs)
    specs, shapes, sems = _exchange_io(xs, gather)

    def body(*refs):
        start, wait = _exchange_plan(refs[:na], refs[na:2 * na], *refs[2 * na:], gather)
        start()
        wait()

    return list(pl.pallas_call(body, name=name, in_specs=specs, out_specs=specs, out_shape=shapes, scratch_shapes=sems)(*xs))


def _adaln_fwd(name, craw16, w_ada, b_cols):
    def body(c_ref, w_ref, b_ref, o_ref):
        cs = jax.nn.silu(c_ref[...]).astype(BF16)
        for l in range(2):
            o_ref[l] = lax.dot_general(cs, w_ref[l].astype(BF16), _NN, preferred_element_type=F32) + b_ref[l]

    return pl.pallas_call(
        body, name=name, out_shape=jax.ShapeDtypeStruct((2, 16, w_ada.shape[2]), F32),
        compiler_params=pltpu.CompilerParams(vmem_limit_bytes=V7X_VMEM_LIMIT),
    )(craw16, w_ada, b_cols)


def _adaln_bwd(name, craw16, w_ada, dm):
    def body(c_ref, w_ref, dm_ref, gw_ref, dc_ref):
        c = c_ref[...]
        sg = jax.nn.sigmoid(c)
        cs = c * sg
        row = lax.broadcasted_iota(jnp.int32, (8, 1), 0)
        dc = jnp.zeros((16, D), F32)
        for l in range(2):
            dmx = dm_ref[2 * l + 1]
            dmc = jnp.where(row == 0, jnp.sum(dm_ref[2 * l], axis=0, keepdims=True), 0.0)
            gw_ref[l] = _hdot(cs[0:8], dmx, _TN) + _hdot(cs[8:16], dmc, _TN)
            dc = dc + _hdot(jnp.concatenate([dmx, dmc], axis=0), w_ref[l], _NT)
        dc_ref[...] = dc * sg * (1.0 + c * (1.0 - sg))

    return pl.pallas_call(
        body, name=name,
        out_shape=[jax.ShapeDtypeStruct(w_ada.shape, F32), jax.ShapeDtypeStruct((16, D), F32)],
        compiler_params=pltpu.CompilerParams(vmem_limit_bytes=V7X_VMEM_LIMIT),
    )(craw16, w_ada, dm)


_IN_OFFS = [sum(IN_SPLITS[:k]) for k in range(len(IN_SPLITS) + 1)]
_MY_ORDER = (11, 12, 13, 0, 1, 2, 7, 10, 5, 6, 3, 4, 8, 9)


_IN_SHARD = IN_WIDTH // N_DEV


def _win_my_cols(pieces):
    parts = []
    for k in _MY_ORDER:
        a, b = _IN_OFFS[k], _IN_OFFS[k + 1]
        for s in range(a // _IN_SHARD, (b - 1) // _IN_SHARD + 1):
            lo, hi = max(a, s * _IN_SHARD), min(b, (s + 1) * _IN_SHARD)
            parts.append(pieces[s][:, lo - s * _IN_SHARD:hi - s * _IN_SHARD])
    parts.append(jnp.zeros((pieces[0].shape[0], PW - IN_WIDTH), pieces[0].dtype))
    return jnp.concatenate(parts, axis=1)


def _win_shards(wp):
    my_offs, pos = {}, 0
    for k in _MY_ORDER:
        my_offs[k] = pos
        pos += IN_SPLITS[k]
    shards = []
    for s in range(N_DEV):
        parts = []
        for k in range(len(IN_SPLITS)):
            lo, hi = max(_IN_OFFS[k], s * _IN_SHARD), min(_IN_OFFS[k + 1], (s + 1) * _IN_SHARD)
            if lo < hi:
                parts.append(wp[:, my_offs[k] + lo - _IN_OFFS[k]:my_offs[k] + hi - _IN_OFFS[k]])
        shards.append(jnp.concatenate(parts, axis=1))
    return jnp.stack(shards)


def _split_shards(full, axis):
    c = full.shape[axis] // N_DEV
    return jnp.stack([lax.slice_in_dim(full, s * c, (s + 1) * c, axis=axis) for s in range(N_DEV)])


def _pack_rows(pieces, row_mult):
    rows = jnp.concatenate([p.reshape(-1, 128) for p in pieces], axis=0)
    padn = (-rows.shape[0]) % row_mult
    if padn:
        rows = jnp.concatenate([rows, jnp.zeros((padn, 128), rows.dtype)], axis=0)
    return rows


def _unpack_rows(rows, shapes):
    out, pos = [], 0
    for s in shapes:
        size = 1
        for d in s:
            size *= d
        out.append(rows[pos:pos + size // 128].reshape(tuple(s)))
        pos += size // 128
    return out


def _heads_front(a, nh):
    return a.reshape(a.shape[0], nh, HEAD_DIM).transpose(1, 0, 2)


def _heads_back(a):
    return a.transpose(1, 0, 2).reshape(a.shape[1], a.shape[0] * HEAD_DIM)


def _rope_tables(t, tc):
    tok = jnp.arange(t, dtype=jnp.int32)
    inv_freq = ROPE_THETA ** (-jnp.arange(ROPE_FREQS, dtype=F32) / ROPE_FREQS)
    ang_r = (tok // GRID_W).astype(F32)[:, None] * inv_freq
    ang_c = (tok % GRID_W).astype(F32)[:, None] * inv_freq
    cos64 = jnp.concatenate([jnp.cos(ang_r), jnp.cos(ang_r), jnp.cos(ang_c), jnp.cos(ang_c)], axis=1)
    sin64 = jnp.concatenate([-jnp.sin(ang_r), jnp.sin(ang_r), -jnp.sin(ang_c), jnp.sin(ang_c)], axis=1)
    cos64 = jnp.concatenate([jnp.ones((tc, HEAD_DIM), F32), cos64], axis=0)
    sin64 = jnp.concatenate([jnp.zeros((tc, HEAD_DIM), F32), sin64], axis=0)
    return jnp.tile(cos64, (1, N_Q_HEADS)), jnp.tile(sin64, (1, N_Q_HEADS))


def _head_mean_matrix(width):
    i = jnp.arange(width) // HEAD_DIM
    return (i[:, None] == i[None, :]).astype(F32) / HEAD_DIM


def _head_tile_matrix(width):
    return (jnp.arange(HEAD_DIM)[:, None] == (jnp.arange(width) % HEAD_DIM)[None, :]).astype(F32)


def _heads_t(a, nh):
    return a.T.reshape(nh, HEAD_DIM, a.shape[0])


def _heads_t_back(a):
    return a.reshape(a.shape[0] * HEAD_DIM, a.shape[2]).T


def _attention_fwd(tag, qr, kr, vv, tc, comm):
    qt8, k2, vt2 = _heads_t(qr, N_Q_HEADS), _heads_front(kr, N_KV_HEADS), _heads_t(vv, N_KV_HEADS)
    vt2 = jnp.concatenate([vt2, jnp.ones((N_KV_HEADS, 8, vt2.shape[2]), BF16)], axis=1)
    o_c, lse_c, _ = _attn_fwd(tag + "_attn_ctx", qt8[:, :, :tc], k2[:, :tc], vt2[:, :, :tc])
    o_x, lse_x, comm_out = _attn_fwd(tag + "_attn_lat", qt8[:, :, tc:], k2, vt2, comm)
    ot8 = jnp.concatenate([o_c, o_x], axis=2)
    lse = jnp.concatenate([lse_c, lse_x], axis=2)
    return _heads_t_back(ot8), (qr, kr, vv, ot8, lse), comm_out


def _attention_bwd(tag, saved, datt, tc, comm):
    qr, kr, vv, ot8, lse = saved
    datt = datt.astype(BF16)
    qt8, dot8 = _heads_t(qr, N_Q_HEADS), _heads_t(datt, N_Q_HEADS)
    k2, v2, kt2 = _heads_front(kr, N_KV_HEADS), _heads_front(vv, N_KV_HEADS), _heads_t(kr, N_KV_HEADS)
    dq_c, dk_c, dv_c, _ = _attn_bwd(tag + "_attn_b_ctx", qt8[:, :, :tc], k2[:, :tc], v2[:, :tc], kt2[:, :, :tc],
                                    ot8[:, :, :tc], dot8[:, :, :tc], lse[:, :, :tc])
    dq_x, dk_x, dv_x, comm_out = _attn_bwd(tag + "_attn_b_lat", qt8[:, :, tc:], k2, v2, kt2,
                                           ot8[:, :, tc:], dot8[:, :, tc:], lse[:, :, tc:], comm)
    dqt8 = jnp.concatenate([dq_c, dq_x], axis=2)
    dkt2 = dk_x.at[:, :, :tc].add(dk_c)
    dvt2 = dv_x.at[:, :, :tc].add(dv_c)
    return _heads_t_back(dqt8), _heads_t_back(dkt2), _heads_t_back(dvt2), comm_out


def _layer_fwd(tag, x, w, modv, consts, R, nct, tc, comm, late_weights):
    sh1, sc1, g1, sh2, sc2, g2 = modv
    cosq, sinq, bdq, eq, bdk, ek = consts
    n = x.shape[0]
    (h1,) = _rowwise(tag + "_ln1", _f_lnmod, R, nct, [(x, D, 0, D)], [], [(w["norm1_g"], False)], [], [sh1, sc1], [(D, BF16)])
    p = _matmul(tag + "_in", h1, w["w_in"], "nn", F32)
    qk_rows = [(p, Q_W, C_Q // Q_W, Q_W), (p, KV_W, C_K // KV_W, KV_W)]
    qk_consts = [(cosq, Q_W, 0, Q_W), (sinq, Q_W, 0, Q_W), (cosq, KV_W, 0, KV_W), (sinq, KV_W, 0, KV_W)]
    qk_params = [(w["q_norm_g"], False), (w["k_norm_g"], False)]
    qk_cparams = [(bdq, False), (eq, False), (bdk, False), (ek, False)]
    qr, kr = _rowwise(tag + "_qk", _f_qknorm, R, nct, qk_rows, qk_consts, qk_params, qk_cparams, [],
                      [(Q_W, BF16), (KV_W, BF16)], post=_qk_post)
    vv = p[:, C_VV:C_VV + KV_W].astype(BF16)
    att, att_saved, comm_out = _attention_fwd(tag, qr, kr, vv, tc, comm)
    w.update(late_weights(comm_out))

    o_f, s_f = _gla_dir(tag + "_gla_f", p, w["w2p_f"], w["b2_f"], rev=False, R=R, nct=nct)
    o_fb, s_b = _gla_dir(tag + "_gla_b", p, w["w2p_b"], w["b2_b"], rev=True, R=R, nct=nct, add=o_f)
    go_rows = [(o_fb, GLA_V_W, 0, GLA_DV), (p, GLA_V_W, C_R // GLA_V_W, GLA_DV)]
    (gla,) = _rowwise(tag + "_glaout", _f_glaout, R, nct, go_rows, [], [(w["gla_norm_g"], True)], [], [], [(GLA_V_W, BF16)])

    rg = GMLP_CHUNK
    gm_rows = [(p, GMLP_W, C_U // GMLP_W, GMLP_W), (p, GMLP_W, C_V // GMLP_W, GMLP_W // GMLP_GROUPS)]
    gm_params = [(w["gmlp_norm_g"], True), (w["w_spatial"], True), (w["b_spatial_t"], True)]
    (gm,) = _rowwise(tag + "_gmlp", _f_gmlp, rg, tc // rg, gm_rows, [], gm_params, [], [], [(GMLP_W, BF16)])

    ya = _matmul(tag + "_br_a", gm, w["w_br_a"], "nn", F32)
    yb = _matmul(tag + "_br_b", att, w["w_br_b"], "nn", F32)
    yc = _matmul(tag + "_br_c", gla, w["w_br_c"], "nn", F32)
    mg_rows = [(p, D, C_GA // D, D), (p, D, C_GB // D, D), (p, D, C_GC // D, D), (ya, D, 0, D), (yb, D, 0, D), (yc, D, 0, D)]
    (merged,) = _rowwise(tag + "_merge", _f_merge, R, nct, mg_rows, [], [], [], [], [(D, BF16)])
    mix = _matmul(tag + "_out", merged, w["w_out"], "nn", F32)
    (x_mid,) = _rowwise(tag + "_res1", _f_resid, R, nct, [(x, D, 0, D), (mix, D, 0, D)], [], [], [], [g1], [(D, F32)])

    (h2,) = _rowwise(tag + "_ln2", _f_lnmod, R, nct, [(x_mid, D, 0, D)], [], [(w["norm2_g"], False)], [], [sh2, sc2], [(D, BF16)])
    a2 = _matmul(tag + "_up", h2, w["w_ffn_up"], "nn", F32, o_halves=True)
    gv = _conv_gate(tag + "_conv", a2, w["conv_w_h"], w["conv_b_h"], R, tc)
    ffn = _matmul(tag + "_down", gv, w["w_ffn_down"], "nn", F32)
    (x_next,) = _rowwise(tag + "_res2", _f_resid, R, nct, [(x_mid, D, 0, D), (ffn, D, 0, D)], [], [], [], [g2], [(D, F32)])
    saved = dict(x=x, h1=h1, p=p, att_saved=att_saved, att=att, o_fb=o_fb, s_f=s_f, s_b=s_b, gla=gla, gm=gm,
                 ya=ya, yb=yb, yc=yc, merged=merged, mix=mix, x_mid=x_mid, h2=h2, a2=a2, gv=gv, ffn=ffn,
                 qk=(qk_rows, qk_consts, qk_params, qk_cparams), go_rows=go_rows, gm_info=(gm_rows, gm_params),
                 mg_rows=mg_rows)
    return x_next, saved, comm_out


def _layer_bwd(tag, dx_next, s, w, modv, R, nct, tc, make_comm, make_tail_comm):
    sh1, sc1, g1, sh2, sc2, g2 = modv
    gw = {}
    (dffn,), _, (dg2,) = _rowwise_bwd(tag + "_res2_b", _f_resid, R, nct, [(s["ffn"], D, 0, D), (s["ffn"], D, 0, D)], [], [], [], [g2],
                                      [(dx_next, D)], [None, BF16])
    dgv = _matmul(tag + "_down_da", dffn, w["w_ffn_down"], "nt", F32)
    gw["w_ffn_down"] = _matmul(tag + "_down_dw", s["gv"], dffn, "tn", F32)
    da2, dcw, dcb = _conv_gate_bwd(tag + "_conv_b", s["a2"], dgv, w["conv_w_h"], w["conv_b_h"], R, tc)
    gw["conv_w_h"], gw["conv_b_h"] = dcw, dcb
    dh2 = _matmul(tag + "_up_da", da2, w["w_ffn_up"], "nt", F32, a_halves=True)
    gw["w_ffn_up"] = _matmul(tag + "_up_dw", s["h2"], da2, "tn", F32, b_halves=True)
    (dx_mid,), (gw["norm2_g"],), (dsh2, dsc2) = _rowwise_bwd(
        tag + "_ln2_b", _f_lnmod, R, nct, [(s["x_mid"], D, 0, D)], [], [(w["norm2_g"], False)], [], [sh2, sc2],
        [(dh2, D)], [F32], adds=[dx_next])
    (dmix,), _, (dg1,) = _rowwise_bwd(tag + "_res1_b", _f_resid, R, nct, [(s["mix"], D, 0, D), (s["mix"], D, 0, D)], [], [], [], [g1],
                                      [(dx_mid, D)], [None, BF16])
    dmerged = _matmul(tag + "_out_da", dmix, w["w_out"], "nt", F32)
    gw["w_out"] = _matmul(tag + "_out_dw", s["merged"], dmix, "tn", F32)
    (dga, dgb, dgc, dya, dyb, dyc), _, _ = _rowwise_bwd(tag + "_merge_b", _f_merge, R, nct, s["mg_rows"], [], [], [], [],
                                                        [(dmerged, D)], [BF16] * 6)
    dgm = _matmul(tag + "_br_a_da", dya, w["w_br_a"], "nt", F32)
    datt = _matmul(tag + "_br_b_da", dyb, w["w_br_b"], "nt", F32)
    dgla = _matmul(tag + "_br_c_da", dyc, w["w_br_c"], "nt", F32)
    gm_rows, gm_params = s["gm_info"]
    gw["w_br_a"] = _matmul(tag + "_br_a_dw", s["gm"], dya, "tn", F32)
    gw["w_br_b"] = _matmul(tag + "_br_b_dw", s["att"], dyb, "tn", F32)
    gw["w_br_c"] = _matmul(tag + "_br_c_dw", s["gla"], dyc, "tn", F32)
    rg = GMLP_CHUNK
    (du, dv_), (gw["gmlp_norm_g"], gw["w_spatial"], gw["b_spatial_t"]), _ = _rowwise_bwd(
        tag + "_gmlp_b", _f_gmlp, rg, tc // rg, gm_rows, [], gm_params, [], [], [(dgm, GMLP_W)], [BF16, BF16])
    (do, dr), (gw["gla_norm_g"],), _ = _rowwise_bwd(tag + "_glaout_b", _f_glaout, R, nct, s["go_rows"], [],
                                                    [(w["gla_norm_g"], True)], [], [], [(dgla, GLA_V_W)], [F32, BF16])
    p = s["p"]
    *d_b, gw["w2p_b"], gw["b2_b"] = _gla_dir_bwd(tag + "_gla_b_b", p, w["w2p_b"], w["b2_b"], s["s_b"], do, rev=True, R=R, nct=nct)
    dgq, dgk, dgv_, daf, gw["w2p_f"], gw["b2_f"] = _gla_dir_bwd(tag + "_gla_f_b", p, w["w2p_f"], w["b2_f"], s["s_f"], do,
                                                              rev=False, R=R, nct=nct, adds=d_b, out_dtype=BF16)
    dqr, dkr, dvv, comm_out = _attention_bwd(tag, s["att_saved"], datt, tc, make_comm(gw))
    qk_rows, qk_consts, qk_params, qk_cparams = s["qk"]
    (dq, dk), (gw["q_norm_g"], gw["k_norm_g"]), _ = _rowwise_bwd(
        tag + "_qk_b", _f_qknorm, R, nct, qk_rows, qk_consts, qk_params, qk_cparams, [],
        [(dqr, Q_W), (dkr, KV_W)], [BF16, BF16], pre=_qk_pre)
    dp = jnp.concatenate([dga, dgb, dgc, du, dv_, dq, dgv_, dr, dgq, dgk, dk, dvv.astype(BF16), daf,
                          jnp.zeros((p.shape[0], PW - C_AF - 128), BF16)], axis=1)
    gw["w_in"] = _matmul(tag + "_in_dw", s["h1"], dp, "tn", F32)
    tail = make_tail_comm(gw)
    if tail is None:
        dh1, tail_out = _matmul(tag + "_in_da", dp, w["w_in"], "nt", F32), []
    else:
        dh1, tail_out = _matmul(tag + "_in_da", dp, w["w_in"], "nt", F32, comm=tail)
    (dx,), (gw["norm1_g"],), (dsh1, dsc1) = _rowwise_bwd(
        tag + "_ln1_b", _f_lnmod, R, nct, [(s["x"], D, 0, D)], [], [(w["norm1_g"], False)], [], [sh1, sc1],
        [(dh1, D)], [F32], adds=[dx_mid])
    return dx, gw, (dsh1, dsc1, dg1, dsh2, dsc2, dg2), comm_out, tail_out


_SHARDED = (("w_in", 1, True), ("w_br_a", 1, True), ("w_br_b", 1, True), ("w_br_c", 1, True), ("w_out", 0, True),
            ("w_ffn_up", 1, True), ("w_ffn_down", 0, True), ("conv_w", 1, False), ("w_alpha2", 2, False), ("b_alpha", 1, False))
_REPLICATED = ("c_ctx", "b_ada", "norm1_g", "norm2_g", "q_norm_g", "k_norm_g", "gmlp_norm_g", "w_spatial", "b_spatial",
               "gla_norm_g", "conv_b", "final_norm_g")
_WEIGHTS = ("c_ctx", "w_ada", "b_ada", "norm1_g", "norm2_g", "w_in", "q_norm_g", "k_norm_g", "gmlp_norm_g", "w_spatial",
            "b_spatial", "w_alpha2", "b_alpha", "gla_norm_g", "w_br_a", "w_br_b", "w_br_c", "w_out", "w_ffn_up", "conv_w",
            "conv_b", "w_ffn_down", "final_norm_g")


def _decay_weights(w_alpha2_l, b_alpha_l):
    out = []
    for d in range(2):
        w2p = jnp.zeros((128, GLA_QK_W), F32).at[GLA_RANK * d:GLA_RANK * (d + 1)].set(w_alpha2_l[d])
        out += [w2p, b_alpha_l[d][None, :]]
    return out


def _step(inp, wts, moms, vels):
    x, c, ctx, loss_target = inp
    t, tc = x.shape[1], ctx.shape[1]
    n = t + tc
    R = min(256, tc)
    nct = tc // R
    me = _my_index()
    depth = wts["w_in"].shape[0]

    late = [(nm, ax) for nm, ax, half in _SHARDED if half and nm != "w_in"]
    small = [(nm, ax) for nm, ax, half in _SHARDED if not half]
    w_in_shard = lambda l: wts["w_in"][l].astype(BF16)
    c8 = jnp.concatenate([c, jnp.zeros((7, D), F32)], axis=0)
    first = _exchange("gather_first", [w_in_shard(0)] + [wts[nm] for nm, _ in small] + [c8], gather=True)
    c_all = first[-1][:, 0, :]
    small_all = dict(zip([nm for nm, _ in small], first[1:-1]))

    def early_weights(l, w_in_all):
        w = {"w_in": _win_my_cols([w_in_all[s] for s in range(N_DEV)])}
        conv_w, w_alpha2, b_alpha = [jnp.concatenate([small_all[nm][s, l] for s in range(N_DEV)], axis=ax) for nm, ax in small]
        w["conv_w_h"] = conv_w.reshape(3, 2, FFN_H).transpose(1, 0, 2)
        w["conv_b_h"] = wts["conv_b"][l].reshape(2, 1, FFN_H)
        w["w2p_f"], w["b2_f"], w["w2p_b"], w["b2_b"] = _decay_weights(w_alpha2, b_alpha)
        w["norm1_g"] = wts["norm1_g"][l][None, :]
        w["norm2_g"] = wts["norm2_g"][l][None, :]
        w["q_norm_g"] = wts["q_norm_g"][l][None, :]
        w["k_norm_g"] = wts["k_norm_g"][l][None, :]
        w["gmlp_norm_g"] = wts["gmlp_norm_g"][l].reshape(GMLP_GROUPS, 1, GMLP_W // GMLP_GROUPS)
        w["w_spatial"] = wts["w_spatial"][l]
        w["b_spatial_t"] = wts["b_spatial"][l][:, :, None]
        w["gla_norm_g"] = wts["gla_norm_g"][l].reshape(GLA_HEADS, 1, GLA_DV)
        return w

    craw16 =jnp.concatenate([c_all, wts["c_ctx"][None, :], jnp.zeros((7, D), F32)], axis=0)
    acols = wts["w_ada"].shape[2]
    b_cols = lax.dynamic_slice_in_dim(wts["b_ada"], me * acols, acols, axis=1)[:, None, :]
    mod_part = _adaln_fwd("adaln", craw16, wts["w_ada"], b_cols)
    send = jnp.stack([mod_part[:, 8, :][None].repeat(N_DEV, 0), mod_part[:, :8, :].transpose(1, 0, 2)], axis=2)
    send = jnp.concatenate([send.reshape(N_DEV, 2 * depth, acols), jnp.zeros((N_DEV, 8 - 2 * depth, acols), F32)], axis=1)
    (got,) = _exchange("scatter_mod", [send], gather=False)
    mod = got[:, :2 * depth, :].transpose(1, 0, 2).reshape(depth, 2, N_MOD, 1, D)
    modv = [[mod[l, :, k] for k in range(N_MOD)] for l in range(depth)]

    cosq, sinq = _rope_tables(t, tc)
    consts = (cosq, sinq, _head_mean_matrix(Q_W), _head_tile_matrix(Q_W), _head_mean_matrix(KV_W), _head_tile_matrix(KV_W))
    xs = jnp.concatenate([ctx[0], x[0]], axis=0)
    saved, layers = [], []
    w_in_all = first[0]

    def late_weights(got):
        return {nm: jnp.concatenate([g[s] for s in range(N_DEV)], axis=ax) for (nm, ax), g in zip(late, got)}

    for l in range(depth):
        layers.append(early_weights(l, w_in_all))
        sending = [wts[nm][l].astype(BF16) for nm, _ in late] + ([w_in_shard(l + 1)] if l + 1 < depth else [])
        xs, sv, got = _layer_fwd("l%d" % l, xs, layers[l], modv[l], consts, R, nct, tc, (sending, True), late_weights)
        if l + 1 < depth:
            w_in_all = got[len(late)]
        saved.append(sv)
    loss_blk, dxs, dgf = _final_loss("final", xs, loss_target[0], wts["final_norm_g"][None, :], R, nct)
    loss = lax.psum(loss_blk[0, 0], ("x", "y", "c"))

    grads = [None] * depth
    dmods = [None] * depth
    late_parts = [None] * depth
    w_in_parts = [None] * depth
    w_in_grad_shards = lambda g: _win_shards(g["w_in"]).astype(BF16)

    def small_grad_shards(g):
        full = dict(conv_w=g["conv_w_h"].transpose(1, 0, 2).reshape(3, F2),
                    w_alpha2=jnp.stack([g["w2p_f"][:GLA_RANK], g["w2p_b"][GLA_RANK:2 * GLA_RANK]]),
                    b_alpha=jnp.stack([g["b2_f"][0], g["b2_b"][0]]))
        return [_split_shards(full[nm], ax) for nm, ax in small]

    for l in range(depth - 1, -1, -1):
        def make_comm(gw, l=l):
            sending = [_split_shards(gw[nm], ax).astype(BF16) for nm, ax in late]
            return sending + ([w_in_grad_shards(grads[l + 1])] if l + 1 < depth else []), False

        def make_tail_comm(gw, l=l):
            if l > 0:
                return None
            per_layer = [small_grad_shards(gw if k == 0 else grads[k]) for k in range(depth)]
            return [w_in_grad_shards(gw)] + [jnp.stack([per_layer[k][i] for k in range(depth)], axis=1) for i in range(len(small))], False

        dxs, grads[l], dmods[l], got, tail = _layer_bwd("l%d" % l, dxs, saved[l], layers[l], modv[l], R, nct, tc,
                                                        make_comm, make_tail_comm)
        late_parts[l] = got[:len(late)]
        if l + 1 < depth:
            w_in_parts[l + 1] = got[len(late)]
    w_in_parts[0], small_parts = tail[0], tail[1:]
    grad_x = dxs[tc:][None]

    dmod = jnp.stack([jnp.stack(dmods[l], axis=1) for l in range(depth)])
    dmod = dmod.reshape(depth, 2, N_DEV, acols).transpose(2, 0, 1, 3).reshape(N_DEV, 2 * depth, acols)
    dmod_send = jnp.concatenate([dmod, jnp.zeros((N_DEV, 8 - 2 * depth, acols), F32)], axis=1)
    (dm_got,) = _exchange("scatter_dmod", [dmod_send], gather=False)
    g_w_ada, dc16 = _adaln_bwd("adaln_b", craw16, wts["w_ada"], dm_got[:, :2 * depth].transpose(1, 0, 2))
    db_ada_part = jnp.stack([jnp.stack(dmods[l], axis=1) for l in range(depth)]).reshape(depth, 2, N_MOD * D).sum(axis=1)

    out = {}
    kinds = ("grad", "delta", "new_m", "new_v")
    view2 = lambda a: a.reshape(-1, a.shape[-1])
    sharded_parts = ([jnp.stack(w_in_parts, axis=1)]
                     + [jnp.stack([late_parts[l][k] for l in range(depth)], axis=1) for k in range(len(late))] + small_parts)
    for (nm, _), parts in zip([("w_in", 1)] + late + small, sharded_parts):
        res = _adamw("adamw_" + nm, view2(wts[nm]), view2(moms[nm]), view2(vels[nm]), parts.reshape(N_DEV, -1, parts.shape[-1]))
        for kind, flat in zip(kinds, res):
            out[kind, nm] = flat.reshape(wts[nm].shape)

    rep_g = dict(
        c_ctx=dc16[8], b_ada=db_ada_part, final_norm_g=dgf[0],
        norm1_g=jnp.stack([grads[l]["norm1_g"][0] for l in range(depth)]),
        norm2_g=jnp.stack([grads[l]["norm2_g"][0] for l in range(depth)]),
        q_norm_g=jnp.stack([grads[l]["q_norm_g"][0] for l in range(depth)]),
        k_norm_g=jnp.stack([grads[l]["k_norm_g"][0] for l in range(depth)]),
        gmlp_norm_g=jnp.stack([grads[l]["gmlp_norm_g"].reshape(GMLP_W) for l in range(depth)]),
        w_spatial=jnp.stack([grads[l]["w_spatial"] for l in range(depth)]),
        b_spatial=jnp.stack([grads[l]["b_spatial_t"][:, :, 0] for l in range(depth)]),
        gla_norm_g=jnp.stack([grads[l]["gla_norm_g"].reshape(GLA_V_W) for l in range(depth)]),
        conv_b=jnp.stack([grads[l]["conv_b_h"].reshape(F2) for l in range(depth)]),
    )
    rep_shapes = [wts[nm].shape for nm in _REPLICATED]
    (rg_parts,) = _exchange("gather_rep_grads", [_pack_rows([rep_g[nm] for nm in _REPLICATED], 16)], gather=True)
    rpk = lambda src: _pack_rows([src[nm] for nm in _REPLICATED], 16)
    res = _adamw("adamw_rep", rpk(wts), rpk(moms), rpk(vels), rg_parts)
    for kind, rows in zip(kinds, res):
        for nm, piece in zip(_REPLICATED, _unpack_rows(rows, rep_shapes)):
            out[kind, nm] = piece

    res = _adamw("adamw_ada", view2(wts["w_ada"]), view2(moms["w_ada"]), view2(vels["w_ada"]), view2(g_w_ada)[None])
    for kind, flat in zip(kinds, res):
        out[kind, "w_ada"] = flat.reshape(wts["w_ada"].shape)

    return (loss, grad_x, *[out[kind, nm] for kind in kinds for nm in _WEIGHTS])


def kernel(x, c, ctx, c_ctx, w_ada, b_ada, norm1_g, norm2_g, w_in, q_norm_g, k_norm_g, gmlp_norm_g, w_spatial, b_spatial, w_alpha2, b_alpha, gla_norm_g, w_br_a, w_br_b, w_br_c, w_out, w_ffn_up, conv_w, conv_b, w_ffn_down, final_norm_g, loss_target, m_c_ctx, m_w_ada, m_b_ada, m_norm1_g, m_norm2_g, m_w_in, m_q_norm_g, m_k_norm_g, m_gmlp_norm_g, m_w_spatial, m_b_spatial, m_w_alpha2, m_b_alpha, m_gla_norm_g, m_w_br_a, m_w_br_b, m_w_br_c, m_w_out, m_w_ffn_up, m_conv_w, m_conv_b, m_w_ffn_down, m_final_norm_g, v_c_ctx, v_w_ada, v_b_ada, v_norm1_g, v_norm2_g, v_w_in, v_q_norm_g, v_k_norm_g, v_gmlp_norm_g, v_w_spatial, v_b_spatial, v_w_alpha2, v_b_alpha, v_gla_norm_g, v_w_br_a, v_w_br_b, v_w_br_c, v_w_out, v_w_ffn_up, v_conv_w, v_conv_b, v_w_ffn_down, v_final_norm_g):
    wts = dict(zip(_WEIGHTS, (c_ctx, w_ada, b_ada, norm1_g, norm2_g, w_in, q_norm_g, k_norm_g, gmlp_norm_g, w_spatial, b_spatial,
                              w_alpha2, b_alpha, gla_norm_g, w_br_a, w_br_b, w_br_c, w_out, w_ffn_up, conv_w, conv_b, w_ffn_down,
                              final_norm_g)))
    moms = dict(zip(_WEIGHTS, (m_c_ctx, m_w_ada, m_b_ada, m_norm1_g, m_norm2_g, m_w_in, m_q_norm_g, m_k_norm_g, m_gmlp_norm_g,
                               m_w_spatial, m_b_spatial, m_w_alpha2, m_b_alpha, m_gla_norm_g, m_w_br_a, m_w_br_b, m_w_br_c, m_w_out,
                               m_w_ffn_up, m_conv_w, m_conv_b, m_w_ffn_down, m_final_norm_g)))
    vels = dict(zip(_WEIGHTS, (v_c_ctx, v_w_ada, v_b_ada, v_norm1_g, v_norm2_g, v_w_in, v_q_norm_g, v_k_norm_g, v_gmlp_norm_g,
                               v_w_spatial, v_b_spatial, v_w_alpha2, v_b_alpha, v_gla_norm_g, v_w_br_a, v_w_br_b, v_w_br_c, v_w_out,
                               v_w_ffn_up, v_conv_w, v_conv_b, v_w_ffn_down, v_final_norm_g)))
    return _step((x, c, ctx, loss_target), wts, moms, vels)
```

```python
import functools

import jax
import jax.numpy as jnp
from jax import lax
from jax.experimental import pallas as pl
from jax.experimental.pallas import tpu as pltpu

F32 = jnp.float32
BF16 = jnp.bfloat16
HI = lax.Precision.HIGHEST
MESH_ID = pl.DeviceIdType.MESH

N_DEV = 8
EPS = 1e-6
D = 1024
N_MOD = 6
HEAD_DIM = 64
N_Q_HEADS = 8
N_KV_HEADS = 2
Q_GROUP = 4
Q_W = 512
KV_W = 128
GRID_W = 64
ROPE_THETA = 10000.0
ROPE_FREQS = 16
GMLP_CHUNK = 128
GMLP_GROUPS = 4
GMLP_W = 512
GLA_HEADS = 4
GLA_QK_W = 256
GLA_V_W = 512
GLA_DK = 64
GLA_DV = 128
GLA_RANK = 16
GLA_TAU = 16.0
GLA_CHUNK = 64
FFN_H = 2816
F2 = 2 * FFN_H
IN_SPLITS = (512, 512, 512, 128, 128, 256, 256, 512, 16, 16, 512, 1024, 1024, 1024)
IN_WIDTH = sum(IN_SPLITS)

C_GA, C_GB, C_GC = 0, 1024, 2048
C_U, C_V, C_Q, C_GV, C_R = 3072, 3584, 4096, 4608, 5120
C_GQ, C_GK = 5632, 5888
C_K, C_VV, C_AF = 6144, 6272, 6400
PW = 6656

ADAM_LR = 0.001
ADAM_B1 = 0.9
ADAM_B2 = 0.999
ADAM_EPS = 1e-08
ADAM_WD = 0.01
ADAM_STEP = 10

V7X_VMEM_LIMIT = 56 * 1024 * 1024

_ARB1 = pltpu.CompilerParams(dimension_semantics=("arbitrary",), vmem_limit_bytes=V7X_VMEM_LIMIT)


def _pick(dim, prefs):
    for p in prefs:
        if dim % p == 0:
            return p
    return dim


def _hdot(a, b, dims=(((1,), (0,)), ((), ()))):
    return lax.dot_general(a, b, dims, precision=HI, preferred_element_type=F32)


_NT = (((1,), (1,)), ((), ()))
_TN = (((0,), (0,)), ((), ()))
_NN = (((1,), (0,)), ((), ()))


def _matmul(name, a, b, mode, out_dtype, *, a_halves=False, b_halves=False, o_halves=False, comm=None):
    def dims2(x, halves):
        return (x.shape[1], 2 * x.shape[2]) if halves else x.shape

    ar, ac = dims2(a, a_halves)
    br, bc = dims2(b, b_halves)
    if mode == "nn":
        M, K, N = ar, ac, bc
    elif mode == "nt":
        M, K, N = ar, ac, br
    else:
        M, K, N = ac, ar, bc
    row_prefs = (768, 512, 384, 256, 128)
    n_unit = N // 2 if (o_halves or (b_halves and mode != "nt")) else N
    k_unit = K // 2 if (a_halves and mode != "tn") else K
    if mode == "tn":
        tm = _pick(M, (1024, 1408, 512, 256, 128))
        tk = _pick(K, (1408,) + row_prefs)
    else:
        tm = _pick(M, row_prefs)
        tk = _pick(k_unit, (1664, 1408, 1024, 512, 256, 128))
    tn = _pick(n_unit, (1664, 1408, 1024, 512, 256, 128))
    nk = K // tk

    def spec(shape2, halves, blk, imap):
        if not halves:
            return pl.BlockSpec(blk, imap)
        nhalf = (shape2[1] // 2) // blk[1]

        def im(i, j, k):
            r, c = imap(i, j, k)
            return (c // nhalf, r, c % nhalf)
        return pl.BlockSpec((None,) + blk, im)

    if mode == "nn":
        a_spec = spec((ar, ac), a_halves, (tm, tk), lambda i, j, k: (i, k))
        b_spec = spec((br, bc), b_halves, (tk, tn), lambda i, j, k: (k, j))
        dn = _NN
    elif mode == "nt":
        a_spec = spec((ar, ac), a_halves, (tm, tk), lambda i, j, k: (i, k))
        b_spec = spec((br, bc), b_halves, (tn, tk), lambda i, j, k: (j, k))
        dn = _NT
    else:
        a_spec = spec((ar, ac), a_halves, (tk, tm), lambda i, j, k: (k, i))
        b_spec = spec((br, bc), b_halves, (tk, tn), lambda i, j, k: (k, j))
        dn = _TN
    o_spec = spec((M, N), o_halves, (tm, tn), lambda i, j, k: (i, j))
    o_shape = (2, M, N // 2) if o_halves else (M, N)

    def body(a_ref, b_ref, o_ref, acc_ref):
        k = pl.program_id(2)
        part = lax.dot_general(a_ref[...], b_ref[...], dn, preferred_element_type=F32)
        if nk == 1:
            o_ref[...] = part.astype(o_ref.dtype)
        else:
            @pl.when(k == 0)
            def _():
                acc_ref[...] = part

            @pl.when(k > 0)
            def _():
                acc_ref[...] += part

            @pl.when(k == nk - 1)
            def _():
                o_ref[...] = acc_ref[...].astype(o_ref.dtype)

    grid = (M // tm, N // tn, nk)
    body, xspecs, xshapes, xsems = _with_exchange(body, 2, 1, grid, comm)
    res = pl.pallas_call(
        body, name=name, grid=grid,
        in_specs=[a_spec, b_spec] + xspecs, out_specs=[o_spec] + xspecs,
        out_shape=[jax.ShapeDtypeStruct(o_shape, out_dtype)] + xshapes,
        scratch_shapes=[pltpu.VMEM((tm, tn), F32)] + xsems,
        compiler_params=pltpu.CompilerParams(dimension_semantics=("arbitrary", "arbitrary", "arbitrary"),
                                             vmem_limit_bytes=V7X_VMEM_LIMIT),
    )(a, b, *(comm[0] if comm else []))
    return res[0] if comm is None else (res[0], list(res[1:]))


def _full_spec(shape):
    nd = len(shape)
    return pl.BlockSpec(tuple(shape), lambda i, _nd=nd: (0,) * _nd)


def _row_spec(R, W, cb):
    return pl.BlockSpec((R, W), lambda i, _cb=cb: (i, _cb))


def _load_rows(refs, specs):
    vals = []
    for ref, (_, W, _, pw) in zip(refs, specs):
        if pw == W:
            vals.append(ref[...].astype(F32))
        else:
            vals.append([ref[:, k * pw:(k + 1) * pw].astype(F32) for k in range(W // pw)])
    return vals


def _load_params(refs, specs):
    vals = []
    for ref, (arr, split) in zip(refs, specs):
        if split:
            vals.append([ref[k] for k in range(arr.shape[0])])
        else:
            vals.append(ref[...])
    return vals


def _mod_spec(nct, width):
    return pl.BlockSpec((None, 1, width), lambda i: (jnp.minimum(i // nct, 1), 0, 0))


def _rowwise(name, f, R, nct, rows, consts, params, cparams, mods, outs, post=None):
    n = rows[0][0].shape[0]
    nr, nc, npar, ncp, nm = len(rows), len(consts), len(params), len(cparams), len(mods)

    def body(*refs):
        pos = 0
        rr = refs[pos:pos + nr]; pos += nr
        cr = refs[pos:pos + nc]; pos += nc
        pr = refs[pos:pos + npar]; pos += npar
        cpr = refs[pos:pos + ncp]; pos += ncp
        mr = refs[pos:pos + nm]; pos += nm
        orefs = refs[pos:]
        res = f(_load_rows(rr, rows), _load_params(pr, params), [m[...] for m in mr],
                _load_rows(cr, consts), _load_params(cpr, cparams))
        if post is not None:
            res = post(res, _load_rows(cr, consts))
        for o_ref, r in zip(orefs, res):
            o_ref[...] = r.astype(o_ref.dtype)

    in_specs = ([_row_spec(R, W, cb) for (_, W, cb, _) in rows + consts]
                + [_full_spec(a.shape) for (a, _) in params + cparams]
                + [_mod_spec(nct, m.shape[2]) for m in mods])
    args = [a for (a, _, _, _) in rows + consts] + [a for (a, _) in params + cparams] + list(mods)
    return pl.pallas_call(
        body, name=name, grid=(n // R,), in_specs=in_specs,
        out_specs=[_row_spec(R, w, 0) for (w, _) in outs],
        out_shape=[jax.ShapeDtypeStruct((n, w), dt) for (w, dt) in outs],
        compiler_params=_ARB1,
    )(*args)


def _rowwise_bwd(name, f, R, nct, rows, consts, params, cparams, mods, douts, drow, adds=None, pre=None):
    n = rows[0][0].shape[0]
    adds = adds or [None] * len(rows)
    nr, nc, npar, ncp, nm, nd = len(rows), len(consts), len(params), len(cparams), len(mods), len(douts)
    add_ix = [k for k in range(nr) if adds[k] is not None]
    out_ix = [k for k in range(nr) if drow[k] is not None]

    def body(*refs):
        i = pl.program_id(0)
        pos = 0
        rr = refs[pos:pos + nr]; pos += nr
        cr = refs[pos:pos + nc]; pos += nc
        pr = refs[pos:pos + npar]; pos += npar
        cpr = refs[pos:pos + ncp]; pos += ncp
        mr = refs[pos:pos + nm]; pos += nm
        dr = refs[pos:pos + nd]; pos += nd
        ar = refs[pos:pos + len(add_ix)]; pos += len(add_ix)
        drr = refs[pos:pos + len(out_ix)]; pos += len(out_ix)
        dpr = refs[pos:pos + npar]; pos += npar
        dmr = refs[pos:pos + nm]; pos += nm

        cv = _load_rows(cr, consts)
        cpv = _load_params(cpr, cparams)
        _, vjp = jax.vjp(lambda rv, pv, mv: f(rv, pv, mv, cv, cpv),
                         _load_rows(rr, rows), _load_params(pr, params), [m[...] for m in mr])
        dv = [d[...].astype(F32) for d in dr]
        if pre is not None:
            dv = pre(dv, cv)
        g_rows, g_params, g_mods = vjp(tuple(dv))

        for ref, k in zip(drr, out_ix):
            _, W, _, pw = rows[k]
            g = g_rows[k]
            extra = ar[add_ix.index(k)] if k in add_ix else None
            if pw == W:
                if extra is not None:
                    g = g + extra[...].astype(F32)
                ref[...] = g.astype(ref.dtype)
            else:
                for q in range(W // pw):
                    gq = g[q]
                    if extra is not None:
                        gq = gq + extra[:, q * pw:(q + 1) * pw].astype(F32)
                    ref[:, q * pw:(q + 1) * pw] = gq.astype(ref.dtype)

        @pl.when(i == 0)
        def _():
            for ref in dpr:
                ref[...] = jnp.zeros_like(ref)

        for ref, (arr, split), g in zip(dpr, params, g_params):
            if split:
                for k in range(arr.shape[0]):
                    ref[k] += g[k]
            else:
                ref[...] += g

        @pl.when((i == 0) | (i == nct))
        def _():
            for ref in dmr:
                ref[...] = jnp.zeros_like(ref)

        for ref, g in zip(dmr, g_mods):
            ref[...] += g

    in_specs = ([_row_spec(R, W, cb) for (_, W, cb, _) in rows + consts]
                + [_full_spec(a.shape) for (a, _) in params + cparams]
                + [_mod_spec(nct, m.shape[2]) for m in mods]
                + [_row_spec(R, W, 0) for (_, W) in douts]
                + [_row_spec(R, rows[k][1], 0) for k in add_ix])
    args = ([a for (a, _, _, _) in rows + consts] + [a for (a, _) in params + cparams] + list(mods)
            + [a for (a, _) in douts] + [adds[k] for k in add_ix])
    out_specs = ([_row_spec(R, rows[k][1], 0) for k in out_ix]
                 + [_full_spec(a.shape) for (a, _) in params]
                 + [_mod_spec(nct, m.shape[2]) for m in mods])
    out_shape = ([jax.ShapeDtypeStruct((n, rows[k][1]), drow[k]) for k in out_ix]
                 + [jax.ShapeDtypeStruct(a.shape, F32) for (a, _) in params]
                 + [jax.ShapeDtypeStruct(m.shape, F32) for m in mods])
    res = pl.pallas_call(
        body, name=name, grid=(n // R,), in_specs=in_specs, out_specs=out_specs, out_shape=out_shape,
        compiler_params=_ARB1,
    )(*args)
    no = len(out_ix)
    return list(res[:no]), list(res[no:no + npar]), list(res[no + npar:])


def _rms(x, g):
    return x * lax.rsqrt(jnp.mean(x * x, axis=-1, keepdims=True) + EPS) * g


def _f_lnmod(rv, pv, mv, cv, cpv):
    (x,), (g,), (shift, scale) = rv, pv, mv
    return (_rms(x, g) * (1.0 + scale) + shift,)


def _f_resid(rv, pv, mv, cv, cpv):
    (x, y), (gate,) = rv, mv
    return (x + gate * y,)


def _f_merge(rv, pv, mv, cv, cpv):
    ga, gb, gc, ya, yb, yc = rv
    return (_sigmoid(ga) * ya + _sigmoid(gb) * yb + _sigmoid(gc) * yc,)


def _split3(x):
    hi = x.astype(BF16)
    rest = x - hi.astype(F32)
    mid = rest.astype(BF16)
    return hi, mid, (rest - mid.astype(F32)).astype(BF16)


def _dot3_right(x, m):
    mb = m.astype(BF16)
    return sum(lax.dot_general(piece, mb, _NN, preferred_element_type=F32) for piece in _split3(x))


@jax.custom_vjp
def _sym_dot(x, m):
    return _dot3_right(x, m)


_sym_dot.defvjp(lambda x, m: (_dot3_right(x, m), m), lambda m, g: (_dot3_right(g, m), jnp.zeros_like(m)))


def _f_qknorm(rv, pv, mv, cv, cpv):
    (q, k), (gq, gk), (bdq, eq, bdk, ek) = rv, pv, cpv
    qn = q * lax.rsqrt(_sym_dot(q * q, bdq) + EPS) * _hdot(gq, eq)
    kn = k * lax.rsqrt(_sym_dot(k * k, bdk) + EPS) * _hdot(gk, ek)
    return (qn, kn)


def _rope(x, cos, sin):
    w = x.shape[1]
    lane = lax.broadcasted_iota(jnp.int32, x.shape, 1)
    partner = jnp.where((lane & 31) < 16, pltpu.roll(x, w - 16, 1), pltpu.roll(x, 16, 1))
    return x * cos + partner * sin


_LOG2E = 1.4426950408889634
_LN2 = 0.6931471805599453


def _qk_post(res, cv):
    (qn, kn), (cq, sq, ck, sk) = res, cv
    return (_rope(qn, cq, sq) * (HEAD_DIM ** -0.5 * _LOG2E), _rope(kn, ck, sk))


def _qk_pre(dv, cv):
    (dq, dk), (cq, sq, ck, sk) = dv, cv
    return (_rope(dq * (HEAD_DIM ** -0.5), cq, -sq), _rope(dk * _LN2, ck, -sk))


def _f_gmlp(rv, pv, mv, cv, cpv):
    (u, vs), (ng, ws, bt) = rv, pv
    pieces = []
    for g in range(GMLP_GROUPS):
        vn = _rms(jax.nn.gelu(vs[g]), ng[g])
        pieces.append(_bdot(ws[g], vn, "nn") + bt[g])
    return (jax.nn.gelu(u) * jnp.concatenate(pieces, axis=1),)


def _f_glaout(rv, pv, mv, cv, cpv):
    (os_, rs), (gn,) = rv, pv
    pieces = [_rms(os_[h], gn[h]) * (rs[h] * _sigmoid(rs[h])) for h in range(GLA_HEADS)]
    return (jnp.concatenate(pieces, axis=1),)


_CONV_CB = 1408
_CONV_STRIP = 128


def _sigmoid(x):
    return 0.5 * jnp.tanh(0.5 * x) + 0.5


def _halo_keep(i, R, tc, n):
    first, end = i * R, (i + 1) * R
    keep_prev = jnp.where((first == 0) | (first == tc), 0.0, 1.0)
    keep_next = jnp.where((end == tc) | (end == n), 0.0, 1.0)
    return keep_prev, keep_next


def _conv_specs(R, n):
    nb8 = n // 8
    main = pl.BlockSpec((2, R, _CONV_CB), lambda j, i: (0, i, j))
    prev = pl.BlockSpec((2, 8, _CONV_CB), lambda j, i: (0, jnp.maximum(i * (R // 8) - 1, 0), j))
    nxt = pl.BlockSpec((2, 8, _CONV_CB), lambda j, i: (0, jnp.minimum((i + 1) * (R // 8), nb8 - 1), j))
    cw = pl.BlockSpec((2, 3, _CONV_CB), lambda j, i: (0, 0, j))
    cb = pl.BlockSpec((2, 1, _CONV_CB), lambda j, i: (0, 0, j))
    return main, prev, nxt, cw, cb


def _row_before(x):
    return pltpu.roll(x, 1, 0)


def _row_after(x):
    return pltpu.roll(x, x.shape[0] - 1, 0)


def _conv_gate(name, a2, cw, cb, R, tc):
    n = a2.shape[1]
    main, prev, nxt, cws, cbs = _conv_specs(R, n)

    def body(a_ref, p_ref, n_ref, cw_ref, cb_ref, o_ref):
        keep_prev, keep_next = _halo_keep(pl.program_id(1), R, tc, n)

        def strip(c, carry):
            ls = pl.ds(pl.multiple_of(c * _CONV_STRIP, _CONV_STRIP), _CONV_STRIP)
            acts = []
            for h in range(2):
                win = jnp.concatenate([p_ref[h, :, ls] * keep_prev, a_ref[h, :, ls], n_ref[h, :, ls] * keep_next], axis=0)
                acts.append((cb_ref[h, :, ls] + cw_ref[h, 1:2, ls] * win + cw_ref[h, 0:1, ls] * _row_before(win)
                             + cw_ref[h, 2:3, ls] * _row_after(win))[8:8 + R])
            g, v = acts
            o_ref[:, ls] = (g * _sigmoid(g) * v).astype(o_ref.dtype)
            return carry

        lax.fori_loop(0, _CONV_CB // _CONV_STRIP, strip, 0)

    return pl.pallas_call(
        body, name=name, grid=(FFN_H // _CONV_CB, n // R),
        in_specs=[main, prev, nxt, cws, cbs],
        out_specs=pl.BlockSpec((R, _CONV_CB), lambda j, i: (i, j)),
        out_shape=jax.ShapeDtypeStruct((n, FFN_H), BF16),
        compiler_params=pltpu.CompilerParams(dimension_semantics=("arbitrary", "arbitrary"),
                                             vmem_limit_bytes=V7X_VMEM_LIMIT),
    )(a2, a2, a2, cw, cb)


def _conv_gate_bwd(name, a2, dgv, cw, cb, R, tc):
    n = a2.shape[1]
    nb8 = n // 8
    main, prev, nxt, cws, cbs = _conv_specs(R, n)
    d_main = pl.BlockSpec((R, _CONV_CB), lambda j, i: (i, j))
    d_prev = pl.BlockSpec((8, _CONV_CB), lambda j, i: (jnp.maximum(i * (R // 8) - 1, 0), j))
    d_next = pl.BlockSpec((8, _CONV_CB), lambda j, i: (jnp.minimum((i + 1) * (R // 8), nb8 - 1), j))
    mid = slice(8, 8 + R)

    def body(a_ref, p_ref, n_ref, cw_ref, cb_ref, d_ref, dp_ref, dn_ref, da_ref, dcw_ref, dcb_ref):
        i = pl.program_id(1)
        keep_prev, keep_next = _halo_keep(i, R, tc, n)

        @pl.when(i == 0)
        def _():
            dcw_ref[...] = jnp.zeros_like(dcw_ref)
            dcb_ref[...] = jnp.zeros_like(dcb_ref)

        def strip(c, carry):
            ls = pl.ds(pl.multiple_of(c * _CONV_STRIP, _CONV_STRIP), _CONV_STRIP)
            cws_ = [[cw_ref[h, k:k + 1, ls] for k in range(3)] for h in range(2)]
            wins = [jnp.concatenate([p_ref[h, :, ls] * keep_prev, a_ref[h, :, ls], n_ref[h, :, ls] * keep_next], axis=0)
                    for h in range(2)]
            g, v = [cb_ref[h, :, ls] + cws_[h][1] * wins[h] + cws_[h][0] * _row_before(wins[h]) + cws_[h][2] * _row_after(wins[h])
                    for h in range(2)]
            dout = jnp.concatenate([dp_ref[:, ls].astype(F32) * keep_prev, d_ref[:, ls].astype(F32),
                                    dn_ref[:, ls].astype(F32) * keep_next], axis=0)
            sg = _sigmoid(g)
            das = [dout * v * sg * (1.0 + g * (1.0 - sg)), dout * g * sg]
            for h in range(2):
                da_win = das[h]
                da_ref[h, :, ls] = (cws_[h][1] * da_win + cws_[h][0] * _row_after(da_win)
                                    + cws_[h][2] * _row_before(da_win))[mid].astype(da_ref.dtype)
                da = da_win[mid]
                dcw_ref[h, 0:1, ls] += jnp.sum(da * _row_before(wins[h])[mid], axis=0, keepdims=True)
                dcw_ref[h, 1:2, ls] += jnp.sum(da * wins[h][mid], axis=0, keepdims=True)
                dcw_ref[h, 2:3, ls] += jnp.sum(da * _row_after(wins[h])[mid], axis=0, keepdims=True)
                dcb_ref[h, :, ls] += jnp.sum(da, axis=0, keepdims=True)
            return carry

        lax.fori_loop(0, _CONV_CB // _CONV_STRIP, strip, 0)

    return pl.pallas_call(
        body, name=name, grid=(FFN_H // _CONV_CB, n // R),
        in_specs=[main, prev, nxt, cws, cbs, d_main, d_prev, d_next],
        out_specs=[main, cws, cbs],
        out_shape=[jax.ShapeDtypeStruct((2, n, FFN_H), BF16), jax.ShapeDtypeStruct((2, 3, FFN_H), F32),
                   jax.ShapeDtypeStruct((2, 1, FFN_H), F32)],
        compiler_params=pltpu.CompilerParams(dimension_semantics=("arbitrary", "arbitrary"),
                                             vmem_limit_bytes=V7X_VMEM_LIMIT),
    )(a2, a2, a2, cw, cb, dgv, dgv, dgv)


_ARB2 = pltpu.CompilerParams(dimension_semantics=("arbitrary", "arbitrary"), vmem_limit_bytes=V7X_VMEM_LIMIT)


def _with_exchange(body, n_in, n_out, grid, comm):
    if comm is None:
        return body, [], [], []
    xs, gather = comm
    na = len(xs)
    specs, shapes, sems = _exchange_io(xs, gather)

    def wrapped(*refs):
        ins, x_refs = refs[:n_in], refs[n_in:n_in + na]
        outs, o_refs = refs[n_in + na:n_in + na + n_out], refs[n_in + na + n_out:n_in + 2 * na + n_out]
        scratch, sem_refs = refs[n_in + 2 * na + n_out:-3], refs[-3:]
        start, wait = _exchange_plan(x_refs, o_refs, *sem_refs, gather)
        ids = [pl.program_id(d) for d in range(len(grid))]
        first, last = ids[0] == 0, ids[0] == grid[0] - 1
        for d in range(1, len(grid)):
            first, last = first & (ids[d] == 0), last & (ids[d] == grid[d] - 1)

        @pl.when(first)
        def _():
            start()

        body(*ins, *outs, *scratch)

        @pl.when(last)
        def _():
            wait()

    return wrapped, specs, shapes, sems


def _attn_fwd(name, qt8, k2, vt2, comm=None):
    nq, nk = qt8.shape[2], k2.shape[1]
    tq = _pick(nq, (256, 128))
    tk = _pick(nk, (1408, 768, 256, 128))

    va = vt2.shape[1]

    def body(qt_ref, k_ref, vt_ref, ot_ref, lse_ref, m_ref, acc_ref):
        m_ref[...] = jnp.full((Q_GROUP, 1, tq), -1e30, F32)
        acc_ref[...] = jnp.zeros((Q_GROUP, va, tq), F32)

        def step(j, carry):
            sl = pl.ds(pl.multiple_of(j * tk, tk), tk)
            kj = k_ref[sl, :]
            vtj = vt_ref[:, sl]
            sts = [lax.dot_general(kj, qt_ref[h], _NN, preferred_element_type=F32) for h in range(Q_GROUP)]
            m_old = [m_ref[h] for h in range(Q_GROUP)]
            acc_old = [acc_ref[h] for h in range(Q_GROUP)]
            m_new = [jnp.maximum(m_old[h], jnp.max(sts[h], axis=0, keepdims=True)) for h in range(Q_GROUP)]
            pts = [jnp.exp2(sts[h] - m_new[h]).astype(BF16) for h in range(Q_GROUP)]
            pvs = [lax.dot_general(vtj, pts[h], _NN, preferred_element_type=F32) for h in range(Q_GROUP)]
            for h in range(Q_GROUP):
                acc_ref[h] = jnp.exp2(m_old[h] - m_new[h]) * acc_old[h] + pvs[h]
                m_ref[h] = m_new[h]
            return carry

        lax.fori_loop(0, nk // tk, step, 0, unroll=2)
        for h in range(Q_GROUP):
            l = acc_ref[h, HEAD_DIM:HEAD_DIM + 1, :]
            ot_ref[h] = (acc_ref[h, 0:HEAD_DIM, :] / l).astype(ot_ref.dtype)
            lse_ref[h] = m_ref[h] + jnp.log2(l)

    qspec = pl.BlockSpec((Q_GROUP, HEAD_DIM, tq), lambda g, i: (g, 0, i))
    lspec = pl.BlockSpec((Q_GROUP, 1, tq), lambda g, i: (g, 0, i))
    grid = (N_KV_HEADS, nq // tq)
    body, xspecs, xshapes, xsems = _with_exchange(body, 3, 2, grid, comm)
    res = pl.pallas_call(
        body, name=name, grid=grid,
        in_specs=[qspec, pl.BlockSpec((None, nk, HEAD_DIM), lambda g, i: (g, 0, 0)),
                  pl.BlockSpec((None, va, nk), lambda g, i: (g, 0, 0))] + xspecs,
        out_specs=[qspec, lspec] + xspecs,
        out_shape=[jax.ShapeDtypeStruct((N_Q_HEADS, HEAD_DIM, nq), BF16), jax.ShapeDtypeStruct((N_Q_HEADS, 1, nq), F32)] + xshapes,
        scratch_shapes=[pltpu.VMEM((Q_GROUP, 1, tq), F32), pltpu.VMEM((Q_GROUP, va, tq), F32)] + xsems,
        compiler_params=_ARB2,
    )(qt8, k2, vt2, *(comm[0] if comm else []))
    return res[0], res[1], list(res[2:])


def _attn_bwd(name, qt8, k2, v2, kt2, ot8, dot8, lse, comm=None):
    nq, nk = qt8.shape[2], k2.shape[1]
    tq = _pick(nq, (256, 128))
    tk = _pick(nk, (1408, 768, 256, 128))
    ts = tk
    heads = range(Q_GROUP)

    def body(qt_ref, k_ref, v_ref, kt_ref, ot_ref, dot_ref, lse_ref, dqt_ref, dkt_ref, dvt_ref, dl_ref, dq_acc):
        @pl.when(pl.program_id(1) == 0)
        def _():
            dkt_ref[...] = jnp.zeros_like(dkt_ref)
            dvt_ref[...] = jnp.zeros_like(dvt_ref)

        for h in heads:
            dl_ref[h] = jnp.sum(dot_ref[h].astype(F32) * ot_ref[h].astype(F32), axis=0, keepdims=True)
        dq_acc[...] = jnp.zeros((Q_GROUP, HEAD_DIM, tq), F32)

        items = [(s, h) for s in range(tk // ts) for h in heads]

        def step(j, carry):
            def keys(s):
                return pl.ds(pl.multiple_of(j * tk + s * ts, ts), ts)

            def scores(item):
                s, h = item
                return (lax.dot_general(k_ref[keys(s), :], qt_ref[h], _NN, preferred_element_type=F32),
                        lax.dot_general(v_ref[keys(s), :], dot_ref[h], _NN, preferred_element_type=F32))

            nxt = scores(items[0])
            for n, (s, h) in enumerate(items):
                st, dpt = nxt
                if n + 1 < len(items):
                    nxt = scores(items[n + 1])
                pt = jnp.exp2(st - lse_ref[h])
                dst = (pt * (dpt - dl_ref[h])).astype(BF16)
                dq_acc[h] += lax.dot_general(kt_ref[:, keys(s)], dst, _NN, preferred_element_type=F32)
                dv_h = lax.dot_general(dot_ref[h], pt.astype(BF16), _NT, preferred_element_type=F32)
                dk_h = lax.dot_general(qt_ref[h], dst, _NT, preferred_element_type=F32)
                dvt_s, dkt_s = (dv_h, dk_h) if h == 0 else (dvt_s + dv_h, dkt_s + dk_h)
                if h == Q_GROUP - 1:
                    dvt_ref[:, keys(s)] += dvt_s
                    dkt_ref[:, keys(s)] += dkt_s
            return carry

        lax.fori_loop(0, nk // tk, step, 0)
        dqt_ref[...] = dq_acc[...]

    tspec = pl.BlockSpec((Q_GROUP, HEAD_DIM, tq), lambda g, i: (g, 0, i))
    lspec = pl.BlockSpec((Q_GROUP, 1, tq), lambda g, i: (g, 0, i))
    kspec = pl.BlockSpec((None, nk, HEAD_DIM), lambda g, i: (g, 0, 0))
    ktspec = pl.BlockSpec((None, HEAD_DIM, nk), lambda g, i: (g, 0, 0))
    grid = (N_KV_HEADS, nq // tq)
    body, xspecs, xshapes, xsems = _with_exchange(body, 7, 3, grid, comm)
    res = pl.pallas_call(
        body, name=name, grid=grid,
        in_specs=[tspec, kspec, kspec, ktspec, tspec, tspec, lspec] + xspecs,
        out_specs=[tspec, ktspec, ktspec] + xspecs,
        out_shape=[jax.ShapeDtypeStruct((N_Q_HEADS, HEAD_DIM, nq), F32), jax.ShapeDtypeStruct((N_KV_HEADS, HEAD_DIM, nk), F32),
                   jax.ShapeDtypeStruct((N_KV_HEADS, HEAD_DIM, nk), F32)] + xshapes,
        scratch_shapes=[pltpu.VMEM((Q_GROUP, 1, tq), F32), pltpu.VMEM((Q_GROUP, HEAD_DIM, tq), F32)] + xsems,
        compiler_params=_ARB2,
    )(qt8, k2, v2, kt2, ot8, dot8, lse, *(comm[0] if comm else []))
    return res[0], res[1], res[2], list(res[3:])


def _log_sigmoid(z):
    return jnp.minimum(z, 0.0) - jnp.log(1.0 + jnp.exp(-jnp.abs(z)))


_BDOT_DIMS = {"nn": _NN, "nt": _NT, "tn": _TN}
_BDOT_BWD = {"nn": (("nt", "gb"), ("tn", "ag")), "nt": (("nn", "gb"), ("tn", "ga")), "tn": (("nt", "bg"), ("nn", "ag"))}


def _bdot_raw(a, b, mode):
    return lax.dot_general(a.astype(BF16), b.astype(BF16), _BDOT_DIMS[mode], preferred_element_type=F32)


@functools.partial(jax.custom_vjp, nondiff_argnums=(2,))
def _bdot(a, b, mode):
    return _bdot_raw(a, b, mode)


def _bdot_fwd(a, b, mode):
    return _bdot_raw(a, b, mode), (a.astype(BF16), b.astype(BF16))


def _bdot_bwd(mode, res, g):
    ops = {"a": res[0], "b": res[1], "g": g}
    (ma, oa), (mb, ob) = _BDOT_BWD[mode]
    return _bdot_raw(ops[oa[0]], ops[oa[1]], ma), _bdot_raw(ops[ob[0]], ops[ob[1]], mb)


_bdot.defvjp(_bdot_fwd, _bdot_bwd)


def _tile_tri(rev, rows):
    r_i = lax.broadcasted_iota(jnp.int32, (rows, rows), 0)
    c_i = lax.broadcasted_iota(jnp.int32, (rows, rows), 1)
    same = (r_i // GLA_CHUNK) == (c_i // GLA_CHUNK)
    return same & ((c_i >= r_i) if rev else (c_i <= r_i))


def _tri_dot(rev, x):
    tri = _tile_tri(rev, x.shape[0]).astype(BF16)
    return sum(lax.dot_general(tri, piece, _NN, preferred_element_type=F32) for piece in _split3(x))


@functools.partial(jax.custom_vjp, nondiff_argnums=(1,))
def _chunk_cumsum(x, rev):
    return _tri_dot(rev, x)


_chunk_cumsum.defvjp(lambda x, rev: (_tri_dot(rev, x), None), lambda rev, _, g: (_tri_dot(not rev, g),))


def _gla_tile(q, k, vs, a, w2, b2, state_t, *, rev):
    rows = q.shape[0]
    nch = rows // GLA_CHUNK
    tri = _tile_tri(rev, rows)
    chunk_of_row = lax.broadcasted_iota(jnp.int32, (rows, 1), 0) // GLA_CHUNK
    in_chunk = [(chunk_of_row == c).astype(F32) for c in range(nch)]
    la = _log_sigmoid(_bdot(a, w2, "nn") + b2) * (1.0 / GLA_TAU)
    cum = _chunk_cumsum(la, rev)
    tots = [jnp.sum(la * in_chunk[c], axis=0, keepdims=True) for c in range(nch)]
    tot_rows = sum(in_chunk[c] * tots[c] for c in range(nch))
    q_in = q * (GLA_DK ** -0.5) * jnp.exp(cum)
    k_in = k * jnp.exp(-cum)
    k_st = k * jnp.exp(tot_rows - cum)
    lane = lax.broadcasted_iota(jnp.int32, (1, GLA_QK_W), 1)
    outs = []
    for h in range(GLA_HEADS):
        head = ((lane >= GLA_DK * h) & (lane < GLA_DK * (h + 1))).astype(F32)
        att = jnp.where(tri, _bdot(q_in * head, k_in, "nt"), 0.0)
        outs.append(_bdot(att, vs[h], "nn"))
    o = jnp.concatenate(outs, axis=1)
    hr = lax.broadcasted_iota(jnp.int32, (GLA_V_W, GLA_QK_W), 0) // GLA_DV
    hc = lax.broadcasted_iota(jnp.int32, (GLA_V_W, GLA_QK_W), 1) // GLA_DK
    same_head = (hr == hc).astype(F32)
    v_all = jnp.concatenate(vs, axis=1)
    for c in (range(nch - 1, -1, -1) if rev else range(nch)):
        o = o + _bdot(q_in * in_chunk[c], state_t, "nt")
        state_t = jnp.exp(tots[c]) * state_t + _bdot(v_all, k_st * in_chunk[c], "tn") * same_head
    return o, state_t


def _gla_tile_of(step, rev, nct, nt):
    if not rev:
        return step
    return jnp.where(step < nct, nct - 1 - step, nt - 1 - (step - nct))


def _gla_row_specs(R, tile):
    return [pl.BlockSpec((R, GLA_QK_W), lambda s: (tile(s), C_GQ // GLA_QK_W)),
            pl.BlockSpec((R, GLA_QK_W), lambda s: (tile(s), C_GK // GLA_QK_W)),
            pl.BlockSpec((R, GLA_V_W), lambda s: (tile(s), C_GV // GLA_V_W)),
            pl.BlockSpec((R, 128), lambda s: (tile(s), C_AF // 128))]


def _gla_dir(name, p, w2, b2, *, rev, R, nct, add=None):
    n = p.shape[0]
    nt = n // R
    tile = lambda s: _gla_tile_of(s, rev, nct, nt)

    def body(q_ref, k_ref, v_ref, a_ref, w2_ref, b2_ref, *rest):
        if add is not None:
            add_ref, o_ref, ssave_ref, state = rest
        else:
            o_ref, ssave_ref, state = rest

        @pl.when(pl.program_id(0) == 0)
        def _():
            state[...] = jnp.zeros_like(state)

        vs = [v_ref[:, GLA_DV * h:GLA_DV * (h + 1)] for h in range(GLA_HEADS)]
        s_in = state[...]
        ssave_ref[...] = s_in
        o, s_out = _gla_tile(q_ref[...], k_ref[...], vs, a_ref[...], w2_ref[...], b2_ref[...], s_in, rev=rev)
        if add is not None:
            o = o + add_ref[...]
        o_ref[...] = o
        state[...] = s_out

    o_spec = pl.BlockSpec((R, GLA_V_W), lambda s: (tile(s), 0))
    in_specs = _gla_row_specs(R, tile) + [_full_spec(w2.shape), _full_spec(b2.shape)]
    args = [p, p, p, p, w2, b2]
    if add is not None:
        in_specs.append(o_spec)
        args.append(add)
    return pl.pallas_call(
        body, name=name, grid=(nt,), in_specs=in_specs,
        out_specs=[o_spec, pl.BlockSpec((None, GLA_V_W, GLA_QK_W), lambda s: (tile(s), 0, 0))],
        out_shape=[jax.ShapeDtypeStruct((n, GLA_V_W), F32), jax.ShapeDtypeStruct((nt, GLA_V_W, GLA_QK_W), F32)],
        scratch_shapes=[pltpu.VMEM((GLA_V_W, GLA_QK_W), F32)],
        compiler_params=_ARB1,
    )(*args)


def _gla_dir_bwd(name, p, w2, b2, ssave, do, *, rev, R, nct, adds=None, out_dtype=F32):
    n = p.shape[0]
    nt = n // R
    tile = lambda s: _gla_tile_of(nt - 1 - s, rev, nct, nt)
    widths = (GLA_QK_W, GLA_QK_W, GLA_V_W, 128)

    def body(q_ref, k_ref, v_ref, a_ref, w2_ref, b2_ref, ss_ref, do_ref, *rest):
        if adds is not None:
            add_refs, rest = rest[:4], rest[4:]
        dq_ref, dk_ref, dv_ref, da_ref, dw2_ref, db2_ref, dstate = rest

        @pl.when(pl.program_id(0) == 0)
        def _():
            dstate[...] = jnp.zeros_like(dstate)
            dw2_ref[...] = jnp.zeros_like(dw2_ref)
            db2_ref[...] = jnp.zeros_like(db2_ref)

        vs = [v_ref[:, GLA_DV * h:GLA_DV * (h + 1)] for h in range(GLA_HEADS)]
        _, vjp = jax.vjp(functools.partial(_gla_tile, rev=rev), q_ref[...], k_ref[...], vs, a_ref[...],
                         w2_ref[...], b2_ref[...], ss_ref[...])
        dq, dk, dvs, da, dw2, db2, ds = vjp((do_ref[...], dstate[...]))
        grads = [dq, dk, jnp.concatenate(dvs, axis=1), da]
        if adds is not None:
            grads = [g + r[...].astype(F32) for g, r in zip(grads, add_refs)]
        for ref, g in zip((dq_ref, dk_ref, dv_ref, da_ref), grads):
            ref[...] = g.astype(ref.dtype)
        dw2_ref[...] += dw2
        db2_ref[...] += db2
        dstate[...] = ds

    d_specs = [pl.BlockSpec((R, w), lambda s: (tile(s), 0)) for w in widths]
    in_specs = (_gla_row_specs(R, tile) + [_full_spec(w2.shape), _full_spec(b2.shape),
                pl.BlockSpec((None, GLA_V_W, GLA_QK_W), lambda s: (tile(s), 0, 0)),
                pl.BlockSpec((R, GLA_V_W), lambda s: (tile(s), 0))])
    args = [p, p, p, p, w2, b2, ssave, do]
    if adds is not None:
        in_specs += d_specs
        args += list(adds)
    return pl.pallas_call(
        body, name=name, grid=(nt,), in_specs=in_specs,
        out_specs=d_specs + [_full_spec(w2.shape), _full_spec(b2.shape)],
        out_shape=[jax.ShapeDtypeStruct((n, w), out_dtype) for w in widths]
        + [jax.ShapeDtypeStruct(w2.shape, F32), jax.ShapeDtypeStruct(b2.shape, F32)],
        scratch_shapes=[pltpu.VMEM((GLA_V_W, GLA_QK_W), F32)],
        compiler_params=_ARB1,
    )(*args)


def _final_loss(name, x, target, gf, R, nct):
    n = x.shape[0]

    def body(x_ref, t_ref, g_ref, loss_ref, dx_ref, dg_ref):
        i = pl.program_id(0)

        @pl.when(i == 0)
        def _():
            loss_ref[...] = jnp.zeros_like(loss_ref)
            dg_ref[...] = jnp.zeros_like(dg_ref)

        @pl.when(i < nct)
        def _():
            dx_ref[...] = jnp.zeros_like(dx_ref)

        @pl.when(i >= nct)
        def _():
            y, vjp = jax.vjp(_rms, x_ref[...], g_ref[...])
            err = y - t_ref[...]
            loss_ref[...] += jnp.sum(0.5 * jnp.mean(err * err, axis=-1, keepdims=True))
            dx, dg = vjp(err * (1.0 / D))
            dx_ref[...] = dx
            dg_ref[...] += dg

    return pl.pallas_call(
        body, name=name, grid=(n // R,),
        in_specs=[_row_spec(R, D, 0), pl.BlockSpec((R, D), lambda i: (jnp.maximum(i - nct, 0), 0)), _full_spec((1, D))],
        out_specs=[_full_spec((8, 128)), _row_spec(R, D, 0), _full_spec((1, D))],
        out_shape=[jax.ShapeDtypeStruct((8, 128), F32), jax.ShapeDtypeStruct((n, D), F32), jax.ShapeDtypeStruct((1, D), F32)],
        compiler_params=_ARB1,
    )(x, target, gf)


def _adamw(name, w, m, v, gparts):
    rows, cols = w.shape
    nparts = gparts.shape[0]
    tr = rows
    for cand in range(min(rows, 256), 15, -16):
        if rows % cand == 0:
            tr = cand
            break

    def body(w_ref, m_ref, v_ref, g_ref, go_ref, d_ref, mo_ref, vo_ref):
        g = g_ref[0].astype(F32)
        for k in range(1, nparts):
            g = g + g_ref[k].astype(F32)
        m_new = ADAM_B1 * m_ref[...] + (1.0 - ADAM_B1) * g
        v_new = ADAM_B2 * v_ref[...] + (1.0 - ADAM_B2) * (g * g)
        m_hat = m_new / (1.0 - ADAM_B1 ** ADAM_STEP)
        v_hat = v_new / (1.0 - ADAM_B2 ** ADAM_STEP)
        go_ref[...] = g
        d_ref[...] = -ADAM_LR * (m_hat / (jnp.sqrt(v_hat) + ADAM_EPS) + ADAM_WD * w_ref[...])
        mo_ref[...] = m_new
        vo_ref[...] = v_new

    spec = pl.BlockSpec((tr, cols), lambda i: (i, 0))
    return pl.pallas_call(
        body, name=name, grid=(rows // tr,),
        in_specs=[spec, spec, spec, pl.BlockSpec((nparts, tr, cols), lambda i: (0, i, 0))],
        out_specs=[spec] * 4, out_shape=[jax.ShapeDtypeStruct((rows, cols), F32)] * 4,
        compiler_params=_ARB1,
    )(w, m, v, gparts)


def _my_index():
    return 4 * lax.axis_index("x") + 2 * lax.axis_index("y") + lax.axis_index("c")


def _xor_peer(k):
    flip = lambda a, bit: (1 - a) if bit else a
    pos = (flip(lax.axis_index("x"), (k >> 2) & 1), flip(lax.axis_index("y"), (k >> 1) & 1), flip(lax.axis_index("c"), k & 1))
    return pos, 4 * pos[0] + 2 * pos[1] + pos[2]


def _exchange_plan(x_refs, o_refs, send_sems, recv_sems, local_sems, gather):
    npeer = N_DEV - 1
    me = _my_index()
    locals_, sends, recvs = [], [], []
    for a, (x_ref, o_ref) in enumerate(zip(x_refs, o_refs)):
        mine = x_ref if gather else x_ref.at[me]
        locals_.append(pltpu.make_async_copy(mine, o_ref.at[me], local_sems.at[a]))
        for k in range(1, N_DEV):
            pos, lin = _xor_peer(k)
            src = x_ref if gather else x_ref.at[lin]
            sem = a * npeer + k - 1
            sends.append(pltpu.make_async_remote_copy(src_ref=src, dst_ref=o_ref.at[me], send_sem=send_sems.at[sem],
                                                      recv_sem=recv_sems.at[sem], device_id=pos, device_id_type=MESH_ID))
            recvs.append(pltpu.make_async_remote_copy(src_ref=src, dst_ref=o_ref.at[lin], send_sem=send_sems.at[sem],
                                                      recv_sem=recv_sems.at[sem], device_id=pos, device_id_type=MESH_ID))

    def start():
        for cp in locals_ + sends:
            cp.start()

    def wait():
        for cp in recvs:
            cp.wait_recv()
        for cp in sends:
            cp.wait_send()
        for cp in locals_:
            cp.wait()

    return start, wait


def _exchange_io(xs, gather):
    na = len(xs)
    hbm = pl.BlockSpec(memory_space=pltpu.HBM)
    shapes = [jax.ShapeDtypeStruct((N_DEV,) + tuple(x.shape if gather else x.shape[1:]), x.dtype) for x in xs]
    sems = [pltpu.SemaphoreType.DMA((na * (N_DEV - 1),)), pltpu.SemaphoreType.DMA((na * (N_DEV - 1),)),
            pltpu.SemaphoreType.DMA((na,))]
    return [hbm] * na, shapes, sems


def _exchange(name, xs, *, gather):
    na = len(xs)
    specs, shapes, sems = _exchange_io(xs, gather)

    def body(*refs):
        start, wait = _exchange_plan(refs[:na], refs[na:2 * na], *refs[2 * na:], gather)
        start()
        wait()

    return list(pl.pallas_call(body, name=name, in_specs=specs, out_specs=specs, out_shape=shapes, scratch_shapes=sems)(*xs))


def _adaln_fwd(name, craw16, w_ada, b_cols):
    def body(c_ref, w_ref, b_ref, o_ref):
        cs = jax.nn.silu(c_ref[...]).astype(BF16)
        for l in range(2):
            o_ref[l] = lax.dot_general(cs, w_ref[l].astype(BF16), _NN, preferred_element_type=F32) + b_ref[l]

    return pl.pallas_call(
        body, name=name, out_shape=jax.ShapeDtypeStruct((2, 16, w_ada.shape[2]), F32),
        compiler_params=pltpu.CompilerParams(vmem_limit_bytes=V7X_VMEM_LIMIT),
    )(craw16, w_ada, b_cols)


def _adaln_bwd(name, craw16, w_ada, dm):
    def body(c_ref, w_ref, dm_ref, gw_ref, dc_ref):
        c = c_ref[...]
        sg = jax.nn.sigmoid(c)
        cs = c * sg
        row = lax.broadcasted_iota(jnp.int32, (8, 1), 0)
        dc = jnp.zeros((16, D), F32)
        for l in range(2):
            dmx = dm_ref[2 * l + 1]
            dmc = jnp.where(row == 0, jnp.sum(dm_ref[2 * l], axis=0, keepdims=True), 0.0)
            gw_ref[l] = _hdot(cs[0:8], dmx, _TN) + _hdot(cs[8:16], dmc, _TN)
            dc = dc + _hdot(jnp.concatenate([dmx, dmc], axis=0), w_ref[l], _NT)
        dc_ref[...] = dc * sg * (1.0 + c * (1.0 - sg))

    return pl.pallas_call(
        body, name=name,
        out_shape=[jax.ShapeDtypeStruct(w_ada.shape, F32), jax.ShapeDtypeStruct((16, D), F32)],
        compiler_params=pltpu.CompilerParams(vmem_limit_bytes=V7X_VMEM_LIMIT),
    )(craw16, w_ada, dm)


_IN_OFFS = [sum(IN_SPLITS[:k]) for k in range(len(IN_SPLITS) + 1)]
_MY_ORDER = (11, 12, 13, 0, 1, 2, 7, 10, 5, 6, 3, 4, 8, 9)


_IN_SHARD = IN_WIDTH // N_DEV


def _win_my_cols(pieces):
    parts = []
    for k in _MY_ORDER:
        a, b = _IN_OFFS[k], _IN_OFFS[k + 1]
        for s in range(a // _IN_SHARD, (b - 1) // _IN_SHARD + 1):
            lo, hi = max(a, s * _IN_SHARD), min(b, (s + 1) * _IN_SHARD)
            parts.append(pieces[s][:, lo - s * _IN_SHARD:hi - s * _IN_SHARD])
    parts.append(jnp.zeros((pieces[0].shape[0], PW - IN_WIDTH), pieces[0].dtype))
    return jnp.concatenate(parts, axis=1)


def _win_shards(wp):
    my_offs, pos = {}, 0
    for k in _MY_ORDER:
        my_offs[k] = pos
        pos += IN_SPLITS[k]
    shards = []
    for s in range(N_DEV):
        parts = []
        for k in range(len(IN_SPLITS)):
            lo, hi = max(_IN_OFFS[k], s * _IN_SHARD), min(_IN_OFFS[k + 1], (s + 1) * _IN_SHARD)
            if lo < hi:
                parts.append(wp[:, my_offs[k] + lo - _IN_OFFS[k]:my_offs[k] + hi - _IN_OFFS[k]])
        shards.append(jnp.concatenate(parts, axis=1))
    return jnp.stack(shards)


def _split_shards(full, axis):
    c = full.shape[axis] // N_DEV
    return jnp.stack([lax.slice_in_dim(full, s * c, (s + 1) * c, axis=axis) for s in range(N_DEV)])


def _pack_rows(pieces, row_mult):
    rows = jnp.concatenate([p.reshape(-1, 128) for p in pieces], axis=0)
    padn = (-rows.shape[0]) % row_mult
    if padn:
        rows = jnp.concatenate([rows, jnp.zeros((padn, 128), rows.dtype)], axis=0)
    return rows


def _unpack_rows(rows, shapes):
    out, pos = [], 0
    for s in shapes:
        size = 1
        for d in s:
            size *= d
        out.append(rows[pos:pos + size // 128].reshape(tuple(s)))
        pos += size // 128
    return out


def _heads_front(a, nh):
    return a.reshape(a.shape[0], nh, HEAD_DIM).transpose(1, 0, 2)


def _heads_back(a):
    return a.transpose(1, 0, 2).reshape(a.shape[1], a.shape[0] * HEAD_DIM)


def _rope_tables(t, tc):
    tok = jnp.arange(t, dtype=jnp.int32)
    inv_freq = ROPE_THETA ** (-jnp.arange(ROPE_FREQS, dtype=F32) / ROPE_FREQS)
    ang_r = (tok // GRID_W).astype(F32)[:, None] * inv_freq
    ang_c = (tok % GRID_W).astype(F32)[:, None] * inv_freq
    cos64 = jnp.concatenate([jnp.cos(ang_r), jnp.cos(ang_r), jnp.cos(ang_c), jnp.cos(ang_c)], axis=1)
    sin64 = jnp.concatenate([-jnp.sin(ang_r), jnp.sin(ang_r), -jnp.sin(ang_c), jnp.sin(ang_c)], axis=1)
    cos64 = jnp.concatenate([jnp.ones((tc, HEAD_DIM), F32), cos64], axis=0)
    sin64 = jnp.concatenate([jnp.zeros((tc, HEAD_DIM), F32), sin64], axis=0)
    return jnp.tile(cos64, (1, N_Q_HEADS)), jnp.tile(sin64, (1, N_Q_HEADS))


def _head_mean_matrix(width):
    i = jnp.arange(width) // HEAD_DIM
    return (i[:, None] == i[None, :]).astype(F32) / HEAD_DIM


def _head_tile_matrix(width):
    return (jnp.arange(HEAD_DIM)[:, None] == (jnp.arange(width) % HEAD_DIM)[None, :]).astype(F32)


def _heads_t(a, nh):
    return a.T.reshape(nh, HEAD_DIM, a.shape[0])


def _heads_t_back(a):
    return a.reshape(a.shape[0] * HEAD_DIM, a.shape[2]).T


def _attention_fwd(tag, qr, kr, vv, tc, comm):
    qt8, k2, vt2 = _heads_t(qr, N_Q_HEADS), _heads_front(kr, N_KV_HEADS), _heads_t(vv, N_KV_HEADS)
    vt2 = jnp.concatenate([vt2, jnp.ones((N_KV_HEADS, 8, vt2.shape[2]), BF16)], axis=1)
    o_c, lse_c, _ = _attn_fwd(tag + "_attn_ctx", qt8[:, :, :tc], k2[:, :tc], vt2[:, :, :tc])
    o_x, lse_x, comm_out = _attn_fwd(tag + "_attn_lat", qt8[:, :, tc:], k2, vt2, comm)
    ot8 = jnp.concatenate([o_c, o_x], axis=2)
    lse = jnp.concatenate([lse_c, lse_x], axis=2)
    return _heads_t_back(ot8), (qr, kr, vv, ot8, lse), comm_out


def _attention_bwd(tag, saved, datt, tc, comm):
    qr, kr, vv, ot8, lse = saved
    datt = datt.astype(BF16)
    qt8, dot8 = _heads_t(qr, N_Q_HEADS), _heads_t(datt, N_Q_HEADS)
    k2, v2, kt2 = _heads_front(kr, N_KV_HEADS), _heads_front(vv, N_KV_HEADS), _heads_t(kr, N_KV_HEADS)
    dq_c, dk_c, dv_c, _ = _attn_bwd(tag + "_attn_b_ctx", qt8[:, :, :tc], k2[:, :tc], v2[:, :tc], kt2[:, :, :tc],
                                    ot8[:, :, :tc], dot8[:, :, :tc], lse[:, :, :tc])
    dq_x, dk_x, dv_x, comm_out = _attn_bwd(tag + "_attn_b_lat", qt8[:, :, tc:], k2, v2, kt2,
                                           ot8[:, :, tc:], dot8[:, :, tc:], lse[:, :, tc:], comm)
    dqt8 = jnp.concatenate([dq_c, dq_x], axis=2)
    dkt2 = dk_x.at[:, :, :tc].add(dk_c)
    dvt2 = dv_x.at[:, :, :tc].add(dv_c)
    return _heads_t_back(dqt8), _heads_t_back(dkt2), _heads_t_back(dvt2), comm_out


def _layer_fwd(tag, x, w, modv, consts, R, nct, tc, comm, late_weights):
    sh1, sc1, g1, sh2, sc2, g2 = modv
    cosq, sinq, bdq, eq, bdk, ek = consts
    n = x.shape[0]
    (h1,) = _rowwise(tag + "_ln1", _f_lnmod, R, nct, [(x, D, 0, D)], [], [(w["norm1_g"], False)], [], [sh1, sc1], [(D, BF16)])
    p = _matmul(tag + "_in", h1, w["w_in"], "nn", F32)
    qk_rows = [(p, Q_W, C_Q // Q_W, Q_W), (p, KV_W, C_K // KV_W, KV_W)]
    qk_consts = [(cosq, Q_W, 0, Q_W), (sinq, Q_W, 0, Q_W), (cosq, KV_W, 0, KV_W), (sinq, KV_W, 0, KV_W)]
    qk_params = [(w["q_norm_g"], False), (w["k_norm_g"], False)]
    qk_cparams = [(bdq, False), (eq, False), (bdk, False), (ek, False)]
    qr, kr = _rowwise(tag + "_qk", _f_qknorm, R, nct, qk_rows, qk_consts, qk_params, qk_cparams, [],
                      [(Q_W, BF16), (KV_W, BF16)], post=_qk_post)
    vv = p[:, C_VV:C_VV + KV_W].astype(BF16)
    att, att_saved, comm_out = _attention_fwd(tag, qr, kr, vv, tc, comm)
    w.update(late_weights(comm_out))

    o_f, s_f = _gla_dir(tag + "_gla_f", p, w["w2p_f"], w["b2_f"], rev=False, R=R, nct=nct)
    o_fb, s_b = _gla_dir(tag + "_gla_b", p, w["w2p_b"], w["b2_b"], rev=True, R=R, nct=nct, add=o_f)
    go_rows = [(o_fb, GLA_V_W, 0, GLA_DV), (p, GLA_V_W, C_R // GLA_V_W, GLA_DV)]
    (gla,) = _rowwise(tag + "_glaout", _f_glaout, R, nct, go_rows, [], [(w["gla_norm_g"], True)], [], [], [(GLA_V_W, BF16)])

    rg = GMLP_CHUNK
    gm_rows = [(p, GMLP_W, C_U // GMLP_W, GMLP_W), (p, GMLP_W, C_V // GMLP_W, GMLP_W // GMLP_GROUPS)]
    gm_params = [(w["gmlp_norm_g"], True), (w["w_spatial"], True), (w["b_spatial_t"], True)]
    (gm,) = _rowwise(tag + "_gmlp", _f_gmlp, rg, tc // rg, gm_rows, [], gm_params, [], [], [(GMLP_W, BF16)])

    ya = _matmul(tag + "_br_a", gm, w["w_br_a"], "nn", F32)
    yb = _matmul(tag + "_br_b", att, w["w_br_b"], "nn", F32)
    yc = _matmul(tag + "_br_c", gla, w["w_br_c"], "nn", F32)
    mg_rows = [(p, D, C_GA // D, D), (p, D, C_GB // D, D), (p, D, C_GC // D, D), (ya, D, 0, D), (yb, D, 0, D), (yc, D, 0, D)]
    (merged,) = _rowwise(tag + "_merge", _f_merge, R, nct, mg_rows, [], [], [], [], [(D, BF16)])
    mix = _matmul(tag + "_out", merged, w["w_out"], "nn", F32)
    (x_mid,) = _rowwise(tag + "_res1", _f_resid, R, nct, [(x, D, 0, D), (mix, D, 0, D)], [], [], [], [g1], [(D, F32)])

    (h2,) = _rowwise(tag + "_ln2", _f_lnmod, R, nct, [(x_mid, D, 0, D)], [], [(w["norm2_g"], False)], [], [sh2, sc2], [(D, BF16)])
    a2 = _matmul(tag + "_up", h2, w["w_ffn_up"], "nn", F32, o_halves=True)
    gv = _conv_gate(tag + "_conv", a2, w["conv_w_h"], w["conv_b_h"], R, tc)
    ffn = _matmul(tag + "_down", gv, w["w_ffn_down"], "nn", F32)
    (x_next,) = _rowwise(tag + "_res2", _f_resid, R, nct, [(x_mid, D, 0, D), (ffn, D, 0, D)], [], [], [], [g2], [(D, F32)])
    saved = dict(x=x, h1=h1, p=p, att_saved=att_saved, att=att, o_fb=o_fb, s_f=s_f, s_b=s_b, gla=gla, gm=gm,
                 ya=ya, yb=yb, yc=yc, merged=merged, mix=mix, x_mid=x_mid, h2=h2, a2=a2, gv=gv, ffn=ffn,
                 qk=(qk_rows, qk_consts, qk_params, qk_cparams), go_rows=go_rows, gm_info=(gm_rows, gm_params),
                 mg_rows=mg_rows)
    return x_next, saved, comm_out


def _layer_bwd(tag, dx_next, s, w, modv, R, nct, tc, make_comm, make_tail_comm):
    sh1, sc1, g1, sh2, sc2, g2 = modv
    gw = {}
    (dffn,), _, (dg2,) = _rowwise_bwd(tag + "_res2_b", _f_resid, R, nct, [(s["ffn"], D, 0, D), (s["ffn"], D, 0, D)], [], [], [], [g2],
                                      [(dx_next, D)], [None, BF16])
    dgv = _matmul(tag + "_down_da", dffn, w["w_ffn_down"], "nt", F32)
    gw["w_ffn_down"] = _matmul(tag + "_down_dw", s["gv"], dffn, "tn", F32)
    da2, dcw, dcb = _conv_gate_bwd(tag + "_conv_b", s["a2"], dgv, w["conv_w_h"], w["conv_b_h"], R, tc)
    gw["conv_w_h"], gw["conv_b_h"] = dcw, dcb
    dh2 = _matmul(tag + "_up_da", da2, w["w_ffn_up"], "nt", F32, a_halves=True)
    gw["w_ffn_up"] = _matmul(tag + "_up_dw", s["h2"], da2, "tn", F32, b_halves=True)
    (dx_mid,), (gw["norm2_g"],), (dsh2, dsc2) = _rowwise_bwd(
        tag + "_ln2_b", _f_lnmod, R, nct, [(s["x_mid"], D, 0, D)], [], [(w["norm2_g"], False)], [], [sh2, sc2],
        [(dh2, D)], [F32], adds=[dx_next])
    (dmix,), _, (dg1,) = _rowwise_bwd(tag + "_res1_b", _f_resid, R, nct, [(s["mix"], D, 0, D), (s["mix"], D, 0, D)], [], [], [], [g1],
                                      [(dx_mid, D)], [None, BF16])
    dmerged = _matmul(tag + "_out_da", dmix, w["w_out"], "nt", F32)
    gw["w_out"] = _matmul(tag + "_out_dw", s["merged"], dmix, "tn", F32)
    (dga, dgb, dgc, dya, dyb, dyc), _, _ = _rowwise_bwd(tag + "_merge_b", _f_merge, R, nct, s["mg_rows"], [], [], [], [],
                                                        [(dmerged, D)], [BF16] * 6)
    dgm = _matmul(tag + "_br_a_da", dya, w["w_br_a"], "nt", F32)
    datt = _matmul(tag + "_br_b_da", dyb, w["w_br_b"], "nt", F32)
    dgla = _matmul(tag + "_br_c_da", dyc, w["w_br_c"], "nt", F32)
    gm_rows, gm_params = s["gm_info"]
    gw["w_br_a"] = _matmul(tag + "_br_a_dw", s["gm"], dya, "tn", F32)
    gw["w_br_b"] = _matmul(tag + "_br_b_dw", s["att"], dyb, "tn", F32)
    gw["w_br_c"] = _matmul(tag + "_br_c_dw", s["gla"], dyc, "tn", F32)
    rg = GMLP_CHUNK
    (du, dv_), (gw["gmlp_norm_g"], gw["w_spatial"], gw["b_spatial_t"]), _ = _rowwise_bwd(
        tag + "_gmlp_b", _f_gmlp, rg, tc // rg, gm_rows, [], gm_params, [], [], [(dgm, GMLP_W)], [BF16, BF16])
    (do, dr), (gw["gla_norm_g"],), _ = _rowwise_bwd(tag + "_glaout_b", _f_glaout, R, nct, s["go_rows"], [],
                                                    [(w["gla_norm_g"], True)], [], [], [(dgla, GLA_V_W)], [F32, BF16])
    p = s["p"]
    *d_b, gw["w2p_b"], gw["b2_b"] = _gla_dir_bwd(tag + "_gla_b_b", p, w["w2p_b"], w["b2_b"], s["s_b"], do, rev=True, R=R, nct=nct)
    dgq, dgk, dgv_, daf, gw["w2p_f"], gw["b2_f"] = _gla_dir_bwd(tag + "_gla_f_b", p, w["w2p_f"], w["b2_f"], s["s_f"], do,
                                                              rev=False, R=R, nct=nct, adds=d_b, out_dtype=BF16)
    dqr, dkr, dvv, comm_out = _attention_bwd(tag, s["att_saved"], datt, tc, make_comm(gw))
    qk_rows, qk_consts, qk_params, qk_cparams = s["qk"]
    (dq, dk), (gw["q_norm_g"], gw["k_norm_g"]), _ = _rowwise_bwd(
        tag + "_qk_b", _f_qknorm, R, nct, qk_rows, qk_consts, qk_params, qk_cparams, [],
        [(dqr, Q_W), (dkr, KV_W)], [BF16, BF16], pre=_qk_pre)
    dp = jnp.concatenate([dga, dgb, dgc, du, dv_, dq, dgv_, dr, dgq, dgk, dk, dvv.astype(BF16), daf,
                          jnp.zeros((p.shape[0], PW - C_AF - 128), BF16)], axis=1)
    gw["w_in"] = _matmul(tag + "_in_dw", s["h1"], dp, "tn", F32)
    tail = make_tail_comm(gw)
    if tail is None:
        dh1, tail_out = _matmul(tag + "_in_da", dp, w["w_in"], "nt", F32), []
    else:
        dh1, tail_out = _matmul(tag + "_in_da", dp, w["w_in"], "nt", F32, comm=tail)
    (dx,), (gw["norm1_g"],), (dsh1, dsc1) = _rowwise_bwd(
        tag + "_ln1_b", _f_lnmod, R, nct, [(s["x"], D, 0, D)], [], [(w["norm1_g"], False)], [], [sh1, sc1],
        [(dh1, D)], [F32], adds=[dx_mid])
    return dx, gw, (dsh1, dsc1, dg1, dsh2, dsc2, dg2), comm_out, tail_out


_SHARDED = (("w_in", 1, True), ("w_br_a", 1, True), ("w_br_b", 1, True), ("w_br_c", 1, True), ("w_out", 0, True),
            ("w_ffn_up", 1, True), ("w_ffn_down", 0, True), ("conv_w", 1, False), ("w_alpha2", 2, False), ("b_alpha", 1, False))
_REPLICATED = ("c_ctx", "b_ada", "norm1_g", "norm2_g", "q_norm_g", "k_norm_g", "gmlp_norm_g", "w_spatial", "b_spatial",
               "gla_norm_g", "conv_b", "final_norm_g")
_WEIGHTS = ("c_ctx", "w_ada", "b_ada", "norm1_g", "norm2_g", "w_in", "q_norm_g", "k_norm_g", "gmlp_norm_g", "w_spatial",
            "b_spatial", "w_alpha2", "b_alpha", "gla_norm_g", "w_br_a", "w_br_b", "w_br_c", "w_out", "w_ffn_up", "conv_w",
            "conv_b", "w_ffn_down", "final_norm_g")


def _decay_weights(w_alpha2_l, b_alpha_l):
    out = []
    for d in range(2):
        w2p = jnp.zeros((128, GLA_QK_W), F32).at[GLA_RANK * d:GLA_RANK * (d + 1)].set(w_alpha2_l[d])
        out += [w2p, b_alpha_l[d][None, :]]
    return out


def _step(inp, wts, moms, vels):
    x, c, ctx, loss_target = inp
    t, tc = x.shape[1], ctx.shape[1]
    n = t + tc
    R = min(256, tc)
    nct = tc // R
    me = _my_index()
    depth = wts["w_in"].shape[0]

    late = [(nm, ax) for nm, ax, half in _SHARDED if half and nm != "w_in"]
    small = [(nm, ax) for nm, ax, half in _SHARDED if not half]
    w_in_shard = lambda l: wts["w_in"][l].astype(BF16)
    c8 = jnp.concatenate([c, jnp.zeros((7, D), F32)], axis=0)
    first = _exchange("gather_first", [w_in_shard(0)] + [wts[nm] for nm, _ in small] + [c8], gather=True)
    c_all = first[-1][:, 0, :]
    small_all = dict(zip([nm for nm, _ in small], first[1:-1]))

    def early_weights(l, w_in_all):
        w = {"w_in": _win_my_cols([w_in_all[s] for s in range(N_DEV)])}
        conv_w, w_alpha2, b_alpha = [jnp.concatenate([small_all[nm][s, l] for s in range(N_DEV)], axis=ax) for nm, ax in small]
        w["conv_w_h"] = conv_w.reshape(3, 2, FFN_H).transpose(1, 0, 2)
        w["conv_b_h"] = wts["conv_b"][l].reshape(2, 1, FFN_H)
        w["w2p_f"], w["b2_f"], w["w2p_b"], w["b2_b"] = _decay_weights(w_alpha2, b_alpha)
        w["norm1_g"] = wts["norm1_g"][l][None, :]
        w["norm2_g"] = wts["norm2_g"][l][None, :]
        w["q_norm_g"] = wts["q_norm_g"][l][None, :]
        w["k_norm_g"] = wts["k_norm_g"][l][None, :]
        w["gmlp_norm_g"] = wts["gmlp_norm_g"][l].reshape(GMLP_GROUPS, 1, GMLP_W // GMLP_GROUPS)
        w["w_spatial"] = wts["w_spatial"][l]
        w["b_spatial_t"] = wts["b_spatial"][l][:, :, None]
        w["gla_norm_g"] = wts["gla_norm_g"][l].reshape(GLA_HEADS, 1, GLA_DV)
        return w

    craw16 =jnp.concatenate([c_all, wts["c_ctx"][None, :], jnp.zeros((7, D), F32)], axis=0)
    acols = wts["w_ada"].shape[2]
    b_cols = lax.dynamic_slice_in_dim(wts["b_ada"], me * acols, acols, axis=1)[:, None, :]
    mod_part = _adaln_fwd("adaln", craw16, wts["w_ada"], b_cols)
    send = jnp.stack([mod_part[:, 8, :][None].repeat(N_DEV, 0), mod_part[:, :8, :].transpose(1, 0, 2)], axis=2)
    send = jnp.concatenate([send.reshape(N_DEV, 2 * depth, acols), jnp.zeros((N_DEV, 8 - 2 * depth, acols), F32)], axis=1)
    (got,) = _exchange("scatter_mod", [send], gather=False)
    mod = got[:, :2 * depth, :].transpose(1, 0, 2).reshape(depth, 2, N_MOD, 1, D)
    modv = [[mod[l, :, k] for k in range(N_MOD)] for l in range(depth)]

    cosq, sinq = _rope_tables(t, tc)
    consts = (cosq, sinq, _head_mean_matrix(Q_W), _head_tile_matrix(Q_W), _head_mean_matrix(KV_W), _head_tile_matrix(KV_W))
    xs = jnp.concatenate([ctx[0], x[0]], axis=0)
    saved, layers = [], []
    w_in_all = first[0]

    def late_weights(got):
        return {nm: jnp.concatenate([g[s] for s in range(N_DEV)], axis=ax) for (nm, ax), g in zip(late, got)}

    for l in range(depth):
        layers.append(early_weights(l, w_in_all))
        sending = [wts[nm][l].astype(BF16) for nm, _ in late] + ([w_in_shard(l + 1)] if l + 1 < depth else [])
        xs, sv, got = _layer_fwd("l%d" % l, xs, layers[l], modv[l], consts, R, nct, tc, (sending, True), late_weights)
        if l + 1 < depth:
            w_in_all = got[len(late)]
        saved.append(sv)
    loss_blk, dxs, dgf = _final_loss("final", xs, loss_target[0], wts["final_norm_g"][None, :], R, nct)
    loss = lax.psum(loss_blk[0, 0], ("x", "y", "c"))

    grads = [None] * depth
    dmods = [None] * depth
    late_parts = [None] * depth
    w_in_parts = [None] * depth
    w_in_grad_shards = lambda g: _win_shards(g["w_in"]).astype(BF16)

    def small_grad_shards(g):
        full = dict(conv_w=g["conv_w_h"].transpose(1, 0, 2).reshape(3, F2),
                    w_alpha2=jnp.stack([g["w2p_f"][:GLA_RANK], g["w2p_b"][GLA_RANK:2 * GLA_RANK]]),
                    b_alpha=jnp.stack([g["b2_f"][0], g["b2_b"][0]]))
        return [_split_shards(full[nm], ax) for nm, ax in small]

    for l in range(depth - 1, -1, -1):
        def make_comm(gw, l=l):
            sending = [_split_shards(gw[nm], ax).astype(BF16) for nm, ax in late]
            return sending + ([w_in_grad_shards(grads[l + 1])] if l + 1 < depth else []), False

        def make_tail_comm(gw, l=l):
            if l > 0:
                return None
            per_layer = [small_grad_shards(gw if k == 0 else grads[k]) for k in range(depth)]
            return [w_in_grad_shards(gw)] + [jnp.stack([per_layer[k][i] for k in range(depth)], axis=1) for i in range(len(small))], False

        dxs, grads[l], dmods[l], got, tail = _layer_bwd("l%d" % l, dxs, saved[l], layers[l], modv[l], R, nct, tc,
                                                        make_comm, make_tail_comm)
        late_parts[l] = got[:len(late)]
        if l + 1 < depth:
            w_in_parts[l + 1] = got[len(late)]
    w_in_parts[0], small_parts = tail[0], tail[1:]
    grad_x = dxs[tc:][None]

    dmod = jnp.stack([jnp.stack(dmods[l], axis=1) for l in range(depth)])
    dmod = dmod.reshape(depth, 2, N_DEV, acols).transpose(2, 0, 1, 3).reshape(N_DEV, 2 * depth, acols)
    dmod_send = jnp.concatenate([dmod, jnp.zeros((N_DEV, 8 - 2 * depth, acols), F32)], axis=1)
    (dm_got,) = _exchange("scatter_dmod", [dmod_send], gather=False)
    g_w_ada, dc16 = _adaln_bwd("adaln_b", craw16, wts["w_ada"], dm_got[:, :2 * depth].transpose(1, 0, 2))
    db_ada_part = jnp.stack([jnp.stack(dmods[l], axis=1) for l in range(depth)]).reshape(depth, 2, N_MOD * D).sum(axis=1)

    out = {}
    kinds = ("grad", "delta", "new_m", "new_v")
    view2 = lambda a: a.reshape(-1, a.shape[-1])
    sharded_parts = ([jnp.stack(w_in_parts, axis=1)]
                     + [jnp.stack([late_parts[l][k] for l in range(depth)], axis=1) for k in range(len(late))] + small_parts)
    for (nm, _), parts in zip([("w_in", 1)] + late + small, sharded_parts):
        res = _adamw("adamw_" + nm, view2(wts[nm]), view2(moms[nm]), view2(vels[nm]), parts.reshape(N_DEV, -1, parts.shape[-1]))
        for kind, flat in zip(kinds, res):
            out[kind, nm] = flat.reshape(wts[nm].shape)

    rep_g = dict(
        c_ctx=dc16[8], b_ada=db_ada_part, final_norm_g=dgf[0],
        norm1_g=jnp.stack([grads[l]["norm1_g"][0] for l in range(depth)]),
        norm2_g=jnp.stack([grads[l]["norm2_g"][0] for l in range(depth)]),
        q_norm_g=jnp.stack([grads[l]["q_norm_g"][0] for l in range(depth)]),
        k_norm_g=jnp.stack([grads[l]["k_norm_g"][0] for l in range(depth)]),
        gmlp_norm_g=jnp.stack([grads[l]["gmlp_norm_g"].reshape(GMLP_W) for l in range(depth)]),
        w_spatial=jnp.stack([grads[l]["w_spatial"] for l in range(depth)]),
        b_spatial=jnp.stack([grads[l]["b_spatial_t"][:, :, 0] for l in range(depth)]),
        gla_norm_g=jnp.stack([grads[l]["gla_norm_g"].reshape(GLA_V_W) for l in range(depth)]),
        conv_b=jnp.stack([grads[l]["conv_b_h"].reshape(F2) for l in range(depth)]),
    )
    rep_shapes = [wts[nm].shape for nm in _REPLICATED]
    (rg_parts,) = _exchange("gather_rep_grads", [_pack_rows([rep_g[nm] for nm in _REPLICATED], 16)], gather=True)
    rpk = lambda src: _pack_rows([src[nm] for nm in _REPLICATED], 16)
    res = _adamw("adamw_rep", rpk(wts), rpk(moms), rpk(vels), rg_parts)
    for kind, rows in zip(kinds, res):
        for nm, piece in zip(_REPLICATED, _unpack_rows(rows, rep_shapes)):
            out[kind, nm] = piece

    res = _adamw("adamw_ada", view2(wts["w_ada"]), view2(moms["w_ada"]), view2(vels["w_ada"]), view2(g_w_ada)[None])
    for kind, flat in zip(kinds, res):
        out[kind, "w_ada"] = flat.reshape(wts["w_ada"].shape)

    return (loss, grad_x, *[out[kind, nm] for kind in kinds for nm in _WEIGHTS])


def kernel(x, c, ctx, c_ctx, w_ada, b_ada, norm1_g, norm2_g, w_in, q_norm_g, k_norm_g, gmlp_norm_g, w_spatial, b_spatial, w_alpha2, b_alpha, gla_norm_g, w_br_a, w_br_b, w_br_c, w_out, w_ffn_up, conv_w, conv_b, w_ffn_down, final_norm_g, loss_target, m_c_ctx, m_w_ada, m_b_ada, m_norm1_g, m_norm2_g, m_w_in, m_q_norm_g, m_k_norm_g, m_gmlp_norm_g, m_w_spatial, m_b_spatial, m_w_alpha2, m_b_alpha, m_gla_norm_g, m_w_br_a, m_w_br_b, m_w_br_c, m_w_out, m_w_ffn_up, m_conv_w, m_conv_b, m_w_ffn_down, m_final_norm_g, v_c_ctx, v_w_ada, v_b_ada, v_norm1_g, v_norm2_g, v_w_in, v_q_norm_g, v_k_norm_g, v_gmlp_norm_g, v_w_spatial, v_b_spatial, v_w_alpha2, v_b_alpha, v_gla_norm_g, v_w_br_a, v_w_br_b, v_w_br_c, v_w_out, v_w_ffn_up, v_conv_w, v_conv_b, v_w_ffn_down, v_final_norm_g):
    wts = dict(zip(_WEIGHTS, (c_ctx, w_ada, b_ada, norm1_g, norm2_g, w_in, q_norm_g, k_norm_g, gmlp_norm_g, w_spatial, b_spatial,
                              w_alpha2, b_alpha, gla_norm_g, w_br_a, w_br_b, w_br_c, w_out, w_ffn_up, conv_w, conv_b, w_ffn_down,
                              final_norm_g)))
    moms = dict(zip(_WEIGHTS, (m_c_ctx, m_w_ada, m_b_ada, m_norm1_g, m_norm2_g, m_w_in, m_q_norm_g, m_k_norm_g, m_gmlp_norm_g,
                               m_w_spatial, m_b_spatial, m_w_alpha2, m_b_alpha, m_gla_norm_g, m_w_br_a, m_w_br_b, m_w_br_c, m_w_out,
                               m_w_ffn_up, m_conv_w, m_conv_b, m_w_ffn_down, m_final_norm_g)))
    vels = dict(zip(_WEIGHTS, (v_c_ctx, v_w_ada, v_b_ada, v_norm1_g, v_norm2_g, v_w_in, v_q_norm_g, v_k_norm_g, v_gmlp_norm_g,
                               v_w_spatial, v_b_spatial, v_w_alpha2, v_b_alpha, v_gla_norm_g, v_w_br_a, v_w_br_b, v_w_br_c, v_w_out,
                               v_w_ffn_up, v_conv_w, v_conv_b, v_w_ffn_down, v_final_norm_g)))
    return _step((x, c, ctx, loss_target), wts, moms, vels)
```

```python
import functools

import jax
import jax.numpy as jnp
from jax import lax
from jax.experimental import pallas as pl
from jax.experimental.pallas import tpu as pltpu

F32 = jnp.float32
BF16 = jnp.bfloat16
HI = lax.Precision.HIGHEST
MESH_ID = pl.DeviceIdType.MESH

N_DEV = 8
EPS = 1e-6
D = 1024
N_MOD = 6
HEAD_DIM = 64
N_Q_HEADS = 8
N_KV_HEADS = 2
Q_GROUP = 4
Q_W = 512
KV_W = 128
GRID_W = 64
ROPE_THETA = 10000.0
ROPE_FREQS = 16
GMLP_CHUNK = 128
GMLP_GROUPS = 4
GMLP_W = 512
GLA_HEADS = 4
GLA_QK_W = 256
GLA_V_W = 512
GLA_DK = 64
GLA_DV = 128
GLA_RANK = 16
GLA_TAU = 16.0
GLA_CHUNK = 64
FFN_H = 2816
F2 = 2 * FFN_H
IN_SPLITS = (512, 512, 512, 128, 128, 256, 256, 512, 16, 16, 512, 1024, 1024, 1024)
IN_WIDTH = sum(IN_SPLITS)

C_GA, C_GB, C_GC = 0, 1024, 2048
C_U, C_V, C_Q, C_GV, C_R = 3072, 3584, 4096, 4608, 5120
C_GQ, C_GK = 5632, 5888
C_K, C_VV, C_AF = 6144, 6272, 6400
PW = 6656

ADAM_LR = 0.001
ADAM_B1 = 0.9
ADAM_B2 = 0.999
ADAM_EPS = 1e-08
ADAM_WD = 0.01
ADAM_STEP = 10

V7X_VMEM_LIMIT = 56 * 1024 * 1024

_ARB1 = pltpu.CompilerParams(dimension_semantics=("arbitrary",), vmem_limit_bytes=V7X_VMEM_LIMIT)


def _pick(dim, prefs):
    for p in prefs:
        if dim % p == 0:
            return p
    return dim


def _hdot(a, b, dims=(((1,), (0,)), ((), ()))):
    return lax.dot_general(a, b, dims, precision=HI, preferred_element_type=F32)


_NT = (((1,), (1,)), ((), ()))
_TN = (((0,), (0,)), ((), ()))
_NN = (((1,), (0,)), ((), ()))


def _matmul(name, a, b, mode, out_dtype, *, a_halves=False, b_halves=False, o_halves=False, comm=None):
    def dims2(x, halves):
        return (x.shape[1], 2 * x.shape[2]) if halves else x.shape

    ar, ac = dims2(a, a_halves)
    br, bc = dims2(b, b_halves)
    if mode == "nn":
        M, K, N = ar, ac, bc
    elif mode == "nt":
        M, K, N = ar, ac, br
    else:
        M, K, N = ac, ar, bc
    row_prefs = (768, 512, 384, 256, 128)
    n_unit = N // 2 if (o_halves or (b_halves and mode != "nt")) else N
    k_unit = K // 2 if (a_halves and mode != "tn") else K
    if mode == "tn":
        tm = _pick(M, (1024, 1408, 512, 256, 128))
        tk = _pick(K, (1408,) + row_prefs)
    else:
        tm = _pick(M, row_prefs)
        tk = _pick(k_unit, (1664, 1408, 1024, 512, 256, 128))
    tn = _pick(n_unit, (1664, 1408, 1024, 512, 256, 128))
    nk = K // tk

    def spec(shape2, halves, blk, imap):
        if not halves:
            return pl.BlockSpec(blk, imap)
        nhalf = (shape2[1] // 2) // blk[1]

        def im(i, j, k):
            r, c = imap(i, j, k)
            return (c // nhalf, r, c % nhalf)
        return pl.BlockSpec((None,) + blk, im)

    if mode == "nn":
        a_spec = spec((ar, ac), a_halves, (tm, tk), lambda i, j, k: (i, k))
        b_spec = spec((br, bc), b_halves, (tk, tn), lambda i, j, k: (k, j))
        dn = _NN
    elif mode == "nt":
        a_spec = spec((ar, ac), a_halves, (tm, tk), lambda i, j, k: (i, k))
        b_spec = spec((br, bc), b_halves, (tn, tk), lambda i, j, k: (j, k))
        dn = _NT
    else:
        a_spec = spec((ar, ac), a_halves, (tk, tm), lambda i, j, k: (k, i))
        b_spec = spec((br, bc), b_halves, (tk, tn), lambda i, j, k: (k, j))
        dn = _TN
    o_spec = spec((M, N), o_halves, (tm, tn), lambda i, j, k: (i, j))
    o_shape = (2, M, N // 2) if o_halves else (M, N)

    def body(a_ref, b_ref, o_ref, acc_ref):
        k = pl.program_id(2)
        part = lax.dot_general(a_ref[...], b_ref[...], dn, preferred_element_type=F32)
        if nk == 1:
            o_ref[...] = part.astype(o_ref.dtype)
        else:
            @pl.when(k == 0)
            def _():
                acc_ref[...] = part

            @pl.when(k > 0)
            def _():
                acc_ref[...] += part

            @pl.when(k == nk - 1)
            def _():
                o_ref[...] = acc_ref[...].astype(o_ref.dtype)

    grid = (M // tm, N // tn, nk)
    body, xspecs, xshapes, xsems = _with_exchange(body, 2, 1, grid, comm)
    res = pl.pallas_call(
        body, name=name, grid=grid,
        in_specs=[a_spec, b_spec] + xspecs, out_specs=[o_spec] + xspecs,
        out_shape=[jax.ShapeDtypeStruct(o_shape, out_dtype)] + xshapes,
        scratch_shapes=[pltpu.VMEM((tm, tn), F32)] + xsems,
        compiler_params=pltpu.CompilerParams(dimension_semantics=("arbitrary", "arbitrary", "arbitrary"),
                                             vmem_limit_bytes=V7X_VMEM_LIMIT),
    )(a, b, *(comm[0] if comm else []))
    return res[0] if comm is None else (res[0], list(res[1:]))


def _full_spec(shape):
    nd = len(shape)
    return pl.BlockSpec(tuple(shape), lambda i, _nd=nd: (0,) * _nd)


def _row_spec(R, W, cb):
    return pl.BlockSpec((R, W), lambda i, _cb=cb: (i, _cb))


def _load_rows(refs, specs):
    vals = []
    for ref, (_, W, _, pw) in zip(refs, specs):
        if pw == W:
            vals.append(ref[...].astype(F32))
        else:
            vals.append([ref[:, k * pw:(k + 1) * pw].astype(F32) for k in range(W // pw)])
    return vals


def _load_params(refs, specs):
    vals = []
    for ref, (arr, split) in zip(refs, specs):
        if split:
            vals.append([ref[k] for k in range(arr.shape[0])])
        else:
            vals.append(ref[...])
    return vals


def _mod_spec(nct, width):
    return pl.BlockSpec((None, 1, width), lambda i: (jnp.minimum(i // nct, 1), 0, 0))


def _rowwise(name, f, R, nct, rows, consts, params, cparams, mods, outs, post=None):
    n = rows[0][0].shape[0]
    nr, nc, npar, ncp, nm = len(rows), len(consts), len(params), len(cparams), len(mods)

    def body(*refs):
        pos = 0
        rr = refs[pos:pos + nr]; pos += nr
        cr = refs[pos:pos + nc]; pos += nc
        pr = refs[pos:pos + npar]; pos += npar
        cpr = refs[pos:pos + ncp]; pos += ncp
        mr = refs[pos:pos + nm]; pos += nm
        orefs = refs[pos:]
        res = f(_load_rows(rr, rows), _load_params(pr, params), [m[...] for m in mr],
                _load_rows(cr, consts), _load_params(cpr, cparams))
        if post is not None:
            res = post(res, _load_rows(cr, consts))
        for o_ref, r in zip(orefs, res):
            o_ref[...] = r.astype(o_ref.dtype)

    in_specs = ([_row_spec(R, W, cb) for (_, W, cb, _) in rows + consts]
                + [_full_spec(a.shape) for (a, _) in params + cparams]
                + [_mod_spec(nct, m.shape[2]) for m in mods])
    args = [a for (a, _, _, _) in rows + consts] + [a for (a, _) in params + cparams] + list(mods)
    return pl.pallas_call(
        body, name=name, grid=(n // R,), in_specs=in_specs,
        out_specs=[_row_spec(R, w, 0) for (w, _) in outs],
        out_shape=[jax.ShapeDtypeStruct((n, w), dt) for (w, dt) in outs],
        compiler_params=_ARB1,
    )(*args)


def _rowwise_bwd(name, f, R, nct, rows, consts, params, cparams, mods, douts, drow, adds=None, pre=None):
    n = rows[0][0].shape[0]
    adds = adds or [None] * len(rows)
    nr, nc, npar, ncp, nm, nd = len(rows), len(consts), len(params), len(cparams), len(mods), len(douts)
    add_ix = [k for k in range(nr) if adds[k] is not None]
    out_ix = [k for k in range(nr) if drow[k] is not None]

    def body(*refs):
        i = pl.program_id(0)
        pos = 0
        rr = refs[pos:pos + nr]; pos += nr
        cr = refs[pos:pos + nc]; pos += nc
        pr = refs[pos:pos + npar]; pos += npar
        cpr = refs[pos:pos + ncp]; pos += ncp
        mr = refs[pos:pos + nm]; pos += nm
        dr = refs[pos:pos + nd]; pos += nd
        ar = refs[pos:pos + len(add_ix)]; pos += len(add_ix)
        drr = refs[pos:pos + len(out_ix)]; pos += len(out_ix)
        dpr = refs[pos:pos + npar]; pos += npar
        dmr = refs[pos:pos + nm]; pos += nm

        cv = _load_rows(cr, consts)
        cpv = _load_params(cpr, cparams)
        _, vjp = jax.vjp(lambda rv, pv, mv: f(rv, pv, mv, cv, cpv),
                         _load_rows(rr, rows), _load_params(pr, params), [m[...] for m in mr])
        dv = [d[...].astype(F32) for d in dr]
        if pre is not None:
            dv = pre(dv, cv)
        g_rows, g_params, g_mods = vjp(tuple(dv))

        for ref, k in zip(drr, out_ix):
            _, W, _, pw = rows[k]
            g = g_rows[k]
            extra = ar[add_ix.index(k)] if k in add_ix else None
            if pw == W:
                if extra is not None:
                    g = g + extra[...].astype(F32)
                ref[...] = g.astype(ref.dtype)
            else:
                for q in range(W // pw):
                    gq = g[q]
                    if extra is not None:
                        gq = gq + extra[:, q * pw:(q + 1) * pw].astype(F32)
                    ref[:, q * pw:(q + 1) * pw] = gq.astype(ref.dtype)

        @pl.when(i == 0)
        def _():
            for ref in dpr:
                ref[...] = jnp.zeros_like(ref)

        for ref, (arr, split), g in zip(dpr, params, g_params):
            if split:
                for k in range(arr.shape[0]):
                    ref[k] += g[k]
            else:
                ref[...] += g

        @pl.when((i == 0) | (i == nct))
        def _():
            for ref in dmr:
                ref[...] = jnp.zeros_like(ref)

        for ref, g in zip(dmr, g_mods):
            ref[...] += g

    in_specs = ([_row_spec(R, W, cb) for (_, W, cb, _) in rows + consts]
                + [_full_spec(a.shape) for (a, _) in params + cparams]
                + [_mod_spec(nct, m.shape[2]) for m in mods]
                + [_row_spec(R, W, 0) for (_, W) in douts]
                + [_row_spec(R, rows[k][1], 0) for k in add_ix])
    args = ([a for (a, _, _, _) in rows + consts] + [a for (a, _) in params + cparams] + list(mods)
            + [a for (a, _) in douts] + [adds[k] for k in add_ix])
    out_specs = ([_row_spec(R, rows[k][1], 0) for k in out_ix]
                 + [_full_spec(a.shape) for (a, _) in params]
                 + [_mod_spec(nct, m.shape[2]) for m in mods])
    out_shape = ([jax.ShapeDtypeStruct((n, rows[k][1]), drow[k]) for k in out_ix]
                 + [jax.ShapeDtypeStruct(a.shape, F32) for (a, _) in params]
                 + [jax.ShapeDtypeStruct(m.shape, F32) for m in mods])
    res = pl.pallas_call(
        body, name=name, grid=(n // R,), in_specs=in_specs, out_specs=out_specs, out_shape=out_shape,
        compiler_params=_ARB1,
    )(*args)
    no = len(out_ix)
    return list(res[:no]), list(res[no:no + npar]), list(res[no + npar:])


def _rms(x, g):
    return x * lax.rsqrt(jnp.mean(x * x, axis=-1, keepdims=True) + EPS) * g


def _f_lnmod(rv, pv, mv, cv, cpv):
    (x,), (g,), (shift, scale) = rv, pv, mv
    return (_rms(x, g) * (1.0 + scale) + shift,)


def _f_resid(rv, pv, mv, cv, cpv):
    (x, y), (gate,) = rv, mv
    return (x + gate * y,)


def _f_merge(rv, pv, mv, cv, cpv):
    ga, gb, gc, ya, yb, yc = rv
    return (_sigmoid(ga) * ya + _sigmoid(gb) * yb + _sigmoid(gc) * yc,)


def _split3(x):
    hi = x.astype(BF16)
    rest = x - hi.astype(F32)
    mid = rest.astype(BF16)
    return hi, mid, (rest - mid.astype(F32)).astype(BF16)


def _dot3_right(x, m):
    mb = m.astype(BF16)
    return sum(lax.dot_general(piece, mb, _NN, preferred_element_type=F32) for piece in _split3(x))


@jax.custom_vjp
def _sym_dot(x, m):
    return _dot3_right(x, m)


_sym_dot.defvjp(lambda x, m: (_dot3_right(x, m), m), lambda m, g: (_dot3_right(g, m), jnp.zeros_like(m)))


def _f_qknorm(rv, pv, mv, cv, cpv):
    (q, k), (gq, gk), (bdq, eq, bdk, ek) = rv, pv, cpv
    qn = q * lax.rsqrt(_sym_dot(q * q, bdq) + EPS) * _hdot(gq, eq)
    kn = k * lax.rsqrt(_sym_dot(k * k, bdk) + EPS) * _hdot(gk, ek)
    return (qn, kn)


def _rope(x, cos, sin):
    w = x.shape[1]
    lane = lax.broadcasted_iota(jnp.int32, x.shape, 1)
    partner = jnp.where((lane & 31) < 16, pltpu.roll(x, w - 16, 1), pltpu.roll(x, 16, 1))
    return x * cos + partner * sin


_LOG2E = 1.4426950408889634
_LN2 = 0.6931471805599453


def _qk_post(res, cv):
    (qn, kn), (cq, sq, ck, sk) = res, cv
    return (_rope(qn, cq, sq) * (HEAD_DIM ** -0.5 * _LOG2E), _rope(kn, ck, sk))


def _qk_pre(dv, cv):
    (dq, dk), (cq, sq, ck, sk) = dv, cv
    return (_rope(dq * (HEAD_DIM ** -0.5), cq, -sq), _rope(dk * _LN2, ck, -sk))


def _f_gmlp(rv, pv, mv, cv, cpv):
    (u, vs), (ng, ws, bt) = rv, pv
    pieces = []
    for g in range(GMLP_GROUPS):
        vn = _rms(jax.nn.gelu(vs[g]), ng[g])
        pieces.append(_bdot(ws[g], vn, "nn") + bt[g])
    return (jax.nn.gelu(u) * jnp.concatenate(pieces, axis=1),)


def _f_glaout(rv, pv, mv, cv, cpv):
    (os_, rs), (gn,) = rv, pv
    pieces = [_rms(os_[h], gn[h]) * (rs[h] * _sigmoid(rs[h])) for h in range(GLA_HEADS)]
    return (jnp.concatenate(pieces, axis=1),)


_CONV_CB = 1408
_CONV_STRIP = 128


def _sigmoid(x):
    return 0.5 * jnp.tanh(0.5 * x) + 0.5


def _halo_keep(i, R, tc, n):
    first, end = i * R, (i + 1) * R
    keep_prev = jnp.where((first == 0) | (first == tc), 0.0, 1.0)
    keep_next = jnp.where((end == tc) | (end == n), 0.0, 1.0)
    return keep_prev, keep_next


def _conv_specs(R, n):
    nb8 = n // 8
    main = pl.BlockSpec((2, R, _CONV_CB), lambda j, i: (0, i, j))
    prev = pl.BlockSpec((2, 8, _CONV_CB), lambda j, i: (0, jnp.maximum(i * (R // 8) - 1, 0), j))
    nxt = pl.BlockSpec((2, 8, _CONV_CB), lambda j, i: (0, jnp.minimum((i + 1) * (R // 8), nb8 - 1), j))
    cw = pl.BlockSpec((2, 3, _CONV_CB), lambda j, i: (0, 0, j))
    cb = pl.BlockSpec((2, 1, _CONV_CB), lambda j, i: (0, 0, j))
    return main, prev, nxt, cw, cb


def _row_before(x):
    return pltpu.roll(x, 1, 0)


def _row_after(x):
    return pltpu.roll(x, x.shape[0] - 1, 0)


def _conv_gate(name, a2, cw, cb, R, tc):
    n = a2.shape[1]
    main, prev, nxt, cws, cbs = _conv_specs(R, n)

    def body(a_ref, p_ref, n_ref, cw_ref, cb_ref, o_ref):
        keep_prev, keep_next = _halo_keep(pl.program_id(1), R, tc, n)

        def strip(c, carry):
            ls = pl.ds(pl.multiple_of(c * _CONV_STRIP, _CONV_STRIP), _CONV_STRIP)
            acts = []
            for h in range(2):
                win = jnp.concatenate([p_ref[h, :, ls] * keep_prev, a_ref[h, :, ls], n_ref[h, :, ls] * keep_next], axis=0)
                acts.append((cb_ref[h, :, ls] + cw_ref[h, 1:2, ls] * win + cw_ref[h, 0:1, ls] * _row_before(win)
                             + cw_ref[h, 2:3, ls] * _row_after(win))[8:8 + R])
            g, v = acts
            o_ref[:, ls] = (g * _sigmoid(g) * v).astype(o_ref.dtype)
            return carry

        lax.fori_loop(0, _CONV_CB // _CONV_STRIP, strip, 0)

    return pl.pallas_call(
        body, name=name, grid=(FFN_H // _CONV_CB, n // R),
        in_specs=[main, prev, nxt, cws, cbs],
        out_specs=pl.BlockSpec((R, _CONV_CB), lambda j, i: (i, j)),
        out_shape=jax.ShapeDtypeStruct((n, FFN_H), BF16),
        compiler_params=pltpu.CompilerParams(dimension_semantics=("arbitrary", "arbitrary"),
                                             vmem_limit_bytes=V7X_VMEM_LIMIT),
    )(a2, a2, a2, cw, cb)


def _conv_gate_bwd(name, a2, dgv, cw, cb, R, tc):
    n = a2.shape[1]
    nb8 = n // 8
    main, prev, nxt, cws, cbs = _conv_specs(R, n)
    d_main = pl.BlockSpec((R, _CONV_CB), lambda j, i: (i, j))
    d_prev = pl.BlockSpec((8, _CONV_CB), lambda j, i: (jnp.maximum(i * (R // 8) - 1, 0), j))
    d_next = pl.BlockSpec((8, _CONV_CB), lambda j, i: (jnp.minimum((i + 1) * (R // 8), nb8 - 1), j))
    mid = slice(8, 8 + R)

    def body(a_ref, p_ref, n_ref, cw_ref, cb_ref, d_ref, dp_ref, dn_ref, da_ref, dcw_ref, dcb_ref):
        i = pl.program_id(1)
        keep_prev, keep_next = _halo_keep(i, R, tc, n)

        @pl.when(i == 0)
        def _():
            dcw_ref[...] = jnp.zeros_like(dcw_ref)
            dcb_ref[...] = jnp.zeros_like(dcb_ref)

        def strip(c, carry):
            ls = pl.ds(pl.multiple_of(c * _CONV_STRIP, _CONV_STRIP), _CONV_STRIP)
            cws_ = [[cw_ref[h, k:k + 1, ls] for k in range(3)] for h in range(2)]
            wins = [jnp.concatenate([p_ref[h, :, ls] * keep_prev, a_ref[h, :, ls], n_ref[h, :, ls] * keep_next], axis=0)
                    for h in range(2)]
            g, v = [cb_ref[h, :, ls] + cws_[h][1] * wins[h] + cws_[h][0] * _row_before(wins[h]) + cws_[h][2] * _row_after(wins[h])
                    for h in range(2)]
            dout = jnp.concatenate([dp_ref[:, ls].astype(F32) * keep_prev, d_ref[:, ls].astype(F32),
                                    dn_ref[:, ls].astype(F32) * keep_next], axis=0)
            sg = _sigmoid(g)
            das = [dout * v * sg * (1.0 + g * (1.0 - sg)), dout * g * sg]
            for h in range(2):
                da_win = das[h]
                da_ref[h, :, ls] = (cws_[h][1] * da_win + cws_[h][0] * _row_after(da_win)
                                    + cws_[h][2] * _row_before(da_win))[mid].astype(da_ref.dtype)
                da = da_win[mid]
                dcw_ref[h, 0:1, ls] += jnp.sum(da * _row_before(wins[h])[mid], axis=0, keepdims=True)
                dcw_ref[h, 1:2, ls] += jnp.sum(da * wins[h][mid], axis=0, keepdims=True)
                dcw_ref[h, 2:3, ls] += jnp.sum(da * _row_after(wins[h])[mid], axis=0, keepdims=True)
                dcb_ref[h, :, ls] += jnp.sum(da, axis=0, keepdims=True)
            return carry

        lax.fori_loop(0, _CONV_CB // _CONV_STRIP, strip, 0)

    return pl.pallas_call(
        body, name=name, grid=(FFN_H // _CONV_CB, n // R),
        in_specs=[main, prev, nxt, cws, cbs, d_main, d_prev, d_next],
        out_specs=[main, cws, cbs],
        out_shape=[jax.ShapeDtypeStruct((2, n, FFN_H), BF16), jax.ShapeDtypeStruct((2, 3, FFN_H), F32),
                   jax.ShapeDtypeStruct((2, 1, FFN_H), F32)],
        compiler_params=pltpu.CompilerParams(dimension_semantics=("arbitrary", "arbitrary"),
                                             vmem_limit_bytes=V7X_VMEM_LIMIT),
    )(a2, a2, a2, cw, cb, dgv, dgv, dgv)


_ARB2 = pltpu.CompilerParams(dimension_semantics=("arbitrary", "arbitrary"), vmem_limit_bytes=V7X_VMEM_LIMIT)


def _with_exchange(body, n_in, n_out, grid, comm):
    if comm is None:
        return body, [], [], []
    xs, gather = comm
    na = len(xs)
    specs, shapes, sems = _exchange_io(xs, gather)

    def wrapped(*refs):
        ins, x_refs = refs[:n_in], refs[n_in:n_in + na]
        outs, o_refs = refs[n_in + na:n_in + na + n_out], refs[n_in + na + n_out:n_in + 2 * na + n_out]
        scratch, sem_refs = refs[n_in + 2 * na + n_out:-3], refs[-3:]
        start, wait = _exchange_plan(x_refs, o_refs, *sem_refs, gather)
        ids = [pl.program_id(d) for d in range(len(grid))]
        first, last = ids[0] == 0, ids[0] == grid[0] - 1
        for d in range(1, len(grid)):
            first, last = first & (ids[d] == 0), last & (ids[d] == grid[d] - 1)

        @pl.when(first)
        def _():
            start()

        body(*ins, *outs, *scratch)

        @pl.when(last)
        def _():
            wait()

    return wrapped, specs, shapes, sems


def _attn_fwd(name, qt8, k2, vt2, comm=None):
    nq, nk = qt8.shape[2], k2.shape[1]
    tq = _pick(nq, (256, 128))
    tk = _pick(nk, (1408, 768, 256, 128))

    va = vt2.shape[1]

    def body(qt_ref, k_ref, vt_ref, ot_ref, lse_ref, m_ref, acc_ref):
        m_ref[...] = jnp.full((Q_GROUP, 1, tq), -1e30, F32)
        acc_ref[...] = jnp.zeros((Q_GROUP, va, tq), F32)

        def step(j, carry):
            sl = pl.ds(pl.multiple_of(j * tk, tk), tk)
            kj = k_ref[sl, :]
            vtj = vt_ref[:, sl]
            sts = [lax.dot_general(kj, qt_ref[h], _NN, preferred_element_type=F32) for h in range(Q_GROUP)]
            m_old = [m_ref[h] for h in range(Q_GROUP)]
            acc_old = [acc_ref[h] for h in range(Q_GROUP)]
            m_new = [jnp.maximum(m_old[h], jnp.max(sts[h], axis=0, keepdims=True)) for h in range(Q_GROUP)]
            pts = [jnp.exp2(sts[h] - m_new[h]).astype(BF16) for h in range(Q_GROUP)]
            pvs = [lax.dot_general(vtj, pts[h], _NN, preferred_element_type=F32) for h in range(Q_GROUP)]
            for h in range(Q_GROUP):
                acc_ref[h] = jnp.exp2(m_old[h] - m_new[h]) * acc_old[h] + pvs[h]
                m_ref[h] = m_new[h]
            return carry

        lax.fori_loop(0, nk // tk, step, 0, unroll=2)
        for h in range(Q_GROUP):
            l = acc_ref[h, HEAD_DIM:HEAD_DIM + 1, :]
            ot_ref[h] = (acc_ref[h, 0:HEAD_DIM, :] / l).astype(ot_ref.dtype)
            lse_ref[h] = m_ref[h] + jnp.log2(l)

    qspec = pl.BlockSpec((Q_GROUP, HEAD_DIM, tq), lambda g, i: (g, 0, i))
    lspec = pl.BlockSpec((Q_GROUP, 1, tq), lambda g, i: (g, 0, i))
    grid = (N_KV_HEADS, nq // tq)
    body, xspecs, xshapes, xsems = _with_exchange(body, 3, 2, grid, comm)
    res = pl.pallas_call(
        body, name=name, grid=grid,
        in_specs=[qspec, pl.BlockSpec((None, nk, HEAD_DIM), lambda g, i: (g, 0, 0)),
                  pl.BlockSpec((None, va, nk), lambda g, i: (g, 0, 0))] + xspecs,
        out_specs=[qspec, lspec] + xspecs,
        out_shape=[jax.ShapeDtypeStruct((N_Q_HEADS, HEAD_DIM, nq), BF16), jax.ShapeDtypeStruct((N_Q_HEADS, 1, nq), F32)] + xshapes,
        scratch_shapes=[pltpu.VMEM((Q_GROUP, 1, tq), F32), pltpu.VMEM((Q_GROUP, va, tq), F32)] + xsems,
        compiler_params=_ARB2,
    )(qt8, k2, vt2, *(comm[0] if comm else []))
    return res[0], res[1], list(res[2:])


def _attn_bwd(name, qt8, k2, v2, kt2, ot8, dot8, lse, dkt0, dvt0, comm=None):
    nq, nk = qt8.shape[2], k2.shape[1]
    n0 = dkt0.shape[2]
    tq = _pick(nq, (256, 128))
    tk = _pick(nk, (1408, 768, 256, 128))
    ts = tk
    heads = range(Q_GROUP)

    def body(qt_ref, k_ref, v_ref, kt_ref, ot_ref, dot_ref, lse_ref, dk0_ref, dv0_ref, dqt_ref, dkt_ref, dvt_ref, dl_ref, dq_acc):
        @pl.when(pl.program_id(1) == 0)
        def _():
            dkt_ref[...] = jnp.zeros_like(dkt_ref)
            dvt_ref[...] = jnp.zeros_like(dvt_ref)
            dkt_ref[:, 0:n0] = dk0_ref[...]
            dvt_ref[:, 0:n0] = dv0_ref[...]

        for h in heads:
            dl_ref[h] = jnp.sum(dot_ref[h].astype(F32) * ot_ref[h].astype(F32), axis=0, keepdims=True)
        dq_acc[...] = jnp.zeros((Q_GROUP, HEAD_DIM, tq), F32)

        items = [(s, h) for s in range(tk // ts) for h in heads]

        def step(j, carry):
            def keys(s):
                return pl.ds(pl.multiple_of(j * tk + s * ts, ts), ts)

            def scores(item):
                s, h = item
                return (lax.dot_general(k_ref[keys(s), :], qt_ref[h], _NN, preferred_element_type=F32),
                        lax.dot_general(v_ref[keys(s), :], dot_ref[h], _NN, preferred_element_type=F32))

            nxt = scores(items[0])
            for n, (s, h) in enumerate(items):
                st, dpt = nxt
                if n + 1 < len(items):
                    nxt = scores(items[n + 1])
                pt = jnp.exp2(st - lse_ref[h])
                dst = (pt * (dpt - dl_ref[h])).astype(BF16)
                dq_acc[h] += lax.dot_general(kt_ref[:, keys(s)], dst, _NN, preferred_element_type=F32)
                dv_h = lax.dot_general(dot_ref[h], pt.astype(BF16), _NT, preferred_element_type=F32)
                dk_h = lax.dot_general(qt_ref[h], dst, _NT, preferred_element_type=F32)
                dvt_s, dkt_s = (dv_h, dk_h) if h == 0 else (dvt_s + dv_h, dkt_s + dk_h)
                if h == Q_GROUP - 1:
                    dvt_ref[:, keys(s)] += dvt_s
                    dkt_ref[:, keys(s)] += dkt_s
            return carry

        lax.fori_loop(0, nk // tk, step, 0)
        dqt_ref[...] = dq_acc[...]

    tspec = pl.BlockSpec((Q_GROUP, HEAD_DIM, tq), lambda g, i: (g, 0, i))
    lspec = pl.BlockSpec((Q_GROUP, 1, tq), lambda g, i: (g, 0, i))
    kspec = pl.BlockSpec((None, nk, HEAD_DIM), lambda g, i: (g, 0, 0))
    ktspec = pl.BlockSpec((None, HEAD_DIM, nk), lambda g, i: (g, 0, 0))
    k0spec = pl.BlockSpec((None, HEAD_DIM, n0), lambda g, i: (g, 0, 0))
    grid = (N_KV_HEADS, nq // tq)
    body, xspecs, xshapes, xsems = _with_exchange(body, 9, 3, grid, comm)
    res = pl.pallas_call(
        body, name=name, grid=grid,
        in_specs=[tspec, kspec, kspec, ktspec, tspec, tspec, lspec, k0spec, k0spec] + xspecs,
        out_specs=[tspec, ktspec, ktspec] + xspecs,
        out_shape=[jax.ShapeDtypeStruct((N_Q_HEADS, HEAD_DIM, nq), F32), jax.ShapeDtypeStruct((N_KV_HEADS, HEAD_DIM, nk), F32),
                   jax.ShapeDtypeStruct((N_KV_HEADS, HEAD_DIM, nk), F32)] + xshapes,
        scratch_shapes=[pltpu.VMEM((Q_GROUP, 1, tq), F32), pltpu.VMEM((Q_GROUP, HEAD_DIM, tq), F32)] + xsems,
        compiler_params=_ARB2,
    )(qt8, k2, v2, kt2, ot8, dot8, lse, dkt0, dvt0, *(comm[0] if comm else []))
    return res[0], res[1], res[2], list(res[3:])


def _log_sigmoid(z):
    return jnp.minimum(z, 0.0) - jnp.log(1.0 + jnp.exp(-jnp.abs(z)))


_BDOT_DIMS = {"nn": _NN, "nt": _NT, "tn": _TN}
_BDOT_BWD = {"nn": (("nt", "gb"), ("tn", "ag")), "nt": (("nn", "gb"), ("tn", "ga")), "tn": (("nt", "bg"), ("nn", "ag"))}


def _bdot_raw(a, b, mode):
    return lax.dot_general(a.astype(BF16), b.astype(BF16), _BDOT_DIMS[mode], preferred_element_type=F32)


@functools.partial(jax.custom_vjp, nondiff_argnums=(2,))
def _bdot(a, b, mode):
    return _bdot_raw(a, b, mode)


def _bdot_fwd(a, b, mode):
    return _bdot_raw(a, b, mode), (a.astype(BF16), b.astype(BF16))


def _bdot_bwd(mode, res, g):
    ops = {"a": res[0], "b": res[1], "g": g}
    (ma, oa), (mb, ob) = _BDOT_BWD[mode]
    return _bdot_raw(ops[oa[0]], ops[oa[1]], ma), _bdot_raw(ops[ob[0]], ops[ob[1]], mb)


_bdot.defvjp(_bdot_fwd, _bdot_bwd)


def _tile_tri(rev, rows):
    r_i = lax.broadcasted_iota(jnp.int32, (rows, rows), 0)
    c_i = lax.broadcasted_iota(jnp.int32, (rows, rows), 1)
    same = (r_i // GLA_CHUNK) == (c_i // GLA_CHUNK)
    return same & ((c_i >= r_i) if rev else (c_i <= r_i))


def _tri_dot(rev, x):
    tri = _tile_tri(rev, x.shape[0]).astype(BF16)
    return sum(lax.dot_general(tri, piece, _NN, preferred_element_type=F32) for piece in _split3(x))


@functools.partial(jax.custom_vjp, nondiff_argnums=(1,))
def _chunk_cumsum(x, rev):
    return _tri_dot(rev, x)


_chunk_cumsum.defvjp(lambda x, rev: (_tri_dot(rev, x), None), lambda rev, _, g: (_tri_dot(not rev, g),))


def _gla_tile(q, k, vs, a, w2, b2, state_t, *, rev):
    rows = q.shape[0]
    nch = rows // GLA_CHUNK
    tri = _tile_tri(rev, rows)
    chunk_of_row = lax.broadcasted_iota(jnp.int32, (rows, 1), 0) // GLA_CHUNK
    in_chunk = [(chunk_of_row == c).astype(F32) for c in range(nch)]
    la = _log_sigmoid(_bdot(a, w2, "nn") + b2) * (1.0 / GLA_TAU)
    cum = _chunk_cumsum(la, rev)
    tots = [jnp.sum(la * in_chunk[c], axis=0, keepdims=True) for c in range(nch)]
    tot_rows = sum(in_chunk[c] * tots[c] for c in range(nch))
    q_in = q * (GLA_DK ** -0.5) * jnp.exp(cum)
    k_in = k * jnp.exp(-cum)
    k_st = k * jnp.exp(tot_rows - cum)
    lane = lax.broadcasted_iota(jnp.int32, (1, GLA_QK_W), 1)
    outs = []
    for h in range(GLA_HEADS):
        head = ((lane >= GLA_DK * h) & (lane < GLA_DK * (h + 1))).astype(F32)
        att = jnp.where(tri, _bdot(q_in * head, k_in, "nt"), 0.0)
        outs.append(_bdot(att, vs[h], "nn"))
    o = jnp.concatenate(outs, axis=1)
    hr = lax.broadcasted_iota(jnp.int32, (GLA_V_W, GLA_QK_W), 0) // GLA_DV
    hc = lax.broadcasted_iota(jnp.int32, (GLA_V_W, GLA_QK_W), 1) // GLA_DK
    same_head = (hr == hc).astype(F32)
    v_all = jnp.concatenate(vs, axis=1)
    for c in (range(nch - 1, -1, -1) if rev else range(nch)):
        o = o + _bdot(q_in * in_chunk[c], state_t, "nt")
        state_t = jnp.exp(tots[c]) * state_t + _bdot(v_all, k_st * in_chunk[c], "tn") * same_head
    return o, state_t


def _gla_tile_of(step, rev, nct, nt):
    if not rev:
        return step
    return jnp.where(step < nct, nct - 1 - step, nt - 1 - (step - nct))


def _gla_row_specs(R, tile):
    return [pl.BlockSpec((R, GLA_QK_W), lambda s: (tile(s), C_GQ // GLA_QK_W)),
            pl.BlockSpec((R, GLA_QK_W), lambda s: (tile(s), C_GK // GLA_QK_W)),
            pl.BlockSpec((R, GLA_V_W), lambda s: (tile(s), C_GV // GLA_V_W)),
            pl.BlockSpec((R, 128), lambda s: (tile(s), C_AF // 128))]


def _gla_dir(name, p, w2, b2, *, rev, R, nct, add=None):
    n = p.shape[0]
    nt = n // R
    tile = lambda s: _gla_tile_of(s, rev, nct, nt)

    def body(q_ref, k_ref, v_ref, a_ref, w2_ref, b2_ref, *rest):
        if add is not None:
            add_ref, o_ref, ssave_ref, state = rest
        else:
            o_ref, ssave_ref, state = rest

        @pl.when(pl.program_id(0) == 0)
        def _():
            state[...] = jnp.zeros_like(state)

        vs = [v_ref[:, GLA_DV * h:GLA_DV * (h + 1)] for h in range(GLA_HEADS)]
        s_in = state[...]
        ssave_ref[...] = s_in
        o, s_out = _gla_tile(q_ref[...], k_ref[...], vs, a_ref[...], w2_ref[...], b2_ref[...], s_in, rev=rev)
        if add is not None:
            o = o + add_ref[...]
        o_ref[...] = o
        state[...] = s_out

    o_spec = pl.BlockSpec((R, GLA_V_W), lambda s: (tile(s), 0))
    in_specs = _gla_row_specs(R, tile) + [_full_spec(w2.shape), _full_spec(b2.shape)]
    args = [p, p, p, p, w2, b2]
    if add is not None:
        in_specs.append(o_spec)
        args.append(add)
    return pl.pallas_call(
        body, name=name, grid=(nt,), in_specs=in_specs,
        out_specs=[o_spec, pl.BlockSpec((None, GLA_V_W, GLA_QK_W), lambda s: (tile(s), 0, 0))],
        out_shape=[jax.ShapeDtypeStruct((n, GLA_V_W), F32), jax.ShapeDtypeStruct((nt, GLA_V_W, GLA_QK_W), F32)],
        scratch_shapes=[pltpu.VMEM((GLA_V_W, GLA_QK_W), F32)],
        compiler_params=_ARB1,
    )(*args)


def _gla_dir_bwd(name, p, w2, b2, ssave, do, *, rev, R, nct, adds=None, out_dtype=F32):
    n = p.shape[0]
    nt = n // R
    tile = lambda s: _gla_tile_of(nt - 1 - s, rev, nct, nt)
    widths = (GLA_QK_W, GLA_QK_W, GLA_V_W, 128)

    def body(q_ref, k_ref, v_ref, a_ref, w2_ref, b2_ref, ss_ref, do_ref, *rest):
        if adds is not None:
            add_refs, rest = rest[:4], rest[4:]
        dq_ref, dk_ref, dv_ref, da_ref, dw2_ref, db2_ref, dstate = rest

        @pl.when(pl.program_id(0) == 0)
        def _():
            dstate[...] = jnp.zeros_like(dstate)
            dw2_ref[...] = jnp.zeros_like(dw2_ref)
            db2_ref[...] = jnp.zeros_like(db2_ref)

        vs = [v_ref[:, GLA_DV * h:GLA_DV * (h + 1)] for h in range(GLA_HEADS)]
        _, vjp = jax.vjp(functools.partial(_gla_tile, rev=rev), q_ref[...], k_ref[...], vs, a_ref[...],
                         w2_ref[...], b2_ref[...], ss_ref[...])
        dq, dk, dvs, da, dw2, db2, ds = vjp((do_ref[...], dstate[...]))
        grads = [dq, dk, jnp.concatenate(dvs, axis=1), da]
        if adds is not None:
            grads = [g + r[...].astype(F32) for g, r in zip(grads, add_refs)]
        for ref, g in zip((dq_ref, dk_ref, dv_ref, da_ref), grads):
            ref[...] = g.astype(ref.dtype)
        dw2_ref[...] += dw2
        db2_ref[...] += db2
        dstate[...] = ds

    d_specs = [pl.BlockSpec((R, w), lambda s: (tile(s), 0)) for w in widths]
    in_specs = (_gla_row_specs(R, tile) + [_full_spec(w2.shape), _full_spec(b2.shape),
                pl.BlockSpec((None, GLA_V_W, GLA_QK_W), lambda s: (tile(s), 0, 0)),
                pl.BlockSpec((R, GLA_V_W), lambda s: (tile(s), 0))])
    args = [p, p, p, p, w2, b2, ssave, do]
    if adds is not None:
        in_specs += d_specs
        args += list(adds)
    return pl.pallas_call(
        body, name=name, grid=(nt,), in_specs=in_specs,
        out_specs=d_specs + [_full_spec(w2.shape), _full_spec(b2.shape)],
        out_shape=[jax.ShapeDtypeStruct((n, w), out_dtype) for w in widths]
        + [jax.ShapeDtypeStruct(w2.shape, F32), jax.ShapeDtypeStruct(b2.shape, F32)],
        scratch_shapes=[pltpu.VMEM((GLA_V_W, GLA_QK_W), F32)],
        compiler_params=_ARB1,
    )(*args)


def _final_loss(name, x, target, gf, R, nct):
    n = x.shape[0]

    def body(x_ref, t_ref, g_ref, loss_ref, dx_ref, dg_ref):
        i = pl.program_id(0)

        @pl.when(i == 0)
        def _():
            loss_ref[...] = jnp.zeros_like(loss_ref)
            dg_ref[...] = jnp.zeros_like(dg_ref)

        @pl.when(i < nct)
        def _():
            dx_ref[...] = jnp.zeros_like(dx_ref)

        @pl.when(i >= nct)
        def _():
            y, vjp = jax.vjp(_rms, x_ref[...], g_ref[...])
            err = y - t_ref[...]
            loss_ref[...] += jnp.sum(0.5 * jnp.mean(err * err, axis=-1, keepdims=True))
            dx, dg = vjp(err * (1.0 / D))
            dx_ref[...] = dx
            dg_ref[...] += dg

    return pl.pallas_call(
        body, name=name, grid=(n // R,),
        in_specs=[_row_spec(R, D, 0), pl.BlockSpec((R, D), lambda i: (jnp.maximum(i - nct, 0), 0)), _full_spec((1, D))],
        out_specs=[_full_spec((8, 128)), _row_spec(R, D, 0), _full_spec((1, D))],
        out_shape=[jax.ShapeDtypeStruct((8, 128), F32), jax.ShapeDtypeStruct((n, D), F32), jax.ShapeDtypeStruct((1, D), F32)],
        compiler_params=_ARB1,
    )(x, target, gf)


def _adamw(name, w, m, v, gparts):
    rows, cols = w.shape
    nparts = gparts.shape[0]
    tr = rows
    for cand in range(min(rows, 256), 15, -16):
        if rows % cand == 0:
            tr = cand
            break

    def body(w_ref, m_ref, v_ref, g_ref, go_ref, d_ref, mo_ref, vo_ref):
        g = g_ref[0].astype(F32)
        for k in range(1, nparts):
            g = g + g_ref[k].astype(F32)
        m_new = ADAM_B1 * m_ref[...] + (1.0 - ADAM_B1) * g
        v_new = ADAM_B2 * v_ref[...] + (1.0 - ADAM_B2) * (g * g)
        m_hat = m_new / (1.0 - ADAM_B1 ** ADAM_STEP)
        v_hat = v_new / (1.0 - ADAM_B2 ** ADAM_STEP)
        go_ref[...] = g
        d_ref[...] = -ADAM_LR * (m_hat / (jnp.sqrt(v_hat) + ADAM_EPS) + ADAM_WD * w_ref[...])
        mo_ref[...] = m_new
        vo_ref[...] = v_new

    spec = pl.BlockSpec((tr, cols), lambda i: (i, 0))
    return pl.pallas_call(
        body, name=name, grid=(rows // tr,),
        in_specs=[spec, spec, spec, pl.BlockSpec((nparts, tr, cols), lambda i: (0, i, 0))],
        out_specs=[spec] * 4, out_shape=[jax.ShapeDtypeStruct((rows, cols), F32)] * 4,
        compiler_params=_ARB1,
    )(w, m, v, gparts)


def _my_index():
    return 4 * lax.axis_index("x") + 2 * lax.axis_index("y") + lax.axis_index("c")


def _xor_peer(k):
    flip = lambda a, bit: (1 - a) if bit else a
    pos = (flip(lax.axis_index("x"), (k >> 2) & 1), flip(lax.axis_index("y"), (k >> 1) & 1), flip(lax.axis_index("c"), k & 1))
    return pos, 4 * pos[0] + 2 * pos[1] + pos[2]


def _exchange_plan(x_refs, o_refs, send_sems, recv_sems, local_sems, gather):
    npeer = N_DEV - 1
    me = _my_index()
    locals_, sends, recvs = [], [], []
    for a, (x_ref, o_ref) in enumerate(zip(x_refs, o_refs)):
        mine = x_ref if gather else x_ref.at[me]
        locals_.append(pltpu.make_async_copy(mine, o_ref.at[me], local_sems.at[a]))
        for k in range(1, N_DEV):
            pos, lin = _xor_peer(k)
            src = x_ref if gather else x_ref.at[lin]
            sem = a * npeer + k - 1
            sends.append(pltpu.make_async_remote_copy(src_ref=src, dst_ref=o_ref.at[me], send_sem=send_sems.at[sem],
                                                      recv_sem=recv_sems.at[sem], device_id=pos, device_id_type=MESH_ID))
            recvs.append(pltpu.make_async_remote_copy(src_ref=src, dst_ref=o_ref.at[lin], send_sem=send_sems.at[sem],
                                                      recv_sem=recv_sems.at[sem], device_id=pos, device_id_type=MESH_ID))

    def start():
        for cp in locals_ + sends:
            cp.start()

    def wait():
        for cp in recvs:
            cp.wait_recv()
        for cp in sends:
            cp.wait_send()
        for cp in locals_:
            cp.wait()

    return start, wait


def _exchange_io(xs, gather):
    na = len(xs)
    hbm = pl.BlockSpec(memory_space=pltpu.HBM)
    shapes = [jax.ShapeDtypeStruct((N_DEV,) + tuple(x.shape if gather else x.shape[1:]), x.dtype) for x in xs]
    sems = [pltpu.SemaphoreType.DMA((na * (N_DEV - 1),)), pltpu.SemaphoreType.DMA((na * (N_DEV - 1),)),
            pltpu.SemaphoreType.DMA((na,))]
    return [hbm] * na, shapes, sems


def _exchange(name, xs, *, gather):
    na = len(xs)
    specs, shapes, sems = _exchange_io(xs, gather)

    def body(*refs):
        start, wait = _exchange_plan(refs[:na], refs[na:2 * na], *refs[2 * na:], gather)
        start()
        wait()

    return list(pl.pallas_call(body, name=name, in_specs=specs, out_specs=specs, out_shape=shapes, scratch_shapes=sems)(*xs))


def _adaln_fwd(name, craw16, w_ada, b_cols):
    def body(c_ref, w_ref, b_ref, o_ref):
        cs = jax.nn.silu(c_ref[...]).astype(BF16)
        for l in range(2):
            o_ref[l] = lax.dot_general(cs, w_ref[l].astype(BF16), _NN, preferred_element_type=F32) + b_ref[l]

    return pl.pallas_call(
        body, name=name, out_shape=jax.ShapeDtypeStruct((2, 16, w_ada.shape[2]), F32),
        compiler_params=pltpu.CompilerParams(vmem_limit_bytes=V7X_VMEM_LIMIT),
    )(craw16, w_ada, b_cols)


def _adaln_bwd(name, craw16, w_ada, dm):
    def body(c_ref, w_ref, dm_ref, gw_ref, dc_ref):
        c = c_ref[...]
        sg = jax.nn.sigmoid(c)
        cs = c * sg
        row = lax.broadcasted_iota(jnp.int32, (8, 1), 0)
        dc = jnp.zeros((16, D), F32)
        for l in range(2):
            dmx = dm_ref[2 * l + 1]
            dmc = jnp.where(row == 0, jnp.sum(dm_ref[2 * l], axis=0, keepdims=True), 0.0)
            gw_ref[l] = _hdot(cs[0:8], dmx, _TN) + _hdot(cs[8:16], dmc, _TN)
            dc = dc + _hdot(jnp.concatenate([dmx, dmc], axis=0), w_ref[l], _NT)
        dc_ref[...] = dc * sg * (1.0 + c * (1.0 - sg))

    return pl.pallas_call(
        body, name=name,
        out_shape=[jax.ShapeDtypeStruct(w_ada.shape, F32), jax.ShapeDtypeStruct((16, D), F32)],
        compiler_params=pltpu.CompilerParams(vmem_limit_bytes=V7X_VMEM_LIMIT),
    )(craw16, w_ada, dm)


_IN_OFFS = [sum(IN_SPLITS[:k]) for k in range(len(IN_SPLITS) + 1)]
_MY_ORDER = (11, 12, 13, 0, 1, 2, 7, 10, 5, 6, 3, 4, 8, 9)


_IN_SHARD = IN_WIDTH // N_DEV


def _win_my_cols(pieces):
    parts = []
    for k in _MY_ORDER:
        a, b = _IN_OFFS[k], _IN_OFFS[k + 1]
        for s in range(a // _IN_SHARD, (b - 1) // _IN_SHARD + 1):
            lo, hi = max(a, s * _IN_SHARD), min(b, (s + 1) * _IN_SHARD)
            parts.append(pieces[s][:, lo - s * _IN_SHARD:hi - s * _IN_SHARD])
    parts.append(jnp.zeros((pieces[0].shape[0], PW - IN_WIDTH), pieces[0].dtype))
    return jnp.concatenate(parts, axis=1)


def _win_shards(wp):
    my_offs, pos = {}, 0
    for k in _MY_ORDER:
        my_offs[k] = pos
        pos += IN_SPLITS[k]
    shards = []
    for s in range(N_DEV):
        parts = []
        for k in range(len(IN_SPLITS)):
            lo, hi = max(_IN_OFFS[k], s * _IN_SHARD), min(_IN_OFFS[k + 1], (s + 1) * _IN_SHARD)
            if lo < hi:
                parts.append(wp[:, my_offs[k] + lo - _IN_OFFS[k]:my_offs[k] + hi - _IN_OFFS[k]])
        shards.append(jnp.concatenate(parts, axis=1))
    return jnp.stack(shards)


def _split_shards(full, axis):
    c = full.shape[axis] // N_DEV
    return jnp.stack([lax.slice_in_dim(full, s * c, (s + 1) * c, axis=axis) for s in range(N_DEV)])


def _pack_rows(pieces, row_mult):
    rows = jnp.concatenate([p.reshape(-1, 128) for p in pieces], axis=0)
    padn = (-rows.shape[0]) % row_mult
    if padn:
        rows = jnp.concatenate([rows, jnp.zeros((padn, 128), rows.dtype)], axis=0)
    return rows


def _unpack_rows(rows, shapes):
    out, pos = [], 0
    for s in shapes:
        size = 1
        for d in s:
            size *= d
        out.append(rows[pos:pos + size // 128].reshape(tuple(s)))
        pos += size // 128
    return out


def _heads_front(a, nh):
    return a.reshape(a.shape[0], nh, HEAD_DIM).transpose(1, 0, 2)


def _heads_back(a):
    return a.transpose(1, 0, 2).reshape(a.shape[1], a.shape[0] * HEAD_DIM)


def _rope_tables(t, tc):
    tok = jnp.arange(t, dtype=jnp.int32)
    inv_freq = ROPE_THETA ** (-jnp.arange(ROPE_FREQS, dtype=F32) / ROPE_FREQS)
    ang_r = (tok // GRID_W).astype(F32)[:, None] * inv_freq
    ang_c = (tok % GRID_W).astype(F32)[:, None] * inv_freq
    cos64 = jnp.concatenate([jnp.cos(ang_r), jnp.cos(ang_r), jnp.cos(ang_c), jnp.cos(ang_c)], axis=1)
    sin64 = jnp.concatenate([-jnp.sin(ang_r), jnp.sin(ang_r), -jnp.sin(ang_c), jnp.sin(ang_c)], axis=1)
    cos64 = jnp.concatenate([jnp.ones((tc, HEAD_DIM), F32), cos64], axis=0)
    sin64 = jnp.concatenate([jnp.zeros((tc, HEAD_DIM), F32), sin64], axis=0)
    return jnp.tile(cos64, (1, N_Q_HEADS)), jnp.tile(sin64, (1, N_Q_HEADS))


def _head_mean_matrix(width):
    i = jnp.arange(width) // HEAD_DIM
    return (i[:, None] == i[None, :]).astype(F32) / HEAD_DIM


def _head_tile_matrix(width):
    return (jnp.arange(HEAD_DIM)[:, None] == (jnp.arange(width) % HEAD_DIM)[None, :]).astype(F32)


def _heads_t(a, nh):
    return a.T.reshape(nh, HEAD_DIM, a.shape[0])


def _heads_t_back(a):
    return a.reshape(a.shape[0] * HEAD_DIM, a.shape[2]).T


def _attention_fwd(tag, qr, kr, vv, tc, comm):
    qt8, k2, vt2 = _heads_t(qr, N_Q_HEADS), _heads_front(kr, N_KV_HEADS), _heads_t(vv, N_KV_HEADS)
    vt2 = jnp.concatenate([vt2, jnp.ones((N_KV_HEADS, 8, vt2.shape[2]), BF16)], axis=1)
    o_c, lse_c, _ = _attn_fwd(tag + "_attn_ctx", qt8[:, :, :tc], k2[:, :tc], vt2[:, :, :tc])
    o_x, lse_x, comm_out = _attn_fwd(tag + "_attn_lat", qt8[:, :, tc:], k2, vt2, comm)
    att = jnp.concatenate([_heads_t_back(o_c), _heads_t_back(o_x)], axis=0)
    return att, (qr, kr, vv, o_c, o_x, lse_c, lse_x), comm_out


def _attention_bwd(tag, saved, datt, tc, comm):
    qr, kr, vv, o_c, o_x, lse_c, lse_x = saved
    datt = datt.astype(BF16)
    qt8, dot8 = _heads_t(qr, N_Q_HEADS), _heads_t(datt, N_Q_HEADS)
    k2, v2, kt2 = _heads_front(kr, N_KV_HEADS), _heads_front(vv, N_KV_HEADS), _heads_t(kr, N_KV_HEADS)
    zero = jnp.zeros((N_KV_HEADS, HEAD_DIM, tc), F32)
    dq_c, dk_c, dv_c, _ = _attn_bwd(tag + "_attn_b_ctx", qt8[:, :, :tc], k2[:, :tc], v2[:, :tc], kt2[:, :, :tc],
                                    o_c, dot8[:, :, :tc], lse_c, zero, zero)
    dq_x, dkt2, dvt2, comm_out = _attn_bwd(tag + "_attn_b_lat", qt8[:, :, tc:], k2, v2, kt2,
                                           o_x, dot8[:, :, tc:], lse_x, dk_c, dv_c, comm)
    dq = jnp.concatenate([_heads_t_back(dq_c), _heads_t_back(dq_x)], axis=0)
    return dq, _heads_t_back(dkt2), _heads_t_back(dvt2), comm_out


def _layer_fwd(tag, x, w, modv, consts, R, nct, tc, comm, late_weights):
    sh1, sc1, g1, sh2, sc2, g2 = modv
    cosq, sinq, bdq, eq, bdk, ek = consts
    n = x.shape[0]
    (h1,) = _rowwise(tag + "_ln1", _f_lnmod, R, nct, [(x, D, 0, D)], [], [(w["norm1_g"], False)], [], [sh1, sc1], [(D, BF16)])
    p = _matmul(tag + "_in", h1, w["w_in"], "nn", F32)
    qk_rows = [(p, Q_W, C_Q // Q_W, Q_W), (p, KV_W, C_K // KV_W, KV_W)]
    qk_consts = [(cosq, Q_W, 0, Q_W), (sinq, Q_W, 0, Q_W), (cosq, KV_W, 0, KV_W), (sinq, KV_W, 0, KV_W)]
    qk_params = [(w["q_norm_g"], False), (w["k_norm_g"], False)]
    qk_cparams = [(bdq, False), (eq, False), (bdk, False), (ek, False)]
    qr, kr = _rowwise(tag + "_qk", _f_qknorm, R, nct, qk_rows, qk_consts, qk_params, qk_cparams, [],
                      [(Q_W, BF16), (KV_W, BF16)], post=_qk_post)
    vv = p[:, C_VV:C_VV + KV_W].astype(BF16)
    att, att_saved, comm_out = _attention_fwd(tag, qr, kr, vv, tc, comm)
    w.update(late_weights(comm_out))

    o_f, s_f = _gla_dir(tag + "_gla_f", p, w["w2p_f"], w["b2_f"], rev=False, R=R, nct=nct)
    o_fb, s_b = _gla_dir(tag + "_gla_b", p, w["w2p_b"], w["b2_b"], rev=True, R=R, nct=nct, add=o_f)
    go_rows = [(o_fb, GLA_V_W, 0, GLA_DV), (p, GLA_V_W, C_R // GLA_V_W, GLA_DV)]
    (gla,) = _rowwise(tag + "_glaout", _f_glaout, R, nct, go_rows, [], [(w["gla_norm_g"], True)], [], [], [(GLA_V_W, BF16)])

    rg = GMLP_CHUNK
    gm_rows = [(p, GMLP_W, C_U // GMLP_W, GMLP_W), (p, GMLP_W, C_V // GMLP_W, GMLP_W // GMLP_GROUPS)]
    gm_params = [(w["gmlp_norm_g"], True), (w["w_spatial"], True), (w["b_spatial_t"], True)]
    (gm,) = _rowwise(tag + "_gmlp", _f_gmlp, rg, tc // rg, gm_rows, [], gm_params, [], [], [(GMLP_W, BF16)])

    ya = _matmul(tag + "_br_a", gm, w["w_br_a"], "nn", F32)
    yb = _matmul(tag + "_br_b", att, w["w_br_b"], "nn", F32)
    yc = _matmul(tag + "_br_c", gla, w["w_br_c"], "nn", F32)
    mg_rows = [(p, D, C_GA // D, D), (p, D, C_GB // D, D), (p, D, C_GC // D, D), (ya, D, 0, D), (yb, D, 0, D), (yc, D, 0, D)]
    (merged,) = _rowwise(tag + "_merge", _f_merge, R, nct, mg_rows, [], [], [], [], [(D, BF16)])
    mix = _matmul(tag + "_out", merged, w["w_out"], "nn", F32)
    (x_mid,) = _rowwise(tag + "_res1", _f_resid, R, nct, [(x, D, 0, D), (mix, D, 0, D)], [], [], [], [g1], [(D, F32)])

    (h2,) = _rowwise(tag + "_ln2", _f_lnmod, R, nct, [(x_mid, D, 0, D)], [], [(w["norm2_g"], False)], [], [sh2, sc2], [(D, BF16)])
    a2 = _matmul(tag + "_up", h2, w["w_ffn_up"], "nn", F32, o_halves=True)
    gv = _conv_gate(tag + "_conv", a2, w["conv_w_h"], w["conv_b_h"], R, tc)
    ffn = _matmul(tag + "_down", gv, w["w_ffn_down"], "nn", F32)
    (x_next,) = _rowwise(tag + "_res2", _f_resid, R, nct, [(x_mid, D, 0, D), (ffn, D, 0, D)], [], [], [], [g2], [(D, F32)])
    saved = dict(x=x, h1=h1, p=p, att_saved=att_saved, att=att, o_fb=o_fb, s_f=s_f, s_b=s_b, gla=gla, gm=gm,
                 ya=ya, yb=yb, yc=yc, merged=merged, mix=mix, x_mid=x_mid, h2=h2, a2=a2, gv=gv, ffn=ffn,
                 qk=(qk_rows, qk_consts, qk_params, qk_cparams), go_rows=go_rows, gm_info=(gm_rows, gm_params),
                 mg_rows=mg_rows)
    return x_next, saved, comm_out


def _layer_bwd(tag, dx_next, s, w, modv, R, nct, tc, make_comm, make_tail_comm):
    sh1, sc1, g1, sh2, sc2, g2 = modv
    gw = {}
    (dffn,), _, (dg2,) = _rowwise_bwd(tag + "_res2_b", _f_resid, R, nct, [(s["ffn"], D, 0, D), (s["ffn"], D, 0, D)], [], [], [], [g2],
                                      [(dx_next, D)], [None, BF16])
    dgv = _matmul(tag + "_down_da", dffn, w["w_ffn_down"], "nt", F32)
    gw["w_ffn_down"] = _matmul(tag + "_down_dw", s["gv"], dffn, "tn", F32)
    da2, dcw, dcb = _conv_gate_bwd(tag + "_conv_b", s["a2"], dgv, w["conv_w_h"], w["conv_b_h"], R, tc)
    gw["conv_w_h"], gw["conv_b_h"] = dcw, dcb
    dh2 = _matmul(tag + "_up_da", da2, w["w_ffn_up"], "nt", F32, a_halves=True)
    gw["w_ffn_up"] = _matmul(tag + "_up_dw", s["h2"], da2, "tn", F32, b_halves=True)
    (dx_mid,), (gw["norm2_g"],), (dsh2, dsc2) = _rowwise_bwd(
        tag + "_ln2_b", _f_lnmod, R, nct, [(s["x_mid"], D, 0, D)], [], [(w["norm2_g"], False)], [], [sh2, sc2],
        [(dh2, D)], [F32], adds=[dx_next])
    (dmix,), _, (dg1,) = _rowwise_bwd(tag + "_res1_b", _f_resid, R, nct, [(s["mix"], D, 0, D), (s["mix"], D, 0, D)], [], [], [], [g1],
                                      [(dx_mid, D)], [None, BF16])
    dmerged = _matmul(tag + "_out_da", dmix, w["w_out"], "nt", F32)
    gw["w_out"] = _matmul(tag + "_out_dw", s["merged"], dmix, "tn", F32)
    (dga, dgb, dgc, dya, dyb, dyc), _, _ = _rowwise_bwd(tag + "_merge_b", _f_merge, R, nct, s["mg_rows"], [], [], [], [],
                                                        [(dmerged, D)], [BF16] * 6)
    dgm = _matmul(tag + "_br_a_da", dya, w["w_br_a"], "nt", F32)
    datt = _matmul(tag + "_br_b_da", dyb, w["w_br_b"], "nt", F32)
    dgla = _matmul(tag + "_br_c_da", dyc, w["w_br_c"], "nt", F32)
    gm_rows, gm_params = s["gm_info"]
    gw["w_br_a"] = _matmul(tag + "_br_a_dw", s["gm"], dya, "tn", F32)
    gw["w_br_b"] = _matmul(tag + "_br_b_dw", s["att"], dyb, "tn", F32)
    gw["w_br_c"] = _matmul(tag + "_br_c_dw", s["gla"], dyc, "tn", F32)
    rg = GMLP_CHUNK
    (du, dv_), (gw["gmlp_norm_g"], gw["w_spatial"], gw["b_spatial_t"]), _ = _rowwise_bwd(
        tag + "_gmlp_b", _f_gmlp, rg, tc // rg, gm_rows, [], gm_params, [], [], [(dgm, GMLP_W)], [BF16, BF16])
    (do, dr), (gw["gla_norm_g"],), _ = _rowwise_bwd(tag + "_glaout_b", _f_glaout, R, nct, s["go_rows"], [],
                                                    [(w["gla_norm_g"], True)], [], [], [(dgla, GLA_V_W)], [F32, BF16])
    p = s["p"]
    *d_b, gw["w2p_b"], gw["b2_b"] = _gla_dir_bwd(tag + "_gla_b_b", p, w["w2p_b"], w["b2_b"], s["s_b"], do, rev=True, R=R, nct=nct)
    dgq, dgk, dgv_, daf, gw["w2p_f"], gw["b2_f"] = _gla_dir_bwd(tag + "_gla_f_b", p, w["w2p_f"], w["b2_f"], s["s_f"], do,
                                                              rev=False, R=R, nct=nct, adds=d_b, out_dtype=BF16)
    dqr, dkr, dvv, comm_out = _attention_bwd(tag, s["att_saved"], datt, tc, make_comm(gw))
    qk_rows, qk_consts, qk_params, qk_cparams = s["qk"]
    (dq, dk), (gw["q_norm_g"], gw["k_norm_g"]), _ = _rowwise_bwd(
        tag + "_qk_b", _f_qknorm, R, nct, qk_rows, qk_consts, qk_params, qk_cparams, [],
        [(dqr, Q_W), (dkr, KV_W)], [BF16, BF16], pre=_qk_pre)
    dp = jnp.concatenate([dga, dgb, dgc, du, dv_, dq, dgv_, dr, dgq, dgk, dk, dvv.astype(BF16), daf,
                          jnp.zeros((p.shape[0], PW - C_AF - 128), BF16)], axis=1)
    gw["w_in"] = _matmul(tag + "_in_dw", s["h1"], dp, "tn", F32)
    tail = make_tail_comm(gw)
    if tail is None:
        dh1, tail_out = _matmul(tag + "_in_da", dp, w["w_in"], "nt", F32), []
    else:
        dh1, tail_out = _matmul(tag + "_in_da", dp, w["w_in"], "nt", F32, comm=tail)
    (dx,), (gw["norm1_g"],), (dsh1, dsc1) = _rowwise_bwd(
        tag + "_ln1_b", _f_lnmod, R, nct, [(s["x"], D, 0, D)], [], [(w["norm1_g"], False)], [], [sh1, sc1],
        [(dh1, D)], [F32], adds=[dx_mid])
    return dx, gw, (dsh1, dsc1, dg1, dsh2, dsc2, dg2), comm_out, tail_out


_SHARDED = (("w_in", 1, True), ("w_br_a", 1, True), ("w_br_b", 1, True), ("w_br_c", 1, True), ("w_out", 0, True),
            ("w_ffn_up", 1, True), ("w_ffn_down", 0, True), ("conv_w", 1, False), ("w_alpha2", 2, False), ("b_alpha", 1, False))
_REPLICATED = ("c_ctx", "b_ada", "norm1_g", "norm2_g", "q_norm_g", "k_norm_g", "gmlp_norm_g", "w_spatial", "b_spatial",
               "gla_norm_g", "conv_b", "final_norm_g")
_WEIGHTS = ("c_ctx", "w_ada", "b_ada", "norm1_g", "norm2_g", "w_in", "q_norm_g", "k_norm_g", "gmlp_norm_g", "w_spatial",
            "b_spatial", "w_alpha2", "b_alpha", "gla_norm_g", "w_br_a", "w_br_b", "w_br_c", "w_out", "w_ffn_up", "conv_w",
            "conv_b", "w_ffn_down", "final_norm_g")


def _decay_weights(w_alpha2_l, b_alpha_l):
    out = []
    for d in range(2):
        w2p = jnp.zeros((128, GLA_QK_W), F32).at[GLA_RANK * d:GLA_RANK * (d + 1)].set(w_alpha2_l[d])
        out += [w2p, b_alpha_l[d][None, :]]
    return out


def _step(inp, wts, moms, vels):
    x, c, ctx, loss_target = inp
    t, tc = x.shape[1], ctx.shape[1]
    n = t + tc
    R = min(256, tc)
    nct = tc // R
    me = _my_index()
    depth = wts["w_in"].shape[0]

    late = [(nm, ax) for nm, ax, half in _SHARDED if half and nm != "w_in"]
    small = [(nm, ax) for nm, ax, half in _SHARDED if not half]
    w_in_shard = lambda l: wts["w_in"][l].astype(BF16)
    c8 = jnp.concatenate([c, jnp.zeros((7, D), F32)], axis=0)
    first = _exchange("gather_first", [w_in_shard(0)] + [wts[nm] for nm, _ in small] + [c8], gather=True)
    c_all = first[-1][:, 0, :]
    small_all = dict(zip([nm for nm, _ in small], first[1:-1]))

    def early_weights(l, w_in_all):
        w = {"w_in": _win_my_cols([w_in_all[s] for s in range(N_DEV)])}
        conv_w, w_alpha2, b_alpha = [jnp.concatenate([small_all[nm][s, l] for s in range(N_DEV)], axis=ax) for nm, ax in small]
        w["conv_w_h"] = conv_w.reshape(3, 2, FFN_H).transpose(1, 0, 2)
        w["conv_b_h"] = wts["conv_b"][l].reshape(2, 1, FFN_H)
        w["w2p_f"], w["b2_f"], w["w2p_b"], w["b2_b"] = _decay_weights(w_alpha2, b_alpha)
        w["norm1_g"] = wts["norm1_g"][l][None, :]
        w["norm2_g"] = wts["norm2_g"][l][None, :]
        w["q_norm_g"] = wts["q_norm_g"][l][None, :]
        w["k_norm_g"] = wts["k_norm_g"][l][None, :]
        w["gmlp_norm_g"] = wts["gmlp_norm_g"][l].reshape(GMLP_GROUPS, 1, GMLP_W // GMLP_GROUPS)
        w["w_spatial"] = wts["w_spatial"][l]
        w["b_spatial_t"] = wts["b_spatial"][l][:, :, None]
        w["gla_norm_g"] = wts["gla_norm_g"][l].reshape(GLA_HEADS, 1, GLA_DV)
        return w

    craw16 =jnp.concatenate([c_all, wts["c_ctx"][None, :], jnp.zeros((7, D), F32)], axis=0)
    acols = wts["w_ada"].shape[2]
    b_cols = lax.dynamic_slice_in_dim(wts["b_ada"], me * acols, acols, axis=1)[:, None, :]
    mod_part = _adaln_fwd("adaln", craw16, wts["w_ada"], b_cols)
    send = jnp.stack([mod_part[:, 8, :][None].repeat(N_DEV, 0), mod_part[:, :8, :].transpose(1, 0, 2)], axis=2)
    send = jnp.concatenate([send.reshape(N_DEV, 2 * depth, acols), jnp.zeros((N_DEV, 8 - 2 * depth, acols), F32)], axis=1)
    (got,) = _exchange("scatter_mod", [send], gather=False)
    mod = got[:, :2 * depth, :].transpose(1, 0, 2).reshape(depth, 2, N_MOD, 1, D)
    modv = [[mod[l, :, k] for k in range(N_MOD)] for l in range(depth)]

    cosq, sinq = _rope_tables(t, tc)
    consts = (cosq, sinq, _head_mean_matrix(Q_W), _head_tile_matrix(Q_W), _head_mean_matrix(KV_W), _head_tile_matrix(KV_W))
    xs = jnp.concatenate([ctx[0], x[0]], axis=0)
    saved, layers = [], []
    w_in_all = first[0]

    def late_weights(got):
        return {nm: jnp.concatenate([g[s] for s in range(N_DEV)], axis=ax) for (nm, ax), g in zip(late, got)}

    for l in range(depth):
        layers.append(early_weights(l, w_in_all))
        sending = [wts[nm][l].astype(BF16) for nm, _ in late] + ([w_in_shard(l + 1)] if l + 1 < depth else [])
        xs, sv, got = _layer_fwd("l%d" % l, xs, layers[l], modv[l], consts, R, nct, tc, (sending, True), late_weights)
        if l + 1 < depth:
            w_in_all = got[len(late)]
        saved.append(sv)
    loss_blk, dxs, dgf = _final_loss("final", xs, loss_target[0], wts["final_norm_g"][None, :], R, nct)
    loss = lax.psum(loss_blk[0, 0], ("x", "y", "c"))

    grads = [None] * depth
    dmods = [None] * depth
    late_parts = [None] * depth
    w_in_parts = [None] * depth
    w_in_grad_shards = lambda g: _win_shards(g["w_in"]).astype(BF16)

    def small_grad_shards(g):
        full = dict(conv_w=g["conv_w_h"].transpose(1, 0, 2).reshape(3, F2),
                    w_alpha2=jnp.stack([g["w2p_f"][:GLA_RANK], g["w2p_b"][GLA_RANK:2 * GLA_RANK]]),
                    b_alpha=jnp.stack([g["b2_f"][0], g["b2_b"][0]]))
        return [_split_shards(full[nm], ax) for nm, ax in small]

    for l in range(depth - 1, -1, -1):
        def make_comm(gw, l=l):
            sending = [_split_shards(gw[nm], ax).astype(BF16) for nm, ax in late]
            return sending + ([w_in_grad_shards(grads[l + 1])] if l + 1 < depth else []), False

        def make_tail_comm(gw, l=l):
            if l > 0:
                return None
            per_layer = [small_grad_shards(gw if k == 0 else grads[k]) for k in range(depth)]
            return [w_in_grad_shards(gw)] + [jnp.stack([per_layer[k][i] for k in range(depth)], axis=1) for i in range(len(small))], False

        dxs, grads[l], dmods[l], got, tail = _layer_bwd("l%d" % l, dxs, saved[l], layers[l], modv[l], R, nct, tc,
                                                        make_comm, make_tail_comm)
        late_parts[l] = got[:len(late)]
        if l + 1 < depth:
            w_in_parts[l + 1] = got[len(late)]
    w_in_parts[0], small_parts = tail[0], tail[1:]
    grad_x = dxs[tc:][None]

    dmod = jnp.stack([jnp.stack(dmods[l], axis=1) for l in range(depth)])
    dmod = dmod.reshape(depth, 2, N_DEV, acols).transpose(2, 0, 1, 3).reshape(N_DEV, 2 * depth, acols)
    dmod_send = jnp.concatenate([dmod, jnp.zeros((N_DEV, 8 - 2 * depth, acols), F32)], axis=1)
    (dm_got,) = _exchange("scatter_dmod", [dmod_send], gather=False)
    g_w_ada, dc16 = _adaln_bwd("adaln_b", craw16, wts["w_ada"], dm_got[:, :2 * depth].transpose(1, 0, 2))
    db_ada_part = jnp.stack([jnp.stack(dmods[l], axis=1) for l in range(depth)]).reshape(depth, 2, N_MOD * D).sum(axis=1)

    out = {}
    kinds = ("grad", "delta", "new_m", "new_v")
    view2 = lambda a: a.reshape(-1, a.shape[-1])
    sharded_parts = ([jnp.stack(w_in_parts, axis=1)]
                     + [jnp.stack([late_parts[l][k] for l in range(depth)], axis=1) for k in range(len(late))] + small_parts)
    for (nm, _), parts in zip([("w_in", 1)] + late + small, sharded_parts):
        res = _adamw("adamw_" + nm, view2(wts[nm]), view2(moms[nm]), view2(vels[nm]), parts.reshape(N_DEV, -1, parts.shape[-1]))
        for kind, flat in zip(kinds, res):
            out[kind, nm] = flat.reshape(wts[nm].shape)

    rep_g = dict(
        c_ctx=dc16[8], b_ada=db_ada_part, final_norm_g=dgf[0],
        norm1_g=jnp.stack([grads[l]["norm1_g"][0] for l in range(depth)]),
        norm2_g=jnp.stack([grads[l]["norm2_g"][0] for l in range(depth)]),
        q_norm_g=jnp.stack([grads[l]["q_norm_g"][0] for l in range(depth)]),
        k_norm_g=jnp.stack([grads[l]["k_norm_g"][0] for l in range(depth)]),
        gmlp_norm_g=jnp.stack([grads[l]["gmlp_norm_g"].reshape(GMLP_W) for l in range(depth)]),
        w_spatial=jnp.stack([grads[l]["w_spatial"] for l in range(depth)]),
        b_spatial=jnp.stack([grads[l]["b_spatial_t"][:, :, 0] for l in range(depth)]),
        gla_norm_g=jnp.stack([grads[l]["gla_norm_g"].reshape(GLA_V_W) for l in range(depth)]),
        conv_b=jnp.stack([grads[l]["conv_b_h"].reshape(F2) for l in range(depth)]),
    )
    rep_shapes = [wts[nm].shape for nm in _REPLICATED]
    (rg_parts,) = _exchange("gather_rep_grads", [_pack_rows([rep_g[nm] for nm in _REPLICATED], 16)], gather=True)
    rpk = lambda src: _pack_rows([src[nm] for nm in _REPLICATED], 16)
    res = _adamw("adamw_rep", rpk(wts), rpk(moms), rpk(vels), rg_parts)
    for kind, rows in zip(kinds, res):
        for nm, piece in zip(_REPLICATED, _unpack_rows(rows, rep_shapes)):
            out[kind, nm] = piece

    res = _adamw("adamw_ada", view2(wts["w_ada"]), view2(moms["w_ada"]), view2(vels["w_ada"]), view2(g_w_ada)[None])
    for kind, flat in zip(kinds, res):
        out[kind, "w_ada"] = flat.reshape(wts["w_ada"].shape)

    return (loss, grad_x, *[out[kind, nm] for kind in kinds for nm in _WEIGHTS])


def kernel(x, c, ctx, c_ctx, w_ada, b_ada, norm1_g, norm2_g, w_in, q_norm_g, k_norm_g, gmlp_norm_g, w_spatial, b_spatial, w_alpha2, b_alpha, gla_norm_g, w_br_a, w_br_b, w_br_c, w_out, w_ffn_up, conv_w, conv_b, w_ffn_down, final_norm_g, loss_target, m_c_ctx, m_w_ada, m_b_ada, m_norm1_g, m_norm2_g, m_w_in, m_q_norm_g, m_k_norm_g, m_gmlp_norm_g, m_w_spatial, m_b_spatial, m_w_alpha2, m_b_alpha, m_gla_norm_g, m_w_br_a, m_w_br_b, m_w_br_c, m_w_out, m_w_ffn_up, m_conv_w, m_conv_b, m_w_ffn_down, m_final_norm_g, v_c_ctx, v_w_ada, v_b_ada, v_norm1_g, v_norm2_g, v_w_in, v_q_norm_g, v_k_norm_g, v_gmlp_norm_g, v_w_spatial, v_b_spatial, v_w_alpha2, v_b_alpha, v_gla_norm_g, v_w_br_a, v_w_br_b, v_w_br_c, v_w_out, v_w_ffn_up, v_conv_w, v_conv_b, v_w_ffn_down, v_final_norm_g):
    wts = dict(zip(_WEIGHTS, (c_ctx, w_ada, b_ada, norm1_g, norm2_g, w_in, q_norm_g, k_norm_g, gmlp_norm_g, w_spatial, b_spatial,
                              w_alpha2, b_alpha, gla_norm_g, w_br_a, w_br_b, w_br_c, w_out, w_ffn_up, conv_w, conv_b, w_ffn_down,
                              final_norm_g)))
    moms = dict(zip(_WEIGHTS, (m_c_ctx, m_w_ada, m_b_ada, m_norm1_g, m_norm2_g, m_w_in, m_q_norm_g, m_k_norm_g, m_gmlp_norm_g,
                               m_w_spatial, m_b_spatial, m_w_alpha2, m_b_alpha, m_gla_norm_g, m_w_br_a, m_w_br_b, m_w_br_c, m_w_out,
                               m_w_ffn_up, m_conv_w, m_conv_b, m_w_ffn_down, m_final_norm_g)))
    vels = dict(zip(_WEIGHTS, (v_c_ctx, v_w_ada, v_b_ada, v_norm1_g, v_norm2_g, v_w_in, v_q_norm_g, v_k_norm_g, v_gmlp_norm_g,
                               v_w_spatial, v_b_spatial, v_w_alpha2, v_b_alpha, v_gla_norm_g, v_w_br_a, v_w_br_b, v_w_br_c, v_w_out,
                               v_w_ffn_up, v_conv_w, v_conv_b, v_w_ffn_down, v_final_norm_g)))
    return _step((x, c, ctx, loss_target), wts, moms, vels)
```

```python
import functools

import jax
import jax.numpy as jnp
from jax import lax
from jax.experimental import pallas as pl
from jax.experimental.pallas import tpu as pltpu

F32 = jnp.float32
BF16 = jnp.bfloat16
HI = lax.Precision.HIGHEST
MESH_ID = pl.DeviceIdType.MESH

N_DEV = 8
EPS = 1e-6
D = 1024
N_MOD = 6
HEAD_DIM = 64
N_Q_HEADS = 8
N_KV_HEADS = 2
Q_GROUP = 4
Q_W = 512
KV_W = 128
GRID_W = 64
ROPE_THETA = 10000.0
ROPE_FREQS = 16
GMLP_CHUNK = 128
GMLP_GROUPS = 4
GMLP_W = 512
GLA_HEADS = 4
GLA_QK_W = 256
GLA_V_W = 512
GLA_DK = 64
GLA_DV = 128
GLA_RANK = 16
GLA_TAU = 16.0
GLA_CHUNK = 64
FFN_H = 2816
F2 = 2 * FFN_H
IN_SPLITS = (512, 512, 512, 128, 128, 256, 256, 512, 16, 16, 512, 1024, 1024, 1024)
IN_WIDTH = sum(IN_SPLITS)

C_GA, C_GB, C_GC = 0, 1024, 2048
C_U, C_V, C_Q, C_GV, C_R = 3072, 3584, 4096, 4608, 5120
C_GQ, C_GK = 5632, 5888
C_K, C_VV, C_AF = 6144, 6272, 6400
PW = 6656

ADAM_LR = 0.001
ADAM_B1 = 0.9
ADAM_B2 = 0.999
ADAM_EPS = 1e-08
ADAM_WD = 0.01
ADAM_STEP = 10

V7X_VMEM_LIMIT = 56 * 1024 * 1024

_ARB1 = pltpu.CompilerParams(dimension_semantics=("arbitrary",), vmem_limit_bytes=V7X_VMEM_LIMIT)


def _pick(dim, prefs):
    for p in prefs:
        if dim % p == 0:
            return p
    return dim


def _hdot(a, b, dims=(((1,), (0,)), ((), ()))):
    return lax.dot_general(a, b, dims, precision=HI, preferred_element_type=F32)


_NT = (((1,), (1,)), ((), ()))
_TN = (((0,), (0,)), ((), ()))
_NN = (((1,), (0,)), ((), ()))


def _matmul(name, a, b, mode, out_dtype, *, a_halves=False, b_halves=False, o_halves=False, comm=None):
    def dims2(x, halves):
        return (x.shape[1], 2 * x.shape[2]) if halves else x.shape

    ar, ac = dims2(a, a_halves)
    br, bc = dims2(b, b_halves)
    if mode == "nn":
        M, K, N = ar, ac, bc
    elif mode == "nt":
        M, K, N = ar, ac, br
    else:
        M, K, N = ac, ar, bc
    row_prefs = (768, 512, 384, 256, 128)
    n_unit = N // 2 if (o_halves or (b_halves and mode != "nt")) else N
    k_unit = K // 2 if (a_halves and mode != "tn") else K
    if mode == "tn":
        tm = _pick(M, (1024, 1408, 512, 256, 128))
        tk = _pick(K, (1408,) + row_prefs)
    else:
        tm = _pick(M, row_prefs)
        tk = _pick(k_unit, (1664, 1408, 1024, 512, 256, 128))
    tn = _pick(n_unit, (1664, 1408, 1024, 512, 256, 128))
    nk = K // tk

    def spec(shape2, halves, blk, imap):
        if not halves:
            return pl.BlockSpec(blk, imap)
        nhalf = (shape2[1] // 2) // blk[1]

        def im(i, j, k):
            r, c = imap(i, j, k)
            return (c // nhalf, r, c % nhalf)
        return pl.BlockSpec((None,) + blk, im)

    if mode == "nn":
        a_spec = spec((ar, ac), a_halves, (tm, tk), lambda i, j, k: (i, k))
        b_spec = spec((br, bc), b_halves, (tk, tn), lambda i, j, k: (k, j))
        dn = _NN
    elif mode == "nt":
        a_spec = spec((ar, ac), a_halves, (tm, tk), lambda i, j, k: (i, k))
        b_spec = spec((br, bc), b_halves, (tn, tk), lambda i, j, k: (j, k))
        dn = _NT
    else:
        a_spec = spec((ar, ac), a_halves, (tk, tm), lambda i, j, k: (k, i))
        b_spec = spec((br, bc), b_halves, (tk, tn), lambda i, j, k: (k, j))
        dn = _TN
    o_spec = spec((M, N), o_halves, (tm, tn), lambda i, j, k: (i, j))
    o_shape = (2, M, N // 2) if o_halves else (M, N)

    def body(a_ref, b_ref, o_ref, acc_ref):
        k = pl.program_id(2)
        part = lax.dot_general(a_ref[...], b_ref[...], dn, preferred_element_type=F32)
        if nk == 1:
            o_ref[...] = part.astype(o_ref.dtype)
        else:
            @pl.when(k == 0)
            def _():
                acc_ref[...] = part

            @pl.when(k > 0)
            def _():
                acc_ref[...] += part

            @pl.when(k == nk - 1)
            def _():
                o_ref[...] = acc_ref[...].astype(o_ref.dtype)

    grid = (M // tm, N // tn, nk)
    body, xspecs, xshapes, xsems = _with_exchange(body, 2, 1, grid, comm)
    res = pl.pallas_call(
        body, name=name, grid=grid,
        in_specs=[a_spec, b_spec] + xspecs, out_specs=[o_spec] + xspecs,
        out_shape=[jax.ShapeDtypeStruct(o_shape, out_dtype)] + xshapes,
        scratch_shapes=[pltpu.VMEM((tm, tn), F32)] + xsems,
        compiler_params=pltpu.CompilerParams(dimension_semantics=("arbitrary", "arbitrary", "arbitrary"),
                                             vmem_limit_bytes=V7X_VMEM_LIMIT),
    )(a, b, *(comm[0] if comm else []))
    return res[0] if comm is None else (res[0], list(res[1:]))


def _full_spec(shape):
    nd = len(shape)
    return pl.BlockSpec(tuple(shape), lambda i, _nd=nd: (0,) * _nd)


def _row_spec(R, W, cb):
    return pl.BlockSpec((R, W), lambda i, _cb=cb: (i, _cb))


def _load_rows(refs, specs):
    vals = []
    for ref, (_, W, _, pw) in zip(refs, specs):
        if pw == W:
            vals.append(ref[...].astype(F32))
        else:
            vals.append([ref[:, k * pw:(k + 1) * pw].astype(F32) for k in range(W // pw)])
    return vals


def _load_params(refs, specs):
    vals = []
    for ref, (arr, split) in zip(refs, specs):
        if split:
            vals.append([ref[k] for k in range(arr.shape[0])])
        else:
            vals.append(ref[...])
    return vals


def _mod_spec(nct, width):
    return pl.BlockSpec((None, 1, width), lambda i: (jnp.minimum(i // nct, 1), 0, 0))


def _rowwise(name, f, R, nct, rows, consts, params, cparams, mods, outs, post=None):
    n = rows[0][0].shape[0]
    nr, nc, npar, ncp, nm = len(rows), len(consts), len(params), len(cparams), len(mods)

    def body(*refs):
        pos = 0
        rr = refs[pos:pos + nr]; pos += nr
        cr = refs[pos:pos + nc]; pos += nc
        pr = refs[pos:pos + npar]; pos += npar
        cpr = refs[pos:pos + ncp]; pos += ncp
        mr = refs[pos:pos + nm]; pos += nm
        orefs = refs[pos:]
        res = f(_load_rows(rr, rows), _load_params(pr, params), [m[...] for m in mr],
                _load_rows(cr, consts), _load_params(cpr, cparams))
        if post is not None:
            res = post(res, _load_rows(cr, consts))
        for o_ref, r in zip(orefs, res):
            o_ref[...] = r.astype(o_ref.dtype)

    in_specs = ([_row_spec(R, W, cb) for (_, W, cb, _) in rows + consts]
                + [_full_spec(a.shape) for (a, _) in params + cparams]
                + [_mod_spec(nct, m.shape[2]) for m in mods])
    args = [a for (a, _, _, _) in rows + consts] + [a for (a, _) in params + cparams] + list(mods)
    return pl.pallas_call(
        body, name=name, grid=(n // R,), in_specs=in_specs,
        out_specs=[_row_spec(R, w, 0) for (w, _) in outs],
        out_shape=[jax.ShapeDtypeStruct((n, w), dt) for (w, dt) in outs],
        compiler_params=_ARB1,
    )(*args)


def _rowwise_bwd(name, f, R, nct, rows, consts, params, cparams, mods, douts, drow, adds=None, pre=None):
    n = rows[0][0].shape[0]
    adds = adds or [None] * len(rows)
    nr, nc, npar, ncp, nm, nd = len(rows), len(consts), len(params), len(cparams), len(mods), len(douts)
    add_ix = [k for k in range(nr) if adds[k] is not None]
    out_ix = [k for k in range(nr) if drow[k] is not None]

    def body(*refs):
        i = pl.program_id(0)
        pos = 0
        rr = refs[pos:pos + nr]; pos += nr
        cr = refs[pos:pos + nc]; pos += nc
        pr = refs[pos:pos + npar]; pos += npar
        cpr = refs[pos:pos + ncp]; pos += ncp
        mr = refs[pos:pos + nm]; pos += nm
        dr = refs[pos:pos + nd]; pos += nd
        ar = refs[pos:pos + len(add_ix)]; pos += len(add_ix)
        drr = refs[pos:pos + len(out_ix)]; pos += len(out_ix)
        dpr = refs[pos:pos + npar]; pos += npar
        dmr = refs[pos:pos + nm]; pos += nm

        cv = _load_rows(cr, consts)
        cpv = _load_params(cpr, cparams)
        _, vjp = jax.vjp(lambda rv, pv, mv: f(rv, pv, mv, cv, cpv),
                         _load_rows(rr, rows), _load_params(pr, params), [m[...] for m in mr])
        dv = [d[...].astype(F32) for d in dr]
        if pre is not None:
            dv = pre(dv, cv)
        g_rows, g_params, g_mods = vjp(tuple(dv))

        for ref, k in zip(drr, out_ix):
            _, W, _, pw = rows[k]
            g = g_rows[k]
            extra = ar[add_ix.index(k)] if k in add_ix else None
            if pw == W:
                if extra is not None:
                    g = g + extra[...].astype(F32)
                ref[...] = g.astype(ref.dtype)
            else:
                for q in range(W // pw):
                    gq = g[q]
                    if extra is not None:
                        gq = gq + extra[:, q * pw:(q + 1) * pw].astype(F32)
                    ref[:, q * pw:(q + 1) * pw] = gq.astype(ref.dtype)

        @pl.when(i == 0)
        def _():
            for ref in dpr:
                ref[...] = jnp.zeros_like(ref)

        for ref, (arr, split), g in zip(dpr, params, g_params):
            if split:
                for k in range(arr.shape[0]):
                    ref[k] += g[k]
            else:
                ref[...] += g

        @pl.when((i == 0) | (i == nct))
        def _():
            for ref in dmr:
                ref[...] = jnp.zeros_like(ref)

        for ref, g in zip(dmr, g_mods):
            ref[...] += g

    in_specs = ([_row_spec(R, W, cb) for (_, W, cb, _) in rows + consts]
                + [_full_spec(a.shape) for (a, _) in params + cparams]
                + [_mod_spec(nct, m.shape[2]) for m in mods]
                + [_row_spec(R, W, 0) for (_, W) in douts]
                + [_row_spec(R, rows[k][1], 0) for k in add_ix])
    args = ([a for (a, _, _, _) in rows + consts] + [a for (a, _) in params + cparams] + list(mods)
            + [a for (a, _) in douts] + [adds[k] for k in add_ix])
    out_specs = ([_row_spec(R, rows[k][1], 0) for k in out_ix]
                 + [_full_spec(a.shape) for (a, _) in params]
                 + [_mod_spec(nct, m.shape[2]) for m in mods])
    out_shape = ([jax.ShapeDtypeStruct((n, rows[k][1]), drow[k]) for k in out_ix]
                 + [jax.ShapeDtypeStruct(a.shape, F32) for (a, _) in params]
                 + [jax.ShapeDtypeStruct(m.shape, F32) for m in mods])
    res = pl.pallas_call(
        body, name=name, grid=(n // R,), in_specs=in_specs, out_specs=out_specs, out_shape=out_shape,
        compiler_params=_ARB1,
    )(*args)
    no = len(out_ix)
    return list(res[:no]), list(res[no:no + npar]), list(res[no + npar:])


def _rms(x, g):
    return x * lax.rsqrt(jnp.mean(x * x, axis=-1, keepdims=True) + EPS) * g


def _f_lnmod(rv, pv, mv, cv, cpv):
    (x,), (g,), (shift, scale) = rv, pv, mv
    return (_rms(x, g) * (1.0 + scale) + shift,)


def _f_resid(rv, pv, mv, cv, cpv):
    (x, y), (gate,) = rv, mv
    return (x + gate * y,)


def _f_merge(rv, pv, mv, cv, cpv):
    ga, gb, gc, ya, yb, yc = rv
    return (_sigmoid(ga) * ya + _sigmoid(gb) * yb + _sigmoid(gc) * yc,)


def _split3(x):
    hi = x.astype(BF16)
    rest = x - hi.astype(F32)
    mid = rest.astype(BF16)
    return hi, mid, (rest - mid.astype(F32)).astype(BF16)


def _dot3_right(x, m):
    mb = m.astype(BF16)
    return sum(lax.dot_general(piece, mb, _NN, preferred_element_type=F32) for piece in _split3(x))


@jax.custom_vjp
def _sym_dot(x, m):
    return _dot3_right(x, m)


_sym_dot.defvjp(lambda x, m: (_dot3_right(x, m), m), lambda m, g: (_dot3_right(g, m), jnp.zeros_like(m)))


def _f_qknorm(rv, pv, mv, cv, cpv):
    (q, k), (gq, gk), (bdq, eq, bdk, ek) = rv, pv, cpv
    qn = q * lax.rsqrt(_sym_dot(q * q, bdq) + EPS) * _hdot(gq, eq)
    kn = k * lax.rsqrt(_sym_dot(k * k, bdk) + EPS) * _hdot(gk, ek)
    return (qn, kn)


def _rope(x, cos, sin):
    w = x.shape[1]
    lane = lax.broadcasted_iota(jnp.int32, x.shape, 1)
    partner = jnp.where((lane & 31) < 16, pltpu.roll(x, w - 16, 1), pltpu.roll(x, 16, 1))
    return x * cos + partner * sin


_LOG2E = 1.4426950408889634
_LN2 = 0.6931471805599453


def _qk_post(res, cv):
    (qn, kn), (cq, sq, ck, sk) = res, cv
    return (_rope(qn, cq, sq) * (HEAD_DIM ** -0.5 * _LOG2E), _rope(kn, ck, sk))


def _qk_pre(dv, cv):
    (dq, dk), (cq, sq, ck, sk) = dv, cv
    return (_rope(dq * (HEAD_DIM ** -0.5), cq, -sq), _rope(dk * _LN2, ck, -sk))


def _f_gmlp(rv, pv, mv, cv, cpv):
    (u, vs), (ng, ws, bt) = rv, pv
    pieces = []
    for g in range(GMLP_GROUPS):
        vn = _rms(jax.nn.gelu(vs[g]), ng[g])
        pieces.append(_bdot(ws[g], vn, "nn") + bt[g])
    return (jax.nn.gelu(u) * jnp.concatenate(pieces, axis=1),)


def _f_glaout(rv, pv, mv, cv, cpv):
    (os_, rs), (gn,) = rv, pv
    pieces = [_rms(os_[h], gn[h]) * (rs[h] * _sigmoid(rs[h])) for h in range(GLA_HEADS)]
    return (jnp.concatenate(pieces, axis=1),)


_CONV_CB = 1408
_CONV_STRIP = 128


def _sigmoid(x):
    return 0.5 * jnp.tanh(0.5 * x) + 0.5


def _halo_keep(i, R, tc, n):
    first, end = i * R, (i + 1) * R
    keep_prev = jnp.where((first == 0) | (first == tc), 0.0, 1.0)
    keep_next = jnp.where((end == tc) | (end == n), 0.0, 1.0)
    return keep_prev, keep_next


def _conv_specs(R, n):
    nb8 = n // 8
    main = pl.BlockSpec((2, R, _CONV_CB), lambda j, i: (0, i, j))
    prev = pl.BlockSpec((2, 8, _CONV_CB), lambda j, i: (0, jnp.maximum(i * (R // 8) - 1, 0), j))
    nxt = pl.BlockSpec((2, 8, _CONV_CB), lambda j, i: (0, jnp.minimum((i + 1) * (R // 8), nb8 - 1), j))
    cw = pl.BlockSpec((2, 3, _CONV_CB), lambda j, i: (0, 0, j))
    cb = pl.BlockSpec((2, 1, _CONV_CB), lambda j, i: (0, 0, j))
    return main, prev, nxt, cw, cb


def _row_before(x):
    return pltpu.roll(x, 1, 0)


def _row_after(x):
    return pltpu.roll(x, x.shape[0] - 1, 0)


def _conv_gate(name, a2, cw, cb, R, tc):
    n = a2.shape[1]
    main, prev, nxt, cws, cbs = _conv_specs(R, n)

    def body(a_ref, p_ref, n_ref, cw_ref, cb_ref, o_ref):
        keep_prev, keep_next = _halo_keep(pl.program_id(1), R, tc, n)

        def strip(c, carry):
            ls = pl.ds(pl.multiple_of(c * _CONV_STRIP, _CONV_STRIP), _CONV_STRIP)
            acts = []
            for h in range(2):
                win = jnp.concatenate([p_ref[h, :, ls] * keep_prev, a_ref[h, :, ls], n_ref[h, :, ls] * keep_next], axis=0)
                acts.append((cb_ref[h, :, ls] + cw_ref[h, 1:2, ls] * win + cw_ref[h, 0:1, ls] * _row_before(win)
                             + cw_ref[h, 2:3, ls] * _row_after(win))[8:8 + R])
            g, v = acts
            o_ref[:, ls] = (g * _sigmoid(g) * v).astype(o_ref.dtype)
            return carry

        lax.fori_loop(0, _CONV_CB // _CONV_STRIP, strip, 0)

    return pl.pallas_call(
        body, name=name, grid=(FFN_H // _CONV_CB, n // R),
        in_specs=[main, prev, nxt, cws, cbs],
        out_specs=pl.BlockSpec((R, _CONV_CB), lambda j, i: (i, j)),
        out_shape=jax.ShapeDtypeStruct((n, FFN_H), BF16),
        compiler_params=pltpu.CompilerParams(dimension_semantics=("arbitrary", "arbitrary"),
                                             vmem_limit_bytes=V7X_VMEM_LIMIT),
    )(a2, a2, a2, cw, cb)


def _conv_gate_bwd(name, a2, dgv, cw, cb, R, tc):
    n = a2.shape[1]
    nb8 = n // 8
    main, prev, nxt, cws, cbs = _conv_specs(R, n)
    d_main = pl.BlockSpec((R, _CONV_CB), lambda j, i: (i, j))
    d_prev = pl.BlockSpec((8, _CONV_CB), lambda j, i: (jnp.maximum(i * (R // 8) - 1, 0), j))
    d_next = pl.BlockSpec((8, _CONV_CB), lambda j, i: (jnp.minimum((i + 1) * (R // 8), nb8 - 1), j))
    mid = slice(8, 8 + R)

    def body(a_ref, p_ref, n_ref, cw_ref, cb_ref, d_ref, dp_ref, dn_ref, da_ref, dcw_ref, dcb_ref):
        i = pl.program_id(1)
        keep_prev, keep_next = _halo_keep(i, R, tc, n)

        @pl.when(i == 0)
        def _():
            dcw_ref[...] = jnp.zeros_like(dcw_ref)
            dcb_ref[...] = jnp.zeros_like(dcb_ref)

        def strip(c, carry):
            ls = pl.ds(pl.multiple_of(c * _CONV_STRIP, _CONV_STRIP), _CONV_STRIP)
            cws_ = [[cw_ref[h, k:k + 1, ls] for k in range(3)] for h in range(2)]
            wins = [jnp.concatenate([p_ref[h, :, ls] * keep_prev, a_ref[h, :, ls], n_ref[h, :, ls] * keep_next], axis=0)
                    for h in range(2)]
            g, v = [cb_ref[h, :, ls] + cws_[h][1] * wins[h] + cws_[h][0] * _row_before(wins[h]) + cws_[h][2] * _row_after(wins[h])
                    for h in range(2)]
            dout = jnp.concatenate([dp_ref[:, ls].astype(F32) * keep_prev, d_ref[:, ls].astype(F32),
                                    dn_ref[:, ls].astype(F32) * keep_next], axis=0)
            sg = _sigmoid(g)
            das = [dout * v * sg * (1.0 + g * (1.0 - sg)), dout * g * sg]
            for h in range(2):
                da_win = das[h]
                da_ref[h, :, ls] = (cws_[h][1] * da_win + cws_[h][0] * _row_after(da_win)
                                    + cws_[h][2] * _row_before(da_win))[mid].astype(da_ref.dtype)
                da = da_win[mid]
                dcw_ref[h, 0:1, ls] += jnp.sum(da * _row_before(wins[h])[mid], axis=0, keepdims=True)
                dcw_ref[h, 1:2, ls] += jnp.sum(da * wins[h][mid], axis=0, keepdims=True)
                dcw_ref[h, 2:3, ls] += jnp.sum(da * _row_after(wins[h])[mid], axis=0, keepdims=True)
                dcb_ref[h, :, ls] += jnp.sum(da, axis=0, keepdims=True)
            return carry

        lax.fori_loop(0, _CONV_CB // _CONV_STRIP, strip, 0)

    return pl.pallas_call(
        body, name=name, grid=(FFN_H // _CONV_CB, n // R),
        in_specs=[main, prev, nxt, cws, cbs, d_main, d_prev, d_next],
        out_specs=[main, cws, cbs],
        out_shape=[jax.ShapeDtypeStruct((2, n, FFN_H), BF16), jax.ShapeDtypeStruct((2, 3, FFN_H), F32),
                   jax.ShapeDtypeStruct((2, 1, FFN_H), F32)],
        compiler_params=pltpu.CompilerParams(dimension_semantics=("arbitrary", "arbitrary"),
                                             vmem_limit_bytes=V7X_VMEM_LIMIT),
    )(a2, a2, a2, cw, cb, dgv, dgv, dgv)


_ARB2 = pltpu.CompilerParams(dimension_semantics=("arbitrary", "arbitrary"), vmem_limit_bytes=V7X_VMEM_LIMIT)


def _with_exchange(body, n_in, n_out, grid, comm):
    if comm is None:
        return body, [], [], []
    xs, gather = comm
    na = len(xs)
    specs, shapes, sems = _exchange_io(xs, gather)

    def wrapped(*refs):
        ins, x_refs = refs[:n_in], refs[n_in:n_in + na]
        outs, o_refs = refs[n_in + na:n_in + na + n_out], refs[n_in + na + n_out:n_in + 2 * na + n_out]
        scratch, sem_refs = refs[n_in + 2 * na + n_out:-3], refs[-3:]
        start, wait = _exchange_plan(x_refs, o_refs, *sem_refs, gather)
        ids = [pl.program_id(d) for d in range(len(grid))]
        first, last = ids[0] == 0, ids[0] == grid[0] - 1
        for d in range(1, len(grid)):
            first, last = first & (ids[d] == 0), last & (ids[d] == grid[d] - 1)

        @pl.when(first)
        def _():
            start()

        body(*ins, *outs, *scratch)

        @pl.when(last)
        def _():
            wait()

    return wrapped, specs, shapes, sems


def _attn_fwd(name, qt8, k2, vt2, comm=None):
    nq, nk = qt8.shape[2], k2.shape[1]
    tq = _pick(nq, (256, 128))
    tk = _pick(nk, (1408, 768, 256, 128))

    va = vt2.shape[1]

    def body(qt_ref, k_ref, vt_ref, ot_ref, lse_ref, m_ref, acc_ref):
        m_ref[...] = jnp.full((Q_GROUP, 1, tq), -1e30, F32)
        acc_ref[...] = jnp.zeros((Q_GROUP, va, tq), F32)

        def step(j, carry):
            sl = pl.ds(pl.multiple_of(j * tk, tk), tk)
            kj = k_ref[sl, :]
            vtj = vt_ref[:, sl]
            sts = [lax.dot_general(kj, qt_ref[h], _NN, preferred_element_type=F32) for h in range(Q_GROUP)]
            m_old = [m_ref[h] for h in range(Q_GROUP)]
            acc_old = [acc_ref[h] for h in range(Q_GROUP)]
            m_new = [jnp.maximum(m_old[h], jnp.max(sts[h], axis=0, keepdims=True)) for h in range(Q_GROUP)]
            pts = [jnp.exp2(sts[h] - m_new[h]).astype(BF16) for h in range(Q_GROUP)]
            pvs = [lax.dot_general(vtj, pts[h], _NN, preferred_element_type=F32) for h in range(Q_GROUP)]
            for h in range(Q_GROUP):
                acc_ref[h] = jnp.exp2(m_old[h] - m_new[h]) * acc_old[h] + pvs[h]
                m_ref[h] = m_new[h]
            return carry

        lax.fori_loop(0, nk // tk, step, 0, unroll=2)
        for h in range(Q_GROUP):
            l = acc_ref[h, HEAD_DIM:HEAD_DIM + 1, :]
            ot_ref[h] = (acc_ref[h, 0:HEAD_DIM, :] / l).astype(ot_ref.dtype)
            lse_ref[h] = m_ref[h] + jnp.log2(l)

    qspec = pl.BlockSpec((Q_GROUP, HEAD_DIM, tq), lambda g, i: (g, 0, i))
    lspec = pl.BlockSpec((Q_GROUP, 1, tq), lambda g, i: (g, 0, i))
    grid = (N_KV_HEADS, nq // tq)
    body, xspecs, xshapes, xsems = _with_exchange(body, 3, 2, grid, comm)
    res = pl.pallas_call(
        body, name=name, grid=grid,
        in_specs=[qspec, pl.BlockSpec((None, nk, HEAD_DIM), lambda g, i: (g, 0, 0)),
                  pl.BlockSpec((None, va, nk), lambda g, i: (g, 0, 0))] + xspecs,
        out_specs=[qspec, lspec] + xspecs,
        out_shape=[jax.ShapeDtypeStruct((N_Q_HEADS, HEAD_DIM, nq), BF16), jax.ShapeDtypeStruct((N_Q_HEADS, 1, nq), F32)] + xshapes,
        scratch_shapes=[pltpu.VMEM((Q_GROUP, 1, tq), F32), pltpu.VMEM((Q_GROUP, va, tq), F32)] + xsems,
        compiler_params=_ARB2,
    )(qt8, k2, vt2, *(comm[0] if comm else []))
    return res[0], res[1], list(res[2:])


def _attn_bwd(name, qt8, k2, v2, kt2, ot8, dot8, lse, dkt0, dvt0, comm=None):
    nq, nk = qt8.shape[2], k2.shape[1]
    n0 = dkt0.shape[2]
    tq = _pick(nq, (256, 128))
    tk = _pick(nk, (1408, 768, 256, 128))
    ts = tk
    heads = range(Q_GROUP)

    def body(qt_ref, k_ref, v_ref, kt_ref, ot_ref, dot_ref, lse_ref, dk0_ref, dv0_ref, dqt_ref, dkt_ref, dvt_ref, dl_ref, dq_acc):
        @pl.when(pl.program_id(1) == 0)
        def _():
            dkt_ref[...] = jnp.zeros_like(dkt_ref)
            dvt_ref[...] = jnp.zeros_like(dvt_ref)
            dkt_ref[:, 0:n0] = dk0_ref[...]
            dvt_ref[:, 0:n0] = dv0_ref[...]

        for h in heads:
            dl_ref[h] = jnp.sum(dot_ref[h].astype(F32) * ot_ref[h].astype(F32), axis=0, keepdims=True)
        dq_acc[...] = jnp.zeros((Q_GROUP, HEAD_DIM, tq), F32)

        items = [(s, h) for s in range(tk // ts) for h in heads]

        def step(j, carry):
            def keys(s):
                return pl.ds(pl.multiple_of(j * tk + s * ts, ts), ts)

            def scores(item):
                s, h = item
                return (lax.dot_general(k_ref[keys(s), :], qt_ref[h], _NN, preferred_element_type=F32),
                        lax.dot_general(v_ref[keys(s), :], dot_ref[h], _NN, preferred_element_type=F32))

            nxt = scores(items[0])
            for n, (s, h) in enumerate(items):
                st, dpt = nxt
                if n + 1 < len(items):
                    nxt = scores(items[n + 1])
                pt = jnp.exp2(st - lse_ref[h])
                dst = (pt * (dpt - dl_ref[h])).astype(BF16)
                dq_acc[h] += lax.dot_general(kt_ref[:, keys(s)], dst, _NN, preferred_element_type=F32)
                dv_h = lax.dot_general(dot_ref[h], pt.astype(BF16), _NT, preferred_element_type=F32)
                dk_h = lax.dot_general(qt_ref[h], dst, _NT, preferred_element_type=F32)
                dvt_s, dkt_s = (dv_h, dk_h) if h == 0 else (dvt_s + dv_h, dkt_s + dk_h)
                if h == Q_GROUP - 1:
                    dvt_ref[:, keys(s)] += dvt_s
                    dkt_ref[:, keys(s)] += dkt_s
            return carry

        lax.fori_loop(0, nk // tk, step, 0)
        dqt_ref[...] = dq_acc[...]

    tspec = pl.BlockSpec((Q_GROUP, HEAD_DIM, tq), lambda g, i: (g, 0, i))
    lspec = pl.BlockSpec((Q_GROUP, 1, tq), lambda g, i: (g, 0, i))
    kspec = pl.BlockSpec((None, nk, HEAD_DIM), lambda g, i: (g, 0, 0))
    ktspec = pl.BlockSpec((None, HEAD_DIM, nk), lambda g, i: (g, 0, 0))
    k0spec = pl.BlockSpec((None, HEAD_DIM, n0), lambda g, i: (g, 0, 0))
    grid = (N_KV_HEADS, nq // tq)
    body, xspecs, xshapes, xsems = _with_exchange(body, 9, 3, grid, comm)
    res = pl.pallas_call(
        body, name=name, grid=grid,
        in_specs=[tspec, kspec, kspec, ktspec, tspec, tspec, lspec, k0spec, k0spec] + xspecs,
        out_specs=[tspec, ktspec, ktspec] + xspecs,
        out_shape=[jax.ShapeDtypeStruct((N_Q_HEADS, HEAD_DIM, nq), F32), jax.ShapeDtypeStruct((N_KV_HEADS, HEAD_DIM, nk), F32),
                   jax.ShapeDtypeStruct((N_KV_HEADS, HEAD_DIM, nk), F32)] + xshapes,
        scratch_shapes=[pltpu.VMEM((Q_GROUP, 1, tq), F32), pltpu.VMEM((Q_GROUP, HEAD_DIM, tq), F32)] + xsems,
        compiler_params=_ARB2,
    )(qt8, k2, v2, kt2, ot8, dot8, lse, dkt0, dvt0, *(comm[0] if comm else []))
    return res[0], res[1], res[2], list(res[3:])


def _log_sigmoid(z):
    return jnp.minimum(z, 0.0) - jnp.log(1.0 + jnp.exp(-jnp.abs(z)))


_BDOT_DIMS = {"nn": _NN, "nt": _NT, "tn": _TN}
_BDOT_BWD = {"nn": (("nt", "gb"), ("tn", "ag")), "nt": (("nn", "gb"), ("tn", "ga")), "tn": (("nt", "bg"), ("nn", "ag"))}


def _bdot_raw(a, b, mode):
    return lax.dot_general(a.astype(BF16), b.astype(BF16), _BDOT_DIMS[mode], preferred_element_type=F32)


@functools.partial(jax.custom_vjp, nondiff_argnums=(2,))
def _bdot(a, b, mode):
    return _bdot_raw(a, b, mode)


def _bdot_fwd(a, b, mode):
    return _bdot_raw(a, b, mode), (a.astype(BF16), b.astype(BF16))


def _bdot_bwd(mode, res, g):
    ops = {"a": res[0], "b": res[1], "g": g}
    (ma, oa), (mb, ob) = _BDOT_BWD[mode]
    return _bdot_raw(ops[oa[0]], ops[oa[1]], ma), _bdot_raw(ops[ob[0]], ops[ob[1]], mb)


_bdot.defvjp(_bdot_fwd, _bdot_bwd)


def _tile_tri(rev, rows):
    r_i = lax.broadcasted_iota(jnp.int32, (rows, rows), 0)
    c_i = lax.broadcasted_iota(jnp.int32, (rows, rows), 1)
    same = (r_i // GLA_CHUNK) == (c_i // GLA_CHUNK)
    return same & ((c_i >= r_i) if rev else (c_i <= r_i))


def _tri_dot(rev, x):
    tri = _tile_tri(rev, x.shape[0]).astype(BF16)
    return sum(lax.dot_general(tri, piece, _NN, preferred_element_type=F32) for piece in _split3(x))


@functools.partial(jax.custom_vjp, nondiff_argnums=(1,))
def _chunk_cumsum(x, rev):
    return _tri_dot(rev, x)


_chunk_cumsum.defvjp(lambda x, rev: (_tri_dot(rev, x), None), lambda rev, _, g: (_tri_dot(not rev, g),))


def _gla_tile(q, k, vs, a, w2, b2, state_t, *, rev):
    rows = q.shape[0]
    nch = rows // GLA_CHUNK
    tri = _tile_tri(rev, rows)
    chunk_of_row = lax.broadcasted_iota(jnp.int32, (rows, 1), 0) // GLA_CHUNK
    in_chunk = [(chunk_of_row == c).astype(F32) for c in range(nch)]
    la = _log_sigmoid(_bdot(a, w2, "nn") + b2) * (1.0 / GLA_TAU)
    cum = _chunk_cumsum(la, rev)
    tots = [jnp.sum(la * in_chunk[c], axis=0, keepdims=True) for c in range(nch)]
    tot_rows = sum(in_chunk[c] * tots[c] for c in range(nch))
    q_in = q * (GLA_DK ** -0.5) * jnp.exp(cum)
    k_in = k * jnp.exp(-cum)
    k_st = k * jnp.exp(tot_rows - cum)
    lane = lax.broadcasted_iota(jnp.int32, (1, GLA_QK_W), 1)
    outs = []
    for h in range(GLA_HEADS):
        head = ((lane >= GLA_DK * h) & (lane < GLA_DK * (h + 1))).astype(F32)
        att = jnp.where(tri, _bdot(q_in * head, k_in, "nt"), 0.0)
        outs.append(_bdot(att, vs[h], "nn"))
    o = jnp.concatenate(outs, axis=1)
    hr = lax.broadcasted_iota(jnp.int32, (GLA_V_W, GLA_QK_W), 0) // GLA_DV
    hc = lax.broadcasted_iota(jnp.int32, (GLA_V_W, GLA_QK_W), 1) // GLA_DK
    same_head = (hr == hc).astype(F32)
    v_all = jnp.concatenate(vs, axis=1)
    for c in (range(nch - 1, -1, -1) if rev else range(nch)):
        o = o + _bdot(q_in * in_chunk[c], state_t, "nt")
        state_t = jnp.exp(tots[c]) * state_t + _bdot(v_all, k_st * in_chunk[c], "tn") * same_head
    return o, state_t


def _gla_tile_of(step, rev, nct, nt):
    if not rev:
        return step
    return jnp.where(step < nct, nct - 1 - step, nt - 1 - (step - nct))


def _gla_row_specs(R, tile):
    return [pl.BlockSpec((R, GLA_QK_W), lambda s: (tile(s), C_GQ // GLA_QK_W)),
            pl.BlockSpec((R, GLA_QK_W), lambda s: (tile(s), C_GK // GLA_QK_W)),
            pl.BlockSpec((R, GLA_V_W), lambda s: (tile(s), C_GV // GLA_V_W)),
            pl.BlockSpec((R, 128), lambda s: (tile(s), C_AF // 128))]


def _gla_dir(name, p, w2, b2, *, rev, R, nct, add=None):
    n = p.shape[0]
    nt = n // R
    tile = lambda s: _gla_tile_of(s, rev, nct, nt)

    def body(q_ref, k_ref, v_ref, a_ref, w2_ref, b2_ref, *rest):
        if add is not None:
            add_ref, o_ref, ssave_ref, state = rest
        else:
            o_ref, ssave_ref, state = rest

        @pl.when(pl.program_id(0) == 0)
        def _():
            state[...] = jnp.zeros_like(state)

        vs = [v_ref[:, GLA_DV * h:GLA_DV * (h + 1)] for h in range(GLA_HEADS)]
        s_in = state[...]
        ssave_ref[...] = s_in
        o, s_out = _gla_tile(q_ref[...], k_ref[...], vs, a_ref[...], w2_ref[...], b2_ref[...], s_in, rev=rev)
        if add is not None:
            o = o + add_ref[...]
        o_ref[...] = o
        state[...] = s_out

    o_spec = pl.BlockSpec((R, GLA_V_W), lambda s: (tile(s), 0))
    in_specs = _gla_row_specs(R, tile) + [_full_spec(w2.shape), _full_spec(b2.shape)]
    args = [p, p, p, p, w2, b2]
    if add is not None:
        in_specs.append(o_spec)
        args.append(add)
    return pl.pallas_call(
        body, name=name, grid=(nt,), in_specs=in_specs,
        out_specs=[o_spec, pl.BlockSpec((None, GLA_V_W, GLA_QK_W), lambda s: (tile(s), 0, 0))],
        out_shape=[jax.ShapeDtypeStruct((n, GLA_V_W), F32), jax.ShapeDtypeStruct((nt, GLA_V_W, GLA_QK_W), F32)],
        scratch_shapes=[pltpu.VMEM((GLA_V_W, GLA_QK_W), F32)],
        compiler_params=_ARB1,
    )(*args)


def _gla_dir_bwd(name, p, w2, b2, ssave, do, *, rev, R, nct, adds=None, out_dtype=F32):
    n = p.shape[0]
    nt = n // R
    tile = lambda s: _gla_tile_of(nt - 1 - s, rev, nct, nt)
    widths = (GLA_QK_W, GLA_QK_W, GLA_V_W, 128)

    def body(q_ref, k_ref, v_ref, a_ref, w2_ref, b2_ref, ss_ref, do_ref, *rest):
        if adds is not None:
            add_refs, rest = rest[:4], rest[4:]
        dq_ref, dk_ref, dv_ref, da_ref, dw2_ref, db2_ref, dstate = rest

        @pl.when(pl.program_id(0) == 0)
        def _():
            dstate[...] = jnp.zeros_like(dstate)
            dw2_ref[...] = jnp.zeros_like(dw2_ref)
            db2_ref[...] = jnp.zeros_like(db2_ref)

        vs = [v_ref[:, GLA_DV * h:GLA_DV * (h + 1)] for h in range(GLA_HEADS)]
        _, vjp = jax.vjp(functools.partial(_gla_tile, rev=rev), q_ref[...], k_ref[...], vs, a_ref[...],
                         w2_ref[...], b2_ref[...], ss_ref[...])
        dq, dk, dvs, da, dw2, db2, ds = vjp((do_ref[...], dstate[...]))
        grads = [dq, dk, jnp.concatenate(dvs, axis=1), da]
        if adds is not None:
            grads = [g + r[...].astype(F32) for g, r in zip(grads, add_refs)]
        for ref, g in zip((dq_ref, dk_ref, dv_ref, da_ref), grads):
            ref[...] = g.astype(ref.dtype)
        dw2_ref[...] += dw2
        db2_ref[...] += db2
        dstate[...] = ds

    d_specs = [pl.BlockSpec((R, w), lambda s: (tile(s), 0)) for w in widths]
    in_specs = (_gla_row_specs(R, tile) + [_full_spec(w2.shape), _full_spec(b2.shape),
                pl.BlockSpec((None, GLA_V_W, GLA_QK_W), lambda s: (tile(s), 0, 0)),
                pl.BlockSpec((R, GLA_V_W), lambda s: (tile(s), 0))])
    args = [p, p, p, p, w2, b2, ssave, do]
    if adds is not None:
        in_specs += d_specs
        args += list(adds)
    return pl.pallas_call(
        body, name=name, grid=(nt,), in_specs=in_specs,
        out_specs=d_specs + [_full_spec(w2.shape), _full_spec(b2.shape)],
        out_shape=[jax.ShapeDtypeStruct((n, w), out_dtype) for w in widths]
        + [jax.ShapeDtypeStruct(w2.shape, F32), jax.ShapeDtypeStruct(b2.shape, F32)],
        scratch_shapes=[pltpu.VMEM((GLA_V_W, GLA_QK_W), F32)],
        compiler_params=_ARB1,
    )(*args)


def _final_loss(name, x, target, gf, R, nct):
    n = x.shape[0]

    def body(x_ref, t_ref, g_ref, loss_ref, dx_ref, dg_ref):
        i = pl.program_id(0)

        @pl.when(i == 0)
        def _():
            loss_ref[...] = jnp.zeros_like(loss_ref)
            dg_ref[...] = jnp.zeros_like(dg_ref)

        @pl.when(i < nct)
        def _():
            dx_ref[...] = jnp.zeros_like(dx_ref)

        @pl.when(i >= nct)
        def _():
            y, vjp = jax.vjp(_rms, x_ref[...], g_ref[...])
            err = y - t_ref[...]
            loss_ref[...] += jnp.sum(0.5 * jnp.mean(err * err, axis=-1, keepdims=True))
            dx, dg = vjp(err * (1.0 / D))
            dx_ref[...] = dx
            dg_ref[...] += dg

    return pl.pallas_call(
        body, name=name, grid=(n // R,),
        in_specs=[_row_spec(R, D, 0), pl.BlockSpec((R, D), lambda i: (jnp.maximum(i - nct, 0), 0)), _full_spec((1, D))],
        out_specs=[_full_spec((8, 128)), _row_spec(R, D, 0), _full_spec((1, D))],
        out_shape=[jax.ShapeDtypeStruct((8, 128), F32), jax.ShapeDtypeStruct((n, D), F32), jax.ShapeDtypeStruct((1, D), F32)],
        compiler_params=_ARB1,
    )(x, target, gf)


def _adamw(name, w, m, v, gparts):
    rows, cols = w.shape
    nparts = gparts.shape[0]
    tr = rows
    for cand in range(min(rows, 256), 15, -16):
        if rows % cand == 0:
            tr = cand
            break

    def body(w_ref, m_ref, v_ref, g_ref, go_ref, d_ref, mo_ref, vo_ref):
        g = g_ref[0].astype(F32)
        for k in range(1, nparts):
            g = g + g_ref[k].astype(F32)
        m_new = ADAM_B1 * m_ref[...] + (1.0 - ADAM_B1) * g
        v_new = ADAM_B2 * v_ref[...] + (1.0 - ADAM_B2) * (g * g)
        m_hat = m_new / (1.0 - ADAM_B1 ** ADAM_STEP)
        v_hat = v_new / (1.0 - ADAM_B2 ** ADAM_STEP)
        go_ref[...] = g
        d_ref[...] = -ADAM_LR * (m_hat / (jnp.sqrt(v_hat) + ADAM_EPS) + ADAM_WD * w_ref[...])
        mo_ref[...] = m_new
        vo_ref[...] = v_new

    spec = pl.BlockSpec((tr, cols), lambda i: (i, 0))
    return pl.pallas_call(
        body, name=name, grid=(rows // tr,),
        in_specs=[spec, spec, spec, pl.BlockSpec((nparts, tr, cols), lambda i: (0, i, 0))],
        out_specs=[spec] * 4, out_shape=[jax.ShapeDtypeStruct((rows, cols), F32)] * 4,
        compiler_params=_ARB1,
    )(w, m, v, gparts)


def _my_index():
    return 4 * lax.axis_index("x") + 2 * lax.axis_index("y") + lax.axis_index("c")


def _xor_peer(k):
    flip = lambda a, bit: (1 - a) if bit else a
    pos = (flip(lax.axis_index("x"), (k >> 2) & 1), flip(lax.axis_index("y"), (k >> 1) & 1), flip(lax.axis_index("c"), k & 1))
    return pos, 4 * pos[0] + 2 * pos[1] + pos[2]


def _exchange_plan(x_refs, o_refs, send_sems, recv_sems, local_sems, gather):
    npeer = N_DEV - 1
    me = _my_index()
    locals_, sends, recvs = [], [], []
    for a, (x_ref, o_ref) in enumerate(zip(x_refs, o_refs)):
        mine = x_ref if gather else x_ref.at[me]
        locals_.append(pltpu.make_async_copy(mine, o_ref.at[me], local_sems.at[a]))
        for k in range(1, N_DEV):
            pos, lin = _xor_peer(k)
            src = x_ref if gather else x_ref.at[lin]
            sem = a * npeer + k - 1
            sends.append(pltpu.make_async_remote_copy(src_ref=src, dst_ref=o_ref.at[me], send_sem=send_sems.at[sem],
                                                      recv_sem=recv_sems.at[sem], device_id=pos, device_id_type=MESH_ID))
            recvs.append(pltpu.make_async_remote_copy(src_ref=src, dst_ref=o_ref.at[lin], send_sem=send_sems.at[sem],
                                                      recv_sem=recv_sems.at[sem], device_id=pos, device_id_type=MESH_ID))

    def start():
        for cp in locals_ + sends:
            cp.start()

    def wait():
        for cp in recvs:
            cp.wait_recv()
        for cp in sends:
            cp.wait_send()
        for cp in locals_:
            cp.wait()

    return start, wait


def _exchange_io(xs, gather):
    na = len(xs)
    hbm = pl.BlockSpec(memory_space=pltpu.HBM)
    shapes = [jax.ShapeDtypeStruct((N_DEV,) + tuple(x.shape if gather else x.shape[1:]), x.dtype) for x in xs]
    sems = [pltpu.SemaphoreType.DMA((na * (N_DEV - 1),)), pltpu.SemaphoreType.DMA((na * (N_DEV - 1),)),
            pltpu.SemaphoreType.DMA((na,))]
    return [hbm] * na, shapes, sems


def _exchange(name, xs, *, gather):
    na = len(xs)
    specs, shapes, sems = _exchange_io(xs, gather)

    def body(*refs):
        start, wait = _exchange_plan(refs[:na], refs[na:2 * na], *refs[2 * na:], gather)
        start()
        wait()

    return list(pl.pallas_call(body, name=name, in_specs=specs, out_specs=specs, out_shape=shapes, scratch_shapes=sems)(*xs))


def _adaln_fwd(name, craw16, w_ada, b_cols):
    def body(c_ref, w_ref, b_ref, o_ref):
        cs = jax.nn.silu(c_ref[...]).astype(BF16)
        for l in range(2):
            o_ref[l] = lax.dot_general(cs, w_ref[l].astype(BF16), _NN, preferred_element_type=F32) + b_ref[l]

    return pl.pallas_call(
        body, name=name, out_shape=jax.ShapeDtypeStruct((2, 16, w_ada.shape[2]), F32),
        compiler_params=pltpu.CompilerParams(vmem_limit_bytes=V7X_VMEM_LIMIT),
    )(craw16, w_ada, b_cols)


def _adaln_bwd(name, craw16, w_ada, dm):
    def body(c_ref, w_ref, dm_ref, gw_ref, dc_ref):
        c = c_ref[...]
        sg = jax.nn.sigmoid(c)
        cs = c * sg
        row = lax.broadcasted_iota(jnp.int32, (8, 1), 0)
        dc = jnp.zeros((16, D), F32)
        for l in range(2):
            dmx = dm_ref[2 * l + 1]
            dmc = jnp.where(row == 0, jnp.sum(dm_ref[2 * l], axis=0, keepdims=True), 0.0)
            gw_ref[l] = _hdot(cs[0:8], dmx, _TN) + _hdot(cs[8:16], dmc, _TN)
            dc = dc + _hdot(jnp.concatenate([dmx, dmc], axis=0), w_ref[l], _NT)
        dc_ref[...] = dc * sg * (1.0 + c * (1.0 - sg))

    return pl.pallas_call(
        body, name=name,
        out_shape=[jax.ShapeDtypeStruct(w_ada.shape, F32), jax.ShapeDtypeStruct((16, D), F32)],
        compiler_params=pltpu.CompilerParams(vmem_limit_bytes=V7X_VMEM_LIMIT),
    )(craw16, w_ada, dm)


_IN_OFFS = [sum(IN_SPLITS[:k]) for k in range(len(IN_SPLITS) + 1)]
_MY_ORDER = (11, 12, 13, 0, 1, 2, 7, 10, 5, 6, 3, 4, 8, 9)


_IN_SHARD = IN_WIDTH // N_DEV


def _win_my_cols(pieces):
    parts = []
    for k in _MY_ORDER:
        a, b = _IN_OFFS[k], _IN_OFFS[k + 1]
        for s in range(a // _IN_SHARD, (b - 1) // _IN_SHARD + 1):
            lo, hi = max(a, s * _IN_SHARD), min(b, (s + 1) * _IN_SHARD)
            parts.append(pieces[s][:, lo - s * _IN_SHARD:hi - s * _IN_SHARD])
    parts.append(jnp.zeros((pieces[0].shape[0], PW - IN_WIDTH), pieces[0].dtype))
    return jnp.concatenate(parts, axis=1)


def _win_shards(wp):
    my_offs, pos = {}, 0
    for k in _MY_ORDER:
        my_offs[k] = pos
        pos += IN_SPLITS[k]
    shards = []
    for s in range(N_DEV):
        parts = []
        for k in range(len(IN_SPLITS)):
            lo, hi = max(_IN_OFFS[k], s * _IN_SHARD), min(_IN_OFFS[k + 1], (s + 1) * _IN_SHARD)
            if lo < hi:
                parts.append(wp[:, my_offs[k] + lo - _IN_OFFS[k]:my_offs[k] + hi - _IN_OFFS[k]])
        shards.append(jnp.concatenate(parts, axis=1))
    return jnp.stack(shards)


def _tile_friendly(shape, axis):
    width = shape[axis] // N_DEV if axis == len(shape) - 1 else 128
    return len(shape) >= 2 and width % 128 == 0


def _split_shards(full, axis):
    s = full.shape
    if _tile_friendly(s, axis):
        return jnp.moveaxis(full.reshape(s[:axis] + (N_DEV, s[axis] // N_DEV) + s[axis + 1:]), axis, 0)
    c = s[axis] // N_DEV
    return jnp.stack([lax.slice_in_dim(full, k * c, (k + 1) * c, axis=axis) for k in range(N_DEV)])


def _join_shards(g, axis):
    s = g.shape[1:]
    full_shape = s[:axis] + (N_DEV * s[axis],) + s[axis + 1:]
    if _tile_friendly(full_shape, axis):
        return jnp.moveaxis(g, 0, axis).reshape(full_shape)
    return jnp.concatenate([g[k] for k in range(N_DEV)], axis=axis)


def _pack_rows(pieces, row_mult):
    rows = jnp.concatenate([p.reshape(-1, 128) for p in pieces], axis=0)
    padn = (-rows.shape[0]) % row_mult
    if padn:
        rows = jnp.concatenate([rows, jnp.zeros((padn, 128), rows.dtype)], axis=0)
    return rows


def _unpack_rows(rows, shapes):
    out, pos = [], 0
    for s in shapes:
        size = 1
        for d in s:
            size *= d
        out.append(rows[pos:pos + size // 128].reshape(tuple(s)))
        pos += size // 128
    return out


def _heads_front(a, nh):
    return a.reshape(a.shape[0], nh, HEAD_DIM).transpose(1, 0, 2)


def _heads_back(a):
    return a.transpose(1, 0, 2).reshape(a.shape[1], a.shape[0] * HEAD_DIM)


def _rope_tables(t, tc):
    tok = jnp.arange(t, dtype=jnp.int32)
    inv_freq = ROPE_THETA ** (-jnp.arange(ROPE_FREQS, dtype=F32) / ROPE_FREQS)
    ang_r = (tok // GRID_W).astype(F32)[:, None] * inv_freq
    ang_c = (tok % GRID_W).astype(F32)[:, None] * inv_freq
    cos64 = jnp.concatenate([jnp.cos(ang_r), jnp.cos(ang_r), jnp.cos(ang_c), jnp.cos(ang_c)], axis=1)
    sin64 = jnp.concatenate([-jnp.sin(ang_r), jnp.sin(ang_r), -jnp.sin(ang_c), jnp.sin(ang_c)], axis=1)
    cos64 = jnp.concatenate([jnp.ones((tc, HEAD_DIM), F32), cos64], axis=0)
    sin64 = jnp.concatenate([jnp.zeros((tc, HEAD_DIM), F32), sin64], axis=0)
    return jnp.tile(cos64, (1, N_Q_HEADS)), jnp.tile(sin64, (1, N_Q_HEADS))


def _head_mean_matrix(width):
    i = jnp.arange(width) // HEAD_DIM
    return (i[:, None] == i[None, :]).astype(F32) / HEAD_DIM


def _head_tile_matrix(width):
    return (jnp.arange(HEAD_DIM)[:, None] == (jnp.arange(width) % HEAD_DIM)[None, :]).astype(F32)


def _heads_t(a, nh):
    return a.T.reshape(nh, HEAD_DIM, a.shape[0])


def _heads_t_back(a):
    return a.reshape(a.shape[0] * HEAD_DIM, a.shape[2]).T


def _attention_fwd(tag, qr, kr, vv, tc, comm):
    qt8, k2, vt2 = _heads_t(qr, N_Q_HEADS), _heads_front(kr, N_KV_HEADS), _heads_t(vv, N_KV_HEADS)
    vt2 = jnp.concatenate([vt2, jnp.ones((N_KV_HEADS, 8, vt2.shape[2]), BF16)], axis=1)
    o_c, lse_c, _ = _attn_fwd(tag + "_attn_ctx", qt8[:, :, :tc], k2[:, :tc], vt2[:, :, :tc])
    o_x, lse_x, comm_out = _attn_fwd(tag + "_attn_lat", qt8[:, :, tc:], k2, vt2, comm)
    ot8 = jnp.concatenate([o_c, o_x], axis=2)
    lse = jnp.concatenate([lse_c, lse_x], axis=2)
    return _heads_t_back(ot8), (qr, kr, vv, ot8, lse), comm_out


def _attention_bwd(tag, saved, datt, tc, comm):
    qr, kr, vv, ot8, lse = saved
    datt = datt.astype(BF16)
    qt8, dot8 = _heads_t(qr, N_Q_HEADS), _heads_t(datt, N_Q_HEADS)
    k2, v2, kt2 = _heads_front(kr, N_KV_HEADS), _heads_front(vv, N_KV_HEADS), _heads_t(kr, N_KV_HEADS)
    zero = jnp.zeros((N_KV_HEADS, HEAD_DIM, tc), F32)
    dq_c, dk_c, dv_c, _ = _attn_bwd(tag + "_attn_b_ctx", qt8[:, :, :tc], k2[:, :tc], v2[:, :tc], kt2[:, :, :tc],
                                    ot8[:, :, :tc], dot8[:, :, :tc], lse[:, :, :tc], zero, zero)
    dq_x, dkt2, dvt2, comm_out = _attn_bwd(tag + "_attn_b_lat", qt8[:, :, tc:], k2, v2, kt2,
                                           ot8[:, :, tc:], dot8[:, :, tc:], lse[:, :, tc:], dk_c, dv_c, comm)
    dqt8 = jnp.concatenate([dq_c, dq_x], axis=2)
    return _heads_t_back(dqt8), _heads_t_back(dkt2), _heads_t_back(dvt2), comm_out


def _layer_fwd(tag, x, w, modv, consts, R, nct, tc, comm, late_weights):
    sh1, sc1, g1, sh2, sc2, g2 = modv
    cosq, sinq, bdq, eq, bdk, ek = consts
    n = x.shape[0]
    (h1,) = _rowwise(tag + "_ln1", _f_lnmod, R, nct, [(x, D, 0, D)], [], [(w["norm1_g"], False)], [], [sh1, sc1], [(D, BF16)])
    p = _matmul(tag + "_in", h1, w["w_in"], "nn", F32)
    qk_rows = [(p, Q_W, C_Q // Q_W, Q_W), (p, KV_W, C_K // KV_W, KV_W)]
    qk_consts = [(cosq, Q_W, 0, Q_W), (sinq, Q_W, 0, Q_W), (cosq, KV_W, 0, KV_W), (sinq, KV_W, 0, KV_W)]
    qk_params = [(w["q_norm_g"], False), (w["k_norm_g"], False)]
    qk_cparams = [(bdq, False), (eq, False), (bdk, False), (ek, False)]
    qr, kr = _rowwise(tag + "_qk", _f_qknorm, R, nct, qk_rows, qk_consts, qk_params, qk_cparams, [],
                      [(Q_W, BF16), (KV_W, BF16)], post=_qk_post)
    vv = p[:, C_VV:C_VV + KV_W].astype(BF16)
    att, att_saved, comm_out = _attention_fwd(tag, qr, kr, vv, tc, comm)
    w.update(late_weights(comm_out))

    o_f, s_f = _gla_dir(tag + "_gla_f", p, w["w2p_f"], w["b2_f"], rev=False, R=R, nct=nct)
    o_fb, s_b = _gla_dir(tag + "_gla_b", p, w["w2p_b"], w["b2_b"], rev=True, R=R, nct=nct, add=o_f)
    go_rows = [(o_fb, GLA_V_W, 0, GLA_DV), (p, GLA_V_W, C_R // GLA_V_W, GLA_DV)]
    (gla,) = _rowwise(tag + "_glaout", _f_glaout, R, nct, go_rows, [], [(w["gla_norm_g"], True)], [], [], [(GLA_V_W, BF16)])

    rg = GMLP_CHUNK
    gm_rows = [(p, GMLP_W, C_U // GMLP_W, GMLP_W), (p, GMLP_W, C_V // GMLP_W, GMLP_W // GMLP_GROUPS)]
    gm_params = [(w["gmlp_norm_g"], True), (w["w_spatial"], True), (w["b_spatial_t"], True)]
    (gm,) = _rowwise(tag + "_gmlp", _f_gmlp, rg, tc // rg, gm_rows, [], gm_params, [], [], [(GMLP_W, BF16)])

    ya = _matmul(tag + "_br_a", gm, w["w_br_a"], "nn", F32)
    yb = _matmul(tag + "_br_b", att, w["w_br_b"], "nn", F32)
    yc = _matmul(tag + "_br_c", gla, w["w_br_c"], "nn", F32)
    mg_rows = [(p, D, C_GA // D, D), (p, D, C_GB // D, D), (p, D, C_GC // D, D), (ya, D, 0, D), (yb, D, 0, D), (yc, D, 0, D)]
    (merged,) = _rowwise(tag + "_merge", _f_merge, R, nct, mg_rows, [], [], [], [], [(D, BF16)])
    mix = _matmul(tag + "_out", merged, w["w_out"], "nn", F32)
    (x_mid,) = _rowwise(tag + "_res1", _f_resid, R, nct, [(x, D, 0, D), (mix, D, 0, D)], [], [], [], [g1], [(D, F32)])

    (h2,) = _rowwise(tag + "_ln2", _f_lnmod, R, nct, [(x_mid, D, 0, D)], [], [(w["norm2_g"], False)], [], [sh2, sc2], [(D, BF16)])
    a2 = _matmul(tag + "_up", h2, w["w_ffn_up"], "nn", F32, o_halves=True)
    gv = _conv_gate(tag + "_conv", a2, w["conv_w_h"], w["conv_b_h"], R, tc)
    ffn = _matmul(tag + "_down", gv, w["w_ffn_down"], "nn", F32)
    (x_next,) = _rowwise(tag + "_res2", _f_resid, R, nct, [(x_mid, D, 0, D), (ffn, D, 0, D)], [], [], [], [g2], [(D, F32)])
    saved = dict(x=x, h1=h1, p=p, att_saved=att_saved, att=att, o_fb=o_fb, s_f=s_f, s_b=s_b, gla=gla, gm=gm,
                 ya=ya, yb=yb, yc=yc, merged=merged, mix=mix, x_mid=x_mid, h2=h2, a2=a2, gv=gv, ffn=ffn,
                 qk=(qk_rows, qk_consts, qk_params, qk_cparams), go_rows=go_rows, gm_info=(gm_rows, gm_params),
                 mg_rows=mg_rows)
    return x_next, saved, comm_out


def _layer_bwd(tag, dx_next, s, w, modv, R, nct, tc, make_comm, make_tail_comm):
    sh1, sc1, g1, sh2, sc2, g2 = modv
    gw = {}
    (dffn,), _, (dg2,) = _rowwise_bwd(tag + "_res2_b", _f_resid, R, nct, [(s["ffn"], D, 0, D), (s["ffn"], D, 0, D)], [], [], [], [g2],
                                      [(dx_next, D)], [None, BF16])
    dgv = _matmul(tag + "_down_da", dffn, w["w_ffn_down"], "nt", F32)
    gw["w_ffn_down"] = _matmul(tag + "_down_dw", s["gv"], dffn, "tn", F32)
    da2, dcw, dcb = _conv_gate_bwd(tag + "_conv_b", s["a2"], dgv, w["conv_w_h"], w["conv_b_h"], R, tc)
    gw["conv_w_h"], gw["conv_b_h"] = dcw, dcb
    dh2 = _matmul(tag + "_up_da", da2, w["w_ffn_up"], "nt", F32, a_halves=True)
    gw["w_ffn_up"] = _matmul(tag + "_up_dw", s["h2"], da2, "tn", F32, b_halves=True)
    (dx_mid,), (gw["norm2_g"],), (dsh2, dsc2) = _rowwise_bwd(
        tag + "_ln2_b", _f_lnmod, R, nct, [(s["x_mid"], D, 0, D)], [], [(w["norm2_g"], False)], [], [sh2, sc2],
        [(dh2, D)], [F32], adds=[dx_next])
    (dmix,), _, (dg1,) = _rowwise_bwd(tag + "_res1_b", _f_resid, R, nct, [(s["mix"], D, 0, D), (s["mix"], D, 0, D)], [], [], [], [g1],
                                      [(dx_mid, D)], [None, BF16])
    dmerged = _matmul(tag + "_out_da", dmix, w["w_out"], "nt", F32)
    gw["w_out"] = _matmul(tag + "_out_dw", s["merged"], dmix, "tn", F32)
    (dga, dgb, dgc, dya, dyb, dyc), _, _ = _rowwise_bwd(tag + "_merge_b", _f_merge, R, nct, s["mg_rows"], [], [], [], [],
                                                        [(dmerged, D)], [BF16] * 6)
    dgm = _matmul(tag + "_br_a_da", dya, w["w_br_a"], "nt", F32)
    datt = _matmul(tag + "_br_b_da", dyb, w["w_br_b"], "nt", F32)
    dgla = _matmul(tag + "_br_c_da", dyc, w["w_br_c"], "nt", F32)
    gm_rows, gm_params = s["gm_info"]
    gw["w_br_a"] = _matmul(tag + "_br_a_dw", s["gm"], dya, "tn", F32)
    gw["w_br_b"] = _matmul(tag + "_br_b_dw", s["att"], dyb, "tn", F32)
    gw["w_br_c"] = _matmul(tag + "_br_c_dw", s["gla"], dyc, "tn", F32)
    rg = GMLP_CHUNK
    (du, dv_), (gw["gmlp_norm_g"], gw["w_spatial"], gw["b_spatial_t"]), _ = _rowwise_bwd(
        tag + "_gmlp_b", _f_gmlp, rg, tc // rg, gm_rows, [], gm_params, [], [], [(dgm, GMLP_W)], [BF16, BF16])
    (do, dr), (gw["gla_norm_g"],), _ = _rowwise_bwd(tag + "_glaout_b", _f_glaout, R, nct, s["go_rows"], [],
                                                    [(w["gla_norm_g"], True)], [], [], [(dgla, GLA_V_W)], [F32, BF16])
    p = s["p"]
    *d_b, gw["w2p_b"], gw["b2_b"] = _gla_dir_bwd(tag + "_gla_b_b", p, w["w2p_b"], w["b2_b"], s["s_b"], do, rev=True, R=R, nct=nct)
    dgq, dgk, dgv_, daf, gw["w2p_f"], gw["b2_f"] = _gla_dir_bwd(tag + "_gla_f_b", p, w["w2p_f"], w["b2_f"], s["s_f"], do,
                                                              rev=False, R=R, nct=nct, adds=d_b, out_dtype=BF16)
    dqr, dkr, dvv, comm_out = _attention_bwd(tag, s["att_saved"], datt, tc, make_comm(gw))
    qk_rows, qk_consts, qk_params, qk_cparams = s["qk"]
    (dq, dk), (gw["q_norm_g"], gw["k_norm_g"]), _ = _rowwise_bwd(
        tag + "_qk_b", _f_qknorm, R, nct, qk_rows, qk_consts, qk_params, qk_cparams, [],
        [(dqr, Q_W), (dkr, KV_W)], [BF16, BF16], pre=_qk_pre)
    dp = jnp.concatenate([dga, dgb, dgc, du, dv_, dq, dgv_, dr, dgq, dgk, dk, dvv.astype(BF16), daf,
                          jnp.zeros((p.shape[0], PW - C_AF - 128), BF16)], axis=1)
    gw["w_in"] = _matmul(tag + "_in_dw", s["h1"], dp, "tn", F32)
    tail = make_tail_comm(gw)
    if tail is None:
        dh1, tail_out = _matmul(tag + "_in_da", dp, w["w_in"], "nt", F32), []
    else:
        dh1, tail_out = _matmul(tag + "_in_da", dp, w["w_in"], "nt", F32, comm=tail)
    (dx,), (gw["norm1_g"],), (dsh1, dsc1) = _rowwise_bwd(
        tag + "_ln1_b", _f_lnmod, R, nct, [(s["x"], D, 0, D)], [], [(w["norm1_g"], False)], [], [sh1, sc1],
        [(dh1, D)], [F32], adds=[dx_mid])
    return dx, gw, (dsh1, dsc1, dg1, dsh2, dsc2, dg2), comm_out, tail_out


_SHARDED = (("w_in", 1, True), ("w_br_a", 1, True), ("w_br_b", 1, True), ("w_br_c", 1, True), ("w_out", 0, True),
            ("w_ffn_up", 1, True), ("w_ffn_down", 0, True), ("conv_w", 1, False), ("w_alpha2", 2, False), ("b_alpha", 1, False))
_REPLICATED = ("c_ctx", "b_ada", "norm1_g", "norm2_g", "q_norm_g", "k_norm_g", "gmlp_norm_g", "w_spatial", "b_spatial",
               "gla_norm_g", "conv_b", "final_norm_g")
_WEIGHTS = ("c_ctx", "w_ada", "b_ada", "norm1_g", "norm2_g", "w_in", "q_norm_g", "k_norm_g", "gmlp_norm_g", "w_spatial",
            "b_spatial", "w_alpha2", "b_alpha", "gla_norm_g", "w_br_a", "w_br_b", "w_br_c", "w_out", "w_ffn_up", "conv_w",
            "conv_b", "w_ffn_down", "final_norm_g")


def _decay_weights(w_alpha2_l, b_alpha_l):
    out = []
    for d in range(2):
        w2p = jnp.zeros((128, GLA_QK_W), F32).at[GLA_RANK * d:GLA_RANK * (d + 1)].set(w_alpha2_l[d])
        out += [w2p, b_alpha_l[d][None, :]]
    return out


def _step(inp, wts, moms, vels):
    x, c, ctx, loss_target = inp
    t, tc = x.shape[1], ctx.shape[1]
    n = t + tc
    R = min(256, tc)
    nct = tc // R
    me = _my_index()
    depth = wts["w_in"].shape[0]

    late = [(nm, ax) for nm, ax, half in _SHARDED if half and nm != "w_in"]
    small = [(nm, ax) for nm, ax, half in _SHARDED if not half]
    w_in_shard = lambda l: wts["w_in"][l].astype(BF16)
    c8 = jnp.concatenate([c, jnp.zeros((7, D), F32)], axis=0)
    first = _exchange("gather_first", [w_in_shard(0)] + [wts[nm] for nm, _ in small] + [c8], gather=True)
    c_all = first[-1][:, 0, :]
    small_all = dict(zip([nm for nm, _ in small], first[1:-1]))

    def early_weights(l, w_in_all):
        w = {"w_in": _win_my_cols([w_in_all[s] for s in range(N_DEV)])}
        conv_w, w_alpha2, b_alpha = [jnp.concatenate([small_all[nm][s, l] for s in range(N_DEV)], axis=ax) for nm, ax in small]
        w["conv_w_h"] = conv_w.reshape(3, 2, FFN_H).transpose(1, 0, 2)
        w["conv_b_h"] = wts["conv_b"][l].reshape(2, 1, FFN_H)
        w["w2p_f"], w["b2_f"], w["w2p_b"], w["b2_b"] = _decay_weights(w_alpha2, b_alpha)
        w["norm1_g"] = wts["norm1_g"][l][None, :]
        w["norm2_g"] = wts["norm2_g"][l][None, :]
        w["q_norm_g"] = wts["q_norm_g"][l][None, :]
        w["k_norm_g"] = wts["k_norm_g"][l][None, :]
        w["gmlp_norm_g"] = wts["gmlp_norm_g"][l].reshape(GMLP_GROUPS, 1, GMLP_W // GMLP_GROUPS)
        w["w_spatial"] = wts["w_spatial"][l]
        w["b_spatial_t"] = wts["b_spatial"][l][:, :, None]
        w["gla_norm_g"] = wts["gla_norm_g"][l].reshape(GLA_HEADS, 1, GLA_DV)
        return w

    craw16 =jnp.concatenate([c_all, wts["c_ctx"][None, :], jnp.zeros((7, D), F32)], axis=0)
    acols = wts["w_ada"].shape[2]
    b_cols = lax.dynamic_slice_in_dim(wts["b_ada"], me * acols, acols, axis=1)[:, None, :]
    mod_part = _adaln_fwd("adaln", craw16, wts["w_ada"], b_cols)
    send = jnp.stack([mod_part[:, 8, :][None].repeat(N_DEV, 0), mod_part[:, :8, :].transpose(1, 0, 2)], axis=2)
    send = jnp.concatenate([send.reshape(N_DEV, 2 * depth, acols), jnp.zeros((N_DEV, 8 - 2 * depth, acols), F32)], axis=1)
    (got,) = _exchange("scatter_mod", [send], gather=False)
    mod = got[:, :2 * depth, :].transpose(1, 0, 2).reshape(depth, 2, N_MOD, 1, D)
    modv = [[mod[l, :, k] for k in range(N_MOD)] for l in range(depth)]

    cosq, sinq = _rope_tables(t, tc)
    consts = (cosq, sinq, _head_mean_matrix(Q_W), _head_tile_matrix(Q_W), _head_mean_matrix(KV_W), _head_tile_matrix(KV_W))
    xs = jnp.concatenate([ctx[0], x[0]], axis=0)
    saved, layers = [], []
    w_in_all = first[0]

    def late_weights(got):
        return {nm: _join_shards(g, ax) for (nm, ax), g in zip(late, got)}

    for l in range(depth):
        layers.append(early_weights(l, w_in_all))
        sending = [wts[nm][l].astype(BF16) for nm, _ in late] + ([w_in_shard(l + 1)] if l + 1 < depth else [])
        xs, sv, got = _layer_fwd("l%d" % l, xs, layers[l], modv[l], consts, R, nct, tc, (sending, True), late_weights)
        if l + 1 < depth:
            w_in_all = got[len(late)]
        saved.append(sv)
    loss_blk, dxs, dgf = _final_loss("final", xs, loss_target[0], wts["final_norm_g"][None, :], R, nct)
    loss = lax.psum(loss_blk[0, 0], ("x", "y", "c"))

    grads = [None] * depth
    dmods = [None] * depth
    late_parts = [None] * depth
    w_in_parts = [None] * depth
    w_in_grad_shards = lambda g: _win_shards(g["w_in"]).astype(BF16)

    def small_grad_shards(g):
        full = dict(conv_w=g["conv_w_h"].transpose(1, 0, 2).reshape(3, F2),
                    w_alpha2=jnp.stack([g["w2p_f"][:GLA_RANK], g["w2p_b"][GLA_RANK:2 * GLA_RANK]]),
                    b_alpha=jnp.stack([g["b2_f"][0], g["b2_b"][0]]))
        return [_split_shards(full[nm], ax) for nm, ax in small]

    for l in range(depth - 1, -1, -1):
        def make_comm(gw, l=l):
            sending = [_split_shards(gw[nm], ax).astype(BF16) for nm, ax in late]
            return sending + ([w_in_grad_shards(grads[l + 1])] if l + 1 < depth else []), False

        def make_tail_comm(gw, l=l):
            if l > 0:
                return None
            per_layer = [small_grad_shards(gw if k == 0 else grads[k]) for k in range(depth)]
            return [w_in_grad_shards(gw)] + [jnp.stack([per_layer[k][i] for k in range(depth)], axis=1) for i in range(len(small))], False

        dxs, grads[l], dmods[l], got, tail = _layer_bwd("l%d" % l, dxs, saved[l], layers[l], modv[l], R, nct, tc,
                                                        make_comm, make_tail_comm)
        late_parts[l] = got[:len(late)]
        if l + 1 < depth:
            w_in_parts[l + 1] = got[len(late)]
    w_in_parts[0], small_parts = tail[0], tail[1:]
    grad_x = dxs[tc:][None]

    dmod = jnp.stack([jnp.stack(dmods[l], axis=1) for l in range(depth)])
    dmod = dmod.reshape(depth, 2, N_DEV, acols).transpose(2, 0, 1, 3).reshape(N_DEV, 2 * depth, acols)
    dmod_send = jnp.concatenate([dmod, jnp.zeros((N_DEV, 8 - 2 * depth, acols), F32)], axis=1)
    (dm_got,) = _exchange("scatter_dmod", [dmod_send], gather=False)
    g_w_ada, dc16 = _adaln_bwd("adaln_b", craw16, wts["w_ada"], dm_got[:, :2 * depth].transpose(1, 0, 2))
    db_ada_part = jnp.stack([jnp.stack(dmods[l], axis=1) for l in range(depth)]).reshape(depth, 2, N_MOD * D).sum(axis=1)

    out = {}
    kinds = ("grad", "delta", "new_m", "new_v")
    view2 = lambda a: a.reshape(-1, a.shape[-1])
    sharded_parts = ([jnp.stack(w_in_parts, axis=1)]
                     + [jnp.stack([late_parts[l][k] for l in range(depth)], axis=1) for k in range(len(late))] + small_parts)
    for (nm, _), parts in zip([("w_in", 1)] + late + small, sharded_parts):
        res = _adamw("adamw_" + nm, view2(wts[nm]), view2(moms[nm]), view2(vels[nm]), parts.reshape(N_DEV, -1, parts.shape[-1]))
        for kind, flat in zip(kinds, res):
            out[kind, nm] = flat.reshape(wts[nm].shape)

    rep_g = dict(
        c_ctx=dc16[8], b_ada=db_ada_part, final_norm_g=dgf[0],
        norm1_g=jnp.stack([grads[l]["norm1_g"][0] for l in range(depth)]),
        norm2_g=jnp.stack([grads[l]["norm2_g"][0] for l in range(depth)]),
        q_norm_g=jnp.stack([grads[l]["q_norm_g"][0] for l in range(depth)]),
        k_norm_g=jnp.stack([grads[l]["k_norm_g"][0] for l in range(depth)]),
        gmlp_norm_g=jnp.stack([grads[l]["gmlp_norm_g"].reshape(GMLP_W) for l in range(depth)]),
        w_spatial=jnp.stack([grads[l]["w_spatial"] for l in range(depth)]),
        b_spatial=jnp.stack([grads[l]["b_spatial_t"][:, :, 0] for l in range(depth)]),
        gla_norm_g=jnp.stack([grads[l]["gla_norm_g"].reshape(GLA_V_W) for l in range(depth)]),
        conv_b=jnp.stack([grads[l]["conv_b_h"].reshape(F2) for l in range(depth)]),
    )
    rep_shapes = [wts[nm].shape for nm in _REPLICATED]
    (rg_parts,) = _exchange("gather_rep_grads", [_pack_rows([rep_g[nm] for nm in _REPLICATED], 16)], gather=True)
    rpk = lambda src: _pack_rows([src[nm] for nm in _REPLICATED], 16)
    res = _adamw("adamw_rep", rpk(wts), rpk(moms), rpk(vels), rg_parts)
    for kind, rows in zip(kinds, res):
        for nm, piece in zip(_REPLICATED, _unpack_rows(rows, rep_shapes)):
            out[kind, nm] = piece

    res = _adamw("adamw_ada", view2(wts["w_ada"]), view2(moms["w_ada"]), view2(vels["w_ada"]), view2(g_w_ada)[None])
    for kind, flat in zip(kinds, res):
        out[kind, "w_ada"] = flat.reshape(wts["w_ada"].shape)

    return (loss, grad_x, *[out[kind, nm] for kind in kinds for nm in _WEIGHTS])


def kernel(x, c, ctx, c_ctx, w_ada, b_ada, norm1_g, norm2_g, w_in, q_norm_g, k_norm_g, gmlp_norm_g, w_spatial, b_spatial, w_alpha2, b_alpha, gla_norm_g, w_br_a, w_br_b, w_br_c, w_out, w_ffn_up, conv_w, conv_b, w_ffn_down, final_norm_g, loss_target, m_c_ctx, m_w_ada, m_b_ada, m_norm1_g, m_norm2_g, m_w_in, m_q_norm_g, m_k_norm_g, m_gmlp_norm_g, m_w_spatial, m_b_spatial, m_w_alpha2, m_b_alpha, m_gla_norm_g, m_w_br_a, m_w_br_b, m_w_br_c, m_w_out, m_w_ffn_up, m_conv_w, m_conv_b, m_w_ffn_down, m_final_norm_g, v_c_ctx, v_w_ada, v_b_ada, v_norm1_g, v_norm2_g, v_w_in, v_q_norm_g, v_k_norm_g, v_gmlp_norm_g, v_w_spatial, v_b_spatial, v_w_alpha2, v_b_alpha, v_gla_norm_g, v_w_br_a, v_w_br_b, v_w_br_c, v_w_out, v_w_ffn_up, v_conv_w, v_conv_b, v_w_ffn_down, v_final_norm_g):
    wts = dict(zip(_WEIGHTS, (c_ctx, w_ada, b_ada, norm1_g, norm2_g, w_in, q_norm_g, k_norm_g, gmlp_norm_g, w_spatial, b_spatial,
                              w_alpha2, b_alpha, gla_norm_g, w_br_a, w_br_b, w_br_c, w_out, w_ffn_up, conv_w, conv_b, w_ffn_down,
                              final_norm_g)))
    moms = dict(zip(_WEIGHTS, (m_c_ctx, m_w_ada, m_b_ada, m_norm1_g, m_norm2_g, m_w_in, m_q_norm_g, m_k_norm_g, m_gmlp_norm_g,
                               m_w_spatial, m_b_spatial, m_w_alpha2, m_b_alpha, m_gla_norm_g, m_w_br_a, m_w_br_b, m_w_br_c, m_w_out,
                               m_w_ffn_up, m_conv_w, m_conv_b, m_w_ffn_down, m_final_norm_g)))
    vels = dict(zip(_WEIGHTS, (v_c_ctx, v_w_ada, v_b_ada, v_norm1_g, v_norm2_g, v_w_in, v_q_norm_g, v_k_norm_g, v_gmlp_norm_g,
                               v_w_spatial, v_b_spatial, v_w_alpha2, v_b_alpha, v_gla_norm_g, v_w_br_a, v_w_br_b, v_w_br_c, v_w_out,
                               v_w_ffn_up, v_conv_w, v_conv_b, v_w_ffn_down, v_final_norm_g)))
    return _step((x, c, ctx, loss_target), wts, moms, vels)
```

```python
import functools

import jax
import jax.numpy as jnp
from jax import lax
from jax.experimental import pallas as pl
from jax.experimental.pallas import tpu as pltpu

F32 = jnp.float32
BF16 = jnp.bfloat16
HI = lax.Precision.HIGHEST
MESH_ID = pl.DeviceIdType.MESH

N_DEV = 8
EPS = 1e-6
D = 1024
N_MOD = 6
HEAD_DIM = 64
N_Q_HEADS = 8
N_KV_HEADS = 2
Q_GROUP = 4
Q_W = 512
KV_W = 128
GRID_W = 64
ROPE_THETA = 10000.0
ROPE_FREQS = 16
GMLP_CHUNK = 128
GMLP_GROUPS = 4
GMLP_W = 512
GLA_HEADS = 4
GLA_QK_W = 256
GLA_V_W = 512
GLA_DK = 64
GLA_DV = 128
GLA_RANK = 16
GLA_TAU = 16.0
GLA_CHUNK = 64
FFN_H = 2816
F2 = 2 * FFN_H
IN_SPLITS = (512, 512, 512, 128, 128, 256, 256, 512, 16, 16, 512, 1024, 1024, 1024)
IN_WIDTH = sum(IN_SPLITS)

C_GA, C_GB, C_GC = 0, 1024, 2048
C_U, C_V, C_Q, C_GV, C_R = 3072, 3584, 4096, 4608, 5120
C_GQ, C_GK = 5632, 5888
C_K, C_VV, C_AF = 6144, 6272, 6400
PW = 6656

ADAM_LR = 0.001
ADAM_B1 = 0.9
ADAM_B2 = 0.999
ADAM_EPS = 1e-08
ADAM_WD = 0.01
ADAM_STEP = 10

V7X_VMEM_LIMIT = 56 * 1024 * 1024

_ARB1 = pltpu.CompilerParams(dimension_semantics=("arbitrary",), vmem_limit_bytes=V7X_VMEM_LIMIT)


def _pick(dim, prefs):
    for p in prefs:
        if dim % p == 0:
            return p
    return dim


def _hdot(a, b, dims=(((1,), (0,)), ((), ()))):
    return lax.dot_general(a, b, dims, precision=HI, preferred_element_type=F32)


_NT = (((1,), (1,)), ((), ()))
_TN = (((0,), (0,)), ((), ()))
_NN = (((1,), (0,)), ((), ()))


def _matmul(name, a, b, mode, out_dtype, *, a_halves=False, b_halves=False, o_halves=False, comm=None):
    def dims2(x, halves):
        return (x.shape[1], 2 * x.shape[2]) if halves else x.shape

    ar, ac = dims2(a, a_halves)
    br, bc = dims2(b, b_halves)
    if mode == "nn":
        M, K, N = ar, ac, bc
    elif mode == "nt":
        M, K, N = ar, ac, br
    else:
        M, K, N = ac, ar, bc
    row_prefs = (768, 512, 384, 256, 128)
    n_unit = N // 2 if (o_halves or (b_halves and mode != "nt")) else N
    k_unit = K // 2 if (a_halves and mode != "tn") else K
    if mode == "tn":
        tm = _pick(M, (1024, 1408, 512, 256, 128))
        tk = _pick(K, (1408,) + row_prefs)
    else:
        tm = _pick(M, row_prefs)
        tk = _pick(k_unit, (3328, 2816, 1664, 1408, 1024, 512, 256, 128))
    tn = _pick(n_unit, (1664, 1408, 1024, 512, 256, 128))
    nk = K // tk

    def spec(shape2, halves, blk, imap):
        if not halves:
            return pl.BlockSpec(blk, imap)
        nhalf = (shape2[1] // 2) // blk[1]

        def im(i, j, k):
            r, c = imap(i, j, k)
            return (c // nhalf, r, c % nhalf)
        return pl.BlockSpec((None,) + blk, im)

    if mode == "nn":
        a_spec = spec((ar, ac), a_halves, (tm, tk), lambda i, j, k: (i, k))
        b_spec = spec((br, bc), b_halves, (tk, tn), lambda i, j, k: (k, j))
        dn = _NN
    elif mode == "nt":
        a_spec = spec((ar, ac), a_halves, (tm, tk), lambda i, j, k: (i, k))
        b_spec = spec((br, bc), b_halves, (tn, tk), lambda i, j, k: (j, k))
        dn = _NT
    else:
        a_spec = spec((ar, ac), a_halves, (tk, tm), lambda i, j, k: (k, i))
        b_spec = spec((br, bc), b_halves, (tk, tn), lambda i, j, k: (k, j))
        dn = _TN
    o_spec = spec((M, N), o_halves, (tm, tn), lambda i, j, k: (i, j))
    o_shape = (2, M, N // 2) if o_halves else (M, N)

    def body(a_ref, b_ref, o_ref, acc_ref):
        k = pl.program_id(2)
        part = lax.dot_general(a_ref[...], b_ref[...], dn, preferred_element_type=F32)
        if nk == 1:
            o_ref[...] = part.astype(o_ref.dtype)
        else:
            @pl.when(k == 0)
            def _():
                acc_ref[...] = part

            @pl.when(k > 0)
            def _():
                acc_ref[...] += part

            @pl.when(k == nk - 1)
            def _():
                o_ref[...] = acc_ref[...].astype(o_ref.dtype)

    grid = (M // tm, N // tn, nk)
    body, xspecs, xshapes, xsems = _with_exchange(body, 2, 1, grid, comm)
    res = pl.pallas_call(
        body, name=name, grid=grid,
        in_specs=[a_spec, b_spec] + xspecs, out_specs=[o_spec] + xspecs,
        out_shape=[jax.ShapeDtypeStruct(o_shape, out_dtype)] + xshapes,
        scratch_shapes=[pltpu.VMEM((tm, tn), F32)] + xsems,
        compiler_params=pltpu.CompilerParams(dimension_semantics=("arbitrary", "arbitrary", "arbitrary"),
                                             vmem_limit_bytes=V7X_VMEM_LIMIT),
    )(a, b, *(comm[0] if comm else []))
    return res[0] if comm is None else (res[0], list(res[1:]))


def _full_spec(shape):
    nd = len(shape)
    return pl.BlockSpec(tuple(shape), lambda i, _nd=nd: (0,) * _nd)


def _row_spec(R, W, cb):
    return pl.BlockSpec((R, W), lambda i, _cb=cb: (i, _cb))


def _load_rows(refs, specs):
    vals = []
    for ref, (_, W, _, pw) in zip(refs, specs):
        if pw == W:
            vals.append(ref[...].astype(F32))
        else:
            vals.append([ref[:, k * pw:(k + 1) * pw].astype(F32) for k in range(W // pw)])
    return vals


def _load_params(refs, specs):
    vals = []
    for ref, (arr, split) in zip(refs, specs):
        if split:
            vals.append([ref[k] for k in range(arr.shape[0])])
        else:
            vals.append(ref[...])
    return vals


def _mod_spec(nct, width):
    return pl.BlockSpec((None, 1, width), lambda i: (jnp.minimum(i // nct, 1), 0, 0))


def _rowwise(name, f, R, nct, rows, consts, params, cparams, mods, outs, post=None):
    n = rows[0][0].shape[0]
    nr, nc, npar, ncp, nm = len(rows), len(consts), len(params), len(cparams), len(mods)

    def body(*refs):
        pos = 0
        rr = refs[pos:pos + nr]; pos += nr
        cr = refs[pos:pos + nc]; pos += nc
        pr = refs[pos:pos + npar]; pos += npar
        cpr = refs[pos:pos + ncp]; pos += ncp
        mr = refs[pos:pos + nm]; pos += nm
        orefs = refs[pos:]
        res = f(_load_rows(rr, rows), _load_params(pr, params), [m[...] for m in mr],
                _load_rows(cr, consts), _load_params(cpr, cparams))
        if post is not None:
            res = post(res, _load_rows(cr, consts))
        for o_ref, r in zip(orefs, res):
            o_ref[...] = r.astype(o_ref.dtype)

    in_specs = ([_row_spec(R, W, cb) for (_, W, cb, _) in rows + consts]
                + [_full_spec(a.shape) for (a, _) in params + cparams]
                + [_mod_spec(nct, m.shape[2]) for m in mods])
    args = [a for (a, _, _, _) in rows + consts] + [a for (a, _) in params + cparams] + list(mods)
    return pl.pallas_call(
        body, name=name, grid=(n // R,), in_specs=in_specs,
        out_specs=[_row_spec(R, w, 0) for (w, _) in outs],
        out_shape=[jax.ShapeDtypeStruct((n, w), dt) for (w, dt) in outs],
        compiler_params=_ARB1,
    )(*args)


def _rowwise_bwd(name, f, R, nct, rows, consts, params, cparams, mods, douts, drow, adds=None, pre=None):
    n = rows[0][0].shape[0]
    adds = adds or [None] * len(rows)
    nr, nc, npar, ncp, nm, nd = len(rows), len(consts), len(params), len(cparams), len(mods), len(douts)
    add_ix = [k for k in range(nr) if adds[k] is not None]
    out_ix = [k for k in range(nr) if drow[k] is not None]

    def body(*refs):
        i = pl.program_id(0)
        pos = 0
        rr = refs[pos:pos + nr]; pos += nr
        cr = refs[pos:pos + nc]; pos += nc
        pr = refs[pos:pos + npar]; pos += npar
        cpr = refs[pos:pos + ncp]; pos += ncp
        mr = refs[pos:pos + nm]; pos += nm
        dr = refs[pos:pos + nd]; pos += nd
        ar = refs[pos:pos + len(add_ix)]; pos += len(add_ix)
        drr = refs[pos:pos + len(out_ix)]; pos += len(out_ix)
        dpr = refs[pos:pos + npar]; pos += npar
        dmr = refs[pos:pos + nm]; pos += nm

        cv = _load_rows(cr, consts)
        cpv = _load_params(cpr, cparams)
        _, vjp = jax.vjp(lambda rv, pv, mv: f(rv, pv, mv, cv, cpv),
                         _load_rows(rr, rows), _load_params(pr, params), [m[...] for m in mr])
        dv = [d[...].astype(F32) for d in dr]
        if pre is not None:
            dv = pre(dv, cv)
        g_rows, g_params, g_mods = vjp(tuple(dv))

        for ref, k in zip(drr, out_ix):
            _, W, _, pw = rows[k]
            g = g_rows[k]
            extra = ar[add_ix.index(k)] if k in add_ix else None
            if pw == W:
                if extra is not None:
                    g = g + extra[...].astype(F32)
                ref[...] = g.astype(ref.dtype)
            else:
                for q in range(W // pw):
                    gq = g[q]
                    if extra is not None:
                        gq = gq + extra[:, q * pw:(q + 1) * pw].astype(F32)
                    ref[:, q * pw:(q + 1) * pw] = gq.astype(ref.dtype)

        @pl.when(i == 0)
        def _():
            for ref in dpr:
                ref[...] = jnp.zeros_like(ref)

        for ref, (arr, split), g in zip(dpr, params, g_params):
            if split:
                for k in range(arr.shape[0]):
                    ref[k] += g[k]
            else:
                ref[...] += g

        @pl.when((i == 0) | (i == nct))
        def _():
            for ref in dmr:
                ref[...] = jnp.zeros_like(ref)

        for ref, g in zip(dmr, g_mods):
            ref[...] += g

    in_specs = ([_row_spec(R, W, cb) for (_, W, cb, _) in rows + consts]
                + [_full_spec(a.shape) for (a, _) in params + cparams]
                + [_mod_spec(nct, m.shape[2]) for m in mods]
                + [_row_spec(R, W, 0) for (_, W) in douts]
                + [_row_spec(R, rows[k][1], 0) for k in add_ix])
    args = ([a for (a, _, _, _) in rows + consts] + [a for (a, _) in params + cparams] + list(mods)
            + [a for (a, _) in douts] + [adds[k] for k in add_ix])
    out_specs = ([_row_spec(R, rows[k][1], 0) for k in out_ix]
                 + [_full_spec(a.shape) for (a, _) in params]
                 + [_mod_spec(nct, m.shape[2]) for m in mods])
    out_shape = ([jax.ShapeDtypeStruct((n, rows[k][1]), drow[k]) for k in out_ix]
                 + [jax.ShapeDtypeStruct(a.shape, F32) for (a, _) in params]
                 + [jax.ShapeDtypeStruct(m.shape, F32) for m in mods])
    res = pl.pallas_call(
        body, name=name, grid=(n // R,), in_specs=in_specs, out_specs=out_specs, out_shape=out_shape,
        compiler_params=_ARB1,
    )(*args)
    no = len(out_ix)
    return list(res[:no]), list(res[no:no + npar]), list(res[no + npar:])


def _rms(x, g):
    return x * lax.rsqrt(jnp.mean(x * x, axis=-1, keepdims=True) + EPS) * g


def _f_lnmod(rv, pv, mv, cv, cpv):
    (x,), (g,), (shift, scale) = rv, pv, mv
    return (_rms(x, g) * (1.0 + scale) + shift,)


def _f_resid(rv, pv, mv, cv, cpv):
    (x, y), (gate,) = rv, mv
    return (x + gate * y,)


def _f_merge(rv, pv, mv, cv, cpv):
    ga, gb, gc, ya, yb, yc = rv
    return (_sigmoid(ga) * ya + _sigmoid(gb) * yb + _sigmoid(gc) * yc,)


def _split3(x):
    hi = x.astype(BF16)
    rest = x - hi.astype(F32)
    mid = rest.astype(BF16)
    return hi, mid, (rest - mid.astype(F32)).astype(BF16)


def _dot3_right(x, m):
    mb = m.astype(BF16)
    return sum(lax.dot_general(piece, mb, _NN, preferred_element_type=F32) for piece in _split3(x))


@jax.custom_vjp
def _sym_dot(x, m):
    return _dot3_right(x, m)


_sym_dot.defvjp(lambda x, m: (_dot3_right(x, m), m), lambda m, g: (_dot3_right(g, m), jnp.zeros_like(m)))


def _f_qknorm(rv, pv, mv, cv, cpv):
    (q, k), (gq, gk), (bdq, eq, bdk, ek) = rv, pv, cpv
    qn = q * lax.rsqrt(_sym_dot(q * q, bdq) + EPS) * _hdot(gq, eq)
    kn = k * lax.rsqrt(_sym_dot(k * k, bdk) + EPS) * _hdot(gk, ek)
    return (qn, kn)


def _rope(x, cos, sin):
    w = x.shape[1]
    lane = lax.broadcasted_iota(jnp.int32, x.shape, 1)
    partner = jnp.where((lane & 31) < 16, pltpu.roll(x, w - 16, 1), pltpu.roll(x, 16, 1))
    return x * cos + partner * sin


_LOG2E = 1.4426950408889634
_LN2 = 0.6931471805599453


def _qk_post(res, cv):
    (qn, kn), (cq, sq, ck, sk) = res, cv
    return (_rope(qn, cq, sq) * (HEAD_DIM ** -0.5 * _LOG2E), _rope(kn, ck, sk))


def _qk_pre(dv, cv):
    (dq, dk), (cq, sq, ck, sk) = dv, cv
    return (_rope(dq * (HEAD_DIM ** -0.5), cq, -sq), _rope(dk * _LN2, ck, -sk))


def _f_gmlp(rv, pv, mv, cv, cpv):
    (u, vs), (ng, ws, bt) = rv, pv
    pieces = []
    for g in range(GMLP_GROUPS):
        vn = _rms(jax.nn.gelu(vs[g]), ng[g])
        pieces.append(_bdot(ws[g], vn, "nn") + bt[g])
    return (jax.nn.gelu(u) * jnp.concatenate(pieces, axis=1),)


def _f_glaout(rv, pv, mv, cv, cpv):
    (ofs, obs, rs), (gn,) = rv, pv
    pieces = [_rms(ofs[h] + obs[h], gn[h]) * (rs[h] * _sigmoid(rs[h])) for h in range(GLA_HEADS)]
    return (jnp.concatenate(pieces, axis=1),)


_CONV_CB = 1408
_CONV_STRIP = 128


def _sigmoid(x):
    return 0.5 * jnp.tanh(0.5 * x) + 0.5


def _halo_keep(i, R, tc, n):
    first, end = i * R, (i + 1) * R
    keep_prev = jnp.where((first == 0) | (first == tc), 0.0, 1.0)
    keep_next = jnp.where((end == tc) | (end == n), 0.0, 1.0)
    return keep_prev, keep_next


def _conv_specs(R, n):
    nb8 = n // 8
    main = pl.BlockSpec((2, R, _CONV_CB), lambda j, i: (0, i, j))
    prev = pl.BlockSpec((2, 8, _CONV_CB), lambda j, i: (0, jnp.maximum(i * (R // 8) - 1, 0), j))
    nxt = pl.BlockSpec((2, 8, _CONV_CB), lambda j, i: (0, jnp.minimum((i + 1) * (R // 8), nb8 - 1), j))
    cw = pl.BlockSpec((2, 3, _CONV_CB), lambda j, i: (0, 0, j))
    cb = pl.BlockSpec((2, 1, _CONV_CB), lambda j, i: (0, 0, j))
    return main, prev, nxt, cw, cb


def _row_before(x):
    return pltpu.roll(x, 1, 0)


def _row_after(x):
    return pltpu.roll(x, x.shape[0] - 1, 0)


def _conv_gate(name, a2, cw, cb, R, tc):
    n = a2.shape[1]
    main, prev, nxt, cws, cbs = _conv_specs(R, n)

    def body(a_ref, p_ref, n_ref, cw_ref, cb_ref, o_ref):
        keep_prev, keep_next = _halo_keep(pl.program_id(1), R, tc, n)

        def strip(c, carry):
            ls = pl.ds(pl.multiple_of(c * _CONV_STRIP, _CONV_STRIP), _CONV_STRIP)
            acts = []
            for h in range(2):
                win = jnp.concatenate([p_ref[h, :, ls] * keep_prev, a_ref[h, :, ls], n_ref[h, :, ls] * keep_next], axis=0)
                acts.append((cb_ref[h, :, ls] + cw_ref[h, 1:2, ls] * win + cw_ref[h, 0:1, ls] * _row_before(win)
                             + cw_ref[h, 2:3, ls] * _row_after(win))[8:8 + R])
            g, v = acts
            o_ref[:, ls] = (g * _sigmoid(g) * v).astype(o_ref.dtype)
            return carry

        lax.fori_loop(0, _CONV_CB // _CONV_STRIP, strip, 0)

    return pl.pallas_call(
        body, name=name, grid=(FFN_H // _CONV_CB, n // R),
        in_specs=[main, prev, nxt, cws, cbs],
        out_specs=pl.BlockSpec((R, _CONV_CB), lambda j, i: (i, j)),
        out_shape=jax.ShapeDtypeStruct((n, FFN_H), BF16),
        compiler_params=pltpu.CompilerParams(dimension_semantics=("arbitrary", "arbitrary"),
                                             vmem_limit_bytes=V7X_VMEM_LIMIT),
    )(a2, a2, a2, cw, cb)


def _conv_gate_bwd(name, a2, dgv, cw, cb, R, tc):
    n = a2.shape[1]
    nb8 = n // 8
    main, prev, nxt, cws, cbs = _conv_specs(R, n)
    d_main = pl.BlockSpec((R, _CONV_CB), lambda j, i: (i, j))
    d_prev = pl.BlockSpec((8, _CONV_CB), lambda j, i: (jnp.maximum(i * (R // 8) - 1, 0), j))
    d_next = pl.BlockSpec((8, _CONV_CB), lambda j, i: (jnp.minimum((i + 1) * (R // 8), nb8 - 1), j))
    mid = slice(8, 8 + R)

    def body(a_ref, p_ref, n_ref, cw_ref, cb_ref, d_ref, dp_ref, dn_ref, da_ref, dcw_ref, dcb_ref):
        i = pl.program_id(1)
        keep_prev, keep_next = _halo_keep(i, R, tc, n)

        @pl.when(i == 0)
        def _():
            dcw_ref[...] = jnp.zeros_like(dcw_ref)
            dcb_ref[...] = jnp.zeros_like(dcb_ref)

        def strip(c, carry):
            ls = pl.ds(pl.multiple_of(c * _CONV_STRIP, _CONV_STRIP), _CONV_STRIP)
            cws_ = [[cw_ref[h, k:k + 1, ls] for k in range(3)] for h in range(2)]
            wins = [jnp.concatenate([p_ref[h, :, ls] * keep_prev, a_ref[h, :, ls], n_ref[h, :, ls] * keep_next], axis=0)
                    for h in range(2)]
            g, v = [cb_ref[h, :, ls] + cws_[h][1] * wins[h] + cws_[h][0] * _row_before(wins[h]) + cws_[h][2] * _row_after(wins[h])
                    for h in range(2)]
            dout = jnp.concatenate([dp_ref[:, ls].astype(F32) * keep_prev, d_ref[:, ls].astype(F32),
                                    dn_ref[:, ls].astype(F32) * keep_next], axis=0)
            sg = _sigmoid(g)
            das = [dout * v * sg * (1.0 + g * (1.0 - sg)), dout * g * sg]
            for h in range(2):
                da_win = das[h]
                da_ref[h, :, ls] = (cws_[h][1] * da_win + cws_[h][0] * _row_after(da_win)
                                    + cws_[h][2] * _row_before(da_win))[mid].astype(da_ref.dtype)
                da = da_win[mid]
                dcw_ref[h, 0:1, ls] += jnp.sum(da * _row_before(wins[h])[mid], axis=0, keepdims=True)
                dcw_ref[h, 1:2, ls] += jnp.sum(da * wins[h][mid], axis=0, keepdims=True)
                dcw_ref[h, 2:3, ls] += jnp.sum(da * _row_after(wins[h])[mid], axis=0, keepdims=True)
                dcb_ref[h, :, ls] += jnp.sum(da, axis=0, keepdims=True)
            return carry

        lax.fori_loop(0, _CONV_CB // _CONV_STRIP, strip, 0)

    return pl.pallas_call(
        body, name=name, grid=(FFN_H // _CONV_CB, n // R),
        in_specs=[main, prev, nxt, cws, cbs, d_main, d_prev, d_next],
        out_specs=[main, cws, cbs],
        out_shape=[jax.ShapeDtypeStruct((2, n, FFN_H), BF16), jax.ShapeDtypeStruct((2, 3, FFN_H), F32),
                   jax.ShapeDtypeStruct((2, 1, FFN_H), F32)],
        compiler_params=pltpu.CompilerParams(dimension_semantics=("arbitrary", "arbitrary"),
                                             vmem_limit_bytes=V7X_VMEM_LIMIT),
    )(a2, a2, a2, cw, cb, dgv, dgv, dgv)


_ARB2 = pltpu.CompilerParams(dimension_semantics=("arbitrary", "arbitrary"), vmem_limit_bytes=V7X_VMEM_LIMIT)


def _with_exchange(body, n_in, n_out, grid, comm):
    if comm is None:
        return body, [], [], []
    xs, gather = comm
    na = len(xs)
    specs, shapes, sems = _exchange_io(xs, gather)

    def wrapped(*refs):
        ins, x_refs = refs[:n_in], refs[n_in:n_in + na]
        outs, o_refs = refs[n_in + na:n_in + na + n_out], refs[n_in + na + n_out:n_in + 2 * na + n_out]
        scratch, sem_refs = refs[n_in + 2 * na + n_out:-3], refs[-3:]
        start, wait = _exchange_plan(x_refs, o_refs, *sem_refs, gather)
        ids = [pl.program_id(d) for d in range(len(grid))]
        first, last = ids[0] == 0, ids[0] == grid[0] - 1
        for d in range(1, len(grid)):
            first, last = first & (ids[d] == 0), last & (ids[d] == grid[d] - 1)

        @pl.when(first)
        def _():
            start()

        body(*ins, *outs, *scratch)

        @pl.when(last)
        def _():
            wait()

    return wrapped, specs, shapes, sems


def _attn_fwd(name, qt8, k2, vt2, comm=None):
    nq, nk = qt8.shape[2], k2.shape[1]
    tq = _pick(nq, (256, 128))
    tk = _pick(nk, (1408, 768, 256, 128))

    va = vt2.shape[1]

    def body(qt_ref, k_ref, vt_ref, ot_ref, lse_ref, m_ref, acc_ref):
        m_ref[...] = jnp.full((Q_GROUP, 1, tq), -1e30, F32)
        acc_ref[...] = jnp.zeros((Q_GROUP, va, tq), F32)

        def step(j, carry):
            sl = pl.ds(pl.multiple_of(j * tk, tk), tk)
            kj = k_ref[sl, :]
            vtj = vt_ref[:, sl]
            sts = [lax.dot_general(kj, qt_ref[h], _NN, preferred_element_type=F32) for h in range(Q_GROUP)]
            m_old = [m_ref[h] for h in range(Q_GROUP)]
            acc_old = [acc_ref[h] for h in range(Q_GROUP)]
            m_new = [jnp.maximum(m_old[h], jnp.max(sts[h], axis=0, keepdims=True)) for h in range(Q_GROUP)]
            pts = [jnp.exp2(sts[h] - m_new[h]).astype(BF16) for h in range(Q_GROUP)]
            pvs = [lax.dot_general(vtj, pts[h], _NN, preferred_element_type=F32) for h in range(Q_GROUP)]
            for h in range(Q_GROUP):
                acc_ref[h] = jnp.exp2(m_old[h] - m_new[h]) * acc_old[h] + pvs[h]
                m_ref[h] = m_new[h]
            return carry

        lax.fori_loop(0, nk // tk, step, 0, unroll=2)
        for h in range(Q_GROUP):
            l = acc_ref[h, HEAD_DIM:HEAD_DIM + 1, :]
            ot_ref[h] = (acc_ref[h, 0:HEAD_DIM, :] / l).astype(ot_ref.dtype)
            lse_ref[h] = m_ref[h] + jnp.log2(l)

    qspec = pl.BlockSpec((Q_GROUP, HEAD_DIM, tq), lambda g, i: (g, 0, i))
    lspec = pl.BlockSpec((Q_GROUP, 1, tq), lambda g, i: (g, 0, i))
    grid = (N_KV_HEADS, nq // tq)
    body, xspecs, xshapes, xsems = _with_exchange(body, 3, 2, grid, comm)
    res = pl.pallas_call(
        body, name=name, grid=grid,
        in_specs=[qspec, pl.BlockSpec((None, nk, HEAD_DIM), lambda g, i: (g, 0, 0)),
                  pl.BlockSpec((None, va, nk), lambda g, i: (g, 0, 0))] + xspecs,
        out_specs=[qspec, lspec] + xspecs,
        out_shape=[jax.ShapeDtypeStruct((N_Q_HEADS, HEAD_DIM, nq), BF16), jax.ShapeDtypeStruct((N_Q_HEADS, 1, nq), F32)] + xshapes,
        scratch_shapes=[pltpu.VMEM((Q_GROUP, 1, tq), F32), pltpu.VMEM((Q_GROUP, va, tq), F32)] + xsems,
        compiler_params=_ARB2,
    )(qt8, k2, vt2, *(comm[0] if comm else []))
    return res[0], res[1], list(res[2:])


def _attn_bwd(name, qt8, k2, v2, kt2, ot8, dot8, lse, dkt0, dvt0, comm=None):
    nq, nk = qt8.shape[2], k2.shape[1]
    n0 = dkt0.shape[2]
    tq = _pick(nq, (256, 128))
    tk = _pick(nk, (1408, 768, 256, 128))
    ts = tk
    heads = range(Q_GROUP)

    def body(qt_ref, k_ref, v_ref, kt_ref, ot_ref, dot_ref, lse_ref, dk0_ref, dv0_ref, dqt_ref, dkt_ref, dvt_ref, dl_ref, dq_acc):
        @pl.when(pl.program_id(1) == 0)
        def _():
            dkt_ref[...] = jnp.zeros_like(dkt_ref)
            dvt_ref[...] = jnp.zeros_like(dvt_ref)
            dkt_ref[:, 0:n0] = dk0_ref[...]
            dvt_ref[:, 0:n0] = dv0_ref[...]

        for h in heads:
            dl_ref[h] = jnp.sum(dot_ref[h].astype(F32) * ot_ref[h].astype(F32), axis=0, keepdims=True)
        dq_acc[...] = jnp.zeros((Q_GROUP, HEAD_DIM, tq), F32)

        items = [(s, h) for s in range(tk // ts) for h in heads]

        def step(j, carry):
            def keys(s):
                return pl.ds(pl.multiple_of(j * tk + s * ts, ts), ts)

            def scores(item):
                s, h = item
                return (lax.dot_general(k_ref[keys(s), :], qt_ref[h], _NN, preferred_element_type=F32),
                        lax.dot_general(v_ref[keys(s), :], dot_ref[h], _NN, preferred_element_type=F32))

            nxt = scores(items[0])
            for n, (s, h) in enumerate(items):
                st, dpt = nxt
                if n + 1 < len(items):
                    nxt = scores(items[n + 1])
                pt = jnp.exp2(st - lse_ref[h])
                dst = (pt * (dpt - dl_ref[h])).astype(BF16)
                dq_acc[h] += lax.dot_general(kt_ref[:, keys(s)], dst, _NN, preferred_element_type=F32)
                dv_h = lax.dot_general(dot_ref[h], pt.astype(BF16), _NT, preferred_element_type=F32)
                dk_h = lax.dot_general(qt_ref[h], dst, _NT, preferred_element_type=F32)
                dvt_s, dkt_s = (dv_h, dk_h) if h == 0 else (dvt_s + dv_h, dkt_s + dk_h)
                if h == Q_GROUP - 1:
                    dvt_ref[:, keys(s)] += dvt_s
                    dkt_ref[:, keys(s)] += dkt_s
            return carry

        lax.fori_loop(0, nk // tk, step, 0)
        dqt_ref[...] = dq_acc[...]

    tspec = pl.BlockSpec((Q_GROUP, HEAD_DIM, tq), lambda g, i: (g, 0, i))
    lspec = pl.BlockSpec((Q_GROUP, 1, tq), lambda g, i: (g, 0, i))
    kspec = pl.BlockSpec((None, nk, HEAD_DIM), lambda g, i: (g, 0, 0))
    ktspec = pl.BlockSpec((None, HEAD_DIM, nk), lambda g, i: (g, 0, 0))
    k0spec = pl.BlockSpec((None, HEAD_DIM, n0), lambda g, i: (g, 0, 0))
    grid = (N_KV_HEADS, nq // tq)
    body, xspecs, xshapes, xsems = _with_exchange(body, 9, 3, grid, comm)
    res = pl.pallas_call(
        body, name=name, grid=grid,
        in_specs=[tspec, kspec, kspec, ktspec, tspec, tspec, lspec, k0spec, k0spec] + xspecs,
        out_specs=[tspec, ktspec, ktspec] + xspecs,
        out_shape=[jax.ShapeDtypeStruct((N_Q_HEADS, HEAD_DIM, nq), F32), jax.ShapeDtypeStruct((N_KV_HEADS, HEAD_DIM, nk), F32),
                   jax.ShapeDtypeStruct((N_KV_HEADS, HEAD_DIM, nk), F32)] + xshapes,
        scratch_shapes=[pltpu.VMEM((Q_GROUP, 1, tq), F32), pltpu.VMEM((Q_GROUP, HEAD_DIM, tq), F32)] + xsems,
        compiler_params=_ARB2,
    )(qt8, k2, v2, kt2, ot8, dot8, lse, dkt0, dvt0, *(comm[0] if comm else []))
    return res[0], res[1], res[2], list(res[3:])


def _log_sigmoid(z):
    return jnp.minimum(z, 0.0) - jnp.log(1.0 + jnp.exp(-jnp.abs(z)))


_BDOT_DIMS = {"nn": _NN, "nt": _NT, "tn": _TN}
_BDOT_BWD = {"nn": (("nt", "gb"), ("tn", "ag")), "nt": (("nn", "gb"), ("tn", "ga")), "tn": (("nt", "bg"), ("nn", "ag"))}


def _bdot_raw(a, b, mode):
    return lax.dot_general(a.astype(BF16), b.astype(BF16), _BDOT_DIMS[mode], preferred_element_type=F32)


@functools.partial(jax.custom_vjp, nondiff_argnums=(2,))
def _bdot(a, b, mode):
    return _bdot_raw(a, b, mode)


def _bdot_fwd(a, b, mode):
    return _bdot_raw(a, b, mode), (a.astype(BF16), b.astype(BF16))


def _bdot_bwd(mode, res, g):
    ops = {"a": res[0], "b": res[1], "g": g}
    (ma, oa), (mb, ob) = _BDOT_BWD[mode]
    return _bdot_raw(ops[oa[0]], ops[oa[1]], ma), _bdot_raw(ops[ob[0]], ops[ob[1]], mb)


_bdot.defvjp(_bdot_fwd, _bdot_bwd)


def _tile_tri(rev, rows):
    r_i = lax.broadcasted_iota(jnp.int32, (rows, rows), 0)
    c_i = lax.broadcasted_iota(jnp.int32, (rows, rows), 1)
    same = (r_i // GLA_CHUNK) == (c_i // GLA_CHUNK)
    return same & ((c_i >= r_i) if rev else (c_i <= r_i))


def _tri_dot(rev, x):
    tri = _tile_tri(rev, x.shape[0]).astype(BF16)
    return sum(lax.dot_general(tri, piece, _NN, preferred_element_type=F32) for piece in _split3(x))


@functools.partial(jax.custom_vjp, nondiff_argnums=(1,))
def _chunk_cumsum(x, rev):
    return _tri_dot(rev, x)


_chunk_cumsum.defvjp(lambda x, rev: (_tri_dot(rev, x), None), lambda rev, _, g: (_tri_dot(not rev, g),))


_GLA_REV = (False, True)


def _gla_tile_pair(qs, ks, vss, as_, w2s, b2s, states):
    both = range(2)
    rows = qs[0].shape[0]
    nch = rows // GLA_CHUNK
    tris = [_tile_tri(_GLA_REV[d], rows) for d in both]
    chunk_of_row = lax.broadcasted_iota(jnp.int32, (rows, 1), 0) // GLA_CHUNK
    in_chunk = [(chunk_of_row == c).astype(F32) for c in range(nch)]
    las = [_log_sigmoid(_bdot(as_[d], w2s[d], "nn") + b2s[d]) * (1.0 / GLA_TAU) for d in both]
    cums = [_chunk_cumsum(las[d], _GLA_REV[d]) for d in both]
    tots = [[jnp.sum(las[d] * in_chunk[c], axis=0, keepdims=True) for c in range(nch)] for d in both]
    tot_rows = [sum(in_chunk[c] * tots[d][c] for c in range(nch)) for d in both]
    q_in = [qs[d] * (GLA_DK ** -0.5) * jnp.exp(cums[d]) for d in both]
    k_in = [ks[d] * jnp.exp(-cums[d]) for d in both]
    k_st = [ks[d] * jnp.exp(tot_rows[d] - cums[d]) for d in both]
    lane = lax.broadcasted_iota(jnp.int32, (1, GLA_QK_W), 1)
    outs = [[], []]
    for h in range(GLA_HEADS):
        head = ((lane >= GLA_DK * h) & (lane < GLA_DK * (h + 1))).astype(F32)
        atts = [jnp.where(tris[d], _bdot(q_in[d] * head, k_in[d], "nt"), 0.0) for d in both]
        for d in both:
            outs[d].append(_bdot(atts[d], vss[d][h], "nn"))
    os_ = [jnp.concatenate(outs[d], axis=1) for d in both]
    hr = lax.broadcasted_iota(jnp.int32, (GLA_V_W, GLA_QK_W), 0) // GLA_DV
    hc = lax.broadcasted_iota(jnp.int32, (GLA_V_W, GLA_QK_W), 1) // GLA_DK
    same_head = (hr == hc).astype(F32)
    v_all = [jnp.concatenate(vss[d], axis=1) for d in both]
    states = list(states)
    for step in range(nch):
        chunk = (step, nch - 1 - step)
        us = [_bdot(v_all[d], k_st[d] * in_chunk[chunk[d]], "tn") * same_head for d in both]
        for d in both:
            os_[d] = os_[d] + _bdot(q_in[d] * in_chunk[chunk[d]], states[d], "nt")
            states[d] = jnp.exp(tots[d][chunk[d]]) * states[d] + us[d]
    return os_, states


def _gla_tile_of(step, rev, nct, nt):
    if not rev:
        return step
    return jnp.where(step < nct, nct - 1 - step, nt - 1 - (step - nct))


def _gla_row_specs(R, tile):
    return [pl.BlockSpec((R, GLA_QK_W), lambda s: (tile(s), C_GQ // GLA_QK_W)),
            pl.BlockSpec((R, GLA_QK_W), lambda s: (tile(s), C_GK // GLA_QK_W)),
            pl.BlockSpec((R, GLA_V_W), lambda s: (tile(s), C_GV // GLA_V_W)),
            pl.BlockSpec((R, 128), lambda s: (tile(s), C_AF // 128))]


_GLA_WIDTHS = (GLA_QK_W, GLA_QK_W, GLA_V_W, 128)


def _gla_load(refs):
    q_ref, k_ref, v_ref, a_ref = refs
    return q_ref[...], k_ref[...], [v_ref[:, GLA_DV * h:GLA_DV * (h + 1)] for h in range(GLA_HEADS)], a_ref[...]


def _gla_fwd(name, p, w2s, b2s, R, nct):
    n = p.shape[0]
    nt = n // R
    tiles = [lambda s, d=d: _gla_tile_of(s, _GLA_REV[d], nct, nt) for d in range(2)]

    def body(*refs):
        rows, (w2f, b2f, w2b, b2b), (of, ob, sf, sb), states = refs[:8], refs[8:12], refs[12:16], refs[16:]

        @pl.when(pl.program_id(0) == 0)
        def _():
            for st in states:
                st[...] = jnp.zeros_like(st)

        ins = [_gla_load(rows[:4]), _gla_load(rows[4:])]
        s_in = [states[0][...], states[1][...]]
        sf[...], sb[...] = s_in
        os_, s_out = _gla_tile_pair([i[0] for i in ins], [i[1] for i in ins], [i[2] for i in ins], [i[3] for i in ins],
                                    [w2f[...], w2b[...]], [b2f[...], b2b[...]], s_in)
        of[...], ob[...] = os_
        states[0][...], states[1][...] = s_out

    in_specs = (_gla_row_specs(R, tiles[0]) + _gla_row_specs(R, tiles[1])
                + [_full_spec(w2s[0].shape), _full_spec(b2s[0].shape), _full_spec(w2s[1].shape), _full_spec(b2s[1].shape)])
    o_specs = [pl.BlockSpec((R, GLA_V_W), lambda s, t=t: (t(s), 0)) for t in tiles]
    s_specs = [pl.BlockSpec((None, GLA_V_W, GLA_QK_W), lambda s, t=t: (t(s), 0, 0)) for t in tiles]
    return pl.pallas_call(
        body, name=name, grid=(nt,), in_specs=in_specs, out_specs=o_specs + s_specs,
        out_shape=[jax.ShapeDtypeStruct((n, GLA_V_W), F32)] * 2 + [jax.ShapeDtypeStruct((nt, GLA_V_W, GLA_QK_W), F32)] * 2,
        scratch_shapes=[pltpu.VMEM((GLA_V_W, GLA_QK_W), F32)] * 2,
        compiler_params=_ARB1,
    )(*([p] * 8), w2s[0], b2s[0], w2s[1], b2s[1])


def _gla_bwd(name, p, w2s, b2s, ssaves, do, R, nct):
    n = p.shape[0]
    nt = n // R
    tiles = [lambda s, d=d: _gla_tile_of(nt - 1 - s, _GLA_REV[d], nct, nt) for d in range(2)]

    def body(*refs):
        rows, (w2f, b2f, w2b, b2b), ss, dos = refs[:8], refs[8:12], refs[12:14], refs[14:16]
        d_rows, (dw2f, db2f, dw2b, db2b), dstates = refs[16:24], refs[24:28], refs[28:]

        @pl.when(pl.program_id(0) == 0)
        def _():
            for ref in (dw2f, db2f, dw2b, db2b) + tuple(dstates):
                ref[...] = jnp.zeros_like(ref)

        ins = [_gla_load(rows[:4]), _gla_load(rows[4:])]
        _, vjp = jax.vjp(_gla_tile_pair, [i[0] for i in ins], [i[1] for i in ins], [i[2] for i in ins], [i[3] for i in ins],
                         [w2f[...], w2b[...]], [b2f[...], b2b[...]], [ss[0][...], ss[1][...]])
        dqs, dks, dvss, das, dw2, db2, ds = vjp(([dos[0][...], dos[1][...]], [dstates[0][...], dstates[1][...]]))
        for d in range(2):
            grads = [dqs[d], dks[d], jnp.concatenate(dvss[d], axis=1), das[d]]
            for ref, g in zip(d_rows[4 * d:4 * d + 4], grads):
                ref[...] = g
            dstates[d][...] = ds[d]
        dw2f[...] += dw2[0]
        db2f[...] += db2[0]
        dw2b[...] += dw2[1]
        db2b[...] += db2[1]

    par_specs = [_full_spec(w2s[0].shape), _full_spec(b2s[0].shape), _full_spec(w2s[1].shape), _full_spec(b2s[1].shape)]
    d_specs = [pl.BlockSpec((R, w), lambda s, t=t: (t(s), 0)) for t in tiles for w in _GLA_WIDTHS]
    in_specs = (_gla_row_specs(R, tiles[0]) + _gla_row_specs(R, tiles[1]) + par_specs
                + [pl.BlockSpec((None, GLA_V_W, GLA_QK_W), lambda s, t=t: (t(s), 0, 0)) for t in tiles]
                + [pl.BlockSpec((R, GLA_V_W), lambda s, t=t: (t(s), 0)) for t in tiles])
    res = pl.pallas_call(
        body, name=name, grid=(nt,), in_specs=in_specs, out_specs=d_specs + par_specs,
        out_shape=[jax.ShapeDtypeStruct((n, w), F32) for _ in range(2) for w in _GLA_WIDTHS]
        + [jax.ShapeDtypeStruct(a.shape, F32) for a in (w2s[0], b2s[0], w2s[1], b2s[1])],
        scratch_shapes=[pltpu.VMEM((GLA_V_W, GLA_QK_W), F32)] * 2,
        compiler_params=_ARB1,
    )(*([p] * 8), w2s[0], b2s[0], w2s[1], b2s[1], ssaves[0], ssaves[1], do, do)
    return res[:4], res[4:8], res[8:]


def _final_loss(name, x, target, gf, R, nct):
    n = x.shape[0]

    def body(x_ref, t_ref, g_ref, loss_ref, dx_ref, dg_ref):
        i = pl.program_id(0)

        @pl.when(i == 0)
        def _():
            loss_ref[...] = jnp.zeros_like(loss_ref)
            dg_ref[...] = jnp.zeros_like(dg_ref)

        @pl.when(i < nct)
        def _():
            dx_ref[...] = jnp.zeros_like(dx_ref)

        @pl.when(i >= nct)
        def _():
            y, vjp = jax.vjp(_rms, x_ref[...], g_ref[...])
            err = y - t_ref[...]
            loss_ref[...] += jnp.sum(0.5 * jnp.mean(err * err, axis=-1, keepdims=True))
            dx, dg = vjp(err * (1.0 / D))
            dx_ref[...] = dx
            dg_ref[...] += dg

    return pl.pallas_call(
        body, name=name, grid=(n // R,),
        in_specs=[_row_spec(R, D, 0), pl.BlockSpec((R, D), lambda i: (jnp.maximum(i - nct, 0), 0)), _full_spec((1, D))],
        out_specs=[_full_spec((8, 128)), _row_spec(R, D, 0), _full_spec((1, D))],
        out_shape=[jax.ShapeDtypeStruct((8, 128), F32), jax.ShapeDtypeStruct((n, D), F32), jax.ShapeDtypeStruct((1, D), F32)],
        compiler_params=_ARB1,
    )(x, target, gf)


def _adamw(name, w, m, v, gparts):
    rows, cols = w.shape
    nparts = gparts.shape[0]
    tr = rows
    for cand in range(min(rows, 256), 15, -16):
        if rows % cand == 0:
            tr = cand
            break

    def body(w_ref, m_ref, v_ref, g_ref, go_ref, d_ref, mo_ref, vo_ref):
        g = g_ref[0].astype(F32)
        for k in range(1, nparts):
            g = g + g_ref[k].astype(F32)
        m_new = ADAM_B1 * m_ref[...] + (1.0 - ADAM_B1) * g
        v_new = ADAM_B2 * v_ref[...] + (1.0 - ADAM_B2) * (g * g)
        m_hat = m_new / (1.0 - ADAM_B1 ** ADAM_STEP)
        v_hat = v_new / (1.0 - ADAM_B2 ** ADAM_STEP)
        go_ref[...] = g
        d_ref[...] = -ADAM_LR * (m_hat / (jnp.sqrt(v_hat) + ADAM_EPS) + ADAM_WD * w_ref[...])
        mo_ref[...] = m_new
        vo_ref[...] = v_new

    spec = pl.BlockSpec((tr, cols), lambda i: (i, 0))
    return pl.pallas_call(
        body, name=name, grid=(rows // tr,),
        in_specs=[spec, spec, spec, pl.BlockSpec((nparts, tr, cols), lambda i: (0, i, 0))],
        out_specs=[spec] * 4, out_shape=[jax.ShapeDtypeStruct((rows, cols), F32)] * 4,
        compiler_params=_ARB1,
    )(w, m, v, gparts)


def _my_index():
    return 4 * lax.axis_index("x") + 2 * lax.axis_index("y") + lax.axis_index("c")


def _xor_peer(k):
    flip = lambda a, bit: (1 - a) if bit else a
    pos = (flip(lax.axis_index("x"), (k >> 2) & 1), flip(lax.axis_index("y"), (k >> 1) & 1), flip(lax.axis_index("c"), k & 1))
    return pos, 4 * pos[0] + 2 * pos[1] + pos[2]


def _exchange_plan(x_refs, o_refs, send_sems, recv_sems, local_sems, gather):
    npeer = N_DEV - 1
    me = _my_index()
    locals_, sends, recvs = [], [], []
    for a, (x_ref, o_ref) in enumerate(zip(x_refs, o_refs)):
        mine = x_ref if gather else x_ref.at[me]
        locals_.append(pltpu.make_async_copy(mine, o_ref.at[me], local_sems.at[a]))
        for k in range(1, N_DEV):
            pos, lin = _xor_peer(k)
            src = x_ref if gather else x_ref.at[lin]
            sem = a * npeer + k - 1
            sends.append(pltpu.make_async_remote_copy(src_ref=src, dst_ref=o_ref.at[me], send_sem=send_sems.at[sem],
                                                      recv_sem=recv_sems.at[sem], device_id=pos, device_id_type=MESH_ID))
            recvs.append(pltpu.make_async_remote_copy(src_ref=src, dst_ref=o_ref.at[lin], send_sem=send_sems.at[sem],
                                                      recv_sem=recv_sems.at[sem], device_id=pos, device_id_type=MESH_ID))

    def start():
        for cp in locals_ + sends:
            cp.start()

    def wait():
        for cp in recvs:
            cp.wait_recv()
        for cp in sends:
            cp.wait_send()
        for cp in locals_:
            cp.wait()

    return start, wait


def _exchange_io(xs, gather):
    na = len(xs)
    hbm = pl.BlockSpec(memory_space=pltpu.HBM)
    shapes = [jax.ShapeDtypeStruct((N_DEV,) + tuple(x.shape if gather else x.shape[1:]), x.dtype) for x in xs]
    sems = [pltpu.SemaphoreType.DMA((na * (N_DEV - 1),)), pltpu.SemaphoreType.DMA((na * (N_DEV - 1),)),
            pltpu.SemaphoreType.DMA((na,))]
    return [hbm] * na, shapes, sems


def _exchange(name, xs, *, gather):
    na = len(xs)
    specs, shapes, sems = _exchange_io(xs, gather)

    def body(*refs):
        start, wait = _exchange_plan(refs[:na], refs[na:2 * na], *refs[2 * na:], gather)
        start()
        wait()

    return list(pl.pallas_call(body, name=name, in_specs=specs, out_specs=specs, out_shape=shapes, scratch_shapes=sems)(*xs))


def _adaln_fwd(name, craw16, w_ada, b_cols):
    def body(c_ref, w_ref, b_ref, o_ref):
        cs = jax.nn.silu(c_ref[...]).astype(BF16)
        for l in range(2):
            o_ref[l] = lax.dot_general(cs, w_ref[l].astype(BF16), _NN, preferred_element_type=F32) + b_ref[l]

    return pl.pallas_call(
        body, name=name, out_shape=jax.ShapeDtypeStruct((2, 16, w_ada.shape[2]), F32),
        compiler_params=pltpu.CompilerParams(vmem_limit_bytes=V7X_VMEM_LIMIT),
    )(craw16, w_ada, b_cols)


def _adaln_bwd(name, craw16, w_ada, dm):
    def body(c_ref, w_ref, dm_ref, gw_ref, dc_ref):
        c = c_ref[...]
        sg = jax.nn.sigmoid(c)
        cs = c * sg
        row = lax.broadcasted_iota(jnp.int32, (8, 1), 0)
        dc = jnp.zeros((16, D), F32)
        for l in range(2):
            dmx = dm_ref[2 * l + 1]
            dmc = jnp.where(row == 0, jnp.sum(dm_ref[2 * l], axis=0, keepdims=True), 0.0)
            gw_ref[l] = _hdot(cs[0:8], dmx, _TN) + _hdot(cs[8:16], dmc, _TN)
            dc = dc + _hdot(jnp.concatenate([dmx, dmc], axis=0), w_ref[l], _NT)
        dc_ref[...] = dc * sg * (1.0 + c * (1.0 - sg))

    return pl.pallas_call(
        body, name=name,
        out_shape=[jax.ShapeDtypeStruct(w_ada.shape, F32), jax.ShapeDtypeStruct((16, D), F32)],
        compiler_params=pltpu.CompilerParams(vmem_limit_bytes=V7X_VMEM_LIMIT),
    )(craw16, w_ada, dm)


_IN_OFFS = [sum(IN_SPLITS[:k]) for k in range(len(IN_SPLITS) + 1)]
_MY_ORDER = (11, 12, 13, 0, 1, 2, 7, 10, 5, 6, 3, 4, 8, 9)


_IN_SHARD = IN_WIDTH // N_DEV


def _win_my_cols(pieces):
    parts = []
    for k in _MY_ORDER:
        a, b = _IN_OFFS[k], _IN_OFFS[k + 1]
        for s in range(a // _IN_SHARD, (b - 1) // _IN_SHARD + 1):
            lo, hi = max(a, s * _IN_SHARD), min(b, (s + 1) * _IN_SHARD)
            parts.append(pieces[s][:, lo - s * _IN_SHARD:hi - s * _IN_SHARD])
    parts.append(jnp.zeros((pieces[0].shape[0], PW - IN_WIDTH), pieces[0].dtype))
    return jnp.concatenate(parts, axis=1)


def _win_shards(wp):
    my_offs, pos = {}, 0
    for k in _MY_ORDER:
        my_offs[k] = pos
        pos += IN_SPLITS[k]
    shards = []
    for s in range(N_DEV):
        parts = []
        for k in range(len(IN_SPLITS)):
            lo, hi = max(_IN_OFFS[k], s * _IN_SHARD), min(_IN_OFFS[k + 1], (s + 1) * _IN_SHARD)
            if lo < hi:
                parts.append(wp[:, my_offs[k] + lo - _IN_OFFS[k]:my_offs[k] + hi - _IN_OFFS[k]])
        shards.append(jnp.concatenate(parts, axis=1))
    return jnp.stack(shards)


def _tile_friendly(shape, axis):
    width = shape[axis] // N_DEV if axis == len(shape) - 1 else 128
    return len(shape) >= 2 and width % 128 == 0


def _split_shards(full, axis):
    s = full.shape
    if _tile_friendly(s, axis):
        return jnp.moveaxis(full.reshape(s[:axis] + (N_DEV, s[axis] // N_DEV) + s[axis + 1:]), axis, 0)
    c = s[axis] // N_DEV
    return jnp.stack([lax.slice_in_dim(full, k * c, (k + 1) * c, axis=axis) for k in range(N_DEV)])


def _join_shards(g, axis):
    s = g.shape[1:]
    full_shape = s[:axis] + (N_DEV * s[axis],) + s[axis + 1:]
    if _tile_friendly(full_shape, axis):
        return jnp.moveaxis(g, 0, axis).reshape(full_shape)
    return jnp.concatenate([g[k] for k in range(N_DEV)], axis=axis)


def _pack_rows(pieces, row_mult):
    rows = jnp.concatenate([p.reshape(-1, 128) for p in pieces], axis=0)
    padn = (-rows.shape[0]) % row_mult
    if padn:
        rows = jnp.concatenate([rows, jnp.zeros((padn, 128), rows.dtype)], axis=0)
    return rows


def _unpack_rows(rows, shapes):
    out, pos = [], 0
    for s in shapes:
        size = 1
        for d in s:
            size *= d
        out.append(rows[pos:pos + size // 128].reshape(tuple(s)))
        pos += size // 128
    return out


def _heads_front(a, nh):
    return a.reshape(a.shape[0], nh, HEAD_DIM).transpose(1, 0, 2)


def _heads_back(a):
    return a.transpose(1, 0, 2).reshape(a.shape[1], a.shape[0] * HEAD_DIM)


def _rope_tables(t, tc):
    tok = jnp.arange(t, dtype=jnp.int32)
    inv_freq = ROPE_THETA ** (-jnp.arange(ROPE_FREQS, dtype=F32) / ROPE_FREQS)
    ang_r = (tok // GRID_W).astype(F32)[:, None] * inv_freq
    ang_c = (tok % GRID_W).astype(F32)[:, None] * inv_freq
    cos64 = jnp.concatenate([jnp.cos(ang_r), jnp.cos(ang_r), jnp.cos(ang_c), jnp.cos(ang_c)], axis=1)
    sin64 = jnp.concatenate([-jnp.sin(ang_r), jnp.sin(ang_r), -jnp.sin(ang_c), jnp.sin(ang_c)], axis=1)
    cos64 = jnp.concatenate([jnp.ones((tc, HEAD_DIM), F32), cos64], axis=0)
    sin64 = jnp.concatenate([jnp.zeros((tc, HEAD_DIM), F32), sin64], axis=0)
    return jnp.tile(cos64, (1, N_Q_HEADS)), jnp.tile(sin64, (1, N_Q_HEADS))


def _head_mean_matrix(width):
    i = jnp.arange(width) // HEAD_DIM
    return (i[:, None] == i[None, :]).astype(F32) / HEAD_DIM


def _head_tile_matrix(width):
    return (jnp.arange(HEAD_DIM)[:, None] == (jnp.arange(width) % HEAD_DIM)[None, :]).astype(F32)


def _heads_t(a, nh):
    return a.T.reshape(nh, HEAD_DIM, a.shape[0])


def _heads_t_back(a):
    return a.reshape(a.shape[0] * HEAD_DIM, a.shape[2]).T


def _attention_fwd(tag, qr, kr, vv, tc, comm):
    qt8, k2, vt2 = _heads_t(qr, N_Q_HEADS), _heads_front(kr, N_KV_HEADS), _heads_t(vv, N_KV_HEADS)
    vt2 = jnp.concatenate([vt2, jnp.ones((N_KV_HEADS, 8, vt2.shape[2]), BF16)], axis=1)
    o_c, lse_c, _ = _attn_fwd(tag + "_attn_ctx", qt8[:, :, :tc], k2[:, :tc], vt2[:, :, :tc])
    o_x, lse_x, comm_out = _attn_fwd(tag + "_attn_lat", qt8[:, :, tc:], k2, vt2, comm)
    ot8 = jnp.concatenate([o_c, o_x], axis=2)
    lse = jnp.concatenate([lse_c, lse_x], axis=2)
    return _heads_t_back(ot8), (qr, kr, vv, ot8, lse), comm_out


def _attention_bwd(tag, saved, datt, tc, comm):
    qr, kr, vv, ot8, lse = saved
    datt = datt.astype(BF16)
    qt8, dot8 = _heads_t(qr, N_Q_HEADS), _heads_t(datt, N_Q_HEADS)
    k2, v2, kt2 = _heads_front(kr, N_KV_HEADS), _heads_front(vv, N_KV_HEADS), _heads_t(kr, N_KV_HEADS)
    zero = jnp.zeros((N_KV_HEADS, HEAD_DIM, tc), F32)
    dq_c, dk_c, dv_c, _ = _attn_bwd(tag + "_attn_b_ctx", qt8[:, :, :tc], k2[:, :tc], v2[:, :tc], kt2[:, :, :tc],
                                    ot8[:, :, :tc], dot8[:, :, :tc], lse[:, :, :tc], zero, zero)
    dq_x, dkt2, dvt2, comm_out = _attn_bwd(tag + "_attn_b_lat", qt8[:, :, tc:], k2, v2, kt2,
                                           ot8[:, :, tc:], dot8[:, :, tc:], lse[:, :, tc:], dk_c, dv_c, comm)
    dqt8 = jnp.concatenate([dq_c, dq_x], axis=2)
    return _heads_t_back(dqt8), _heads_t_back(dkt2), _heads_t_back(dvt2), comm_out


def _layer_fwd(tag, x, w, modv, consts, R, nct, tc, comm, late_weights):
    sh1, sc1, g1, sh2, sc2, g2 = modv
    cosq, sinq, bdq, eq, bdk, ek = consts
    n = x.shape[0]
    (h1,) = _rowwise(tag + "_ln1", _f_lnmod, R, nct, [(x, D, 0, D)], [], [(w["norm1_g"], False)], [], [sh1, sc1], [(D, BF16)])
    p = _matmul(tag + "_in", h1, w["w_in"], "nn", F32)
    qk_rows = [(p, Q_W, C_Q // Q_W, Q_W), (p, KV_W, C_K // KV_W, KV_W)]
    qk_consts = [(cosq, Q_W, 0, Q_W), (sinq, Q_W, 0, Q_W), (cosq, KV_W, 0, KV_W), (sinq, KV_W, 0, KV_W)]
    qk_params = [(w["q_norm_g"], False), (w["k_norm_g"], False)]
    qk_cparams = [(bdq, False), (eq, False), (bdk, False), (ek, False)]
    qr, kr = _rowwise(tag + "_qk", _f_qknorm, R, nct, qk_rows, qk_consts, qk_params, qk_cparams, [],
                      [(Q_W, BF16), (KV_W, BF16)], post=_qk_post)
    vv = p[:, C_VV:C_VV + KV_W].astype(BF16)
    att, att_saved, comm_out = _attention_fwd(tag, qr, kr, vv, tc, comm)
    w.update(late_weights(comm_out))

    o_f, o_b, s_f, s_b = _gla_fwd(tag + "_gla", p, (w["w2p_f"], w["w2p_b"]), (w["b2_f"], w["b2_b"]), R, nct)
    go_rows = [(o_f, GLA_V_W, 0, GLA_DV), (o_b, GLA_V_W, 0, GLA_DV), (p, GLA_V_W, C_R // GLA_V_W, GLA_DV)]
    (gla,) = _rowwise(tag + "_glaout", _f_glaout, R, nct, go_rows, [], [(w["gla_norm_g"], True)], [], [], [(GLA_V_W, BF16)])

    rg = GMLP_CHUNK
    gm_rows = [(p, GMLP_W, C_U // GMLP_W, GMLP_W), (p, GMLP_W, C_V // GMLP_W, GMLP_W // GMLP_GROUPS)]
    gm_params = [(w["gmlp_norm_g"], True), (w["w_spatial"], True), (w["b_spatial_t"], True)]
    (gm,) = _rowwise(tag + "_gmlp", _f_gmlp, rg, tc // rg, gm_rows, [], gm_params, [], [], [(GMLP_W, BF16)])

    ya = _matmul(tag + "_br_a", gm, w["w_br_a"], "nn", F32)
    yb = _matmul(tag + "_br_b", att, w["w_br_b"], "nn", F32)
    yc = _matmul(tag + "_br_c", gla, w["w_br_c"], "nn", F32)
    mg_rows = [(p, D, C_GA // D, D), (p, D, C_GB // D, D), (p, D, C_GC // D, D), (ya, D, 0, D), (yb, D, 0, D), (yc, D, 0, D)]
    (merged,) = _rowwise(tag + "_merge", _f_merge, R, nct, mg_rows, [], [], [], [], [(D, BF16)])
    mix = _matmul(tag + "_out", merged, w["w_out"], "nn", F32)
    (x_mid,) = _rowwise(tag + "_res1", _f_resid, R, nct, [(x, D, 0, D), (mix, D, 0, D)], [], [], [], [g1], [(D, F32)])

    (h2,) = _rowwise(tag + "_ln2", _f_lnmod, R, nct, [(x_mid, D, 0, D)], [], [(w["norm2_g"], False)], [], [sh2, sc2], [(D, BF16)])
    a2 = _matmul(tag + "_up", h2, w["w_ffn_up"], "nn", F32, o_halves=True)
    gv = _conv_gate(tag + "_conv", a2, w["conv_w_h"], w["conv_b_h"], R, tc)
    ffn = _matmul(tag + "_down", gv, w["w_ffn_down"], "nn", F32)
    (x_next,) = _rowwise(tag + "_res2", _f_resid, R, nct, [(x_mid, D, 0, D), (ffn, D, 0, D)], [], [], [], [g2], [(D, F32)])
    saved = dict(x=x, h1=h1, p=p, att_saved=att_saved, att=att, s_f=s_f, s_b=s_b, gla=gla, gm=gm,
                 ya=ya, yb=yb, yc=yc, merged=merged, mix=mix, x_mid=x_mid, h2=h2, a2=a2, gv=gv, ffn=ffn,
                 qk=(qk_rows, qk_consts, qk_params, qk_cparams), go_rows=go_rows, gm_info=(gm_rows, gm_params),
                 mg_rows=mg_rows)
    return x_next, saved, comm_out


def _layer_bwd(tag, dx_next, s, w, modv, R, nct, tc, make_comm, make_tail_comm):
    sh1, sc1, g1, sh2, sc2, g2 = modv
    gw = {}
    (dffn,), _, (dg2,) = _rowwise_bwd(tag + "_res2_b", _f_resid, R, nct, [(s["ffn"], D, 0, D), (s["ffn"], D, 0, D)], [], [], [], [g2],
                                      [(dx_next, D)], [None, BF16])
    dgv = _matmul(tag + "_down_da", dffn, w["w_ffn_down"], "nt", F32)
    gw["w_ffn_down"] = _matmul(tag + "_down_dw", s["gv"], dffn, "tn", F32)
    da2, dcw, dcb = _conv_gate_bwd(tag + "_conv_b", s["a2"], dgv, w["conv_w_h"], w["conv_b_h"], R, tc)
    gw["conv_w_h"], gw["conv_b_h"] = dcw, dcb
    dh2 = _matmul(tag + "_up_da", da2, w["w_ffn_up"], "nt", F32, a_halves=True)
    gw["w_ffn_up"] = _matmul(tag + "_up_dw", s["h2"], da2, "tn", F32, b_halves=True)
    (dx_mid,), (gw["norm2_g"],), (dsh2, dsc2) = _rowwise_bwd(
        tag + "_ln2_b", _f_lnmod, R, nct, [(s["x_mid"], D, 0, D)], [], [(w["norm2_g"], False)], [], [sh2, sc2],
        [(dh2, D)], [F32], adds=[dx_next])
    (dmix,), _, (dg1,) = _rowwise_bwd(tag + "_res1_b", _f_resid, R, nct, [(s["mix"], D, 0, D), (s["mix"], D, 0, D)], [], [], [], [g1],
                                      [(dx_mid, D)], [None, BF16])
    dmerged = _matmul(tag + "_out_da", dmix, w["w_out"], "nt", F32)
    gw["w_out"] = _matmul(tag + "_out_dw", s["merged"], dmix, "tn", F32)
    (dga, dgb, dgc, dya, dyb, dyc), _, _ = _rowwise_bwd(tag + "_merge_b", _f_merge, R, nct, s["mg_rows"], [], [], [], [],
                                                        [(dmerged, D)], [BF16] * 6)
    dgm = _matmul(tag + "_br_a_da", dya, w["w_br_a"], "nt", F32)
    datt = _matmul(tag + "_br_b_da", dyb, w["w_br_b"], "nt", F32)
    dgla = _matmul(tag + "_br_c_da", dyc, w["w_br_c"], "nt", F32)
    gm_rows, gm_params = s["gm_info"]
    gw["w_br_a"] = _matmul(tag + "_br_a_dw", s["gm"], dya, "tn", F32)
    gw["w_br_b"] = _matmul(tag + "_br_b_dw", s["att"], dyb, "tn", F32)
    gw["w_br_c"] = _matmul(tag + "_br_c_dw", s["gla"], dyc, "tn", F32)
    rg = GMLP_CHUNK
    (du, dv_), (gw["gmlp_norm_g"], gw["w_spatial"], gw["b_spatial_t"]), _ = _rowwise_bwd(
        tag + "_gmlp_b", _f_gmlp, rg, tc // rg, gm_rows, [], gm_params, [], [], [(dgm, GMLP_W)], [BF16, BF16])
    (do, dr), (gw["gla_norm_g"],), _ = _rowwise_bwd(tag + "_glaout_b", _f_glaout, R, nct, s["go_rows"], [],
                                                    [(w["gla_norm_g"], True)], [], [], [(dgla, GLA_V_W)], [F32, None, BF16])
    p = s["p"]
    d_f, d_b, (gw["w2p_f"], gw["b2_f"], gw["w2p_b"], gw["b2_b"]) = _gla_bwd(
        tag + "_gla_b", p, (w["w2p_f"], w["w2p_b"]), (w["b2_f"], w["b2_b"]), (s["s_f"], s["s_b"]), do, R, nct)
    dgq, dgk, dgv_, daf = [(a + b).astype(BF16) for a, b in zip(d_f, d_b)]
    dqr, dkr, dvv, comm_out = _attention_bwd(tag, s["att_saved"], datt, tc, make_comm(gw))
    qk_rows, qk_consts, qk_params, qk_cparams = s["qk"]
    (dq, dk), (gw["q_norm_g"], gw["k_norm_g"]), _ = _rowwise_bwd(
        tag + "_qk_b", _f_qknorm, R, nct, qk_rows, qk_consts, qk_params, qk_cparams, [],
        [(dqr, Q_W), (dkr, KV_W)], [BF16, BF16], pre=_qk_pre)
    dp = jnp.concatenate([dga, dgb, dgc, du, dv_, dq, dgv_, dr, dgq, dgk, dk, dvv.astype(BF16), daf,
                          jnp.zeros((p.shape[0], PW - C_AF - 128), BF16)], axis=1)
    gw["w_in"] = _matmul(tag + "_in_dw", s["h1"], dp, "tn", F32)
    tail = make_tail_comm(gw)
    if tail is None:
        dh1, tail_out = _matmul(tag + "_in_da", dp, w["w_in"], "nt", F32), []
    else:
        dh1, tail_out = _matmul(tag + "_in_da", dp, w["w_in"], "nt", F32, comm=tail)
    (dx,), (gw["norm1_g"],), (dsh1, dsc1) = _rowwise_bwd(
        tag + "_ln1_b", _f_lnmod, R, nct, [(s["x"], D, 0, D)], [], [(w["norm1_g"], False)], [], [sh1, sc1],
        [(dh1, D)], [F32], adds=[dx_mid])
    return dx, gw, (dsh1, dsc1, dg1, dsh2, dsc2, dg2), comm_out, tail_out


_SHARDED = (("w_in", 1, True), ("w_br_a", 1, True), ("w_br_b", 1, True), ("w_br_c", 1, True), ("w_out", 0, True),
            ("w_ffn_up", 1, True), ("w_ffn_down", 0, True), ("conv_w", 1, False), ("w_alpha2", 2, False), ("b_alpha", 1, False))
_REPLICATED = ("c_ctx", "b_ada", "norm1_g", "norm2_g", "q_norm_g", "k_norm_g", "gmlp_norm_g", "w_spatial", "b_spatial",
               "gla_norm_g", "conv_b", "final_norm_g")
_WEIGHTS = ("c_ctx", "w_ada", "b_ada", "norm1_g", "norm2_g", "w_in", "q_norm_g", "k_norm_g", "gmlp_norm_g", "w_spatial",
            "b_spatial", "w_alpha2", "b_alpha", "gla_norm_g", "w_br_a", "w_br_b", "w_br_c", "w_out", "w_ffn_up", "conv_w",
            "conv_b", "w_ffn_down", "final_norm_g")


def _decay_weights(w_alpha2_l, b_alpha_l):
    out = []
    for d in range(2):
        w2p = jnp.zeros((128, GLA_QK_W), F32).at[GLA_RANK * d:GLA_RANK * (d + 1)].set(w_alpha2_l[d])
        out += [w2p, b_alpha_l[d][None, :]]
    return out


def _step(inp, wts, moms, vels):
    x, c, ctx, loss_target = inp
    t, tc = x.shape[1], ctx.shape[1]
    n = t + tc
    R = min(256, tc)
    nct = tc // R
    me = _my_index()
    depth = wts["w_in"].shape[0]

    late = [(nm, ax) for nm, ax, half in _SHARDED if half and nm != "w_in"]
    small = [(nm, ax) for nm, ax, half in _SHARDED if not half]
    w_in_shard = lambda l: wts["w_in"][l].astype(BF16)
    c8 = jnp.concatenate([c, jnp.zeros((7, D), F32)], axis=0)
    first = _exchange("gather_first", [w_in_shard(0)] + [wts[nm] for nm, _ in small] + [c8], gather=True)
    c_all = first[-1][:, 0, :]
    small_all = dict(zip([nm for nm, _ in small], first[1:-1]))

    def early_weights(l, w_in_all):
        w = {"w_in": _win_my_cols([w_in_all[s] for s in range(N_DEV)])}
        conv_w, w_alpha2, b_alpha = [jnp.concatenate([small_all[nm][s, l] for s in range(N_DEV)], axis=ax) for nm, ax in small]
        w["conv_w_h"] = conv_w.reshape(3, 2, FFN_H).transpose(1, 0, 2)
        w["conv_b_h"] = wts["conv_b"][l].reshape(2, 1, FFN_H)
        w["w2p_f"], w["b2_f"], w["w2p_b"], w["b2_b"] = _decay_weights(w_alpha2, b_alpha)
        w["norm1_g"] = wts["norm1_g"][l][None, :]
        w["norm2_g"] = wts["norm2_g"][l][None, :]
        w["q_norm_g"] = wts["q_norm_g"][l][None, :]
        w["k_norm_g"] = wts["k_norm_g"][l][None, :]
        w["gmlp_norm_g"] = wts["gmlp_norm_g"][l].reshape(GMLP_GROUPS, 1, GMLP_W // GMLP_GROUPS)
        w["w_spatial"] = wts["w_spatial"][l]
        w["b_spatial_t"] = wts["b_spatial"][l][:, :, None]
        w["gla_norm_g"] = wts["gla_norm_g"][l].reshape(GLA_HEADS, 1, GLA_DV)
        return w

    craw16 =jnp.concatenate([c_all, wts["c_ctx"][None, :], jnp.zeros((7, D), F32)], axis=0)
    acols = wts["w_ada"].shape[2]
    b_cols = lax.dynamic_slice_in_dim(wts["b_ada"], me * acols, acols, axis=1)[:, None, :]
    mod_part = _adaln_fwd("adaln", craw16, wts["w_ada"], b_cols)
    send = jnp.stack([mod_part[:, 8, :][None].repeat(N_DEV, 0), mod_part[:, :8, :].transpose(1, 0, 2)], axis=2)
    send = jnp.concatenate([send.reshape(N_DEV, 2 * depth, acols), jnp.zeros((N_DEV, 8 - 2 * depth, acols), F32)], axis=1)
    (got,) = _exchange("scatter_mod", [send], gather=False)
    mod = got[:, :2 * depth, :].transpose(1, 0, 2).reshape(depth, 2, N_MOD, 1, D)
    modv = [[mod[l, :, k] for k in range(N_MOD)] for l in range(depth)]

    cosq, sinq = _rope_tables(t, tc)
    consts = (cosq, sinq, _head_mean_matrix(Q_W), _head_tile_matrix(Q_W), _head_mean_matrix(KV_W), _head_tile_matrix(KV_W))
    xs = jnp.concatenate([ctx[0], x[0]], axis=0)
    saved, layers = [], []
    w_in_all = first[0]

    def late_weights(got):
        return {nm: _join_shards(g, ax) for (nm, ax), g in zip(late, got)}

    for l in range(depth):
        layers.append(early_weights(l, w_in_all))
        sending = [wts[nm][l].astype(BF16) for nm, _ in late] + ([w_in_shard(l + 1)] if l + 1 < depth else [])
        xs, sv, got = _layer_fwd("l%d" % l, xs, layers[l], modv[l], consts, R, nct, tc, (sending, True), late_weights)
        if l + 1 < depth:
            w_in_all = got[len(late)]
        saved.append(sv)
    loss_blk, dxs, dgf = _final_loss("final", xs, loss_target[0], wts["final_norm_g"][None, :], R, nct)
    loss = lax.psum(loss_blk[0, 0], ("x", "y", "c"))

    grads = [None] * depth
    dmods = [None] * depth
    late_parts = [None] * depth
    w_in_parts = [None] * depth
    w_in_grad_shards = lambda g: _win_shards(g["w_in"]).astype(BF16)

    def small_grad_shards(g):
        full = dict(conv_w=g["conv_w_h"].transpose(1, 0, 2).reshape(3, F2),
                    w_alpha2=jnp.stack([g["w2p_f"][:GLA_RANK], g["w2p_b"][GLA_RANK:2 * GLA_RANK]]),
                    b_alpha=jnp.stack([g["b2_f"][0], g["b2_b"][0]]))
        return [_split_shards(full[nm], ax) for nm, ax in small]

    for l in range(depth - 1, -1, -1):
        def make_comm(gw, l=l):
            sending = [_split_shards(gw[nm], ax).astype(BF16) for nm, ax in late]
            return sending + ([w_in_grad_shards(grads[l + 1])] if l + 1 < depth else []), False

        def make_tail_comm(gw, l=l):
            if l > 0:
                return None
            per_layer = [small_grad_shards(gw if k == 0 else grads[k]) for k in range(depth)]
            return [w_in_grad_shards(gw)] + [jnp.stack([per_layer[k][i] for k in range(depth)], axis=1) for i in range(len(small))], False

        dxs, grads[l], dmods[l], got, tail = _layer_bwd("l%d" % l, dxs, saved[l], layers[l], modv[l], R, nct, tc,
                                                        make_comm, make_tail_comm)
        late_parts[l] = got[:len(late)]
        if l + 1 < depth:
            w_in_parts[l + 1] = got[len(late)]
    w_in_parts[0], small_parts = tail[0], tail[1:]
    grad_x = dxs[tc:][None]

    dmod = jnp.stack([jnp.stack(dmods[l], axis=1) for l in range(depth)])
    dmod = dmod.reshape(depth, 2, N_DEV, acols).transpose(2, 0, 1, 3).reshape(N_DEV, 2 * depth, acols)
    dmod_send = jnp.concatenate([dmod, jnp.zeros((N_DEV, 8 - 2 * depth, acols), F32)], axis=1)
    (dm_got,) = _exchange("scatter_dmod", [dmod_send], gather=False)
    g_w_ada, dc16 = _adaln_bwd("adaln_b", craw16, wts["w_ada"], dm_got[:, :2 * depth].transpose(1, 0, 2))
    db_ada_part = jnp.stack([jnp.stack(dmods[l], axis=1) for l in range(depth)]).reshape(depth, 2, N_MOD * D).sum(axis=1)

    out = {}
    kinds = ("grad", "delta", "new_m", "new_v")
    view2 = lambda a: a.reshape(-1, a.shape[-1])
    sharded_parts = ([jnp.stack(w_in_parts, axis=1)]
                     + [jnp.stack([late_parts[l][k] for l in range(depth)], axis=1) for k in range(len(late))] + small_parts)
    for (nm, _), parts in zip([("w_in", 1)] + late + small, sharded_parts):
        res = _adamw("adamw_" + nm, view2(wts[nm]), view2(moms[nm]), view2(vels[nm]), parts.reshape(N_DEV, -1, parts.shape[-1]))
        for kind, flat in zip(kinds, res):
            out[kind, nm] = flat.reshape(wts[nm].shape)

    rep_g = dict(
        c_ctx=dc16[8], b_ada=db_ada_part, final_norm_g=dgf[0],
        norm1_g=jnp.stack([grads[l]["norm1_g"][0] for l in range(depth)]),
        norm2_g=jnp.stack([grads[l]["norm2_g"][0] for l in range(depth)]),
        q_norm_g=jnp.stack([grads[l]["q_norm_g"][0] for l in range(depth)]),
        k_norm_g=jnp.stack([grads[l]["k_norm_g"][0] for l in range(depth)]),
        gmlp_norm_g=jnp.stack([grads[l]["gmlp_norm_g"].reshape(GMLP_W) for l in range(depth)]),
        w_spatial=jnp.stack([grads[l]["w_spatial"] for l in range(depth)]),
        b_spatial=jnp.stack([grads[l]["b_spatial_t"][:, :, 0] for l in range(depth)]),
        gla_norm_g=jnp.stack([grads[l]["gla_norm_g"].reshape(GLA_V_W) for l in range(depth)]),
        conv_b=jnp.stack([grads[l]["conv_b_h"].reshape(F2) for l in range(depth)]),
    )
    rep_shapes = [wts[nm].shape for nm in _REPLICATED]
    (rg_parts,) = _exchange("gather_rep_grads", [_pack_rows([rep_g[nm] for nm in _REPLICATED], 16)], gather=True)
    rpk = lambda src: _pack_rows([src[nm] for nm in _REPLICATED], 16)
    res = _adamw("adamw_rep", rpk(wts), rpk(moms), rpk(vels), rg_parts)
    for kind, rows in zip(kinds, res):
        for nm, piece in zip(_REPLICATED, _unpack_rows(rows, rep_shapes)):
            out[kind, nm] = piece

    res = _adamw("adamw_ada", view2(wts["w_ada"]), view2(moms["w_ada"]), view2(vels["w_ada"]), view2(g_w_ada)[None])
    for kind, flat in zip(kinds, res):
        out[kind, "w_ada"] = flat.reshape(wts["w_ada"].shape)

    return (loss, grad_x, *[out[kind, nm] for kind in kinds for nm in _WEIGHTS])


def kernel(x, c, ctx, c_ctx, w_ada, b_ada, norm1_g, norm2_g, w_in, q_norm_g, k_norm_g, gmlp_norm_g, w_spatial, b_spatial, w_alpha2, b_alpha, gla_norm_g, w_br_a, w_br_b, w_br_c, w_out, w_ffn_up, conv_w, conv_b, w_ffn_down, final_norm_g, loss_target, m_c_ctx, m_w_ada, m_b_ada, m_norm1_g, m_norm2_g, m_w_in, m_q_norm_g, m_k_norm_g, m_gmlp_norm_g, m_w_spatial, m_b_spatial, m_w_alpha2, m_b_alpha, m_gla_norm_g, m_w_br_a, m_w_br_b, m_w_br_c, m_w_out, m_w_ffn_up, m_conv_w, m_conv_b, m_w_ffn_down, m_final_norm_g, v_c_ctx, v_w_ada, v_b_ada, v_norm1_g, v_norm2_g, v_w_in, v_q_norm_g, v_k_norm_g, v_gmlp_norm_g, v_w_spatial, v_b_spatial, v_w_alpha2, v_b_alpha, v_gla_norm_g, v_w_br_a, v_w_br_b, v_w_br_c, v_w_out, v_w_ffn_up, v_conv_w, v_conv_b, v_w_ffn_down, v_final_norm_g):
    wts = dict(zip(_WEIGHTS, (c_ctx, w_ada, b_ada, norm1_g, norm2_g, w_in, q_norm_g, k_norm_g, gmlp_norm_g, w_spatial, b_spatial,
                              w_alpha2, b_alpha, gla_norm_g, w_br_a, w_br_b, w_br_c, w_out, w_ffn_up, conv_w, conv_b, w_ffn_down,
                              final_norm_g)))
    moms = dict(zip(_WEIGHTS, (m_c_ctx, m_w_ada, m_b_ada, m_norm1_g, m_norm2_g, m_w_in, m_q_norm_g, m_k_norm_g, m_gmlp_norm_g,
                               m_w_spatial, m_b_spatial, m_w_alpha2, m_b_alpha, m_gla_norm_g, m_w_br_a, m_w_br_b, m_w_br_c, m_w_out,
                               m_w_ffn_up, m_conv_w, m_conv_b, m_w_ffn_down, m_final_norm_g)))
    vels = dict(zip(_WEIGHTS, (v_c_ctx, v_w_ada, v_b_ada, v_norm1_g, v_norm2_g, v_w_in, v_q_norm_g, v_k_norm_g, v_gmlp_norm_g,
                               v_w_spatial, v_b_spatial, v_w_alpha2, v_b_alpha, v_gla_norm_g, v_w_br_a, v_w_br_b, v_w_br_c, v_w_out,
                               v_w_ffn_up, v_conv_w, v_conv_b, v_w_ffn_down, v_final_norm_g)))
    return _step((x, c, ctx, loss_target), wts, moms, vels)
```

```python
import functools

import jax
import jax.numpy as jnp
from jax import lax
from jax.experimental import pallas as pl
from jax.experimental.pallas import tpu as pltpu

F32 = jnp.float32
BF16 = jnp.bfloat16
HI = lax.Precision.HIGHEST
MESH_ID = pl.DeviceIdType.MESH

N_DEV = 8
EPS = 1e-6
D = 1024
N_MOD = 6
HEAD_DIM = 64
N_Q_HEADS = 8
N_KV_HEADS = 2
Q_GROUP = 4
Q_W = 512
KV_W = 128
GRID_W = 64
ROPE_THETA = 10000.0
ROPE_FREQS = 16
GMLP_CHUNK = 128
GMLP_GROUPS = 4
GMLP_W = 512
GLA_HEADS = 4
GLA_QK_W = 256
GLA_V_W = 512
GLA_DK = 64
GLA_DV = 128
GLA_RANK = 16
GLA_TAU = 16.0
GLA_CHUNK = 64
FFN_H = 2816
F2 = 2 * FFN_H
IN_SPLITS = (512, 512, 512, 128, 128, 256, 256, 512, 16, 16, 512, 1024, 1024, 1024)
IN_WIDTH = sum(IN_SPLITS)

C_GA, C_GB, C_GC = 0, 1024, 2048
C_U, C_V, C_Q, C_GV, C_R = 3072, 3584, 4096, 4608, 5120
C_GQ, C_GK = 5632, 5888
C_K, C_VV, C_AF = 6144, 6272, 6400
PW = 6656

ADAM_LR = 0.001
ADAM_B1 = 0.9
ADAM_B2 = 0.999
ADAM_EPS = 1e-08
ADAM_WD = 0.01
ADAM_STEP = 10

V7X_VMEM_LIMIT = 56 * 1024 * 1024

_ARB1 = pltpu.CompilerParams(dimension_semantics=("arbitrary",), vmem_limit_bytes=V7X_VMEM_LIMIT)


def _pick(dim, prefs):
    for p in prefs:
        if dim % p == 0:
            return p
    return dim


def _hdot(a, b, dims=(((1,), (0,)), ((), ()))):
    return lax.dot_general(a, b, dims, precision=HI, preferred_element_type=F32)


_NT = (((1,), (1,)), ((), ()))
_TN = (((0,), (0,)), ((), ()))
_NN = (((1,), (0,)), ((), ()))


def _matmul(name, a, b, mode, out_dtype, *, a_halves=False, b_halves=False, o_halves=False, comm=None):
    def dims2(x, halves):
        return (x.shape[1], 2 * x.shape[2]) if halves else x.shape

    ar, ac = dims2(a, a_halves)
    br, bc = dims2(b, b_halves)
    if mode == "nn":
        M, K, N = ar, ac, bc
    elif mode == "nt":
        M, K, N = ar, ac, br
    else:
        M, K, N = ac, ar, bc
    n_unit = N // 2 if (o_halves or (b_halves and mode != "nt")) else N
    k_unit = K // 2 if (a_halves and mode != "tn") else K
    tokens = lambda d: d > 7168
    if tokens(K):
        tm = _pick(M, (512, 1408, 256, 128))
        tk = _pick(k_unit, (2816, 1408, 768, 512, 384, 256, 128))
    else:
        tm = _pick(M, (768, 512, 384, 256, 128) if tokens(M) else (1024, 1408, 512, 256, 128))
        tk = _pick(k_unit, (3328, 2816, 1664, 1408, 1024, 512, 256, 128))
    tn = _pick(n_unit, (1664, 1408, 1024, 512, 256, 128))
    nk = K // tk

    def spec(shape2, halves, blk, imap):
        if not halves:
            return pl.BlockSpec(blk, imap)
        nhalf = (shape2[1] // 2) // blk[1]

        def im(i, j, k):
            r, c = imap(i, j, k)
            return (c // nhalf, r, c % nhalf)
        return pl.BlockSpec((None,) + blk, im)

    if mode == "nn":
        a_spec = spec((ar, ac), a_halves, (tm, tk), lambda i, j, k: (i, k))
        b_spec = spec((br, bc), b_halves, (tk, tn), lambda i, j, k: (k, j))
        dn = _NN
    elif mode == "nt":
        a_spec = spec((ar, ac), a_halves, (tm, tk), lambda i, j, k: (i, k))
        b_spec = spec((br, bc), b_halves, (tn, tk), lambda i, j, k: (j, k))
        dn = _NT
    else:
        a_spec = spec((ar, ac), a_halves, (tk, tm), lambda i, j, k: (k, i))
        b_spec = spec((br, bc), b_halves, (tk, tn), lambda i, j, k: (k, j))
        dn = _TN
    o_spec = spec((M, N), o_halves, (tm, tn), lambda i, j, k: (i, j))
    o_shape = (2, M, N // 2) if o_halves else (M, N)

    def body(a_ref, b_ref, o_ref, acc_ref):
        k = pl.program_id(2)
        part = lax.dot_general(a_ref[...], b_ref[...], dn, preferred_element_type=F32)
        if nk == 1:
            o_ref[...] = part.astype(o_ref.dtype)
        else:
            @pl.when(k == 0)
            def _():
                acc_ref[...] = part

            @pl.when(k > 0)
            def _():
                acc_ref[...] += part

            @pl.when(k == nk - 1)
            def _():
                o_ref[...] = acc_ref[...].astype(o_ref.dtype)

    grid = (M // tm, N // tn, nk)
    body, xspecs, xshapes, xsems = _with_exchange(body, 2, 1, grid, comm)
    res = pl.pallas_call(
        body, name=name, grid=grid,
        in_specs=[a_spec, b_spec] + xspecs, out_specs=[o_spec] + xspecs,
        out_shape=[jax.ShapeDtypeStruct(o_shape, out_dtype)] + xshapes,
        scratch_shapes=[pltpu.VMEM((tm, tn), F32)] + xsems,
        compiler_params=pltpu.CompilerParams(dimension_semantics=("arbitrary", "arbitrary", "arbitrary"),
                                             vmem_limit_bytes=V7X_VMEM_LIMIT),
    )(a, b, *(comm[0] if comm else []))
    return res[0] if comm is None else (res[0], list(res[1:]))


def _full_spec(shape):
    nd = len(shape)
    return pl.BlockSpec(tuple(shape), lambda i, _nd=nd: (0,) * _nd)


def _row_spec(R, W, cb):
    return pl.BlockSpec((R, W), lambda i, _cb=cb: (i, _cb))


def _load_rows(refs, specs):
    vals = []
    for ref, (_, W, _, pw) in zip(refs, specs):
        if pw == W:
            vals.append(ref[...].astype(F32))
        else:
            vals.append([ref[:, k * pw:(k + 1) * pw].astype(F32) for k in range(W // pw)])
    return vals


def _load_params(refs, specs):
    vals = []
    for ref, (arr, split) in zip(refs, specs):
        if split:
            vals.append([ref[k] for k in range(arr.shape[0])])
        else:
            vals.append(ref[...])
    return vals


def _mod_spec(nct, width):
    return pl.BlockSpec((None, 1, width), lambda i: (jnp.minimum(i // nct, 1), 0, 0))


def _rowwise(name, f, R, nct, rows, consts, params, cparams, mods, outs, post=None):
    n = rows[0][0].shape[0]
    nr, nc, npar, ncp, nm = len(rows), len(consts), len(params), len(cparams), len(mods)

    def body(*refs):
        pos = 0
        rr = refs[pos:pos + nr]; pos += nr
        cr = refs[pos:pos + nc]; pos += nc
        pr = refs[pos:pos + npar]; pos += npar
        cpr = refs[pos:pos + ncp]; pos += ncp
        mr = refs[pos:pos + nm]; pos += nm
        orefs = refs[pos:]
        res = f(_load_rows(rr, rows), _load_params(pr, params), [m[...] for m in mr],
                _load_rows(cr, consts), _load_params(cpr, cparams))
        if post is not None:
            res = post(res, _load_rows(cr, consts))
        for o_ref, r in zip(orefs, res):
            o_ref[...] = r.astype(o_ref.dtype)

    in_specs = ([_row_spec(R, W, cb) for (_, W, cb, _) in rows + consts]
                + [_full_spec(a.shape) for (a, _) in params + cparams]
                + [_mod_spec(nct, m.shape[2]) for m in mods])
    args = [a for (a, _, _, _) in rows + consts] + [a for (a, _) in params + cparams] + list(mods)
    return pl.pallas_call(
        body, name=name, grid=(n // R,), in_specs=in_specs,
        out_specs=[_row_spec(R, w, 0) for (w, _) in outs],
        out_shape=[jax.ShapeDtypeStruct((n, w), dt) for (w, dt) in outs],
        compiler_params=_ARB1,
    )(*args)


def _rowwise_bwd(name, f, R, nct, rows, consts, params, cparams, mods, douts, drow, adds=None, pre=None):
    n = rows[0][0].shape[0]
    adds = adds or [None] * len(rows)
    nr, nc, npar, ncp, nm, nd = len(rows), len(consts), len(params), len(cparams), len(mods), len(douts)
    add_ix = [k for k in range(nr) if adds[k] is not None]
    out_ix = [k for k in range(nr) if drow[k] is not None]

    def body(*refs):
        i = pl.program_id(0)
        pos = 0
        rr = refs[pos:pos + nr]; pos += nr
        cr = refs[pos:pos + nc]; pos += nc
        pr = refs[pos:pos + npar]; pos += npar
        cpr = refs[pos:pos + ncp]; pos += ncp
        mr = refs[pos:pos + nm]; pos += nm
        dr = refs[pos:pos + nd]; pos += nd
        ar = refs[pos:pos + len(add_ix)]; pos += len(add_ix)
        drr = refs[pos:pos + len(out_ix)]; pos += len(out_ix)
        dpr = refs[pos:pos + npar]; pos += npar
        dmr = refs[pos:pos + nm]; pos += nm

        cv = _load_rows(cr, consts)
        cpv = _load_params(cpr, cparams)
        _, vjp = jax.vjp(lambda rv, pv, mv: f(rv, pv, mv, cv, cpv),
                         _load_rows(rr, rows), _load_params(pr, params), [m[...] for m in mr])
        dv = [d[...].astype(F32) for d in dr]
        if pre is not None:
            dv = pre(dv, cv)
        g_rows, g_params, g_mods = vjp(tuple(dv))

        for ref, k in zip(drr, out_ix):
            _, W, _, pw = rows[k]
            g = g_rows[k]
            extra = ar[add_ix.index(k)] if k in add_ix else None
            if pw == W:
                if extra is not None:
                    g = g + extra[...].astype(F32)
                ref[...] = g.astype(ref.dtype)
            else:
                for q in range(W // pw):
                    gq = g[q]
                    if extra is not None:
                        gq = gq + extra[:, q * pw:(q + 1) * pw].astype(F32)
                    ref[:, q * pw:(q + 1) * pw] = gq.astype(ref.dtype)

        @pl.when(i == 0)
        def _():
            for ref in dpr:
                ref[...] = jnp.zeros_like(ref)

        for ref, (arr, split), g in zip(dpr, params, g_params):
            if split:
                for k in range(arr.shape[0]):
                    ref[k] += g[k]
            else:
                ref[...] += g

        @pl.when((i == 0) | (i == nct))
        def _():
            for ref in dmr:
                ref[...] = jnp.zeros_like(ref)

        for ref, g in zip(dmr, g_mods):
            ref[...] += g

    in_specs = ([_row_spec(R, W, cb) for (_, W, cb, _) in rows + consts]
                + [_full_spec(a.shape) for (a, _) in params + cparams]
                + [_mod_spec(nct, m.shape[2]) for m in mods]
                + [_row_spec(R, W, 0) for (_, W) in douts]
                + [_row_spec(R, rows[k][1], 0) for k in add_ix])
    args = ([a for (a, _, _, _) in rows + consts] + [a for (a, _) in params + cparams] + list(mods)
            + [a for (a, _) in douts] + [adds[k] for k in add_ix])
    out_specs = ([_row_spec(R, rows[k][1], 0) for k in out_ix]
                 + [_full_spec(a.shape) for (a, _) in params]
                 + [_mod_spec(nct, m.shape[2]) for m in mods])
    out_shape = ([jax.ShapeDtypeStruct((n, rows[k][1]), drow[k]) for k in out_ix]
                 + [jax.ShapeDtypeStruct(a.shape, F32) for (a, _) in params]
                 + [jax.ShapeDtypeStruct(m.shape, F32) for m in mods])
    res = pl.pallas_call(
        body, name=name, grid=(n // R,), in_specs=in_specs, out_specs=out_specs, out_shape=out_shape,
        compiler_params=_ARB1,
    )(*args)
    no = len(out_ix)
    return list(res[:no]), list(res[no:no + npar]), list(res[no + npar:])


def _rms(x, g):
    return x * lax.rsqrt(jnp.mean(x * x, axis=-1, keepdims=True) + EPS) * g


def _f_lnmod(rv, pv, mv, cv, cpv):
    (x,), (g,), (shift, scale) = rv, pv, mv
    return (_rms(x, g) * (1.0 + scale) + shift,)


def _f_resid(rv, pv, mv, cv, cpv):
    (x, y), (gate,) = rv, mv
    return (x + gate * y,)


def _f_merge(rv, pv, mv, cv, cpv):
    ga, gb, gc, ya, yb, yc = rv
    return (_sigmoid(ga) * ya + _sigmoid(gb) * yb + _sigmoid(gc) * yc,)


def _split3(x):
    hi = x.astype(BF16)
    rest = x - hi.astype(F32)
    mid = rest.astype(BF16)
    return hi, mid, (rest - mid.astype(F32)).astype(BF16)


def _dot3_right(x, m):
    mb = m.astype(BF16)
    return sum(lax.dot_general(piece, mb, _NN, preferred_element_type=F32) for piece in _split3(x))


@jax.custom_vjp
def _sym_dot(x, m):
    return _dot3_right(x, m)


_sym_dot.defvjp(lambda x, m: (_dot3_right(x, m), m), lambda m, g: (_dot3_right(g, m), jnp.zeros_like(m)))


def _f_qknorm(rv, pv, mv, cv, cpv):
    (q, k), (gq, gk), (bdq, eq, bdk, ek) = rv, pv, cpv
    qn = q * lax.rsqrt(_sym_dot(q * q, bdq) + EPS) * _hdot(gq, eq)
    kn = k * lax.rsqrt(_sym_dot(k * k, bdk) + EPS) * _hdot(gk, ek)
    return (qn, kn)


def _rope(x, cos, sin):
    w = x.shape[1]
    lane = lax.broadcasted_iota(jnp.int32, x.shape, 1)
    partner = jnp.where((lane & 31) < 16, pltpu.roll(x, w - 16, 1), pltpu.roll(x, 16, 1))
    return x * cos + partner * sin


_LOG2E = 1.4426950408889634
_LN2 = 0.6931471805599453


def _qk_post(res, cv):
    (qn, kn), (cq, sq, ck, sk) = res, cv
    return (_rope(qn, cq, sq) * (HEAD_DIM ** -0.5 * _LOG2E), _rope(kn, ck, sk))


def _qk_pre(dv, cv):
    (dq, dk), (cq, sq, ck, sk) = dv, cv
    return (_rope(dq * (HEAD_DIM ** -0.5), cq, -sq), _rope(dk * _LN2, ck, -sk))


def _f_gmlp(rv, pv, mv, cv, cpv):
    (u, vs), (ng, ws, bt) = rv, pv
    pieces = []
    for g in range(GMLP_GROUPS):
        vn = _rms(jax.nn.gelu(vs[g]), ng[g])
        pieces.append(_bdot(ws[g], vn, "nn") + bt[g])
    return (jax.nn.gelu(u) * jnp.concatenate(pieces, axis=1),)


def _f_glaout(rv, pv, mv, cv, cpv):
    (ofs, obs, rs), (gn,) = rv, pv
    pieces = [_rms(ofs[h] + obs[h], gn[h]) * (rs[h] * _sigmoid(rs[h])) for h in range(GLA_HEADS)]
    return (jnp.concatenate(pieces, axis=1),)


_CONV_CB = 1408
_CONV_STRIP = 128


def _sigmoid(x):
    return 0.5 * jnp.tanh(0.5 * x) + 0.5


def _halo_keep(i, R, tc, n):
    first, end = i * R, (i + 1) * R
    keep_prev = jnp.where((first == 0) | (first == tc), 0.0, 1.0)
    keep_next = jnp.where((end == tc) | (end == n), 0.0, 1.0)
    return keep_prev, keep_next


def _conv_specs(R, n):
    nb8 = n // 8
    main = pl.BlockSpec((2, R, _CONV_CB), lambda j, i: (0, i, j))
    prev = pl.BlockSpec((2, 8, _CONV_CB), lambda j, i: (0, jnp.maximum(i * (R // 8) - 1, 0), j))
    nxt = pl.BlockSpec((2, 8, _CONV_CB), lambda j, i: (0, jnp.minimum((i + 1) * (R // 8), nb8 - 1), j))
    cw = pl.BlockSpec((2, 3, _CONV_CB), lambda j, i: (0, 0, j))
    cb = pl.BlockSpec((2, 1, _CONV_CB), lambda j, i: (0, 0, j))
    return main, prev, nxt, cw, cb


def _row_before(x):
    return pltpu.roll(x, 1, 0)


def _row_after(x):
    return pltpu.roll(x, x.shape[0] - 1, 0)


def _conv_gate(name, a2, cw, cb, R, tc):
    n = a2.shape[1]
    main, prev, nxt, cws, cbs = _conv_specs(R, n)

    def body(a_ref, p_ref, n_ref, cw_ref, cb_ref, o_ref):
        keep_prev, keep_next = _halo_keep(pl.program_id(1), R, tc, n)

        def strip(c, carry):
            ls = pl.ds(pl.multiple_of(c * _CONV_STRIP, _CONV_STRIP), _CONV_STRIP)
            acts = []
            for h in range(2):
                win = jnp.concatenate([p_ref[h, :, ls] * keep_prev, a_ref[h, :, ls], n_ref[h, :, ls] * keep_next], axis=0)
                acts.append((cb_ref[h, :, ls] + cw_ref[h, 1:2, ls] * win + cw_ref[h, 0:1, ls] * _row_before(win)
                             + cw_ref[h, 2:3, ls] * _row_after(win))[8:8 + R])
            g, v = acts
            o_ref[:, ls] = (g * _sigmoid(g) * v).astype(o_ref.dtype)
            return carry

        lax.fori_loop(0, _CONV_CB // _CONV_STRIP, strip, 0)

    return pl.pallas_call(
        body, name=name, grid=(FFN_H // _CONV_CB, n // R),
        in_specs=[main, prev, nxt, cws, cbs],
        out_specs=pl.BlockSpec((R, _CONV_CB), lambda j, i: (i, j)),
        out_shape=jax.ShapeDtypeStruct((n, FFN_H), BF16),
        compiler_params=pltpu.CompilerParams(dimension_semantics=("arbitrary", "arbitrary"),
                                             vmem_limit_bytes=V7X_VMEM_LIMIT),
    )(a2, a2, a2, cw, cb)


def _conv_gate_bwd(name, a2, dgv, cw, cb, R, tc):
    n = a2.shape[1]
    nb8 = n // 8
    main, prev, nxt, cws, cbs = _conv_specs(R, n)
    d_main = pl.BlockSpec((R, _CONV_CB), lambda j, i: (i, j))
    d_prev = pl.BlockSpec((8, _CONV_CB), lambda j, i: (jnp.maximum(i * (R // 8) - 1, 0), j))
    d_next = pl.BlockSpec((8, _CONV_CB), lambda j, i: (jnp.minimum((i + 1) * (R // 8), nb8 - 1), j))
    mid = slice(8, 8 + R)

    def body(a_ref, p_ref, n_ref, cw_ref, cb_ref, d_ref, dp_ref, dn_ref, da_ref, dcw_ref, dcb_ref):
        i = pl.program_id(1)
        keep_prev, keep_next = _halo_keep(i, R, tc, n)

        @pl.when(i == 0)
        def _():
            dcw_ref[...] = jnp.zeros_like(dcw_ref)
            dcb_ref[...] = jnp.zeros_like(dcb_ref)

        def strip(c, carry):
            ls = pl.ds(pl.multiple_of(c * _CONV_STRIP, _CONV_STRIP), _CONV_STRIP)
            cws_ = [[cw_ref[h, k:k + 1, ls] for k in range(3)] for h in range(2)]
            wins = [jnp.concatenate([p_ref[h, :, ls] * keep_prev, a_ref[h, :, ls], n_ref[h, :, ls] * keep_next], axis=0)
                    for h in range(2)]
            g, v = [cb_ref[h, :, ls] + cws_[h][1] * wins[h] + cws_[h][0] * _row_before(wins[h]) + cws_[h][2] * _row_after(wins[h])
                    for h in range(2)]
            dout = jnp.concatenate([dp_ref[:, ls].astype(F32) * keep_prev, d_ref[:, ls].astype(F32),
                                    dn_ref[:, ls].astype(F32) * keep_next], axis=0)
            sg = _sigmoid(g)
            das = [dout * v * sg * (1.0 + g * (1.0 - sg)), dout * g * sg]
            for h in range(2):
                da_win = das[h]
                da_ref[h, :, ls] = (cws_[h][1] * da_win + cws_[h][0] * _row_after(da_win)
                                    + cws_[h][2] * _row_before(da_win))[mid].astype(da_ref.dtype)
                da = da_win[mid]
                dcw_ref[h, 0:1, ls] += jnp.sum(da * _row_before(wins[h])[mid], axis=0, keepdims=True)
                dcw_ref[h, 1:2, ls] += jnp.sum(da * wins[h][mid], axis=0, keepdims=True)
                dcw_ref[h, 2:3, ls] += jnp.sum(da * _row_after(wins[h])[mid], axis=0, keepdims=True)
                dcb_ref[h, :, ls] += jnp.sum(da, axis=0, keepdims=True)
            return carry

        lax.fori_loop(0, _CONV_CB // _CONV_STRIP, strip, 0)

    return pl.pallas_call(
        body, name=name, grid=(FFN_H // _CONV_CB, n // R),
        in_specs=[main, prev, nxt, cws, cbs, d_main, d_prev, d_next],
        out_specs=[main, cws, cbs],
        out_shape=[jax.ShapeDtypeStruct((2, n, FFN_H), BF16), jax.ShapeDtypeStruct((2, 3, FFN_H), F32),
                   jax.ShapeDtypeStruct((2, 1, FFN_H), F32)],
        compiler_params=pltpu.CompilerParams(dimension_semantics=("arbitrary", "arbitrary"),
                                             vmem_limit_bytes=V7X_VMEM_LIMIT),
    )(a2, a2, a2, cw, cb, dgv, dgv, dgv)


_ARB2 = pltpu.CompilerParams(dimension_semantics=("arbitrary", "arbitrary"), vmem_limit_bytes=V7X_VMEM_LIMIT)


def _with_exchange(body, n_in, n_out, grid, comm):
    if comm is None:
        return body, [], [], []
    xs, gather = comm
    na = len(xs)
    specs, shapes, sems = _exchange_io(xs, gather)

    def wrapped(*refs):
        ins, x_refs = refs[:n_in], refs[n_in:n_in + na]
        outs, o_refs = refs[n_in + na:n_in + na + n_out], refs[n_in + na + n_out:n_in + 2 * na + n_out]
        scratch, sem_refs = refs[n_in + 2 * na + n_out:-3], refs[-3:]
        start, wait = _exchange_plan(x_refs, o_refs, *sem_refs, gather)
        ids = [pl.program_id(d) for d in range(len(grid))]
        first, last = ids[0] == 0, ids[0] == grid[0] - 1
        for d in range(1, len(grid)):
            first, last = first & (ids[d] == 0), last & (ids[d] == grid[d] - 1)

        @pl.when(first)
        def _():
            start()

        body(*ins, *outs, *scratch)

        @pl.when(last)
        def _():
            wait()

    return wrapped, specs, shapes, sems


def _attn_fwd(name, qt8, k2, vt2, comm=None):
    nq, nk = qt8.shape[2], k2.shape[1]
    tq = _pick(nq, (256, 128))
    tk = _pick(nk, (1408, 768, 256, 128))

    va = vt2.shape[1]

    def body(qt_ref, k_ref, vt_ref, ot_ref, lse_ref, m_ref, acc_ref):
        m_ref[...] = jnp.full((Q_GROUP, 1, tq), -1e30, F32)
        acc_ref[...] = jnp.zeros((Q_GROUP, va, tq), F32)

        def step(j, carry):
            sl = pl.ds(pl.multiple_of(j * tk, tk), tk)
            kj = k_ref[sl, :]
            vtj = vt_ref[:, sl]
            sts = [lax.dot_general(kj, qt_ref[h], _NN, preferred_element_type=F32) for h in range(Q_GROUP)]
            m_old = [m_ref[h] for h in range(Q_GROUP)]
            acc_old = [acc_ref[h] for h in range(Q_GROUP)]
            m_new = [jnp.maximum(m_old[h], jnp.max(sts[h], axis=0, keepdims=True)) for h in range(Q_GROUP)]
            pts = [jnp.exp2(sts[h] - m_new[h]).astype(BF16) for h in range(Q_GROUP)]
            pvs = [lax.dot_general(vtj, pts[h], _NN, preferred_element_type=F32) for h in range(Q_GROUP)]
            for h in range(Q_GROUP):
                acc_ref[h] = jnp.exp2(m_old[h] - m_new[h]) * acc_old[h] + pvs[h]
                m_ref[h] = m_new[h]
            return carry

        lax.fori_loop(0, nk // tk, step, 0, unroll=2)
        for h in range(Q_GROUP):
            l = acc_ref[h, HEAD_DIM:HEAD_DIM + 1, :]
            ot_ref[h] = (acc_ref[h, 0:HEAD_DIM, :] / l).astype(ot_ref.dtype)
            lse_ref[h] = m_ref[h] + jnp.log2(l)

    qspec = pl.BlockSpec((Q_GROUP, HEAD_DIM, tq), lambda g, i: (g, 0, i))
    lspec = pl.BlockSpec((Q_GROUP, 1, tq), lambda g, i: (g, 0, i))
    grid = (N_KV_HEADS, nq // tq)
    body, xspecs, xshapes, xsems = _with_exchange(body, 3, 2, grid, comm)
    res = pl.pallas_call(
        body, name=name, grid=grid,
        in_specs=[qspec, pl.BlockSpec((None, nk, HEAD_DIM), lambda g, i: (g, 0, 0)),
                  pl.BlockSpec((None, va, nk), lambda g, i: (g, 0, 0))] + xspecs,
        out_specs=[qspec, lspec] + xspecs,
        out_shape=[jax.ShapeDtypeStruct((N_Q_HEADS, HEAD_DIM, nq), BF16), jax.ShapeDtypeStruct((N_Q_HEADS, 1, nq), F32)] + xshapes,
        scratch_shapes=[pltpu.VMEM((Q_GROUP, 1, tq), F32), pltpu.VMEM((Q_GROUP, va, tq), F32)] + xsems,
        compiler_params=_ARB2,
    )(qt8, k2, vt2, *(comm[0] if comm else []))
    return res[0], res[1], list(res[2:])


def _attn_bwd(name, qt8, k2, v2, kt2, ot8, dot8, lse, dkt0, dvt0, comm=None):
    nq, nk = qt8.shape[2], k2.shape[1]
    n0 = dkt0.shape[2]
    tq = _pick(nq, (256, 128))
    tk = _pick(nk, (1408, 768, 256, 128))
    ts = tk
    heads = range(Q_GROUP)

    def body(qt_ref, k_ref, v_ref, kt_ref, ot_ref, dot_ref, lse_ref, dk0_ref, dv0_ref, dqt_ref, dkt_ref, dvt_ref, dl_ref, dq_acc):
        @pl.when(pl.program_id(1) == 0)
        def _():
            dkt_ref[...] = jnp.zeros_like(dkt_ref)
            dvt_ref[...] = jnp.zeros_like(dvt_ref)
            dkt_ref[:, 0:n0] = dk0_ref[...]
            dvt_ref[:, 0:n0] = dv0_ref[...]

        for h in heads:
            dl_ref[h] = jnp.sum(dot_ref[h].astype(F32) * ot_ref[h].astype(F32), axis=0, keepdims=True)
        dq_acc[...] = jnp.zeros((Q_GROUP, HEAD_DIM, tq), F32)

        items = [(s, h) for s in range(tk // ts) for h in heads]

        def step(j, carry):
            def keys(s):
                return pl.ds(pl.multiple_of(j * tk + s * ts, ts), ts)

            def scores(item):
                s, h = item
                return (lax.dot_general(k_ref[keys(s), :], qt_ref[h], _NN, preferred_element_type=F32),
                        lax.dot_general(v_ref[keys(s), :], dot_ref[h], _NN, preferred_element_type=F32))

            nxt = scores(items[0])
            for n, (s, h) in enumerate(items):
                st, dpt = nxt
                if n + 1 < len(items):
                    nxt = scores(items[n + 1])
                pt = jnp.exp2(st - lse_ref[h])
                dst = (pt * (dpt - dl_ref[h])).astype(BF16)
                dq_acc[h] += lax.dot_general(kt_ref[:, keys(s)], dst, _NN, preferred_element_type=F32)
                dv_h = lax.dot_general(dot_ref[h], pt.astype(BF16), _NT, preferred_element_type=F32)
                dk_h = lax.dot_general(qt_ref[h], dst, _NT, preferred_element_type=F32)
                dvt_s, dkt_s = (dv_h, dk_h) if h == 0 else (dvt_s + dv_h, dkt_s + dk_h)
                if h == Q_GROUP - 1:
                    dvt_ref[:, keys(s)] += dvt_s
                    dkt_ref[:, keys(s)] += dkt_s
            return carry

        lax.fori_loop(0, nk // tk, step, 0)
        dqt_ref[...] = dq_acc[...]

    tspec = pl.BlockSpec((Q_GROUP, HEAD_DIM, tq), lambda g, i: (g, 0, i))
    lspec = pl.BlockSpec((Q_GROUP, 1, tq), lambda g, i: (g, 0, i))
    kspec = pl.BlockSpec((None, nk, HEAD_DIM), lambda g, i: (g, 0, 0))
    ktspec = pl.BlockSpec((None, HEAD_DIM, nk), lambda g, i: (g, 0, 0))
    k0spec = pl.BlockSpec((None, HEAD_DIM, n0), lambda g, i: (g, 0, 0))
    grid = (N_KV_HEADS, nq // tq)
    body, xspecs, xshapes, xsems = _with_exchange(body, 9, 3, grid, comm)
    res = pl.pallas_call(
        body, name=name, grid=grid,
        in_specs=[tspec, kspec, kspec, ktspec, tspec, tspec, lspec, k0spec, k0spec] + xspecs,
        out_specs=[tspec, ktspec, ktspec] + xspecs,
        out_shape=[jax.ShapeDtypeStruct((N_Q_HEADS, HEAD_DIM, nq), F32), jax.ShapeDtypeStruct((N_KV_HEADS, HEAD_DIM, nk), F32),
                   jax.ShapeDtypeStruct((N_KV_HEADS, HEAD_DIM, nk), F32)] + xshapes,
        scratch_shapes=[pltpu.VMEM((Q_GROUP, 1, tq), F32), pltpu.VMEM((Q_GROUP, HEAD_DIM, tq), F32)] + xsems,
        compiler_params=_ARB2,
    )(qt8, k2, v2, kt2, ot8, dot8, lse, dkt0, dvt0, *(comm[0] if comm else []))
    return res[0], res[1], res[2], list(res[3:])


def _log_sigmoid(z):
    return jnp.minimum(z, 0.0) - jnp.log(1.0 + jnp.exp(-jnp.abs(z)))


_BDOT_DIMS = {"nn": _NN, "nt": _NT, "tn": _TN}
_BDOT_BWD = {"nn": (("nt", "gb"), ("tn", "ag")), "nt": (("nn", "gb"), ("tn", "ga")), "tn": (("nt", "bg"), ("nn", "ag"))}


def _bdot_raw(a, b, mode):
    return lax.dot_general(a.astype(BF16), b.astype(BF16), _BDOT_DIMS[mode], preferred_element_type=F32)


@functools.partial(jax.custom_vjp, nondiff_argnums=(2,))
def _bdot(a, b, mode):
    return _bdot_raw(a, b, mode)


def _bdot_fwd(a, b, mode):
    return _bdot_raw(a, b, mode), (a.astype(BF16), b.astype(BF16))


def _bdot_bwd(mode, res, g):
    ops = {"a": res[0], "b": res[1], "g": g}
    (ma, oa), (mb, ob) = _BDOT_BWD[mode]
    return _bdot_raw(ops[oa[0]], ops[oa[1]], ma), _bdot_raw(ops[ob[0]], ops[ob[1]], mb)


_bdot.defvjp(_bdot_fwd, _bdot_bwd)


def _tile_tri(rev, rows):
    r_i = lax.broadcasted_iota(jnp.int32, (rows, rows), 0)
    c_i = lax.broadcasted_iota(jnp.int32, (rows, rows), 1)
    same = (r_i // GLA_CHUNK) == (c_i // GLA_CHUNK)
    return same & ((c_i >= r_i) if rev else (c_i <= r_i))


def _tri_dot(rev, x):
    tri = _tile_tri(rev, x.shape[0]).astype(BF16)
    return sum(lax.dot_general(tri, piece, _NN, preferred_element_type=F32) for piece in _split3(x))


@functools.partial(jax.custom_vjp, nondiff_argnums=(1,))
def _chunk_cumsum(x, rev):
    return _tri_dot(rev, x)


_chunk_cumsum.defvjp(lambda x, rev: (_tri_dot(rev, x), None), lambda rev, _, g: (_tri_dot(not rev, g),))


_GLA_REV = (False, True)


def _gla_tile_pair(qs, ks, vss, as_, w2s, b2s, states):
    both = range(2)
    rows = qs[0].shape[0]
    nch = rows // GLA_CHUNK
    tris = [_tile_tri(_GLA_REV[d], rows) for d in both]
    chunk_of_row = lax.broadcasted_iota(jnp.int32, (rows, 1), 0) // GLA_CHUNK
    in_chunk = [(chunk_of_row == c).astype(F32) for c in range(nch)]
    las = [_log_sigmoid(_bdot(as_[d], w2s[d], "nn") + b2s[d]) * (1.0 / GLA_TAU) for d in both]
    cums = [_chunk_cumsum(las[d], _GLA_REV[d]) for d in both]
    tots = [[jnp.sum(las[d] * in_chunk[c], axis=0, keepdims=True) for c in range(nch)] for d in both]
    tot_rows = [sum(in_chunk[c] * tots[d][c] for c in range(nch)) for d in both]
    q_in = [qs[d] * (GLA_DK ** -0.5) * jnp.exp(cums[d]) for d in both]
    k_in = [ks[d] * jnp.exp(-cums[d]) for d in both]
    k_st = [ks[d] * jnp.exp(tot_rows[d] - cums[d]) for d in both]
    lane = lax.broadcasted_iota(jnp.int32, (1, GLA_QK_W), 1)
    outs = [[], []]
    for h in range(GLA_HEADS):
        head = ((lane >= GLA_DK * h) & (lane < GLA_DK * (h + 1))).astype(F32)
        atts = [jnp.where(tris[d], _bdot(q_in[d] * head, k_in[d], "nt"), 0.0) for d in both]
        for d in both:
            outs[d].append(_bdot(atts[d], vss[d][h], "nn"))
    os_ = [jnp.concatenate(outs[d], axis=1) for d in both]
    hr = lax.broadcasted_iota(jnp.int32, (GLA_V_W, GLA_QK_W), 0) // GLA_DV
    hc = lax.broadcasted_iota(jnp.int32, (GLA_V_W, GLA_QK_W), 1) // GLA_DK
    same_head = (hr == hc).astype(F32)
    v_all = [jnp.concatenate(vss[d], axis=1) for d in both]
    states = list(states)
    for step in range(nch):
        chunk = (step, nch - 1 - step)
        us = [_bdot(v_all[d], k_st[d] * in_chunk[chunk[d]], "tn") * same_head for d in both]
        for d in both:
            os_[d] = os_[d] + _bdot(q_in[d] * in_chunk[chunk[d]], states[d], "nt")
            states[d] = jnp.exp(tots[d][chunk[d]]) * states[d] + us[d]
    return os_, states


def _gla_tile_of(step, rev, nct, nt):
    if not rev:
        return step
    return jnp.where(step < nct, nct - 1 - step, nt - 1 - (step - nct))


def _gla_row_specs(R, tile):
    return [pl.BlockSpec((R, GLA_QK_W), lambda s: (tile(s), C_GQ // GLA_QK_W)),
            pl.BlockSpec((R, GLA_QK_W), lambda s: (tile(s), C_GK // GLA_QK_W)),
            pl.BlockSpec((R, GLA_V_W), lambda s: (tile(s), C_GV // GLA_V_W)),
            pl.BlockSpec((R, 128), lambda s: (tile(s), C_AF // 128))]


_GLA_WIDTHS = (GLA_QK_W, GLA_QK_W, GLA_V_W, 128)


def _gla_load(refs):
    q_ref, k_ref, v_ref, a_ref = refs
    return q_ref[...], k_ref[...], [v_ref[:, GLA_DV * h:GLA_DV * (h + 1)] for h in range(GLA_HEADS)], a_ref[...]


def _gla_fwd(name, p, w2s, b2s, R, nct):
    n = p.shape[0]
    nt = n // R
    tiles = [lambda s, d=d: _gla_tile_of(s, _GLA_REV[d], nct, nt) for d in range(2)]

    def body(*refs):
        rows, (w2f, b2f, w2b, b2b), (of, ob, sf, sb), states = refs[:8], refs[8:12], refs[12:16], refs[16:]

        @pl.when(pl.program_id(0) == 0)
        def _():
            for st in states:
                st[...] = jnp.zeros_like(st)

        ins = [_gla_load(rows[:4]), _gla_load(rows[4:])]
        s_in = [states[0][...], states[1][...]]
        sf[...], sb[...] = s_in
        os_, s_out = _gla_tile_pair([i[0] for i in ins], [i[1] for i in ins], [i[2] for i in ins], [i[3] for i in ins],
                                    [w2f[...], w2b[...]], [b2f[...], b2b[...]], s_in)
        of[...], ob[...] = os_
        states[0][...], states[1][...] = s_out

    in_specs = (_gla_row_specs(R, tiles[0]) + _gla_row_specs(R, tiles[1])
                + [_full_spec(w2s[0].shape), _full_spec(b2s[0].shape), _full_spec(w2s[1].shape), _full_spec(b2s[1].shape)])
    o_specs = [pl.BlockSpec((R, GLA_V_W), lambda s, t=t: (t(s), 0)) for t in tiles]
    s_specs = [pl.BlockSpec((None, GLA_V_W, GLA_QK_W), lambda s, t=t: (t(s), 0, 0)) for t in tiles]
    return pl.pallas_call(
        body, name=name, grid=(nt,), in_specs=in_specs, out_specs=o_specs + s_specs,
        out_shape=[jax.ShapeDtypeStruct((n, GLA_V_W), F32)] * 2 + [jax.ShapeDtypeStruct((nt, GLA_V_W, GLA_QK_W), F32)] * 2,
        scratch_shapes=[pltpu.VMEM((GLA_V_W, GLA_QK_W), F32)] * 2,
        compiler_params=_ARB1,
    )(*([p] * 8), w2s[0], b2s[0], w2s[1], b2s[1])


def _gla_bwd(name, p, w2s, b2s, ssaves, do, R, nct):
    n = p.shape[0]
    nt = n // R
    tiles = [lambda s, d=d: _gla_tile_of(nt - 1 - s, _GLA_REV[d], nct, nt) for d in range(2)]

    def body(*refs):
        rows, (w2f, b2f, w2b, b2b), ss, dos = refs[:8], refs[8:12], refs[12:14], refs[14:16]
        d_rows, (dw2f, db2f, dw2b, db2b), dstates = refs[16:24], refs[24:28], refs[28:]

        @pl.when(pl.program_id(0) == 0)
        def _():
            for ref in (dw2f, db2f, dw2b, db2b) + tuple(dstates):
                ref[...] = jnp.zeros_like(ref)

        ins = [_gla_load(rows[:4]), _gla_load(rows[4:])]
        _, vjp = jax.vjp(_gla_tile_pair, [i[0] for i in ins], [i[1] for i in ins], [i[2] for i in ins], [i[3] for i in ins],
                         [w2f[...], w2b[...]], [b2f[...], b2b[...]], [ss[0][...], ss[1][...]])
        dqs, dks, dvss, das, dw2, db2, ds = vjp(([dos[0][...], dos[1][...]], [dstates[0][...], dstates[1][...]]))
        for d in range(2):
            grads = [dqs[d], dks[d], jnp.concatenate(dvss[d], axis=1), das[d]]
            for ref, g in zip(d_rows[4 * d:4 * d + 4], grads):
                ref[...] = g
            dstates[d][...] = ds[d]
        dw2f[...] += dw2[0]
        db2f[...] += db2[0]
        dw2b[...] += dw2[1]
        db2b[...] += db2[1]

    par_specs = [_full_spec(w2s[0].shape), _full_spec(b2s[0].shape), _full_spec(w2s[1].shape), _full_spec(b2s[1].shape)]
    d_specs = [pl.BlockSpec((R, w), lambda s, t=t: (t(s), 0)) for t in tiles for w in _GLA_WIDTHS]
    in_specs = (_gla_row_specs(R, tiles[0]) + _gla_row_specs(R, tiles[1]) + par_specs
                + [pl.BlockSpec((None, GLA_V_W, GLA_QK_W), lambda s, t=t: (t(s), 0, 0)) for t in tiles]
                + [pl.BlockSpec((R, GLA_V_W), lambda s, t=t: (t(s), 0)) for t in tiles])
    res = pl.pallas_call(
        body, name=name, grid=(nt,), in_specs=in_specs, out_specs=d_specs + par_specs,
        out_shape=[jax.ShapeDtypeStruct((n, w), F32) for _ in range(2) for w in _GLA_WIDTHS]
        + [jax.ShapeDtypeStruct(a.shape, F32) for a in (w2s[0], b2s[0], w2s[1], b2s[1])],
        scratch_shapes=[pltpu.VMEM((GLA_V_W, GLA_QK_W), F32)] * 2,
        compiler_params=_ARB1,
    )(*([p] * 8), w2s[0], b2s[0], w2s[1], b2s[1], ssaves[0], ssaves[1], do, do)
    return res[:4], res[4:8], res[8:]


def _final_loss(name, x, target, gf, R, nct):
    n = x.shape[0]

    def body(x_ref, t_ref, g_ref, loss_ref, dx_ref, dg_ref):
        i = pl.program_id(0)

        @pl.when(i == 0)
        def _():
            loss_ref[...] = jnp.zeros_like(loss_ref)
            dg_ref[...] = jnp.zeros_like(dg_ref)

        @pl.when(i < nct)
        def _():
            dx_ref[...] = jnp.zeros_like(dx_ref)

        @pl.when(i >= nct)
        def _():
            y, vjp = jax.vjp(_rms, x_ref[...], g_ref[...])
            err = y - t_ref[...]
            loss_ref[...] += jnp.sum(0.5 * jnp.mean(err * err, axis=-1, keepdims=True))
            dx, dg = vjp(err * (1.0 / D))
            dx_ref[...] = dx
            dg_ref[...] += dg

    return pl.pallas_call(
        body, name=name, grid=(n // R,),
        in_specs=[_row_spec(R, D, 0), pl.BlockSpec((R, D), lambda i: (jnp.maximum(i - nct, 0), 0)), _full_spec((1, D))],
        out_specs=[_full_spec((8, 128)), _row_spec(R, D, 0), _full_spec((1, D))],
        out_shape=[jax.ShapeDtypeStruct((8, 128), F32), jax.ShapeDtypeStruct((n, D), F32), jax.ShapeDtypeStruct((1, D), F32)],
        compiler_params=_ARB1,
    )(x, target, gf)


def _adamw(name, w, m, v, gparts):
    rows, cols = w.shape
    nparts = gparts.shape[0]
    tr = rows
    for cand in range(min(rows, 256), 15, -16):
        if rows % cand == 0:
            tr = cand
            break

    def body(w_ref, m_ref, v_ref, g_ref, go_ref, d_ref, mo_ref, vo_ref):
        g = g_ref[0].astype(F32)
        for k in range(1, nparts):
            g = g + g_ref[k].astype(F32)
        m_new = ADAM_B1 * m_ref[...] + (1.0 - ADAM_B1) * g
        v_new = ADAM_B2 * v_ref[...] + (1.0 - ADAM_B2) * (g * g)
        m_hat = m_new / (1.0 - ADAM_B1 ** ADAM_STEP)
        v_hat = v_new / (1.0 - ADAM_B2 ** ADAM_STEP)
        go_ref[...] = g
        d_ref[...] = -ADAM_LR * (m_hat / (jnp.sqrt(v_hat) + ADAM_EPS) + ADAM_WD * w_ref[...])
        mo_ref[...] = m_new
        vo_ref[...] = v_new

    spec = pl.BlockSpec((tr, cols), lambda i: (i, 0))
    return pl.pallas_call(
        body, name=name, grid=(rows // tr,),
        in_specs=[spec, spec, spec, pl.BlockSpec((nparts, tr, cols), lambda i: (0, i, 0))],
        out_specs=[spec] * 4, out_shape=[jax.ShapeDtypeStruct((rows, cols), F32)] * 4,
        compiler_params=_ARB1,
    )(w, m, v, gparts)


def _my_index():
    return 4 * lax.axis_index("x") + 2 * lax.axis_index("y") + lax.axis_index("c")


def _xor_peer(k):
    flip = lambda a, bit: (1 - a) if bit else a
    pos = (flip(lax.axis_index("x"), (k >> 2) & 1), flip(lax.axis_index("y"), (k >> 1) & 1), flip(lax.axis_index("c"), k & 1))
    return pos, 4 * pos[0] + 2 * pos[1] + pos[2]


def _exchange_plan(x_refs, o_refs, send_sems, recv_sems, local_sems, gather):
    npeer = N_DEV - 1
    me = _my_index()
    locals_, sends, recvs = [], [], []
    for a, (x_ref, o_ref) in enumerate(zip(x_refs, o_refs)):
        mine = x_ref if gather else x_ref.at[me]
        locals_.append(pltpu.make_async_copy(mine, o_ref.at[me], local_sems.at[a]))
        for k in range(1, N_DEV):
            pos, lin = _xor_peer(k)
            src = x_ref if gather else x_ref.at[lin]
            sem = a * npeer + k - 1
            sends.append(pltpu.make_async_remote_copy(src_ref=src, dst_ref=o_ref.at[me], send_sem=send_sems.at[sem],
                                                      recv_sem=recv_sems.at[sem], device_id=pos, device_id_type=MESH_ID))
            recvs.append(pltpu.make_async_remote_copy(src_ref=src, dst_ref=o_ref.at[lin], send_sem=send_sems.at[sem],
                                                      recv_sem=recv_sems.at[sem], device_id=pos, device_id_type=MESH_ID))

    def start():
        for cp in locals_ + sends:
            cp.start()

    def wait():
        for cp in recvs:
            cp.wait_recv()
        for cp in sends:
            cp.wait_send()
        for cp in locals_:
            cp.wait()

    return start, wait


def _exchange_io(xs, gather):
    na = len(xs)
    hbm = pl.BlockSpec(memory_space=pltpu.HBM)
    shapes = [jax.ShapeDtypeStruct((N_DEV,) + tuple(x.shape if gather else x.shape[1:]), x.dtype) for x in xs]
    sems = [pltpu.SemaphoreType.DMA((na * (N_DEV - 1),)), pltpu.SemaphoreType.DMA((na * (N_DEV - 1),)),
            pltpu.SemaphoreType.DMA((na,))]
    return [hbm] * na, shapes, sems


def _exchange(name, xs, *, gather):
    na = len(xs)
    specs, shapes, sems = _exchange_io(xs, gather)

    def body(*refs):
        start, wait = _exchange_plan(refs[:na], refs[na:2 * na], *refs[2 * na:], gather)
        start()
        wait()

    return list(pl.pallas_call(body, name=name, in_specs=specs, out_specs=specs, out_shape=shapes, scratch_shapes=sems)(*xs))


def _adaln_fwd(name, craw16, w_ada, b_cols):
    def body(c_ref, w_ref, b_ref, o_ref):
        cs = jax.nn.silu(c_ref[...]).astype(BF16)
        for l in range(2):
            o_ref[l] = lax.dot_general(cs, w_ref[l].astype(BF16), _NN, preferred_element_type=F32) + b_ref[l]

    return pl.pallas_call(
        body, name=name, out_shape=jax.ShapeDtypeStruct((2, 16, w_ada.shape[2]), F32),
        compiler_params=pltpu.CompilerParams(vmem_limit_bytes=V7X_VMEM_LIMIT),
    )(craw16, w_ada, b_cols)


def _adaln_bwd(name, craw16, w_ada, dm):
    def body(c_ref, w_ref, dm_ref, gw_ref, dc_ref):
        c = c_ref[...]
        sg = jax.nn.sigmoid(c)
        cs = c * sg
        row = lax.broadcasted_iota(jnp.int32, (8, 1), 0)
        dc = jnp.zeros((16, D), F32)
        for l in range(2):
            dmx = dm_ref[2 * l + 1]
            dmc = jnp.where(row == 0, jnp.sum(dm_ref[2 * l], axis=0, keepdims=True), 0.0)
            gw_ref[l] = _hdot(cs[0:8], dmx, _TN) + _hdot(cs[8:16], dmc, _TN)
            dc = dc + _hdot(jnp.concatenate([dmx, dmc], axis=0), w_ref[l], _NT)
        dc_ref[...] = dc * sg * (1.0 + c * (1.0 - sg))

    return pl.pallas_call(
        body, name=name,
        out_shape=[jax.ShapeDtypeStruct(w_ada.shape, F32), jax.ShapeDtypeStruct((16, D), F32)],
        compiler_params=pltpu.CompilerParams(vmem_limit_bytes=V7X_VMEM_LIMIT),
    )(craw16, w_ada, dm)


_IN_OFFS = [sum(IN_SPLITS[:k]) for k in range(len(IN_SPLITS) + 1)]
_MY_ORDER = (11, 12, 13, 0, 1, 2, 7, 10, 5, 6, 3, 4, 8, 9)


_IN_SHARD = IN_WIDTH // N_DEV


def _win_my_cols(pieces):
    parts = []
    for k in _MY_ORDER:
        a, b = _IN_OFFS[k], _IN_OFFS[k + 1]
        for s in range(a // _IN_SHARD, (b - 1) // _IN_SHARD + 1):
            lo, hi = max(a, s * _IN_SHARD), min(b, (s + 1) * _IN_SHARD)
            parts.append(pieces[s][:, lo - s * _IN_SHARD:hi - s * _IN_SHARD])
    parts.append(jnp.zeros((pieces[0].shape[0], PW - IN_WIDTH), pieces[0].dtype))
    return jnp.concatenate(parts, axis=1)


def _win_shards(wp):
    my_offs, pos = {}, 0
    for k in _MY_ORDER:
        my_offs[k] = pos
        pos += IN_SPLITS[k]
    shards = []
    for s in range(N_DEV):
        parts = []
        for k in range(len(IN_SPLITS)):
            lo, hi = max(_IN_OFFS[k], s * _IN_SHARD), min(_IN_OFFS[k + 1], (s + 1) * _IN_SHARD)
            if lo < hi:
                parts.append(wp[:, my_offs[k] + lo - _IN_OFFS[k]:my_offs[k] + hi - _IN_OFFS[k]])
        shards.append(jnp.concatenate(parts, axis=1))
    return jnp.stack(shards)


def _tile_friendly(shape, axis):
    width = shape[axis] // N_DEV if axis == len(shape) - 1 else 128
    return len(shape) >= 2 and width % 128 == 0


def _split_shards(full, axis):
    s = full.shape
    if _tile_friendly(s, axis):
        return jnp.moveaxis(full.reshape(s[:axis] + (N_DEV, s[axis] // N_DEV) + s[axis + 1:]), axis, 0)
    c = s[axis] // N_DEV
    return jnp.stack([lax.slice_in_dim(full, k * c, (k + 1) * c, axis=axis) for k in range(N_DEV)])


def _join_shards(g, axis):
    s = g.shape[1:]
    full_shape = s[:axis] + (N_DEV * s[axis],) + s[axis + 1:]
    if _tile_friendly(full_shape, axis):
        return jnp.moveaxis(g, 0, axis).reshape(full_shape)
    return jnp.concatenate([g[k] for k in range(N_DEV)], axis=axis)


def _pack_rows(pieces, row_mult):
    rows = jnp.concatenate([p.reshape(-1, 128) for p in pieces], axis=0)
    padn = (-rows.shape[0]) % row_mult
    if padn:
        rows = jnp.concatenate([rows, jnp.zeros((padn, 128), rows.dtype)], axis=0)
    return rows


def _unpack_rows(rows, shapes):
    out, pos = [], 0
    for s in shapes:
        size = 1
        for d in s:
            size *= d
        out.append(rows[pos:pos + size // 128].reshape(tuple(s)))
        pos += size // 128
    return out


def _heads_front(a, nh):
    return a.reshape(a.shape[0], nh, HEAD_DIM).transpose(1, 0, 2)


def _heads_back(a):
    return a.transpose(1, 0, 2).reshape(a.shape[1], a.shape[0] * HEAD_DIM)


def _rope_tables(t, tc):
    tok = jnp.arange(t, dtype=jnp.int32)
    inv_freq = ROPE_THETA ** (-jnp.arange(ROPE_FREQS, dtype=F32) / ROPE_FREQS)
    ang_r = (tok // GRID_W).astype(F32)[:, None] * inv_freq
    ang_c = (tok % GRID_W).astype(F32)[:, None] * inv_freq
    cos64 = jnp.concatenate([jnp.cos(ang_r), jnp.cos(ang_r), jnp.cos(ang_c), jnp.cos(ang_c)], axis=1)
    sin64 = jnp.concatenate([-jnp.sin(ang_r), jnp.sin(ang_r), -jnp.sin(ang_c), jnp.sin(ang_c)], axis=1)
    cos64 = jnp.concatenate([jnp.ones((tc, HEAD_DIM), F32), cos64], axis=0)
    sin64 = jnp.concatenate([jnp.zeros((tc, HEAD_DIM), F32), sin64], axis=0)
    return jnp.tile(cos64, (1, N_Q_HEADS)), jnp.tile(sin64, (1, N_Q_HEADS))


def _head_mean_matrix(width):
    i = jnp.arange(width) // HEAD_DIM
    return (i[:, None] == i[None, :]).astype(F32) / HEAD_DIM


def _head_tile_matrix(width):
    return (jnp.arange(HEAD_DIM)[:, None] == (jnp.arange(width) % HEAD_DIM)[None, :]).astype(F32)


def _heads_t(a, nh):
    return a.T.reshape(nh, HEAD_DIM, a.shape[0])


def _heads_t_back(a):
    return a.reshape(a.shape[0] * HEAD_DIM, a.shape[2]).T


def _attention_fwd(tag, qr, kr, vv, tc, comm):
    qt8, k2, vt2 = _heads_t(qr, N_Q_HEADS), _heads_front(kr, N_KV_HEADS), _heads_t(vv, N_KV_HEADS)
    vt2 = jnp.concatenate([vt2, jnp.ones((N_KV_HEADS, 8, vt2.shape[2]), BF16)], axis=1)
    o_c, lse_c, _ = _attn_fwd(tag + "_attn_ctx", qt8[:, :, :tc], k2[:, :tc], vt2[:, :, :tc])
    o_x, lse_x, comm_out = _attn_fwd(tag + "_attn_lat", qt8[:, :, tc:], k2, vt2, comm)
    ot8 = jnp.concatenate([o_c, o_x], axis=2)
    lse = jnp.concatenate([lse_c, lse_x], axis=2)
    return _heads_t_back(ot8), (qr, kr, vv, ot8, lse), comm_out


def _attention_bwd(tag, saved, datt, tc, comm):
    qr, kr, vv, ot8, lse = saved
    datt = datt.astype(BF16)
    qt8, dot8 = _heads_t(qr, N_Q_HEADS), _heads_t(datt, N_Q_HEADS)
    k2, v2, kt2 = _heads_front(kr, N_KV_HEADS), _heads_front(vv, N_KV_HEADS), _heads_t(kr, N_KV_HEADS)
    zero = jnp.zeros((N_KV_HEADS, HEAD_DIM, tc), F32)
    dq_c, dk_c, dv_c, _ = _attn_bwd(tag + "_attn_b_ctx", qt8[:, :, :tc], k2[:, :tc], v2[:, :tc], kt2[:, :, :tc],
                                    ot8[:, :, :tc], dot8[:, :, :tc], lse[:, :, :tc], zero, zero)
    dq_x, dkt2, dvt2, comm_out = _attn_bwd(tag + "_attn_b_lat", qt8[:, :, tc:], k2, v2, kt2,
                                           ot8[:, :, tc:], dot8[:, :, tc:], lse[:, :, tc:], dk_c, dv_c, comm)
    dqt8 = jnp.concatenate([dq_c, dq_x], axis=2)
    return _heads_t_back(dqt8), _heads_t_back(dkt2), _heads_t_back(dvt2), comm_out


def _layer_fwd(tag, x, w, modv, consts, R, nct, tc, comm, late_weights):
    sh1, sc1, g1, sh2, sc2, g2 = modv
    cosq, sinq, bdq, eq, bdk, ek = consts
    n = x.shape[0]
    (h1,) = _rowwise(tag + "_ln1", _f_lnmod, R, nct, [(x, D, 0, D)], [], [(w["norm1_g"], False)], [], [sh1, sc1], [(D, BF16)])
    p = _matmul(tag + "_in", h1, w["w_in"], "nn", F32)
    qk_rows = [(p, Q_W, C_Q // Q_W, Q_W), (p, KV_W, C_K // KV_W, KV_W)]
    qk_consts = [(cosq, Q_W, 0, Q_W), (sinq, Q_W, 0, Q_W), (cosq, KV_W, 0, KV_W), (sinq, KV_W, 0, KV_W)]
    qk_params = [(w["q_norm_g"], False), (w["k_norm_g"], False)]
    qk_cparams = [(bdq, False), (eq, False), (bdk, False), (ek, False)]
    qr, kr = _rowwise(tag + "_qk", _f_qknorm, R, nct, qk_rows, qk_consts, qk_params, qk_cparams, [],
                      [(Q_W, BF16), (KV_W, BF16)], post=_qk_post)
    vv = p[:, C_VV:C_VV + KV_W].astype(BF16)
    att, att_saved, comm_out = _attention_fwd(tag, qr, kr, vv, tc, comm)
    w.update(late_weights(comm_out))

    o_f, o_b, s_f, s_b = _gla_fwd(tag + "_gla", p, (w["w2p_f"], w["w2p_b"]), (w["b2_f"], w["b2_b"]), R, nct)
    go_rows = [(o_f, GLA_V_W, 0, GLA_DV), (o_b, GLA_V_W, 0, GLA_DV), (p, GLA_V_W, C_R // GLA_V_W, GLA_DV)]
    (gla,) = _rowwise(tag + "_glaout", _f_glaout, R, nct, go_rows, [], [(w["gla_norm_g"], True)], [], [], [(GLA_V_W, BF16)])

    rg = GMLP_CHUNK
    gm_rows = [(p, GMLP_W, C_U // GMLP_W, GMLP_W), (p, GMLP_W, C_V // GMLP_W, GMLP_W // GMLP_GROUPS)]
    gm_params = [(w["gmlp_norm_g"], True), (w["w_spatial"], True), (w["b_spatial_t"], True)]
    (gm,) = _rowwise(tag + "_gmlp", _f_gmlp, rg, tc // rg, gm_rows, [], gm_params, [], [], [(GMLP_W, BF16)])

    ya = _matmul(tag + "_br_a", gm, w["w_br_a"], "nn", F32)
    yb = _matmul(tag + "_br_b", att, w["w_br_b"], "nn", F32)
    yc = _matmul(tag + "_br_c", gla, w["w_br_c"], "nn", F32)
    mg_rows = [(p, D, C_GA // D, D), (p, D, C_GB // D, D), (p, D, C_GC // D, D), (ya, D, 0, D), (yb, D, 0, D), (yc, D, 0, D)]
    (merged,) = _rowwise(tag + "_merge", _f_merge, R, nct, mg_rows, [], [], [], [], [(D, BF16)])
    mix = _matmul(tag + "_out", merged, w["w_out"], "nn", F32)
    (x_mid,) = _rowwise(tag + "_res1", _f_resid, R, nct, [(x, D, 0, D), (mix, D, 0, D)], [], [], [], [g1], [(D, F32)])

    (h2,) = _rowwise(tag + "_ln2", _f_lnmod, R, nct, [(x_mid, D, 0, D)], [], [(w["norm2_g"], False)], [], [sh2, sc2], [(D, BF16)])
    a2 = _matmul(tag + "_up", h2, w["w_ffn_up"], "nn", F32, o_halves=True)
    gv = _conv_gate(tag + "_conv", a2, w["conv_w_h"], w["conv_b_h"], R, tc)
    ffn = _matmul(tag + "_down", gv, w["w_ffn_down"], "nn", F32)
    (x_next,) = _rowwise(tag + "_res2", _f_resid, R, nct, [(x_mid, D, 0, D), (ffn, D, 0, D)], [], [], [], [g2], [(D, F32)])
    saved = dict(x=x, h1=h1, p=p, att_saved=att_saved, att=att, s_f=s_f, s_b=s_b, gla=gla, gm=gm,
                 ya=ya, yb=yb, yc=yc, merged=merged, mix=mix, x_mid=x_mid, h2=h2, a2=a2, gv=gv, ffn=ffn,
                 qk=(qk_rows, qk_consts, qk_params, qk_cparams), go_rows=go_rows, gm_info=(gm_rows, gm_params),
                 mg_rows=mg_rows)
    return x_next, saved, comm_out


def _layer_bwd(tag, dx_next, s, w, modv, R, nct, tc, make_comm, make_tail_comm):
    sh1, sc1, g1, sh2, sc2, g2 = modv
    gw = {}
    (dffn,), _, (dg2,) = _rowwise_bwd(tag + "_res2_b", _f_resid, R, nct, [(s["ffn"], D, 0, D), (s["ffn"], D, 0, D)], [], [], [], [g2],
                                      [(dx_next, D)], [None, BF16])
    dgv = _matmul(tag + "_down_da", dffn, w["w_ffn_down"], "nt", F32)
    gw["w_ffn_down"] = _matmul(tag + "_down_dw", s["gv"], dffn, "tn", F32)
    da2, dcw, dcb = _conv_gate_bwd(tag + "_conv_b", s["a2"], dgv, w["conv_w_h"], w["conv_b_h"], R, tc)
    gw["conv_w_h"], gw["conv_b_h"] = dcw, dcb
    dh2 = _matmul(tag + "_up_da", da2, w["w_ffn_up"], "nt", F32, a_halves=True)
    gw["w_ffn_up"] = _matmul(tag + "_up_dw", s["h2"], da2, "tn", F32, b_halves=True)
    (dx_mid,), (gw["norm2_g"],), (dsh2, dsc2) = _rowwise_bwd(
        tag + "_ln2_b", _f_lnmod, R, nct, [(s["x_mid"], D, 0, D)], [], [(w["norm2_g"], False)], [], [sh2, sc2],
        [(dh2, D)], [F32], adds=[dx_next])
    (dmix,), _, (dg1,) = _rowwise_bwd(tag + "_res1_b", _f_resid, R, nct, [(s["mix"], D, 0, D), (s["mix"], D, 0, D)], [], [], [], [g1],
                                      [(dx_mid, D)], [None, BF16])
    dmerged = _matmul(tag + "_out_da", dmix, w["w_out"], "nt", F32)
    gw["w_out"] = _matmul(tag + "_out_dw", s["merged"], dmix, "tn", F32)
    (dga, dgb, dgc, dya, dyb, dyc), _, _ = _rowwise_bwd(tag + "_merge_b", _f_merge, R, nct, s["mg_rows"], [], [], [], [],
                                                        [(dmerged, D)], [BF16] * 6)
    dgm = _matmul(tag + "_br_a_da", dya, w["w_br_a"], "nt", F32)
    datt = _matmul(tag + "_br_b_da", dyb, w["w_br_b"], "nt", F32)
    dgla = _matmul(tag + "_br_c_da", dyc, w["w_br_c"], "nt", F32)
    gm_rows, gm_params = s["gm_info"]
    gw["w_br_a"] = _matmul(tag + "_br_a_dw", s["gm"], dya, "tn", F32)
    gw["w_br_b"] = _matmul(tag + "_br_b_dw", s["att"], dyb, "tn", F32)
    gw["w_br_c"] = _matmul(tag + "_br_c_dw", s["gla"], dyc, "tn", F32)
    rg = GMLP_CHUNK
    (du, dv_), (gw["gmlp_norm_g"], gw["w_spatial"], gw["b_spatial_t"]), _ = _rowwise_bwd(
        tag + "_gmlp_b", _f_gmlp, rg, tc // rg, gm_rows, [], gm_params, [], [], [(dgm, GMLP_W)], [BF16, BF16])
    (do, dr), (gw["gla_norm_g"],), _ = _rowwise_bwd(tag + "_glaout_b", _f_glaout, R, nct, s["go_rows"], [],
                                                    [(w["gla_norm_g"], True)], [], [], [(dgla, GLA_V_W)], [F32, None, BF16])
    p = s["p"]
    d_f, d_b, (gw["w2p_f"], gw["b2_f"], gw["w2p_b"], gw["b2_b"]) = _gla_bwd(
        tag + "_gla_b", p, (w["w2p_f"], w["w2p_b"]), (w["b2_f"], w["b2_b"]), (s["s_f"], s["s_b"]), do, R, nct)
    dgq, dgk, dgv_, daf = [(a + b).astype(BF16) for a, b in zip(d_f, d_b)]
    dqr, dkr, dvv, comm_out = _attention_bwd(tag, s["att_saved"], datt, tc, make_comm(gw))
    qk_rows, qk_consts, qk_params, qk_cparams = s["qk"]
    (dq, dk), (gw["q_norm_g"], gw["k_norm_g"]), _ = _rowwise_bwd(
        tag + "_qk_b", _f_qknorm, R, nct, qk_rows, qk_consts, qk_params, qk_cparams, [],
        [(dqr, Q_W), (dkr, KV_W)], [BF16, BF16], pre=_qk_pre)
    dp = jnp.concatenate([dga, dgb, dgc, du, dv_, dq, dgv_, dr, dgq, dgk, dk, dvv.astype(BF16), daf,
                          jnp.zeros((p.shape[0], PW - C_AF - 128), BF16)], axis=1)
    gw["w_in"] = _matmul(tag + "_in_dw", s["h1"], dp, "tn", F32)
    tail = make_tail_comm(gw)
    if tail is None:
        dh1, tail_out = _matmul(tag + "_in_da", dp, w["w_in"], "nt", F32), []
    else:
        dh1, tail_out = _matmul(tag + "_in_da", dp, w["w_in"], "nt", F32, comm=tail)
    (dx,), (gw["norm1_g"],), (dsh1, dsc1) = _rowwise_bwd(
        tag + "_ln1_b", _f_lnmod, R, nct, [(s["x"], D, 0, D)], [], [(w["norm1_g"], False)], [], [sh1, sc1],
        [(dh1, D)], [F32], adds=[dx_mid])
    return dx, gw, (dsh1, dsc1, dg1, dsh2, dsc2, dg2), comm_out, tail_out


_SHARDED = (("w_in", 1, True), ("w_br_a", 1, True), ("w_br_b", 1, True), ("w_br_c", 1, True), ("w_out", 0, True),
            ("w_ffn_up", 1, True), ("w_ffn_down", 0, True), ("conv_w", 1, False), ("w_alpha2", 2, False), ("b_alpha", 1, False))
_REPLICATED = ("c_ctx", "b_ada", "norm1_g", "norm2_g", "q_norm_g", "k_norm_g", "gmlp_norm_g", "w_spatial", "b_spatial",
               "gla_norm_g", "conv_b", "final_norm_g")
_WEIGHTS = ("c_ctx", "w_ada", "b_ada", "norm1_g", "norm2_g", "w_in", "q_norm_g", "k_norm_g", "gmlp_norm_g", "w_spatial",
            "b_spatial", "w_alpha2", "b_alpha", "gla_norm_g", "w_br_a", "w_br_b", "w_br_c", "w_out", "w_ffn_up", "conv_w",
            "conv_b", "w_ffn_down", "final_norm_g")


def _decay_weights(w_alpha2_l, b_alpha_l):
    out = []
    for d in range(2):
        w2p = jnp.zeros((128, GLA_QK_W), F32).at[GLA_RANK * d:GLA_RANK * (d + 1)].set(w_alpha2_l[d])
        out += [w2p, b_alpha_l[d][None, :]]
    return out


def _step(inp, wts, moms, vels):
    x, c, ctx, loss_target = inp
    t, tc = x.shape[1], ctx.shape[1]
    n = t + tc
    R = min(256, tc)
    nct = tc // R
    me = _my_index()
    depth = wts["w_in"].shape[0]

    late = [(nm, ax) for nm, ax, half in _SHARDED if half and nm != "w_in"]
    small = [(nm, ax) for nm, ax, half in _SHARDED if not half]
    w_in_shard = lambda l: wts["w_in"][l].astype(BF16)
    c8 = jnp.concatenate([c, jnp.zeros((7, D), F32)], axis=0)
    first = _exchange("gather_first", [w_in_shard(0)] + [wts[nm] for nm, _ in small] + [c8], gather=True)
    c_all = first[-1][:, 0, :]
    small_all = dict(zip([nm for nm, _ in small], first[1:-1]))

    def early_weights(l, w_in_all):
        w = {"w_in": _win_my_cols([w_in_all[s] for s in range(N_DEV)])}
        conv_w, w_alpha2, b_alpha = [jnp.concatenate([small_all[nm][s, l] for s in range(N_DEV)], axis=ax) for nm, ax in small]
        w["conv_w_h"] = conv_w.reshape(3, 2, FFN_H).transpose(1, 0, 2)
        w["conv_b_h"] = wts["conv_b"][l].reshape(2, 1, FFN_H)
        w["w2p_f"], w["b2_f"], w["w2p_b"], w["b2_b"] = _decay_weights(w_alpha2, b_alpha)
        w["norm1_g"] = wts["norm1_g"][l][None, :]
        w["norm2_g"] = wts["norm2_g"][l][None, :]
        w["q_norm_g"] = wts["q_norm_g"][l][None, :]
        w["k_norm_g"] = wts["k_norm_g"][l][None, :]
        w["gmlp_norm_g"] = wts["gmlp_norm_g"][l].reshape(GMLP_GROUPS, 1, GMLP_W // GMLP_GROUPS)
        w["w_spatial"] = wts["w_spatial"][l]
        w["b_spatial_t"] = wts["b_spatial"][l][:, :, None]
        w["gla_norm_g"] = wts["gla_norm_g"][l].reshape(GLA_HEADS, 1, GLA_DV)
        return w

    craw16 =jnp.concatenate([c_all, wts["c_ctx"][None, :], jnp.zeros((7, D), F32)], axis=0)
    acols = wts["w_ada"].shape[2]
    b_cols = lax.dynamic_slice_in_dim(wts["b_ada"], me * acols, acols, axis=1)[:, None, :]
    mod_part = _adaln_fwd("adaln", craw16, wts["w_ada"], b_cols)
    send = jnp.stack([mod_part[:, 8, :][None].repeat(N_DEV, 0), mod_part[:, :8, :].transpose(1, 0, 2)], axis=2)
    send = jnp.concatenate([send.reshape(N_DEV, 2 * depth, acols), jnp.zeros((N_DEV, 8 - 2 * depth, acols), F32)], axis=1)
    (got,) = _exchange("scatter_mod", [send], gather=False)
    mod = got[:, :2 * depth, :].transpose(1, 0, 2).reshape(depth, 2, N_MOD, 1, D)
    modv = [[mod[l, :, k] for k in range(N_MOD)] for l in range(depth)]

    cosq, sinq = _rope_tables(t, tc)
    consts = (cosq, sinq, _head_mean_matrix(Q_W), _head_tile_matrix(Q_W), _head_mean_matrix(KV_W), _head_tile_matrix(KV_W))
    xs = jnp.concatenate([ctx[0], x[0]], axis=0)
    saved, layers = [], []
    w_in_all = first[0]

    def late_weights(got):
        return {nm: _join_shards(g, ax) for (nm, ax), g in zip(late, got)}

    for l in range(depth):
        layers.append(early_weights(l, w_in_all))
        sending = [wts[nm][l].astype(BF16) for nm, _ in late] + ([w_in_shard(l + 1)] if l + 1 < depth else [])
        xs, sv, got = _layer_fwd("l%d" % l, xs, layers[l], modv[l], consts, R, nct, tc, (sending, True), late_weights)
        if l + 1 < depth:
            w_in_all = got[len(late)]
        saved.append(sv)
    loss_blk, dxs, dgf = _final_loss("final", xs, loss_target[0], wts["final_norm_g"][None, :], R, nct)
    loss = lax.psum(loss_blk[0, 0], ("x", "y", "c"))

    grads = [None] * depth
    dmods = [None] * depth
    late_parts = [None] * depth
    w_in_parts = [None] * depth
    w_in_grad_shards = lambda g: _win_shards(g["w_in"]).astype(BF16)

    def small_grad_shards(g):
        full = dict(conv_w=g["conv_w_h"].transpose(1, 0, 2).reshape(3, F2),
                    w_alpha2=jnp.stack([g["w2p_f"][:GLA_RANK], g["w2p_b"][GLA_RANK:2 * GLA_RANK]]),
                    b_alpha=jnp.stack([g["b2_f"][0], g["b2_b"][0]]))
        return [_split_shards(full[nm], ax) for nm, ax in small]

    for l in range(depth - 1, -1, -1):
        def make_comm(gw, l=l):
            sending = [_split_shards(gw[nm], ax).astype(BF16) for nm, ax in late]
            return sending + ([w_in_grad_shards(grads[l + 1])] if l + 1 < depth else []), False

        def make_tail_comm(gw, l=l):
            if l > 0:
                return None
            per_layer = [small_grad_shards(gw if k == 0 else grads[k]) for k in range(depth)]
            return [w_in_grad_shards(gw)] + [jnp.stack([per_layer[k][i] for k in range(depth)], axis=1) for i in range(len(small))], False

        dxs, grads[l], dmods[l], got, tail = _layer_bwd("l%d" % l, dxs, saved[l], layers[l], modv[l], R, nct, tc,
                                                        make_comm, make_tail_comm)
        late_parts[l] = got[:len(late)]
        if l + 1 < depth:
            w_in_parts[l + 1] = got[len(late)]
    w_in_parts[0], small_parts = tail[0], tail[1:]
    grad_x = dxs[tc:][None]

    dmod = jnp.stack([jnp.stack(dmods[l], axis=1) for l in range(depth)])
    dmod = dmod.reshape(depth, 2, N_DEV, acols).transpose(2, 0, 1, 3).reshape(N_DEV, 2 * depth, acols)
    dmod_send = jnp.concatenate([dmod, jnp.zeros((N_DEV, 8 - 2 * depth, acols), F32)], axis=1)
    (dm_got,) = _exchange("scatter_dmod", [dmod_send], gather=False)
    g_w_ada, dc16 = _adaln_bwd("adaln_b", craw16, wts["w_ada"], dm_got[:, :2 * depth].transpose(1, 0, 2))
    db_ada_part = jnp.stack([jnp.stack(dmods[l], axis=1) for l in range(depth)]).reshape(depth, 2, N_MOD * D).sum(axis=1)

    out = {}
    kinds = ("grad", "delta", "new_m", "new_v")
    view2 = lambda a: a.reshape(-1, a.shape[-1])
    sharded_parts = ([jnp.stack(w_in_parts, axis=1)]
                     + [jnp.stack([late_parts[l][k] for l in range(depth)], axis=1) for k in range(len(late))] + small_parts)
    for (nm, _), parts in zip([("w_in", 1)] + late + small, sharded_parts):
        res = _adamw("adamw_" + nm, view2(wts[nm]), view2(moms[nm]), view2(vels[nm]), parts.reshape(N_DEV, -1, parts.shape[-1]))
        for kind, flat in zip(kinds, res):
            out[kind, nm] = flat.reshape(wts[nm].shape)

    rep_g = dict(
        c_ctx=dc16[8], b_ada=db_ada_part, final_norm_g=dgf[0],
        norm1_g=jnp.stack([grads[l]["norm1_g"][0] for l in range(depth)]),
        norm2_g=jnp.stack([grads[l]["norm2_g"][0] for l in range(depth)]),
        q_norm_g=jnp.stack([grads[l]["q_norm_g"][0] for l in range(depth)]),
        k_norm_g=jnp.stack([grads[l]["k_norm_g"][0] for l in range(depth)]),
        gmlp_norm_g=jnp.stack([grads[l]["gmlp_norm_g"].reshape(GMLP_W) for l in range(depth)]),
        w_spatial=jnp.stack([grads[l]["w_spatial"] for l in range(depth)]),
        b_spatial=jnp.stack([grads[l]["b_spatial_t"][:, :, 0] for l in range(depth)]),
        gla_norm_g=jnp.stack([grads[l]["gla_norm_g"].reshape(GLA_V_W) for l in range(depth)]),
        conv_b=jnp.stack([grads[l]["conv_b_h"].reshape(F2) for l in range(depth)]),
    )
    rep_shapes = [wts[nm].shape for nm in _REPLICATED]
    (rg_parts,) = _exchange("gather_rep_grads", [_pack_rows([rep_g[nm] for nm in _REPLICATED], 16)], gather=True)
    rpk = lambda src: _pack_rows([src[nm] for nm in _REPLICATED], 16)
    res = _adamw("adamw_rep", rpk(wts), rpk(moms), rpk(vels), rg_parts)
    for kind, rows in zip(kinds, res):
        for nm, piece in zip(_REPLICATED, _unpack_rows(rows, rep_shapes)):
            out[kind, nm] = piece

    res = _adamw("adamw_ada", view2(wts["w_ada"]), view2(moms["w_ada"]), view2(vels["w_ada"]), view2(g_w_ada)[None])
    for kind, flat in zip(kinds, res):
        out[kind, "w_ada"] = flat.reshape(wts["w_ada"].shape)

    return (loss, grad_x, *[out[kind, nm] for kind in kinds for nm in _WEIGHTS])


def kernel(x, c, ctx, c_ctx, w_ada, b_ada, norm1_g, norm2_g, w_in, q_norm_g, k_norm_g, gmlp_norm_g, w_spatial, b_spatial, w_alpha2, b_alpha, gla_norm_g, w_br_a, w_br_b, w_br_c, w_out, w_ffn_up, conv_w, conv_b, w_ffn_down, final_norm_g, loss_target, m_c_ctx, m_w_ada, m_b_ada, m_norm1_g, m_norm2_g, m_w_in, m_q_norm_g, m_k_norm_g, m_gmlp_norm_g, m_w_spatial, m_b_spatial, m_w_alpha2, m_b_alpha, m_gla_norm_g, m_w_br_a, m_w_br_b, m_w_br_c, m_w_out, m_w_ffn_up, m_conv_w, m_conv_b, m_w_ffn_down, m_final_norm_g, v_c_ctx, v_w_ada, v_b_ada, v_norm1_g, v_norm2_g, v_w_in, v_q_norm_g, v_k_norm_g, v_gmlp_norm_g, v_w_spatial, v_b_spatial, v_w_alpha2, v_b_alpha, v_gla_norm_g, v_w_br_a, v_w_br_b, v_w_br_c, v_w_out, v_w_ffn_up, v_conv_w, v_conv_b, v_w_ffn_down, v_final_norm_g):
    wts = dict(zip(_WEIGHTS, (c_ctx, w_ada, b_ada, norm1_g, norm2_g, w_in, q_norm_g, k_norm_g, gmlp_norm_g, w_spatial, b_spatial,
                              w_alpha2, b_alpha, gla_norm_g, w_br_a, w_br_b, w_br_c, w_out, w_ffn_up, conv_w, conv_b, w_ffn_down,
                              final_norm_g)))
    moms = dict(zip(_WEIGHTS, (m_c_ctx, m_w_ada, m_b_ada, m_norm1_g, m_norm2_g, m_w_in, m_q_norm_g, m_k_norm_g, m_gmlp_norm_g,
                               m_w_spatial, m_b_spatial, m_w_alpha2, m_b_alpha, m_gla_norm_g, m_w_br_a, m_w_br_b, m_w_br_c, m_w_out,
                               m_w_ffn_up, m_conv_w, m_conv_b, m_w_ffn_down, m_final_norm_g)))
    vels = dict(zip(_WEIGHTS, (v_c_ctx, v_w_ada, v_b_ada, v_norm1_g, v_norm2_g, v_w_in, v_q_norm_g, v_k_norm_g, v_gmlp_norm_g,
                               v_w_spatial, v_b_spatial, v_w_alpha2, v_b_alpha, v_gla_norm_g, v_w_br_a, v_w_br_b, v_w_br_c, v_w_out,
                               v_w_ffn_up, v_conv_w, v_conv_b, v_w_ffn_down, v_final_norm_g)))
    return _step((x, c, ctx, loss_target), wts, moms, vels)
```

```python
import functools

import jax
import jax.numpy as jnp
from jax import lax
from jax.experimental import pallas as pl
from jax.experimental.pallas import tpu as pltpu

F32 = jnp.float32
BF16 = jnp.bfloat16
HI = lax.Precision.HIGHEST
MESH_ID = pl.DeviceIdType.MESH

N_DEV = 8
EPS = 1e-6
D = 1024
N_MOD = 6
HEAD_DIM = 64
N_Q_HEADS = 8
N_KV_HEADS = 2
Q_GROUP = 4
Q_W = 512
KV_W = 128
GRID_W = 64
ROPE_THETA = 10000.0
ROPE_FREQS = 16
GMLP_CHUNK = 128
GMLP_GROUPS = 4
GMLP_W = 512
GLA_HEADS = 4
GLA_QK_W = 256
GLA_V_W = 512
GLA_DK = 64
GLA_DV = 128
GLA_RANK = 16
GLA_TAU = 16.0
GLA_CHUNK = 64
FFN_H = 2816
F2 = 2 * FFN_H
IN_SPLITS = (512, 512, 512, 128, 128, 256, 256, 512, 16, 16, 512, 1024, 1024, 1024)
IN_WIDTH = sum(IN_SPLITS)

C_GA, C_GB, C_GC = 0, 1024, 2048
C_U, C_V, C_Q, C_GV, C_R = 3072, 3584, 4096, 4608, 5120
C_GQ, C_GK = 5632, 5888
C_K, C_VV, C_AF = 6144, 6272, 6400
PW = 6656

ADAM_LR = 0.001
ADAM_B1 = 0.9
ADAM_B2 = 0.999
ADAM_EPS = 1e-08
ADAM_WD = 0.01
ADAM_STEP = 10

V7X_VMEM_LIMIT = 56 * 1024 * 1024

_ARB1 = pltpu.CompilerParams(dimension_semantics=("arbitrary",), vmem_limit_bytes=V7X_VMEM_LIMIT)


def _pick(dim, prefs):
    for p in prefs:
        if dim % p == 0:
            return p
    return dim


def _hdot(a, b, dims=(((1,), (0,)), ((), ()))):
    return lax.dot_general(a, b, dims, precision=HI, preferred_element_type=F32)


_NT = (((1,), (1,)), ((), ()))
_TN = (((0,), (0,)), ((), ()))
_NN = (((1,), (0,)), ((), ()))


def _matmul(name, a, b, mode, out_dtype, *, a_halves=False, b_halves=False, o_halves=False, comm=None):
    def dims2(x, halves):
        return (x.shape[1], 2 * x.shape[2]) if halves else x.shape

    ar, ac = dims2(a, a_halves)
    br, bc = dims2(b, b_halves)
    if mode == "nn":
        M, K, N = ar, ac, bc
    elif mode == "nt":
        M, K, N = ar, ac, br
    else:
        M, K, N = ac, ar, bc
    n_unit = N // 2 if (o_halves or (b_halves and mode != "nt")) else N
    k_unit = K // 2 if (a_halves and mode != "tn") else K
    tokens = lambda d: d > 7168
    if tokens(K):
        tm = _pick(M, (512, 1408, 256, 128))
        tk = _pick(k_unit, (2816, 1408, 768, 512, 384, 256, 128))
    else:
        tm = _pick(M, (768, 512, 384, 256, 128) if tokens(M) else (1024, 1408, 512, 256, 128))
        tk = _pick(k_unit, (3328, 2816, 1664, 1408, 1024, 512, 256, 128))
    tn = _pick(n_unit, (1664, 1408, 1024, 512, 256, 128))
    nk = K // tk

    def spec(shape2, halves, blk, imap):
        if not halves:
            return pl.BlockSpec(blk, imap)
        nhalf = (shape2[1] // 2) // blk[1]

        def im(i, j, k):
            r, c = imap(i, j, k)
            return (c // nhalf, r, c % nhalf)
        return pl.BlockSpec((None,) + blk, im)

    if mode == "nn":
        a_spec = spec((ar, ac), a_halves, (tm, tk), lambda i, j, k: (i, k))
        b_spec = spec((br, bc), b_halves, (tk, tn), lambda i, j, k: (k, j))
        dn = _NN
    elif mode == "nt":
        a_spec = spec((ar, ac), a_halves, (tm, tk), lambda i, j, k: (i, k))
        b_spec = spec((br, bc), b_halves, (tn, tk), lambda i, j, k: (j, k))
        dn = _NT
    else:
        a_spec = spec((ar, ac), a_halves, (tk, tm), lambda i, j, k: (k, i))
        b_spec = spec((br, bc), b_halves, (tk, tn), lambda i, j, k: (k, j))
        dn = _TN
    o_spec = spec((M, N), o_halves, (tm, tn), lambda i, j, k: (i, j))
    o_shape = (2, M, N // 2) if o_halves else (M, N)

    def body(a_ref, b_ref, o_ref, acc_ref):
        k = pl.program_id(2)
        part = lax.dot_general(a_ref[...], b_ref[...], dn, preferred_element_type=F32)
        if nk == 1:
            o_ref[...] = part.astype(o_ref.dtype)
        else:
            @pl.when(k == 0)
            def _():
                acc_ref[...] = part

            @pl.when(k > 0)
            def _():
                acc_ref[...] += part

            @pl.when(k == nk - 1)
            def _():
                o_ref[...] = acc_ref[...].astype(o_ref.dtype)

    grid = (M // tm, N // tn, nk)
    body, xspecs, xshapes, xsems = _with_exchange(body, 2, 1, grid, comm)
    res = pl.pallas_call(
        body, name=name, grid=grid,
        in_specs=[a_spec, b_spec] + xspecs, out_specs=[o_spec] + xspecs,
        out_shape=[jax.ShapeDtypeStruct(o_shape, out_dtype)] + xshapes,
        scratch_shapes=[pltpu.VMEM((tm, tn), F32)] + xsems,
        compiler_params=pltpu.CompilerParams(dimension_semantics=("arbitrary", "arbitrary", "arbitrary"),
                                             vmem_limit_bytes=V7X_VMEM_LIMIT),
    )(a, b, *(comm[0] if comm else []))
    return res[0] if comm is None else (res[0], list(res[1:]))


def _full_spec(shape):
    nd = len(shape)
    return pl.BlockSpec(tuple(shape), lambda i, _nd=nd: (0,) * _nd)


def _row_spec(R, W, cb):
    return pl.BlockSpec((R, W), lambda i, _cb=cb: (i, _cb))


def _load_rows(refs, specs):
    vals = []
    for ref, (_, W, _, pw) in zip(refs, specs):
        if pw == W:
            vals.append(ref[...].astype(F32))
        else:
            vals.append([ref[:, k * pw:(k + 1) * pw].astype(F32) for k in range(W // pw)])
    return vals


def _load_params(refs, specs):
    vals = []
    for ref, (arr, split) in zip(refs, specs):
        if split:
            vals.append([ref[k] for k in range(arr.shape[0])])
        else:
            vals.append(ref[...])
    return vals


def _row_mods(pairs, first_row, rows, tc):
    is_latent = ((first_row + lax.broadcasted_iota(jnp.int32, (rows, 1), 0)) >= tc).astype(F32)
    return [mc + is_latent * (mx - mc) for mc, mx in pairs]


def _rowwise(name, f, R, tc, rows, consts, params, cparams, mods, outs, post=None):
    n = rows[0][0].shape[0]
    nr, nc, npar, ncp, nm = len(rows), len(consts), len(params), len(cparams), len(mods)

    def body(*refs):
        pos = 0
        rr = refs[pos:pos + nr]; pos += nr
        cr = refs[pos:pos + nc]; pos += nc
        pr = refs[pos:pos + npar]; pos += npar
        cpr = refs[pos:pos + ncp]; pos += ncp
        mr = refs[pos:pos + nm]; pos += nm
        orefs = refs[pos:]
        res = f(_load_rows(rr, rows), _load_params(pr, params),
                _row_mods([(m[0], m[1]) for m in mr], pl.program_id(0) * R, R, tc),
                _load_rows(cr, consts), _load_params(cpr, cparams))
        if post is not None:
            res = post(res, _load_rows(cr, consts))
        for o_ref, r in zip(orefs, res):
            o_ref[...] = r.astype(o_ref.dtype)

    in_specs = ([_row_spec(R, W, cb) for (_, W, cb, _) in rows + consts]
                + [_full_spec(a.shape) for (a, _) in params + cparams]
                + [_full_spec(m.shape) for m in mods])
    args = [a for (a, _, _, _) in rows + consts] + [a for (a, _) in params + cparams] + list(mods)
    return pl.pallas_call(
        body, name=name, grid=(n // R,), in_specs=in_specs,
        out_specs=[_row_spec(R, w, 0) for (w, _) in outs],
        out_shape=[jax.ShapeDtypeStruct((n, w), dt) for (w, dt) in outs],
        compiler_params=_ARB1,
    )(*args)


def _rowwise_bwd(name, f, R, tc, rows, consts, params, cparams, mods, douts, drow, adds=None, pre=None):
    n = rows[0][0].shape[0]
    adds = adds or [None] * len(rows)
    nr, nc, npar, ncp, nm, nd = len(rows), len(consts), len(params), len(cparams), len(mods), len(douts)
    add_ix = [k for k in range(nr) if adds[k] is not None]
    out_ix = [k for k in range(nr) if drow[k] is not None]

    def body(*refs):
        i = pl.program_id(0)
        pos = 0
        rr = refs[pos:pos + nr]; pos += nr
        cr = refs[pos:pos + nc]; pos += nc
        pr = refs[pos:pos + npar]; pos += npar
        cpr = refs[pos:pos + ncp]; pos += ncp
        mr = refs[pos:pos + nm]; pos += nm
        dr = refs[pos:pos + nd]; pos += nd
        ar = refs[pos:pos + len(add_ix)]; pos += len(add_ix)
        drr = refs[pos:pos + len(out_ix)]; pos += len(out_ix)
        dpr = refs[pos:pos + npar]; pos += npar
        dmr = refs[pos:pos + nm]; pos += nm

        cv = _load_rows(cr, consts)
        cpv = _load_params(cpr, cparams)
        _, vjp = jax.vjp(lambda rv, pv, mv: f(rv, pv, _row_mods(mv, i * R, R, tc), cv, cpv),
                         _load_rows(rr, rows), _load_params(pr, params), [(m[0], m[1]) for m in mr])
        dv = [d[...].astype(F32) for d in dr]
        if pre is not None:
            dv = pre(dv, cv)
        g_rows, g_params, g_mods = vjp(tuple(dv))

        for ref, k in zip(drr, out_ix):
            _, W, _, pw = rows[k]
            g = g_rows[k]
            extra = ar[add_ix.index(k)] if k in add_ix else None
            if pw == W:
                if extra is not None:
                    g = g + extra[...].astype(F32)
                ref[...] = g.astype(ref.dtype)
            else:
                for q in range(W // pw):
                    gq = g[q]
                    if extra is not None:
                        gq = gq + extra[:, q * pw:(q + 1) * pw].astype(F32)
                    ref[:, q * pw:(q + 1) * pw] = gq.astype(ref.dtype)

        @pl.when(i == 0)
        def _():
            for ref in tuple(dpr) + tuple(dmr):
                ref[...] = jnp.zeros_like(ref)

        for ref, (arr, split), g in zip(dpr, params, g_params):
            if split:
                for k in range(arr.shape[0]):
                    ref[k] += g[k]
            else:
                ref[...] += g

        for ref, (g_ctx, g_lat) in zip(dmr, g_mods):
            ref[0] += g_ctx
            ref[1] += g_lat

    in_specs = ([_row_spec(R, W, cb) for (_, W, cb, _) in rows + consts]
                + [_full_spec(a.shape) for (a, _) in params + cparams]
                + [_full_spec(m.shape) for m in mods]
                + [_row_spec(R, W, 0) for (_, W) in douts]
                + [_row_spec(R, rows[k][1], 0) for k in add_ix])
    args = ([a for (a, _, _, _) in rows + consts] + [a for (a, _) in params + cparams] + list(mods)
            + [a for (a, _) in douts] + [adds[k] for k in add_ix])
    out_specs = ([_row_spec(R, rows[k][1], 0) for k in out_ix]
                 + [_full_spec(a.shape) for (a, _) in params]
                 + [_full_spec(m.shape) for m in mods])
    out_shape = ([jax.ShapeDtypeStruct((n, rows[k][1]), drow[k]) for k in out_ix]
                 + [jax.ShapeDtypeStruct(a.shape, F32) for (a, _) in params]
                 + [jax.ShapeDtypeStruct(m.shape, F32) for m in mods])
    res = pl.pallas_call(
        body, name=name, grid=(n // R,), in_specs=in_specs, out_specs=out_specs, out_shape=out_shape,
        compiler_params=_ARB1,
    )(*args)
    no = len(out_ix)
    return list(res[:no]), list(res[no:no + npar]), list(res[no + npar:])


def _rms(x, g):
    return x * lax.rsqrt(jnp.mean(x * x, axis=-1, keepdims=True) + EPS) * g


def _f_lnmod(rv, pv, mv, cv, cpv):
    (x,), (g,), (shift, scale) = rv, pv, mv
    return (_rms(x, g) * (1.0 + scale) + shift,)


def _f_resid(rv, pv, mv, cv, cpv):
    (x, y), (gate,) = rv, mv
    return (x + gate * y,)


def _f_merge(rv, pv, mv, cv, cpv):
    ga, gb, gc, ya, yb, yc = rv
    return (_sigmoid(ga) * ya + _sigmoid(gb) * yb + _sigmoid(gc) * yc,)


def _split3(x):
    hi = x.astype(BF16)
    rest = x - hi.astype(F32)
    mid = rest.astype(BF16)
    return hi, mid, (rest - mid.astype(F32)).astype(BF16)


def _dot3_right(x, m):
    mb = m.astype(BF16)
    return sum(lax.dot_general(piece, mb, _NN, preferred_element_type=F32) for piece in _split3(x))


@jax.custom_vjp
def _sym_dot(x, m):
    return _dot3_right(x, m)


_sym_dot.defvjp(lambda x, m: (_dot3_right(x, m), m), lambda m, g: (_dot3_right(g, m), jnp.zeros_like(m)))


def _f_qknorm(rv, pv, mv, cv, cpv):
    (q, k), (gq, gk), (bdq, eq, bdk, ek) = rv, pv, cpv
    qn = q * lax.rsqrt(_sym_dot(q * q, bdq) + EPS) * _hdot(gq, eq)
    kn = k * lax.rsqrt(_sym_dot(k * k, bdk) + EPS) * _hdot(gk, ek)
    return (qn, kn)


def _rope(x, cos, sin):
    w = x.shape[1]
    lane = lax.broadcasted_iota(jnp.int32, x.shape, 1)
    partner = jnp.where((lane & 31) < 16, pltpu.roll(x, w - 16, 1), pltpu.roll(x, 16, 1))
    return x * cos + partner * sin


_LOG2E = 1.4426950408889634
_LN2 = 0.6931471805599453


def _qk_post(res, cv):
    (qn, kn), (cq, sq, ck, sk) = res, cv
    return (_rope(qn, cq, sq) * (HEAD_DIM ** -0.5 * _LOG2E), _rope(kn, ck, sk))


def _qk_pre(dv, cv):
    (dq, dk), (cq, sq, ck, sk) = dv, cv
    return (_rope(dq * (HEAD_DIM ** -0.5), cq, -sq), _rope(dk * _LN2, ck, -sk))


def _f_gmlp(rv, pv, mv, cv, cpv):
    (u, vs), (ng, ws, bt) = rv, pv
    pieces = []
    for g in range(GMLP_GROUPS):
        vn = _rms(jax.nn.gelu(vs[g]), ng[g])
        pieces.append(_bdot(ws[g], vn, "nn") + bt[g])
    return (jax.nn.gelu(u) * jnp.concatenate(pieces, axis=1),)


def _f_glaout(rv, pv, mv, cv, cpv):
    (ofs, obs, rs), (gn,) = rv, pv
    pieces = [_rms(ofs[h] + obs[h], gn[h]) * (rs[h] * _sigmoid(rs[h])) for h in range(GLA_HEADS)]
    return (jnp.concatenate(pieces, axis=1),)


_CONV_CB = 1408
_CONV_STRIP = 128


def _sigmoid(x):
    return 0.5 * jnp.tanh(0.5 * x) + 0.5


def _halo_keep(i, R, tc, n):
    first, end = i * R, (i + 1) * R
    keep_prev = jnp.where((first == 0) | (first == tc), 0.0, 1.0)
    keep_next = jnp.where((end == tc) | (end == n), 0.0, 1.0)
    return keep_prev, keep_next


def _conv_specs(R, n):
    nb8 = n // 8
    main = pl.BlockSpec((2, R, _CONV_CB), lambda j, i: (0, i, j))
    prev = pl.BlockSpec((2, 8, _CONV_CB), lambda j, i: (0, jnp.maximum(i * (R // 8) - 1, 0), j))
    nxt = pl.BlockSpec((2, 8, _CONV_CB), lambda j, i: (0, jnp.minimum((i + 1) * (R // 8), nb8 - 1), j))
    cw = pl.BlockSpec((2, 3, _CONV_CB), lambda j, i: (0, 0, j))
    cb = pl.BlockSpec((2, 1, _CONV_CB), lambda j, i: (0, 0, j))
    return main, prev, nxt, cw, cb


def _row_before(x):
    return pltpu.roll(x, 1, 0)


def _row_after(x):
    return pltpu.roll(x, x.shape[0] - 1, 0)


def _conv_gate(name, a2, cw, cb, R, tc):
    n = a2.shape[1]
    main, prev, nxt, cws, cbs = _conv_specs(R, n)

    def body(a_ref, p_ref, n_ref, cw_ref, cb_ref, o_ref):
        keep_prev, keep_next = _halo_keep(pl.program_id(1), R, tc, n)

        def strip(c, carry):
            ls = pl.ds(pl.multiple_of(c * _CONV_STRIP, _CONV_STRIP), _CONV_STRIP)
            acts = []
            for h in range(2):
                win = jnp.concatenate([p_ref[h, :, ls] * keep_prev, a_ref[h, :, ls], n_ref[h, :, ls] * keep_next], axis=0)
                acts.append((cb_ref[h, :, ls] + cw_ref[h, 1:2, ls] * win + cw_ref[h, 0:1, ls] * _row_before(win)
                             + cw_ref[h, 2:3, ls] * _row_after(win))[8:8 + R])
            g, v = acts
            o_ref[:, ls] = (g * _sigmoid(g) * v).astype(o_ref.dtype)
            return carry

        lax.fori_loop(0, _CONV_CB // _CONV_STRIP, strip, 0)

    return pl.pallas_call(
        body, name=name, grid=(FFN_H // _CONV_CB, n // R),
        in_specs=[main, prev, nxt, cws, cbs],
        out_specs=pl.BlockSpec((R, _CONV_CB), lambda j, i: (i, j)),
        out_shape=jax.ShapeDtypeStruct((n, FFN_H), BF16),
        compiler_params=pltpu.CompilerParams(dimension_semantics=("arbitrary", "arbitrary"),
                                             vmem_limit_bytes=V7X_VMEM_LIMIT),
    )(a2, a2, a2, cw, cb)


def _conv_gate_bwd(name, a2, dgv, cw, cb, R, tc):
    n = a2.shape[1]
    nb8 = n // 8
    main, prev, nxt, cws, cbs = _conv_specs(R, n)
    d_main = pl.BlockSpec((R, _CONV_CB), lambda j, i: (i, j))
    d_prev = pl.BlockSpec((8, _CONV_CB), lambda j, i: (jnp.maximum(i * (R // 8) - 1, 0), j))
    d_next = pl.BlockSpec((8, _CONV_CB), lambda j, i: (jnp.minimum((i + 1) * (R // 8), nb8 - 1), j))
    mid = slice(8, 8 + R)

    def body(a_ref, p_ref, n_ref, cw_ref, cb_ref, d_ref, dp_ref, dn_ref, da_ref, dcw_ref, dcb_ref):
        i = pl.program_id(1)
        keep_prev, keep_next = _halo_keep(i, R, tc, n)

        @pl.when(i == 0)
        def _():
            dcw_ref[...] = jnp.zeros_like(dcw_ref)
            dcb_ref[...] = jnp.zeros_like(dcb_ref)

        def strip(c, carry):
            ls = pl.ds(pl.multiple_of(c * _CONV_STRIP, _CONV_STRIP), _CONV_STRIP)
            cws_ = [[cw_ref[h, k:k + 1, ls] for k in range(3)] for h in range(2)]
            wins = [jnp.concatenate([p_ref[h, :, ls] * keep_prev, a_ref[h, :, ls], n_ref[h, :, ls] * keep_next], axis=0)
                    for h in range(2)]
            g, v = [cb_ref[h, :, ls] + cws_[h][1] * wins[h] + cws_[h][0] * _row_before(wins[h]) + cws_[h][2] * _row_after(wins[h])
                    for h in range(2)]
            dout = jnp.concatenate([dp_ref[:, ls].astype(F32) * keep_prev, d_ref[:, ls].astype(F32),
                                    dn_ref[:, ls].astype(F32) * keep_next], axis=0)
            sg = _sigmoid(g)
            das = [dout * v * sg * (1.0 + g * (1.0 - sg)), dout * g * sg]
            for h in range(2):
                da_win = das[h]
                da_ref[h, :, ls] = (cws_[h][1] * da_win + cws_[h][0] * _row_after(da_win)
                                    + cws_[h][2] * _row_before(da_win))[mid].astype(da_ref.dtype)
                da = da_win[mid]
                dcw_ref[h, 0:1, ls] += jnp.sum(da * _row_before(wins[h])[mid], axis=0, keepdims=True)
                dcw_ref[h, 1:2, ls] += jnp.sum(da * wins[h][mid], axis=0, keepdims=True)
                dcw_ref[h, 2:3, ls] += jnp.sum(da * _row_after(wins[h])[mid], axis=0, keepdims=True)
                dcb_ref[h, :, ls] += jnp.sum(da, axis=0, keepdims=True)
            return carry

        lax.fori_loop(0, _CONV_CB // _CONV_STRIP, strip, 0)

    return pl.pallas_call(
        body, name=name, grid=(FFN_H // _CONV_CB, n // R),
        in_specs=[main, prev, nxt, cws, cbs, d_main, d_prev, d_next],
        out_specs=[main, cws, cbs],
        out_shape=[jax.ShapeDtypeStruct((2, n, FFN_H), BF16), jax.ShapeDtypeStruct((2, 3, FFN_H), F32),
                   jax.ShapeDtypeStruct((2, 1, FFN_H), F32)],
        compiler_params=pltpu.CompilerParams(dimension_semantics=("arbitrary", "arbitrary"),
                                             vmem_limit_bytes=V7X_VMEM_LIMIT),
    )(a2, a2, a2, cw, cb, dgv, dgv, dgv)


_ARB2 = pltpu.CompilerParams(dimension_semantics=("arbitrary", "arbitrary"), vmem_limit_bytes=V7X_VMEM_LIMIT)


def _with_exchange(body, n_in, n_out, grid, comm):
    if comm is None:
        return body, [], [], []
    xs, gather = comm
    na = len(xs)
    specs, shapes, sems = _exchange_io(xs, gather)

    def wrapped(*refs):
        ins, x_refs = refs[:n_in], refs[n_in:n_in + na]
        outs, o_refs = refs[n_in + na:n_in + na + n_out], refs[n_in + na + n_out:n_in + 2 * na + n_out]
        scratch, sem_refs = refs[n_in + 2 * na + n_out:-3], refs[-3:]
        start, wait = _exchange_plan(x_refs, o_refs, *sem_refs, gather)
        ids = [pl.program_id(d) for d in range(len(grid))]
        first, last = ids[0] == 0, ids[0] == grid[0] - 1
        for d in range(1, len(grid)):
            first, last = first & (ids[d] == 0), last & (ids[d] == grid[d] - 1)

        @pl.when(first)
        def _():
            start()

        body(*ins, *outs, *scratch)

        @pl.when(last)
        def _():
            wait()

    return wrapped, specs, shapes, sems


def _attn_fwd(name, qt8, k2, vt2, comm=None):
    nq, nk = qt8.shape[2], k2.shape[1]
    tq = _pick(nq, (256, 128))
    tk = _pick(nk, (1408, 768, 256, 128))

    va = vt2.shape[1]

    def body(qt_ref, k_ref, vt_ref, ot_ref, lse_ref, m_ref, acc_ref):
        m_ref[...] = jnp.full((Q_GROUP, 1, tq), -1e30, F32)
        acc_ref[...] = jnp.zeros((Q_GROUP, va, tq), F32)

        def step(j, carry):
            sl = pl.ds(pl.multiple_of(j * tk, tk), tk)
            kj = k_ref[sl, :]
            vtj = vt_ref[:, sl]
            sts = [lax.dot_general(kj, qt_ref[h], _NN, preferred_element_type=F32) for h in range(Q_GROUP)]
            m_old = [m_ref[h] for h in range(Q_GROUP)]
            acc_old = [acc_ref[h] for h in range(Q_GROUP)]
            m_new = [jnp.maximum(m_old[h], jnp.max(sts[h], axis=0, keepdims=True)) for h in range(Q_GROUP)]
            pts = [jnp.exp2(sts[h] - m_new[h]).astype(BF16) for h in range(Q_GROUP)]
            pvs = [lax.dot_general(vtj, pts[h], _NN, preferred_element_type=F32) for h in range(Q_GROUP)]
            for h in range(Q_GROUP):
                acc_ref[h] = jnp.exp2(m_old[h] - m_new[h]) * acc_old[h] + pvs[h]
                m_ref[h] = m_new[h]
            return carry

        lax.fori_loop(0, nk // tk, step, 0, unroll=2)
        for h in range(Q_GROUP):
            l = acc_ref[h, HEAD_DIM:HEAD_DIM + 1, :]
            ot_ref[h] = (acc_ref[h, 0:HEAD_DIM, :] / l).astype(ot_ref.dtype)
            lse_ref[h] = m_ref[h] + jnp.log2(l)

    qspec = pl.BlockSpec((Q_GROUP, HEAD_DIM, tq), lambda g, i: (g, 0, i))
    lspec = pl.BlockSpec((Q_GROUP, 1, tq), lambda g, i: (g, 0, i))
    grid = (N_KV_HEADS, nq // tq)
    body, xspecs, xshapes, xsems = _with_exchange(body, 3, 2, grid, comm)
    res = pl.pallas_call(
        body, name=name, grid=grid,
        in_specs=[qspec, pl.BlockSpec((None, nk, HEAD_DIM), lambda g, i: (g, 0, 0)),
                  pl.BlockSpec((None, va, nk), lambda g, i: (g, 0, 0))] + xspecs,
        out_specs=[qspec, lspec] + xspecs,
        out_shape=[jax.ShapeDtypeStruct((N_Q_HEADS, HEAD_DIM, nq), BF16), jax.ShapeDtypeStruct((N_Q_HEADS, 1, nq), F32)] + xshapes,
        scratch_shapes=[pltpu.VMEM((Q_GROUP, 1, tq), F32), pltpu.VMEM((Q_GROUP, va, tq), F32)] + xsems,
        compiler_params=_ARB2,
    )(qt8, k2, vt2, *(comm[0] if comm else []))
    return res[0], res[1], list(res[2:])


def _attn_bwd(name, qt8, k2, v2, kt2, ot8, dot8, lse, dkt0, dvt0, comm=None):
    nq, nk = qt8.shape[2], k2.shape[1]
    n0 = dkt0.shape[2]
    tq = _pick(nq, (256, 128))
    tk = _pick(nk, (1408, 768, 256, 128))
    ts = tk
    heads = range(Q_GROUP)

    def body(qt_ref, k_ref, v_ref, kt_ref, ot_ref, dot_ref, lse_ref, dk0_ref, dv0_ref, dqt_ref, dkt_ref, dvt_ref, dl_ref, dq_acc):
        @pl.when(pl.program_id(1) == 0)
        def _():
            dkt_ref[...] = jnp.zeros_like(dkt_ref)
            dvt_ref[...] = jnp.zeros_like(dvt_ref)
            dkt_ref[:, 0:n0] = dk0_ref[...]
            dvt_ref[:, 0:n0] = dv0_ref[...]

        for h in heads:
            dl_ref[h] = jnp.sum(dot_ref[h].astype(F32) * ot_ref[h].astype(F32), axis=0, keepdims=True)
        dq_acc[...] = jnp.zeros((Q_GROUP, HEAD_DIM, tq), F32)

        items = [(s, h) for s in range(tk // ts) for h in heads]

        def step(j, carry):
            def keys(s):
                return pl.ds(pl.multiple_of(j * tk + s * ts, ts), ts)

            def scores(item):
                s, h = item
                return (lax.dot_general(k_ref[keys(s), :], qt_ref[h], _NN, preferred_element_type=F32),
                        lax.dot_general(v_ref[keys(s), :], dot_ref[h], _NN, preferred_element_type=F32))

            nxt = scores(items[0])
            for n, (s, h) in enumerate(items):
                st, dpt = nxt
                if n + 1 < len(items):
                    nxt = scores(items[n + 1])
                pt = jnp.exp2(st - lse_ref[h])
                dst = (pt * (dpt - dl_ref[h])).astype(BF16)
                dq_acc[h] += lax.dot_general(kt_ref[:, keys(s)], dst, _NN, preferred_element_type=F32)
                dv_h = lax.dot_general(dot_ref[h], pt.astype(BF16), _NT, preferred_element_type=F32)
                dk_h = lax.dot_general(qt_ref[h], dst, _NT, preferred_element_type=F32)
                dvt_s, dkt_s = (dv_h, dk_h) if h == 0 else (dvt_s + dv_h, dkt_s + dk_h)
                if h == Q_GROUP - 1:
                    dvt_ref[:, keys(s)] += dvt_s
                    dkt_ref[:, keys(s)] += dkt_s
            return carry

        lax.fori_loop(0, nk // tk, step, 0)
        dqt_ref[...] = dq_acc[...]

    tspec = pl.BlockSpec((Q_GROUP, HEAD_DIM, tq), lambda g, i: (g, 0, i))
    lspec = pl.BlockSpec((Q_GROUP, 1, tq), lambda g, i: (g, 0, i))
    kspec = pl.BlockSpec((None, nk, HEAD_DIM), lambda g, i: (g, 0, 0))
    ktspec = pl.BlockSpec((None, HEAD_DIM, nk), lambda g, i: (g, 0, 0))
    k0spec = pl.BlockSpec((None, HEAD_DIM, n0), lambda g, i: (g, 0, 0))
    grid = (N_KV_HEADS, nq // tq)
    body, xspecs, xshapes, xsems = _with_exchange(body, 9, 3, grid, comm)
    res = pl.pallas_call(
        body, name=name, grid=grid,
        in_specs=[tspec, kspec, kspec, ktspec, tspec, tspec, lspec, k0spec, k0spec] + xspecs,
        out_specs=[tspec, ktspec, ktspec] + xspecs,
        out_shape=[jax.ShapeDtypeStruct((N_Q_HEADS, HEAD_DIM, nq), F32), jax.ShapeDtypeStruct((N_KV_HEADS, HEAD_DIM, nk), F32),
                   jax.ShapeDtypeStruct((N_KV_HEADS, HEAD_DIM, nk), F32)] + xshapes,
        scratch_shapes=[pltpu.VMEM((Q_GROUP, 1, tq), F32), pltpu.VMEM((Q_GROUP, HEAD_DIM, tq), F32)] + xsems,
        compiler_params=_ARB2,
    )(qt8, k2, v2, kt2, ot8, dot8, lse, dkt0, dvt0, *(comm[0] if comm else []))
    return res[0], res[1], res[2], list(res[3:])


def _log_sigmoid(z):
    return jnp.minimum(z, 0.0) - jnp.log(1.0 + jnp.exp(-jnp.abs(z)))


_BDOT_DIMS = {"nn": _NN, "nt": _NT, "tn": _TN}
_BDOT_BWD = {"nn": (("nt", "gb"), ("tn", "ag")), "nt": (("nn", "gb"), ("tn", "ga")), "tn": (("nt", "bg"), ("nn", "ag"))}


def _bdot_raw(a, b, mode):
    return lax.dot_general(a.astype(BF16), b.astype(BF16), _BDOT_DIMS[mode], preferred_element_type=F32)


@functools.partial(jax.custom_vjp, nondiff_argnums=(2,))
def _bdot(a, b, mode):
    return _bdot_raw(a, b, mode)


def _bdot_fwd(a, b, mode):
    return _bdot_raw(a, b, mode), (a.astype(BF16), b.astype(BF16))


def _bdot_bwd(mode, res, g):
    ops = {"a": res[0], "b": res[1], "g": g}
    (ma, oa), (mb, ob) = _BDOT_BWD[mode]
    return _bdot_raw(ops[oa[0]], ops[oa[1]], ma), _bdot_raw(ops[ob[0]], ops[ob[1]], mb)


_bdot.defvjp(_bdot_fwd, _bdot_bwd)


def _tile_tri(rev, rows):
    r_i = lax.broadcasted_iota(jnp.int32, (rows, rows), 0)
    c_i = lax.broadcasted_iota(jnp.int32, (rows, rows), 1)
    same = (r_i // GLA_CHUNK) == (c_i // GLA_CHUNK)
    return same & ((c_i >= r_i) if rev else (c_i <= r_i))


def _tri_dot(rev, x):
    tri = _tile_tri(rev, x.shape[0]).astype(BF16)
    return sum(lax.dot_general(tri, piece, _NN, preferred_element_type=F32) for piece in _split3(x))


@functools.partial(jax.custom_vjp, nondiff_argnums=(1,))
def _chunk_cumsum(x, rev):
    return _tri_dot(rev, x)


_chunk_cumsum.defvjp(lambda x, rev: (_tri_dot(rev, x), None), lambda rev, _, g: (_tri_dot(not rev, g),))


_GLA_REV = (False, True)


def _gla_tile_pair(qs, ks, vss, as_, w2s, b2s, states):
    both = range(2)
    rows = qs[0].shape[0]
    nch = rows // GLA_CHUNK
    tris = [_tile_tri(_GLA_REV[d], rows) for d in both]
    chunk_of_row = lax.broadcasted_iota(jnp.int32, (rows, 1), 0) // GLA_CHUNK
    in_chunk = [(chunk_of_row == c).astype(F32) for c in range(nch)]
    las = [_log_sigmoid(_bdot(as_[d], w2s[d], "nn") + b2s[d]) * (1.0 / GLA_TAU) for d in both]
    cums = [_chunk_cumsum(las[d], _GLA_REV[d]) for d in both]
    tots = [[jnp.sum(las[d] * in_chunk[c], axis=0, keepdims=True) for c in range(nch)] for d in both]
    tot_rows = [sum(in_chunk[c] * tots[d][c] for c in range(nch)) for d in both]
    q_in = [qs[d] * (GLA_DK ** -0.5) * jnp.exp(cums[d]) for d in both]
    k_in = [ks[d] * jnp.exp(-cums[d]) for d in both]
    k_st = [ks[d] * jnp.exp(tot_rows[d] - cums[d]) for d in both]
    lane = lax.broadcasted_iota(jnp.int32, (1, GLA_QK_W), 1)
    outs = [[], []]
    for h in range(GLA_HEADS):
        head = ((lane >= GLA_DK * h) & (lane < GLA_DK * (h + 1))).astype(F32)
        atts = [jnp.where(tris[d], _bdot(q_in[d] * head, k_in[d], "nt"), 0.0) for d in both]
        for d in both:
            outs[d].append(_bdot(atts[d], vss[d][h], "nn"))
    os_ = [jnp.concatenate(outs[d], axis=1) for d in both]
    hr = lax.broadcasted_iota(jnp.int32, (GLA_V_W, GLA_QK_W), 0) // GLA_DV
    hc = lax.broadcasted_iota(jnp.int32, (GLA_V_W, GLA_QK_W), 1) // GLA_DK
    same_head = (hr == hc).astype(F32)
    v_all = [jnp.concatenate(vss[d], axis=1) for d in both]
    states = list(states)
    for step in range(nch):
        chunk = (step, nch - 1 - step)
        us = [_bdot(v_all[d], k_st[d] * in_chunk[chunk[d]], "tn") * same_head for d in both]
        for d in both:
            os_[d] = os_[d] + _bdot(q_in[d] * in_chunk[chunk[d]], states[d], "nt")
            states[d] = jnp.exp(tots[d][chunk[d]]) * states[d] + us[d]
    return os_, states


def _gla_tile_of(step, rev, nct, nt):
    if not rev:
        return step
    return jnp.where(step < nct, nct - 1 - step, nt - 1 - (step - nct))


def _gla_row_specs(R, tile):
    return [pl.BlockSpec((R, GLA_QK_W), lambda s: (tile(s), C_GQ // GLA_QK_W)),
            pl.BlockSpec((R, GLA_QK_W), lambda s: (tile(s), C_GK // GLA_QK_W)),
            pl.BlockSpec((R, GLA_V_W), lambda s: (tile(s), C_GV // GLA_V_W)),
            pl.BlockSpec((R, 128), lambda s: (tile(s), C_AF // 128))]


_GLA_WIDTHS = (GLA_QK_W, GLA_QK_W, GLA_V_W, 128)


def _gla_load(refs):
    q_ref, k_ref, v_ref, a_ref = refs
    return q_ref[...], k_ref[...], [v_ref[:, GLA_DV * h:GLA_DV * (h + 1)] for h in range(GLA_HEADS)], a_ref[...]


def _gla_fwd(name, p, w2s, b2s, R, nct):
    n = p.shape[0]
    nt = n // R
    tiles = [lambda s, d=d: _gla_tile_of(s, _GLA_REV[d], nct, nt) for d in range(2)]

    def body(*refs):
        rows, (w2f, b2f, w2b, b2b), (of, ob, sf, sb), states = refs[:8], refs[8:12], refs[12:16], refs[16:]

        @pl.when(pl.program_id(0) == 0)
        def _():
            for st in states:
                st[...] = jnp.zeros_like(st)

        ins = [_gla_load(rows[:4]), _gla_load(rows[4:])]
        s_in = [states[0][...], states[1][...]]
        sf[...], sb[...] = s_in
        os_, s_out = _gla_tile_pair([i[0] for i in ins], [i[1] for i in ins], [i[2] for i in ins], [i[3] for i in ins],
                                    [w2f[...], w2b[...]], [b2f[...], b2b[...]], s_in)
        of[...], ob[...] = os_
        states[0][...], states[1][...] = s_out

    in_specs = (_gla_row_specs(R, tiles[0]) + _gla_row_specs(R, tiles[1])
                + [_full_spec(w2s[0].shape), _full_spec(b2s[0].shape), _full_spec(w2s[1].shape), _full_spec(b2s[1].shape)])
    o_specs = [pl.BlockSpec((R, GLA_V_W), lambda s, t=t: (t(s), 0)) for t in tiles]
    s_specs = [pl.BlockSpec((None, GLA_V_W, GLA_QK_W), lambda s, t=t: (t(s), 0, 0)) for t in tiles]
    return pl.pallas_call(
        body, name=name, grid=(nt,), in_specs=in_specs, out_specs=o_specs + s_specs,
        out_shape=[jax.ShapeDtypeStruct((n, GLA_V_W), F32)] * 2 + [jax.ShapeDtypeStruct((nt, GLA_V_W, GLA_QK_W), F32)] * 2,
        scratch_shapes=[pltpu.VMEM((GLA_V_W, GLA_QK_W), F32)] * 2,
        compiler_params=_ARB1,
    )(*([p] * 8), w2s[0], b2s[0], w2s[1], b2s[1])


def _gla_bwd(name, p, w2s, b2s, ssaves, do, R, nct):
    n = p.shape[0]
    nt = n // R
    tiles = [lambda s, d=d: _gla_tile_of(nt - 1 - s, _GLA_REV[d], nct, nt) for d in range(2)]

    def body(*refs):
        rows, (w2f, b2f, w2b, b2b), ss, dos = refs[:8], refs[8:12], refs[12:14], refs[14:16]
        d_rows, (dw2f, db2f, dw2b, db2b), dstates = refs[16:24], refs[24:28], refs[28:]

        @pl.when(pl.program_id(0) == 0)
        def _():
            for ref in (dw2f, db2f, dw2b, db2b) + tuple(dstates):
                ref[...] = jnp.zeros_like(ref)

        ins = [_gla_load(rows[:4]), _gla_load(rows[4:])]
        _, vjp = jax.vjp(_gla_tile_pair, [i[0] for i in ins], [i[1] for i in ins], [i[2] for i in ins], [i[3] for i in ins],
                         [w2f[...], w2b[...]], [b2f[...], b2b[...]], [ss[0][...], ss[1][...]])
        dqs, dks, dvss, das, dw2, db2, ds = vjp(([dos[0][...], dos[1][...]], [dstates[0][...], dstates[1][...]]))
        for d in range(2):
            grads = [dqs[d], dks[d], jnp.concatenate(dvss[d], axis=1), das[d]]
            for ref, g in zip(d_rows[4 * d:4 * d + 4], grads):
                ref[...] = g
            dstates[d][...] = ds[d]
        dw2f[...] += dw2[0]
        db2f[...] += db2[0]
        dw2b[...] += dw2[1]
        db2b[...] += db2[1]

    par_specs = [_full_spec(w2s[0].shape), _full_spec(b2s[0].shape), _full_spec(w2s[1].shape), _full_spec(b2s[1].shape)]
    d_specs = [pl.BlockSpec((R, w), lambda s, t=t: (t(s), 0)) for t in tiles for w in _GLA_WIDTHS]
    in_specs = (_gla_row_specs(R, tiles[0]) + _gla_row_specs(R, tiles[1]) + par_specs
                + [pl.BlockSpec((None, GLA_V_W, GLA_QK_W), lambda s, t=t: (t(s), 0, 0)) for t in tiles]
                + [pl.BlockSpec((R, GLA_V_W), lambda s, t=t: (t(s), 0)) for t in tiles])
    res = pl.pallas_call(
        body, name=name, grid=(nt,), in_specs=in_specs, out_specs=d_specs + par_specs,
        out_shape=[jax.ShapeDtypeStruct((n, w), F32) for _ in range(2) for w in _GLA_WIDTHS]
        + [jax.ShapeDtypeStruct(a.shape, F32) for a in (w2s[0], b2s[0], w2s[1], b2s[1])],
        scratch_shapes=[pltpu.VMEM((GLA_V_W, GLA_QK_W), F32)] * 2,
        compiler_params=_ARB1,
    )(*([p] * 8), w2s[0], b2s[0], w2s[1], b2s[1], ssaves[0], ssaves[1], do, do)
    return res[:4], res[4:8], res[8:]


def _final_loss(name, x, target, gf, R, nct):
    n = x.shape[0]

    def body(x_ref, t_ref, g_ref, loss_ref, dx_ref, dg_ref):
        i = pl.program_id(0)

        @pl.when(i == 0)
        def _():
            loss_ref[...] = jnp.zeros_like(loss_ref)
            dg_ref[...] = jnp.zeros_like(dg_ref)

        @pl.when(i < nct)
        def _():
            dx_ref[...] = jnp.zeros_like(dx_ref)

        @pl.when(i >= nct)
        def _():
            y, vjp = jax.vjp(_rms, x_ref[...], g_ref[...])
            err = y - t_ref[...]
            loss_ref[...] += jnp.sum(0.5 * jnp.mean(err * err, axis=-1, keepdims=True))
            dx, dg = vjp(err * (1.0 / D))
            dx_ref[...] = dx
            dg_ref[...] += dg

    return pl.pallas_call(
        body, name=name, grid=(n // R,),
        in_specs=[_row_spec(R, D, 0), pl.BlockSpec((R, D), lambda i: (jnp.maximum(i - nct, 0), 0)), _full_spec((1, D))],
        out_specs=[_full_spec((8, 128)), _row_spec(R, D, 0), _full_spec((1, D))],
        out_shape=[jax.ShapeDtypeStruct((8, 128), F32), jax.ShapeDtypeStruct((n, D), F32), jax.ShapeDtypeStruct((1, D), F32)],
        compiler_params=_ARB1,
    )(x, target, gf)


def _adamw(name, w, m, v, gparts):
    rows, cols = w.shape
    nparts = gparts.shape[0]
    tr = rows
    for cand in range(min(rows, 256), 15, -16):
        if rows % cand == 0:
            tr = cand
            break

    def body(w_ref, m_ref, v_ref, g_ref, go_ref, d_ref, mo_ref, vo_ref):
        g = g_ref[0].astype(F32)
        for k in range(1, nparts):
            g = g + g_ref[k].astype(F32)
        m_new = ADAM_B1 * m_ref[...] + (1.0 - ADAM_B1) * g
        v_new = ADAM_B2 * v_ref[...] + (1.0 - ADAM_B2) * (g * g)
        m_hat = m_new / (1.0 - ADAM_B1 ** ADAM_STEP)
        v_hat = v_new / (1.0 - ADAM_B2 ** ADAM_STEP)
        go_ref[...] = g
        d_ref[...] = -ADAM_LR * (m_hat / (jnp.sqrt(v_hat) + ADAM_EPS) + ADAM_WD * w_ref[...])
        mo_ref[...] = m_new
        vo_ref[...] = v_new

    spec = pl.BlockSpec((tr, cols), lambda i: (i, 0))
    return pl.pallas_call(
        body, name=name, grid=(rows // tr,),
        in_specs=[spec, spec, spec, pl.BlockSpec((nparts, tr, cols), lambda i: (0, i, 0))],
        out_specs=[spec] * 4, out_shape=[jax.ShapeDtypeStruct((rows, cols), F32)] * 4,
        compiler_params=_ARB1,
    )(w, m, v, gparts)


def _my_index():
    return 4 * lax.axis_index("x") + 2 * lax.axis_index("y") + lax.axis_index("c")


def _xor_peer(k):
    flip = lambda a, bit: (1 - a) if bit else a
    pos = (flip(lax.axis_index("x"), (k >> 2) & 1), flip(lax.axis_index("y"), (k >> 1) & 1), flip(lax.axis_index("c"), k & 1))
    return pos, 4 * pos[0] + 2 * pos[1] + pos[2]


def _exchange_plan(x_refs, o_refs, send_sems, recv_sems, local_sems, gather):
    npeer = N_DEV - 1
    me = _my_index()
    locals_, sends, recvs = [], [], []
    for a, (x_ref, o_ref) in enumerate(zip(x_refs, o_refs)):
        mine = x_ref if gather else x_ref.at[me]
        locals_.append(pltpu.make_async_copy(mine, o_ref.at[me], local_sems.at[a]))
        for k in range(1, N_DEV):
            pos, lin = _xor_peer(k)
            src = x_ref if gather else x_ref.at[lin]
            sem = a * npeer + k - 1
            sends.append(pltpu.make_async_remote_copy(src_ref=src, dst_ref=o_ref.at[me], send_sem=send_sems.at[sem],
                                                      recv_sem=recv_sems.at[sem], device_id=pos, device_id_type=MESH_ID))
            recvs.append(pltpu.make_async_remote_copy(src_ref=src, dst_ref=o_ref.at[lin], send_sem=send_sems.at[sem],
                                                      recv_sem=recv_sems.at[sem], device_id=pos, device_id_type=MESH_ID))

    def start():
        for cp in locals_ + sends:
            cp.start()

    def wait():
        for cp in recvs:
            cp.wait_recv()
        for cp in sends:
            cp.wait_send()
        for cp in locals_:
            cp.wait()

    return start, wait


def _exchange_io(xs, gather):
    na = len(xs)
    hbm = pl.BlockSpec(memory_space=pltpu.HBM)
    shapes = [jax.ShapeDtypeStruct((N_DEV,) + tuple(x.shape if gather else x.shape[1:]), x.dtype) for x in xs]
    sems = [pltpu.SemaphoreType.DMA((na * (N_DEV - 1),)), pltpu.SemaphoreType.DMA((na * (N_DEV - 1),)),
            pltpu.SemaphoreType.DMA((na,))]
    return [hbm] * na, shapes, sems


def _exchange(name, xs, *, gather):
    na = len(xs)
    specs, shapes, sems = _exchange_io(xs, gather)

    def body(*refs):
        start, wait = _exchange_plan(refs[:na], refs[na:2 * na], *refs[2 * na:], gather)
        start()
        wait()

    return list(pl.pallas_call(body, name=name, in_specs=specs, out_specs=specs, out_shape=shapes, scratch_shapes=sems)(*xs))


def _adaln_fwd(name, craw16, w_ada, b_cols):
    def body(c_ref, w_ref, b_ref, o_ref):
        cs = jax.nn.silu(c_ref[...]).astype(BF16)
        for l in range(2):
            o_ref[l] = lax.dot_general(cs, w_ref[l].astype(BF16), _NN, preferred_element_type=F32) + b_ref[l]

    return pl.pallas_call(
        body, name=name, out_shape=jax.ShapeDtypeStruct((2, 16, w_ada.shape[2]), F32),
        compiler_params=pltpu.CompilerParams(vmem_limit_bytes=V7X_VMEM_LIMIT),
    )(craw16, w_ada, b_cols)


def _adaln_bwd(name, craw16, w_ada, dm):
    def body(c_ref, w_ref, dm_ref, gw_ref, dc_ref):
        c = c_ref[...]
        sg = jax.nn.sigmoid(c)
        cs = c * sg
        row = lax.broadcasted_iota(jnp.int32, (8, 1), 0)
        dc = jnp.zeros((16, D), F32)
        for l in range(2):
            dmx = dm_ref[2 * l + 1]
            dmc = jnp.where(row == 0, jnp.sum(dm_ref[2 * l], axis=0, keepdims=True), 0.0)
            gw_ref[l] = _hdot(cs[0:8], dmx, _TN) + _hdot(cs[8:16], dmc, _TN)
            dc = dc + _hdot(jnp.concatenate([dmx, dmc], axis=0), w_ref[l], _NT)
        dc_ref[...] = dc * sg * (1.0 + c * (1.0 - sg))

    return pl.pallas_call(
        body, name=name,
        out_shape=[jax.ShapeDtypeStruct(w_ada.shape, F32), jax.ShapeDtypeStruct((16, D), F32)],
        compiler_params=pltpu.CompilerParams(vmem_limit_bytes=V7X_VMEM_LIMIT),
    )(craw16, w_ada, dm)


_IN_OFFS = [sum(IN_SPLITS[:k]) for k in range(len(IN_SPLITS) + 1)]
_MY_ORDER = (11, 12, 13, 0, 1, 2, 7, 10, 5, 6, 3, 4, 8, 9)


_IN_SHARD = IN_WIDTH // N_DEV


def _win_my_cols(pieces):
    parts = []
    for k in _MY_ORDER:
        a, b = _IN_OFFS[k], _IN_OFFS[k + 1]
        for s in range(a // _IN_SHARD, (b - 1) // _IN_SHARD + 1):
            lo, hi = max(a, s * _IN_SHARD), min(b, (s + 1) * _IN_SHARD)
            parts.append(pieces[s][:, lo - s * _IN_SHARD:hi - s * _IN_SHARD])
    parts.append(jnp.zeros((pieces[0].shape[0], PW - IN_WIDTH), pieces[0].dtype))
    return jnp.concatenate(parts, axis=1)


def _win_shards(wp):
    my_offs, pos = {}, 0
    for k in _MY_ORDER:
        my_offs[k] = pos
        pos += IN_SPLITS[k]
    shards = []
    for s in range(N_DEV):
        parts = []
        for k in range(len(IN_SPLITS)):
            lo, hi = max(_IN_OFFS[k], s * _IN_SHARD), min(_IN_OFFS[k + 1], (s + 1) * _IN_SHARD)
            if lo < hi:
                parts.append(wp[:, my_offs[k] + lo - _IN_OFFS[k]:my_offs[k] + hi - _IN_OFFS[k]])
        shards.append(jnp.concatenate(parts, axis=1))
    return jnp.stack(shards)


def _tile_friendly(shape, axis):
    width = shape[axis] // N_DEV if axis == len(shape) - 1 else 128
    return len(shape) >= 2 and width % 128 == 0


def _split_shards(full, axis):
    s = full.shape
    if _tile_friendly(s, axis):
        return jnp.moveaxis(full.reshape(s[:axis] + (N_DEV, s[axis] // N_DEV) + s[axis + 1:]), axis, 0)
    c = s[axis] // N_DEV
    return jnp.stack([lax.slice_in_dim(full, k * c, (k + 1) * c, axis=axis) for k in range(N_DEV)])


def _join_shards(g, axis):
    s = g.shape[1:]
    full_shape = s[:axis] + (N_DEV * s[axis],) + s[axis + 1:]
    if _tile_friendly(full_shape, axis):
        return jnp.moveaxis(g, 0, axis).reshape(full_shape)
    return jnp.concatenate([g[k] for k in range(N_DEV)], axis=axis)


def _pack_rows(pieces, row_mult):
    rows = jnp.concatenate([p.reshape(-1, 128) for p in pieces], axis=0)
    padn = (-rows.shape[0]) % row_mult
    if padn:
        rows = jnp.concatenate([rows, jnp.zeros((padn, 128), rows.dtype)], axis=0)
    return rows


def _unpack_rows(rows, shapes):
    out, pos = [], 0
    for s in shapes:
        size = 1
        for d in s:
            size *= d
        out.append(rows[pos:pos + size // 128].reshape(tuple(s)))
        pos += size // 128
    return out


def _heads_front(a, nh):
    return a.reshape(a.shape[0], nh, HEAD_DIM).transpose(1, 0, 2)


def _heads_back(a):
    return a.transpose(1, 0, 2).reshape(a.shape[1], a.shape[0] * HEAD_DIM)


def _rope_tables(t, tc):
    tok = jnp.arange(t, dtype=jnp.int32)
    inv_freq = ROPE_THETA ** (-jnp.arange(ROPE_FREQS, dtype=F32) / ROPE_FREQS)
    ang_r = (tok // GRID_W).astype(F32)[:, None] * inv_freq
    ang_c = (tok % GRID_W).astype(F32)[:, None] * inv_freq
    cos64 = jnp.concatenate([jnp.cos(ang_r), jnp.cos(ang_r), jnp.cos(ang_c), jnp.cos(ang_c)], axis=1)
    sin64 = jnp.concatenate([-jnp.sin(ang_r), jnp.sin(ang_r), -jnp.sin(ang_c), jnp.sin(ang_c)], axis=1)
    cos64 = jnp.concatenate([jnp.ones((tc, HEAD_DIM), F32), cos64], axis=0)
    sin64 = jnp.concatenate([jnp.zeros((tc, HEAD_DIM), F32), sin64], axis=0)
    return jnp.tile(cos64, (1, N_Q_HEADS)), jnp.tile(sin64, (1, N_Q_HEADS))


def _head_mean_matrix(width):
    i = jnp.arange(width) // HEAD_DIM
    return (i[:, None] == i[None, :]).astype(F32) / HEAD_DIM


def _head_tile_matrix(width):
    return (jnp.arange(HEAD_DIM)[:, None] == (jnp.arange(width) % HEAD_DIM)[None, :]).astype(F32)


def _heads_t(a, nh):
    return a.T.reshape(nh, HEAD_DIM, a.shape[0])


def _heads_t_back(a):
    return a.reshape(a.shape[0] * HEAD_DIM, a.shape[2]).T


def _attention_fwd(tag, qr, kr, vv, tc, comm):
    qt8, k2, vt2 = _heads_t(qr, N_Q_HEADS), _heads_front(kr, N_KV_HEADS), _heads_t(vv, N_KV_HEADS)
    vt2 = jnp.concatenate([vt2, jnp.ones((N_KV_HEADS, 8, vt2.shape[2]), BF16)], axis=1)
    o_c, lse_c, _ = _attn_fwd(tag + "_attn_ctx", qt8[:, :, :tc], k2[:, :tc], vt2[:, :, :tc])
    o_x, lse_x, comm_out = _attn_fwd(tag + "_attn_lat", qt8[:, :, tc:], k2, vt2, comm)
    ot8 = jnp.concatenate([o_c, o_x], axis=2)
    lse = jnp.concatenate([lse_c, lse_x], axis=2)
    return _heads_t_back(ot8), (qr, kr, vv, ot8, lse), comm_out


def _attention_bwd(tag, saved, datt, tc, comm):
    qr, kr, vv, ot8, lse = saved
    datt = datt.astype(BF16)
    qt8, dot8 = _heads_t(qr, N_Q_HEADS), _heads_t(datt, N_Q_HEADS)
    k2, v2, kt2 = _heads_front(kr, N_KV_HEADS), _heads_front(vv, N_KV_HEADS), _heads_t(kr, N_KV_HEADS)
    zero = jnp.zeros((N_KV_HEADS, HEAD_DIM, tc), F32)
    dq_c, dk_c, dv_c, _ = _attn_bwd(tag + "_attn_b_ctx", qt8[:, :, :tc], k2[:, :tc], v2[:, :tc], kt2[:, :, :tc],
                                    ot8[:, :, :tc], dot8[:, :, :tc], lse[:, :, :tc], zero, zero)
    dq_x, dkt2, dvt2, comm_out = _attn_bwd(tag + "_attn_b_lat", qt8[:, :, tc:], k2, v2, kt2,
                                           ot8[:, :, tc:], dot8[:, :, tc:], lse[:, :, tc:], dk_c, dv_c, comm)
    dqt8 = jnp.concatenate([dq_c, dq_x], axis=2)
    return _heads_t_back(dqt8), _heads_t_back(dkt2), _heads_t_back(dvt2), comm_out


def _row_tiles(n, R):
    return _pick(n, (768, R)), _pick(n, (384, R))


def _layer_fwd(tag, x, w, modv, consts, R, nct, tc, comm, late_weights):
    sh1, sc1, g1, sh2, sc2, g2 = modv
    cosq, sinq, bdq, eq, bdk, ek = consts
    rw, rm = _row_tiles(x.shape[0], R)
    (h1,) = _rowwise(tag + "_ln1", _f_lnmod, rw, tc, [(x, D, 0, D)], [], [(w["norm1_g"], False)], [], [sh1, sc1], [(D, BF16)])
    p = _matmul(tag + "_in", h1, w["w_in"], "nn", F32)
    qk_rows = [(p, Q_W, C_Q // Q_W, Q_W), (p, KV_W, C_K // KV_W, KV_W)]
    qk_consts = [(cosq, Q_W, 0, Q_W), (sinq, Q_W, 0, Q_W), (cosq, KV_W, 0, KV_W), (sinq, KV_W, 0, KV_W)]
    qk_params = [(w["q_norm_g"], False), (w["k_norm_g"], False)]
    qk_cparams = [(bdq, False), (eq, False), (bdk, False), (ek, False)]
    qr, kr = _rowwise(tag + "_qk", _f_qknorm, rw, tc, qk_rows, qk_consts, qk_params, qk_cparams, [],
                      [(Q_W, BF16), (KV_W, BF16)], post=_qk_post)
    vv = p[:, C_VV:C_VV + KV_W].astype(BF16)
    att, att_saved, comm_out = _attention_fwd(tag, qr, kr, vv, tc, comm)
    w.update(late_weights(comm_out))

    o_f, o_b, s_f, s_b = _gla_fwd(tag + "_gla", p, (w["w2p_f"], w["w2p_b"]), (w["b2_f"], w["b2_b"]), R, nct)
    go_rows = [(o_f, GLA_V_W, 0, GLA_DV), (o_b, GLA_V_W, 0, GLA_DV), (p, GLA_V_W, C_R // GLA_V_W, GLA_DV)]
    (gla,) = _rowwise(tag + "_glaout", _f_glaout, rw, tc, go_rows, [], [(w["gla_norm_g"], True)], [], [], [(GLA_V_W, BF16)])

    rg = GMLP_CHUNK
    gm_rows = [(p, GMLP_W, C_U // GMLP_W, GMLP_W), (p, GMLP_W, C_V // GMLP_W, GMLP_W // GMLP_GROUPS)]
    gm_params = [(w["gmlp_norm_g"], True), (w["w_spatial"], True), (w["b_spatial_t"], True)]
    (gm,) = _rowwise(tag + "_gmlp", _f_gmlp, rg, tc, gm_rows, [], gm_params, [], [], [(GMLP_W, BF16)])

    ya = _matmul(tag + "_br_a", gm, w["w_br_a"], "nn", F32)
    yb = _matmul(tag + "_br_b", att, w["w_br_b"], "nn", F32)
    yc = _matmul(tag + "_br_c", gla, w["w_br_c"], "nn", F32)
    mg_rows = [(p, D, C_GA // D, D), (p, D, C_GB // D, D), (p, D, C_GC // D, D), (ya, D, 0, D), (yb, D, 0, D), (yc, D, 0, D)]
    (merged,) = _rowwise(tag + "_merge", _f_merge, rm, tc, mg_rows, [], [], [], [], [(D, BF16)])
    mix = _matmul(tag + "_out", merged, w["w_out"], "nn", F32)
    (x_mid,) = _rowwise(tag + "_res1", _f_resid, rw, tc, [(x, D, 0, D), (mix, D, 0, D)], [], [], [], [g1], [(D, F32)])

    (h2,) = _rowwise(tag + "_ln2", _f_lnmod, rw, tc, [(x_mid, D, 0, D)], [], [(w["norm2_g"], False)], [], [sh2, sc2], [(D, BF16)])
    a2 = _matmul(tag + "_up", h2, w["w_ffn_up"], "nn", F32, o_halves=True)
    gv = _conv_gate(tag + "_conv", a2, w["conv_w_h"], w["conv_b_h"], R, tc)
    ffn = _matmul(tag + "_down", gv, w["w_ffn_down"], "nn", F32)
    (x_next,) = _rowwise(tag + "_res2", _f_resid, rw, tc, [(x_mid, D, 0, D), (ffn, D, 0, D)], [], [], [], [g2], [(D, F32)])
    saved = dict(x=x, h1=h1, p=p, att_saved=att_saved, att=att, s_f=s_f, s_b=s_b, gla=gla, gm=gm,
                 ya=ya, yb=yb, yc=yc, merged=merged, mix=mix, x_mid=x_mid, h2=h2, a2=a2, gv=gv, ffn=ffn,
                 qk=(qk_rows, qk_consts, qk_params, qk_cparams), go_rows=go_rows, gm_info=(gm_rows, gm_params),
                 mg_rows=mg_rows)
    return x_next, saved, comm_out


def _layer_bwd(tag, dx_next, s, w, modv, R, nct, tc, make_comm, make_tail_comm):
    sh1, sc1, g1, sh2, sc2, g2 = modv
    rw, rm = _row_tiles(dx_next.shape[0], R)
    gw = {}
    (dffn,), _, (dg2,) = _rowwise_bwd(tag + "_res2_b", _f_resid, rw, tc, [(s["ffn"], D, 0, D), (s["ffn"], D, 0, D)], [], [], [], [g2],
                                      [(dx_next, D)], [None, BF16])
    dgv = _matmul(tag + "_down_da", dffn, w["w_ffn_down"], "nt", F32)
    gw["w_ffn_down"] = _matmul(tag + "_down_dw", s["gv"], dffn, "tn", F32)
    da2, dcw, dcb = _conv_gate_bwd(tag + "_conv_b", s["a2"], dgv, w["conv_w_h"], w["conv_b_h"], R, tc)
    gw["conv_w_h"], gw["conv_b_h"] = dcw, dcb
    dh2 = _matmul(tag + "_up_da", da2, w["w_ffn_up"], "nt", F32, a_halves=True)
    gw["w_ffn_up"] = _matmul(tag + "_up_dw", s["h2"], da2, "tn", F32, b_halves=True)
    (dx_mid,), (gw["norm2_g"],), (dsh2, dsc2) = _rowwise_bwd(
        tag + "_ln2_b", _f_lnmod, rw, tc, [(s["x_mid"], D, 0, D)], [], [(w["norm2_g"], False)], [], [sh2, sc2],
        [(dh2, D)], [F32], adds=[dx_next])
    (dmix,), _, (dg1,) = _rowwise_bwd(tag + "_res1_b", _f_resid, rw, tc, [(s["mix"], D, 0, D), (s["mix"], D, 0, D)], [], [], [], [g1],
                                      [(dx_mid, D)], [None, BF16])
    dmerged = _matmul(tag + "_out_da", dmix, w["w_out"], "nt", F32)
    gw["w_out"] = _matmul(tag + "_out_dw", s["merged"], dmix, "tn", F32)
    (dga, dgb, dgc, dya, dyb, dyc), _, _ = _rowwise_bwd(tag + "_merge_b", _f_merge, rm, tc, s["mg_rows"], [], [], [], [],
                                                        [(dmerged, D)], [BF16] * 6)
    dgm = _matmul(tag + "_br_a_da", dya, w["w_br_a"], "nt", F32)
    datt = _matmul(tag + "_br_b_da", dyb, w["w_br_b"], "nt", F32)
    dgla = _matmul(tag + "_br_c_da", dyc, w["w_br_c"], "nt", F32)
    gm_rows, gm_params = s["gm_info"]
    gw["w_br_a"] = _matmul(tag + "_br_a_dw", s["gm"], dya, "tn", F32)
    gw["w_br_b"] = _matmul(tag + "_br_b_dw", s["att"], dyb, "tn", F32)
    gw["w_br_c"] = _matmul(tag + "_br_c_dw", s["gla"], dyc, "tn", F32)
    rg = GMLP_CHUNK
    (du, dv_), (gw["gmlp_norm_g"], gw["w_spatial"], gw["b_spatial_t"]), _ = _rowwise_bwd(
        tag + "_gmlp_b", _f_gmlp, rg, tc, gm_rows, [], gm_params, [], [], [(dgm, GMLP_W)], [BF16, BF16])
    (do, dr), (gw["gla_norm_g"],), _ = _rowwise_bwd(tag + "_glaout_b", _f_glaout, rw, tc, s["go_rows"], [],
                                                    [(w["gla_norm_g"], True)], [], [], [(dgla, GLA_V_W)], [F32, None, BF16])
    p = s["p"]
    d_f, d_b, (gw["w2p_f"], gw["b2_f"], gw["w2p_b"], gw["b2_b"]) = _gla_bwd(
        tag + "_gla_b", p, (w["w2p_f"], w["w2p_b"]), (w["b2_f"], w["b2_b"]), (s["s_f"], s["s_b"]), do, R, nct)
    dgq, dgk, dgv_, daf = [(a + b).astype(BF16) for a, b in zip(d_f, d_b)]
    dqr, dkr, dvv, comm_out = _attention_bwd(tag, s["att_saved"], datt, tc, make_comm(gw))
    qk_rows, qk_consts, qk_params, qk_cparams = s["qk"]
    (dq, dk), (gw["q_norm_g"], gw["k_norm_g"]), _ = _rowwise_bwd(
        tag + "_qk_b", _f_qknorm, rw, tc, qk_rows, qk_consts, qk_params, qk_cparams, [],
        [(dqr, Q_W), (dkr, KV_W)], [BF16, BF16], pre=_qk_pre)
    dp = jnp.concatenate([dga, dgb, dgc, du, dv_, dq, dgv_, dr, dgq, dgk, dk, dvv.astype(BF16), daf,
                          jnp.zeros((p.shape[0], PW - C_AF - 128), BF16)], axis=1)
    gw["w_in"] = _matmul(tag + "_in_dw", s["h1"], dp, "tn", F32)
    tail = make_tail_comm(gw)
    if tail is None:
        dh1, tail_out = _matmul(tag + "_in_da", dp, w["w_in"], "nt", F32), []
    else:
        dh1, tail_out = _matmul(tag + "_in_da", dp, w["w_in"], "nt", F32, comm=tail)
    (dx,), (gw["norm1_g"],), (dsh1, dsc1) = _rowwise_bwd(
        tag + "_ln1_b", _f_lnmod, rw, tc, [(s["x"], D, 0, D)], [], [(w["norm1_g"], False)], [], [sh1, sc1],
        [(dh1, D)], [F32], adds=[dx_mid])
    return dx, gw, (dsh1, dsc1, dg1, dsh2, dsc2, dg2), comm_out, tail_out


_SHARDED = (("w_in", 1, True), ("w_br_a", 1, True), ("w_br_b", 1, True), ("w_br_c", 1, True), ("w_out", 0, True),
            ("w_ffn_up", 1, True), ("w_ffn_down", 0, True), ("conv_w", 1, False), ("w_alpha2", 2, False), ("b_alpha", 1, False))
_REPLICATED = ("c_ctx", "b_ada", "norm1_g", "norm2_g", "q_norm_g", "k_norm_g", "gmlp_norm_g", "w_spatial", "b_spatial",
               "gla_norm_g", "conv_b", "final_norm_g")
_WEIGHTS = ("c_ctx", "w_ada", "b_ada", "norm1_g", "norm2_g", "w_in", "q_norm_g", "k_norm_g", "gmlp_norm_g", "w_spatial",
            "b_spatial", "w_alpha2", "b_alpha", "gla_norm_g", "w_br_a", "w_br_b", "w_br_c", "w_out", "w_ffn_up", "conv_w",
            "conv_b", "w_ffn_down", "final_norm_g")


def _decay_weights(w_alpha2_l, b_alpha_l):
    out = []
    for d in range(2):
        w2p = jnp.zeros((128, GLA_QK_W), F32).at[GLA_RANK * d:GLA_RANK * (d + 1)].set(w_alpha2_l[d])
        out += [w2p, b_alpha_l[d][None, :]]
    return out


def _step(inp, wts, moms, vels):
    x, c, ctx, loss_target = inp
    t, tc = x.shape[1], ctx.shape[1]
    n = t + tc
    R = min(256, tc)
    nct = tc // R
    me = _my_index()
    depth = wts["w_in"].shape[0]

    late = [(nm, ax) for nm, ax, half in _SHARDED if half and nm != "w_in"]
    small = [(nm, ax) for nm, ax, half in _SHARDED if not half]
    w_in_shard = lambda l: wts["w_in"][l].astype(BF16)
    c8 = jnp.concatenate([c, jnp.zeros((7, D), F32)], axis=0)
    first = _exchange("gather_first", [w_in_shard(0)] + [wts[nm] for nm, _ in small] + [c8], gather=True)
    c_all = first[-1][:, 0, :]
    small_all = dict(zip([nm for nm, _ in small], first[1:-1]))

    def early_weights(l, w_in_all):
        w = {"w_in": _win_my_cols([w_in_all[s] for s in range(N_DEV)])}
        conv_w, w_alpha2, b_alpha = [jnp.concatenate([small_all[nm][s, l] for s in range(N_DEV)], axis=ax) for nm, ax in small]
        w["conv_w_h"] = conv_w.reshape(3, 2, FFN_H).transpose(1, 0, 2)
        w["conv_b_h"] = wts["conv_b"][l].reshape(2, 1, FFN_H)
        w["w2p_f"], w["b2_f"], w["w2p_b"], w["b2_b"] = _decay_weights(w_alpha2, b_alpha)
        w["norm1_g"] = wts["norm1_g"][l][None, :]
        w["norm2_g"] = wts["norm2_g"][l][None, :]
        w["q_norm_g"] = wts["q_norm_g"][l][None, :]
        w["k_norm_g"] = wts["k_norm_g"][l][None, :]
        w["gmlp_norm_g"] = wts["gmlp_norm_g"][l].reshape(GMLP_GROUPS, 1, GMLP_W // GMLP_GROUPS)
        w["w_spatial"] = wts["w_spatial"][l]
        w["b_spatial_t"] = wts["b_spatial"][l][:, :, None]
        w["gla_norm_g"] = wts["gla_norm_g"][l].reshape(GLA_HEADS, 1, GLA_DV)
        return w

    craw16 =jnp.concatenate([c_all, wts["c_ctx"][None, :], jnp.zeros((7, D), F32)], axis=0)
    acols = wts["w_ada"].shape[2]
    b_cols = lax.dynamic_slice_in_dim(wts["b_ada"], me * acols, acols, axis=1)[:, None, :]
    mod_part = _adaln_fwd("adaln", craw16, wts["w_ada"], b_cols)
    send = jnp.stack([mod_part[:, 8, :][None].repeat(N_DEV, 0), mod_part[:, :8, :].transpose(1, 0, 2)], axis=2)
    send = jnp.concatenate([send.reshape(N_DEV, 2 * depth, acols), jnp.zeros((N_DEV, 8 - 2 * depth, acols), F32)], axis=1)
    (got,) = _exchange("scatter_mod", [send], gather=False)
    mod = got[:, :2 * depth, :].transpose(1, 0, 2).reshape(depth, 2, N_MOD, 1, D)
    modv = [[mod[l, :, k] for k in range(N_MOD)] for l in range(depth)]

    cosq, sinq = _rope_tables(t, tc)
    consts = (cosq, sinq, _head_mean_matrix(Q_W), _head_tile_matrix(Q_W), _head_mean_matrix(KV_W), _head_tile_matrix(KV_W))
    xs = jnp.concatenate([ctx[0], x[0]], axis=0)
    saved, layers = [], []
    w_in_all = first[0]

    def late_weights(got):
        return {nm: _join_shards(g, ax) for (nm, ax), g in zip(late, got)}

    for l in range(depth):
        layers.append(early_weights(l, w_in_all))
        sending = [wts[nm][l].astype(BF16) for nm, _ in late] + ([w_in_shard(l + 1)] if l + 1 < depth else [])
        xs, sv, got = _layer_fwd("l%d" % l, xs, layers[l], modv[l], consts, R, nct, tc, (sending, True), late_weights)
        if l + 1 < depth:
            w_in_all = got[len(late)]
        saved.append(sv)
    loss_blk, dxs, dgf = _final_loss("final", xs, loss_target[0], wts["final_norm_g"][None, :], R, nct)
    loss = lax.psum(loss_blk[0, 0], ("x", "y", "c"))

    grads = [None] * depth
    dmods = [None] * depth
    late_parts = [None] * depth
    w_in_parts = [None] * depth
    w_in_grad_shards = lambda g: _win_shards(g["w_in"]).astype(BF16)

    def small_grad_shards(g):
        full = dict(conv_w=g["conv_w_h"].transpose(1, 0, 2).reshape(3, F2),
                    w_alpha2=jnp.stack([g["w2p_f"][:GLA_RANK], g["w2p_b"][GLA_RANK:2 * GLA_RANK]]),
                    b_alpha=jnp.stack([g["b2_f"][0], g["b2_b"][0]]))
        return [_split_shards(full[nm], ax) for nm, ax in small]

    for l in range(depth - 1, -1, -1):
        def make_comm(gw, l=l):
            sending = [_split_shards(gw[nm], ax).astype(BF16) for nm, ax in late]
            return sending + ([w_in_grad_shards(grads[l + 1])] if l + 1 < depth else []), False

        def make_tail_comm(gw, l=l):
            if l > 0:
                return None
            per_layer = [small_grad_shards(gw if k == 0 else grads[k]) for k in range(depth)]
            return [w_in_grad_shards(gw)] + [jnp.stack([per_layer[k][i] for k in range(depth)], axis=1) for i in range(len(small))], False

        dxs, grads[l], dmods[l], got, tail = _layer_bwd("l%d" % l, dxs, saved[l], layers[l], modv[l], R, nct, tc,
                                                        make_comm, make_tail_comm)
        late_parts[l] = got[:len(late)]
        if l + 1 < depth:
            w_in_parts[l + 1] = got[len(late)]
    w_in_parts[0], small_parts = tail[0], tail[1:]
    grad_x = dxs[tc:][None]

    dmod = jnp.stack([jnp.stack(dmods[l], axis=1) for l in range(depth)])
    dmod = dmod.reshape(depth, 2, N_DEV, acols).transpose(2, 0, 1, 3).reshape(N_DEV, 2 * depth, acols)
    dmod_send = jnp.concatenate([dmod, jnp.zeros((N_DEV, 8 - 2 * depth, acols), F32)], axis=1)
    (dm_got,) = _exchange("scatter_dmod", [dmod_send], gather=False)
    g_w_ada, dc16 = _adaln_bwd("adaln_b", craw16, wts["w_ada"], dm_got[:, :2 * depth].transpose(1, 0, 2))
    db_ada_part = jnp.stack([jnp.stack(dmods[l], axis=1) for l in range(depth)]).reshape(depth, 2, N_MOD * D).sum(axis=1)

    out = {}
    kinds = ("grad", "delta", "new_m", "new_v")
    view2 = lambda a: a.reshape(-1, a.shape[-1])
    sharded_parts = ([jnp.stack(w_in_parts, axis=1)]
                     + [jnp.stack([late_parts[l][k] for l in range(depth)], axis=1) for k in range(len(late))] + small_parts)
    for (nm, _), parts in zip([("w_in", 1)] + late + small, sharded_parts):
        res = _adamw("adamw_" + nm, view2(wts[nm]), view2(moms[nm]), view2(vels[nm]), parts.reshape(N_DEV, -1, parts.shape[-1]))
        for kind, flat in zip(kinds, res):
            out[kind, nm] = flat.reshape(wts[nm].shape)

    rep_g = dict(
        c_ctx=dc16[8], b_ada=db_ada_part, final_norm_g=dgf[0],
        norm1_g=jnp.stack([grads[l]["norm1_g"][0] for l in range(depth)]),
        norm2_g=jnp.stack([grads[l]["norm2_g"][0] for l in range(depth)]),
        q_norm_g=jnp.stack([grads[l]["q_norm_g"][0] for l in range(depth)]),
        k_norm_g=jnp.stack([grads[l]["k_norm_g"][0] for l in range(depth)]),
        gmlp_norm_g=jnp.stack([grads[l]["gmlp_norm_g"].reshape(GMLP_W) for l in range(depth)]),
        w_spatial=jnp.stack([grads[l]["w_spatial"] for l in range(depth)]),
        b_spatial=jnp.stack([grads[l]["b_spatial_t"][:, :, 0] for l in range(depth)]),
        gla_norm_g=jnp.stack([grads[l]["gla_norm_g"].reshape(GLA_V_W) for l in range(depth)]),
        conv_b=jnp.stack([grads[l]["conv_b_h"].reshape(F2) for l in range(depth)]),
    )
    rep_shapes = [wts[nm].shape for nm in _REPLICATED]
    (rg_parts,) = _exchange("gather_rep_grads", [_pack_rows([rep_g[nm] for nm in _REPLICATED], 16)], gather=True)
    rpk = lambda src: _pack_rows([src[nm] for nm in _REPLICATED], 16)
    res = _adamw("adamw_rep", rpk(wts), rpk(moms), rpk(vels), rg_parts)
    for kind, rows in zip(kinds, res):
        for nm, piece in zip(_REPLICATED, _unpack_rows(rows, rep_shapes)):
            out[kind, nm] = piece

    res = _adamw("adamw_ada", view2(wts["w_ada"]), view2(moms["w_ada"]), view2(vels["w_ada"]), view2(g_w_ada)[None])
    for kind, flat in zip(kinds, res):
        out[kind, "w_ada"] = flat.reshape(wts["w_ada"].shape)

    return (loss, grad_x, *[out[kind, nm] for kind in kinds for nm in _WEIGHTS])


def kernel(x, c, ctx, c_ctx, w_ada, b_ada, norm1_g, norm2_g, w_in, q_norm_g, k_norm_g, gmlp_norm_g, w_spatial, b_spatial, w_alpha2, b_alpha, gla_norm_g, w_br_a, w_br_b, w_br_c, w_out, w_ffn_up, conv_w, conv_b, w_ffn_down, final_norm_g, loss_target, m_c_ctx, m_w_ada, m_b_ada, m_norm1_g, m_norm2_g, m_w_in, m_q_norm_g, m_k_norm_g, m_gmlp_norm_g, m_w_spatial, m_b_spatial, m_w_alpha2, m_b_alpha, m_gla_norm_g, m_w_br_a, m_w_br_b, m_w_br_c, m_w_out, m_w_ffn_up, m_conv_w, m_conv_b, m_w_ffn_down, m_final_norm_g, v_c_ctx, v_w_ada, v_b_ada, v_norm1_g, v_norm2_g, v_w_in, v_q_norm_g, v_k_norm_g, v_gmlp_norm_g, v_w_spatial, v_b_spatial, v_w_alpha2, v_b_alpha, v_gla_norm_g, v_w_br_a, v_w_br_b, v_w_br_c, v_w_out, v_w_ffn_up, v_conv_w, v_conv_b, v_w_ffn_down, v_final_norm_g):
    wts = dict(zip(_WEIGHTS, (c_ctx, w_ada, b_ada, norm1_g, norm2_g, w_in, q_norm_g, k_norm_g, gmlp_norm_g, w_spatial, b_spatial,
                              w_alpha2, b_alpha, gla_norm_g, w_br_a, w_br_b, w_br_c, w_out, w_ffn_up, conv_w, conv_b, w_ffn_down,
                              final_norm_g)))
    moms = dict(zip(_WEIGHTS, (m_c_ctx, m_w_ada, m_b_ada, m_norm1_g, m_norm2_g, m_w_in, m_q_norm_g, m_k_norm_g, m_gmlp_norm_g,
                               m_w_spatial, m_b_spatial, m_w_alpha2, m_b_alpha, m_gla_norm_g, m_w_br_a, m_w_br_b, m_w_br_c, m_w_out,
                               m_w_ffn_up, m_conv_w, m_conv_b, m_w_ffn_down, m_final_norm_g)))
    vels = dict(zip(_WEIGHTS, (v_c_ctx, v_w_ada, v_b_ada, v_norm1_g, v_norm2_g, v_w_in, v_q_norm_g, v_k_norm_g, v_gmlp_norm_g,
                               v_w_spatial, v_b_spatial, v_w_alpha2, v_b_alpha, v_gla_norm_g, v_w_br_a, v_w_br_b, v_w_br_c, v_w_out,
                               v_w_ffn_up, v_conv_w, v_conv_b, v_w_ffn_down, v_final_norm_g)))
    return _step((x, c, ctx, loss_target), wts, moms, vels)
```

```python
import functools

import jax
import jax.numpy as jnp
from jax import lax
from jax.experimental import pallas as pl
from jax.experimental.pallas import tpu as pltpu

F32 = jnp.float32
BF16 = jnp.bfloat16
HI = lax.Precision.HIGHEST
MESH_ID = pl.DeviceIdType.MESH

N_DEV = 8
EPS = 1e-6
D = 1024
N_MOD = 6
HEAD_DIM = 64
N_Q_HEADS = 8
N_KV_HEADS = 2
Q_GROUP = 4
Q_W = 512
KV_W = 128
GRID_W = 64
ROPE_THETA = 10000.0
ROPE_FREQS = 16
GMLP_CHUNK = 128
GMLP_GROUPS = 4
GMLP_W = 512
GLA_HEADS = 4
GLA_QK_W = 256
GLA_V_W = 512
GLA_DK = 64
GLA_DV = 128
GLA_RANK = 16
GLA_TAU = 16.0
GLA_CHUNK = 64
FFN_H = 2816
F2 = 2 * FFN_H
IN_SPLITS = (512, 512, 512, 128, 128, 256, 256, 512, 16, 16, 512, 1024, 1024, 1024)
IN_WIDTH = sum(IN_SPLITS)

C_GA, C_GB, C_GC = 0, 1024, 2048
C_U, C_V, C_Q, C_GV, C_R = 3072, 3584, 4096, 4608, 5120
C_GQ, C_GK = 5632, 5888
C_K, C_VV, C_AF = 6144, 6272, 6400
PW = 6656

ADAM_LR = 0.001
ADAM_B1 = 0.9
ADAM_B2 = 0.999
ADAM_EPS = 1e-08
ADAM_WD = 0.01
ADAM_STEP = 10

V7X_VMEM_LIMIT = 56 * 1024 * 1024

_ARB1 = pltpu.CompilerParams(dimension_semantics=("arbitrary",), vmem_limit_bytes=V7X_VMEM_LIMIT)


def _pick(dim, prefs):
    for p in prefs:
        if dim % p == 0:
            return p
    return dim


def _hdot(a, b, dims=(((1,), (0,)), ((), ()))):
    return lax.dot_general(a, b, dims, precision=HI, preferred_element_type=F32)


_NT = (((1,), (1,)), ((), ()))
_TN = (((0,), (0,)), ((), ()))
_NN = (((1,), (0,)), ((), ()))


def _matmul(name, a, b, mode, out_dtype, *, a_halves=False, b_halves=False, o_halves=False, comm=None):
    def dims2(x, halves):
        return (x.shape[1], 2 * x.shape[2]) if halves else x.shape

    ar, ac = dims2(a, a_halves)
    br, bc = dims2(b, b_halves)
    if mode == "nn":
        M, K, N = ar, ac, bc
    elif mode == "nt":
        M, K, N = ar, ac, br
    else:
        M, K, N = ac, ar, bc
    n_unit = N // 2 if (o_halves or (b_halves and mode != "nt")) else N
    k_unit = K // 2 if (a_halves and mode != "tn") else K
    tokens = lambda d: d > 7168
    if tokens(K):
        tm = _pick(M, (512, 1408, 256, 128))
        tk = _pick(k_unit, (2816, 1408, 768, 512, 384, 256, 128))
    else:
        tm = _pick(M, (768, 512, 384, 256, 128) if tokens(M) else (1024, 1408, 512, 256, 128))
        tk = _pick(k_unit, (3328, 2816, 1664, 1408, 1024, 512, 256, 128))
    tn = _pick(n_unit, (1664, 1408, 1024, 512, 256, 128))
    nk = K // tk

    def spec(shape2, halves, blk, imap):
        if not halves:
            return pl.BlockSpec(blk, imap)
        nhalf = (shape2[1] // 2) // blk[1]

        def im(i, j, k):
            r, c = imap(i, j, k)
            return (c // nhalf, r, c % nhalf)
        return pl.BlockSpec((None,) + blk, im)

    if mode == "nn":
        a_spec = spec((ar, ac), a_halves, (tm, tk), lambda i, j, k: (i, k))
        b_spec = spec((br, bc), b_halves, (tk, tn), lambda i, j, k: (k, j))
        dn = _NN
    elif mode == "nt":
        a_spec = spec((ar, ac), a_halves, (tm, tk), lambda i, j, k: (i, k))
        b_spec = spec((br, bc), b_halves, (tn, tk), lambda i, j, k: (j, k))
        dn = _NT
    else:
        a_spec = spec((ar, ac), a_halves, (tk, tm), lambda i, j, k: (k, i))
        b_spec = spec((br, bc), b_halves, (tk, tn), lambda i, j, k: (k, j))
        dn = _TN
    o_spec = spec((M, N), o_halves, (tm, tn), lambda i, j, k: (i, j))
    o_shape = (2, M, N // 2) if o_halves else (M, N)

    def body(a_ref, b_ref, o_ref, acc_ref):
        k = pl.program_id(2)
        part = lax.dot_general(a_ref[...], b_ref[...], dn, preferred_element_type=F32)
        if nk == 1:
            o_ref[...] = part.astype(o_ref.dtype)
        else:
            @pl.when(k == 0)
            def _():
                acc_ref[...] = part

            @pl.when(k > 0)
            def _():
                acc_ref[...] += part

            @pl.when(k == nk - 1)
            def _():
                o_ref[...] = acc_ref[...].astype(o_ref.dtype)

    grid = (M // tm, N // tn, nk)
    body, xspecs, xshapes, xsems = _with_exchange(body, 2, 1, grid, comm)
    res = pl.pallas_call(
        body, name=name, grid=grid,
        in_specs=[a_spec, b_spec] + xspecs, out_specs=[o_spec] + xspecs,
        out_shape=[jax.ShapeDtypeStruct(o_shape, out_dtype)] + xshapes,
        scratch_shapes=[pltpu.VMEM((tm, tn), F32)] + xsems,
        compiler_params=pltpu.CompilerParams(dimension_semantics=("arbitrary", "arbitrary", "arbitrary"),
                                             vmem_limit_bytes=V7X_VMEM_LIMIT),
    )(a, b, *(comm[0] if comm else []))
    return res[0] if comm is None else (res[0], list(res[1:]))


def _full_spec(shape):
    nd = len(shape)
    return pl.BlockSpec(tuple(shape), lambda i, _nd=nd: (0,) * _nd)


def _row_spec(R, W, cb):
    return pl.BlockSpec((R, W), lambda i, _cb=cb: (i, _cb))


def _load_rows(refs, specs):
    vals = []
    for ref, (_, W, _, pw) in zip(refs, specs):
        if pw == W:
            vals.append(ref[...].astype(F32))
        else:
            vals.append([ref[:, k * pw:(k + 1) * pw].astype(F32) for k in range(W // pw)])
    return vals


def _load_params(refs, specs):
    vals = []
    for ref, (arr, split) in zip(refs, specs):
        if split:
            vals.append([ref[k] for k in range(arr.shape[0])])
        else:
            vals.append(ref[...])
    return vals


def _row_mods(pairs, first_row, rows, tc):
    is_latent = ((first_row + lax.broadcasted_iota(jnp.int32, (rows, 1), 0)) >= tc).astype(F32)
    return [mc + is_latent * (mx - mc) for mc, mx in pairs]


def _rowwise(name, f, R, tc, rows, consts, params, cparams, mods, outs, post=None):
    n = rows[0][0].shape[0]
    nr, nc, npar, ncp, nm = len(rows), len(consts), len(params), len(cparams), len(mods)

    def body(*refs):
        pos = 0
        rr = refs[pos:pos + nr]; pos += nr
        cr = refs[pos:pos + nc]; pos += nc
        pr = refs[pos:pos + npar]; pos += npar
        cpr = refs[pos:pos + ncp]; pos += ncp
        mr = refs[pos:pos + nm]; pos += nm
        orefs = refs[pos:]
        res = f(_load_rows(rr, rows), _load_params(pr, params),
                _row_mods([(m[0], m[1]) for m in mr], pl.program_id(0) * R, R, tc),
                _load_rows(cr, consts), _load_params(cpr, cparams))
        if post is not None:
            res = post(res, _load_rows(cr, consts))
        for o_ref, r in zip(orefs, res):
            o_ref[...] = r.astype(o_ref.dtype)

    in_specs = ([_row_spec(R, W, cb) for (_, W, cb, _) in rows + consts]
                + [_full_spec(a.shape) for (a, _) in params + cparams]
                + [_full_spec(m.shape) for m in mods])
    args = [a for (a, _, _, _) in rows + consts] + [a for (a, _) in params + cparams] + list(mods)
    return pl.pallas_call(
        body, name=name, grid=(n // R,), in_specs=in_specs,
        out_specs=[_row_spec(R, w, 0) for (w, _) in outs],
        out_shape=[jax.ShapeDtypeStruct((n, w), dt) for (w, dt) in outs],
        compiler_params=_ARB1,
    )(*args)


def _rowwise_bwd(name, f, R, tc, rows, consts, params, cparams, mods, douts, drow, adds=None, pre=None):
    n = rows[0][0].shape[0]
    adds = adds or [None] * len(rows)
    nr, nc, npar, ncp, nm, nd = len(rows), len(consts), len(params), len(cparams), len(mods), len(douts)
    add_ix = [k for k in range(nr) if adds[k] is not None]
    out_ix = [k for k in range(nr) if drow[k] is not None]

    def body(*refs):
        i = pl.program_id(0)
        pos = 0
        rr = refs[pos:pos + nr]; pos += nr
        cr = refs[pos:pos + nc]; pos += nc
        pr = refs[pos:pos + npar]; pos += npar
        cpr = refs[pos:pos + ncp]; pos += ncp
        mr = refs[pos:pos + nm]; pos += nm
        dr = refs[pos:pos + nd]; pos += nd
        ar = refs[pos:pos + len(add_ix)]; pos += len(add_ix)
        drr = refs[pos:pos + len(out_ix)]; pos += len(out_ix)
        dpr = refs[pos:pos + npar]; pos += npar
        dmr = refs[pos:pos + nm]; pos += nm

        cv = _load_rows(cr, consts)
        cpv = _load_params(cpr, cparams)
        _, vjp = jax.vjp(lambda rv, pv, mv: f(rv, pv, _row_mods(mv, i * R, R, tc), cv, cpv),
                         _load_rows(rr, rows), _load_params(pr, params), [(m[0], m[1]) for m in mr])
        dv = [d[...].astype(F32) for d in dr]
        if pre is not None:
            dv = pre(dv, cv)
        g_rows, g_params, g_mods = vjp(tuple(dv))

        for ref, k in zip(drr, out_ix):
            _, W, _, pw = rows[k]
            g = g_rows[k]
            extra = ar[add_ix.index(k)] if k in add_ix else None
            if pw == W:
                if extra is not None:
                    g = g + extra[...].astype(F32)
                ref[...] = g.astype(ref.dtype)
            else:
                for q in range(W // pw):
                    gq = g[q]
                    if extra is not None:
                        gq = gq + extra[:, q * pw:(q + 1) * pw].astype(F32)
                    ref[:, q * pw:(q + 1) * pw] = gq.astype(ref.dtype)

        @pl.when(i == 0)
        def _():
            for ref in tuple(dpr) + tuple(dmr):
                ref[...] = jnp.zeros_like(ref)

        for ref, (arr, split), g in zip(dpr, params, g_params):
            if split:
                for k in range(arr.shape[0]):
                    ref[k] += g[k]
            else:
                ref[...] += g

        for ref, (g_ctx, g_lat) in zip(dmr, g_mods):
            ref[0] += g_ctx
            ref[1] += g_lat

    in_specs = ([_row_spec(R, W, cb) for (_, W, cb, _) in rows + consts]
                + [_full_spec(a.shape) for (a, _) in params + cparams]
                + [_full_spec(m.shape) for m in mods]
                + [_row_spec(R, W, 0) for (_, W) in douts]
                + [_row_spec(R, rows[k][1], 0) for k in add_ix])
    args = ([a for (a, _, _, _) in rows + consts] + [a for (a, _) in params + cparams] + list(mods)
            + [a for (a, _) in douts] + [adds[k] for k in add_ix])
    out_specs = ([_row_spec(R, rows[k][1], 0) for k in out_ix]
                 + [_full_spec(a.shape) for (a, _) in params]
                 + [_full_spec(m.shape) for m in mods])
    out_shape = ([jax.ShapeDtypeStruct((n, rows[k][1]), drow[k]) for k in out_ix]
                 + [jax.ShapeDtypeStruct(a.shape, F32) for (a, _) in params]
                 + [jax.ShapeDtypeStruct(m.shape, F32) for m in mods])
    res = pl.pallas_call(
        body, name=name, grid=(n // R,), in_specs=in_specs, out_specs=out_specs, out_shape=out_shape,
        compiler_params=_ARB1,
    )(*args)
    no = len(out_ix)
    return list(res[:no]), list(res[no:no + npar]), list(res[no + npar:])


def _rms(x, g):
    return x * lax.rsqrt(jnp.mean(x * x, axis=-1, keepdims=True) + EPS) * g


def _f_lnmod(rv, pv, mv, cv, cpv):
    (x,), (g,), (shift, scale) = rv, pv, mv
    return (_rms(x, g) * (1.0 + scale) + shift,)


def _f_resid(rv, pv, mv, cv, cpv):
    (x, y), (gate,) = rv, mv
    return (x + gate * y,)


def _f_merge(rv, pv, mv, cv, cpv):
    ga, gb, gc, ya, yb, yc = rv
    return (_sigmoid(ga) * ya + _sigmoid(gb) * yb + _sigmoid(gc) * yc,)


def _split3(x):
    hi = x.astype(BF16)
    rest = x - hi.astype(F32)
    mid = rest.astype(BF16)
    return hi, mid, (rest - mid.astype(F32)).astype(BF16)


def _dot3_right(x, m):
    mb = m.astype(BF16)
    return sum(lax.dot_general(piece, mb, _NN, preferred_element_type=F32) for piece in _split3(x))


@jax.custom_vjp
def _sym_dot(x, m):
    return _dot3_right(x, m)


_sym_dot.defvjp(lambda x, m: (_dot3_right(x, m), m), lambda m, g: (_dot3_right(g, m), jnp.zeros_like(m)))


def _f_qknorm(rv, pv, mv, cv, cpv):
    (q, k), (gq, gk), (bdq, eq, bdk, ek) = rv, pv, cpv
    qn = q * lax.rsqrt(_sym_dot(q * q, bdq) + EPS) * _hdot(gq, eq)
    kn = k * lax.rsqrt(_sym_dot(k * k, bdk) + EPS) * _hdot(gk, ek)
    return (qn, kn)


def _rope(x, cos, sin):
    w = x.shape[1]
    lane = lax.broadcasted_iota(jnp.int32, x.shape, 1)
    partner = jnp.where((lane & 31) < 16, pltpu.roll(x, w - 16, 1), pltpu.roll(x, 16, 1))
    return x * cos + partner * sin


_LOG2E = 1.4426950408889634
_LN2 = 0.6931471805599453


def _qk_post(res, cv):
    (qn, kn), (cq, sq, ck, sk) = res, cv
    return (_rope(qn, cq, sq) * (HEAD_DIM ** -0.5 * _LOG2E), _rope(kn, ck, sk))


def _qk_pre(dv, cv):
    (dq, dk), (cq, sq, ck, sk) = dv, cv
    return (_rope(dq * (HEAD_DIM ** -0.5), cq, -sq), _rope(dk * _LN2, ck, -sk))


def _f_gmlp(rv, pv, mv, cv, cpv):
    (u, vs), (ng, ws, bt) = rv, pv
    pieces = []
    for g in range(GMLP_GROUPS):
        vn = _rms(jax.nn.gelu(vs[g]), ng[g])
        pieces.append(_bdot(ws[g], vn, "nn") + bt[g])
    return (jax.nn.gelu(u) * jnp.concatenate(pieces, axis=1),)


def _f_glaout(rv, pv, mv, cv, cpv):
    (ofs, obs, rs), (gn,) = rv, pv
    pieces = [_rms(ofs[h] + obs[h], gn[h]) * (rs[h] * _sigmoid(rs[h])) for h in range(GLA_HEADS)]
    return (jnp.concatenate(pieces, axis=1),)


_CONV_CB = 1408
_CONV_STRIP = 128


def _sigmoid(x):
    return 0.5 * jnp.tanh(0.5 * x) + 0.5


def _halo_keep(i, R, tc, n):
    first, end = i * R, (i + 1) * R
    keep_prev = jnp.where((first == 0) | (first == tc), 0.0, 1.0)
    keep_next = jnp.where((end == tc) | (end == n), 0.0, 1.0)
    return keep_prev, keep_next


def _conv_specs(R, n):
    nb8 = n // 8
    main = pl.BlockSpec((2, R, _CONV_CB), lambda j, i: (0, i, j))
    prev = pl.BlockSpec((2, 8, _CONV_CB), lambda j, i: (0, jnp.maximum(i * (R // 8) - 1, 0), j))
    nxt = pl.BlockSpec((2, 8, _CONV_CB), lambda j, i: (0, jnp.minimum((i + 1) * (R // 8), nb8 - 1), j))
    cw = pl.BlockSpec((2, 3, _CONV_CB), lambda j, i: (0, 0, j))
    cb = pl.BlockSpec((2, 1, _CONV_CB), lambda j, i: (0, 0, j))
    return main, prev, nxt, cw, cb


def _row_before(x):
    return pltpu.roll(x, 1, 0)


def _row_after(x):
    return pltpu.roll(x, x.shape[0] - 1, 0)


def _conv_gate(name, a2, cw, cb, R, tc):
    n = a2.shape[1]
    main, prev, nxt, cws, cbs = _conv_specs(R, n)

    def body(a_ref, p_ref, n_ref, cw_ref, cb_ref, o_ref):
        keep_prev, keep_next = _halo_keep(pl.program_id(1), R, tc, n)

        def strip(c, carry):
            ls = pl.ds(pl.multiple_of(c * _CONV_STRIP, _CONV_STRIP), _CONV_STRIP)
            acts = []
            for h in range(2):
                win = jnp.concatenate([p_ref[h, :, ls] * keep_prev, a_ref[h, :, ls], n_ref[h, :, ls] * keep_next], axis=0)
                acts.append((cb_ref[h, :, ls] + cw_ref[h, 1:2, ls] * win + cw_ref[h, 0:1, ls] * _row_before(win)
                             + cw_ref[h, 2:3, ls] * _row_after(win))[8:8 + R])
            g, v = acts
            o_ref[:, ls] = (g * _sigmoid(g) * v).astype(o_ref.dtype)
            return carry

        lax.fori_loop(0, _CONV_CB // _CONV_STRIP, strip, 0)

    return pl.pallas_call(
        body, name=name, grid=(FFN_H // _CONV_CB, n // R),
        in_specs=[main, prev, nxt, cws, cbs],
        out_specs=pl.BlockSpec((R, _CONV_CB), lambda j, i: (i, j)),
        out_shape=jax.ShapeDtypeStruct((n, FFN_H), BF16),
        compiler_params=pltpu.CompilerParams(dimension_semantics=("arbitrary", "arbitrary"),
                                             vmem_limit_bytes=V7X_VMEM_LIMIT),
    )(a2, a2, a2, cw, cb)


def _conv_gate_bwd(name, a2, dgv, cw, cb, R, tc):
    n = a2.shape[1]
    nb8 = n // 8
    main, prev, nxt, cws, cbs = _conv_specs(R, n)
    d_main = pl.BlockSpec((R, _CONV_CB), lambda j, i: (i, j))
    d_prev = pl.BlockSpec((8, _CONV_CB), lambda j, i: (jnp.maximum(i * (R // 8) - 1, 0), j))
    d_next = pl.BlockSpec((8, _CONV_CB), lambda j, i: (jnp.minimum((i + 1) * (R // 8), nb8 - 1), j))
    mid = slice(8, 8 + R)

    def body(a_ref, p_ref, n_ref, cw_ref, cb_ref, d_ref, dp_ref, dn_ref, da_ref, dcw_ref, dcb_ref):
        i = pl.program_id(1)
        keep_prev, keep_next = _halo_keep(i, R, tc, n)

        @pl.when(i == 0)
        def _():
            dcw_ref[...] = jnp.zeros_like(dcw_ref)
            dcb_ref[...] = jnp.zeros_like(dcb_ref)

        def strip(c, carry):
            ls = pl.ds(pl.multiple_of(c * _CONV_STRIP, _CONV_STRIP), _CONV_STRIP)
            cws_ = [[cw_ref[h, k:k + 1, ls] for k in range(3)] for h in range(2)]
            wins = [jnp.concatenate([p_ref[h, :, ls] * keep_prev, a_ref[h, :, ls], n_ref[h, :, ls] * keep_next], axis=0)
                    for h in range(2)]
            g, v = [cb_ref[h, :, ls] + cws_[h][1] * wins[h] + cws_[h][0] * _row_before(wins[h]) + cws_[h][2] * _row_after(wins[h])
                    for h in range(2)]
            dout = jnp.concatenate([dp_ref[:, ls].astype(F32) * keep_prev, d_ref[:, ls].astype(F32),
                                    dn_ref[:, ls].astype(F32) * keep_next], axis=0)
            sg = _sigmoid(g)
            das = [dout * v * sg * (1.0 + g * (1.0 - sg)), dout * g * sg]
            for h in range(2):
                da_win = das[h]
                da_ref[h, :, ls] = (cws_[h][1] * da_win + cws_[h][0] * _row_after(da_win)
                                    + cws_[h][2] * _row_before(da_win))[mid].astype(da_ref.dtype)
                da = da_win[mid]
                dcw_ref[h, 0:1, ls] += jnp.sum(da * _row_before(wins[h])[mid], axis=0, keepdims=True)
                dcw_ref[h, 1:2, ls] += jnp.sum(da * wins[h][mid], axis=0, keepdims=True)
                dcw_ref[h, 2:3, ls] += jnp.sum(da * _row_after(wins[h])[mid], axis=0, keepdims=True)
                dcb_ref[h, :, ls] += jnp.sum(da, axis=0, keepdims=True)
            return carry

        lax.fori_loop(0, _CONV_CB // _CONV_STRIP, strip, 0)

    return pl.pallas_call(
        body, name=name, grid=(FFN_H // _CONV_CB, n // R),
        in_specs=[main, prev, nxt, cws, cbs, d_main, d_prev, d_next],
        out_specs=[main, cws, cbs],
        out_shape=[jax.ShapeDtypeStruct((2, n, FFN_H), BF16), jax.ShapeDtypeStruct((2, 3, FFN_H), F32),
                   jax.ShapeDtypeStruct((2, 1, FFN_H), F32)],
        compiler_params=pltpu.CompilerParams(dimension_semantics=("arbitrary", "arbitrary"),
                                             vmem_limit_bytes=V7X_VMEM_LIMIT),
    )(a2, a2, a2, cw, cb, dgv, dgv, dgv)


_ARB2 = pltpu.CompilerParams(dimension_semantics=("arbitrary", "arbitrary"), vmem_limit_bytes=V7X_VMEM_LIMIT)


def _with_exchange(body, n_in, n_out, grid, comm):
    if comm is None:
        return body, [], [], []
    xs, gather = comm
    na = len(xs)
    specs, shapes, sems = _exchange_io(xs, gather)

    def wrapped(*refs):
        ins, x_refs = refs[:n_in], refs[n_in:n_in + na]
        outs, o_refs = refs[n_in + na:n_in + na + n_out], refs[n_in + na + n_out:n_in + 2 * na + n_out]
        scratch, sem_refs = refs[n_in + 2 * na + n_out:-3], refs[-3:]
        start, wait = _exchange_plan(x_refs, o_refs, *sem_refs, gather)
        ids = [pl.program_id(d) for d in range(len(grid))]
        first, last = ids[0] == 0, ids[0] == grid[0] - 1
        for d in range(1, len(grid)):
            first, last = first & (ids[d] == 0), last & (ids[d] == grid[d] - 1)

        @pl.when(first)
        def _():
            start()

        body(*ins, *outs, *scratch)

        @pl.when(last)
        def _():
            wait()

    return wrapped, specs, shapes, sems


def _attn_fwd(name, qt8, k2, vt2, comm=None):
    nq, nk = qt8.shape[2], k2.shape[1]
    tq = _pick(nq, (256, 128))
    tk = _pick(nk, (1408, 768, 256, 128))

    va = vt2.shape[1]

    def body(qt_ref, k_ref, vt_ref, ot_ref, lse_ref, m_ref, acc_ref):
        m_ref[...] = jnp.full((Q_GROUP, 1, tq), -1e30, F32)
        acc_ref[...] = jnp.zeros((Q_GROUP, va, tq), F32)

        def step(j, carry):
            sl = pl.ds(pl.multiple_of(j * tk, tk), tk)
            kj = k_ref[sl, :]
            vtj = vt_ref[:, sl]
            sts = [lax.dot_general(kj, qt_ref[h], _NN, preferred_element_type=F32) for h in range(Q_GROUP)]
            m_old = [m_ref[h] for h in range(Q_GROUP)]
            acc_old = [acc_ref[h] for h in range(Q_GROUP)]
            m_new = [jnp.maximum(m_old[h], jnp.max(sts[h], axis=0, keepdims=True)) for h in range(Q_GROUP)]
            pts = [jnp.exp2(sts[h] - m_new[h]).astype(BF16) for h in range(Q_GROUP)]
            pvs = [lax.dot_general(vtj, pts[h], _NN, preferred_element_type=F32) for h in range(Q_GROUP)]
            for h in range(Q_GROUP):
                acc_ref[h] = jnp.exp2(m_old[h] - m_new[h]) * acc_old[h] + pvs[h]
                m_ref[h] = m_new[h]
            return carry

        lax.fori_loop(0, nk // tk, step, 0, unroll=3)
        for h in range(Q_GROUP):
            l = acc_ref[h, HEAD_DIM:HEAD_DIM + 1, :]
            ot_ref[h] = (acc_ref[h, 0:HEAD_DIM, :] / l).astype(ot_ref.dtype)
            lse_ref[h] = m_ref[h] + jnp.log2(l)

    qspec = pl.BlockSpec((Q_GROUP, HEAD_DIM, tq), lambda g, i: (g, 0, i))
    lspec = pl.BlockSpec((Q_GROUP, 1, tq), lambda g, i: (g, 0, i))
    grid = (N_KV_HEADS, nq // tq)
    body, xspecs, xshapes, xsems = _with_exchange(body, 3, 2, grid, comm)
    res = pl.pallas_call(
        body, name=name, grid=grid,
        in_specs=[qspec, pl.BlockSpec((None, nk, HEAD_DIM), lambda g, i: (g, 0, 0)),
                  pl.BlockSpec((None, va, nk), lambda g, i: (g, 0, 0))] + xspecs,
        out_specs=[qspec, lspec] + xspecs,
        out_shape=[jax.ShapeDtypeStruct((N_Q_HEADS, HEAD_DIM, nq), BF16), jax.ShapeDtypeStruct((N_Q_HEADS, 1, nq), F32)] + xshapes,
        scratch_shapes=[pltpu.VMEM((Q_GROUP, 1, tq), F32), pltpu.VMEM((Q_GROUP, va, tq), F32)] + xsems,
        compiler_params=_ARB2,
    )(qt8, k2, vt2, *(comm[0] if comm else []))
    return res[0], res[1], list(res[2:])


def _attn_bwd(name, qt8, k2, v2, kt2, ot8, dot8, lse, dkt0, dvt0, comm=None):
    nq, nk = qt8.shape[2], k2.shape[1]
    n0 = dkt0.shape[2]
    tq = _pick(nq, (256, 128))
    tk = _pick(nk, (1408, 768, 256, 128))
    ts = tk
    heads = range(Q_GROUP)

    def body(qt_ref, k_ref, v_ref, kt_ref, ot_ref, dot_ref, lse_ref, dk0_ref, dv0_ref, dqt_ref, dkt_ref, dvt_ref, dl_ref, dq_acc):
        @pl.when(pl.program_id(1) == 0)
        def _():
            dkt_ref[...] = jnp.zeros_like(dkt_ref)
            dvt_ref[...] = jnp.zeros_like(dvt_ref)
            dkt_ref[:, 0:n0] = dk0_ref[...]
            dvt_ref[:, 0:n0] = dv0_ref[...]

        for h in heads:
            dl_ref[h] = jnp.sum(dot_ref[h].astype(F32) * ot_ref[h].astype(F32), axis=0, keepdims=True)
        dq_acc[...] = jnp.zeros((Q_GROUP, HEAD_DIM, tq), F32)

        items = [(s, h) for s in range(tk // ts) for h in heads]

        def step(j, carry):
            def keys(s):
                return pl.ds(pl.multiple_of(j * tk + s * ts, ts), ts)

            def scores(item):
                s, h = item
                return (lax.dot_general(k_ref[keys(s), :], qt_ref[h], _NN, preferred_element_type=F32),
                        lax.dot_general(v_ref[keys(s), :], dot_ref[h], _NN, preferred_element_type=F32))

            nxt = scores(items[0])
            for n, (s, h) in enumerate(items):
                st, dpt = nxt
                if n + 1 < len(items):
                    nxt = scores(items[n + 1])
                pt = jnp.exp2(st - lse_ref[h])
                dst = (pt * (dpt - dl_ref[h])).astype(BF16)
                dq_acc[h] += lax.dot_general(kt_ref[:, keys(s)], dst, _NN, preferred_element_type=F32)
                dv_h = lax.dot_general(dot_ref[h], pt.astype(BF16), _NT, preferred_element_type=F32)
                dk_h = lax.dot_general(qt_ref[h], dst, _NT, preferred_element_type=F32)
                dvt_s, dkt_s = (dv_h, dk_h) if h == 0 else (dvt_s + dv_h, dkt_s + dk_h)
                if h == Q_GROUP - 1:
                    dvt_ref[:, keys(s)] += dvt_s
                    dkt_ref[:, keys(s)] += dkt_s
            return carry

        lax.fori_loop(0, nk // tk, step, 0)
        dqt_ref[...] = dq_acc[...]

    tspec = pl.BlockSpec((Q_GROUP, HEAD_DIM, tq), lambda g, i: (g, 0, i))
    lspec = pl.BlockSpec((Q_GROUP, 1, tq), lambda g, i: (g, 0, i))
    kspec = pl.BlockSpec((None, nk, HEAD_DIM), lambda g, i: (g, 0, 0))
    ktspec = pl.BlockSpec((None, HEAD_DIM, nk), lambda g, i: (g, 0, 0))
    k0spec = pl.BlockSpec((None, HEAD_DIM, n0), lambda g, i: (g, 0, 0))
    grid = (N_KV_HEADS, nq // tq)
    body, xspecs, xshapes, xsems = _with_exchange(body, 9, 3, grid, comm)
    res = pl.pallas_call(
        body, name=name, grid=grid,
        in_specs=[tspec, kspec, kspec, ktspec, tspec, tspec, lspec, k0spec, k0spec] + xspecs,
        out_specs=[tspec, ktspec, ktspec] + xspecs,
        out_shape=[jax.ShapeDtypeStruct((N_Q_HEADS, HEAD_DIM, nq), F32), jax.ShapeDtypeStruct((N_KV_HEADS, HEAD_DIM, nk), F32),
                   jax.ShapeDtypeStruct((N_KV_HEADS, HEAD_DIM, nk), F32)] + xshapes,
        scratch_shapes=[pltpu.VMEM((Q_GROUP, 1, tq), F32), pltpu.VMEM((Q_GROUP, HEAD_DIM, tq), F32)] + xsems,
        compiler_params=_ARB2,
    )(qt8, k2, v2, kt2, ot8, dot8, lse, dkt0, dvt0, *(comm[0] if comm else []))
    return res[0], res[1], res[2], list(res[3:])


def _log_sigmoid(z):
    return jnp.minimum(z, 0.0) - jnp.log(1.0 + jnp.exp(-jnp.abs(z)))


_BDOT_DIMS = {"nn": _NN, "nt": _NT, "tn": _TN}
_BDOT_BWD = {"nn": (("nt", "gb"), ("tn", "ag")), "nt": (("nn", "gb"), ("tn", "ga")), "tn": (("nt", "bg"), ("nn", "ag"))}


def _bdot_raw(a, b, mode):
    return lax.dot_general(a.astype(BF16), b.astype(BF16), _BDOT_DIMS[mode], preferred_element_type=F32)


@functools.partial(jax.custom_vjp, nondiff_argnums=(2,))
def _bdot(a, b, mode):
    return _bdot_raw(a, b, mode)


def _bdot_fwd(a, b, mode):
    return _bdot_raw(a, b, mode), (a.astype(BF16), b.astype(BF16))


def _bdot_bwd(mode, res, g):
    ops = {"a": res[0], "b": res[1], "g": g}
    (ma, oa), (mb, ob) = _BDOT_BWD[mode]
    return _bdot_raw(ops[oa[0]], ops[oa[1]], ma), _bdot_raw(ops[ob[0]], ops[ob[1]], mb)


_bdot.defvjp(_bdot_fwd, _bdot_bwd)


def _tile_tri(rev, rows):
    r_i = lax.broadcasted_iota(jnp.int32, (rows, rows), 0)
    c_i = lax.broadcasted_iota(jnp.int32, (rows, rows), 1)
    same = (r_i // GLA_CHUNK) == (c_i // GLA_CHUNK)
    return same & ((c_i >= r_i) if rev else (c_i <= r_i))


def _tri_dot(rev, x):
    tri = _tile_tri(rev, x.shape[0]).astype(BF16)
    return sum(lax.dot_general(tri, piece, _NN, preferred_element_type=F32) for piece in _split3(x))


@functools.partial(jax.custom_vjp, nondiff_argnums=(1,))
def _chunk_cumsum(x, rev):
    return _tri_dot(rev, x)


_chunk_cumsum.defvjp(lambda x, rev: (_tri_dot(rev, x), None), lambda rev, _, g: (_tri_dot(not rev, g),))


_GLA_REV = (False, True)


def _gla_tile_pair(qs, ks, vss, as_, w2s, b2s, states):
    both = range(2)
    rows = qs[0].shape[0]
    nch = rows // GLA_CHUNK
    tris = [_tile_tri(_GLA_REV[d], rows) for d in both]
    chunk_of_row = lax.broadcasted_iota(jnp.int32, (rows, 1), 0) // GLA_CHUNK
    in_chunk = [(chunk_of_row == c).astype(F32) for c in range(nch)]
    las = [_log_sigmoid(_bdot(as_[d], w2s[d], "nn") + b2s[d]) * (1.0 / GLA_TAU) for d in both]
    cums = [_chunk_cumsum(las[d], _GLA_REV[d]) for d in both]
    tots = [[jnp.sum(las[d] * in_chunk[c], axis=0, keepdims=True) for c in range(nch)] for d in both]
    tot_rows = [sum(in_chunk[c] * tots[d][c] for c in range(nch)) for d in both]
    q_in = [qs[d] * (GLA_DK ** -0.5) * jnp.exp(cums[d]) for d in both]
    k_in = [ks[d] * jnp.exp(-cums[d]) for d in both]
    k_st = [ks[d] * jnp.exp(tot_rows[d] - cums[d]) for d in both]
    lane = lax.broadcasted_iota(jnp.int32, (1, GLA_QK_W), 1)
    outs = [[], []]
    for h in range(GLA_HEADS):
        head = ((lane >= GLA_DK * h) & (lane < GLA_DK * (h + 1))).astype(F32)
        atts = [jnp.where(tris[d], _bdot(q_in[d] * head, k_in[d], "nt"), 0.0) for d in both]
        for d in both:
            outs[d].append(_bdot(atts[d], vss[d][h], "nn"))
    os_ = [jnp.concatenate(outs[d], axis=1) for d in both]
    hr = lax.broadcasted_iota(jnp.int32, (GLA_V_W, GLA_QK_W), 0) // GLA_DV
    hc = lax.broadcasted_iota(jnp.int32, (GLA_V_W, GLA_QK_W), 1) // GLA_DK
    same_head = (hr == hc).astype(F32)
    v_all = [jnp.concatenate(vss[d], axis=1) for d in both]
    states = list(states)
    for step in range(nch):
        chunk = (step, nch - 1 - step)
        us = [_bdot(v_all[d], k_st[d] * in_chunk[chunk[d]], "tn") * same_head for d in both]
        for d in both:
            os_[d] = os_[d] + _bdot(q_in[d] * in_chunk[chunk[d]], states[d], "nt")
            states[d] = jnp.exp(tots[d][chunk[d]]) * states[d] + us[d]
    return os_, states


def _gla_tile_of(step, rev, nct, nt):
    if not rev:
        return step
    return jnp.where(step < nct, nct - 1 - step, nt - 1 - (step - nct))


def _gla_row_specs(R, tile):
    return [pl.BlockSpec((R, GLA_QK_W), lambda s: (tile(s), C_GQ // GLA_QK_W)),
            pl.BlockSpec((R, GLA_QK_W), lambda s: (tile(s), C_GK // GLA_QK_W)),
            pl.BlockSpec((R, GLA_V_W), lambda s: (tile(s), C_GV // GLA_V_W)),
            pl.BlockSpec((R, 128), lambda s: (tile(s), C_AF // 128))]


_GLA_WIDTHS = (GLA_QK_W, GLA_QK_W, GLA_V_W, 128)


def _gla_load(refs):
    q_ref, k_ref, v_ref, a_ref = refs
    return q_ref[...], k_ref[...], [v_ref[:, GLA_DV * h:GLA_DV * (h + 1)] for h in range(GLA_HEADS)], a_ref[...]


def _gla_fwd(name, p, w2s, b2s, R, nct):
    n = p.shape[0]
    nt = n // R
    tiles = [lambda s, d=d: _gla_tile_of(s, _GLA_REV[d], nct, nt) for d in range(2)]

    def body(*refs):
        rows, (w2f, b2f, w2b, b2b), (of, ob, sf, sb), states = refs[:8], refs[8:12], refs[12:16], refs[16:]

        @pl.when(pl.program_id(0) == 0)
        def _():
            for st in states:
                st[...] = jnp.zeros_like(st)

        ins = [_gla_load(rows[:4]), _gla_load(rows[4:])]
        s_in = [states[0][...], states[1][...]]
        sf[...], sb[...] = s_in
        os_, s_out = _gla_tile_pair([i[0] for i in ins], [i[1] for i in ins], [i[2] for i in ins], [i[3] for i in ins],
                                    [w2f[...], w2b[...]], [b2f[...], b2b[...]], s_in)
        of[...], ob[...] = os_
        states[0][...], states[1][...] = s_out

    in_specs = (_gla_row_specs(R, tiles[0]) + _gla_row_specs(R, tiles[1])
                + [_full_spec(w2s[0].shape), _full_spec(b2s[0].shape), _full_spec(w2s[1].shape), _full_spec(b2s[1].shape)])
    o_specs = [pl.BlockSpec((R, GLA_V_W), lambda s, t=t: (t(s), 0)) for t in tiles]
    s_specs = [pl.BlockSpec((None, GLA_V_W, GLA_QK_W), lambda s, t=t: (t(s), 0, 0)) for t in tiles]
    return pl.pallas_call(
        body, name=name, grid=(nt,), in_specs=in_specs, out_specs=o_specs + s_specs,
        out_shape=[jax.ShapeDtypeStruct((n, GLA_V_W), F32)] * 2 + [jax.ShapeDtypeStruct((nt, GLA_V_W, GLA_QK_W), F32)] * 2,
        scratch_shapes=[pltpu.VMEM((GLA_V_W, GLA_QK_W), F32)] * 2,
        compiler_params=_ARB1,
    )(*([p] * 8), w2s[0], b2s[0], w2s[1], b2s[1])


def _gla_bwd(name, p, w2s, b2s, ssaves, do, R, nct):
    n = p.shape[0]
    nt = n // R
    tiles = [lambda s, d=d: _gla_tile_of(nt - 1 - s, _GLA_REV[d], nct, nt) for d in range(2)]

    def body(*refs):
        rows, (w2f, b2f, w2b, b2b), ss, dos = refs[:8], refs[8:12], refs[12:14], refs[14:16]
        d_rows, (dw2f, db2f, dw2b, db2b), dstates = refs[16:24], refs[24:28], refs[28:]

        @pl.when(pl.program_id(0) == 0)
        def _():
            for ref in (dw2f, db2f, dw2b, db2b) + tuple(dstates):
                ref[...] = jnp.zeros_like(ref)

        ins = [_gla_load(rows[:4]), _gla_load(rows[4:])]
        _, vjp = jax.vjp(_gla_tile_pair, [i[0] for i in ins], [i[1] for i in ins], [i[2] for i in ins], [i[3] for i in ins],
                         [w2f[...], w2b[...]], [b2f[...], b2b[...]], [ss[0][...], ss[1][...]])
        dqs, dks, dvss, das, dw2, db2, ds = vjp(([dos[0][...], dos[1][...]], [dstates[0][...], dstates[1][...]]))
        for d in range(2):
            grads = [dqs[d], dks[d], jnp.concatenate(dvss[d], axis=1), das[d]]
            for ref, g in zip(d_rows[4 * d:4 * d + 4], grads):
                ref[...] = g
            dstates[d][...] = ds[d]
        dw2f[...] += dw2[0]
        db2f[...] += db2[0]
        dw2b[...] += dw2[1]
        db2b[...] += db2[1]

    par_specs = [_full_spec(w2s[0].shape), _full_spec(b2s[0].shape), _full_spec(w2s[1].shape), _full_spec(b2s[1].shape)]
    d_specs = [pl.BlockSpec((R, w), lambda s, t=t: (t(s), 0)) for t in tiles for w in _GLA_WIDTHS]
    in_specs = (_gla_row_specs(R, tiles[0]) + _gla_row_specs(R, tiles[1]) + par_specs
                + [pl.BlockSpec((None, GLA_V_W, GLA_QK_W), lambda s, t=t: (t(s), 0, 0)) for t in tiles]
                + [pl.BlockSpec((R, GLA_V_W), lambda s, t=t: (t(s), 0)) for t in tiles])
    res = pl.pallas_call(
        body, name=name, grid=(nt,), in_specs=in_specs, out_specs=d_specs + par_specs,
        out_shape=[jax.ShapeDtypeStruct((n, w), F32) for _ in range(2) for w in _GLA_WIDTHS]
        + [jax.ShapeDtypeStruct(a.shape, F32) for a in (w2s[0], b2s[0], w2s[1], b2s[1])],
        scratch_shapes=[pltpu.VMEM((GLA_V_W, GLA_QK_W), F32)] * 2,
        compiler_params=_ARB1,
    )(*([p] * 8), w2s[0], b2s[0], w2s[1], b2s[1], ssaves[0], ssaves[1], do, do)
    return res[:4], res[4:8], res[8:]


def _final_loss(name, x, target, gf, R, nct):
    n = x.shape[0]

    def body(x_ref, t_ref, g_ref, loss_ref, dx_ref, dg_ref):
        i = pl.program_id(0)

        @pl.when(i == 0)
        def _():
            loss_ref[...] = jnp.zeros_like(loss_ref)
            dg_ref[...] = jnp.zeros_like(dg_ref)

        @pl.when(i < nct)
        def _():
            dx_ref[...] = jnp.zeros_like(dx_ref)

        @pl.when(i >= nct)
        def _():
            y, vjp = jax.vjp(_rms, x_ref[...], g_ref[...])
            err = y - t_ref[...]
            loss_ref[...] += jnp.sum(0.5 * jnp.mean(err * err, axis=-1, keepdims=True))
            dx, dg = vjp(err * (1.0 / D))
            dx_ref[...] = dx
            dg_ref[...] += dg

    return pl.pallas_call(
        body, name=name, grid=(n // R,),
        in_specs=[_row_spec(R, D, 0), pl.BlockSpec((R, D), lambda i: (jnp.maximum(i - nct, 0), 0)), _full_spec((1, D))],
        out_specs=[_full_spec((8, 128)), _row_spec(R, D, 0), _full_spec((1, D))],
        out_shape=[jax.ShapeDtypeStruct((8, 128), F32), jax.ShapeDtypeStruct((n, D), F32), jax.ShapeDtypeStruct((1, D), F32)],
        compiler_params=_ARB1,
    )(x, target, gf)


def _adamw(name, w, m, v, gparts):
    rows, cols = w.shape
    nparts = gparts.shape[0]
    tr = rows
    for cand in range(min(rows, 256), 15, -16):
        if rows % cand == 0:
            tr = cand
            break

    def body(w_ref, m_ref, v_ref, g_ref, go_ref, d_ref, mo_ref, vo_ref):
        g = g_ref[0].astype(F32)
        for k in range(1, nparts):
            g = g + g_ref[k].astype(F32)
        m_new = ADAM_B1 * m_ref[...] + (1.0 - ADAM_B1) * g
        v_new = ADAM_B2 * v_ref[...] + (1.0 - ADAM_B2) * (g * g)
        m_hat = m_new / (1.0 - ADAM_B1 ** ADAM_STEP)
        v_hat = v_new / (1.0 - ADAM_B2 ** ADAM_STEP)
        go_ref[...] = g
        d_ref[...] = -ADAM_LR * (m_hat / (jnp.sqrt(v_hat) + ADAM_EPS) + ADAM_WD * w_ref[...])
        mo_ref[...] = m_new
        vo_ref[...] = v_new

    spec = pl.BlockSpec((tr, cols), lambda i: (i, 0))
    return pl.pallas_call(
        body, name=name, grid=(rows // tr,),
        in_specs=[spec, spec, spec, pl.BlockSpec((nparts, tr, cols), lambda i: (0, i, 0))],
        out_specs=[spec] * 4, out_shape=[jax.ShapeDtypeStruct((rows, cols), F32)] * 4,
        compiler_params=_ARB1,
    )(w, m, v, gparts)


def _my_index():
    return 4 * lax.axis_index("x") + 2 * lax.axis_index("y") + lax.axis_index("c")


def _xor_peer(k):
    flip = lambda a, bit: (1 - a) if bit else a
    pos = (flip(lax.axis_index("x"), (k >> 2) & 1), flip(lax.axis_index("y"), (k >> 1) & 1), flip(lax.axis_index("c"), k & 1))
    return pos, 4 * pos[0] + 2 * pos[1] + pos[2]


def _exchange_plan(x_refs, o_refs, send_sems, recv_sems, local_sems, gather):
    npeer = N_DEV - 1
    me = _my_index()
    locals_, sends, recvs = [], [], []
    for a, (x_ref, o_ref) in enumerate(zip(x_refs, o_refs)):
        mine = x_ref if gather else x_ref.at[me]
        locals_.append(pltpu.make_async_copy(mine, o_ref.at[me], local_sems.at[a]))
        for k in range(1, N_DEV):
            pos, lin = _xor_peer(k)
            src = x_ref if gather else x_ref.at[lin]
            sem = a * npeer + k - 1
            sends.append(pltpu.make_async_remote_copy(src_ref=src, dst_ref=o_ref.at[me], send_sem=send_sems.at[sem],
                                                      recv_sem=recv_sems.at[sem], device_id=pos, device_id_type=MESH_ID))
            recvs.append(pltpu.make_async_remote_copy(src_ref=src, dst_ref=o_ref.at[lin], send_sem=send_sems.at[sem],
                                                      recv_sem=recv_sems.at[sem], device_id=pos, device_id_type=MESH_ID))

    def start():
        for cp in locals_ + sends:
            cp.start()

    def wait():
        for cp in recvs:
            cp.wait_recv()
        for cp in sends:
            cp.wait_send()
        for cp in locals_:
            cp.wait()

    return start, wait


def _exchange_io(xs, gather):
    na = len(xs)
    hbm = pl.BlockSpec(memory_space=pltpu.HBM)
    shapes = [jax.ShapeDtypeStruct((N_DEV,) + tuple(x.shape if gather else x.shape[1:]), x.dtype) for x in xs]
    sems = [pltpu.SemaphoreType.DMA((na * (N_DEV - 1),)), pltpu.SemaphoreType.DMA((na * (N_DEV - 1),)),
            pltpu.SemaphoreType.DMA((na,))]
    return [hbm] * na, shapes, sems


def _exchange(name, xs, *, gather):
    na = len(xs)
    specs, shapes, sems = _exchange_io(xs, gather)

    def body(*refs):
        start, wait = _exchange_plan(refs[:na], refs[na:2 * na], *refs[2 * na:], gather)
        start()
        wait()

    return list(pl.pallas_call(body, name=name, in_specs=specs, out_specs=specs, out_shape=shapes, scratch_shapes=sems)(*xs))


def _adaln_fwd(name, craw16, w_ada, b_cols):
    def body(c_ref, w_ref, b_ref, o_ref):
        cs = jax.nn.silu(c_ref[...]).astype(BF16)
        for l in range(2):
            o_ref[l] = lax.dot_general(cs, w_ref[l].astype(BF16), _NN, preferred_element_type=F32) + b_ref[l]

    return pl.pallas_call(
        body, name=name, out_shape=jax.ShapeDtypeStruct((2, 16, w_ada.shape[2]), F32),
        compiler_params=pltpu.CompilerParams(vmem_limit_bytes=V7X_VMEM_LIMIT),
    )(craw16, w_ada, b_cols)


def _adaln_bwd(name, craw16, w_ada, dm):
    def body(c_ref, w_ref, dm_ref, gw_ref, dc_ref):
        c = c_ref[...]
        sg = jax.nn.sigmoid(c)
        cs = c * sg
        row = lax.broadcasted_iota(jnp.int32, (8, 1), 0)
        dc = jnp.zeros((16, D), F32)
        for l in range(2):
            dmx = dm_ref[2 * l + 1]
            dmc = jnp.where(row == 0, jnp.sum(dm_ref[2 * l], axis=0, keepdims=True), 0.0)
            gw_ref[l] = _hdot(cs[0:8], dmx, _TN) + _hdot(cs[8:16], dmc, _TN)
            dc = dc + _hdot(jnp.concatenate([dmx, dmc], axis=0), w_ref[l], _NT)
        dc_ref[...] = dc * sg * (1.0 + c * (1.0 - sg))

    return pl.pallas_call(
        body, name=name,
        out_shape=[jax.ShapeDtypeStruct(w_ada.shape, F32), jax.ShapeDtypeStruct((16, D), F32)],
        compiler_params=pltpu.CompilerParams(vmem_limit_bytes=V7X_VMEM_LIMIT),
    )(craw16, w_ada, dm)


_IN_OFFS = [sum(IN_SPLITS[:k]) for k in range(len(IN_SPLITS) + 1)]
_MY_ORDER = (11, 12, 13, 0, 1, 2, 7, 10, 5, 6, 3, 4, 8, 9)


_IN_SHARD = IN_WIDTH // N_DEV


def _win_my_cols(pieces):
    parts = []
    for k in _MY_ORDER:
        a, b = _IN_OFFS[k], _IN_OFFS[k + 1]
        for s in range(a // _IN_SHARD, (b - 1) // _IN_SHARD + 1):
            lo, hi = max(a, s * _IN_SHARD), min(b, (s + 1) * _IN_SHARD)
            parts.append(pieces[s][:, lo - s * _IN_SHARD:hi - s * _IN_SHARD])
    parts.append(jnp.zeros((pieces[0].shape[0], PW - IN_WIDTH), pieces[0].dtype))
    return jnp.concatenate(parts, axis=1)


def _win_shards(wp):
    my_offs, pos = {}, 0
    for k in _MY_ORDER:
        my_offs[k] = pos
        pos += IN_SPLITS[k]
    shards = []
    for s in range(N_DEV):
        parts = []
        for k in range(len(IN_SPLITS)):
            lo, hi = max(_IN_OFFS[k], s * _IN_SHARD), min(_IN_OFFS[k + 1], (s + 1) * _IN_SHARD)
            if lo < hi:
                parts.append(wp[:, my_offs[k] + lo - _IN_OFFS[k]:my_offs[k] + hi - _IN_OFFS[k]])
        shards.append(jnp.concatenate(parts, axis=1))
    return jnp.stack(shards)


def _tile_friendly(shape, axis):
    width = shape[axis] // N_DEV if axis == len(shape) - 1 else 128
    return len(shape) >= 2 and width % 128 == 0


def _split_shards(full, axis):
    s = full.shape
    if _tile_friendly(s, axis):
        return jnp.moveaxis(full.reshape(s[:axis] + (N_DEV, s[axis] // N_DEV) + s[axis + 1:]), axis, 0)
    c = s[axis] // N_DEV
    return jnp.stack([lax.slice_in_dim(full, k * c, (k + 1) * c, axis=axis) for k in range(N_DEV)])


def _join_shards(g, axis):
    s = g.shape[1:]
    full_shape = s[:axis] + (N_DEV * s[axis],) + s[axis + 1:]
    if _tile_friendly(full_shape, axis):
        return jnp.moveaxis(g, 0, axis).reshape(full_shape)
    return jnp.concatenate([g[k] for k in range(N_DEV)], axis=axis)


def _pack_rows(pieces, row_mult):
    rows = jnp.concatenate([p.reshape(-1, 128) for p in pieces], axis=0)
    padn = (-rows.shape[0]) % row_mult
    if padn:
        rows = jnp.concatenate([rows, jnp.zeros((padn, 128), rows.dtype)], axis=0)
    return rows


def _unpack_rows(rows, shapes):
    out, pos = [], 0
    for s in shapes:
        size = 1
        for d in s:
            size *= d
        out.append(rows[pos:pos + size // 128].reshape(tuple(s)))
        pos += size // 128
    return out


def _heads_front(a, nh):
    return a.reshape(a.shape[0], nh, HEAD_DIM).transpose(1, 0, 2)


def _rope_tables(t, tc):
    tok = jnp.arange(t, dtype=jnp.int32)
    inv_freq = ROPE_THETA ** (-jnp.arange(ROPE_FREQS, dtype=F32) / ROPE_FREQS)
    ang_r = (tok // GRID_W).astype(F32)[:, None] * inv_freq
    ang_c = (tok % GRID_W).astype(F32)[:, None] * inv_freq
    cos64 = jnp.concatenate([jnp.cos(ang_r), jnp.cos(ang_r), jnp.cos(ang_c), jnp.cos(ang_c)], axis=1)
    sin64 = jnp.concatenate([-jnp.sin(ang_r), jnp.sin(ang_r), -jnp.sin(ang_c), jnp.sin(ang_c)], axis=1)
    cos64 = jnp.concatenate([jnp.ones((tc, HEAD_DIM), F32), cos64], axis=0)
    sin64 = jnp.concatenate([jnp.zeros((tc, HEAD_DIM), F32), sin64], axis=0)
    return jnp.tile(cos64, (1, N_Q_HEADS)), jnp.tile(sin64, (1, N_Q_HEADS))


def _head_mean_matrix(width):
    i = jnp.arange(width) // HEAD_DIM
    return (i[:, None] == i[None, :]).astype(F32) / HEAD_DIM


def _head_tile_matrix(width):
    return (jnp.arange(HEAD_DIM)[:, None] == (jnp.arange(width) % HEAD_DIM)[None, :]).astype(F32)


def _heads_t(a, nh):
    return a.T.reshape(nh, HEAD_DIM, a.shape[0])


def _heads_t_back(a):
    return a.reshape(a.shape[0] * HEAD_DIM, a.shape[2]).T


def _attention_fwd(tag, qr, kr, vv, tc, comm):
    qt8, k2, vt2 = _heads_t(qr, N_Q_HEADS), _heads_front(kr, N_KV_HEADS), _heads_t(vv, N_KV_HEADS)
    vt2 = jnp.concatenate([vt2, jnp.ones((N_KV_HEADS, 8, vt2.shape[2]), BF16)], axis=1)
    o_c, lse_c, _ = _attn_fwd(tag + "_attn_ctx", qt8[:, :, :tc], k2[:, :tc], vt2[:, :, :tc])
    o_x, lse_x, comm_out = _attn_fwd(tag + "_attn_lat", qt8[:, :, tc:], k2, vt2, comm)
    ot8 = jnp.concatenate([o_c, o_x], axis=2)
    lse = jnp.concatenate([lse_c, lse_x], axis=2)
    return _heads_t_back(ot8), (qr, kr, vv, ot8, lse), comm_out


def _attention_bwd(tag, saved, datt, tc, comm):
    qr, kr, vv, ot8, lse = saved
    datt = datt.astype(BF16)
    qt8, dot8 = _heads_t(qr, N_Q_HEADS), _heads_t(datt, N_Q_HEADS)
    k2, v2, kt2 = _heads_front(kr, N_KV_HEADS), _heads_front(vv, N_KV_HEADS), _heads_t(kr, N_KV_HEADS)
    zero = jnp.zeros((N_KV_HEADS, HEAD_DIM, tc), F32)
    dq_c, dk_c, dv_c, _ = _attn_bwd(tag + "_attn_b_ctx", qt8[:, :, :tc], k2[:, :tc], v2[:, :tc], kt2[:, :, :tc],
                                    ot8[:, :, :tc], dot8[:, :, :tc], lse[:, :, :tc], zero, zero)
    dq_x, dkt2, dvt2, comm_out = _attn_bwd(tag + "_attn_b_lat", qt8[:, :, tc:], k2, v2, kt2,
                                           ot8[:, :, tc:], dot8[:, :, tc:], lse[:, :, tc:], dk_c, dv_c, comm)
    dqt8 = jnp.concatenate([dq_c, dq_x], axis=2)
    return _heads_t_back(dqt8), _heads_t_back(dkt2), _heads_t_back(dvt2), comm_out


def _row_tiles(n, R):
    return _pick(n, (768, R)), _pick(n, (384, R))


def _layer_fwd(tag, x, w, modv, consts, R, nct, tc, comm, late_weights):
    sh1, sc1, g1, sh2, sc2, g2 = modv
    cosq, sinq, bdq, eq, bdk, ek = consts
    rw, rm = _row_tiles(x.shape[0], R)
    (h1,) = _rowwise(tag + "_ln1", _f_lnmod, rw, tc, [(x, D, 0, D)], [], [(w["norm1_g"], False)], [], [sh1, sc1], [(D, BF16)])
    p = _matmul(tag + "_in", h1, w["w_in"], "nn", F32)
    qk_rows = [(p, Q_W, C_Q // Q_W, Q_W), (p, KV_W, C_K // KV_W, KV_W)]
    qk_consts = [(cosq, Q_W, 0, Q_W), (sinq, Q_W, 0, Q_W), (cosq, KV_W, 0, KV_W), (sinq, KV_W, 0, KV_W)]
    qk_params = [(w["q_norm_g"], False), (w["k_norm_g"], False)]
    qk_cparams = [(bdq, False), (eq, False), (bdk, False), (ek, False)]
    qr, kr = _rowwise(tag + "_qk", _f_qknorm, rw, tc, qk_rows, qk_consts, qk_params, qk_cparams, [],
                      [(Q_W, BF16), (KV_W, BF16)], post=_qk_post)
    vv = p[:, C_VV:C_VV + KV_W].astype(BF16)
    att, att_saved, comm_out = _attention_fwd(tag, qr, kr, vv, tc, comm)
    w.update(late_weights(comm_out))

    o_f, o_b, s_f, s_b = _gla_fwd(tag + "_gla", p, (w["w2p_f"], w["w2p_b"]), (w["b2_f"], w["b2_b"]), R, nct)
    go_rows = [(o_f, GLA_V_W, 0, GLA_DV), (o_b, GLA_V_W, 0, GLA_DV), (p, GLA_V_W, C_R // GLA_V_W, GLA_DV)]
    (gla,) = _rowwise(tag + "_glaout", _f_glaout, rw, tc, go_rows, [], [(w["gla_norm_g"], True)], [], [], [(GLA_V_W, BF16)])

    rg = GMLP_CHUNK
    gm_rows = [(p, GMLP_W, C_U // GMLP_W, GMLP_W), (p, GMLP_W, C_V // GMLP_W, GMLP_W // GMLP_GROUPS)]
    gm_params = [(w["gmlp_norm_g"], True), (w["w_spatial"], True), (w["b_spatial_t"], True)]
    (gm,) = _rowwise(tag + "_gmlp", _f_gmlp, rg, tc, gm_rows, [], gm_params, [], [], [(GMLP_W, BF16)])

    ya = _matmul(tag + "_br_a", gm, w["w_br_a"], "nn", F32)
    yb = _matmul(tag + "_br_b", att, w["w_br_b"], "nn", F32)
    yc = _matmul(tag + "_br_c", gla, w["w_br_c"], "nn", F32)
    mg_rows = [(p, D, C_GA // D, D), (p, D, C_GB // D, D), (p, D, C_GC // D, D), (ya, D, 0, D), (yb, D, 0, D), (yc, D, 0, D)]
    (merged,) = _rowwise(tag + "_merge", _f_merge, rm, tc, mg_rows, [], [], [], [], [(D, BF16)])
    mix = _matmul(tag + "_out", merged, w["w_out"], "nn", F32)
    (x_mid,) = _rowwise(tag + "_res1", _f_resid, rw, tc, [(x, D, 0, D), (mix, D, 0, D)], [], [], [], [g1], [(D, F32)])

    (h2,) = _rowwise(tag + "_ln2", _f_lnmod, rw, tc, [(x_mid, D, 0, D)], [], [(w["norm2_g"], False)], [], [sh2, sc2], [(D, BF16)])
    a2 = _matmul(tag + "_up", h2, w["w_ffn_up"], "nn", F32, o_halves=True)
    gv = _conv_gate(tag + "_conv", a2, w["conv_w_h"], w["conv_b_h"], R, tc)
    ffn = _matmul(tag + "_down", gv, w["w_ffn_down"], "nn", F32)
    (x_next,) = _rowwise(tag + "_res2", _f_resid, rw, tc, [(x_mid, D, 0, D), (ffn, D, 0, D)], [], [], [], [g2], [(D, F32)])
    saved = dict(x=x, h1=h1, p=p, att_saved=att_saved, att=att, s_f=s_f, s_b=s_b, gla=gla, gm=gm,
                 ya=ya, yb=yb, yc=yc, merged=merged, mix=mix, x_mid=x_mid, h2=h2, a2=a2, gv=gv, ffn=ffn,
                 qk=(qk_rows, qk_consts, qk_params, qk_cparams), go_rows=go_rows, gm_info=(gm_rows, gm_params),
                 mg_rows=mg_rows)
    return x_next, saved, comm_out


def _layer_bwd(tag, dx_next, s, w, modv, R, nct, tc, make_comm, make_tail_comm):
    sh1, sc1, g1, sh2, sc2, g2 = modv
    rw, rm = _row_tiles(dx_next.shape[0], R)
    gw = {}
    (dffn,), _, (dg2,) = _rowwise_bwd(tag + "_res2_b", _f_resid, rw, tc, [(s["ffn"], D, 0, D), (s["ffn"], D, 0, D)], [], [], [], [g2],
                                      [(dx_next, D)], [None, BF16])
    dgv = _matmul(tag + "_down_da", dffn, w["w_ffn_down"], "nt", F32)
    gw["w_ffn_down"] = _matmul(tag + "_down_dw", s["gv"], dffn, "tn", F32)
    da2, dcw, dcb = _conv_gate_bwd(tag + "_conv_b", s["a2"], dgv, w["conv_w_h"], w["conv_b_h"], R, tc)
    gw["conv_w_h"], gw["conv_b_h"] = dcw, dcb
    dh2 = _matmul(tag + "_up_da", da2, w["w_ffn_up"], "nt", F32, a_halves=True)
    gw["w_ffn_up"] = _matmul(tag + "_up_dw", s["h2"], da2, "tn", F32, b_halves=True)
    (dx_mid,), (gw["norm2_g"],), (dsh2, dsc2) = _rowwise_bwd(
        tag + "_ln2_b", _f_lnmod, rw, tc, [(s["x_mid"], D, 0, D)], [], [(w["norm2_g"], False)], [], [sh2, sc2],
        [(dh2, D)], [F32], adds=[dx_next])
    (dmix,), _, (dg1,) = _rowwise_bwd(tag + "_res1_b", _f_resid, rw, tc, [(s["mix"], D, 0, D), (s["mix"], D, 0, D)], [], [], [], [g1],
                                      [(dx_mid, D)], [None, BF16])
    dmerged = _matmul(tag + "_out_da", dmix, w["w_out"], "nt", F32)
    gw["w_out"] = _matmul(tag + "_out_dw", s["merged"], dmix, "tn", F32)
    (dga, dgb, dgc, dya, dyb, dyc), _, _ = _rowwise_bwd(tag + "_merge_b", _f_merge, rm, tc, s["mg_rows"], [], [], [], [],
                                                        [(dmerged, D)], [BF16] * 6)
    dgm = _matmul(tag + "_br_a_da", dya, w["w_br_a"], "nt", F32)
    datt = _matmul(tag + "_br_b_da", dyb, w["w_br_b"], "nt", F32)
    dgla = _matmul(tag + "_br_c_da", dyc, w["w_br_c"], "nt", F32)
    gm_rows, gm_params = s["gm_info"]
    gw["w_br_a"] = _matmul(tag + "_br_a_dw", s["gm"], dya, "tn", F32)
    gw["w_br_b"] = _matmul(tag + "_br_b_dw", s["att"], dyb, "tn", F32)
    gw["w_br_c"] = _matmul(tag + "_br_c_dw", s["gla"], dyc, "tn", F32)
    rg = GMLP_CHUNK
    (du, dv_), (gw["gmlp_norm_g"], gw["w_spatial"], gw["b_spatial_t"]), _ = _rowwise_bwd(
        tag + "_gmlp_b", _f_gmlp, rg, tc, gm_rows, [], gm_params, [], [], [(dgm, GMLP_W)], [BF16, BF16])
    (do, dr), (gw["gla_norm_g"],), _ = _rowwise_bwd(tag + "_glaout_b", _f_glaout, rw, tc, s["go_rows"], [],
                                                    [(w["gla_norm_g"], True)], [], [], [(dgla, GLA_V_W)], [F32, None, BF16])
    p = s["p"]
    d_f, d_b, (gw["w2p_f"], gw["b2_f"], gw["w2p_b"], gw["b2_b"]) = _gla_bwd(
        tag + "_gla_b", p, (w["w2p_f"], w["w2p_b"]), (w["b2_f"], w["b2_b"]), (s["s_f"], s["s_b"]), do, R, nct)
    dgq, dgk, dgv_, daf = [(a + b).astype(BF16) for a, b in zip(d_f, d_b)]
    dqr, dkr, dvv, comm_out = _attention_bwd(tag, s["att_saved"], datt, tc, make_comm(gw))
    qk_rows, qk_consts, qk_params, qk_cparams = s["qk"]
    (dq, dk), (gw["q_norm_g"], gw["k_norm_g"]), _ = _rowwise_bwd(
        tag + "_qk_b", _f_qknorm, rw, tc, qk_rows, qk_consts, qk_params, qk_cparams, [],
        [(dqr, Q_W), (dkr, KV_W)], [BF16, BF16], pre=_qk_pre)
    dp = jnp.concatenate([dga, dgb, dgc, du, dv_, dq, dgv_, dr, dgq, dgk, dk, dvv.astype(BF16), daf,
                          jnp.zeros((p.shape[0], PW - C_AF - 128), BF16)], axis=1)
    gw["w_in"] = _matmul(tag + "_in_dw", s["h1"], dp, "tn", F32)
    tail = make_tail_comm(gw)
    if tail is None:
        dh1, tail_out = _matmul(tag + "_in_da", dp, w["w_in"], "nt", F32), []
    else:
        dh1, tail_out = _matmul(tag + "_in_da", dp, w["w_in"], "nt", F32, comm=tail)
    (dx,), (gw["norm1_g"],), (dsh1, dsc1) = _rowwise_bwd(
        tag + "_ln1_b", _f_lnmod, rw, tc, [(s["x"], D, 0, D)], [], [(w["norm1_g"], False)], [], [sh1, sc1],
        [(dh1, D)], [F32], adds=[dx_mid])
    return dx, gw, (dsh1, dsc1, dg1, dsh2, dsc2, dg2), comm_out, tail_out


_SHARDED = (("w_in", 1, True), ("w_br_a", 1, True), ("w_br_b", 1, True), ("w_br_c", 1, True), ("w_out", 0, True),
            ("w_ffn_up", 1, True), ("w_ffn_down", 0, True), ("conv_w", 1, False), ("w_alpha2", 2, False), ("b_alpha", 1, False))
_REPLICATED = ("c_ctx", "b_ada", "norm1_g", "norm2_g", "q_norm_g", "k_norm_g", "gmlp_norm_g", "w_spatial", "b_spatial",
               "gla_norm_g", "conv_b", "final_norm_g")
_WEIGHTS = ("c_ctx", "w_ada", "b_ada", "norm1_g", "norm2_g", "w_in", "q_norm_g", "k_norm_g", "gmlp_norm_g", "w_spatial",
            "b_spatial", "w_alpha2", "b_alpha", "gla_norm_g", "w_br_a", "w_br_b", "w_br_c", "w_out", "w_ffn_up", "conv_w",
            "conv_b", "w_ffn_down", "final_norm_g")


def _decay_weights(w_alpha2_l, b_alpha_l):
    out = []
    for d in range(2):
        w2p = jnp.zeros((128, GLA_QK_W), F32).at[GLA_RANK * d:GLA_RANK * (d + 1)].set(w_alpha2_l[d])
        out += [w2p, b_alpha_l[d][None, :]]
    return out


def _step(inp, wts, moms, vels):
    x, c, ctx, loss_target = inp
    t, tc = x.shape[1], ctx.shape[1]
    n = t + tc
    R = min(256, tc)
    nct = tc // R
    me = _my_index()
    depth = wts["w_in"].shape[0]

    late = [(nm, ax) for nm, ax, half in _SHARDED if half and nm != "w_in"]
    small = [(nm, ax) for nm, ax, half in _SHARDED if not half]
    w_in_shard = lambda l: wts["w_in"][l].astype(BF16)
    c8 = jnp.concatenate([c, jnp.zeros((7, D), F32)], axis=0)
    first = _exchange("gather_first", [w_in_shard(0)] + [wts[nm] for nm, _ in small] + [c8], gather=True)
    c_all = first[-1][:, 0, :]
    small_all = dict(zip([nm for nm, _ in small], first[1:-1]))

    def early_weights(l, w_in_all):
        w = {"w_in": _win_my_cols([w_in_all[s] for s in range(N_DEV)])}
        conv_w, w_alpha2, b_alpha = [jnp.concatenate([small_all[nm][s, l] for s in range(N_DEV)], axis=ax) for nm, ax in small]
        w["conv_w_h"] = conv_w.reshape(3, 2, FFN_H).transpose(1, 0, 2)
        w["conv_b_h"] = wts["conv_b"][l].reshape(2, 1, FFN_H)
        w["w2p_f"], w["b2_f"], w["w2p_b"], w["b2_b"] = _decay_weights(w_alpha2, b_alpha)
        w["norm1_g"] = wts["norm1_g"][l][None, :]
        w["norm2_g"] = wts["norm2_g"][l][None, :]
        w["q_norm_g"] = wts["q_norm_g"][l][None, :]
        w["k_norm_g"] = wts["k_norm_g"][l][None, :]
        w["gmlp_norm_g"] = wts["gmlp_norm_g"][l].reshape(GMLP_GROUPS, 1, GMLP_W // GMLP_GROUPS)
        w["w_spatial"] = wts["w_spatial"][l]
        w["b_spatial_t"] = wts["b_spatial"][l][:, :, None]
        w["gla_norm_g"] = wts["gla_norm_g"][l].reshape(GLA_HEADS, 1, GLA_DV)
        return w

    craw16 =jnp.concatenate([c_all, wts["c_ctx"][None, :], jnp.zeros((7, D), F32)], axis=0)
    acols = wts["w_ada"].shape[2]
    b_cols = lax.dynamic_slice_in_dim(wts["b_ada"], me * acols, acols, axis=1)[:, None, :]
    mod_part = _adaln_fwd("adaln", craw16, wts["w_ada"], b_cols)
    send = jnp.stack([mod_part[:, 8, :][None].repeat(N_DEV, 0), mod_part[:, :8, :].transpose(1, 0, 2)], axis=2)
    send = jnp.concatenate([send.reshape(N_DEV, 2 * depth, acols), jnp.zeros((N_DEV, 8 - 2 * depth, acols), F32)], axis=1)
    (got,) = _exchange("scatter_mod", [send], gather=False)
    mod = got[:, :2 * depth, :].transpose(1, 0, 2).reshape(depth, 2, N_MOD, 1, D)
    modv = [[mod[l, :, k] for k in range(N_MOD)] for l in range(depth)]

    cosq, sinq = _rope_tables(t, tc)
    consts = (cosq, sinq, _head_mean_matrix(Q_W), _head_tile_matrix(Q_W), _head_mean_matrix(KV_W), _head_tile_matrix(KV_W))
    xs = jnp.concatenate([ctx[0], x[0]], axis=0)
    saved, layers = [], []
    w_in_all = first[0]

    def late_weights(got):
        return {nm: _join_shards(g, ax) for (nm, ax), g in zip(late, got)}

    for l in range(depth):
        layers.append(early_weights(l, w_in_all))
        sending = [wts[nm][l].astype(BF16) for nm, _ in late] + ([w_in_shard(l + 1)] if l + 1 < depth else [])
        xs, sv, got = _layer_fwd("l%d" % l, xs, layers[l], modv[l], consts, R, nct, tc, (sending, True), late_weights)
        if l + 1 < depth:
            w_in_all = got[len(late)]
        saved.append(sv)
    loss_blk, dxs, dgf = _final_loss("final", xs, loss_target[0], wts["final_norm_g"][None, :], R, nct)
    loss = lax.psum(loss_blk[0, 0], ("x", "y", "c"))

    grads = [None] * depth
    dmods = [None] * depth
    late_parts = [None] * depth
    w_in_parts = [None] * depth
    w_in_grad_shards = lambda g: _win_shards(g["w_in"]).astype(BF16)

    def small_grad_shards(g):
        full = dict(conv_w=g["conv_w_h"].transpose(1, 0, 2).reshape(3, F2),
                    w_alpha2=jnp.stack([g["w2p_f"][:GLA_RANK], g["w2p_b"][GLA_RANK:2 * GLA_RANK]]),
                    b_alpha=jnp.stack([g["b2_f"][0], g["b2_b"][0]]))
        return [_split_shards(full[nm], ax) for nm, ax in small]

    for l in range(depth - 1, -1, -1):
        def make_comm(gw, l=l):
            sending = [_split_shards(gw[nm], ax).astype(BF16) for nm, ax in late]
            return sending + ([w_in_grad_shards(grads[l + 1])] if l + 1 < depth else []), False

        def make_tail_comm(gw, l=l):
            if l > 0:
                return None
            per_layer = [small_grad_shards(gw if k == 0 else grads[k]) for k in range(depth)]
            return [w_in_grad_shards(gw)] + [jnp.stack([per_layer[k][i] for k in range(depth)], axis=1) for i in range(len(small))], False

        dxs, grads[l], dmods[l], got, tail = _layer_bwd("l%d" % l, dxs, saved[l], layers[l], modv[l], R, nct, tc,
                                                        make_comm, make_tail_comm)
        late_parts[l] = got[:len(late)]
        if l + 1 < depth:
            w_in_parts[l + 1] = got[len(late)]
    w_in_parts[0], small_parts = tail[0], tail[1:]
    grad_x = dxs[tc:][None]

    dmod = jnp.stack([jnp.stack(dmods[l], axis=1) for l in range(depth)])
    dmod = dmod.reshape(depth, 2, N_DEV, acols).transpose(2, 0, 1, 3).reshape(N_DEV, 2 * depth, acols)
    dmod_send = jnp.concatenate([dmod, jnp.zeros((N_DEV, 8 - 2 * depth, acols), F32)], axis=1)
    (dm_got,) = _exchange("scatter_dmod", [dmod_send], gather=False)
    g_w_ada, dc16 = _adaln_bwd("adaln_b", craw16, wts["w_ada"], dm_got[:, :2 * depth].transpose(1, 0, 2))
    db_ada_part = jnp.stack([jnp.stack(dmods[l], axis=1) for l in range(depth)]).reshape(depth, 2, N_MOD * D).sum(axis=1)

    out = {}
    kinds = ("grad", "delta", "new_m", "new_v")
    view2 = lambda a: a.reshape(-1, a.shape[-1])
    sharded_parts = ([jnp.stack(w_in_parts, axis=1)]
                     + [jnp.stack([late_parts[l][k] for l in range(depth)], axis=1) for k in range(len(late))] + small_parts)
    for (nm, _), parts in zip([("w_in", 1)] + late + small, sharded_parts):
        res = _adamw("adamw_" + nm, view2(wts[nm]), view2(moms[nm]), view2(vels[nm]), parts.reshape(N_DEV, -1, parts.shape[-1]))
        for kind, flat in zip(kinds, res):
            out[kind, nm] = flat.reshape(wts[nm].shape)

    rep_g = dict(
        c_ctx=dc16[8], b_ada=db_ada_part, final_norm_g=dgf[0],
        norm1_g=jnp.stack([grads[l]["norm1_g"][0] for l in range(depth)]),
        norm2_g=jnp.stack([grads[l]["norm2_g"][0] for l in range(depth)]),
        q_norm_g=jnp.stack([grads[l]["q_norm_g"][0] for l in range(depth)]),
        k_norm_g=jnp.stack([grads[l]["k_norm_g"][0] for l in range(depth)]),
        gmlp_norm_g=jnp.stack([grads[l]["gmlp_norm_g"].reshape(GMLP_W) for l in range(depth)]),
        w_spatial=jnp.stack([grads[l]["w_spatial"] for l in range(depth)]),
        b_spatial=jnp.stack([grads[l]["b_spatial_t"][:, :, 0] for l in range(depth)]),
        gla_norm_g=jnp.stack([grads[l]["gla_norm_g"].reshape(GLA_V_W) for l in range(depth)]),
        conv_b=jnp.stack([grads[l]["conv_b_h"].reshape(F2) for l in range(depth)]),
    )
    rep_shapes = [wts[nm].shape for nm in _REPLICATED]
    (rg_parts,) = _exchange("gather_rep_grads", [_pack_rows([rep_g[nm] for nm in _REPLICATED], 16)], gather=True)
    rpk = lambda src: _pack_rows([src[nm] for nm in _REPLICATED], 16)
    res = _adamw("adamw_rep", rpk(wts), rpk(moms), rpk(vels), rg_parts)
    for kind, rows in zip(kinds, res):
        for nm, piece in zip(_REPLICATED, _unpack_rows(rows, rep_shapes)):
            out[kind, nm] = piece

    res = _adamw("adamw_ada", view2(wts["w_ada"]), view2(moms["w_ada"]), view2(vels["w_ada"]), view2(g_w_ada)[None])
    for kind, flat in zip(kinds, res):
        out[kind, "w_ada"] = flat.reshape(wts["w_ada"].shape)

    return (loss, grad_x, *[out[kind, nm] for kind in kinds for nm in _WEIGHTS])


def kernel(x, c, ctx, c_ctx, w_ada, b_ada, norm1_g, norm2_g, w_in, q_norm_g, k_norm_g, gmlp_norm_g, w_spatial, b_spatial, w_alpha2, b_alpha, gla_norm_g, w_br_a, w_br_b, w_br_c, w_out, w_ffn_up, conv_w, conv_b, w_ffn_down, final_norm_g, loss_target, m_c_ctx, m_w_ada, m_b_ada, m_norm1_g, m_norm2_g, m_w_in, m_q_norm_g, m_k_norm_g, m_gmlp_norm_g, m_w_spatial, m_b_spatial, m_w_alpha2, m_b_alpha, m_gla_norm_g, m_w_br_a, m_w_br_b, m_w_br_c, m_w_out, m_w_ffn_up, m_conv_w, m_conv_b, m_w_ffn_down, m_final_norm_g, v_c_ctx, v_w_ada, v_b_ada, v_norm1_g, v_norm2_g, v_w_in, v_q_norm_g, v_k_norm_g, v_gmlp_norm_g, v_w_spatial, v_b_spatial, v_w_alpha2, v_b_alpha, v_gla_norm_g, v_w_br_a, v_w_br_b, v_w_br_c, v_w_out, v_w_ffn_up, v_conv_w, v_conv_b, v_w_ffn_down, v_final_norm_g):
    wts = dict(zip(_WEIGHTS, (c_ctx, w_ada, b_ada, norm1_g, norm2_g, w_in, q_norm_g, k_norm_g, gmlp_norm_g, w_spatial, b_spatial,
                              w_alpha2, b_alpha, gla_norm_g, w_br_a, w_br_b, w_br_c, w_out, w_ffn_up, conv_w, conv_b, w_ffn_down,
                              final_norm_g)))
    moms = dict(zip(_WEIGHTS, (m_c_ctx, m_w_ada, m_b_ada, m_norm1_g, m_norm2_g, m_w_in, m_q_norm_g, m_k_norm_g, m_gmlp_norm_g,
                               m_w_spatial, m_b_spatial, m_w_alpha2, m_b_alpha, m_gla_norm_g, m_w_br_a, m_w_br_b, m_w_br_c, m_w_out,
                               m_w_ffn_up, m_conv_w, m_conv_b, m_w_ffn_down, m_final_norm_g)))
    vels = dict(zip(_WEIGHTS, (v_c_ctx, v_w_ada, v_b_ada, v_norm1_g, v_norm2_g, v_w_in, v_q_norm_g, v_k_norm_g, v_gmlp_norm_g,
                               v_w_spatial, v_b_spatial, v_w_alpha2, v_b_alpha, v_gla_norm_g, v_w_br_a, v_w_br_b, v_w_br_c, v_w_out,
                               v_w_ffn_up, v_conv_w, v_conv_b, v_w_ffn_down, v_final_norm_g)))
    return _step((x, c, ctx, loss_target), wts, moms, vels)
```

```python
import functools

import jax
import jax.numpy as jnp
from jax import lax
from jax.experimental import pallas as pl
from jax.experimental.pallas import tpu as pltpu

F32 = jnp.float32
BF16 = jnp.bfloat16
HI = lax.Precision.HIGHEST
MESH_ID = pl.DeviceIdType.MESH

N_DEV = 8
EPS = 1e-6
D = 1024
N_MOD = 6
HEAD_DIM = 64
N_Q_HEADS = 8
N_KV_HEADS = 2
Q_GROUP = 4
Q_W = 512
KV_W = 128
GRID_W = 64
ROPE_THETA = 10000.0
ROPE_FREQS = 16
GMLP_CHUNK = 128
GMLP_GROUPS = 4
GMLP_W = 512
GLA_HEADS = 4
GLA_QK_W = 256
GLA_V_W = 512
GLA_DK = 64
GLA_DV = 128
GLA_RANK = 16
GLA_TAU = 16.0
GLA_CHUNK = 64
FFN_H = 2816
F2 = 2 * FFN_H
IN_SPLITS = (512, 512, 512, 128, 128, 256, 256, 512, 16, 16, 512, 1024, 1024, 1024)
IN_WIDTH = sum(IN_SPLITS)

C_GA, C_GB, C_GC = 0, 1024, 2048
C_U, C_V, C_Q, C_GV, C_R = 3072, 3584, 4096, 4608, 5120
C_GQ, C_GK = 5632, 5888
C_K, C_VV, C_AF = 6144, 6272, 6400
PW = 6656

ADAM_LR = 0.001
ADAM_B1 = 0.9
ADAM_B2 = 0.999
ADAM_EPS = 1e-08
ADAM_WD = 0.01
ADAM_STEP = 10

V7X_VMEM_LIMIT = 56 * 1024 * 1024

_ARB1 = pltpu.CompilerParams(dimension_semantics=("arbitrary",), vmem_limit_bytes=V7X_VMEM_LIMIT)


def _pick(dim, prefs):
    for p in prefs:
        if dim % p == 0:
            return p
    return dim


def _hdot(a, b, dims=(((1,), (0,)), ((), ()))):
    return lax.dot_general(a, b, dims, precision=HI, preferred_element_type=F32)


_NT = (((1,), (1,)), ((), ()))
_TN = (((0,), (0,)), ((), ()))
_NN = (((1,), (0,)), ((), ()))


def _matmul(name, a, b, mode, out_dtype, *, a_halves=False, b_halves=False, o_halves=False, comm=None):
    def dims2(x, halves):
        return (x.shape[1], 2 * x.shape[2]) if halves else x.shape

    ar, ac = dims2(a, a_halves)
    br, bc = dims2(b, b_halves)
    if mode == "nn":
        M, K, N = ar, ac, bc
    elif mode == "nt":
        M, K, N = ar, ac, br
    else:
        M, K, N = ac, ar, bc
    n_unit = N // 2 if (o_halves or (b_halves and mode != "nt")) else N
    k_unit = K // 2 if (a_halves and mode != "tn") else K
    tokens = lambda d: d > 7168
    if tokens(K):
        tm = _pick(M, (512, 1408, 256, 128))
        tk = _pick(k_unit, (2816, 1408, 768, 512, 384, 256, 128))
    else:
        tm = _pick(M, (768, 512, 384, 256, 128) if tokens(M) else (1024, 1408, 512, 256, 128))
        tk = _pick(k_unit, (3328, 2816, 1664, 1408, 1024, 512, 256, 128))
    tn = _pick(n_unit, (1664, 1408, 1024, 512, 256, 128))
    nk = K // tk

    def spec(shape2, halves, blk, imap):
        if not halves:
            return pl.BlockSpec(blk, imap)
        nhalf = (shape2[1] // 2) // blk[1]

        def im(i, j, k):
            r, c = imap(i, j, k)
            return (c // nhalf, r, c % nhalf)
        return pl.BlockSpec((None,) + blk, im)

    if mode == "nn":
        a_spec = spec((ar, ac), a_halves, (tm, tk), lambda i, j, k: (i, k))
        b_spec = spec((br, bc), b_halves, (tk, tn), lambda i, j, k: (k, j))
        dn = _NN
    elif mode == "nt":
        a_spec = spec((ar, ac), a_halves, (tm, tk), lambda i, j, k: (i, k))
        b_spec = spec((br, bc), b_halves, (tn, tk), lambda i, j, k: (j, k))
        dn = _NT
    else:
        a_spec = spec((ar, ac), a_halves, (tk, tm), lambda i, j, k: (k, i))
        b_spec = spec((br, bc), b_halves, (tk, tn), lambda i, j, k: (k, j))
        dn = _TN
    o_spec = spec((M, N), o_halves, (tm, tn), lambda i, j, k: (i, j))
    o_shape = (2, M, N // 2) if o_halves else (M, N)

    def body(a_ref, b_ref, o_ref, acc_ref):
        k = pl.program_id(2)
        part = lax.dot_general(a_ref[...], b_ref[...], dn, preferred_element_type=F32)
        if nk == 1:
            o_ref[...] = part.astype(o_ref.dtype)
        else:
            @pl.when(k == 0)
            def _():
                acc_ref[...] = part

            @pl.when(k > 0)
            def _():
                acc_ref[...] += part

            @pl.when(k == nk - 1)
            def _():
                o_ref[...] = acc_ref[...].astype(o_ref.dtype)

    grid = (M // tm, N // tn, nk)
    body, xspecs, xshapes, xsems = _with_exchange(body, 2, 1, grid, comm)
    res = pl.pallas_call(
        body, name=name, grid=grid,
        in_specs=[a_spec, b_spec] + xspecs, out_specs=[o_spec] + xspecs,
        out_shape=[jax.ShapeDtypeStruct(o_shape, out_dtype)] + xshapes,
        scratch_shapes=[pltpu.VMEM((tm, tn), F32)] + xsems,
        compiler_params=pltpu.CompilerParams(dimension_semantics=("arbitrary", "arbitrary", "arbitrary"),
                                             vmem_limit_bytes=V7X_VMEM_LIMIT),
    )(a, b, *(comm[0] if comm else []))
    return res[0] if comm is None else (res[0], list(res[1:]))


def _full_spec(shape):
    nd = len(shape)
    return pl.BlockSpec(tuple(shape), lambda i, _nd=nd: (0,) * _nd)


def _row_spec(R, W, cb):
    return pl.BlockSpec((R, W), lambda i, _cb=cb: (i, _cb))


def _load_rows(refs, specs):
    vals = []
    for ref, (_, W, _, pw) in zip(refs, specs):
        if pw == W:
            vals.append(ref[...].astype(F32))
        else:
            vals.append([ref[:, k * pw:(k + 1) * pw].astype(F32) for k in range(W // pw)])
    return vals


def _load_params(refs, specs):
    vals = []
    for ref, (arr, split) in zip(refs, specs):
        if split:
            vals.append([ref[k] for k in range(arr.shape[0])])
        else:
            vals.append(ref[...])
    return vals


def _row_mods(pairs, first_row, rows, tc):
    is_latent = ((first_row + lax.broadcasted_iota(jnp.int32, (rows, 1), 0)) >= tc).astype(F32)
    return [mc + is_latent * (mx - mc) for mc, mx in pairs]


def _rowwise(name, f, R, tc, rows, consts, params, cparams, mods, outs, post=None):
    n = rows[0][0].shape[0]
    nr, nc, npar, ncp, nm = len(rows), len(consts), len(params), len(cparams), len(mods)

    def body(*refs):
        pos = 0
        rr = refs[pos:pos + nr]; pos += nr
        cr = refs[pos:pos + nc]; pos += nc
        pr = refs[pos:pos + npar]; pos += npar
        cpr = refs[pos:pos + ncp]; pos += ncp
        mr = refs[pos:pos + nm]; pos += nm
        orefs = refs[pos:]
        res = f(_load_rows(rr, rows), _load_params(pr, params),
                _row_mods([(m[0], m[1]) for m in mr], pl.program_id(0) * R, R, tc),
                _load_rows(cr, consts), _load_params(cpr, cparams))
        if post is not None:
            res = post(res, _load_rows(cr, consts))
        for o_ref, r in zip(orefs, res):
            o_ref[...] = r.astype(o_ref.dtype)

    in_specs = ([_row_spec(R, W, cb) for (_, W, cb, _) in rows + consts]
                + [_full_spec(a.shape) for (a, _) in params + cparams]
                + [_full_spec(m.shape) for m in mods])
    args = [a for (a, _, _, _) in rows + consts] + [a for (a, _) in params + cparams] + list(mods)
    return pl.pallas_call(
        body, name=name, grid=(n // R,), in_specs=in_specs,
        out_specs=[_row_spec(R, w, 0) for (w, _) in outs],
        out_shape=[jax.ShapeDtypeStruct((n, w), dt) for (w, dt) in outs],
        compiler_params=_ARB1,
    )(*args)


def _rowwise_bwd(name, f, R, tc, rows, consts, params, cparams, mods, douts, drow, adds=None, pre=None):
    n = rows[0][0].shape[0]
    adds = adds or [None] * len(rows)
    nr, nc, npar, ncp, nm, nd = len(rows), len(consts), len(params), len(cparams), len(mods), len(douts)
    add_ix = [k for k in range(nr) if adds[k] is not None]
    out_ix = [k for k in range(nr) if drow[k] is not None]

    def body(*refs):
        i = pl.program_id(0)
        pos = 0
        rr = refs[pos:pos + nr]; pos += nr
        cr = refs[pos:pos + nc]; pos += nc
        pr = refs[pos:pos + npar]; pos += npar
        cpr = refs[pos:pos + ncp]; pos += ncp
        mr = refs[pos:pos + nm]; pos += nm
        dr = refs[pos:pos + nd]; pos += nd
        ar = refs[pos:pos + len(add_ix)]; pos += len(add_ix)
        drr = refs[pos:pos + len(out_ix)]; pos += len(out_ix)
        dpr = refs[pos:pos + npar]; pos += npar
        dmr = refs[pos:pos + nm]; pos += nm

        cv = _load_rows(cr, consts)
        cpv = _load_params(cpr, cparams)
        _, vjp = jax.vjp(lambda rv, pv, mv: f(rv, pv, _row_mods(mv, i * R, R, tc), cv, cpv),
                         _load_rows(rr, rows), _load_params(pr, params), [(m[0], m[1]) for m in mr])
        dv = [d[...].astype(F32) for d in dr]
        if pre is not None:
            dv = pre(dv, cv)
        g_rows, g_params, g_mods = vjp(tuple(dv))

        for ref, k in zip(drr, out_ix):
            _, W, _, pw = rows[k]
            g = g_rows[k]
            extra = ar[add_ix.index(k)] if k in add_ix else None
            if pw == W:
                if extra is not None:
                    g = g + extra[...].astype(F32)
                ref[...] = g.astype(ref.dtype)
            else:
                for q in range(W // pw):
                    gq = g[q]
                    if extra is not None:
                        gq = gq + extra[:, q * pw:(q + 1) * pw].astype(F32)
                    ref[:, q * pw:(q + 1) * pw] = gq.astype(ref.dtype)

        @pl.when(i == 0)
        def _():
            for ref in tuple(dpr) + tuple(dmr):
                ref[...] = jnp.zeros_like(ref)

        for ref, (arr, split), g in zip(dpr, params, g_params):
            if split:
                for k in range(arr.shape[0]):
                    ref[k] += g[k]
            else:
                ref[...] += g

        for ref, (g_ctx, g_lat) in zip(dmr, g_mods):
            ref[0] += g_ctx
            ref[1] += g_lat

    in_specs = ([_row_spec(R, W, cb) for (_, W, cb, _) in rows + consts]
                + [_full_spec(a.shape) for (a, _) in params + cparams]
                + [_full_spec(m.shape) for m in mods]
                + [_row_spec(R, W, 0) for (_, W) in douts]
                + [_row_spec(R, rows[k][1], 0) for k in add_ix])
    args = ([a for (a, _, _, _) in rows + consts] + [a for (a, _) in params + cparams] + list(mods)
            + [a for (a, _) in douts] + [adds[k] for k in add_ix])
    out_specs = ([_row_spec(R, rows[k][1], 0) for k in out_ix]
                 + [_full_spec(a.shape) for (a, _) in params]
                 + [_full_spec(m.shape) for m in mods])
    out_shape = ([jax.ShapeDtypeStruct((n, rows[k][1]), drow[k]) for k in out_ix]
                 + [jax.ShapeDtypeStruct(a.shape, F32) for (a, _) in params]
                 + [jax.ShapeDtypeStruct(m.shape, F32) for m in mods])
    res = pl.pallas_call(
        body, name=name, grid=(n // R,), in_specs=in_specs, out_specs=out_specs, out_shape=out_shape,
        compiler_params=_ARB1,
    )(*args)
    no = len(out_ix)
    return list(res[:no]), list(res[no:no + npar]), list(res[no + npar:])


def _rms(x, g):
    return x * lax.rsqrt(jnp.mean(x * x, axis=-1, keepdims=True) + EPS) * g


def _f_lnmod(rv, pv, mv, cv, cpv):
    (x,), (g,), (shift, scale) = rv, pv, mv
    return (_rms(x, g) * (1.0 + scale) + shift,)


def _f_resid(rv, pv, mv, cv, cpv):
    (x, y), (gate,) = rv, mv
    return (x + gate * y,)


def _f_merge(rv, pv, mv, cv, cpv):
    ga, gb, gc, ya, yb, yc = rv
    return (_sigmoid(ga) * ya + _sigmoid(gb) * yb + _sigmoid(gc) * yc,)


def _split3(x):
    hi = x.astype(BF16)
    rest = x - hi.astype(F32)
    mid = rest.astype(BF16)
    return hi, mid, (rest - mid.astype(F32)).astype(BF16)


def _dot3_right(x, m):
    mb = m.astype(BF16)
    return sum(lax.dot_general(piece, mb, _NN, preferred_element_type=F32) for piece in _split3(x))


@jax.custom_vjp
def _sym_dot(x, m):
    return _dot3_right(x, m)


_sym_dot.defvjp(lambda x, m: (_dot3_right(x, m), m), lambda m, g: (_dot3_right(g, m), jnp.zeros_like(m)))


def _f_qknorm(rv, pv, mv, cv, cpv):
    (q, k), (gq, gk), (bdq, eq, bdk, ek) = rv, pv, cpv
    qn = q * lax.rsqrt(_sym_dot(q * q, bdq) + EPS) * _hdot(gq, eq)
    kn = k * lax.rsqrt(_sym_dot(k * k, bdk) + EPS) * _hdot(gk, ek)
    return (qn, kn)


def _rope(x, cos, sin):
    w = x.shape[1]
    lane = lax.broadcasted_iota(jnp.int32, x.shape, 1)
    partner = jnp.where((lane & 31) < 16, pltpu.roll(x, w - 16, 1), pltpu.roll(x, 16, 1))
    return x * cos + partner * sin


_LOG2E = 1.4426950408889634
_LN2 = 0.6931471805599453


def _qk_post(res, cv):
    (qn, kn), (cq, sq, ck, sk) = res, cv
    return (_rope(qn, cq, sq) * (HEAD_DIM ** -0.5 * _LOG2E), _rope(kn, ck, sk))


def _qk_pre(dv, cv):
    (dq, dk), (cq, sq, ck, sk) = dv, cv
    return (_rope(dq * (HEAD_DIM ** -0.5), cq, -sq), _rope(dk * _LN2, ck, -sk))


def _f_gmlp(rv, pv, mv, cv, cpv):
    (u, vs), (ng, ws, bt) = rv, pv
    pieces = []
    for g in range(GMLP_GROUPS):
        vn = _rms(jax.nn.gelu(vs[g]), ng[g])
        pieces.append(_bdot(ws[g], vn, "nn") + bt[g])
    return (jax.nn.gelu(u) * jnp.concatenate(pieces, axis=1),)


def _f_glaout(rv, pv, mv, cv, cpv):
    (ofs, obs, rs), (gn,) = rv, pv
    pieces = [_rms(ofs[h] + obs[h], gn[h]) * (rs[h] * _sigmoid(rs[h])) for h in range(GLA_HEADS)]
    return (jnp.concatenate(pieces, axis=1),)


_CONV_CB = 1408
_CONV_STRIP = 128


def _sigmoid(x):
    return 0.5 * jnp.tanh(0.5 * x) + 0.5


def _halo_keep(i, R, tc, n):
    first, end = i * R, (i + 1) * R
    keep_prev = jnp.where((first == 0) | (first == tc), 0.0, 1.0)
    keep_next = jnp.where((end == tc) | (end == n), 0.0, 1.0)
    return keep_prev, keep_next


def _conv_specs(R, n):
    nb8 = n // 8
    main = pl.BlockSpec((2, R, _CONV_CB), lambda j, i: (0, i, j))
    prev = pl.BlockSpec((2, 8, _CONV_CB), lambda j, i: (0, jnp.maximum(i * (R // 8) - 1, 0), j))
    nxt = pl.BlockSpec((2, 8, _CONV_CB), lambda j, i: (0, jnp.minimum((i + 1) * (R // 8), nb8 - 1), j))
    cw = pl.BlockSpec((2, 3, _CONV_CB), lambda j, i: (0, 0, j))
    cb = pl.BlockSpec((2, 1, _CONV_CB), lambda j, i: (0, 0, j))
    return main, prev, nxt, cw, cb


def _row_before(x):
    return pltpu.roll(x, 1, 0)


def _row_after(x):
    return pltpu.roll(x, x.shape[0] - 1, 0)


def _conv_gate(name, a2, cw, cb, R, tc):
    n = a2.shape[1]
    main, prev, nxt, cws, cbs = _conv_specs(R, n)

    def body(a_ref, p_ref, n_ref, cw_ref, cb_ref, o_ref):
        keep_prev, keep_next = _halo_keep(pl.program_id(1), R, tc, n)

        def strip(c, carry):
            ls = pl.ds(pl.multiple_of(c * _CONV_STRIP, _CONV_STRIP), _CONV_STRIP)
            acts = []
            for h in range(2):
                win = jnp.concatenate([p_ref[h, :, ls] * keep_prev, a_ref[h, :, ls], n_ref[h, :, ls] * keep_next], axis=0)
                acts.append((cb_ref[h, :, ls] + cw_ref[h, 1:2, ls] * win + cw_ref[h, 0:1, ls] * _row_before(win)
                             + cw_ref[h, 2:3, ls] * _row_after(win))[8:8 + R])
            g, v = acts
            o_ref[:, ls] = (g * _sigmoid(g) * v).astype(o_ref.dtype)
            return carry

        lax.fori_loop(0, _CONV_CB // _CONV_STRIP, strip, 0)

    return pl.pallas_call(
        body, name=name, grid=(FFN_H // _CONV_CB, n // R),
        in_specs=[main, prev, nxt, cws, cbs],
        out_specs=pl.BlockSpec((R, _CONV_CB), lambda j, i: (i, j)),
        out_shape=jax.ShapeDtypeStruct((n, FFN_H), BF16),
        compiler_params=pltpu.CompilerParams(dimension_semantics=("arbitrary", "arbitrary"),
                                             vmem_limit_bytes=V7X_VMEM_LIMIT),
    )(a2, a2, a2, cw, cb)


def _conv_gate_bwd(name, a2, dgv, cw, cb, R, tc):
    n = a2.shape[1]
    nb8 = n // 8
    main, prev, nxt, cws, cbs = _conv_specs(R, n)
    d_main = pl.BlockSpec((R, _CONV_CB), lambda j, i: (i, j))
    d_prev = pl.BlockSpec((8, _CONV_CB), lambda j, i: (jnp.maximum(i * (R // 8) - 1, 0), j))
    d_next = pl.BlockSpec((8, _CONV_CB), lambda j, i: (jnp.minimum((i + 1) * (R // 8), nb8 - 1), j))
    mid = slice(8, 8 + R)

    def body(a_ref, p_ref, n_ref, cw_ref, cb_ref, d_ref, dp_ref, dn_ref, da_ref, dcw_ref, dcb_ref):
        i = pl.program_id(1)
        keep_prev, keep_next = _halo_keep(i, R, tc, n)

        @pl.when(i == 0)
        def _():
            dcw_ref[...] = jnp.zeros_like(dcw_ref)
            dcb_ref[...] = jnp.zeros_like(dcb_ref)

        def strip(c, carry):
            ls = pl.ds(pl.multiple_of(c * _CONV_STRIP, _CONV_STRIP), _CONV_STRIP)
            cws_ = [[cw_ref[h, k:k + 1, ls] for k in range(3)] for h in range(2)]
            wins = [jnp.concatenate([p_ref[h, :, ls] * keep_prev, a_ref[h, :, ls], n_ref[h, :, ls] * keep_next], axis=0)
                    for h in range(2)]
            g, v = [cb_ref[h, :, ls] + cws_[h][1] * wins[h] + cws_[h][0] * _row_before(wins[h]) + cws_[h][2] * _row_after(wins[h])
                    for h in range(2)]
            dout = jnp.concatenate([dp_ref[:, ls].astype(F32) * keep_prev, d_ref[:, ls].astype(F32),
                                    dn_ref[:, ls].astype(F32) * keep_next], axis=0)
            sg = _sigmoid(g)
            das = [dout * v * sg * (1.0 + g * (1.0 - sg)), dout * g * sg]
            for h in range(2):
                da_win = das[h]
                da_ref[h, :, ls] = (cws_[h][1] * da_win + cws_[h][0] * _row_after(da_win)
                                    + cws_[h][2] * _row_before(da_win))[mid].astype(da_ref.dtype)
                da = da_win[mid]
                dcw_ref[h, 0:1, ls] += jnp.sum(da * _row_before(wins[h])[mid], axis=0, keepdims=True)
                dcw_ref[h, 1:2, ls] += jnp.sum(da * wins[h][mid], axis=0, keepdims=True)
                dcw_ref[h, 2:3, ls] += jnp.sum(da * _row_after(wins[h])[mid], axis=0, keepdims=True)
                dcb_ref[h, :, ls] += jnp.sum(da, axis=0, keepdims=True)
            return carry

        lax.fori_loop(0, _CONV_CB // _CONV_STRIP, strip, 0)

    return pl.pallas_call(
        body, name=name, grid=(FFN_H // _CONV_CB, n // R),
        in_specs=[main, prev, nxt, cws, cbs, d_main, d_prev, d_next],
        out_specs=[main, cws, cbs],
        out_shape=[jax.ShapeDtypeStruct((2, n, FFN_H), BF16), jax.ShapeDtypeStruct((2, 3, FFN_H), F32),
                   jax.ShapeDtypeStruct((2, 1, FFN_H), F32)],
        compiler_params=pltpu.CompilerParams(dimension_semantics=("arbitrary", "arbitrary"),
                                             vmem_limit_bytes=V7X_VMEM_LIMIT),
    )(a2, a2, a2, cw, cb, dgv, dgv, dgv)


_ARB2 = pltpu.CompilerParams(dimension_semantics=("arbitrary", "arbitrary"), vmem_limit_bytes=V7X_VMEM_LIMIT)


def _with_exchange(body, n_in, n_out, grid, comm):
    if comm is None:
        return body, [], [], []
    xs, gather = comm
    na = len(xs)
    specs, shapes, sems = _exchange_io(xs, gather)

    def wrapped(*refs):
        ins, x_refs = refs[:n_in], refs[n_in:n_in + na]
        outs, o_refs = refs[n_in + na:n_in + na + n_out], refs[n_in + na + n_out:n_in + 2 * na + n_out]
        scratch, sem_refs = refs[n_in + 2 * na + n_out:-3], refs[-3:]
        start, wait = _exchange_plan(x_refs, o_refs, *sem_refs, gather)
        ids = [pl.program_id(d) for d in range(len(grid))]
        first, last = ids[0] == 0, ids[0] == grid[0] - 1
        for d in range(1, len(grid)):
            first, last = first & (ids[d] == 0), last & (ids[d] == grid[d] - 1)

        @pl.when(first)
        def _():
            start()

        body(*ins, *outs, *scratch)

        @pl.when(last)
        def _():
            wait()

    return wrapped, specs, shapes, sems


def _attn_fwd(name, qt8, k2, vt2, comm=None):
    nq, nk = qt8.shape[2], k2.shape[1]
    tq = _pick(nq, (256, 128))
    tk = _pick(nk, (1408, 768, 256, 128))

    va = vt2.shape[1]

    def body(qt_ref, k_ref, vt_ref, ot_ref, lse_ref, m_ref, acc_ref):
        m_ref[...] = jnp.full((Q_GROUP, 1, tq), -1e30, F32)
        acc_ref[...] = jnp.zeros((Q_GROUP, va, tq), F32)

        def step(j, carry):
            sl = pl.ds(pl.multiple_of(j * tk, tk), tk)
            kj = k_ref[sl, :]
            vtj = vt_ref[:, sl]
            sts = [lax.dot_general(kj, qt_ref[h], _NN, preferred_element_type=F32) for h in range(Q_GROUP)]
            m_old = [m_ref[h] for h in range(Q_GROUP)]
            acc_old = [acc_ref[h] for h in range(Q_GROUP)]
            m_new = [jnp.maximum(m_old[h], jnp.max(sts[h], axis=0, keepdims=True)) for h in range(Q_GROUP)]
            pts = [jnp.exp2(sts[h] - m_new[h]).astype(BF16) for h in range(Q_GROUP)]
            pvs = [lax.dot_general(vtj, pts[h], _NN, preferred_element_type=F32) for h in range(Q_GROUP)]
            for h in range(Q_GROUP):
                acc_ref[h] = jnp.exp2(m_old[h] - m_new[h]) * acc_old[h] + pvs[h]
                m_ref[h] = m_new[h]
            return carry

        lax.fori_loop(0, nk // tk, step, 0, unroll=3)
        for h in range(Q_GROUP):
            l = acc_ref[h, HEAD_DIM:HEAD_DIM + 1, :]
            ot_ref[h] = (acc_ref[h, 0:HEAD_DIM, :] / l).astype(ot_ref.dtype)
            lse_ref[h] = m_ref[h] + jnp.log2(l)

    qspec = pl.BlockSpec((Q_GROUP, HEAD_DIM, tq), lambda g, i: (g, 0, i))
    lspec = pl.BlockSpec((Q_GROUP, 1, tq), lambda g, i: (g, 0, i))
    grid = (N_KV_HEADS, nq // tq)
    body, xspecs, xshapes, xsems = _with_exchange(body, 3, 2, grid, comm)
    res = pl.pallas_call(
        body, name=name, grid=grid,
        in_specs=[qspec, pl.BlockSpec((None, nk, HEAD_DIM), lambda g, i: (g, 0, 0)),
                  pl.BlockSpec((None, va, nk), lambda g, i: (g, 0, 0))] + xspecs,
        out_specs=[qspec, lspec] + xspecs,
        out_shape=[jax.ShapeDtypeStruct((N_Q_HEADS, HEAD_DIM, nq), BF16), jax.ShapeDtypeStruct((N_Q_HEADS, 1, nq), F32)] + xshapes,
        scratch_shapes=[pltpu.VMEM((Q_GROUP, 1, tq), F32), pltpu.VMEM((Q_GROUP, va, tq), F32)] + xsems,
        compiler_params=_ARB2,
    )(qt8, k2, vt2, *(comm[0] if comm else []))
    return res[0], res[1], list(res[2:])


def _attn_bwd(name, qt8, k2, v2, kt2, ot8, dot8, lse, dkt0, dvt0, comm=None):
    nq, nk = qt8.shape[2], k2.shape[1]
    n0 = dkt0.shape[2]
    tq = _pick(nq, (256, 128))
    tk = _pick(nk, (1408, 768, 256, 128))
    ts = tk
    heads = range(Q_GROUP)

    def body(qt_ref, k_ref, v_ref, kt_ref, ot_ref, dot_ref, lse_ref, dk0_ref, dv0_ref, dqt_ref, dkt_ref, dvt_ref, dl_ref, dq_acc):
        @pl.when(pl.program_id(1) == 0)
        def _():
            dkt_ref[...] = jnp.zeros_like(dkt_ref)
            dvt_ref[...] = jnp.zeros_like(dvt_ref)
            dkt_ref[:, 0:n0] = dk0_ref[...]
            dvt_ref[:, 0:n0] = dv0_ref[...]

        for h in heads:
            dl_ref[h] = jnp.sum(dot_ref[h].astype(F32) * ot_ref[h].astype(F32), axis=0, keepdims=True)
        dq_acc[...] = jnp.zeros((Q_GROUP, HEAD_DIM, tq), F32)

        items = [(s, h) for s in range(tk // ts) for h in heads]

        def step(j, carry):
            def keys(s):
                return pl.ds(pl.multiple_of(j * tk + s * ts, ts), ts)

            def scores(item):
                s, h = item
                return (lax.dot_general(k_ref[keys(s), :], qt_ref[h], _NN, preferred_element_type=F32),
                        lax.dot_general(v_ref[keys(s), :], dot_ref[h], _NN, preferred_element_type=F32))

            nxt = scores(items[0])
            for n, (s, h) in enumerate(items):
                st, dpt = nxt
                if n + 1 < len(items):
                    nxt = scores(items[n + 1])
                pt = jnp.exp2(st - lse_ref[h])
                dst = (pt * (dpt - dl_ref[h])).astype(BF16)
                dq_acc[h] += lax.dot_general(kt_ref[:, keys(s)], dst, _NN, preferred_element_type=F32)
                dv_h = lax.dot_general(dot_ref[h], pt.astype(BF16), _NT, preferred_element_type=F32)
                dk_h = lax.dot_general(qt_ref[h], dst, _NT, preferred_element_type=F32)
                dvt_s, dkt_s = (dv_h, dk_h) if h == 0 else (dvt_s + dv_h, dkt_s + dk_h)
                if h == Q_GROUP - 1:
                    dvt_ref[:, keys(s)] += dvt_s
                    dkt_ref[:, keys(s)] += dkt_s
            return carry

        lax.fori_loop(0, nk // tk, step, 0)
        dqt_ref[...] = dq_acc[...]

    tspec = pl.BlockSpec((Q_GROUP, HEAD_DIM, tq), lambda g, i: (g, 0, i))
    lspec = pl.BlockSpec((Q_GROUP, 1, tq), lambda g, i: (g, 0, i))
    kspec = pl.BlockSpec((None, nk, HEAD_DIM), lambda g, i: (g, 0, 0))
    ktspec = pl.BlockSpec((None, HEAD_DIM, nk), lambda g, i: (g, 0, 0))
    k0spec = pl.BlockSpec((None, HEAD_DIM, n0), lambda g, i: (g, 0, 0))
    grid = (N_KV_HEADS, nq // tq)
    body, xspecs, xshapes, xsems = _with_exchange(body, 9, 3, grid, comm)
    res = pl.pallas_call(
        body, name=name, grid=grid,
        in_specs=[tspec, kspec, kspec, ktspec, tspec, tspec, lspec, k0spec, k0spec] + xspecs,
        out_specs=[tspec, ktspec, ktspec] + xspecs,
        out_shape=[jax.ShapeDtypeStruct((N_Q_HEADS, HEAD_DIM, nq), F32), jax.ShapeDtypeStruct((N_KV_HEADS, HEAD_DIM, nk), F32),
                   jax.ShapeDtypeStruct((N_KV_HEADS, HEAD_DIM, nk), F32)] + xshapes,
        scratch_shapes=[pltpu.VMEM((Q_GROUP, 1, tq), F32), pltpu.VMEM((Q_GROUP, HEAD_DIM, tq), F32)] + xsems,
        compiler_params=_ARB2,
    )(qt8, k2, v2, kt2, ot8, dot8, lse, dkt0, dvt0, *(comm[0] if comm else []))
    return res[0], res[1], res[2], list(res[3:])


def _log_sigmoid(z):
    return jnp.minimum(z, 0.0) - jnp.log(1.0 + jnp.exp(-jnp.abs(z)))


_BDOT_DIMS = {"nn": _NN, "nt": _NT, "tn": _TN}
_BDOT_BWD = {"nn": (("nt", "gb"), ("tn", "ag")), "nt": (("nn", "gb"), ("tn", "ga")), "tn": (("nt", "bg"), ("nn", "ag"))}


def _bdot_raw(a, b, mode):
    return lax.dot_general(a.astype(BF16), b.astype(BF16), _BDOT_DIMS[mode], preferred_element_type=F32)


@functools.partial(jax.custom_vjp, nondiff_argnums=(2,))
def _bdot(a, b, mode):
    return _bdot_raw(a, b, mode)


def _bdot_fwd(a, b, mode):
    return _bdot_raw(a, b, mode), (a.astype(BF16), b.astype(BF16))


def _bdot_bwd(mode, res, g):
    ops = {"a": res[0], "b": res[1], "g": g}
    (ma, oa), (mb, ob) = _BDOT_BWD[mode]
    return _bdot_raw(ops[oa[0]], ops[oa[1]], ma), _bdot_raw(ops[ob[0]], ops[ob[1]], mb)


_bdot.defvjp(_bdot_fwd, _bdot_bwd)


def _tile_tri(rev, rows):
    r_i = lax.broadcasted_iota(jnp.int32, (rows, rows), 0)
    c_i = lax.broadcasted_iota(jnp.int32, (rows, rows), 1)
    same = (r_i // GLA_CHUNK) == (c_i // GLA_CHUNK)
    return same & ((c_i >= r_i) if rev else (c_i <= r_i))


def _tri_dot(rev, x):
    tri = _tile_tri(rev, x.shape[0]).astype(BF16)
    return sum(lax.dot_general(tri, piece, _NN, preferred_element_type=F32) for piece in _split3(x))


@functools.partial(jax.custom_vjp, nondiff_argnums=(1,))
def _chunk_cumsum(x, rev):
    return _tri_dot(rev, x)


_chunk_cumsum.defvjp(lambda x, rev: (_tri_dot(rev, x), None), lambda rev, _, g: (_tri_dot(not rev, g),))


_GLA_REV = (False, True)


def _gla_tile_pair(qs, ks, vss, as_, w2s, b2s, states):
    both = range(2)
    rows = qs[0].shape[0]
    nch = rows // GLA_CHUNK
    tris = [_tile_tri(_GLA_REV[d], rows) for d in both]
    chunk_of_row = lax.broadcasted_iota(jnp.int32, (rows, 1), 0) // GLA_CHUNK
    in_chunk = [(chunk_of_row == c).astype(F32) for c in range(nch)]
    las = [_log_sigmoid(_bdot(as_[d], w2s[d], "nn") + b2s[d]) * (1.0 / GLA_TAU) for d in both]
    cums = [_chunk_cumsum(las[d], _GLA_REV[d]) for d in both]
    tots = [[jnp.sum(las[d] * in_chunk[c], axis=0, keepdims=True) for c in range(nch)] for d in both]
    tot_rows = [sum(in_chunk[c] * tots[d][c] for c in range(nch)) for d in both]
    q_in = [qs[d] * (GLA_DK ** -0.5) * jnp.exp(cums[d]) for d in both]
    k_in = [ks[d] * jnp.exp(-cums[d]) for d in both]
    k_st = [ks[d] * jnp.exp(tot_rows[d] - cums[d]) for d in both]
    lane = lax.broadcasted_iota(jnp.int32, (1, GLA_QK_W), 1)
    outs = [[], []]
    for h in range(GLA_HEADS):
        head = ((lane >= GLA_DK * h) & (lane < GLA_DK * (h + 1))).astype(F32)
        atts = [jnp.where(tris[d], _bdot(q_in[d] * head, k_in[d], "nt"), 0.0) for d in both]
        for d in both:
            outs[d].append(_bdot(atts[d], vss[d][h], "nn"))
    os_ = [jnp.concatenate(outs[d], axis=1) for d in both]
    hr = lax.broadcasted_iota(jnp.int32, (GLA_V_W, GLA_QK_W), 0) // GLA_DV
    hc = lax.broadcasted_iota(jnp.int32, (GLA_V_W, GLA_QK_W), 1) // GLA_DK
    same_head = (hr == hc).astype(F32)
    v_all = [jnp.concatenate(vss[d], axis=1) for d in both]
    states = list(states)
    for step in range(nch):
        chunk = (step, nch - 1 - step)
        us = [_bdot(v_all[d], k_st[d] * in_chunk[chunk[d]], "tn") * same_head for d in both]
        for d in both:
            os_[d] = os_[d] + _bdot(q_in[d] * in_chunk[chunk[d]], states[d], "nt")
            states[d] = jnp.exp(tots[d][chunk[d]]) * states[d] + us[d]
    return os_, states


def _gla_tile_of(step, rev, nct, nt):
    if not rev:
        return step
    return jnp.where(step < nct, nct - 1 - step, nt - 1 - (step - nct))


def _gla_row_specs(R, tile):
    return [pl.BlockSpec((R, GLA_QK_W), lambda s: (tile(s), C_GQ // GLA_QK_W)),
            pl.BlockSpec((R, GLA_QK_W), lambda s: (tile(s), C_GK // GLA_QK_W)),
            pl.BlockSpec((R, GLA_V_W), lambda s: (tile(s), C_GV // GLA_V_W)),
            pl.BlockSpec((R, 128), lambda s: (tile(s), C_AF // 128))]


_GLA_WIDTHS = (GLA_QK_W, GLA_QK_W, GLA_V_W, 128)


def _gla_load(refs):
    q_ref, k_ref, v_ref, a_ref = refs
    return q_ref[...], k_ref[...], [v_ref[:, GLA_DV * h:GLA_DV * (h + 1)] for h in range(GLA_HEADS)], a_ref[...]


def _gla_fwd(name, p, w2s, b2s, R, nct):
    n = p.shape[0]
    nt = n // R
    tiles = [lambda s, d=d: _gla_tile_of(s, _GLA_REV[d], nct, nt) for d in range(2)]

    def body(*refs):
        rows, (w2f, b2f, w2b, b2b), (of, ob, sf, sb), states = refs[:8], refs[8:12], refs[12:16], refs[16:]

        @pl.when(pl.program_id(0) == 0)
        def _():
            for st in states:
                st[...] = jnp.zeros_like(st)

        ins = [_gla_load(rows[:4]), _gla_load(rows[4:])]
        s_in = [states[0][...], states[1][...]]
        sf[...], sb[...] = s_in
        os_, s_out = _gla_tile_pair([i[0] for i in ins], [i[1] for i in ins], [i[2] for i in ins], [i[3] for i in ins],
                                    [w2f[...], w2b[...]], [b2f[...], b2b[...]], s_in)
        of[...], ob[...] = os_
        states[0][...], states[1][...] = s_out

    in_specs = (_gla_row_specs(R, tiles[0]) + _gla_row_specs(R, tiles[1])
                + [_full_spec(w2s[0].shape), _full_spec(b2s[0].shape), _full_spec(w2s[1].shape), _full_spec(b2s[1].shape)])
    o_specs = [pl.BlockSpec((R, GLA_V_W), lambda s, t=t: (t(s), 0)) for t in tiles]
    s_specs = [pl.BlockSpec((None, GLA_V_W, GLA_QK_W), lambda s, t=t: (t(s), 0, 0)) for t in tiles]
    return pl.pallas_call(
        body, name=name, grid=(nt,), in_specs=in_specs, out_specs=o_specs + s_specs,
        out_shape=[jax.ShapeDtypeStruct((n, GLA_V_W), F32)] * 2 + [jax.ShapeDtypeStruct((nt, GLA_V_W, GLA_QK_W), F32)] * 2,
        scratch_shapes=[pltpu.VMEM((GLA_V_W, GLA_QK_W), F32)] * 2,
        compiler_params=_ARB1,
    )(*([p] * 8), w2s[0], b2s[0], w2s[1], b2s[1])


def _gla_bwd(name, p, w2s, b2s, ssaves, do, R, nct):
    n = p.shape[0]
    nt = n // R
    tiles = [lambda s, d=d: _gla_tile_of(nt - 1 - s, _GLA_REV[d], nct, nt) for d in range(2)]

    def body(*refs):
        rows, (w2f, b2f, w2b, b2b), ss, dos = refs[:8], refs[8:12], refs[12:14], refs[14:16]
        d_rows, (dw2f, db2f, dw2b, db2b), dstates = refs[16:24], refs[24:28], refs[28:]

        @pl.when(pl.program_id(0) == 0)
        def _():
            for ref in (dw2f, db2f, dw2b, db2b) + tuple(dstates):
                ref[...] = jnp.zeros_like(ref)

        ins = [_gla_load(rows[:4]), _gla_load(rows[4:])]
        _, vjp = jax.vjp(_gla_tile_pair, [i[0] for i in ins], [i[1] for i in ins], [i[2] for i in ins], [i[3] for i in ins],
                         [w2f[...], w2b[...]], [b2f[...], b2b[...]], [ss[0][...], ss[1][...]])
        dqs, dks, dvss, das, dw2, db2, ds = vjp(([dos[0][...], dos[1][...]], [dstates[0][...], dstates[1][...]]))
        for d in range(2):
            grads = [dqs[d], dks[d], jnp.concatenate(dvss[d], axis=1), das[d]]
            for ref, g in zip(d_rows[4 * d:4 * d + 4], grads):
                ref[...] = g
            dstates[d][...] = ds[d]
        dw2f[...] += dw2[0]
        db2f[...] += db2[0]
        dw2b[...] += dw2[1]
        db2b[...] += db2[1]

    par_specs = [_full_spec(w2s[0].shape), _full_spec(b2s[0].shape), _full_spec(w2s[1].shape), _full_spec(b2s[1].shape)]
    d_specs = [pl.BlockSpec((R, w), lambda s, t=t: (t(s), 0)) for t in tiles for w in _GLA_WIDTHS]
    in_specs = (_gla_row_specs(R, tiles[0]) + _gla_row_specs(R, tiles[1]) + par_specs
                + [pl.BlockSpec((None, GLA_V_W, GLA_QK_W), lambda s, t=t: (t(s), 0, 0)) for t in tiles]
                + [pl.BlockSpec((R, GLA_V_W), lambda s, t=t: (t(s), 0)) for t in tiles])
    res = pl.pallas_call(
        body, name=name, grid=(nt,), in_specs=in_specs, out_specs=d_specs + par_specs,
        out_shape=[jax.ShapeDtypeStruct((n, w), F32) for _ in range(2) for w in _GLA_WIDTHS]
        + [jax.ShapeDtypeStruct(a.shape, F32) for a in (w2s[0], b2s[0], w2s[1], b2s[1])],
        scratch_shapes=[pltpu.VMEM((GLA_V_W, GLA_QK_W), F32)] * 2,
        compiler_params=_ARB1,
    )(*([p] * 8), w2s[0], b2s[0], w2s[1], b2s[1], ssaves[0], ssaves[1], do, do)
    return res[:4], res[4:8], res[8:]


def _final_loss(name, x, target, gf, R, nct):
    n = x.shape[0]

    def body(x_ref, t_ref, g_ref, loss_ref, dx_ref, dg_ref):
        i = pl.program_id(0)

        @pl.when(i == 0)
        def _():
            loss_ref[...] = jnp.zeros_like(loss_ref)
            dg_ref[...] = jnp.zeros_like(dg_ref)

        @pl.when(i < nct)
        def _():
            dx_ref[...] = jnp.zeros_like(dx_ref)

        @pl.when(i >= nct)
        def _():
            y, vjp = jax.vjp(_rms, x_ref[...], g_ref[...])
            err = y - t_ref[...]
            loss_ref[...] += jnp.sum(0.5 * jnp.mean(err * err, axis=-1, keepdims=True))
            dx, dg = vjp(err * (1.0 / D))
            dx_ref[...] = dx
            dg_ref[...] += dg

    return pl.pallas_call(
        body, name=name, grid=(n // R,),
        in_specs=[_row_spec(R, D, 0), pl.BlockSpec((R, D), lambda i: (jnp.maximum(i - nct, 0), 0)), _full_spec((1, D))],
        out_specs=[_full_spec((8, 128)), _row_spec(R, D, 0), _full_spec((1, D))],
        out_shape=[jax.ShapeDtypeStruct((8, 128), F32), jax.ShapeDtypeStruct((n, D), F32), jax.ShapeDtypeStruct((1, D), F32)],
        compiler_params=_ARB1,
    )(x, target, gf)


def _adamw(name, w, m, v, gparts):
    rows, cols = w.shape
    nparts = gparts.shape[0]
    tr = rows
    for cand in range(min(rows, 256), 15, -16):
        if rows % cand == 0:
            tr = cand
            break

    def body(w_ref, m_ref, v_ref, g_ref, go_ref, d_ref, mo_ref, vo_ref):
        g = g_ref[0].astype(F32)
        for k in range(1, nparts):
            g = g + g_ref[k].astype(F32)
        m_new = ADAM_B1 * m_ref[...] + (1.0 - ADAM_B1) * g
        v_new = ADAM_B2 * v_ref[...] + (1.0 - ADAM_B2) * (g * g)
        m_hat = m_new / (1.0 - ADAM_B1 ** ADAM_STEP)
        v_hat = v_new / (1.0 - ADAM_B2 ** ADAM_STEP)
        go_ref[...] = g
        d_ref[...] = -ADAM_LR * (m_hat / (jnp.sqrt(v_hat) + ADAM_EPS) + ADAM_WD * w_ref[...])
        mo_ref[...] = m_new
        vo_ref[...] = v_new

    spec = pl.BlockSpec((tr, cols), lambda i: (i, 0))
    return pl.pallas_call(
        body, name=name, grid=(rows // tr,),
        in_specs=[spec, spec, spec, pl.BlockSpec((nparts, tr, cols), lambda i: (0, i, 0))],
        out_specs=[spec] * 4, out_shape=[jax.ShapeDtypeStruct((rows, cols), F32)] * 4,
        compiler_params=_ARB1,
    )(w, m, v, gparts)


def _my_index():
    return 4 * lax.axis_index("x") + 2 * lax.axis_index("y") + lax.axis_index("c")


def _xor_peer(k):
    flip = lambda a, bit: (1 - a) if bit else a
    pos = (flip(lax.axis_index("x"), (k >> 2) & 1), flip(lax.axis_index("y"), (k >> 1) & 1), flip(lax.axis_index("c"), k & 1))
    return pos, 4 * pos[0] + 2 * pos[1] + pos[2]


def _exchange_plan(x_refs, o_refs, send_sems, recv_sems, local_sems, gather):
    npeer = N_DEV - 1
    me = _my_index()
    locals_, sends, recvs = [], [], []
    for a, (x_ref, o_ref) in enumerate(zip(x_refs, o_refs)):
        mine = x_ref if gather else x_ref.at[me]
        locals_.append(pltpu.make_async_copy(mine, o_ref.at[me], local_sems.at[a]))
        for k in range(1, N_DEV):
            pos, lin = _xor_peer(k)
            src = x_ref if gather else x_ref.at[lin]
            sem = a * npeer + k - 1
            sends.append(pltpu.make_async_remote_copy(src_ref=src, dst_ref=o_ref.at[me], send_sem=send_sems.at[sem],
                                                      recv_sem=recv_sems.at[sem], device_id=pos, device_id_type=MESH_ID))
            recvs.append(pltpu.make_async_remote_copy(src_ref=src, dst_ref=o_ref.at[lin], send_sem=send_sems.at[sem],
                                                      recv_sem=recv_sems.at[sem], device_id=pos, device_id_type=MESH_ID))

    def start():
        for cp in locals_ + sends:
            cp.start()

    def wait():
        for cp in recvs:
            cp.wait_recv()
        for cp in sends:
            cp.wait_send()
        for cp in locals_:
            cp.wait()

    return start, wait


def _exchange_io(xs, gather):
    na = len(xs)
    hbm = pl.BlockSpec(memory_space=pltpu.HBM)
    shapes = [jax.ShapeDtypeStruct((N_DEV,) + tuple(x.shape if gather else x.shape[1:]), x.dtype) for x in xs]
    sems = [pltpu.SemaphoreType.DMA((na * (N_DEV - 1),)), pltpu.SemaphoreType.DMA((na * (N_DEV - 1),)),
            pltpu.SemaphoreType.DMA((na,))]
    return [hbm] * na, shapes, sems


def _exchange(name, xs, *, gather):
    na = len(xs)
    specs, shapes, sems = _exchange_io(xs, gather)

    def body(*refs):
        start, wait = _exchange_plan(refs[:na], refs[na:2 * na], *refs[2 * na:], gather)
        start()
        wait()

    return list(pl.pallas_call(body, name=name, in_specs=specs, out_specs=specs, out_shape=shapes, scratch_shapes=sems)(*xs))


def _adaln_fwd(name, craw16, w_ada, b_cols):
    def body(c_ref, w_ref, b_ref, o_ref):
        cs = jax.nn.silu(c_ref[...]).astype(BF16)
        for l in range(2):
            o_ref[l] = lax.dot_general(cs, w_ref[l].astype(BF16), _NN, preferred_element_type=F32) + b_ref[l]

    return pl.pallas_call(
        body, name=name, out_shape=jax.ShapeDtypeStruct((2, 16, w_ada.shape[2]), F32),
        compiler_params=pltpu.CompilerParams(vmem_limit_bytes=V7X_VMEM_LIMIT),
    )(craw16, w_ada, b_cols)


def _adaln_bwd(name, craw16, w_ada, dm):
    def body(c_ref, w_ref, dm_ref, gw_ref, dc_ref):
        c = c_ref[...]
        sg = jax.nn.sigmoid(c)
        cs = c * sg
        row = lax.broadcasted_iota(jnp.int32, (8, 1), 0)
        dc = jnp.zeros((16, D), F32)
        for l in range(2):
            dmx = dm_ref[2 * l + 1]
            dmc = jnp.where(row == 0, jnp.sum(dm_ref[2 * l], axis=0, keepdims=True), 0.0)
            gw_ref[l] = _hdot(cs[0:8], dmx, _TN) + _hdot(cs[8:16], dmc, _TN)
            dc = dc + _hdot(jnp.concatenate([dmx, dmc], axis=0), w_ref[l], _NT)
        dc_ref[...] = dc * sg * (1.0 + c * (1.0 - sg))

    return pl.pallas_call(
        body, name=name,
        out_shape=[jax.ShapeDtypeStruct(w_ada.shape, F32), jax.ShapeDtypeStruct((16, D), F32)],
        compiler_params=pltpu.CompilerParams(vmem_limit_bytes=V7X_VMEM_LIMIT),
    )(craw16, w_ada, dm)


_IN_OFFS = [sum(IN_SPLITS[:k]) for k in range(len(IN_SPLITS) + 1)]
_MY_ORDER = (11, 12, 13, 0, 1, 2, 7, 10, 5, 6, 3, 4, 8, 9)


_IN_SHARD = IN_WIDTH // N_DEV


def _win_my_cols(pieces):
    parts = []
    for k in _MY_ORDER:
        a, b = _IN_OFFS[k], _IN_OFFS[k + 1]
        for s in range(a // _IN_SHARD, (b - 1) // _IN_SHARD + 1):
            lo, hi = max(a, s * _IN_SHARD), min(b, (s + 1) * _IN_SHARD)
            parts.append(pieces[s][:, lo - s * _IN_SHARD:hi - s * _IN_SHARD])
    parts.append(jnp.zeros((pieces[0].shape[0], PW - IN_WIDTH), pieces[0].dtype))
    return jnp.concatenate(parts, axis=1)


def _win_shards(wp):
    my_offs, pos = {}, 0
    for k in _MY_ORDER:
        my_offs[k] = pos
        pos += IN_SPLITS[k]
    shards = []
    for s in range(N_DEV):
        parts = []
        for k in range(len(IN_SPLITS)):
            lo, hi = max(_IN_OFFS[k], s * _IN_SHARD), min(_IN_OFFS[k + 1], (s + 1) * _IN_SHARD)
            if lo < hi:
                parts.append(wp[:, my_offs[k] + lo - _IN_OFFS[k]:my_offs[k] + hi - _IN_OFFS[k]])
        shards.append(jnp.concatenate(parts, axis=1))
    return jnp.stack(shards)


def _tile_friendly(shape, axis):
    width = shape[axis] // N_DEV if axis == len(shape) - 1 else 128
    return len(shape) >= 2 and width % 128 == 0


def _split_shards(full, axis):
    s = full.shape
    if _tile_friendly(s, axis):
        return jnp.moveaxis(full.reshape(s[:axis] + (N_DEV, s[axis] // N_DEV) + s[axis + 1:]), axis, 0)
    c = s[axis] // N_DEV
    return jnp.stack([lax.slice_in_dim(full, k * c, (k + 1) * c, axis=axis) for k in range(N_DEV)])


def _join_shards(g, axis):
    s = g.shape[1:]
    full_shape = s[:axis] + (N_DEV * s[axis],) + s[axis + 1:]
    if _tile_friendly(full_shape, axis):
        return jnp.moveaxis(g, 0, axis).reshape(full_shape)
    return jnp.concatenate([g[k] for k in range(N_DEV)], axis=axis)


def _pack_rows(pieces, row_mult):
    rows = jnp.concatenate([p.reshape(-1, 128) for p in pieces], axis=0)
    padn = (-rows.shape[0]) % row_mult
    if padn:
        rows = jnp.concatenate([rows, jnp.zeros((padn, 128), rows.dtype)], axis=0)
    return rows


def _unpack_rows(rows, shapes):
    out, pos = [], 0
    for s in shapes:
        size = 1
        for d in s:
            size *= d
        out.append(rows[pos:pos + size // 128].reshape(tuple(s)))
        pos += size // 128
    return out


def _heads_front(a, nh):
    return a.reshape(a.shape[0], nh, HEAD_DIM).transpose(1, 0, 2)


def _rope_tables(t, tc):
    tok = jnp.arange(t, dtype=jnp.int32)
    inv_freq = ROPE_THETA ** (-jnp.arange(ROPE_FREQS, dtype=F32) / ROPE_FREQS)
    ang_r = (tok // GRID_W).astype(F32)[:, None] * inv_freq
    ang_c = (tok % GRID_W).astype(F32)[:, None] * inv_freq
    cos64 = jnp.concatenate([jnp.cos(ang_r), jnp.cos(ang_r), jnp.cos(ang_c), jnp.cos(ang_c)], axis=1)
    sin64 = jnp.concatenate([-jnp.sin(ang_r), jnp.sin(ang_r), -jnp.sin(ang_c), jnp.sin(ang_c)], axis=1)
    cos64 = jnp.concatenate([jnp.ones((tc, HEAD_DIM), F32), cos64], axis=0)
    sin64 = jnp.concatenate([jnp.zeros((tc, HEAD_DIM), F32), sin64], axis=0)
    return jnp.tile(cos64, (1, N_Q_HEADS)), jnp.tile(sin64, (1, N_Q_HEADS))


def _head_mean_matrix(width):
    i = jnp.arange(width) // HEAD_DIM
    return (i[:, None] == i[None, :]).astype(F32) / HEAD_DIM


def _head_tile_matrix(width):
    return (jnp.arange(HEAD_DIM)[:, None] == (jnp.arange(width) % HEAD_DIM)[None, :]).astype(F32)


def _heads_t(a, nh):
    return a.T.reshape(nh, HEAD_DIM, a.shape[0])


def _heads_t_back(a):
    return a.reshape(a.shape[0] * HEAD_DIM, a.shape[2]).T


def _attention_fwd(tag, qr, kr, vv, tc, comm):
    qt8, k2, vt2 = _heads_t(qr, N_Q_HEADS), _heads_front(kr, N_KV_HEADS), _heads_t(vv, N_KV_HEADS)
    vt2 = jnp.concatenate([vt2, jnp.ones((N_KV_HEADS, 8, vt2.shape[2]), BF16)], axis=1)
    o_c, lse_c, _ = _attn_fwd(tag + "_attn_ctx", qt8[:, :, :tc], k2[:, :tc], vt2[:, :, :tc])
    o_x, lse_x, comm_out = _attn_fwd(tag + "_attn_lat", qt8[:, :, tc:], k2, vt2, comm)
    ot8 = jnp.concatenate([o_c, o_x], axis=2)
    lse = jnp.concatenate([lse_c, lse_x], axis=2)
    return _heads_t_back(ot8), (qr, kr, vv, ot8, lse), comm_out


def _attention_bwd(tag, saved, datt, tc, comm):
    qr, kr, vv, ot8, lse = saved
    datt = datt.astype(BF16)
    qt8, dot8 = _heads_t(qr, N_Q_HEADS), _heads_t(datt, N_Q_HEADS)
    k2, v2, kt2 = _heads_front(kr, N_KV_HEADS), _heads_front(vv, N_KV_HEADS), _heads_t(kr, N_KV_HEADS)
    zero = jnp.zeros((N_KV_HEADS, HEAD_DIM, tc), F32)
    dq_c, dk_c, dv_c, _ = _attn_bwd(tag + "_attn_b_ctx", qt8[:, :, :tc], k2[:, :tc], v2[:, :tc], kt2[:, :, :tc],
                                    ot8[:, :, :tc], dot8[:, :, :tc], lse[:, :, :tc], zero, zero)
    dq_x, dkt2, dvt2, comm_out = _attn_bwd(tag + "_attn_b_lat", qt8[:, :, tc:], k2, v2, kt2,
                                           ot8[:, :, tc:], dot8[:, :, tc:], lse[:, :, tc:], dk_c, dv_c, comm)
    dqt8 = jnp.concatenate([dq_c, dq_x], axis=2)
    return _heads_t_back(dqt8), _heads_t_back(dkt2), _heads_t_back(dvt2), comm_out


def _row_tiles(n, R):
    return _pick(n, (768, R)), _pick(n, (384, R))


def _layer_fwd(tag, x, w, modv, consts, R, nct, tc, comm, late_weights):
    sh1, sc1, g1, sh2, sc2, g2 = modv
    cosq, sinq, bdq, eq, bdk, ek = consts
    rw, rm = _row_tiles(x.shape[0], R)
    (h1,) = _rowwise(tag + "_ln1", _f_lnmod, rw, tc, [(x, D, 0, D)], [], [(w["norm1_g"], False)], [], [sh1, sc1], [(D, BF16)])
    p = _matmul(tag + "_in", h1, w["w_in"], "nn", F32)
    qk_rows = [(p, Q_W, C_Q // Q_W, Q_W), (p, KV_W, C_K // KV_W, KV_W)]
    qk_consts = [(cosq, Q_W, 0, Q_W), (sinq, Q_W, 0, Q_W), (cosq, KV_W, 0, KV_W), (sinq, KV_W, 0, KV_W)]
    qk_params = [(w["q_norm_g"], False), (w["k_norm_g"], False)]
    qk_cparams = [(bdq, False), (eq, False), (bdk, False), (ek, False)]
    qr, kr = _rowwise(tag + "_qk", _f_qknorm, rw, tc, qk_rows, qk_consts, qk_params, qk_cparams, [],
                      [(Q_W, BF16), (KV_W, BF16)], post=_qk_post)
    vv = p[:, C_VV:C_VV + KV_W].astype(BF16)
    att, att_saved, comm_out = _attention_fwd(tag, qr, kr, vv, tc, comm)
    w.update(late_weights(comm_out))

    o_f, o_b, s_f, s_b = _gla_fwd(tag + "_gla", p, (w["w2p_f"], w["w2p_b"]), (w["b2_f"], w["b2_b"]), R, nct)
    go_rows = [(o_f, GLA_V_W, 0, GLA_DV), (o_b, GLA_V_W, 0, GLA_DV), (p, GLA_V_W, C_R // GLA_V_W, GLA_DV)]
    (gla,) = _rowwise(tag + "_glaout", _f_glaout, rw, tc, go_rows, [], [(w["gla_norm_g"], True)], [], [], [(GLA_V_W, BF16)])

    rg = w["w_spatial"].shape[1]
    gm_rows = [(p, GMLP_W, C_U // GMLP_W, GMLP_W), (p, GMLP_W, C_V // GMLP_W, GMLP_W // GMLP_GROUPS)]
    gm_params = [(w["gmlp_norm_g"], True), (w["w_spatial"], True), (w["b_spatial_t"], True)]
    (gm,) = _rowwise(tag + "_gmlp", _f_gmlp, rg, tc, gm_rows, [], gm_params, [], [], [(GMLP_W, BF16)])

    ya = _matmul(tag + "_br_a", gm, w["w_br_a"], "nn", F32)
    yb = _matmul(tag + "_br_b", att, w["w_br_b"], "nn", F32)
    yc = _matmul(tag + "_br_c", gla, w["w_br_c"], "nn", F32)
    mg_rows = [(p, D, C_GA // D, D), (p, D, C_GB // D, D), (p, D, C_GC // D, D), (ya, D, 0, D), (yb, D, 0, D), (yc, D, 0, D)]
    (merged,) = _rowwise(tag + "_merge", _f_merge, rm, tc, mg_rows, [], [], [], [], [(D, BF16)])
    mix = _matmul(tag + "_out", merged, w["w_out"], "nn", F32)
    (x_mid,) = _rowwise(tag + "_res1", _f_resid, rw, tc, [(x, D, 0, D), (mix, D, 0, D)], [], [], [], [g1], [(D, F32)])

    (h2,) = _rowwise(tag + "_ln2", _f_lnmod, rw, tc, [(x_mid, D, 0, D)], [], [(w["norm2_g"], False)], [], [sh2, sc2], [(D, BF16)])
    a2 = _matmul(tag + "_up", h2, w["w_ffn_up"], "nn", F32, o_halves=True)
    gv = _conv_gate(tag + "_conv", a2, w["conv_w_h"], w["conv_b_h"], R, tc)
    ffn = _matmul(tag + "_down", gv, w["w_ffn_down"], "nn", F32)
    (x_next,) = _rowwise(tag + "_res2", _f_resid, rw, tc, [(x_mid, D, 0, D), (ffn, D, 0, D)], [], [], [], [g2], [(D, F32)])
    saved = dict(x=x, h1=h1, p=p, att_saved=att_saved, att=att, s_f=s_f, s_b=s_b, gla=gla, gm=gm,
                 ya=ya, yb=yb, yc=yc, merged=merged, mix=mix, x_mid=x_mid, h2=h2, a2=a2, gv=gv, ffn=ffn,
                 qk=(qk_rows, qk_consts, qk_params, qk_cparams), go_rows=go_rows, gm_info=(gm_rows, gm_params),
                 mg_rows=mg_rows)
    return x_next, saved, comm_out


def _layer_bwd(tag, dx_next, s, w, modv, R, nct, tc, make_comm, make_tail_comm):
    sh1, sc1, g1, sh2, sc2, g2 = modv
    rw, rm = _row_tiles(dx_next.shape[0], R)
    gw = {}
    (dffn,), _, (dg2,) = _rowwise_bwd(tag + "_res2_b", _f_resid, rw, tc, [(s["ffn"], D, 0, D), (s["ffn"], D, 0, D)], [], [], [], [g2],
                                      [(dx_next, D)], [None, BF16])
    dgv = _matmul(tag + "_down_da", dffn, w["w_ffn_down"], "nt", F32)
    gw["w_ffn_down"] = _matmul(tag + "_down_dw", s["gv"], dffn, "tn", F32)
    da2, dcw, dcb = _conv_gate_bwd(tag + "_conv_b", s["a2"], dgv, w["conv_w_h"], w["conv_b_h"], R, tc)
    gw["conv_w_h"], gw["conv_b_h"] = dcw, dcb
    dh2 = _matmul(tag + "_up_da", da2, w["w_ffn_up"], "nt", F32, a_halves=True)
    gw["w_ffn_up"] = _matmul(tag + "_up_dw", s["h2"], da2, "tn", F32, b_halves=True)
    (dx_mid,), (gw["norm2_g"],), (dsh2, dsc2) = _rowwise_bwd(
        tag + "_ln2_b", _f_lnmod, rw, tc, [(s["x_mid"], D, 0, D)], [], [(w["norm2_g"], False)], [], [sh2, sc2],
        [(dh2, D)], [F32], adds=[dx_next])
    (dmix,), _, (dg1,) = _rowwise_bwd(tag + "_res1_b", _f_resid, rw, tc, [(s["mix"], D, 0, D), (s["mix"], D, 0, D)], [], [], [], [g1],
                                      [(dx_mid, D)], [None, BF16])
    dmerged = _matmul(tag + "_out_da", dmix, w["w_out"], "nt", F32)
    gw["w_out"] = _matmul(tag + "_out_dw", s["merged"], dmix, "tn", F32)
    (dga, dgb, dgc, dya, dyb, dyc), _, _ = _rowwise_bwd(tag + "_merge_b", _f_merge, rm, tc, s["mg_rows"], [], [], [], [],
                                                        [(dmerged, D)], [BF16] * 6)
    dgm = _matmul(tag + "_br_a_da", dya, w["w_br_a"], "nt", F32)
    datt = _matmul(tag + "_br_b_da", dyb, w["w_br_b"], "nt", F32)
    dgla = _matmul(tag + "_br_c_da", dyc, w["w_br_c"], "nt", F32)
    gm_rows, gm_params = s["gm_info"]
    gw["w_br_a"] = _matmul(tag + "_br_a_dw", s["gm"], dya, "tn", F32)
    gw["w_br_b"] = _matmul(tag + "_br_b_dw", s["att"], dyb, "tn", F32)
    gw["w_br_c"] = _matmul(tag + "_br_c_dw", s["gla"], dyc, "tn", F32)
    rg = w["w_spatial"].shape[1]
    (du, dv_), (gw["gmlp_norm_g"], gw["w_spatial"], gw["b_spatial_t"]), _ = _rowwise_bwd(
        tag + "_gmlp_b", _f_gmlp, rg, tc, gm_rows, [], gm_params, [], [], [(dgm, GMLP_W)], [BF16, BF16])
    (do, dr), (gw["gla_norm_g"],), _ = _rowwise_bwd(tag + "_glaout_b", _f_glaout, rw, tc, s["go_rows"], [],
                                                    [(w["gla_norm_g"], True)], [], [], [(dgla, GLA_V_W)], [F32, None, BF16])
    p = s["p"]
    d_f, d_b, (gw["w2p_f"], gw["b2_f"], gw["w2p_b"], gw["b2_b"]) = _gla_bwd(
        tag + "_gla_b", p, (w["w2p_f"], w["w2p_b"]), (w["b2_f"], w["b2_b"]), (s["s_f"], s["s_b"]), do, R, nct)
    dgq, dgk, dgv_, daf = [(a + b).astype(BF16) for a, b in zip(d_f, d_b)]
    dqr, dkr, dvv, comm_out = _attention_bwd(tag, s["att_saved"], datt, tc, make_comm(gw))
    qk_rows, qk_consts, qk_params, qk_cparams = s["qk"]
    (dq, dk), (gw["q_norm_g"], gw["k_norm_g"]), _ = _rowwise_bwd(
        tag + "_qk_b", _f_qknorm, rw, tc, qk_rows, qk_consts, qk_params, qk_cparams, [],
        [(dqr, Q_W), (dkr, KV_W)], [BF16, BF16], pre=_qk_pre)
    dp = jnp.concatenate([dga, dgb, dgc, du, dv_, dq, dgv_, dr, dgq, dgk, dk, dvv.astype(BF16), daf,
                          jnp.zeros((p.shape[0], PW - C_AF - 128), BF16)], axis=1)
    gw["w_in"] = _matmul(tag + "_in_dw", s["h1"], dp, "tn", F32)
    tail = make_tail_comm(gw)
    if tail is None:
        dh1, tail_out = _matmul(tag + "_in_da", dp, w["w_in"], "nt", F32), []
    else:
        dh1, tail_out = _matmul(tag + "_in_da", dp, w["w_in"], "nt", F32, comm=tail)
    (dx,), (gw["norm1_g"],), (dsh1, dsc1) = _rowwise_bwd(
        tag + "_ln1_b", _f_lnmod, rw, tc, [(s["x"], D, 0, D)], [], [(w["norm1_g"], False)], [], [sh1, sc1],
        [(dh1, D)], [F32], adds=[dx_mid])
    return dx, gw, (dsh1, dsc1, dg1, dsh2, dsc2, dg2), comm_out, tail_out


_SHARDED = (("w_in", 1, True), ("w_br_a", 1, True), ("w_br_b", 1, True), ("w_br_c", 1, True), ("w_out", 0, True),
            ("w_ffn_up", 1, True), ("w_ffn_down", 0, True), ("conv_w", 1, False), ("w_alpha2", 2, False), ("b_alpha", 1, False))
_REPLICATED = ("c_ctx", "b_ada", "norm1_g", "norm2_g", "q_norm_g", "k_norm_g", "gmlp_norm_g", "w_spatial", "b_spatial",
               "gla_norm_g", "conv_b", "final_norm_g")
_WEIGHTS = ("c_ctx", "w_ada", "b_ada", "norm1_g", "norm2_g", "w_in", "q_norm_g", "k_norm_g", "gmlp_norm_g", "w_spatial",
            "b_spatial", "w_alpha2", "b_alpha", "gla_norm_g", "w_br_a", "w_br_b", "w_br_c", "w_out", "w_ffn_up", "conv_w",
            "conv_b", "w_ffn_down", "final_norm_g")


def _decay_weights(w_alpha2_l, b_alpha_l):
    out = []
    for d in range(2):
        w2p = jnp.zeros((128, GLA_QK_W), F32).at[GLA_RANK * d:GLA_RANK * (d + 1)].set(w_alpha2_l[d])
        out += [w2p, b_alpha_l[d][None, :]]
    return out


def _step(inp, wts, moms, vels):
    x, c, ctx, loss_target = inp
    t, tc = x.shape[1], ctx.shape[1]
    n = t + tc
    R = min(256, tc)
    nct = tc // R
    me = _my_index()
    depth = wts["w_in"].shape[0]
    gk = 3 if n % (3 * GMLP_CHUNK) == 0 else 1

    late = [(nm, ax) for nm, ax, half in _SHARDED if half and nm != "w_in"]
    small = [(nm, ax) for nm, ax, half in _SHARDED if not half]
    w_in_shard = lambda l: wts["w_in"][l].astype(BF16)
    c8 = jnp.concatenate([c, jnp.zeros((7, D), F32)], axis=0)
    first = _exchange("gather_first", [w_in_shard(0)] + [wts[nm] for nm, _ in small] + [c8], gather=True)
    c_all = first[-1][:, 0, :]
    small_all = dict(zip([nm for nm, _ in small], first[1:-1]))

    def early_weights(l, w_in_all):
        w = {"w_in": _win_my_cols([w_in_all[s] for s in range(N_DEV)])}
        conv_w, w_alpha2, b_alpha = [jnp.concatenate([small_all[nm][s, l] for s in range(N_DEV)], axis=ax) for nm, ax in small]
        w["conv_w_h"] = conv_w.reshape(3, 2, FFN_H).transpose(1, 0, 2)
        w["conv_b_h"] = wts["conv_b"][l].reshape(2, 1, FFN_H)
        w["w2p_f"], w["b2_f"], w["w2p_b"], w["b2_b"] = _decay_weights(w_alpha2, b_alpha)
        w["norm1_g"] = wts["norm1_g"][l][None, :]
        w["norm2_g"] = wts["norm2_g"][l][None, :]
        w["q_norm_g"] = wts["q_norm_g"][l][None, :]
        w["k_norm_g"] = wts["k_norm_g"][l][None, :]
        w["gmlp_norm_g"] = wts["gmlp_norm_g"][l].reshape(GMLP_GROUPS, 1, GMLP_W // GMLP_GROUPS)
        w["w_spatial"] = jnp.einsum("ab,gij->gaibj", jnp.eye(gk, dtype=F32), wts["w_spatial"][l]).reshape(
            GMLP_GROUPS, gk * GMLP_CHUNK, gk * GMLP_CHUNK)
        w["b_spatial_t"] = jnp.tile(wts["b_spatial"][l][:, :, None], (1, gk, 1))
        w["gla_norm_g"] = wts["gla_norm_g"][l].reshape(GLA_HEADS, 1, GLA_DV)
        return w

    craw16 =jnp.concatenate([c_all, wts["c_ctx"][None, :], jnp.zeros((7, D), F32)], axis=0)
    acols = wts["w_ada"].shape[2]
    b_cols = lax.dynamic_slice_in_dim(wts["b_ada"], me * acols, acols, axis=1)[:, None, :]
    mod_part = _adaln_fwd("adaln", craw16, wts["w_ada"], b_cols)
    send = jnp.stack([mod_part[:, 8, :][None].repeat(N_DEV, 0), mod_part[:, :8, :].transpose(1, 0, 2)], axis=2)
    send = jnp.concatenate([send.reshape(N_DEV, 2 * depth, acols), jnp.zeros((N_DEV, 8 - 2 * depth, acols), F32)], axis=1)
    (got,) = _exchange("scatter_mod", [send], gather=False)
    mod = got[:, :2 * depth, :].transpose(1, 0, 2).reshape(depth, 2, N_MOD, 1, D)
    modv = [[mod[l, :, k] for k in range(N_MOD)] for l in range(depth)]

    cosq, sinq = _rope_tables(t, tc)
    consts = (cosq, sinq, _head_mean_matrix(Q_W), _head_tile_matrix(Q_W), _head_mean_matrix(KV_W), _head_tile_matrix(KV_W))
    xs = jnp.concatenate([ctx[0], x[0]], axis=0)
    saved, layers = [], []
    w_in_all = first[0]

    def late_weights(got):
        return {nm: _join_shards(g, ax) for (nm, ax), g in zip(late, got)}

    for l in range(depth):
        layers.append(early_weights(l, w_in_all))
        sending = [wts[nm][l].astype(BF16) for nm, _ in late] + ([w_in_shard(l + 1)] if l + 1 < depth else [])
        xs, sv, got = _layer_fwd("l%d" % l, xs, layers[l], modv[l], consts, R, nct, tc, (sending, True), late_weights)
        if l + 1 < depth:
            w_in_all = got[len(late)]
        saved.append(sv)
    loss_blk, dxs, dgf = _final_loss("final", xs, loss_target[0], wts["final_norm_g"][None, :], R, nct)
    loss = lax.psum(loss_blk[0, 0], ("x", "y", "c"))

    grads = [None] * depth
    dmods = [None] * depth
    late_parts = [None] * depth
    w_in_parts = [None] * depth
    w_in_grad_shards = lambda g: _win_shards(g["w_in"]).astype(BF16)

    def small_grad_shards(g):
        full = dict(conv_w=g["conv_w_h"].transpose(1, 0, 2).reshape(3, F2),
                    w_alpha2=jnp.stack([g["w2p_f"][:GLA_RANK], g["w2p_b"][GLA_RANK:2 * GLA_RANK]]),
                    b_alpha=jnp.stack([g["b2_f"][0], g["b2_b"][0]]))
        return [_split_shards(full[nm], ax) for nm, ax in small]

    for l in range(depth - 1, -1, -1):
        def make_comm(gw, l=l):
            sending = [_split_shards(gw[nm], ax).astype(BF16) for nm, ax in late]
            return sending + ([w_in_grad_shards(grads[l + 1])] if l + 1 < depth else []), False

        def make_tail_comm(gw, l=l):
            if l > 0:
                return None
            per_layer = [small_grad_shards(gw if k == 0 else grads[k]) for k in range(depth)]
            return [w_in_grad_shards(gw)] + [jnp.stack([per_layer[k][i] for k in range(depth)], axis=1) for i in range(len(small))], False

        dxs, grads[l], dmods[l], got, tail = _layer_bwd("l%d" % l, dxs, saved[l], layers[l], modv[l], R, nct, tc,
                                                        make_comm, make_tail_comm)
        late_parts[l] = got[:len(late)]
        if l + 1 < depth:
            w_in_parts[l + 1] = got[len(late)]
    w_in_parts[0], small_parts = tail[0], tail[1:]
    grad_x = dxs[tc:][None]

    dmod = jnp.stack([jnp.stack(dmods[l], axis=1) for l in range(depth)])
    dmod = dmod.reshape(depth, 2, N_DEV, acols).transpose(2, 0, 1, 3).reshape(N_DEV, 2 * depth, acols)
    dmod_send = jnp.concatenate([dmod, jnp.zeros((N_DEV, 8 - 2 * depth, acols), F32)], axis=1)
    (dm_got,) = _exchange("scatter_dmod", [dmod_send], gather=False)
    g_w_ada, dc16 = _adaln_bwd("adaln_b", craw16, wts["w_ada"], dm_got[:, :2 * depth].transpose(1, 0, 2))
    db_ada_part = jnp.stack([jnp.stack(dmods[l], axis=1) for l in range(depth)]).reshape(depth, 2, N_MOD * D).sum(axis=1)

    out = {}
    kinds = ("grad", "delta", "new_m", "new_v")
    view2 = lambda a: a.reshape(-1, a.shape[-1])
    sharded_parts = ([jnp.stack(w_in_parts, axis=1)]
                     + [jnp.stack([late_parts[l][k] for l in range(depth)], axis=1) for k in range(len(late))] + small_parts)
    for (nm, _), parts in zip([("w_in", 1)] + late + small, sharded_parts):
        res = _adamw("adamw_" + nm, view2(wts[nm]), view2(moms[nm]), view2(vels[nm]), parts.reshape(N_DEV, -1, parts.shape[-1]))
        for kind, flat in zip(kinds, res):
            out[kind, nm] = flat.reshape(wts[nm].shape)

    rep_g = dict(
        c_ctx=dc16[8], b_ada=db_ada_part, final_norm_g=dgf[0],
        norm1_g=jnp.stack([grads[l]["norm1_g"][0] for l in range(depth)]),
        norm2_g=jnp.stack([grads[l]["norm2_g"][0] for l in range(depth)]),
        q_norm_g=jnp.stack([grads[l]["q_norm_g"][0] for l in range(depth)]),
        k_norm_g=jnp.stack([grads[l]["k_norm_g"][0] for l in range(depth)]),
        gmlp_norm_g=jnp.stack([grads[l]["gmlp_norm_g"].reshape(GMLP_W) for l in range(depth)]),
        w_spatial=jnp.stack([sum(grads[l]["w_spatial"][:, a * GMLP_CHUNK:(a + 1) * GMLP_CHUNK, a * GMLP_CHUNK:(a + 1) * GMLP_CHUNK]
                                 for a in range(gk)) for l in range(depth)]),
        b_spatial=jnp.stack([grads[l]["b_spatial_t"][:, :, 0].reshape(GMLP_GROUPS, gk, GMLP_CHUNK).sum(axis=1)
                             for l in range(depth)]),
        gla_norm_g=jnp.stack([grads[l]["gla_norm_g"].reshape(GLA_V_W) for l in range(depth)]),
        conv_b=jnp.stack([grads[l]["conv_b_h"].reshape(F2) for l in range(depth)]),
    )
    rep_shapes = [wts[nm].shape for nm in _REPLICATED]
    (rg_parts,) = _exchange("gather_rep_grads", [_pack_rows([rep_g[nm] for nm in _REPLICATED], 16)], gather=True)
    rpk = lambda src: _pack_rows([src[nm] for nm in _REPLICATED], 16)
    res = _adamw("adamw_rep", rpk(wts), rpk(moms), rpk(vels), rg_parts)
    for kind, rows in zip(kinds, res):
        for nm, piece in zip(_REPLICATED, _unpack_rows(rows, rep_shapes)):
            out[kind, nm] = piece

    res = _adamw("adamw_ada", view2(wts["w_ada"]), view2(moms["w_ada"]), view2(vels["w_ada"]), view2(g_w_ada)[None])
    for kind, flat in zip(kinds, res):
        out[kind, "w_ada"] = flat.reshape(wts["w_ada"].shape)

    return (loss, grad_x, *[out[kind, nm] for kind in kinds for nm in _WEIGHTS])


def kernel(x, c, ctx, c_ctx, w_ada, b_ada, norm1_g, norm2_g, w_in, q_norm_g, k_norm_g, gmlp_norm_g, w_spatial, b_spatial, w_alpha2, b_alpha, gla_norm_g, w_br_a, w_br_b, w_br_c, w_out, w_ffn_up, conv_w, conv_b, w_ffn_down, final_norm_g, loss_target, m_c_ctx, m_w_ada, m_b_ada, m_norm1_g, m_norm2_g, m_w_in, m_q_norm_g, m_k_norm_g, m_gmlp_norm_g, m_w_spatial, m_b_spatial, m_w_alpha2, m_b_alpha, m_gla_norm_g, m_w_br_a, m_w_br_b, m_w_br_c, m_w_out, m_w_ffn_up, m_conv_w, m_conv_b, m_w_ffn_down, m_final_norm_g, v_c_ctx, v_w_ada, v_b_ada, v_norm1_g, v_norm2_g, v_w_in, v_q_norm_g, v_k_norm_g, v_gmlp_norm_g, v_w_spatial, v_b_spatial, v_w_alpha2, v_b_alpha, v_gla_norm_g, v_w_br_a, v_w_br_b, v_w_br_c, v_w_out, v_w_ffn_up, v_conv_w, v_conv_b, v_w_ffn_down, v_final_norm_g):
    wts = dict(zip(_WEIGHTS, (c_ctx, w_ada, b_ada, norm1_g, norm2_g, w_in, q_norm_g, k_norm_g, gmlp_norm_g, w_spatial, b_spatial,
                              w_alpha2, b_alpha, gla_norm_g, w_br_a, w_br_b, w_br_c, w_out, w_ffn_up, conv_w, conv_b, w_ffn_down,
                              final_norm_g)))
    moms = dict(zip(_WEIGHTS, (m_c_ctx, m_w_ada, m_b_ada, m_norm1_g, m_norm2_g, m_w_in, m_q_norm_g, m_k_norm_g, m_gmlp_norm_g,
                               m_w_spatial, m_b_spatial, m_w_alpha2, m_b_alpha, m_gla_norm_g, m_w_br_a, m_w_br_b, m_w_br_c, m_w_out,
                               m_w_ffn_up, m_conv_w, m_conv_b, m_w_ffn_down, m_final_norm_g)))
    vels = dict(zip(_WEIGHTS, (v_c_ctx, v_w_ada, v_b_ada, v_norm1_g, v_norm2_g, v_w_in, v_q_norm_g, v_k_norm_g, v_gmlp_norm_g,
                               v_w_spatial, v_b_spatial, v_w_alpha2, v_b_alpha, v_gla_norm_g, v_w_br_a, v_w_br_b, v_w_br_c, v_w_out,
                               v_w_ffn_up, v_conv_w, v_conv_b, v_w_ffn_down, v_final_norm_g)))
    return _step((x, c, ctx, loss_target), wts, moms, vels)
```
